```python
import jax, jax.numpy as jnp
from jax import lax
import numpy as np

D_MODEL = 1024
BATCH = 8
SEQ = 8192
DEPTH = 1

D_MIX = D_MODEL
D_POOL = D_MIX // 2
D_LRU = D_MIX - D_POOL
POOL_WINDOWS = (2, 4, 8, 16)
N_POOL_GROUPS = len(POOL_WINDOWS)
POOL_GROUP_WIDTH = D_POOL // N_POOL_GROUPS
LRU_HEADS = 8
LRU_BLOCK = D_LRU // LRU_HEADS
CONV_WIDTH = 4
LRU_C = 8.0
D_FF = 4 * D_MODEL
PLE_DIM = 256
RMS_EPS = 1e-6
D_IN_PROJ = D_POOL + D_LRU + D_LRU

kernel_name = "hybrid_pool_rglru_block"


def rms_norm(x, g):
    xf = x.astype(jnp.float32)
    y = xf * lax.rsqrt(jnp.mean(xf * xf, axis=-1, keepdims=True) + RMS_EPS)
    return (y * g.astype(jnp.float32)).astype(x.dtype)


def multiscale_pool_mixer(u, pool_w, pool_b, pool_scale):
    B, S, _ = u.shape
    ug = u.reshape(B, S, N_POOL_GROUPS, POOL_GROUP_WIDTH).astype(jnp.float32)
    csum = jnp.cumsum(ug, axis=1)
    t = jnp.arange(S)
    outs = []
    for g, w in enumerate(POOL_WINDOWS):
        c = csum[:, :, g]
        c_lag = jnp.pad(c, ((0, 0), (w, 0), (0, 0)))[:, :S]
        count = jnp.minimum(t + 1, w).astype(jnp.float32)
        outs.append((c - c_lag) / count[None, :, None] - ug[:, :, g])
    d = jnp.stack(outs, axis=2).astype(u.dtype)
    y = jnp.einsum('bsgc,gcd->bsgd', d, pool_w) + pool_b
    return y.reshape(B, S, D_POOL) * pool_scale


def causal_depthwise_conv(u, w, b):
    S = u.shape[1]
    upad = jnp.pad(u, ((0, 0), (CONV_WIDTH - 1, 0), (0, 0)))
    y = b
    for k in range(CONV_WIDTH):
        y = y + upad[:, k:k + S] * w[k]
    return y


def _linear_recurrence_combine(c1, c2):
    a1, b1 = c1
    a2, b2 = c2
    return a1 * a2, a2 * b1 + b2


def rg_lru(u, gate_a_w, gate_a_b, gate_x_w, gate_x_b, lru_L):
    B, S, W = u.shape
    uh = u.reshape(B, S, LRU_HEADS, LRU_BLOCK)
    r = jax.nn.sigmoid(jnp.einsum('bshi,hij->bshj', uh, gate_a_w) + gate_a_b).reshape(B, S, W)
    i = jax.nn.sigmoid(jnp.einsum('bshi,hij->bshj', uh, gate_x_w) + gate_x_b).reshape(B, S, W)
    log_a = LRU_C * r.astype(jnp.float32) * jax.nn.log_sigmoid(lru_L.astype(jnp.float32))
    a = jnp.exp(log_a)
    mult = jnp.sqrt(-jnp.expm1(2.0 * log_a))
    is_first = (jnp.arange(S) == 0)[None, :, None]
    mult = jnp.where(is_first, 1.0, mult)
    b = mult * (i * u).astype(jnp.float32)
    _, h = lax.associative_scan(_linear_recurrence_combine, (a, b), axis=1)
    return h.astype(u.dtype)


def _fwd_setup_inputs(seed: int = 0) -> dict:
    key = jax.random.key(seed)
    ks = jax.random.split(key, 24)
    f32 = jnp.float32
    nrm = lambda k, shape, scale: jax.random.normal(k, shape, f32) * scale
    gain = lambda k, shape: 1.0 + 0.05 * jax.random.normal(k, shape, f32)
    L = DEPTH
    rad = jnp.sqrt(jax.random.uniform(ks[12], (L, D_LRU), f32, 0.9 ** 2, 0.999 ** 2))
    lru_L = jnp.log(rad) - jnp.log1p(-rad)
    return {
        "x": jax.random.normal(ks[0], (BATCH, SEQ, D_MODEL), f32),
        "p": jax.random.normal(ks[1], (DEPTH, BATCH, SEQ, PLE_DIM), f32),
        "norm_mix_g": gain(ks[2], (L, D_MODEL)),
        "w_in": nrm(ks[3], (L, D_MODEL, D_IN_PROJ), D_MODEL ** -0.5),
        "pool_w": nrm(ks[4], (L, N_POOL_GROUPS, POOL_GROUP_WIDTH, POOL_GROUP_WIDTH), POOL_GROUP_WIDTH ** -0.5),
        "pool_b": nrm(ks[5], (L, N_POOL_GROUPS, POOL_GROUP_WIDTH), 0.01),
        "pool_scale": gain(ks[6], (L, D_POOL)),
        "conv_w": nrm(ks[7], (L, CONV_WIDTH, D_LRU), CONV_WIDTH ** -0.5),
        "conv_b": nrm(ks[8], (L, D_LRU), 0.01),
        "gate_a_w": nrm(ks[9], (L, LRU_HEADS, LRU_BLOCK, LRU_BLOCK), LRU_BLOCK ** -0.5),
        "gate_a_b": nrm(ks[10], (L, LRU_HEADS, LRU_BLOCK), 0.01),
        "gate_x_w": nrm(ks[11], (L, LRU_HEADS, LRU_BLOCK, LRU_BLOCK), LRU_BLOCK ** -0.5),
        "gate_x_b": nrm(ks[13], (L, LRU_HEADS, LRU_BLOCK), 0.01),
        "lru_L": lru_L,
        "w_out": nrm(ks[14], (L, D_MIX, D_MODEL), D_MIX ** -0.5),
        "norm_mlp_g": gain(ks[15], (L, D_MODEL)),
        "w_up": nrm(ks[16], (L, D_MODEL, D_FF), D_MODEL ** -0.5),
        "w_down": nrm(ks[17], (L, D_FF, D_MODEL), D_FF ** -0.5),
        "norm_ple_g": gain(ks[18], (L, D_MODEL)),
        "w_ple_gate": nrm(ks[19], (L, D_MODEL, D_MODEL), D_MODEL ** -0.5),
        "b_ple_gate": nrm(ks[20], (L, D_MODEL), 0.01),
        "w_ple_proj": nrm(ks[21], (L, PLE_DIM, D_MODEL), PLE_DIM ** -0.5),
        "norm_final_g": gain(ks[22], (D_MODEL,)),
    }


def _fwd_reference(x, p, norm_mix_g, w_in, pool_w, pool_b, pool_scale, conv_w, conv_b,
              gate_a_w, gate_a_b, gate_x_w, gate_x_b, lru_L, w_out, norm_mlp_g,
              w_up, w_down, norm_ple_g, w_ple_gate, b_ple_gate, w_ple_proj, norm_final_g):
    h = x
    for l in range(DEPTH):
        z = rms_norm(h, norm_mix_g[l])
        proj = z @ w_in[l]
        u_pool = proj[..., :D_POOL]
        u_lru = proj[..., D_POOL:D_POOL + D_LRU]
        u_gate = proj[..., D_POOL + D_LRU:]
        y_pool = multiscale_pool_mixer(u_pool, pool_w[l], pool_b[l], pool_scale[l])
        xb = causal_depthwise_conv(u_lru, conv_w[l], conv_b[l])
        y_lru = rg_lru(xb, gate_a_w[l], gate_a_b[l], gate_x_w[l], gate_x_b[l], lru_L[l])
        y_lru = y_lru * jax.nn.gelu(u_gate)
        h = h + jnp.concatenate([y_pool, y_lru], axis=-1) @ w_out[l]
        z = rms_norm(h, norm_mlp_g[l])
        h = h + jnp.square(jax.nn.relu(z @ w_up[l])) @ w_down[l]
        z = rms_norm(h, norm_ple_g[l])
        gate = jax.nn.sigmoid(z @ w_ple_gate[l] + b_ple_gate[l])
        h = h + gate * (p[l] @ w_ple_proj[l])
    return rms_norm(h, norm_final_g)


import jax as _jax
import jax.numpy as _jnp

TWIN_FORMAT = 'train_step'
FWD_PARAMS = ['x', 'p', 'norm_mix_g', 'w_in', 'pool_w', 'pool_b', 'pool_scale', 'conv_w', 'conv_b', 'gate_a_w', 'gate_a_b', 'gate_x_w', 'gate_x_b', 'lru_L', 'w_out', 'norm_mlp_g', 'w_up', 'w_down', 'norm_ple_g', 'w_ple_gate', 'b_ple_gate', 'w_ple_proj', 'norm_final_g']
TWIN_WEIGHTS = ['norm_mix_g', 'w_in', 'pool_w', 'pool_b', 'pool_scale', 'conv_w', 'conv_b', 'gate_a_w', 'gate_a_b', 'gate_x_w', 'gate_x_b', 'lru_L', 'w_out', 'norm_mlp_g', 'w_up', 'w_down', 'norm_ple_g', 'w_ple_gate', 'b_ple_gate', 'w_ple_proj', 'norm_final_g']
TWIN_DIFF_INPUT = 'x'
TWIN_INPUTS = ['x', 'p', 'norm_mix_g', 'w_in', 'pool_w', 'pool_b', 'pool_scale', 'conv_w', 'conv_b', 'gate_a_w', 'gate_a_b', 'gate_x_w', 'gate_x_b', 'lru_L', 'w_out', 'norm_mlp_g', 'w_up', 'w_down', 'norm_ple_g', 'w_ple_gate', 'b_ple_gate', 'w_ple_proj', 'norm_final_g', 'loss_target', 'm_norm_mix_g', 'm_w_in', 'm_pool_w', 'm_pool_b', 'm_pool_scale', 'm_conv_w', 'm_conv_b', 'm_gate_a_w', 'm_gate_a_b', 'm_gate_x_w', 'm_gate_x_b', 'm_lru_L', 'm_w_out', 'm_norm_mlp_g', 'm_w_up', 'm_w_down', 'm_norm_ple_g', 'm_w_ple_gate', 'm_b_ple_gate', 'm_w_ple_proj', 'm_norm_final_g', 'v_norm_mix_g', 'v_w_in', 'v_pool_w', 'v_pool_b', 'v_pool_scale', 'v_conv_w', 'v_conv_b', 'v_gate_a_w', 'v_gate_a_b', 'v_gate_x_w', 'v_gate_x_b', 'v_lru_L', 'v_w_out', 'v_norm_mlp_g', 'v_w_up', 'v_w_down', 'v_norm_ple_g', 'v_w_ple_gate', 'v_b_ple_gate', 'v_w_ple_proj', 'v_norm_final_g']
TWIN_OUTPUTS = ['loss', 'grad_x', 'grad_norm_mix_g', 'grad_w_in', 'grad_pool_w', 'grad_pool_b', 'grad_pool_scale', 'grad_conv_w', 'grad_conv_b', 'grad_gate_a_w', 'grad_gate_a_b', 'grad_gate_x_w', 'grad_gate_x_b', 'grad_lru_L', 'grad_w_out', 'grad_norm_mlp_g', 'grad_w_up', 'grad_w_down', 'grad_norm_ple_g', 'grad_w_ple_gate', 'grad_b_ple_gate', 'grad_w_ple_proj', 'grad_norm_final_g', 'delta_norm_mix_g', 'delta_w_in', 'delta_pool_w', 'delta_pool_b', 'delta_pool_scale', 'delta_conv_w', 'delta_conv_b', 'delta_gate_a_w', 'delta_gate_a_b', 'delta_gate_x_w', 'delta_gate_x_b', 'delta_lru_L', 'delta_w_out', 'delta_norm_mlp_g', 'delta_w_up', 'delta_w_down', 'delta_norm_ple_g', 'delta_w_ple_gate', 'delta_b_ple_gate', 'delta_w_ple_proj', 'delta_norm_final_g', 'new_m_norm_mix_g', 'new_m_w_in', 'new_m_pool_w', 'new_m_pool_b', 'new_m_pool_scale', 'new_m_conv_w', 'new_m_conv_b', 'new_m_gate_a_w', 'new_m_gate_a_b', 'new_m_gate_x_w', 'new_m_gate_x_b', 'new_m_lru_L', 'new_m_w_out', 'new_m_norm_mlp_g', 'new_m_w_up', 'new_m_w_down', 'new_m_norm_ple_g', 'new_m_w_ple_gate', 'new_m_b_ple_gate', 'new_m_w_ple_proj', 'new_m_norm_final_g', 'new_v_norm_mix_g', 'new_v_w_in', 'new_v_pool_w', 'new_v_pool_b', 'new_v_pool_scale', 'new_v_conv_w', 'new_v_conv_b', 'new_v_gate_a_w', 'new_v_gate_a_b', 'new_v_gate_x_w', 'new_v_gate_x_b', 'new_v_lru_L', 'new_v_w_out', 'new_v_norm_mlp_g', 'new_v_w_up', 'new_v_w_down', 'new_v_norm_ple_g', 'new_v_w_ple_gate', 'new_v_b_ple_gate', 'new_v_w_ple_proj', 'new_v_norm_final_g']
TWIN_LEAF_KINDS = {'loss': 'loss', 'grad_x': 'grad_x', 'grad_norm_mix_g': 'grad_w', 'grad_w_in': 'grad_w', 'grad_pool_w': 'grad_w', 'grad_pool_b': 'grad_w', 'grad_pool_scale': 'grad_w', 'grad_conv_w': 'grad_w', 'grad_conv_b': 'grad_w', 'grad_gate_a_w': 'grad_w', 'grad_gate_a_b': 'grad_w', 'grad_gate_x_w': 'grad_w', 'grad_gate_x_b': 'grad_w', 'grad_lru_L': 'grad_w', 'grad_w_out': 'grad_w', 'grad_norm_mlp_g': 'grad_w', 'grad_w_up': 'grad_w', 'grad_w_down': 'grad_w', 'grad_norm_ple_g': 'grad_w', 'grad_w_ple_gate': 'grad_w', 'grad_b_ple_gate': 'grad_w', 'grad_w_ple_proj': 'grad_w', 'grad_norm_final_g': 'grad_w', 'delta_norm_mix_g': 'delta_w', 'delta_w_in': 'delta_w', 'delta_pool_w': 'delta_w', 'delta_pool_b': 'delta_w', 'delta_pool_scale': 'delta_w', 'delta_conv_w': 'delta_w', 'delta_conv_b': 'delta_w', 'delta_gate_a_w': 'delta_w', 'delta_gate_a_b': 'delta_w', 'delta_gate_x_w': 'delta_w', 'delta_gate_x_b': 'delta_w', 'delta_lru_L': 'delta_w', 'delta_w_out': 'delta_w', 'delta_norm_mlp_g': 'delta_w', 'delta_w_up': 'delta_w', 'delta_w_down': 'delta_w', 'delta_norm_ple_g': 'delta_w', 'delta_w_ple_gate': 'delta_w', 'delta_b_ple_gate': 'delta_w', 'delta_w_ple_proj': 'delta_w', 'delta_norm_final_g': 'delta_w', 'new_m_norm_mix_g': 'new_m', 'new_m_w_in': 'new_m', 'new_m_pool_w': 'new_m', 'new_m_pool_b': 'new_m', 'new_m_pool_scale': 'new_m', 'new_m_conv_w': 'new_m', 'new_m_conv_b': 'new_m', 'new_m_gate_a_w': 'new_m', 'new_m_gate_a_b': 'new_m', 'new_m_gate_x_w': 'new_m', 'new_m_gate_x_b': 'new_m', 'new_m_lru_L': 'new_m', 'new_m_w_out': 'new_m', 'new_m_norm_mlp_g': 'new_m', 'new_m_w_up': 'new_m', 'new_m_w_down': 'new_m', 'new_m_norm_ple_g': 'new_m', 'new_m_w_ple_gate': 'new_m', 'new_m_b_ple_gate': 'new_m', 'new_m_w_ple_proj': 'new_m', 'new_m_norm_final_g': 'new_m', 'new_v_norm_mix_g': 'new_v', 'new_v_w_in': 'new_v', 'new_v_pool_w': 'new_v', 'new_v_pool_b': 'new_v', 'new_v_pool_scale': 'new_v', 'new_v_conv_w': 'new_v', 'new_v_conv_b': 'new_v', 'new_v_gate_a_w': 'new_v', 'new_v_gate_a_b': 'new_v', 'new_v_gate_x_w': 'new_v', 'new_v_gate_x_b': 'new_v', 'new_v_lru_L': 'new_v', 'new_v_w_out': 'new_v', 'new_v_norm_mlp_g': 'new_v', 'new_v_w_up': 'new_v', 'new_v_w_down': 'new_v', 'new_v_norm_ple_g': 'new_v', 'new_v_w_ple_gate': 'new_v', 'new_v_b_ple_gate': 'new_v', 'new_v_w_ple_proj': 'new_v', 'new_v_norm_final_g': 'new_v'}


def _forward(args):
    return _fwd_reference(*[args[k] for k in FWD_PARAMS])


def _output_shape():
    def fwd():
        inp = _fwd_setup_inputs(0)
        return _fwd_reference(*[inp[k] for k in FWD_PARAMS])
    out = _jax.eval_shape(fwd)
    return out.shape, out.dtype

N_MICROBATCH = 1
ADAM_LR = 0.001
ADAM_B1 = 0.9
ADAM_B2 = 0.999
ADAM_EPS = 1e-08
ADAM_WD = 0.01
ADAM_STEP = 10
PER_EXAMPLE_BATCH_AXIS = {'x': 0, 'p': 1, 'loss_target': 0}
SHARED_INPUTS = []
_WEIGHT_DTYPES = {'norm_mix_g': _jnp.float32, 'w_in': _jnp.float32, 'pool_w': _jnp.float32, 'pool_b': _jnp.float32, 'pool_scale': _jnp.float32, 'conv_w': _jnp.float32, 'conv_b': _jnp.float32, 'gate_a_w': _jnp.float32, 'gate_a_b': _jnp.float32, 'gate_x_w': _jnp.float32, 'gate_x_b': _jnp.float32, 'lru_L': _jnp.float32, 'w_out': _jnp.float32, 'norm_mlp_g': _jnp.float32, 'w_up': _jnp.float32, 'w_down': _jnp.float32, 'norm_ple_g': _jnp.float32, 'w_ple_gate': _jnp.float32, 'b_ple_gate': _jnp.float32, 'w_ple_proj': _jnp.float32, 'norm_final_g': _jnp.float32}
MOMENT_SCALE = {'norm_mix_g': 1.623247e-01, 'w_in': 1.332354e-01, 'pool_w': 2.011498e-01, 'pool_b': 1.000216e+00, 'pool_scale': 2.152588e-01, 'conv_w': 8.227483e-02, 'conv_b': 6.363584e-01, 'gate_a_w': 2.725123e-02, 'gate_a_b': 2.058944e-02, 'gate_x_w': 5.052215e-02, 'gate_x_b': 2.959019e-02, 'lru_L': 3.765781e-02, 'w_out': 1.566775e-01, 'norm_mlp_g': 2.022284e-01, 'w_up': 1.030670e-01, 'w_down': 3.587338e-01, 'norm_ple_g': 4.554714e-02, 'w_ple_gate': 4.447232e-02, 'b_ple_gate': 1.636961e-01, 'w_ple_proj': 7.801231e-02, 'norm_final_g': 6.449946e+01}


def _to_microbatches(a, axis):
    t = _jnp.moveaxis(a, axis, 0)
    t = t.reshape((N_MICROBATCH, t.shape[0] // N_MICROBATCH) + t.shape[1:])
    return _jnp.moveaxis(t, 1, axis + 1)


def setup_inputs(seed: int = 0) -> dict:
    inp = _fwd_setup_inputs(seed)
    key = _jax.random.fold_in(_jax.random.key(seed), 7919)
    shape, _ = _output_shape()
    out = dict(inp)
    out["loss_target"] = _jax.random.normal(_jax.random.fold_in(key, 0), shape, _jnp.float32)
    for i, name in enumerate(TWIN_WEIGHTS):
        w = inp[name].astype(_jnp.float32)
        if MOMENT_SCALE is None:
            s = _jnp.sqrt(_jnp.mean(_jnp.square(w)) + 1e-30)
        else:
            s = MOMENT_SCALE[name]
        km, kv = _jax.random.split(_jax.random.fold_in(key, i + 1))
        out[name] = w
        out["m_" + name] = s * _jax.random.normal(km, w.shape, _jnp.float32)
        out["v_" + name] = (s * s) * _jax.random.uniform(kv, w.shape, _jnp.float32, 0.5, 1.5)
    if N_MICROBATCH > 1:
        for name, axis in PER_EXAMPLE_BATCH_AXIS.items():
            out[name] = _to_microbatches(out[name], axis)
    return {'x': out['x'], 'p': out['p'], 'norm_mix_g': out['norm_mix_g'], 'w_in': out['w_in'], 'pool_w': out['pool_w'], 'pool_b': out['pool_b'], 'pool_scale': out['pool_scale'], 'conv_w': out['conv_w'], 'conv_b': out['conv_b'], 'gate_a_w': out['gate_a_w'], 'gate_a_b': out['gate_a_b'], 'gate_x_w': out['gate_x_w'], 'gate_x_b': out['gate_x_b'], 'lru_L': out['lru_L'], 'w_out': out['w_out'], 'norm_mlp_g': out['norm_mlp_g'], 'w_up': out['w_up'], 'w_down': out['w_down'], 'norm_ple_g': out['norm_ple_g'], 'w_ple_gate': out['w_ple_gate'], 'b_ple_gate': out['b_ple_gate'], 'w_ple_proj': out['w_ple_proj'], 'norm_final_g': out['norm_final_g'], 'loss_target': out['loss_target'], 'm_norm_mix_g': out['m_norm_mix_g'], 'm_w_in': out['m_w_in'], 'm_pool_w': out['m_pool_w'], 'm_pool_b': out['m_pool_b'], 'm_pool_scale': out['m_pool_scale'], 'm_conv_w': out['m_conv_w'], 'm_conv_b': out['m_conv_b'], 'm_gate_a_w': out['m_gate_a_w'], 'm_gate_a_b': out['m_gate_a_b'], 'm_gate_x_w': out['m_gate_x_w'], 'm_gate_x_b': out['m_gate_x_b'], 'm_lru_L': out['m_lru_L'], 'm_w_out': out['m_w_out'], 'm_norm_mlp_g': out['m_norm_mlp_g'], 'm_w_up': out['m_w_up'], 'm_w_down': out['m_w_down'], 'm_norm_ple_g': out['m_norm_ple_g'], 'm_w_ple_gate': out['m_w_ple_gate'], 'm_b_ple_gate': out['m_b_ple_gate'], 'm_w_ple_proj': out['m_w_ple_proj'], 'm_norm_final_g': out['m_norm_final_g'], 'v_norm_mix_g': out['v_norm_mix_g'], 'v_w_in': out['v_w_in'], 'v_pool_w': out['v_pool_w'], 'v_pool_b': out['v_pool_b'], 'v_pool_scale': out['v_pool_scale'], 'v_conv_w': out['v_conv_w'], 'v_conv_b': out['v_conv_b'], 'v_gate_a_w': out['v_gate_a_w'], 'v_gate_a_b': out['v_gate_a_b'], 'v_gate_x_w': out['v_gate_x_w'], 'v_gate_x_b': out['v_gate_x_b'], 'v_lru_L': out['v_lru_L'], 'v_w_out': out['v_w_out'], 'v_norm_mlp_g': out['v_norm_mlp_g'], 'v_w_up': out['v_w_up'], 'v_w_down': out['v_w_down'], 'v_norm_ple_g': out['v_norm_ple_g'], 'v_w_ple_gate': out['v_w_ple_gate'], 'v_b_ple_gate': out['v_b_ple_gate'], 'v_w_ple_proj': out['v_w_ple_proj'], 'v_norm_final_g': out['v_norm_final_g']}


def _loss(weights, diff, rest, loss_target):
    with _jax.named_scope("forward"):
        args = {**rest, TWIN_DIFF_INPUT: diff, **{k: w.astype(_WEIGHT_DTYPES[k]) for k, w in weights.items()}}
        y = _forward(args)
    with _jax.named_scope("loss_head"):
        err = _jnp.square(y.astype(_jnp.float32) - loss_target)
        return 0.5 * _jnp.sum(_jnp.mean(err, axis=-1)) if err.ndim else 0.5 * err


def _adamw(w, g, m, v):
    m = ADAM_B1 * m + (1.0 - ADAM_B1) * g
    v = ADAM_B2 * v + (1.0 - ADAM_B2) * _jnp.square(g)
    m_hat = m / (1.0 - ADAM_B1 ** ADAM_STEP)
    v_hat = v / (1.0 - ADAM_B2 ** ADAM_STEP)
    delta = -ADAM_LR * (m_hat / (_jnp.sqrt(v_hat) + ADAM_EPS) + ADAM_WD * w)
    return delta, m, v


def reference(x, p, norm_mix_g, w_in, pool_w, pool_b, pool_scale, conv_w, conv_b, gate_a_w, gate_a_b, gate_x_w, gate_x_b, lru_L, w_out, norm_mlp_g, w_up, w_down, norm_ple_g, w_ple_gate, b_ple_gate, w_ple_proj, norm_final_g, loss_target, m_norm_mix_g, m_w_in, m_pool_w, m_pool_b, m_pool_scale, m_conv_w, m_conv_b, m_gate_a_w, m_gate_a_b, m_gate_x_w, m_gate_x_b, m_lru_L, m_w_out, m_norm_mlp_g, m_w_up, m_w_down, m_norm_ple_g, m_w_ple_gate, m_b_ple_gate, m_w_ple_proj, m_norm_final_g, v_norm_mix_g, v_w_in, v_pool_w, v_pool_b, v_pool_scale, v_conv_w, v_conv_b, v_gate_a_w, v_gate_a_b, v_gate_x_w, v_gate_x_b, v_lru_L, v_w_out, v_norm_mlp_g, v_w_up, v_w_down, v_norm_ple_g, v_w_ple_gate, v_b_ple_gate, v_w_ple_proj, v_norm_final_g):
    given = dict(x=x, p=p, norm_mix_g=norm_mix_g, w_in=w_in, pool_w=pool_w, pool_b=pool_b, pool_scale=pool_scale, conv_w=conv_w, conv_b=conv_b, gate_a_w=gate_a_w, gate_a_b=gate_a_b, gate_x_w=gate_x_w, gate_x_b=gate_x_b, lru_L=lru_L, w_out=w_out, norm_mlp_g=norm_mlp_g, w_up=w_up, w_down=w_down, norm_ple_g=norm_ple_g, w_ple_gate=w_ple_gate, b_ple_gate=b_ple_gate, w_ple_proj=w_ple_proj, norm_final_g=norm_final_g, loss_target=loss_target, m_norm_mix_g=m_norm_mix_g, m_w_in=m_w_in, m_pool_w=m_pool_w, m_pool_b=m_pool_b, m_pool_scale=m_pool_scale, m_conv_w=m_conv_w, m_conv_b=m_conv_b, m_gate_a_w=m_gate_a_w, m_gate_a_b=m_gate_a_b, m_gate_x_w=m_gate_x_w, m_gate_x_b=m_gate_x_b, m_lru_L=m_lru_L, m_w_out=m_w_out, m_norm_mlp_g=m_norm_mlp_g, m_w_up=m_w_up, m_w_down=m_w_down, m_norm_ple_g=m_norm_ple_g, m_w_ple_gate=m_w_ple_gate, m_b_ple_gate=m_b_ple_gate, m_w_ple_proj=m_w_ple_proj, m_norm_final_g=m_norm_final_g, v_norm_mix_g=v_norm_mix_g, v_w_in=v_w_in, v_pool_w=v_pool_w, v_pool_b=v_pool_b, v_pool_scale=v_pool_scale, v_conv_w=v_conv_w, v_conv_b=v_conv_b, v_gate_a_w=v_gate_a_w, v_gate_a_b=v_gate_a_b, v_gate_x_w=v_gate_x_w, v_gate_x_b=v_gate_x_b, v_lru_L=v_lru_L, v_w_out=v_w_out, v_norm_mlp_g=v_norm_mlp_g, v_w_up=v_w_up, v_w_down=v_w_down, v_norm_ple_g=v_norm_ple_g, v_w_ple_gate=v_w_ple_gate, v_b_ple_gate=v_b_ple_gate, v_w_ple_proj=v_w_ple_proj, v_norm_final_g=v_norm_final_g)
    weights = {n: given[n] for n in TWIN_WEIGHTS}
    shared = {n: given[n] for n in SHARED_INPUTS}
    per_example = {n: given[n] for n in ['x', 'p']}
    grad_fn = _jax.value_and_grad(_loss, argnums=(0, 1))

    def one_microbatch(ex, loss_target):
        ex = dict(ex)
        diff = ex.pop(TWIN_DIFF_INPUT)
        return grad_fn(weights, diff, {**shared, **ex}, loss_target)

    if N_MICROBATCH == 1:
        loss, (grad_w, grad_x) = one_microbatch(per_example, given["loss_target"])
    else:
        def body(carry, xs):
            loss_sum, grad_sum = carry
            l_k, (gw_k, gx_k) = one_microbatch(xs[0], xs[1])
            with _jax.named_scope("update"):
                return (loss_sum + l_k, _jax.tree.map(_jnp.add, grad_sum, gw_k)), gx_k

        init = (_jnp.zeros((), _jnp.float32), _jax.tree.map(_jnp.zeros_like, weights))
        (loss, grad_w), grad_x = _jax.lax.scan(body, init, (per_example, given["loss_target"]))
    with _jax.named_scope("update"):
        delta_w, new_m, new_v = {}, {}, {}
        for n in TWIN_WEIGHTS:
            delta_w[n], new_m[n], new_v[n] = _adamw(weights[n], grad_w[n], given["m_" + n], given["v_" + n])
    return (loss, grad_x, *[grad_w[n] for n in TWIN_WEIGHTS], *[delta_w[n] for n in TWIN_WEIGHTS],
            *[new_m[n] for n in TWIN_WEIGHTS], *[new_v[n] for n in TWIN_WEIGHTS])
```

```python
import functools

import jax
import jax.numpy as jnp
from jax import lax
from jax.experimental import pallas as pl
from jax.experimental.pallas import tpu as pltpu

F32 = jnp.float32
BF16 = jnp.bfloat16
MESH = pl.DeviceIdType.MESH

D_MODEL = 1024
D_POOL = 512
D_LRU = 512
POOL_WINDOWS = (2, 4, 8, 16)
POOL_GROUP = 128
POOL_HALO = 16
CONV_WIDTH = 4
CONV_HALO = 8
LRU_HEADS = 8
LRU_BLOCK = 64
LRU_C = 8.0
D_FF = 4096
PLE_DIM = 256
D_IN_PROJ = 1536
RMS_EPS = 1e-6
N_CHIPS = 4
FF_BLOCK = D_FF // N_CHIPS

ADAM_LR = 0.001
ADAM_B1 = 0.9
ADAM_B2 = 0.999
ADAM_EPS = 1e-08
ADAM_WD = 0.01
ADAM_STEP = 10
ADAM_C1 = 1.0 / (1.0 - ADAM_B1 ** ADAM_STEP)
ADAM_C2 = 1.0 / (1.0 - ADAM_B2 ** ADAM_STEP)

VMEM_LIMIT = 56 * 1024 * 1024
GELU_C = 0.7978845608028654
GELU_A = 0.044715

NT = (((1,), (1,)), ((), ()))
TN = (((0,), (0,)), ((), ()))


def _dot(a, b):
    return jnp.dot(a, b, preferred_element_type=F32)


def _dot_nt(a, b):
    return lax.dot_general(a, b, NT, preferred_element_type=F32)


def _dot_tn(a, b):
    return lax.dot_general(a, b, TN, preferred_element_type=F32)


def _params(*sem):
    return pltpu.CompilerParams(dimension_semantics=sem, vmem_limit_bytes=VMEM_LIMIT)


def _full(shape):
    nd = len(shape)
    return pl.BlockSpec(shape, lambda *_: (0,) * nd)


def _rstd(x):
    return lax.rsqrt(jnp.mean(x * x, axis=-1, keepdims=True) + RMS_EPS)


def _rms_bwd(x, g, dz):
    xr = x * _rstd(x)
    r = _rstd(x)
    dyg = dz * g
    dx = r * (dyg - xr * jnp.mean(dyg * xr, axis=-1, keepdims=True))
    dg = jnp.sum(dz * xr, axis=0, keepdims=True)
    return dx, dg


def _sigmoid(x):
    return 1.0 / (1.0 + jnp.exp(-x))


def _log_sigmoid(v):
    u = jnp.exp(-jnp.abs(v))
    w = 1.0 + u
    l1p = jnp.where(w == 1.0, u, jnp.log(w) * u / jnp.where(w == 1.0, 1.0, w - 1.0))
    return jnp.minimum(v, 0.0) - l1p


def _gelu(x):
    t = jnp.tanh(GELU_C * (x + GELU_A * x * x * x))
    return 0.5 * x * (1.0 + t), t


def _gelu_grad(x, t):
    return 0.5 * (1.0 + t) + 0.5 * x * (1.0 - t * t) * GELU_C * (1.0 + 3.0 * GELU_A * x * x)


def _rows(shape, t0):
    return lax.broadcasted_iota(jnp.int32, shape, 0) + t0


def _pool_diff(u_pool, prev, t0):
    tm = u_pool.shape[0]
    rows = _rows((tm, POOL_GROUP), t0)
    outs, invs = [], []
    for g, w in enumerate(POOL_WINDOWS):
        sl = slice(POOL_GROUP * g, POOL_GROUP * (g + 1))
        ug = u_pool[:, sl]
        s = jnp.concatenate([prev[:, sl], ug], axis=0)
        k = 1
        while k < w:
            s = s + pltpu.roll(s, k, 0)
            k *= 2
        inv = 1.0 / jnp.minimum(rows + 1, w).astype(F32)
        outs.append(s[POOL_HALO:] * inv - ug)
        invs.append(inv)
    return jnp.concatenate(outs, axis=1), jnp.concatenate(invs, axis=1)


def _pool_diff_bwd(dd, inv, nxt):
    ddc = dd * inv
    outs = []
    for g, w in enumerate(POOL_WINDOWS):
        sl = slice(POOL_GROUP * g, POOL_GROUP * (g + 1))
        s = jnp.concatenate([ddc[:, sl], nxt[:, sl]], axis=0)
        n = s.shape[0]
        k = 1
        while k < w:
            s = s + pltpu.roll(s, n - k, 0)
            k *= 2
        outs.append(s[:n - POOL_HALO] - dd[:, sl])
    return jnp.concatenate(outs, axis=1), ddc


def _conv_taps(u, prev):
    ext = jnp.concatenate([prev, u], axis=0)
    return [pltpu.roll(ext, CONV_WIDTH - 1 - k, 0)[CONV_HALO:] if k < CONV_WIDTH - 1 else u for k in range(CONV_WIDTH)]


def _scan_fwd(a, b):
    tm = a.shape[0]
    rows = _rows(a.shape, 0)
    k = 1
    while k < tm:
        ar = pltpu.roll(a, k, 0)
        br = pltpu.roll(b, k, 0)
        m = rows >= k
        b = jnp.where(m, a * br + b, b)
        a = jnp.where(m, a * ar, a)
        k *= 2
    return a, b


def _scan_rev(a, b):
    tm = a.shape[0]
    rows = _rows(a.shape, 0)
    k = 1
    while k < tm:
        ar = pltpu.roll(a, tm - k, 0)
        br = pltpu.roll(b, tm - k, 0)
        m = rows < tm - k
        b = jnp.where(m, a * br + b, b)
        a = jnp.where(m, a * ar, a)
        k *= 2
    return a, b


def _lru_gates(xb, wa, ba, wx, bx, lsl8, t0):
    xbb = xb.astype(BF16)
    r = _sigmoid(_dot(xbb, wa) + ba)
    ig = _sigmoid(_dot(xbb, wx) + bx)
    a = jnp.exp(r * lsl8)
    first = _rows(xb.shape, t0) == 0
    mult = jnp.where(first, 1.0, jnp.sqrt(1.0 - a * a))
    return r, ig, a, mult, first


def _fwd_mix(x, g1, w_in, pool_w, pool_b, pool_scale, conv_w, conv_b, wa, ba, wx, bx, lru_l, w_out, tm):
    s_len = x.shape[0]
    n = s_len // tm

    def body(x_ref, g1_ref, win_ref, pw_ref, pb_ref, ps_ref, cw_ref, cb_ref, wa_ref, ba_ref, wx_ref, bx_ref, l_ref,
             wout_ref, z1_ref, proj_ref, h_ref, cat_ref, h1_ref, cpool, clru, ch):
        i = pl.program_id(0)

        @pl.when(i == 0)
        def _():
            cpool[...] = jnp.zeros_like(cpool)
            clru[...] = jnp.zeros_like(clru)
            ch[...] = jnp.zeros_like(ch)

        t0 = i * tm
        xv = x_ref[...]
        zb = (xv * _rstd(xv) * g1_ref[...]).astype(BF16)
        z1_ref[...] = zb
        proj = jnp.concatenate([_dot(zb, win_ref[j]) for j in range(N_CHIPS)], axis=1)
        proj_ref[...] = proj
        u_pool = proj[:, :D_POOL]
        u_lru = proj[:, D_POOL:D_POOL + D_LRU]
        u_gate = proj[:, D_POOL + D_LRU:]

        d, _ = _pool_diff(u_pool, cpool[...], t0)
        cpool[...] = u_pool[tm - POOL_HALO:]
        db = d.astype(BF16)
        yp = jnp.concatenate(
            [_dot(db[:, POOL_GROUP * g:POOL_GROUP * (g + 1)], pw_ref[g]) for g in range(len(POOL_WINDOWS))], axis=1)
        y_pool = (yp + pb_ref[...]) * ps_ref[...]

        taps = _conv_taps(u_lru, clru[...])
        clru[...] = u_lru[tm - CONV_HALO:]
        xb = cb_ref[...]
        for k in range(CONV_WIDTH):
            xb = xb + taps[k] * cw_ref[k:k + 1, :]
        lsl8 = LRU_C * _log_sigmoid(l_ref[...])
        _, ig, a, mult, _ = _lru_gates(xb, wa_ref[...], ba_ref[...], wx_ref[...], bx_ref[...], lsl8, t0)
        pa, hb = _scan_fwd(a, mult * (ig * xb))
        h = hb + pa * ch[7:8, :]
        ch[...] = h[tm - 8:]
        h_ref[...] = h
        gl, _ = _gelu(u_gate)
        cat = jnp.concatenate([y_pool, h * gl], axis=1).astype(BF16)
        cat_ref[...] = cat
        h1_ref[...] = xv + _dot(cat, wout_ref[...])

    row = lambda w: pl.BlockSpec((tm, w), lambda i: (i, 0))
    ins = [x, g1, w_in, pool_w, pool_b, pool_scale, conv_w, conv_b, wa, ba, wx, bx, lru_l, w_out]
    return pl.pallas_call(
        body, name="fwd_mix", grid=(n,),
        in_specs=[row(D_MODEL)] + [_full(a.shape) for a in ins[1:]],
        out_specs=[row(D_MODEL), row(D_IN_PROJ), row(D_LRU), row(D_MODEL), row(D_MODEL)],
        out_shape=[jax.ShapeDtypeStruct((s_len, D_MODEL), BF16), jax.ShapeDtypeStruct((s_len, D_IN_PROJ), F32),
                   jax.ShapeDtypeStruct((s_len, D_LRU), F32), jax.ShapeDtypeStruct((s_len, D_MODEL), BF16),
                   jax.ShapeDtypeStruct((s_len, D_MODEL), F32)],
        scratch_shapes=[pltpu.VMEM((POOL_HALO, D_POOL), F32), pltpu.VMEM((CONV_HALO, D_LRU), F32),
                        pltpu.VMEM((8, D_LRU), F32)],
        compiler_params=_params("arbitrary"),
    )(*ins)


def _fwd_mlp(h1, g2, w_up, w_down, tm):
    s_len = h1.shape[0]
    n = s_len // tm

    def body(h1_ref, g2_ref, wup_ref, wdn_ref, z2_ref, ru_ref, h2_ref, acc):
        j = pl.program_id(1)

        @pl.when(j == 0)
        def _():
            hv = h1_ref[...]
            z2_ref[...] = (hv * _rstd(hv) * g2_ref[...]).astype(BF16)
            acc[...] = hv

        ru = jnp.maximum(_dot(z2_ref[...], wup_ref[0]), 0.0)
        ru_ref[...] = ru.astype(BF16)
        acc[...] += _dot((ru * ru).astype(BF16), wdn_ref[0])

        @pl.when(j == N_CHIPS - 1)
        def _():
            h2_ref[...] = acc[...]

    return pl.pallas_call(
        body, name="fwd_mlp", grid=(n, N_CHIPS),
        in_specs=[pl.BlockSpec((tm, D_MODEL), lambda i, j: (i, 0)), _full(g2.shape),
                  pl.BlockSpec((1, D_MODEL, FF_BLOCK), lambda i, j: (j, 0, 0)),
                  pl.BlockSpec((1, FF_BLOCK, D_MODEL), lambda i, j: (j, 0, 0))],
        out_specs=[pl.BlockSpec((tm, D_MODEL), lambda i, j: (i, 0)),
                   pl.BlockSpec((tm, FF_BLOCK), lambda i, j: (i, j)),
                   pl.BlockSpec((tm, D_MODEL), lambda i, j: (i, 0))],
        out_shape=[jax.ShapeDtypeStruct((s_len, D_MODEL), BF16), jax.ShapeDtypeStruct((s_len, D_FF), BF16),
                   jax.ShapeDtypeStruct((s_len, D_MODEL), F32)],
        scratch_shapes=[pltpu.VMEM((tm, D_MODEL), F32)],
        compiler_params=_params("arbitrary", "arbitrary"),
    )(h1, g2, w_up, w_down)


def _head(h2, p, target, g3, w_pg, b_pg, w_pp, g4, tm):
    s_len = h2.shape[0]
    n = s_len // tm

    def body(h2_ref, p_ref, t_ref, g3_ref, wpg_ref, bpg_ref, wpp_ref, g4_ref,
             dh2_ref, dwpg_ref, dwpp_ref, vec_ref, a_pg, a_pp, a_vec):
        i = pl.program_id(0)

        @pl.when(i == 0)
        def _():
            a_pg[...] = jnp.zeros_like(a_pg)
            a_pp[...] = jnp.zeros_like(a_pp)
            a_vec[...] = jnp.zeros_like(a_vec)

        h2v = h2_ref[...]
        g3v = g3_ref[...]
        g4v = g4_ref[...]
        z3 = (h2v * _rstd(h2v) * g3v).astype(BF16)
        gate = _sigmoid(_dot(z3, wpg_ref[...]) + bpg_ref[...])
        pb = p_ref[...].astype(BF16)
        pp = _dot(pb, wpp_ref[...])
        h3 = h2v + gate * pp
        r4 = _rstd(h3)
        diff = h3 * r4 * g4v - t_ref[...]
        loss = 0.5 * jnp.sum(jnp.mean(diff * diff, axis=-1, keepdims=True), axis=0, keepdims=True)
        dy = diff * (1.0 / D_MODEL)
        dh3, dg4 = _rms_bwd(h3, g4v, dy)
        dpp = (dh3 * gate).astype(BF16)
        dpre = dh3 * pp * gate * (1.0 - gate)
        dpreb = dpre.astype(BF16)
        dz3 = _dot_nt(dpreb, wpg_ref[...])
        dx, dg3 = _rms_bwd(h2v, g3v, dz3)
        dh2_ref[...] = dh3 + dx
        a_pg[...] += _dot_tn(z3, dpreb)
        a_pp[...] += _dot_tn(pb, dpp)
        a_vec[0:1, :] += dg3
        a_vec[1:2, :] += dg4
        a_vec[2:3, :] += jnp.sum(dpre, axis=0, keepdims=True)
        a_vec[3:4, :] += jnp.broadcast_to(loss, (1, D_MODEL))

        @pl.when(i == n - 1)
        def _():
            dwpg_ref[...] = a_pg[...].astype(BF16)
            for j in range(N_CHIPS):
                dwpp_ref[j] = a_pp[:, PLE_DIM * j:PLE_DIM * (j + 1)].astype(BF16)
            vec_ref[...] = a_vec[...]

    row = lambda w: pl.BlockSpec((tm, w), lambda i: (i, 0))
    ins = [h2, p, target, g3, w_pg, b_pg, w_pp, g4]
    return pl.pallas_call(
        body, name="head", grid=(n,),
        in_specs=[row(D_MODEL), row(PLE_DIM), row(D_MODEL)] + [_full(a.shape) for a in ins[3:]],
        out_specs=[row(D_MODEL), _full((D_MODEL, D_MODEL)), _full((N_CHIPS, PLE_DIM, PLE_DIM)), _full((8, D_MODEL))],
        out_shape=[jax.ShapeDtypeStruct((s_len, D_MODEL), F32), jax.ShapeDtypeStruct((D_MODEL, D_MODEL), BF16),
                   jax.ShapeDtypeStruct((N_CHIPS, PLE_DIM, PLE_DIM), BF16), jax.ShapeDtypeStruct((8, D_MODEL), F32)],
        scratch_shapes=[pltpu.VMEM((D_MODEL, D_MODEL), F32), pltpu.VMEM((PLE_DIM, D_MODEL), F32),
                        pltpu.VMEM((8, D_MODEL), F32)],
        compiler_params=_params("arbitrary"),
    )(*ins)


def _bwd_mlp_x(dh2, ru, h1, g2, w_up, w_down, tm):
    s_len = dh2.shape[0]
    n = s_len // tm

    def body(dh2_ref, ru_ref, h1_ref, g2_ref, wup_ref, wdn_ref, dup_ref, dh1_ref, dg2_ref, acc, a_g):
        i = pl.program_id(0)
        j = pl.program_id(1)

        @pl.when((i == 0) & (j == 0))
        def _():
            a_g[...] = jnp.zeros_like(a_g)

        @pl.when(j == 0)
        def _():
            acc[...] = jnp.zeros_like(acc)

        dact = _dot_nt(dh2_ref[...].astype(BF16), wdn_ref[0])
        dup = (dact * (2.0 * ru_ref[...].astype(F32))).astype(BF16)
        dup_ref[...] = dup
        acc[...] += _dot_nt(dup, wup_ref[0])

        @pl.when(j == N_CHIPS - 1)
        def _():
            dx, dg = _rms_bwd(h1_ref[...], g2_ref[...], acc[...])
            dh1_ref[...] = dh2_ref[...] + dx
            a_g[0:1, :] += dg

        @pl.when((i == n - 1) & (j == N_CHIPS - 1))
        def _():
            dg2_ref[...] = a_g[...]

    tile = pl.BlockSpec((tm, D_MODEL), lambda i, j: (i, 0))
    ffb = pl.BlockSpec((tm, FF_BLOCK), lambda i, j: (i, j))
    return pl.pallas_call(
        body, name="bwd_mlp_x", grid=(n, N_CHIPS),
        in_specs=[tile, ffb, tile, _full(g2.shape),
                  pl.BlockSpec((1, D_MODEL, FF_BLOCK), lambda i, j: (j, 0, 0)),
                  pl.BlockSpec((1, FF_BLOCK, D_MODEL), lambda i, j: (j, 0, 0))],
        out_specs=[ffb, tile, _full((8, D_MODEL))],
        out_shape=[jax.ShapeDtypeStruct((s_len, D_FF), BF16), jax.ShapeDtypeStruct((s_len, D_MODEL), F32),
                   jax.ShapeDtypeStruct((8, D_MODEL), F32)],
        scratch_shapes=[pltpu.VMEM((tm, D_MODEL), F32), pltpu.VMEM((8, D_MODEL), F32)],
        compiler_params=_params("arbitrary", "arbitrary"),
    )(dh2, ru, h1, g2, w_up, w_down)


def _bwd_mlp_w(z2, dup, ru, dh2, tk):
    s_len = z2.shape[0]
    n = s_len // tk

    def body(z2_ref, dup_ref, ru_ref, dh2_ref, dwup_ref, dwdn_ref, a_up, a_dn):
        t = pl.program_id(1)

        @pl.when(t == 0)
        def _():
            a_up[...] = jnp.zeros_like(a_up)
            a_dn[...] = jnp.zeros_like(a_dn)

        ruv = ru_ref[...]
        a_up[...] += _dot_tn(z2_ref[...], dup_ref[...])
        a_dn[...] += _dot_tn(ruv * ruv, dh2_ref[...].astype(BF16))

        @pl.when(t == n - 1)
        def _():
            dwup_ref[0] = a_up[...].astype(BF16)
            dwdn_ref[0] = a_dn[...].astype(BF16)

    tile = pl.BlockSpec((tk, D_MODEL), lambda j, t: (t, 0))
    ffb = pl.BlockSpec((tk, FF_BLOCK), lambda j, t: (t, j))
    return pl.pallas_call(
        body, name="bwd_mlp_w", grid=(N_CHIPS, n),
        in_specs=[tile, ffb, ffb, tile],
        out_specs=[pl.BlockSpec((1, D_MODEL, FF_BLOCK), lambda j, t: (j, 0, 0)),
                   pl.BlockSpec((1, FF_BLOCK, D_MODEL), lambda j, t: (j, 0, 0))],
        out_shape=[jax.ShapeDtypeStruct((N_CHIPS, D_MODEL, FF_BLOCK), BF16),
                   jax.ShapeDtypeStruct((N_CHIPS, FF_BLOCK, D_MODEL), BF16)],
        scratch_shapes=[pltpu.VMEM((D_MODEL, FF_BLOCK), F32), pltpu.VMEM((FF_BLOCK, D_MODEL), F32)],
        compiler_params=_params("arbitrary", "arbitrary"),
    )(z2, dup, ru, dh2)


MIX_VEC_ROWS = 16


def _bwd_mix(dh1, proj, h, cat, pool_w, pool_b, pool_scale, conv_w, conv_b, wa, ba, wx, bx, lru_l, w_out, tm):
    s_len = dh1.shape[0]
    n = s_len // tm
    ng = len(POOL_WINDOWS)

    def body(dh1_ref, proj_ref, h_ref, cat_ref, projh_ref, hh_ref, pw_ref, pb_ref, ps_ref, cw_ref, cb_ref,
             wa_ref, ba_ref, wx_ref, bx_ref, l_ref, wout_ref,
             dproj_ref, dwout_ref, dpw_ref, dwa_ref, dwx_ref, vec_ref,
             a_out, a_pw, a_wa, a_wx, a_vec, c_g, c_dxb, c_ddc):
        q = pl.program_id(0)
        i = n - 1 - q

        @pl.when(q == 0)
        def _():
            for r in (a_out, a_pw, a_wa, a_wx, a_vec, c_g, c_dxb, c_ddc):
                r[...] = jnp.zeros_like(r)

        t0 = i * tm
        has_prev = (i > 0).astype(F32)
        dh1b = dh1_ref[...].astype(BF16)
        dcat = _dot_nt(dh1b, wout_ref[...])
        a_out[...] += _dot_tn(cat_ref[...], dh1b)
        dy_pool = dcat[:, :D_POOL]
        dy_lru = dcat[:, D_POOL:]

        proj = proj_ref[...]
        u_pool = proj[:, :D_POOL]
        u_lru = proj[:, D_POOL:D_POOL + D_LRU]
        u_gate = proj[:, D_POOL + D_LRU:]
        halo = projh_ref[...] * has_prev

        d, inv = _pool_diff(u_pool, halo[:, :D_POOL], t0)
        db = d.astype(BF16)
        ypre = jnp.concatenate(
            [_dot(db[:, POOL_GROUP * g:POOL_GROUP * (g + 1)], pw_ref[g]) for g in range(ng)], axis=1) + pb_ref[...]
        dyp = dy_pool * ps_ref[...]
        dypb = dyp.astype(BF16)
        dds = []
        for g in range(ng):
            sl = slice(POOL_GROUP * g, POOL_GROUP * (g + 1))
            a_pw[g] += _dot_tn(db[:, sl], dypb[:, sl])
            dds.append(_dot_nt(dypb[:, sl], pw_ref[g]))
        du_pool, ddc = _pool_diff_bwd(jnp.concatenate(dds, axis=1), inv, c_ddc[...])
        c_ddc[...] = ddc[:POOL_HALO]
        a_vec[0:1, :] += jnp.sum(dyp, axis=0, keepdims=True)
        a_vec[1:2, :] += jnp.sum(dy_pool * ypre, axis=0, keepdims=True)

        taps = _conv_taps(u_lru, halo[POOL_HALO - CONV_HALO:, D_POOL:D_POOL + D_LRU])
        xb = cb_ref[...]
        for k in range(CONV_WIDTH):
            xb = xb + taps[k] * cw_ref[k:k + 1, :]
        lv = l_ref[...]
        lsl8 = LRU_C * _log_sigmoid(lv)
        r, ig, a, mult, first = _lru_gates(xb, wa_ref[...], ba_ref[...], wx_ref[...], bx_ref[...], lsl8, t0)
        hv = h_ref[...]
        gl, th = _gelu(u_gate)
        du_gate = dy_lru * hv * _gelu_grad(u_gate, th)
        last = _rows(a.shape, 0) == tm - 1
        a_next = jnp.where(last, 1.0, pltpu.roll(a, tm - 1, 0))
        pa, gb = _scan_rev(a_next, dy_lru * gl)
        gh = gb + pa * c_g[0:1, :]
        c_g[...] = jnp.broadcast_to(a[0:1, :] * gh[0:1, :], c_g.shape)
        h_prev_row = hh_ref[7:8, :] * has_prev
        h_prev = jnp.where(_rows(hv.shape, 0) == 0, h_prev_row, pltpu.roll(hv, 1, 0))
        gix = gh * ig * xb
        dla = gh * h_prev * a - jnp.where(first, 0.0, gix * a * a / mult)
        dpre_r = dla * lsl8 * r * (1.0 - r)
        dpre_i = gh * mult * xb * ig * (1.0 - ig)
        dprb = dpre_r.astype(BF16)
        dpib = dpre_i.astype(BF16)
        xbb = xb.astype(BF16)
        a_wa[...] += _dot_tn(xbb, dprb)
        a_wx[...] += _dot_tn(xbb, dpib)
        dxb = gh * mult * ig + _dot_nt(dprb, wa_ref[...]) + _dot_nt(dpib, wx_ref[...])
        a_vec[2:3, :] += jnp.sum(dxb, axis=0, keepdims=True)
        a_vec[3:4, :] += jnp.sum(dpre_r, axis=0, keepdims=True)
        a_vec[4:5, :] += jnp.sum(dpre_i, axis=0, keepdims=True)
        a_vec[5:6, :] += jnp.sum(dla * r, axis=0, keepdims=True)
        ext = jnp.concatenate([dxb, c_dxb[...]], axis=0)
        c_dxb[...] = dxb[:CONV_HALO]
        ne = tm + CONV_HALO
        du_lru = dxb * cw_ref[CONV_WIDTH - 1:CONV_WIDTH, :]
        for k in range(CONV_WIDTH):
            a_vec[8 + k:9 + k, :] += jnp.sum(dxb * taps[k], axis=0, keepdims=True)
            if k < CONV_WIDTH - 1:
                du_lru = du_lru + pltpu.roll(ext, ne - (CONV_WIDTH - 1 - k), 0)[:tm] * cw_ref[k:k + 1, :]
        dproj_ref[...] = jnp.concatenate([du_pool, du_lru, du_gate], axis=1).astype(BF16)

        @pl.when(q == n - 1)
        def _():
            dwout_ref[...] = a_out[...].astype(BF16)
            dpw_ref[...] = a_pw[...]
            dwa_ref[...] = a_wa[...]
            dwx_ref[...] = a_wx[...]
            vec_ref[...] = a_vec[...]
            vec_ref[5:6, :] = a_vec[5:6, :] * (LRU_C * _sigmoid(-lv))

    rev = lambda w: pl.BlockSpec((tm, w), lambda q: (n - 1 - q, 0))
    halo_p = pl.BlockSpec((POOL_HALO, D_IN_PROJ), lambda q: (jnp.maximum((n - 1 - q) * (tm // POOL_HALO) - 1, 0), 0))
    halo_h = pl.BlockSpec((8, D_LRU), lambda q: (jnp.maximum((n - 1 - q) * (tm // 8) - 1, 0), 0))
    wts = [pool_w, pool_b, pool_scale, conv_w, conv_b, wa, ba, wx, bx, lru_l, w_out]
    return pl.pallas_call(
        body, name="bwd_mix", grid=(n,),
        in_specs=[rev(D_MODEL), rev(D_IN_PROJ), rev(D_LRU), rev(D_MODEL), halo_p, halo_h] + [_full(a.shape) for a in wts],
        out_specs=[rev(D_IN_PROJ), _full((D_MODEL, D_MODEL)), _full((ng, POOL_GROUP, POOL_GROUP)),
                   _full((D_LRU, D_LRU)), _full((D_LRU, D_LRU)), _full((MIX_VEC_ROWS, D_LRU))],
        out_shape=[jax.ShapeDtypeStruct((s_len, D_IN_PROJ), BF16), jax.ShapeDtypeStruct((D_MODEL, D_MODEL), BF16),
                   jax.ShapeDtypeStruct((ng, POOL_GROUP, POOL_GROUP), F32), jax.ShapeDtypeStruct((D_LRU, D_LRU), F32),
                   jax.ShapeDtypeStruct((D_LRU, D_LRU), F32), jax.ShapeDtypeStruct((MIX_VEC_ROWS, D_LRU), F32)],
        scratch_shapes=[pltpu.VMEM((D_MODEL, D_MODEL), F32), pltpu.VMEM((ng, POOL_GROUP, POOL_GROUP), F32),
                        pltpu.VMEM((D_LRU, D_LRU), F32), pltpu.VMEM((D_LRU, D_LRU), F32),
                        pltpu.VMEM((MIX_VEC_ROWS, D_LRU), F32), pltpu.VMEM((8, D_LRU), F32),
                        pltpu.VMEM((CONV_HALO, D_LRU), F32), pltpu.VMEM((POOL_HALO, D_POOL), F32)],
        compiler_params=_params("arbitrary"),
    )(dh1, proj, h, cat, proj, h, *wts)


def _bwd_in(dproj, z1, x, dh1, g1, w_in, tm):
    s_len = x.shape[0]
    n = s_len // tm
    cb = D_IN_PROJ // N_CHIPS

    def body(dp_ref, z1_ref, x_ref, dh1_ref, g1_ref, win_ref, dx_ref, dwin_ref, dg1_ref, a_w, a_g):
        i = pl.program_id(0)

        @pl.when(i == 0)
        def _():
            a_w[...] = jnp.zeros_like(a_w)
            a_g[...] = jnp.zeros_like(a_g)

        dp = dp_ref[...]
        zb = z1_ref[...]
        dz = jnp.zeros((tm, D_MODEL), F32)
        for j in range(N_CHIPS):
            dpj = dp[:, cb * j:cb * (j + 1)]
            dz = dz + _dot_nt(dpj, win_ref[j])
            a_w[j] += _dot_tn(zb, dpj)
        dx, dg = _rms_bwd(x_ref[...], g1_ref[...], dz)
        dx_ref[...] = dh1_ref[...] + dx
        a_g[0:1, :] += dg

        @pl.when(i == n - 1)
        def _():
            dwin_ref[...] = a_w[...].astype(BF16)
            dg1_ref[...] = a_g[...]

    row = lambda w: pl.BlockSpec((tm, w), lambda i: (i, 0))
    return pl.pallas_call(
        body, name="bwd_in", grid=(n,),
        in_specs=[row(D_IN_PROJ), row(D_MODEL), row(D_MODEL), row(D_MODEL), _full(g1.shape), _full(w_in.shape)],
        out_specs=[row(D_MODEL), _full(w_in.shape), _full((8, D_MODEL))],
        out_shape=[jax.ShapeDtypeStruct((s_len, D_MODEL), F32), jax.ShapeDtypeStruct(w_in.shape, BF16),
                   jax.ShapeDtypeStruct((8, D_MODEL), F32)],
        scratch_shapes=[pltpu.VMEM(w_in.shape, F32), pltpu.VMEM((8, D_MODEL), F32)],
        compiler_params=_params("arbitrary"),
    )(dproj, z1, x, dh1, g1, w_in)


def _place():
    x, y, c = lax.axis_index("x"), lax.axis_index("y"), lax.axis_index("c")
    chips = [(1 - x, y), (x, 1 - y), (1 - x, 1 - y)]
    return x, y, c, chips


def _rcopy(src, dst, ssem, rsem, dev):
    return pltpu.make_async_remote_copy(src_ref=src, dst_ref=dst, send_sem=ssem, recv_sem=rsem,
                                        device_id=dev, device_id_type=MESH)


ANY = pl.BlockSpec(memory_space=pl.ANY)


def _gather_weights(shards):
    nw = len(shards)

    def body(*refs):
        ins, outs = refs[:nw], refs[nw:2 * nw]
        ssem, rsem, lsem = refs[2 * nw:]
        x, y, c, chips = _place()
        me = 2 * x + y
        sib = (x, y, 1 - c)
        half = [s.shape[0] // 2 for s in shards]
        local, sends = [], []
        for w in range(nw):
            cp = pltpu.make_async_copy(ins[w], outs[w].at[me], lsem.at[w])
            cp.start()
            local.append(cp)
            for s, (px, py) in enumerate(chips):
                cp = _rcopy(ins[w].at[pl.ds(c * half[w], half[w])], outs[w].at[me, pl.ds(c * half[w], half[w])],
                            ssem.at[w, s], rsem.at[w, s], (px, py, c))
                cp.start()
                sends.append(cp)
        for w in range(nw):
            for s, (px, py) in enumerate(chips):
                blk = outs[w].at[2 * px + py, pl.ds(c * half[w], half[w])]
                _rcopy(blk, blk, ssem.at[w, s], rsem.at[w, s], sib).wait_recv()
                cp = _rcopy(blk, blk, ssem.at[w, 3 + s], rsem.at[w, 3 + s], sib)
                cp.start()
                sends.append(cp)
        for w in range(nw):
            for s, (px, py) in enumerate(chips):
                blk = outs[w].at[2 * px + py, pl.ds((1 - c) * half[w], half[w])]
                _rcopy(blk, blk, ssem.at[w, 3 + s], rsem.at[w, 3 + s], sib).wait_recv()
        for cp in sends:
            cp.wait_send()
        for cp in local:
            cp.wait()

    return pl.pallas_call(
        body, name="gather_weights",
        in_specs=[ANY] * nw, out_specs=[ANY] * nw,
        out_shape=[jax.ShapeDtypeStruct((N_CHIPS,) + s.shape, s.dtype) for s in shards],
        scratch_shapes=[pltpu.SemaphoreType.DMA((nw, 6)), pltpu.SemaphoreType.DMA((nw, 6)),
                        pltpu.SemaphoreType.DMA((nw,))],
    )(*shards)


def _swap_halves(grads):
    nw = len(grads)

    def body(*refs):
        ins, own, got = refs[:nw], refs[nw:2 * nw], refs[2 * nw:3 * nw]
        ssem, rsem, lsem = refs[3 * nw:]
        x, y, c, _ = _place()
        cps = []
        for w in range(nw):
            hr = grads[w].shape[1] // 2
            lc = pltpu.make_async_copy(ins[w].at[:, pl.ds(c * hr, hr)], own[w], lsem.at[w])
            lc.start()
            rc = _rcopy(ins[w].at[:, pl.ds((1 - c) * hr, hr)], got[w], ssem.at[w], rsem.at[w], (x, y, 1 - c))
            rc.start()
            cps += [lc, rc]
        for cp in cps:
            cp.wait()

    halves = [jax.ShapeDtypeStruct((g.shape[0], g.shape[1] // 2, g.shape[2]), g.dtype) for g in grads]
    outs = pl.pallas_call(
        body, name="swap_halves",
        in_specs=[ANY] * nw, out_specs=[ANY] * (2 * nw), out_shape=halves + halves,
        scratch_shapes=[pltpu.SemaphoreType.DMA((nw,)), pltpu.SemaphoreType.DMA((nw,)), pltpu.SemaphoreType.DMA((nw,))],
    )(*grads)
    return outs[:nw], outs[nw:]


def _add_pairs(a_list, b_list, steps):
    nw = len(a_list)

    def body(*refs):
        for w in range(nw):
            refs[2 * nw + w][...] = (refs[w][...].astype(F32) + refs[nw + w][...].astype(F32)).astype(BF16)

    specs = [pl.BlockSpec((a.shape[0], a.shape[1] // steps, a.shape[2]), lambda i: (0, i, 0)) for a in a_list]
    return pl.pallas_call(
        body, name="add_pairs", grid=(steps,),
        in_specs=specs + specs, out_specs=specs,
        out_shape=[jax.ShapeDtypeStruct(a.shape, BF16) for a in a_list],
        compiler_params=_params("arbitrary"),
    )(*a_list, *b_list)


def _scatter_chips(parts):
    nw = len(parts)

    def body(*refs):
        ins, got = refs[:nw], refs[nw:2 * nw]
        ssem, rsem = refs[2 * nw:]
        x, y, c, chips = _place()
        cps = []
        for w in range(nw):
            for s, (px, py) in enumerate(chips):
                cp = _rcopy(ins[w].at[2 * px + py], got[w].at[s], ssem.at[w, s], rsem.at[w, s], (px, py, c))
                cp.start()
                cps.append(cp)
        for cp in cps:
            cp.wait()

    return pl.pallas_call(
        body, name="scatter_chips",
        in_specs=[ANY] * nw, out_specs=[ANY] * nw,
        out_shape=[jax.ShapeDtypeStruct((3,) + p.shape[1:], p.dtype) for p in parts],
        scratch_shapes=[pltpu.SemaphoreType.DMA((nw, 3)), pltpu.SemaphoreType.DMA((nw, 3))],
    )(*parts)


def _sum_chips(parts, got, chip, steps):
    nw = len(parts)

    def body(chip_ref, *refs):
        for w in range(nw):
            acc = refs[w][0].astype(F32)
            for s in range(3):
                acc = acc + refs[nw + w][s].astype(F32)
            refs[2 * nw + w][...] = acc

    own = [pl.BlockSpec((1, p.shape[1] // steps, p.shape[2]), lambda i, ch: (ch[0], i, 0)) for p in parts]
    rec = [pl.BlockSpec((3, p.shape[1] // steps, p.shape[2]), lambda i, ch: (0, i, 0)) for p in parts]
    outs = [pl.BlockSpec((p.shape[1] // steps, p.shape[2]), lambda i, ch: (i, 0)) for p in parts]
    return pl.pallas_call(
        body, name="sum_chips",
        grid_spec=pltpu.PrefetchScalarGridSpec(num_scalar_prefetch=1, grid=(steps,), in_specs=own + rec, out_specs=outs),
        out_shape=[jax.ShapeDtypeStruct(p.shape[1:], F32) for p in parts],
        compiler_params=_params("arbitrary"),
    )(chip, *parts, *got)


def _join_halves(halves):
    nw = len(halves)

    def body(*refs):
        ins, outs = refs[:nw], refs[nw:2 * nw]
        ssem, rsem, lsem = refs[2 * nw:]
        x, y, c, _ = _place()
        cps = []
        for w in range(nw):
            hr = halves[w].shape[0]
            dst = outs[w].at[pl.ds(c * hr, hr)]
            lc = pltpu.make_async_copy(ins[w], dst, lsem.at[w])
            lc.start()
            rc = _rcopy(ins[w], dst, ssem.at[w], rsem.at[w], (x, y, 1 - c))
            rc.start()
            cps += [lc, rc]
        for cp in cps:
            cp.wait()

    return pl.pallas_call(
        body, name="join_halves",
        in_specs=[ANY] * nw, out_specs=[ANY] * nw,
        out_shape=[jax.ShapeDtypeStruct((2 * h.shape[0], h.shape[1]), F32) for h in halves],
        scratch_shapes=[pltpu.SemaphoreType.DMA((nw,)), pltpu.SemaphoreType.DMA((nw,)), pltpu.SemaphoreType.DMA((nw,))],
    )(*halves)


def _allreduce_small(packed):
    shape = packed.shape

    def body(p_ref, out_ref, rbuf, ssem, rsem):
        x, y, c, _ = _place()
        out_ref[...] = p_ref[...]
        for st, peer in enumerate([(x, y, 1 - c), (1 - x, y, c), (x, 1 - y, c)]):
            cp = _rcopy(out_ref, rbuf.at[st], ssem.at[st], rsem.at[st], peer)
            cp.start()
            cp.wait()
            out_ref[...] = out_ref[...] + rbuf[st]

    vm = pl.BlockSpec(memory_space=pltpu.VMEM)
    return pl.pallas_call(
        body, name="allreduce_small",
        in_specs=[vm], out_specs=vm, out_shape=jax.ShapeDtypeStruct(shape, F32),
        scratch_shapes=[pltpu.VMEM((3,) + shape, F32), pltpu.SemaphoreType.DMA((3,)), pltpu.SemaphoreType.DMA((3,))],
        compiler_params=pltpu.CompilerParams(vmem_limit_bytes=VMEM_LIMIT),
    )(packed)


def _adamw_math(w, g, m, v):
    m = ADAM_B1 * m + (1.0 - ADAM_B1) * g
    v = ADAM_B2 * v + (1.0 - ADAM_B2) * (g * g)
    delta = -ADAM_LR * ((m * ADAM_C1) / (jnp.sqrt(v * ADAM_C2) + ADAM_EPS) + ADAM_WD * w)
    return delta, m, v


def _adamw(ws, gs, ms, vs, steps, name):
    nw = len(ws)

    def body(*refs):
        for k in range(nw):
            d, m, v = _adamw_math(refs[k][...], refs[nw + k][...], refs[2 * nw + k][...], refs[3 * nw + k][...])
            refs[4 * nw + k][...] = d
            refs[5 * nw + k][...] = m
            refs[6 * nw + k][...] = v

    specs = [pl.BlockSpec((a.shape[0] // steps, a.shape[1]), lambda i: (i, 0)) for a in ws]
    shapes = [jax.ShapeDtypeStruct(a.shape, F32) for a in ws]
    outs = pl.pallas_call(
        body, name=name, grid=(steps,),
        in_specs=specs * 4, out_specs=specs * 3, out_shape=shapes * 3,
        compiler_params=_params("arbitrary"),
    )(*ws, *gs, *ms, *vs)
    return outs[:nw], outs[nw:2 * nw], outs[2 * nw:]


SMALL = ["norm_mix_g", "pool_w", "pool_b", "pool_scale", "conv_b", "gate_a_w", "gate_a_b", "gate_x_w", "gate_x_b",
         "lru_L", "norm_mlp_g", "norm_ple_g", "b_ple_gate", "norm_final_g"]
BIG = ["w_in", "w_out", "w_up", "w_down", "w_ple_gate", "w_ple_proj"]
ORDER = ["norm_mix_g", "w_in", "pool_w", "pool_b", "pool_scale", "conv_w", "conv_b", "gate_a_w", "gate_a_b", "gate_x_w",
         "gate_x_b", "lru_L", "w_out", "norm_mlp_g", "w_up", "w_down", "norm_ple_g", "w_ple_gate", "b_ple_gate",
         "w_ple_proj", "norm_final_g"]
LANES = 128


def _block_diag(w):
    eye = jnp.eye(LRU_HEADS, dtype=w.dtype)
    return jnp.einsum("hij,hk->hikj", w, eye).reshape(D_LRU, D_LRU)


def _diag_blocks(full):
    f = full.reshape(LRU_HEADS, LRU_BLOCK, LRU_HEADS, LRU_BLOCK)
    return jnp.stack([f[h, :, h, :] for h in range(LRU_HEADS)])


def _rows128(a):
    return a.reshape(-1, LANES)


def _pad8(a):
    r = (-a.shape[0]) % 8
    return jnp.pad(a, ((0, r), (0, 0))) if r else a


def kernel(x, p, norm_mix_g, w_in, pool_w, pool_b, pool_scale, conv_w, conv_b, gate_a_w, gate_a_b, gate_x_w, gate_x_b, lru_L, w_out, norm_mlp_g, w_up, w_down, norm_ple_g, w_ple_gate, b_ple_gate, w_ple_proj, norm_final_g, loss_target, m_norm_mix_g, m_w_in, m_pool_w, m_pool_b, m_pool_scale, m_conv_w, m_conv_b, m_gate_a_w, m_gate_a_b, m_gate_x_w, m_gate_x_b, m_lru_L, m_w_out, m_norm_mlp_g, m_w_up, m_w_down, m_norm_ple_g, m_w_ple_gate, m_b_ple_gate, m_w_ple_proj, m_norm_final_g, v_norm_mix_g, v_w_in, v_pool_w, v_pool_b, v_pool_scale, v_conv_w, v_conv_b, v_gate_a_w, v_gate_a_b, v_gate_x_w, v_gate_x_b, v_lru_L, v_w_out, v_norm_mlp_g, v_w_up, v_w_down, v_norm_ple_g, v_w_ple_gate, v_b_ple_gate, v_w_ple_proj, v_norm_final_g):
    W = dict(norm_mix_g=norm_mix_g, w_in=w_in, pool_w=pool_w, pool_b=pool_b, pool_scale=pool_scale, conv_w=conv_w,
             conv_b=conv_b, gate_a_w=gate_a_w, gate_a_b=gate_a_b, gate_x_w=gate_x_w, gate_x_b=gate_x_b, lru_L=lru_L,
             w_out=w_out, norm_mlp_g=norm_mlp_g, w_up=w_up, w_down=w_down, norm_ple_g=norm_ple_g,
             w_ple_gate=w_ple_gate, b_ple_gate=b_ple_gate, w_ple_proj=w_ple_proj, norm_final_g=norm_final_g)
    M = dict(norm_mix_g=m_norm_mix_g, w_in=m_w_in, pool_w=m_pool_w, pool_b=m_pool_b, pool_scale=m_pool_scale,
             conv_w=m_conv_w, conv_b=m_conv_b, gate_a_w=m_gate_a_w, gate_a_b=m_gate_a_b, gate_x_w=m_gate_x_w,
             gate_x_b=m_gate_x_b, lru_L=m_lru_L, w_out=m_w_out, norm_mlp_g=m_norm_mlp_g, w_up=m_w_up, w_down=m_w_down,
             norm_ple_g=m_norm_ple_g, w_ple_gate=m_w_ple_gate, b_ple_gate=m_b_ple_gate, w_ple_proj=m_w_ple_proj,
             norm_final_g=m_norm_final_g)
    V = dict(norm_mix_g=v_norm_mix_g, w_in=v_w_in, pool_w=v_pool_w, pool_b=v_pool_b, pool_scale=v_pool_scale,
             conv_w=v_conv_w, conv_b=v_conv_b, gate_a_w=v_gate_a_w, gate_a_b=v_gate_a_b, gate_x_w=v_gate_x_w,
             gate_x_b=v_gate_x_b, lru_L=v_lru_L, w_out=v_w_out, norm_mlp_g=v_norm_mlp_g, w_up=v_w_up, w_down=v_w_down,
             norm_ple_g=v_norm_ple_g, w_ple_gate=v_w_ple_gate, b_ple_gate=v_b_ple_gate, w_ple_proj=v_w_ple_proj,
             norm_final_g=v_norm_final_g)

    s_len = x.shape[1]
    tm_mix = min(256, s_len)
    tm = min(512, s_len)
    chip = (2 * lax.axis_index("x") + lax.axis_index("y")).astype(jnp.int32)

    shards = [w_in[0].astype(BF16), w_out[0].astype(BF16), w_up[0].astype(BF16), w_down[0].astype(BF16),
              w_ple_gate[0].astype(BF16), w_ple_proj[0].astype(BF16), jnp.pad(conv_w[0], ((0, 12), (0, 0)))]
    win_g, wout_g, wup_g, wdn_g, wpg_g, wpp_g, cw_g = _gather_weights(shards)
    wout_f = wout_g.reshape(D_MODEL, D_MODEL)
    wpg_f = wpg_g.reshape(D_MODEL, D_MODEL)
    wpp_f = jnp.transpose(wpp_g, (1, 0, 2)).reshape(PLE_DIM, D_MODEL)
    cw_f = jnp.transpose(cw_g[:, :CONV_WIDTH], (1, 0, 2)).reshape(CONV_WIDTH, D_LRU)
    pw_b = pool_w[0].astype(BF16)
    wa_b = _block_diag(gate_a_w[0]).astype(BF16)
    wx_b = _block_diag(gate_x_w[0]).astype(BF16)
    pb_r = pool_b.reshape(1, D_POOL)
    ba_r = gate_a_b.reshape(1, D_LRU)
    bx_r = gate_x_b.reshape(1, D_LRU)
    g4 = norm_final_g.reshape(1, D_MODEL)
    mix_w = (pw_b, pb_r, pool_scale, cw_f, conv_b, wa_b, ba_r, wx_b, bx_r, lru_L, wout_f)

    xs, ps, ts = x[0], p[0, 0], loss_target[0]
    z1, proj, hst, cat, h1 = _fwd_mix(xs, norm_mix_g, win_g, *mix_w, tm_mix)
    z2, ru, h2 = _fwd_mlp(h1, norm_mlp_g, wup_g, wdn_g, tm)
    dh2, d_wpg, d_wpp, head_vec = _head(h2, ps, ts, norm_ple_g, wpg_f, b_ple_gate, wpp_f, g4, tm)
    dup, dh1, mlp_vec = _bwd_mlp_x(dh2, ru, h1, norm_mlp_g, wup_g, wdn_g, tm)
    d_wup, d_wdn = _bwd_mlp_w(z2, dup, ru, dh2, tm)
    dproj, d_wout, d_pw, d_wa, d_wx, mix_vec = _bwd_mix(dh1, proj, hst, cat, *mix_w, tm_mix)
    dx, d_win, in_vec = _bwd_in(dproj, z1, xs, dh1, norm_mix_g, win_g, tm)

    big = [d_win, d_wout.reshape(N_CHIPS, D_MODEL // N_CHIPS, D_MODEL), d_wup, d_wdn,
           d_wpg.reshape(N_CHIPS, D_MODEL // N_CHIPS, D_MODEL), d_wpp]
    own, sib = _swap_halves(big)
    pair = _add_pairs(own, sib, 8)
    got = _scatter_chips(pair)
    halves = _sum_chips(pair, got, chip.reshape(1), 8)
    g_big = _join_halves(halves)

    g_small = {
        "norm_mix_g": in_vec[0:1], "pool_w": d_pw, "pool_b": mix_vec[0:1], "pool_scale": mix_vec[1:2],
        "conv_b": mix_vec[2:3], "gate_a_w": _diag_blocks(d_wa), "gate_a_b": mix_vec[3:4],
        "gate_x_w": _diag_blocks(d_wx), "gate_x_b": mix_vec[4:5], "lru_L": mix_vec[5:6], "norm_mlp_g": mlp_vec[0:1],
        "norm_ple_g": head_vec[0:1], "b_ple_gate": head_vec[2:3], "norm_final_g": head_vec[1:2],
    }
    d_cw = jnp.transpose(mix_vec[8:8 + CONV_WIDTH].reshape(CONV_WIDTH, N_CHIPS, LANES), (1, 0, 2)).reshape(-1, LANES)
    pieces = [_pad8(_rows128(g_small[k])) for k in SMALL] + [d_cw, _pad8(head_vec[3:4, :LANES])]
    offs = [0]
    for pc in pieces:
        offs.append(offs[-1] + pc.shape[0])
    red = _allreduce_small(jnp.concatenate(pieces, axis=0))
    loss = red[offs[-2], 0]
    g_cw = lax.dynamic_slice(red, (offs[len(SMALL)] + CONV_WIDTH * chip, 0), (CONV_WIDTH, LANES))

    def packed(src):
        return jnp.concatenate([_pad8(_rows128(src[k])) for k in SMALL] + [_pad8(src["conv_w"][0])], axis=0)

    n_small = offs[len(SMALL)]
    g_pack = jnp.concatenate([red[:n_small], _pad8(g_cw)], axis=0)
    (d_pack,), (m_pack,), (v_pack,) = _adamw([packed(W)], [g_pack], [packed(M)], [packed(V)], 1, "adamw_small")

    big2d = lambda src: [src[k][0] for k in BIG]
    d_big, m_big, v_big = _adamw(big2d(W), g_big, big2d(M), big2d(V), 8, "adamw_big")

    def unpack(pack, big_list):
        out = {}
        for idx, k in enumerate(SMALL):
            n_el = W[k].size
            out[k] = pack[offs[idx]:offs[idx + 1]].reshape(-1)[:n_el].reshape(W[k].shape)
        out["conv_w"] = pack[n_small:n_small + CONV_WIDTH].reshape(W["conv_w"].shape)
        for k, a in zip(BIG, big_list):
            out[k] = a.reshape(W[k].shape)
        return out

    grads = unpack(g_pack, g_big)
    deltas = unpack(d_pack, d_big)
    new_m = unpack(m_pack, m_big)
    new_v = unpack(v_pack, v_big)
    return (loss, dx[None], *[grads[k] for k in ORDER], *[deltas[k] for k in ORDER],
            *[new_m[k] for k in ORDER], *[new_v[k] for k in ORDER])
```

```python
import functools

import jax
import jax.numpy as jnp
from jax import lax
from jax.experimental import pallas as pl
from jax.experimental.pallas import tpu as pltpu

F32 = jnp.float32
BF16 = jnp.bfloat16
MESH = pl.DeviceIdType.MESH

D_MODEL = 1024
D_POOL = 512
D_LRU = 512
POOL_WINDOWS = (2, 4, 8, 16)
POOL_GROUP = 128
POOL_HALO = 16
CONV_WIDTH = 4
CONV_HALO = 8
LRU_HEADS = 8
LRU_BLOCK = 64
LRU_C = 8.0
D_FF = 4096
PLE_DIM = 256
D_IN_PROJ = 1536
RMS_EPS = 1e-6
N_CHIPS = 4
FF_BLOCK = D_FF // N_CHIPS

ADAM_LR = 0.001
ADAM_B1 = 0.9
ADAM_B2 = 0.999
ADAM_EPS = 1e-08
ADAM_WD = 0.01
ADAM_STEP = 10
ADAM_C1 = 1.0 / (1.0 - ADAM_B1 ** ADAM_STEP)
ADAM_C2 = 1.0 / (1.0 - ADAM_B2 ** ADAM_STEP)

VMEM_LIMIT = 56 * 1024 * 1024
GELU_C = 0.7978845608028654
GELU_A = 0.044715

NT = (((1,), (1,)), ((), ()))
TN = (((0,), (0,)), ((), ()))


def _dot(a, b):
    return jnp.dot(a, b, preferred_element_type=F32)


def _dot_nt(a, b):
    return lax.dot_general(a, b, NT, preferred_element_type=F32)


def _dot_tn(a, b):
    return lax.dot_general(a, b, TN, preferred_element_type=F32)


def _params(*sem):
    return pltpu.CompilerParams(dimension_semantics=sem, vmem_limit_bytes=VMEM_LIMIT)


def _full(shape):
    nd = len(shape)
    return pl.BlockSpec(shape, lambda *_: (0,) * nd)


def _rstd(x):
    return lax.rsqrt(jnp.mean(x * x, axis=-1, keepdims=True) + RMS_EPS)


def _rms_bwd(x, g, dz):
    xr = x * _rstd(x)
    r = _rstd(x)
    dyg = dz * g
    dx = r * (dyg - xr * jnp.mean(dyg * xr, axis=-1, keepdims=True))
    dg = jnp.sum(dz * xr, axis=0, keepdims=True)
    return dx, dg


def _sigmoid(x):
    return 1.0 / (1.0 + jnp.exp(-x))


def _log_sigmoid(v):
    u = jnp.exp(-jnp.abs(v))
    w = 1.0 + u
    l1p = jnp.where(w == 1.0, u, jnp.log(w) * u / jnp.where(w == 1.0, 1.0, w - 1.0))
    return jnp.minimum(v, 0.0) - l1p


def _gelu(x):
    t = jnp.tanh(GELU_C * (x + GELU_A * x * x * x))
    return 0.5 * x * (1.0 + t), t


def _gelu_grad(x, t):
    return 0.5 * (1.0 + t) + 0.5 * x * (1.0 - t * t) * GELU_C * (1.0 + 3.0 * GELU_A * x * x)


def _rows(shape, t0):
    return lax.broadcasted_iota(jnp.int32, shape, 0) + t0


def _pool_diff(u_pool, prev, t0):
    tm = u_pool.shape[0]
    rows = _rows((tm, POOL_GROUP), t0)
    outs, invs = [], []
    for g, w in enumerate(POOL_WINDOWS):
        sl = slice(POOL_GROUP * g, POOL_GROUP * (g + 1))
        ug = u_pool[:, sl]
        s = jnp.concatenate([prev[:, sl], ug], axis=0)
        k = 1
        while k < w:
            s = s + pltpu.roll(s, k, 0)
            k *= 2
        inv = 1.0 / jnp.minimum(rows + 1, w).astype(F32)
        outs.append(s[POOL_HALO:] * inv - ug)
        invs.append(inv)
    return jnp.concatenate(outs, axis=1), jnp.concatenate(invs, axis=1)


def _pool_diff_bwd(dd, inv, nxt):
    ddc = dd * inv
    outs = []
    for g, w in enumerate(POOL_WINDOWS):
        sl = slice(POOL_GROUP * g, POOL_GROUP * (g + 1))
        s = jnp.concatenate([ddc[:, sl], nxt[:, sl]], axis=0)
        n = s.shape[0]
        k = 1
        while k < w:
            s = s + pltpu.roll(s, n - k, 0)
            k *= 2
        outs.append(s[:n - POOL_HALO] - dd[:, sl])
    return jnp.concatenate(outs, axis=1), ddc


def _conv_taps(u, prev):
    ext = jnp.concatenate([prev, u], axis=0)
    return [pltpu.roll(ext, CONV_WIDTH - 1 - k, 0)[CONV_HALO:] if k < CONV_WIDTH - 1 else u for k in range(CONV_WIDTH)]


def _scan_fwd(a, b):
    tm = a.shape[0]
    rows = _rows(a.shape, 0)
    k = 1
    while k < tm:
        ar = pltpu.roll(a, k, 0)
        br = pltpu.roll(b, k, 0)
        m = rows >= k
        b = jnp.where(m, a * br + b, b)
        a = jnp.where(m, a * ar, a)
        k *= 2
    return a, b


def _scan_rev(a, b):
    tm = a.shape[0]
    rows = _rows(a.shape, 0)
    k = 1
    while k < tm:
        ar = pltpu.roll(a, tm - k, 0)
        br = pltpu.roll(b, tm - k, 0)
        m = rows < tm - k
        b = jnp.where(m, a * br + b, b)
        a = jnp.where(m, a * ar, a)
        k *= 2
    return a, b


def _lru_gates(xb, wa, ba, wx, bx, lsl8, t0):
    xbb = xb.astype(BF16)
    r = _sigmoid(_dot(xbb, wa) + ba)
    ig = _sigmoid(_dot(xbb, wx) + bx)
    a = jnp.exp(r * lsl8)
    first = _rows(xb.shape, t0) == 0
    mult = jnp.where(first, 1.0, jnp.sqrt(1.0 - a * a))
    return r, ig, a, mult, first


def _fwd_mix(x, g1, w_in, pool_w, pool_b, pool_scale, conv_w, conv_b, wa, ba, wx, bx, lru_l, w_out, tm):
    s_len = x.shape[0]
    n = s_len // tm

    def body(x_ref, g1_ref, win_ref, pw_ref, pb_ref, ps_ref, cw_ref, cb_ref, wa_ref, ba_ref, wx_ref, bx_ref, l_ref,
             wout_ref, z1_ref, proj_ref, h_ref, cat_ref, h1_ref, cpool, clru, ch):
        i = pl.program_id(0)

        @pl.when(i == 0)
        def _():
            cpool[...] = jnp.zeros_like(cpool)
            clru[...] = jnp.zeros_like(clru)
            ch[...] = jnp.zeros_like(ch)

        t0 = i * tm
        xv = x_ref[...]
        zb = (xv * _rstd(xv) * g1_ref[...]).astype(BF16)
        z1_ref[...] = zb
        proj = jnp.concatenate([_dot(zb, win_ref[j]) for j in range(N_CHIPS)], axis=1)
        proj_ref[...] = proj
        u_pool = proj[:, :D_POOL]
        u_lru = proj[:, D_POOL:D_POOL + D_LRU]
        u_gate = proj[:, D_POOL + D_LRU:]

        d, _ = _pool_diff(u_pool, cpool[...], t0)
        cpool[...] = u_pool[tm - POOL_HALO:]
        db = d.astype(BF16)
        yp = jnp.concatenate(
            [_dot(db[:, POOL_GROUP * g:POOL_GROUP * (g + 1)], pw_ref[g]) for g in range(len(POOL_WINDOWS))], axis=1)
        y_pool = (yp + pb_ref[...]) * ps_ref[...]

        taps = _conv_taps(u_lru, clru[...])
        clru[...] = u_lru[tm - CONV_HALO:]
        xb = cb_ref[...]
        for k in range(CONV_WIDTH):
            xb = xb + taps[k] * cw_ref[k:k + 1, :]
        lsl8 = LRU_C * _log_sigmoid(l_ref[...])
        _, ig, a, mult, _ = _lru_gates(xb, wa_ref[...], ba_ref[...], wx_ref[...], bx_ref[...], lsl8, t0)
        pa, hb = _scan_fwd(a, mult * (ig * xb))
        h = hb + pa * ch[7:8, :]
        ch[...] = h[tm - 8:]
        h_ref[...] = h
        gl, _ = _gelu(u_gate)
        cat = jnp.concatenate([y_pool, h * gl], axis=1).astype(BF16)
        cat_ref[...] = cat
        h1_ref[...] = xv + _dot(cat, wout_ref[...])

    row = lambda w: pl.BlockSpec((tm, w), lambda i: (i, 0))
    ins = [x, g1, w_in, pool_w, pool_b, pool_scale, conv_w, conv_b, wa, ba, wx, bx, lru_l, w_out]
    return pl.pallas_call(
        body, name="fwd_mix", grid=(n,),
        in_specs=[row(D_MODEL)] + [_full(a.shape) for a in ins[1:]],
        out_specs=[row(D_MODEL), row(D_IN_PROJ), row(D_LRU), row(D_MODEL), row(D_MODEL)],
        out_shape=[jax.ShapeDtypeStruct((s_len, D_MODEL), BF16), jax.ShapeDtypeStruct((s_len, D_IN_PROJ), F32),
                   jax.ShapeDtypeStruct((s_len, D_LRU), F32), jax.ShapeDtypeStruct((s_len, D_MODEL), BF16),
                   jax.ShapeDtypeStruct((s_len, D_MODEL), F32)],
        scratch_shapes=[pltpu.VMEM((POOL_HALO, D_POOL), F32), pltpu.VMEM((CONV_HALO, D_LRU), F32),
                        pltpu.VMEM((8, D_LRU), F32)],
        compiler_params=_params("arbitrary"),
    )(*ins)


def _fwd_mlp(h1, g2, w_up, w_down, tm):
    s_len = h1.shape[0]
    n = s_len // tm

    def body(h1_ref, g2_ref, wup_ref, wdn_ref, z2_ref, ru_ref, h2_ref, acc):
        j = pl.program_id(1)

        @pl.when(j == 0)
        def _():
            hv = h1_ref[...]
            z2_ref[...] = (hv * _rstd(hv) * g2_ref[...]).astype(BF16)
            acc[...] = hv

        ru = jnp.maximum(_dot(z2_ref[...], wup_ref[0]), 0.0)
        ru_ref[...] = ru.astype(BF16)
        acc[...] += _dot((ru * ru).astype(BF16), wdn_ref[0])

        @pl.when(j == N_CHIPS - 1)
        def _():
            h2_ref[...] = acc[...]

    return pl.pallas_call(
        body, name="fwd_mlp", grid=(n, N_CHIPS),
        in_specs=[pl.BlockSpec((tm, D_MODEL), lambda i, j: (i, 0)), _full(g2.shape),
                  pl.BlockSpec((1, D_MODEL, FF_BLOCK), lambda i, j: (j, 0, 0)),
                  pl.BlockSpec((1, FF_BLOCK, D_MODEL), lambda i, j: (j, 0, 0))],
        out_specs=[pl.BlockSpec((tm, D_MODEL), lambda i, j: (i, 0)),
                   pl.BlockSpec((tm, FF_BLOCK), lambda i, j: (i, j)),
                   pl.BlockSpec((tm, D_MODEL), lambda i, j: (i, 0))],
        out_shape=[jax.ShapeDtypeStruct((s_len, D_MODEL), BF16), jax.ShapeDtypeStruct((s_len, D_FF), BF16),
                   jax.ShapeDtypeStruct((s_len, D_MODEL), F32)],
        scratch_shapes=[pltpu.VMEM((tm, D_MODEL), F32)],
        compiler_params=_params("arbitrary", "arbitrary"),
    )(h1, g2, w_up, w_down)


def _head(h2, p, target, g3, w_pg, b_pg, w_pp, g4, tm):
    s_len = h2.shape[0]
    n = s_len // tm

    def body(h2_ref, p_ref, t_ref, g3_ref, wpg_ref, bpg_ref, wpp_ref, g4_ref,
             dh2_ref, dwpg_ref, dwpp_ref, vec_ref, a_pg, a_pp, a_vec):
        i = pl.program_id(0)

        @pl.when(i == 0)
        def _():
            a_pg[...] = jnp.zeros_like(a_pg)
            a_pp[...] = jnp.zeros_like(a_pp)
            a_vec[...] = jnp.zeros_like(a_vec)

        h2v = h2_ref[...]
        g3v = g3_ref[...]
        g4v = g4_ref[...]
        z3 = (h2v * _rstd(h2v) * g3v).astype(BF16)
        gate = _sigmoid(_dot(z3, wpg_ref[...]) + bpg_ref[...])
        pb = p_ref[...].astype(BF16)
        pp = _dot(pb, wpp_ref[...])
        h3 = h2v + gate * pp
        r4 = _rstd(h3)
        diff = h3 * r4 * g4v - t_ref[...]
        loss = 0.5 * jnp.sum(jnp.mean(diff * diff, axis=-1, keepdims=True), axis=0, keepdims=True)
        dy = diff * (1.0 / D_MODEL)
        dh3, dg4 = _rms_bwd(h3, g4v, dy)
        dpp = (dh3 * gate).astype(BF16)
        dpre = dh3 * pp * gate * (1.0 - gate)
        dpreb = dpre.astype(BF16)
        dz3 = _dot_nt(dpreb, wpg_ref[...])
        dx, dg3 = _rms_bwd(h2v, g3v, dz3)
        dh2_ref[...] = dh3 + dx
        a_pg[...] += _dot_tn(z3, dpreb)
        a_pp[...] += _dot_tn(pb, dpp)
        a_vec[0:1, :] += dg3
        a_vec[1:2, :] += dg4
        a_vec[2:3, :] += jnp.sum(dpre, axis=0, keepdims=True)
        a_vec[3:4, :] += jnp.broadcast_to(loss, (1, D_MODEL))

        @pl.when(i == n - 1)
        def _():
            dwpg_ref[...] = a_pg[...].astype(BF16)
            for j in range(N_CHIPS):
                dwpp_ref[j] = a_pp[:, PLE_DIM * j:PLE_DIM * (j + 1)].astype(BF16)
            vec_ref[...] = a_vec[...]

    row = lambda w: pl.BlockSpec((tm, w), lambda i: (i, 0))
    ins = [h2, p, target, g3, w_pg, b_pg, w_pp, g4]
    return pl.pallas_call(
        body, name="head", grid=(n,),
        in_specs=[row(D_MODEL), row(PLE_DIM), row(D_MODEL)] + [_full(a.shape) for a in ins[3:]],
        out_specs=[row(D_MODEL), _full((D_MODEL, D_MODEL)), _full((N_CHIPS, PLE_DIM, PLE_DIM)), _full((8, D_MODEL))],
        out_shape=[jax.ShapeDtypeStruct((s_len, D_MODEL), F32), jax.ShapeDtypeStruct((D_MODEL, D_MODEL), BF16),
                   jax.ShapeDtypeStruct((N_CHIPS, PLE_DIM, PLE_DIM), BF16), jax.ShapeDtypeStruct((8, D_MODEL), F32)],
        scratch_shapes=[pltpu.VMEM((D_MODEL, D_MODEL), F32), pltpu.VMEM((PLE_DIM, D_MODEL), F32),
                        pltpu.VMEM((8, D_MODEL), F32)],
        compiler_params=_params("arbitrary"),
    )(*ins)


def _bwd_mlp_x(dh2, ru, h1, g2, w_up, w_down, tm):
    s_len = dh2.shape[0]
    n = s_len // tm

    def body(dh2_ref, ru_ref, h1_ref, g2_ref, wup_ref, wdn_ref, dup_ref, dh1_ref, dg2_ref, acc, a_g):
        i = pl.program_id(0)
        j = pl.program_id(1)

        @pl.when((i == 0) & (j == 0))
        def _():
            a_g[...] = jnp.zeros_like(a_g)

        @pl.when(j == 0)
        def _():
            acc[...] = jnp.zeros_like(acc)

        dact = _dot_nt(dh2_ref[...].astype(BF16), wdn_ref[0])
        dup = (dact * (2.0 * ru_ref[...].astype(F32))).astype(BF16)
        dup_ref[...] = dup
        acc[...] += _dot_nt(dup, wup_ref[0])

        @pl.when(j == N_CHIPS - 1)
        def _():
            dx, dg = _rms_bwd(h1_ref[...], g2_ref[...], acc[...])
            dh1_ref[...] = dh2_ref[...] + dx
            a_g[0:1, :] += dg

        @pl.when((i == n - 1) & (j == N_CHIPS - 1))
        def _():
            dg2_ref[...] = a_g[...]

    tile = pl.BlockSpec((tm, D_MODEL), lambda i, j: (i, 0))
    ffb = pl.BlockSpec((tm, FF_BLOCK), lambda i, j: (i, j))
    return pl.pallas_call(
        body, name="bwd_mlp_x", grid=(n, N_CHIPS),
        in_specs=[tile, ffb, tile, _full(g2.shape),
                  pl.BlockSpec((1, D_MODEL, FF_BLOCK), lambda i, j: (j, 0, 0)),
                  pl.BlockSpec((1, FF_BLOCK, D_MODEL), lambda i, j: (j, 0, 0))],
        out_specs=[ffb, tile, _full((8, D_MODEL))],
        out_shape=[jax.ShapeDtypeStruct((s_len, D_FF), BF16), jax.ShapeDtypeStruct((s_len, D_MODEL), F32),
                   jax.ShapeDtypeStruct((8, D_MODEL), F32)],
        scratch_shapes=[pltpu.VMEM((tm, D_MODEL), F32), pltpu.VMEM((8, D_MODEL), F32)],
        compiler_params=_params("arbitrary", "arbitrary"),
    )(dh2, ru, h1, g2, w_up, w_down)


def _bwd_mlp_w(z2, dup, ru, dh2, tk):
    s_len = z2.shape[0]
    n = s_len // tk

    def body(z2_ref, dup_ref, ru_ref, dh2_ref, dwup_ref, dwdn_ref, a_up, a_dn):
        t = pl.program_id(1)

        @pl.when(t == 0)
        def _():
            a_up[...] = jnp.zeros_like(a_up)
            a_dn[...] = jnp.zeros_like(a_dn)

        ruv = ru_ref[...]
        a_up[...] += _dot_tn(z2_ref[...], dup_ref[...])
        a_dn[...] += _dot_tn(ruv * ruv, dh2_ref[...].astype(BF16))

        @pl.when(t == n - 1)
        def _():
            dwup_ref[0] = a_up[...].astype(BF16)
            dwdn_ref[0] = a_dn[...].astype(BF16)

    tile = pl.BlockSpec((tk, D_MODEL), lambda j, t: (t, 0))
    ffb = pl.BlockSpec((tk, FF_BLOCK), lambda j, t: (t, j))
    return pl.pallas_call(
        body, name="bwd_mlp_w", grid=(N_CHIPS, n),
        in_specs=[tile, ffb, ffb, tile],
        out_specs=[pl.BlockSpec((1, D_MODEL, FF_BLOCK), lambda j, t: (j, 0, 0)),
                   pl.BlockSpec((1, FF_BLOCK, D_MODEL), lambda j, t: (j, 0, 0))],
        out_shape=[jax.ShapeDtypeStruct((N_CHIPS, D_MODEL, FF_BLOCK), BF16),
                   jax.ShapeDtypeStruct((N_CHIPS, FF_BLOCK, D_MODEL), BF16)],
        scratch_shapes=[pltpu.VMEM((D_MODEL, FF_BLOCK), F32), pltpu.VMEM((FF_BLOCK, D_MODEL), F32)],
        compiler_params=_params("arbitrary", "arbitrary"),
    )(z2, dup, ru, dh2)


MIX_VEC_ROWS = 16


def _bwd_mix(dh1, proj, h, cat, pool_w, pool_b, pool_scale, conv_w, conv_b, wa, ba, wx, bx, lru_l, w_out, tm):
    s_len = dh1.shape[0]
    n = s_len // tm
    ng = len(POOL_WINDOWS)

    def body(dh1_ref, proj_ref, h_ref, cat_ref, projh_ref, hh_ref, pw_ref, pb_ref, ps_ref, cw_ref, cb_ref,
             wa_ref, ba_ref, wx_ref, bx_ref, l_ref, wout_ref,
             dproj_ref, dwout_ref, dpw_ref, dwa_ref, dwx_ref, vec_ref,
             a_out, a_pw, a_wa, a_wx, a_vec, c_g, c_dxb, c_ddc):
        q = pl.program_id(0)
        i = n - 1 - q

        @pl.when(q == 0)
        def _():
            for r in (a_out, a_pw, a_wa, a_wx, a_vec, c_g, c_dxb, c_ddc):
                r[...] = jnp.zeros_like(r)

        t0 = i * tm
        has_prev = (i > 0).astype(F32)
        dh1b = dh1_ref[...].astype(BF16)
        dcat = _dot_nt(dh1b, wout_ref[...])
        a_out[...] += _dot_tn(cat_ref[...], dh1b)
        dy_pool = dcat[:, :D_POOL]
        dy_lru = dcat[:, D_POOL:]

        proj = proj_ref[...]
        u_pool = proj[:, :D_POOL]
        u_lru = proj[:, D_POOL:D_POOL + D_LRU]
        u_gate = proj[:, D_POOL + D_LRU:]
        halo = projh_ref[...] * has_prev

        d, inv = _pool_diff(u_pool, halo[:, :D_POOL], t0)
        db = d.astype(BF16)
        ypre = jnp.concatenate(
            [_dot(db[:, POOL_GROUP * g:POOL_GROUP * (g + 1)], pw_ref[g]) for g in range(ng)], axis=1) + pb_ref[...]
        dyp = dy_pool * ps_ref[...]
        dypb = dyp.astype(BF16)
        dds = []
        for g in range(ng):
            sl = slice(POOL_GROUP * g, POOL_GROUP * (g + 1))
            a_pw[g] += _dot_tn(db[:, sl], dypb[:, sl])
            dds.append(_dot_nt(dypb[:, sl], pw_ref[g]))
        du_pool, ddc = _pool_diff_bwd(jnp.concatenate(dds, axis=1), inv, c_ddc[...])
        c_ddc[...] = ddc[:POOL_HALO]
        a_vec[0:1, :] += jnp.sum(dyp, axis=0, keepdims=True)
        a_vec[1:2, :] += jnp.sum(dy_pool * ypre, axis=0, keepdims=True)

        taps = _conv_taps(u_lru, halo[POOL_HALO - CONV_HALO:, D_POOL:D_POOL + D_LRU])
        xb = cb_ref[...]
        for k in range(CONV_WIDTH):
            xb = xb + taps[k] * cw_ref[k:k + 1, :]
        lv = l_ref[...]
        lsl8 = LRU_C * _log_sigmoid(lv)
        r, ig, a, mult, first = _lru_gates(xb, wa_ref[...], ba_ref[...], wx_ref[...], bx_ref[...], lsl8, t0)
        hv = h_ref[...]
        gl, th = _gelu(u_gate)
        du_gate = dy_lru * hv * _gelu_grad(u_gate, th)
        last = _rows(a.shape, 0) == tm - 1
        a_next = jnp.where(last, 1.0, pltpu.roll(a, tm - 1, 0))
        pa, gb = _scan_rev(a_next, dy_lru * gl)
        gh = gb + pa * c_g[0:1, :]
        c_g[...] = jnp.broadcast_to(a[0:1, :] * gh[0:1, :], c_g.shape)
        h_prev_row = hh_ref[7:8, :] * has_prev
        h_prev = jnp.where(_rows(hv.shape, 0) == 0, h_prev_row, pltpu.roll(hv, 1, 0))
        gix = gh * ig * xb
        dla = gh * h_prev * a - jnp.where(first, 0.0, gix * a * a / mult)
        dpre_r = dla * lsl8 * r * (1.0 - r)
        dpre_i = gh * mult * xb * ig * (1.0 - ig)
        dprb = dpre_r.astype(BF16)
        dpib = dpre_i.astype(BF16)
        xbb = xb.astype(BF16)
        a_wa[...] += _dot_tn(xbb, dprb)
        a_wx[...] += _dot_tn(xbb, dpib)
        dxb = gh * mult * ig + _dot_nt(dprb, wa_ref[...]) + _dot_nt(dpib, wx_ref[...])
        a_vec[2:3, :] += jnp.sum(dxb, axis=0, keepdims=True)
        a_vec[3:4, :] += jnp.sum(dpre_r, axis=0, keepdims=True)
        a_vec[4:5, :] += jnp.sum(dpre_i, axis=0, keepdims=True)
        a_vec[5:6, :] += jnp.sum(dla * r, axis=0, keepdims=True)
        ext = jnp.concatenate([dxb, c_dxb[...]], axis=0)
        c_dxb[...] = dxb[:CONV_HALO]
        ne = tm + CONV_HALO
        du_lru = dxb * cw_ref[CONV_WIDTH - 1:CONV_WIDTH, :]
        for k in range(CONV_WIDTH):
            a_vec[8 + k:9 + k, :] += jnp.sum(dxb * taps[k], axis=0, keepdims=True)
            if k < CONV_WIDTH - 1:
                du_lru = du_lru + pltpu.roll(ext, ne - (CONV_WIDTH - 1 - k), 0)[:tm] * cw_ref[k:k + 1, :]
        dproj_ref[...] = jnp.concatenate([du_pool, du_lru, du_gate], axis=1).astype(BF16)

        @pl.when(q == n - 1)
        def _():
            dwout_ref[...] = a_out[...].astype(BF16)
            dpw_ref[...] = a_pw[...]
            dwa_ref[...] = a_wa[...]
            dwx_ref[...] = a_wx[...]
            vec_ref[...] = a_vec[...]
            vec_ref[5:6, :] = a_vec[5:6, :] * (LRU_C * _sigmoid(-lv))

    rev = lambda w: pl.BlockSpec((tm, w), lambda q: (n - 1 - q, 0))
    halo_p = pl.BlockSpec((POOL_HALO, D_IN_PROJ), lambda q: (jnp.maximum((n - 1 - q) * (tm // POOL_HALO) - 1, 0), 0))
    halo_h = pl.BlockSpec((8, D_LRU), lambda q: (jnp.maximum((n - 1 - q) * (tm // 8) - 1, 0), 0))
    wts = [pool_w, pool_b, pool_scale, conv_w, conv_b, wa, ba, wx, bx, lru_l, w_out]
    return pl.pallas_call(
        body, name="bwd_mix", grid=(n,),
        in_specs=[rev(D_MODEL), rev(D_IN_PROJ), rev(D_LRU), rev(D_MODEL), halo_p, halo_h] + [_full(a.shape) for a in wts],
        out_specs=[rev(D_IN_PROJ), _full((D_MODEL, D_MODEL)), _full((ng, POOL_GROUP, POOL_GROUP)),
                   _full((D_LRU, D_LRU)), _full((D_LRU, D_LRU)), _full((MIX_VEC_ROWS, D_LRU))],
        out_shape=[jax.ShapeDtypeStruct((s_len, D_IN_PROJ), BF16), jax.ShapeDtypeStruct((D_MODEL, D_MODEL), BF16),
                   jax.ShapeDtypeStruct((ng, POOL_GROUP, POOL_GROUP), F32), jax.ShapeDtypeStruct((D_LRU, D_LRU), F32),
                   jax.ShapeDtypeStruct((D_LRU, D_LRU), F32), jax.ShapeDtypeStruct((MIX_VEC_ROWS, D_LRU), F32)],
        scratch_shapes=[pltpu.VMEM((D_MODEL, D_MODEL), F32), pltpu.VMEM((ng, POOL_GROUP, POOL_GROUP), F32),
                        pltpu.VMEM((D_LRU, D_LRU), F32), pltpu.VMEM((D_LRU, D_LRU), F32),
                        pltpu.VMEM((MIX_VEC_ROWS, D_LRU), F32), pltpu.VMEM((8, D_LRU), F32),
                        pltpu.VMEM((CONV_HALO, D_LRU), F32), pltpu.VMEM((POOL_HALO, D_POOL), F32)],
        compiler_params=_params("arbitrary"),
    )(dh1, proj, h, cat, proj, h, *wts)


def _bwd_in(dproj, z1, x, dh1, g1, w_in, tm):
    s_len = x.shape[0]
    n = s_len // tm
    cb = D_IN_PROJ // N_CHIPS

    def body(dp_ref, z1_ref, x_ref, dh1_ref, g1_ref, win_ref, dx_ref, dwin_ref, dg1_ref, a_w, a_g):
        i = pl.program_id(0)

        @pl.when(i == 0)
        def _():
            a_w[...] = jnp.zeros_like(a_w)
            a_g[...] = jnp.zeros_like(a_g)

        dp = dp_ref[...]
        zb = z1_ref[...]
        dz = jnp.zeros((tm, D_MODEL), F32)
        for j in range(N_CHIPS):
            dpj = dp[:, cb * j:cb * (j + 1)]
            dz = dz + _dot_nt(dpj, win_ref[j])
            a_w[j] += _dot_tn(zb, dpj)
        dx, dg = _rms_bwd(x_ref[...], g1_ref[...], dz)
        dx_ref[...] = dh1_ref[...] + dx
        a_g[0:1, :] += dg

        @pl.when(i == n - 1)
        def _():
            dwin_ref[...] = a_w[...].astype(BF16)
            dg1_ref[...] = a_g[...]

    row = lambda w: pl.BlockSpec((tm, w), lambda i: (i, 0))
    return pl.pallas_call(
        body, name="bwd_in", grid=(n,),
        in_specs=[row(D_IN_PROJ), row(D_MODEL), row(D_MODEL), row(D_MODEL), _full(g1.shape), _full(w_in.shape)],
        out_specs=[row(D_MODEL), _full(w_in.shape), _full((8, D_MODEL))],
        out_shape=[jax.ShapeDtypeStruct((s_len, D_MODEL), F32), jax.ShapeDtypeStruct(w_in.shape, BF16),
                   jax.ShapeDtypeStruct((8, D_MODEL), F32)],
        scratch_shapes=[pltpu.VMEM(w_in.shape, F32), pltpu.VMEM((8, D_MODEL), F32)],
        compiler_params=_params("arbitrary"),
    )(dproj, z1, x, dh1, g1, w_in)


def _place():
    x, y, c = lax.axis_index("x"), lax.axis_index("y"), lax.axis_index("c")
    chips = [(1 - x, y), (x, 1 - y), (1 - x, 1 - y)]
    return x, y, c, chips


def _rcopy(src, dst, ssem, rsem, dev):
    return pltpu.make_async_remote_copy(src_ref=src, dst_ref=dst, send_sem=ssem, recv_sem=rsem,
                                        device_id=dev, device_id_type=MESH)


ANY = pl.BlockSpec(memory_space=pl.ANY)
COPY_CHUNK_BYTES = 128 * 1024
ROW_ALIGN = 16


def _row_chunks(rows, row_bytes):
    per = max(ROW_ALIGN, (COPY_CHUNK_BYTES // row_bytes) // ROW_ALIGN * ROW_ALIGN)
    return [(r0, min(per, rows - r0)) for r0 in range(0, rows, per)]


def _row_bytes(a):
    return a.shape[-1] * jnp.dtype(a.dtype).itemsize


def _gather_weights(shards):
    nw = len(shards)

    def body(*refs):
        ins, outs = refs[:nw], refs[nw:2 * nw]
        ssem, rsem, lsem = refs[2 * nw:]
        x, y, c, chips = _place()
        me = 2 * x + y
        sib = (x, y, 1 - c)
        half = [s.shape[0] // 2 for s in shards]
        local, sends = [], []
        for w in range(nw):
            cp = pltpu.make_async_copy(ins[w], outs[w].at[me], lsem.at[w])
            cp.start()
            local.append(cp)
            for s, (px, py) in enumerate(chips):
                for r0, rs in _row_chunks(half[w], _row_bytes(shards[w])):
                    rows = pl.ds(c * half[w] + r0, rs)
                    _rcopy(ins[w].at[rows], outs[w].at[me, rows], ssem.at[w, s], rsem.at[w, s], (px, py, c)).start()
                blk = outs[w].at[me, pl.ds(c * half[w], half[w])]
                sends.append(_rcopy(blk, blk, ssem.at[w, s], rsem.at[w, s], sib))
        for w in range(nw):
            for s, (px, py) in enumerate(chips):
                blk = outs[w].at[2 * px + py, pl.ds(c * half[w], half[w])]
                _rcopy(blk, blk, ssem.at[w, s], rsem.at[w, s], sib).wait_recv()
                for r0, rs in _row_chunks(half[w], _row_bytes(shards[w])):
                    piece = outs[w].at[2 * px + py, pl.ds(c * half[w] + r0, rs)]
                    _rcopy(piece, piece, ssem.at[w, 3 + s], rsem.at[w, 3 + s], sib).start()
                sends.append(_rcopy(blk, blk, ssem.at[w, 3 + s], rsem.at[w, 3 + s], sib))
        for w in range(nw):
            for s, (px, py) in enumerate(chips):
                blk = outs[w].at[2 * px + py, pl.ds((1 - c) * half[w], half[w])]
                _rcopy(blk, blk, ssem.at[w, 3 + s], rsem.at[w, 3 + s], sib).wait_recv()
        for cp in sends:
            cp.wait_send()
        for cp in local:
            cp.wait()

    return pl.pallas_call(
        body, name="gather_weights",
        in_specs=[ANY] * nw, out_specs=[ANY] * nw,
        out_shape=[jax.ShapeDtypeStruct((N_CHIPS,) + s.shape, s.dtype) for s in shards],
        scratch_shapes=[pltpu.SemaphoreType.DMA((nw, 6)), pltpu.SemaphoreType.DMA((nw, 6)),
                        pltpu.SemaphoreType.DMA((nw,))],
    )(*shards)


def _swap_halves(grads):
    nw = len(grads)

    def body(*refs):
        ins, own, got = refs[:nw], refs[nw:2 * nw], refs[2 * nw:3 * nw]
        ssem, rsem, lsem = refs[3 * nw:]
        x, y, c, _ = _place()
        cps = []
        for w in range(nw):
            hr = grads[w].shape[1] // 2
            lc = pltpu.make_async_copy(ins[w].at[:, pl.ds(c * hr, hr)], own[w], lsem.at[w])
            lc.start()
            for k in range(N_CHIPS):
                for r0, rs in _row_chunks(hr, _row_bytes(grads[w])):
                    _rcopy(ins[w].at[k, pl.ds((1 - c) * hr + r0, rs)], got[w].at[k, pl.ds(r0, rs)],
                           ssem.at[w], rsem.at[w], (x, y, 1 - c)).start()
            cps += [lc, _rcopy(got[w], got[w], ssem.at[w], rsem.at[w], (x, y, 1 - c))]
        for cp in cps:
            cp.wait()

    halves = [jax.ShapeDtypeStruct((g.shape[0], g.shape[1] // 2, g.shape[2]), g.dtype) for g in grads]
    outs = pl.pallas_call(
        body, name="swap_halves",
        in_specs=[ANY] * nw, out_specs=[ANY] * (2 * nw), out_shape=halves + halves,
        scratch_shapes=[pltpu.SemaphoreType.DMA((nw,)), pltpu.SemaphoreType.DMA((nw,)), pltpu.SemaphoreType.DMA((nw,))],
    )(*grads)
    return outs[:nw], outs[nw:]


def _add_pairs(a_list, b_list, steps):
    nw = len(a_list)

    def body(*refs):
        for w in range(nw):
            refs[2 * nw + w][...] = (refs[w][...].astype(F32) + refs[nw + w][...].astype(F32)).astype(BF16)

    specs = [pl.BlockSpec((a.shape[0], a.shape[1] // steps, a.shape[2]), lambda i: (0, i, 0)) for a in a_list]
    return pl.pallas_call(
        body, name="add_pairs", grid=(steps,),
        in_specs=specs + specs, out_specs=specs,
        out_shape=[jax.ShapeDtypeStruct(a.shape, BF16) for a in a_list],
        compiler_params=_params("arbitrary"),
    )(*a_list, *b_list)


def _scatter_chips(parts):
    nw = len(parts)

    def body(*refs):
        ins, got = refs[:nw], refs[nw:2 * nw]
        ssem, rsem = refs[2 * nw:]
        x, y, c, chips = _place()
        cps = []
        for w in range(nw):
            for s, (px, py) in enumerate(chips):
                for r0, rs in _row_chunks(parts[w].shape[1], _row_bytes(parts[w])):
                    _rcopy(ins[w].at[2 * px + py, pl.ds(r0, rs)], got[w].at[s, pl.ds(r0, rs)],
                           ssem.at[w, s], rsem.at[w, s], (px, py, c)).start()
                cps.append(_rcopy(got[w].at[s], got[w].at[s], ssem.at[w, s], rsem.at[w, s], (px, py, c)))
        for cp in cps:
            cp.wait()

    return pl.pallas_call(
        body, name="scatter_chips",
        in_specs=[ANY] * nw, out_specs=[ANY] * nw,
        out_shape=[jax.ShapeDtypeStruct((3,) + p.shape[1:], p.dtype) for p in parts],
        scratch_shapes=[pltpu.SemaphoreType.DMA((nw, 3)), pltpu.SemaphoreType.DMA((nw, 3))],
    )(*parts)


def _sum_chips(parts, got, chip, steps):
    nw = len(parts)

    def body(chip_ref, *refs):
        for w in range(nw):
            acc = refs[w][0].astype(F32)
            for s in range(3):
                acc = acc + refs[nw + w][s].astype(F32)
            refs[2 * nw + w][...] = acc

    own = [pl.BlockSpec((1, p.shape[1] // steps, p.shape[2]), lambda i, ch: (ch[0], i, 0)) for p in parts]
    rec = [pl.BlockSpec((3, p.shape[1] // steps, p.shape[2]), lambda i, ch: (0, i, 0)) for p in parts]
    outs = [pl.BlockSpec((p.shape[1] // steps, p.shape[2]), lambda i, ch: (i, 0)) for p in parts]
    return pl.pallas_call(
        body, name="sum_chips",
        grid_spec=pltpu.PrefetchScalarGridSpec(num_scalar_prefetch=1, grid=(steps,), in_specs=own + rec, out_specs=outs),
        out_shape=[jax.ShapeDtypeStruct(p.shape[1:], F32) for p in parts],
        compiler_params=_params("arbitrary"),
    )(chip, *parts, *got)


def _join_halves(halves):
    nw = len(halves)

    def body(*refs):
        ins, outs = refs[:nw], refs[nw:2 * nw]
        ssem, rsem, lsem = refs[2 * nw:]
        x, y, c, _ = _place()
        cps = []
        for w in range(nw):
            hr = halves[w].shape[0]
            dst = outs[w].at[pl.ds(c * hr, hr)]
            lc = pltpu.make_async_copy(ins[w], dst, lsem.at[w])
            lc.start()
            for r0, rs in _row_chunks(hr, _row_bytes(halves[w])):
                _rcopy(ins[w].at[pl.ds(r0, rs)], outs[w].at[pl.ds(c * hr + r0, rs)],
                       ssem.at[w], rsem.at[w], (x, y, 1 - c)).start()
            cps += [lc, _rcopy(ins[w], dst, ssem.at[w], rsem.at[w], (x, y, 1 - c))]
        for cp in cps:
            cp.wait()

    return pl.pallas_call(
        body, name="join_halves",
        in_specs=[ANY] * nw, out_specs=[ANY] * nw,
        out_shape=[jax.ShapeDtypeStruct((2 * h.shape[0], h.shape[1]), F32) for h in halves],
        scratch_shapes=[pltpu.SemaphoreType.DMA((nw,)), pltpu.SemaphoreType.DMA((nw,)), pltpu.SemaphoreType.DMA((nw,))],
    )(*halves)


def _allreduce_small(packed):
    shape = packed.shape

    def body(p_ref, out_ref, rbuf, ssem, rsem):
        x, y, c, _ = _place()
        out_ref[...] = p_ref[...]
        for st, peer in enumerate([(x, y, 1 - c), (1 - x, y, c), (x, 1 - y, c)]):
            cp = _rcopy(out_ref, rbuf.at[st], ssem.at[st], rsem.at[st], peer)
            cp.start()
            cp.wait()
            out_ref[...] = out_ref[...] + rbuf[st]

    vm = pl.BlockSpec(memory_space=pltpu.VMEM)
    return pl.pallas_call(
        body, name="allreduce_small",
        in_specs=[vm], out_specs=vm, out_shape=jax.ShapeDtypeStruct(shape, F32),
        scratch_shapes=[pltpu.VMEM((3,) + shape, F32), pltpu.SemaphoreType.DMA((3,)), pltpu.SemaphoreType.DMA((3,))],
        compiler_params=pltpu.CompilerParams(vmem_limit_bytes=VMEM_LIMIT),
    )(packed)


def _adamw_math(w, g, m, v):
    m = ADAM_B1 * m + (1.0 - ADAM_B1) * g
    v = ADAM_B2 * v + (1.0 - ADAM_B2) * (g * g)
    delta = -ADAM_LR * ((m * ADAM_C1) / (jnp.sqrt(v * ADAM_C2) + ADAM_EPS) + ADAM_WD * w)
    return delta, m, v


def _adamw(ws, gs, ms, vs, steps, name):
    nw = len(ws)

    def body(*refs):
        for k in range(nw):
            d, m, v = _adamw_math(refs[k][...], refs[nw + k][...], refs[2 * nw + k][...], refs[3 * nw + k][...])
            refs[4 * nw + k][...] = d
            refs[5 * nw + k][...] = m
            refs[6 * nw + k][...] = v

    specs = [pl.BlockSpec((a.shape[0] // steps, a.shape[1]), lambda i: (i, 0)) for a in ws]
    shapes = [jax.ShapeDtypeStruct(a.shape, F32) for a in ws]
    outs = pl.pallas_call(
        body, name=name, grid=(steps,),
        in_specs=specs * 4, out_specs=specs * 3, out_shape=shapes * 3,
        compiler_params=_params("arbitrary"),
    )(*ws, *gs, *ms, *vs)
    return outs[:nw], outs[nw:2 * nw], outs[2 * nw:]


SMALL = ["norm_mix_g", "pool_w", "pool_b", "pool_scale", "conv_b", "gate_a_w", "gate_a_b", "gate_x_w", "gate_x_b",
         "lru_L", "norm_mlp_g", "norm_ple_g", "b_ple_gate", "norm_final_g"]
BIG = ["w_in", "w_out", "w_up", "w_down", "w_ple_gate", "w_ple_proj"]
ORDER = ["norm_mix_g", "w_in", "pool_w", "pool_b", "pool_scale", "conv_w", "conv_b", "gate_a_w", "gate_a_b", "gate_x_w",
         "gate_x_b", "lru_L", "w_out", "norm_mlp_g", "w_up", "w_down", "norm_ple_g", "w_ple_gate", "b_ple_gate",
         "w_ple_proj", "norm_final_g"]
LANES = 128


def _block_diag(w):
    eye = jnp.eye(LRU_HEADS, dtype=w.dtype)
    return jnp.einsum("hij,hk->hikj", w, eye).reshape(D_LRU, D_LRU)


def _diag_blocks(full):
    f = full.reshape(LRU_HEADS, LRU_BLOCK, LRU_HEADS, LRU_BLOCK)
    return jnp.stack([f[h, :, h, :] for h in range(LRU_HEADS)])


def _rows128(a):
    return a.reshape(-1, LANES)


def _pad8(a):
    r = (-a.shape[0]) % 8
    return jnp.pad(a, ((0, r), (0, 0))) if r else a


def kernel(x, p, norm_mix_g, w_in, pool_w, pool_b, pool_scale, conv_w, conv_b, gate_a_w, gate_a_b, gate_x_w, gate_x_b, lru_L, w_out, norm_mlp_g, w_up, w_down, norm_ple_g, w_ple_gate, b_ple_gate, w_ple_proj, norm_final_g, loss_target, m_norm_mix_g, m_w_in, m_pool_w, m_pool_b, m_pool_scale, m_conv_w, m_conv_b, m_gate_a_w, m_gate_a_b, m_gate_x_w, m_gate_x_b, m_lru_L, m_w_out, m_norm_mlp_g, m_w_up, m_w_down, m_norm_ple_g, m_w_ple_gate, m_b_ple_gate, m_w_ple_proj, m_norm_final_g, v_norm_mix_g, v_w_in, v_pool_w, v_pool_b, v_pool_scale, v_conv_w, v_conv_b, v_gate_a_w, v_gate_a_b, v_gate_x_w, v_gate_x_b, v_lru_L, v_w_out, v_norm_mlp_g, v_w_up, v_w_down, v_norm_ple_g, v_w_ple_gate, v_b_ple_gate, v_w_ple_proj, v_norm_final_g):
    W = dict(norm_mix_g=norm_mix_g, w_in=w_in, pool_w=pool_w, pool_b=pool_b, pool_scale=pool_scale, conv_w=conv_w,
             conv_b=conv_b, gate_a_w=gate_a_w, gate_a_b=gate_a_b, gate_x_w=gate_x_w, gate_x_b=gate_x_b, lru_L=lru_L,
             w_out=w_out, norm_mlp_g=norm_mlp_g, w_up=w_up, w_down=w_down, norm_ple_g=norm_ple_g,
             w_ple_gate=w_ple_gate, b_ple_gate=b_ple_gate, w_ple_proj=w_ple_proj, norm_final_g=norm_final_g)
    M = dict(norm_mix_g=m_norm_mix_g, w_in=m_w_in, pool_w=m_pool_w, pool_b=m_pool_b, pool_scale=m_pool_scale,
             conv_w=m_conv_w, conv_b=m_conv_b, gate_a_w=m_gate_a_w, gate_a_b=m_gate_a_b, gate_x_w=m_gate_x_w,
             gate_x_b=m_gate_x_b, lru_L=m_lru_L, w_out=m_w_out, norm_mlp_g=m_norm_mlp_g, w_up=m_w_up, w_down=m_w_down,
             norm_ple_g=m_norm_ple_g, w_ple_gate=m_w_ple_gate, b_ple_gate=m_b_ple_gate, w_ple_proj=m_w_ple_proj,
             norm_final_g=m_norm_final_g)
    V = dict(norm_mix_g=v_norm_mix_g, w_in=v_w_in, pool_w=v_pool_w, pool_b=v_pool_b, pool_scale=v_pool_scale,
             conv_w=v_conv_w, conv_b=v_conv_b, gate_a_w=v_gate_a_w, gate_a_b=v_gate_a_b, gate_x_w=v_gate_x_w,
             gate_x_b=v_gate_x_b, lru_L=v_lru_L, w_out=v_w_out, norm_mlp_g=v_norm_mlp_g, w_up=v_w_up, w_down=v_w_down,
             norm_ple_g=v_norm_ple_g, w_ple_gate=v_w_ple_gate, b_ple_gate=v_b_ple_gate, w_ple_proj=v_w_ple_proj,
             norm_final_g=v_norm_final_g)

    s_len = x.shape[1]
    tm_mix = min(256, s_len)
    tm = min(512, s_len)
    chip = (2 * lax.axis_index("x") + lax.axis_index("y")).astype(jnp.int32)

    shards = [w_in[0].astype(BF16), w_out[0].astype(BF16), w_up[0].astype(BF16), w_down[0].astype(BF16),
              w_ple_gate[0].astype(BF16), w_ple_proj[0].astype(BF16), jnp.pad(conv_w[0], ((0, 12), (0, 0)))]
    win_g, wout_g, wup_g, wdn_g, wpg_g, wpp_g, cw_g = _gather_weights(shards)
    wout_f = wout_g.reshape(D_MODEL, D_MODEL)
    wpg_f = wpg_g.reshape(D_MODEL, D_MODEL)
    wpp_f = jnp.transpose(wpp_g, (1, 0, 2)).reshape(PLE_DIM, D_MODEL)
    cw_f = jnp.transpose(cw_g[:, :CONV_WIDTH], (1, 0, 2)).reshape(CONV_WIDTH, D_LRU)
    pw_b = pool_w[0].astype(BF16)
    wa_b = _block_diag(gate_a_w[0]).astype(BF16)
    wx_b = _block_diag(gate_x_w[0]).astype(BF16)
    pb_r = pool_b.reshape(1, D_POOL)
    ba_r = gate_a_b.reshape(1, D_LRU)
    bx_r = gate_x_b.reshape(1, D_LRU)
    g4 = norm_final_g.reshape(1, D_MODEL)
    mix_w = (pw_b, pb_r, pool_scale, cw_f, conv_b, wa_b, ba_r, wx_b, bx_r, lru_L, wout_f)

    xs, ps, ts = x[0], p[0, 0], loss_target[0]
    z1, proj, hst, cat, h1 = _fwd_mix(xs, norm_mix_g, win_g, *mix_w, tm_mix)
    z2, ru, h2 = _fwd_mlp(h1, norm_mlp_g, wup_g, wdn_g, tm)
    dh2, d_wpg, d_wpp, head_vec = _head(h2, ps, ts, norm_ple_g, wpg_f, b_ple_gate, wpp_f, g4, tm)
    dup, dh1, mlp_vec = _bwd_mlp_x(dh2, ru, h1, norm_mlp_g, wup_g, wdn_g, tm)
    d_wup, d_wdn = _bwd_mlp_w(z2, dup, ru, dh2, tm)
    dproj, d_wout, d_pw, d_wa, d_wx, mix_vec = _bwd_mix(dh1, proj, hst, cat, *mix_w, tm_mix)
    dx, d_win, in_vec = _bwd_in(dproj, z1, xs, dh1, norm_mix_g, win_g, tm)

    big = [d_win, d_wout.reshape(N_CHIPS, D_MODEL // N_CHIPS, D_MODEL), d_wup, d_wdn,
           d_wpg.reshape(N_CHIPS, D_MODEL // N_CHIPS, D_MODEL), d_wpp]
    own, sib = _swap_halves(big)
    pair = _add_pairs(own, sib, 8)
    got = _scatter_chips(pair)
    halves = _sum_chips(pair, got, chip.reshape(1), 8)
    g_big = _join_halves(halves)

    g_small = {
        "norm_mix_g": in_vec[0:1], "pool_w": d_pw, "pool_b": mix_vec[0:1], "pool_scale": mix_vec[1:2],
        "conv_b": mix_vec[2:3], "gate_a_w": _diag_blocks(d_wa), "gate_a_b": mix_vec[3:4],
        "gate_x_w": _diag_blocks(d_wx), "gate_x_b": mix_vec[4:5], "lru_L": mix_vec[5:6], "norm_mlp_g": mlp_vec[0:1],
        "norm_ple_g": head_vec[0:1], "b_ple_gate": head_vec[2:3], "norm_final_g": head_vec[1:2],
    }
    d_cw = jnp.transpose(mix_vec[8:8 + CONV_WIDTH].reshape(CONV_WIDTH, N_CHIPS, LANES), (1, 0, 2)).reshape(-1, LANES)
    pieces = [_pad8(_rows128(g_small[k])) for k in SMALL] + [d_cw, _pad8(head_vec[3:4, :LANES])]
    offs = [0]
    for pc in pieces:
        offs.append(offs[-1] + pc.shape[0])
    red = _allreduce_small(jnp.concatenate(pieces, axis=0))
    loss = red[offs[-2], 0]
    g_cw = lax.dynamic_slice(red, (offs[len(SMALL)] + CONV_WIDTH * chip, 0), (CONV_WIDTH, LANES))

    def packed(src):
        return jnp.concatenate([_pad8(_rows128(src[k])) for k in SMALL] + [_pad8(src["conv_w"][0])], axis=0)

    n_small = offs[len(SMALL)]
    g_pack = jnp.concatenate([red[:n_small], _pad8(g_cw)], axis=0)
    (d_pack,), (m_pack,), (v_pack,) = _adamw([packed(W)], [g_pack], [packed(M)], [packed(V)], 1, "adamw_small")

    big2d = lambda src: [src[k][0] for k in BIG]
    d_big, m_big, v_big = _adamw(big2d(W), g_big, big2d(M), big2d(V), 8, "adamw_big")

    def unpack(pack, big_list):
        out = {}
        for idx, k in enumerate(SMALL):
            n_el = W[k].size
            out[k] = pack[offs[idx]:offs[idx + 1]].reshape(-1)[:n_el].reshape(W[k].shape)
        out["conv_w"] = pack[n_small:n_small + CONV_WIDTH].reshape(W["conv_w"].shape)
        for k, a in zip(BIG, big_list):
            out[k] = a.reshape(W[k].shape)
        return out

    grads = unpack(g_pack, g_big)
    deltas = unpack(d_pack, d_big)
    new_m = unpack(m_pack, m_big)
    new_v = unpack(v_pack, v_big)
    return (loss, dx[None], *[grads[k] for k in ORDER], *[deltas[k] for k in ORDER],
            *[new_m[k] for k in ORDER], *[new_v[k] for k in ORDER])
```

```python
import functools

import jax
import jax.numpy as jnp
from jax import lax
from jax.experimental import pallas as pl
from jax.experimental.pallas import tpu as pltpu

F32 = jnp.float32
BF16 = jnp.bfloat16
MESH = pl.DeviceIdType.MESH

D_MODEL = 1024
D_POOL = 512
D_LRU = 512
POOL_WINDOWS = (2, 4, 8, 16)
POOL_GROUP = 128
POOL_HALO = 16
CONV_WIDTH = 4
CONV_HALO = 8
LRU_HEADS = 8
LRU_BLOCK = 64
LRU_C = 8.0
D_FF = 4096
PLE_DIM = 256
D_IN_PROJ = 1536
RMS_EPS = 1e-6
N_CHIPS = 4
FF_BLOCK = D_FF // N_CHIPS

ADAM_LR = 0.001
ADAM_B1 = 0.9
ADAM_B2 = 0.999
ADAM_EPS = 1e-08
ADAM_WD = 0.01
ADAM_STEP = 10
ADAM_C1 = 1.0 / (1.0 - ADAM_B1 ** ADAM_STEP)
ADAM_C2 = 1.0 / (1.0 - ADAM_B2 ** ADAM_STEP)

VMEM_LIMIT = 56 * 1024 * 1024
GELU_C = 0.7978845608028654
GELU_A = 0.044715

NT = (((1,), (1,)), ((), ()))
TN = (((0,), (0,)), ((), ()))


def _dot(a, b):
    return jnp.dot(a, b, preferred_element_type=F32)


def _dot_nt(a, b):
    return lax.dot_general(a, b, NT, preferred_element_type=F32)


def _dot_tn(a, b):
    return lax.dot_general(a, b, TN, preferred_element_type=F32)


def _params(*sem):
    return pltpu.CompilerParams(dimension_semantics=sem, vmem_limit_bytes=VMEM_LIMIT)


def _full(shape):
    nd = len(shape)
    return pl.BlockSpec(shape, lambda *_: (0,) * nd)


def _rstd(x):
    return lax.rsqrt(jnp.mean(x * x, axis=-1, keepdims=True) + RMS_EPS)


def _rms_bwd(x, g, dz):
    xr = x * _rstd(x)
    r = _rstd(x)
    dyg = dz * g
    dx = r * (dyg - xr * jnp.mean(dyg * xr, axis=-1, keepdims=True))
    dg = jnp.sum(dz * xr, axis=0, keepdims=True)
    return dx, dg


def _sigmoid(x):
    return 1.0 / (1.0 + jnp.exp(-x))


def _log_sigmoid(v):
    u = jnp.exp(-jnp.abs(v))
    w = 1.0 + u
    l1p = jnp.where(w == 1.0, u, jnp.log(w) * u / jnp.where(w == 1.0, 1.0, w - 1.0))
    return jnp.minimum(v, 0.0) - l1p


def _gelu(x):
    t = jnp.tanh(GELU_C * (x + GELU_A * x * x * x))
    return 0.5 * x * (1.0 + t), t


def _gelu_grad(x, t):
    return 0.5 * (1.0 + t) + 0.5 * x * (1.0 - t * t) * GELU_C * (1.0 + 3.0 * GELU_A * x * x)


def _rows(shape, t0):
    return lax.broadcasted_iota(jnp.int32, shape, 0) + t0


def _pool_diff(u_pool, prev, t0):
    tm = u_pool.shape[0]
    rows = _rows((tm, POOL_GROUP), t0)
    outs, invs = [], []
    for g, w in enumerate(POOL_WINDOWS):
        sl = slice(POOL_GROUP * g, POOL_GROUP * (g + 1))
        ug = u_pool[:, sl]
        s = jnp.concatenate([prev[:, sl], ug], axis=0)
        k = 1
        while k < w:
            s = s + pltpu.roll(s, k, 0)
            k *= 2
        inv = 1.0 / jnp.minimum(rows + 1, w).astype(F32)
        outs.append(s[POOL_HALO:] * inv - ug)
        invs.append(inv)
    return jnp.concatenate(outs, axis=1), jnp.concatenate(invs, axis=1)


def _pool_diff_bwd(dd, inv, nxt):
    ddc = dd * inv
    outs = []
    for g, w in enumerate(POOL_WINDOWS):
        sl = slice(POOL_GROUP * g, POOL_GROUP * (g + 1))
        s = jnp.concatenate([ddc[:, sl], nxt[:, sl]], axis=0)
        n = s.shape[0]
        k = 1
        while k < w:
            s = s + pltpu.roll(s, n - k, 0)
            k *= 2
        outs.append(s[:n - POOL_HALO] - dd[:, sl])
    return jnp.concatenate(outs, axis=1), ddc


def _conv_taps(u, prev):
    ext = jnp.concatenate([prev, u], axis=0)
    return [pltpu.roll(ext, CONV_WIDTH - 1 - k, 0)[CONV_HALO:] if k < CONV_WIDTH - 1 else u for k in range(CONV_WIDTH)]


def _scan_fwd(a, b):
    tm = a.shape[0]
    rows = _rows(a.shape, 0)
    k = 1
    while k < tm:
        ar = pltpu.roll(a, k, 0)
        br = pltpu.roll(b, k, 0)
        m = rows >= k
        b = jnp.where(m, a * br + b, b)
        a = jnp.where(m, a * ar, a)
        k *= 2
    return a, b


def _scan_rev(a, b):
    tm = a.shape[0]
    rows = _rows(a.shape, 0)
    k = 1
    while k < tm:
        ar = pltpu.roll(a, tm - k, 0)
        br = pltpu.roll(b, tm - k, 0)
        m = rows < tm - k
        b = jnp.where(m, a * br + b, b)
        a = jnp.where(m, a * ar, a)
        k *= 2
    return a, b


def _lru_gates(xb, wa, ba, wx, bx, lsl8, t0):
    xbb = xb.astype(BF16)
    r = _sigmoid(_dot(xbb, wa) + ba)
    ig = _sigmoid(_dot(xbb, wx) + bx)
    a = jnp.exp(r * lsl8)
    first = _rows(xb.shape, t0) == 0
    mult = jnp.where(first, 1.0, jnp.sqrt(1.0 - a * a))
    return r, ig, a, mult, first


def _fwd_mix(x, g1, w_in, pool_w, pool_b, pool_scale, conv_w, conv_b, wa, ba, wx, bx, lru_l, w_out, tm):
    s_len = x.shape[0]
    n = s_len // tm

    def body(x_ref, g1_ref, win_ref, pw_ref, pb_ref, ps_ref, cw_ref, cb_ref, wa_ref, ba_ref, wx_ref, bx_ref, l_ref,
             wout_ref, z1_ref, proj_ref, h_ref, cat_ref, h1_ref, cpool, clru, ch):
        i = pl.program_id(0)

        @pl.when(i == 0)
        def _():
            cpool[...] = jnp.zeros_like(cpool)
            clru[...] = jnp.zeros_like(clru)
            ch[...] = jnp.zeros_like(ch)

        t0 = i * tm
        xv = x_ref[...]
        zb = (xv * _rstd(xv) * g1_ref[...]).astype(BF16)
        z1_ref[...] = zb
        proj = jnp.concatenate([_dot(zb, win_ref[j]) for j in range(N_CHIPS)], axis=1)
        proj_ref[...] = proj
        u_pool = proj[:, :D_POOL]
        u_lru = proj[:, D_POOL:D_POOL + D_LRU]
        u_gate = proj[:, D_POOL + D_LRU:]

        d, _ = _pool_diff(u_pool, cpool[...], t0)
        cpool[...] = u_pool[tm - POOL_HALO:]
        db = d.astype(BF16)
        yp = jnp.concatenate(
            [_dot(db[:, POOL_GROUP * g:POOL_GROUP * (g + 1)], pw_ref[g]) for g in range(len(POOL_WINDOWS))], axis=1)
        y_pool = (yp + pb_ref[...]) * ps_ref[...]

        taps = _conv_taps(u_lru, clru[...])
        clru[...] = u_lru[tm - CONV_HALO:]
        xb = cb_ref[...]
        for k in range(CONV_WIDTH):
            xb = xb + taps[k] * cw_ref[k:k + 1, :]
        lsl8 = LRU_C * _log_sigmoid(l_ref[...])
        _, ig, a, mult, _ = _lru_gates(xb, wa_ref[...], ba_ref[...], wx_ref[...], bx_ref[...], lsl8, t0)
        pa, hb = _scan_fwd(a, mult * (ig * xb))
        h = hb + pa * ch[7:8, :]
        ch[...] = h[tm - 8:]
        h_ref[...] = h
        gl, _ = _gelu(u_gate)
        cat = jnp.concatenate([y_pool, h * gl], axis=1).astype(BF16)
        cat_ref[...] = cat
        h1_ref[...] = xv + _dot(cat, wout_ref[...])

    row = lambda w: pl.BlockSpec((tm, w), lambda i: (i, 0))
    ins = [x, g1, w_in, pool_w, pool_b, pool_scale, conv_w, conv_b, wa, ba, wx, bx, lru_l, w_out]
    return pl.pallas_call(
        body, name="fwd_mix", grid=(n,),
        in_specs=[row(D_MODEL)] + [_full(a.shape) for a in ins[1:]],
        out_specs=[row(D_MODEL), row(D_IN_PROJ), row(D_LRU), row(D_MODEL), row(D_MODEL)],
        out_shape=[jax.ShapeDtypeStruct((s_len, D_MODEL), BF16), jax.ShapeDtypeStruct((s_len, D_IN_PROJ), F32),
                   jax.ShapeDtypeStruct((s_len, D_LRU), F32), jax.ShapeDtypeStruct((s_len, D_MODEL), BF16),
                   jax.ShapeDtypeStruct((s_len, D_MODEL), F32)],
        scratch_shapes=[pltpu.VMEM((POOL_HALO, D_POOL), F32), pltpu.VMEM((CONV_HALO, D_LRU), F32),
                        pltpu.VMEM((8, D_LRU), F32)],
        compiler_params=_params("arbitrary"),
    )(*ins)


def _fwd_mlp(h1, g2, w_up, w_down, tm):
    s_len = h1.shape[0]
    n = s_len // tm

    def body(h1_ref, g2_ref, wup_ref, wdn_ref, z2_ref, ru_ref, h2_ref, acc):
        j = pl.program_id(1)

        @pl.when(j == 0)
        def _():
            hv = h1_ref[...]
            z2_ref[...] = (hv * _rstd(hv) * g2_ref[...]).astype(BF16)
            acc[...] = hv

        ru = jnp.maximum(_dot(z2_ref[...], wup_ref[0]), 0.0)
        ru_ref[...] = ru.astype(BF16)
        acc[...] += _dot((ru * ru).astype(BF16), wdn_ref[0])

        @pl.when(j == N_CHIPS - 1)
        def _():
            h2_ref[...] = acc[...]

    return pl.pallas_call(
        body, name="fwd_mlp", grid=(n, N_CHIPS),
        in_specs=[pl.BlockSpec((tm, D_MODEL), lambda i, j: (i, 0)), _full(g2.shape),
                  pl.BlockSpec((1, D_MODEL, FF_BLOCK), lambda i, j: (j, 0, 0)),
                  pl.BlockSpec((1, FF_BLOCK, D_MODEL), lambda i, j: (j, 0, 0))],
        out_specs=[pl.BlockSpec((tm, D_MODEL), lambda i, j: (i, 0)),
                   pl.BlockSpec((tm, FF_BLOCK), lambda i, j: (i, j)),
                   pl.BlockSpec((tm, D_MODEL), lambda i, j: (i, 0))],
        out_shape=[jax.ShapeDtypeStruct((s_len, D_MODEL), BF16), jax.ShapeDtypeStruct((s_len, D_FF), BF16),
                   jax.ShapeDtypeStruct((s_len, D_MODEL), F32)],
        scratch_shapes=[pltpu.VMEM((tm, D_MODEL), F32)],
        compiler_params=_params("arbitrary", "arbitrary"),
    )(h1, g2, w_up, w_down)


def _head(h2, p, target, g3, w_pg, b_pg, w_pp, g4, tm):
    s_len = h2.shape[0]
    n = s_len // tm

    def body(h2_ref, p_ref, t_ref, g3_ref, wpg_ref, bpg_ref, wpp_ref, g4_ref,
             dh2_ref, dwpg_ref, dwpp_ref, vec_ref, a_pg, a_pp, a_vec):
        i = pl.program_id(0)

        @pl.when(i == 0)
        def _():
            a_pg[...] = jnp.zeros_like(a_pg)
            a_pp[...] = jnp.zeros_like(a_pp)
            a_vec[...] = jnp.zeros_like(a_vec)

        h2v = h2_ref[...]
        g3v = g3_ref[...]
        g4v = g4_ref[...]
        z3 = (h2v * _rstd(h2v) * g3v).astype(BF16)
        gate = _sigmoid(_dot(z3, wpg_ref[...]) + bpg_ref[...])
        pb = p_ref[...].astype(BF16)
        pp = _dot(pb, wpp_ref[...])
        h3 = h2v + gate * pp
        r4 = _rstd(h3)
        diff = h3 * r4 * g4v - t_ref[...]
        loss = 0.5 * jnp.sum(jnp.mean(diff * diff, axis=-1, keepdims=True), axis=0, keepdims=True)
        dy = diff * (1.0 / D_MODEL)
        dh3, dg4 = _rms_bwd(h3, g4v, dy)
        dpp = (dh3 * gate).astype(BF16)
        dpre = dh3 * pp * gate * (1.0 - gate)
        dpreb = dpre.astype(BF16)
        dz3 = _dot_nt(dpreb, wpg_ref[...])
        dx, dg3 = _rms_bwd(h2v, g3v, dz3)
        dh2_ref[...] = dh3 + dx
        a_pg[...] += _dot_tn(z3, dpreb)
        a_pp[...] += _dot_tn(pb, dpp)
        a_vec[0:1, :] += dg3
        a_vec[1:2, :] += dg4
        a_vec[2:3, :] += jnp.sum(dpre, axis=0, keepdims=True)
        a_vec[3:4, :] += jnp.broadcast_to(loss, (1, D_MODEL))

        @pl.when(i == n - 1)
        def _():
            dwpg_ref[...] = a_pg[...].astype(BF16)
            for j in range(N_CHIPS):
                dwpp_ref[j] = a_pp[:, PLE_DIM * j:PLE_DIM * (j + 1)].astype(BF16)
            vec_ref[...] = a_vec[...]

    row = lambda w: pl.BlockSpec((tm, w), lambda i: (i, 0))
    ins = [h2, p, target, g3, w_pg, b_pg, w_pp, g4]
    return pl.pallas_call(
        body, name="head", grid=(n,),
        in_specs=[row(D_MODEL), row(PLE_DIM), row(D_MODEL)] + [_full(a.shape) for a in ins[3:]],
        out_specs=[row(D_MODEL), _full((D_MODEL, D_MODEL)), _full((N_CHIPS, PLE_DIM, PLE_DIM)), _full((8, D_MODEL))],
        out_shape=[jax.ShapeDtypeStruct((s_len, D_MODEL), F32), jax.ShapeDtypeStruct((D_MODEL, D_MODEL), BF16),
                   jax.ShapeDtypeStruct((N_CHIPS, PLE_DIM, PLE_DIM), BF16), jax.ShapeDtypeStruct((8, D_MODEL), F32)],
        scratch_shapes=[pltpu.VMEM((D_MODEL, D_MODEL), F32), pltpu.VMEM((PLE_DIM, D_MODEL), F32),
                        pltpu.VMEM((8, D_MODEL), F32)],
        compiler_params=_params("arbitrary"),
    )(*ins)


def _bwd_mlp_x(dh2, ru, h1, g2, w_up, w_down, tm):
    s_len = dh2.shape[0]
    n = s_len // tm

    def body(dh2_ref, ru_ref, h1_ref, g2_ref, wup_ref, wdn_ref, dup_ref, dh1_ref, dg2_ref, acc, a_g):
        i = pl.program_id(0)
        j = pl.program_id(1)

        @pl.when((i == 0) & (j == 0))
        def _():
            a_g[...] = jnp.zeros_like(a_g)

        @pl.when(j == 0)
        def _():
            acc[...] = jnp.zeros_like(acc)

        dact = _dot_nt(dh2_ref[...].astype(BF16), wdn_ref[0])
        dup = (dact * (2.0 * ru_ref[...].astype(F32))).astype(BF16)
        dup_ref[...] = dup
        acc[...] += _dot_nt(dup, wup_ref[0])

        @pl.when(j == N_CHIPS - 1)
        def _():
            dx, dg = _rms_bwd(h1_ref[...], g2_ref[...], acc[...])
            dh1_ref[...] = dh2_ref[...] + dx
            a_g[0:1, :] += dg

        @pl.when((i == n - 1) & (j == N_CHIPS - 1))
        def _():
            dg2_ref[...] = a_g[...]

    tile = pl.BlockSpec((tm, D_MODEL), lambda i, j: (i, 0))
    ffb = pl.BlockSpec((tm, FF_BLOCK), lambda i, j: (i, j))
    return pl.pallas_call(
        body, name="bwd_mlp_x", grid=(n, N_CHIPS),
        in_specs=[tile, ffb, tile, _full(g2.shape),
                  pl.BlockSpec((1, D_MODEL, FF_BLOCK), lambda i, j: (j, 0, 0)),
                  pl.BlockSpec((1, FF_BLOCK, D_MODEL), lambda i, j: (j, 0, 0))],
        out_specs=[ffb, tile, _full((8, D_MODEL))],
        out_shape=[jax.ShapeDtypeStruct((s_len, D_FF), BF16), jax.ShapeDtypeStruct((s_len, D_MODEL), F32),
                   jax.ShapeDtypeStruct((8, D_MODEL), F32)],
        scratch_shapes=[pltpu.VMEM((tm, D_MODEL), F32), pltpu.VMEM((8, D_MODEL), F32)],
        compiler_params=_params("arbitrary", "arbitrary"),
    )(dh2, ru, h1, g2, w_up, w_down)


def _bwd_mlp_w(z2, dup, ru, dh2, tk):
    s_len = z2.shape[0]
    n = s_len // tk

    def body(z2_ref, dup_ref, ru_ref, dh2_ref, dwup_ref, dwdn_ref, a_up, a_dn):
        t = pl.program_id(1)

        @pl.when(t == 0)
        def _():
            a_up[...] = jnp.zeros_like(a_up)
            a_dn[...] = jnp.zeros_like(a_dn)

        ruv = ru_ref[...]
        a_up[...] += _dot_tn(z2_ref[...], dup_ref[...])
        a_dn[...] += _dot_tn(ruv * ruv, dh2_ref[...].astype(BF16))

        @pl.when(t == n - 1)
        def _():
            dwup_ref[0] = a_up[...].astype(BF16)
            dwdn_ref[0] = a_dn[...].astype(BF16)

    tile = pl.BlockSpec((tk, D_MODEL), lambda j, t: (t, 0))
    ffb = pl.BlockSpec((tk, FF_BLOCK), lambda j, t: (t, j))
    return pl.pallas_call(
        body, name="bwd_mlp_w", grid=(N_CHIPS, n),
        in_specs=[tile, ffb, ffb, tile],
        out_specs=[pl.BlockSpec((1, D_MODEL, FF_BLOCK), lambda j, t: (j, 0, 0)),
                   pl.BlockSpec((1, FF_BLOCK, D_MODEL), lambda j, t: (j, 0, 0))],
        out_shape=[jax.ShapeDtypeStruct((N_CHIPS, D_MODEL, FF_BLOCK), BF16),
                   jax.ShapeDtypeStruct((N_CHIPS, FF_BLOCK, D_MODEL), BF16)],
        scratch_shapes=[pltpu.VMEM((D_MODEL, FF_BLOCK), F32), pltpu.VMEM((FF_BLOCK, D_MODEL), F32)],
        compiler_params=_params("arbitrary", "arbitrary"),
    )(z2, dup, ru, dh2)


MIX_VEC_ROWS = 16


def _bwd_mix(dh1, proj, h, cat, pool_w, pool_b, pool_scale, conv_w, conv_b, wa, ba, wx, bx, lru_l, w_out, tm):
    s_len = dh1.shape[0]
    n = s_len // tm
    ng = len(POOL_WINDOWS)

    def body(dh1_ref, proj_ref, h_ref, cat_ref, projh_ref, hh_ref, pw_ref, pb_ref, ps_ref, cw_ref, cb_ref,
             wa_ref, ba_ref, wx_ref, bx_ref, l_ref, wout_ref,
             dproj_ref, dwout_ref, dpw_ref, dwa_ref, dwx_ref, vec_ref,
             a_out, a_pw, a_wa, a_wx, a_vec, c_g, c_dxb, c_ddc):
        q = pl.program_id(0)
        i = n - 1 - q

        @pl.when(q == 0)
        def _():
            for r in (a_out, a_pw, a_wa, a_wx, a_vec, c_g, c_dxb, c_ddc):
                r[...] = jnp.zeros_like(r)

        t0 = i * tm
        has_prev = (i > 0).astype(F32)
        dh1b = dh1_ref[...].astype(BF16)
        dcat = _dot_nt(dh1b, wout_ref[...])
        a_out[...] += _dot_tn(cat_ref[...], dh1b)
        dy_pool = dcat[:, :D_POOL]
        dy_lru = dcat[:, D_POOL:]

        proj = proj_ref[...]
        u_pool = proj[:, :D_POOL]
        u_lru = proj[:, D_POOL:D_POOL + D_LRU]
        u_gate = proj[:, D_POOL + D_LRU:]
        halo = projh_ref[...] * has_prev

        d, inv = _pool_diff(u_pool, halo[:, :D_POOL], t0)
        db = d.astype(BF16)
        ypre = jnp.concatenate(
            [_dot(db[:, POOL_GROUP * g:POOL_GROUP * (g + 1)], pw_ref[g]) for g in range(ng)], axis=1) + pb_ref[...]
        dyp = dy_pool * ps_ref[...]
        dypb = dyp.astype(BF16)
        dds = []
        for g in range(ng):
            sl = slice(POOL_GROUP * g, POOL_GROUP * (g + 1))
            a_pw[g] += _dot_tn(db[:, sl], dypb[:, sl])
            dds.append(_dot_nt(dypb[:, sl], pw_ref[g]))
        du_pool, ddc = _pool_diff_bwd(jnp.concatenate(dds, axis=1), inv, c_ddc[...])
        c_ddc[...] = ddc[:POOL_HALO]
        a_vec[0:1, :] += jnp.sum(dyp, axis=0, keepdims=True)
        a_vec[1:2, :] += jnp.sum(dy_pool * ypre, axis=0, keepdims=True)

        taps = _conv_taps(u_lru, halo[POOL_HALO - CONV_HALO:, D_POOL:D_POOL + D_LRU])
        xb = cb_ref[...]
        for k in range(CONV_WIDTH):
            xb = xb + taps[k] * cw_ref[k:k + 1, :]
        lv = l_ref[...]
        lsl8 = LRU_C * _log_sigmoid(lv)
        r, ig, a, mult, first = _lru_gates(xb, wa_ref[...], ba_ref[...], wx_ref[...], bx_ref[...], lsl8, t0)
        hv = h_ref[...]
        gl, th = _gelu(u_gate)
        du_gate = dy_lru * hv * _gelu_grad(u_gate, th)
        last = _rows(a.shape, 0) == tm - 1
        a_next = jnp.where(last, 1.0, pltpu.roll(a, tm - 1, 0))
        pa, gb = _scan_rev(a_next, dy_lru * gl)
        gh = gb + pa * c_g[0:1, :]
        c_g[...] = jnp.broadcast_to(a[0:1, :] * gh[0:1, :], c_g.shape)
        h_prev_row = hh_ref[7:8, :] * has_prev
        h_prev = jnp.where(_rows(hv.shape, 0) == 0, h_prev_row, pltpu.roll(hv, 1, 0))
        gix = gh * ig * xb
        dla = gh * h_prev * a - jnp.where(first, 0.0, gix * a * a / mult)
        dpre_r = dla * lsl8 * r * (1.0 - r)
        dpre_i = gh * mult * xb * ig * (1.0 - ig)
        dprb = dpre_r.astype(BF16)
        dpib = dpre_i.astype(BF16)
        xbb = xb.astype(BF16)
        a_wa[...] += _dot_tn(xbb, dprb)
        a_wx[...] += _dot_tn(xbb, dpib)
        dxb = gh * mult * ig + _dot_nt(dprb, wa_ref[...]) + _dot_nt(dpib, wx_ref[...])
        a_vec[2:3, :] += jnp.sum(dxb, axis=0, keepdims=True)
        a_vec[3:4, :] += jnp.sum(dpre_r, axis=0, keepdims=True)
        a_vec[4:5, :] += jnp.sum(dpre_i, axis=0, keepdims=True)
        a_vec[5:6, :] += jnp.sum(dla * r, axis=0, keepdims=True)
        ext = jnp.concatenate([dxb, c_dxb[...]], axis=0)
        c_dxb[...] = dxb[:CONV_HALO]
        ne = tm + CONV_HALO
        du_lru = dxb * cw_ref[CONV_WIDTH - 1:CONV_WIDTH, :]
        for k in range(CONV_WIDTH):
            a_vec[8 + k:9 + k, :] += jnp.sum(dxb * taps[k], axis=0, keepdims=True)
            if k < CONV_WIDTH - 1:
                du_lru = du_lru + pltpu.roll(ext, ne - (CONV_WIDTH - 1 - k), 0)[:tm] * cw_ref[k:k + 1, :]
        dproj_ref[...] = jnp.concatenate([du_pool, du_lru, du_gate], axis=1).astype(BF16)

        @pl.when(q == n - 1)
        def _():
            dwout_ref[...] = a_out[...].astype(BF16)
            dpw_ref[...] = a_pw[...]
            dwa_ref[...] = a_wa[...]
            dwx_ref[...] = a_wx[...]
            vec_ref[...] = a_vec[...]
            vec_ref[5:6, :] = a_vec[5:6, :] * (LRU_C * _sigmoid(-lv))

    rev = lambda w: pl.BlockSpec((tm, w), lambda q: (n - 1 - q, 0))
    halo_p = pl.BlockSpec((POOL_HALO, D_IN_PROJ), lambda q: (jnp.maximum((n - 1 - q) * (tm // POOL_HALO) - 1, 0), 0))
    halo_h = pl.BlockSpec((8, D_LRU), lambda q: (jnp.maximum((n - 1 - q) * (tm // 8) - 1, 0), 0))
    wts = [pool_w, pool_b, pool_scale, conv_w, conv_b, wa, ba, wx, bx, lru_l, w_out]
    return pl.pallas_call(
        body, name="bwd_mix", grid=(n,),
        in_specs=[rev(D_MODEL), rev(D_IN_PROJ), rev(D_LRU), rev(D_MODEL), halo_p, halo_h] + [_full(a.shape) for a in wts],
        out_specs=[rev(D_IN_PROJ), _full((D_MODEL, D_MODEL)), _full((ng, POOL_GROUP, POOL_GROUP)),
                   _full((D_LRU, D_LRU)), _full((D_LRU, D_LRU)), _full((MIX_VEC_ROWS, D_LRU))],
        out_shape=[jax.ShapeDtypeStruct((s_len, D_IN_PROJ), BF16), jax.ShapeDtypeStruct((D_MODEL, D_MODEL), BF16),
                   jax.ShapeDtypeStruct((ng, POOL_GROUP, POOL_GROUP), F32), jax.ShapeDtypeStruct((D_LRU, D_LRU), F32),
                   jax.ShapeDtypeStruct((D_LRU, D_LRU), F32), jax.ShapeDtypeStruct((MIX_VEC_ROWS, D_LRU), F32)],
        scratch_shapes=[pltpu.VMEM((D_MODEL, D_MODEL), F32), pltpu.VMEM((ng, POOL_GROUP, POOL_GROUP), F32),
                        pltpu.VMEM((D_LRU, D_LRU), F32), pltpu.VMEM((D_LRU, D_LRU), F32),
                        pltpu.VMEM((MIX_VEC_ROWS, D_LRU), F32), pltpu.VMEM((8, D_LRU), F32),
                        pltpu.VMEM((CONV_HALO, D_LRU), F32), pltpu.VMEM((POOL_HALO, D_POOL), F32)],
        compiler_params=_params("arbitrary"),
    )(dh1, proj, h, cat, proj, h, *wts)


def _bwd_in(dproj, z1, x, dh1, g1, w_in, tm):
    s_len = x.shape[0]
    n = s_len // tm
    cb = D_IN_PROJ // N_CHIPS

    def body(dp_ref, z1_ref, x_ref, dh1_ref, g1_ref, win_ref, dx_ref, dwin_ref, dg1_ref, a_w, a_g):
        i = pl.program_id(0)

        @pl.when(i == 0)
        def _():
            a_w[...] = jnp.zeros_like(a_w)
            a_g[...] = jnp.zeros_like(a_g)

        dp = dp_ref[...]
        zb = z1_ref[...]
        dz = jnp.zeros((tm, D_MODEL), F32)
        for j in range(N_CHIPS):
            dpj = dp[:, cb * j:cb * (j + 1)]
            dz = dz + _dot_nt(dpj, win_ref[j])
            a_w[j] += _dot_tn(zb, dpj)
        dx, dg = _rms_bwd(x_ref[...], g1_ref[...], dz)
        dx_ref[...] = dh1_ref[...] + dx
        a_g[0:1, :] += dg

        @pl.when(i == n - 1)
        def _():
            dwin_ref[...] = a_w[...].astype(BF16)
            dg1_ref[...] = a_g[...]

    row = lambda w: pl.BlockSpec((tm, w), lambda i: (i, 0))
    return pl.pallas_call(
        body, name="bwd_in", grid=(n,),
        in_specs=[row(D_IN_PROJ), row(D_MODEL), row(D_MODEL), row(D_MODEL), _full(g1.shape), _full(w_in.shape)],
        out_specs=[row(D_MODEL), _full(w_in.shape), _full((8, D_MODEL))],
        out_shape=[jax.ShapeDtypeStruct((s_len, D_MODEL), F32), jax.ShapeDtypeStruct(w_in.shape, BF16),
                   jax.ShapeDtypeStruct((8, D_MODEL), F32)],
        scratch_shapes=[pltpu.VMEM(w_in.shape, F32), pltpu.VMEM((8, D_MODEL), F32)],
        compiler_params=_params("arbitrary"),
    )(dproj, z1, x, dh1, g1, w_in)


def _place():
    x, y, c = lax.axis_index("x"), lax.axis_index("y"), lax.axis_index("c")
    chips = [(1 - x, y), (x, 1 - y), (1 - x, 1 - y)]
    return x, y, c, chips


def _rcopy(src, dst, ssem, rsem, dev):
    return pltpu.make_async_remote_copy(src_ref=src, dst_ref=dst, send_sem=ssem, recv_sem=rsem,
                                        device_id=dev, device_id_type=MESH)


ANY = pl.BlockSpec(memory_space=pl.ANY)
COPY_CHUNK_BYTES = 128 * 1024
ROW_ALIGN = 16


def _row_chunks(rows, row_bytes):
    per = max(ROW_ALIGN, (COPY_CHUNK_BYTES // row_bytes) // ROW_ALIGN * ROW_ALIGN)
    return [(r0, min(per, rows - r0)) for r0 in range(0, rows, per)]


def _row_bytes(a):
    return a.shape[-1] * jnp.dtype(a.dtype).itemsize


def _stack_own(shards, dtypes, pos, steps):
    nw = len(shards)

    def body(pos_ref, *refs):
        for w in range(nw):
            refs[nw + w][0] = refs[w][...].astype(dtypes[w])

    def split(s):
        return s.shape[0] % (steps * ROW_ALIGN) == 0

    ins = [pl.BlockSpec((s.shape[0] // steps, s.shape[1]), lambda i, p: (i, 0)) if split(s)
           else pl.BlockSpec(s.shape, lambda i, p: (0, 0)) for s in shards]
    outs = [pl.BlockSpec((1, s.shape[0] // steps, s.shape[1]), lambda i, p: (p[0], i, 0)) if split(s)
            else pl.BlockSpec((1,) + s.shape, lambda i, p: (p[0], 0, 0)) for s in shards]
    return pl.pallas_call(
        body, name="stack_own",
        grid_spec=pltpu.PrefetchScalarGridSpec(num_scalar_prefetch=1, grid=(steps,), in_specs=ins, out_specs=outs),
        out_shape=[jax.ShapeDtypeStruct((N_CHIPS,) + s.shape, d) for s, d in zip(shards, dtypes)],
        compiler_params=_params("arbitrary"),
    )(pos, *shards)


def _gather_weights(stacks):
    nw = len(stacks)

    def body(*refs):
        outs = refs[nw:2 * nw]
        ssem, rsem = refs[2 * nw:]
        x, y, c, chips = _place()
        me = 2 * x + y
        sib = (x, y, 1 - c)
        half = [s.shape[1] // 2 for s in stacks]
        sends = []
        for w in range(nw):
            for s, (px, py) in enumerate(chips):
                for r0, rs in _row_chunks(half[w], _row_bytes(stacks[w])):
                    piece = outs[w].at[me, pl.ds(c * half[w] + r0, rs)]
                    _rcopy(piece, piece, ssem.at[w, s], rsem.at[w, s], (px, py, c)).start()
                blk = outs[w].at[me, pl.ds(c * half[w], half[w])]
                sends.append(_rcopy(blk, blk, ssem.at[w, s], rsem.at[w, s], sib))
        for w in range(nw):
            for s, (px, py) in enumerate(chips):
                blk = outs[w].at[2 * px + py, pl.ds(c * half[w], half[w])]
                _rcopy(blk, blk, ssem.at[w, s], rsem.at[w, s], sib).wait_recv()
                for r0, rs in _row_chunks(half[w], _row_bytes(stacks[w])):
                    piece = outs[w].at[2 * px + py, pl.ds(c * half[w] + r0, rs)]
                    _rcopy(piece, piece, ssem.at[w, 3 + s], rsem.at[w, 3 + s], sib).start()
                sends.append(_rcopy(blk, blk, ssem.at[w, 3 + s], rsem.at[w, 3 + s], sib))
        for w in range(nw):
            for s, (px, py) in enumerate(chips):
                blk = outs[w].at[2 * px + py, pl.ds((1 - c) * half[w], half[w])]
                _rcopy(blk, blk, ssem.at[w, 3 + s], rsem.at[w, 3 + s], sib).wait_recv()
        for cp in sends:
            cp.wait_send()

    return pl.pallas_call(
        body, name="gather_weights",
        in_specs=[ANY] * nw, out_specs=[ANY] * nw,
        out_shape=[jax.ShapeDtypeStruct(s.shape, s.dtype) for s in stacks],
        input_output_aliases={w: w for w in range(nw)},
        scratch_shapes=[pltpu.SemaphoreType.DMA((nw, 6)), pltpu.SemaphoreType.DMA((nw, 6))],
    )(*stacks)


def _swap_halves(grads):
    nw = len(grads)

    def body(*refs):
        ins, got = refs[:nw], refs[nw:2 * nw]
        ssem, rsem = refs[2 * nw:]
        x, y, c, _ = _place()
        cps = []
        for w in range(nw):
            hr = grads[w].shape[1] // 2
            for k in range(N_CHIPS):
                for r0, rs in _row_chunks(hr, _row_bytes(grads[w])):
                    _rcopy(ins[w].at[k, pl.ds((1 - c) * hr + r0, rs)], got[w].at[k, pl.ds(r0, rs)],
                           ssem.at[w], rsem.at[w], (x, y, 1 - c)).start()
            cps.append(_rcopy(got[w], got[w], ssem.at[w], rsem.at[w], (x, y, 1 - c)))
        for cp in cps:
            cp.wait()

    return pl.pallas_call(
        body, name="swap_halves",
        in_specs=[ANY] * nw, out_specs=[ANY] * nw,
        out_shape=[jax.ShapeDtypeStruct((g.shape[0], g.shape[1] // 2, g.shape[2]), g.dtype) for g in grads],
        scratch_shapes=[pltpu.SemaphoreType.DMA((nw,)), pltpu.SemaphoreType.DMA((nw,))],
    )(*grads)


def _add_pairs(grads, got, pos, steps):
    nw = len(grads)

    def body(pos_ref, *refs):
        for w in range(nw):
            refs[2 * nw + w][...] = (refs[w][...].astype(F32) + refs[nw + w][...].astype(F32)).astype(BF16)

    blk = lambda a: (a.shape[0], a.shape[1] // steps, a.shape[2])
    own = [pl.BlockSpec(blk(a), lambda i, p: (0, p[1] * steps + i, 0)) for a in got]
    rec = [pl.BlockSpec(blk(a), lambda i, p: (0, i, 0)) for a in got]
    return pl.pallas_call(
        body, name="add_pairs",
        grid_spec=pltpu.PrefetchScalarGridSpec(num_scalar_prefetch=1, grid=(steps,), in_specs=own + rec, out_specs=rec),
        out_shape=[jax.ShapeDtypeStruct(a.shape, BF16) for a in got],
        compiler_params=_params("arbitrary"),
    )(pos, *grads, *got)


def _scatter_chips(parts):
    nw = len(parts)

    def body(*refs):
        ins, got = refs[:nw], refs[nw:2 * nw]
        ssem, rsem = refs[2 * nw:]
        x, y, c, chips = _place()
        cps = []
        for w in range(nw):
            for s, (px, py) in enumerate(chips):
                for r0, rs in _row_chunks(parts[w].shape[1], _row_bytes(parts[w])):
                    _rcopy(ins[w].at[2 * px + py, pl.ds(r0, rs)], got[w].at[s, pl.ds(r0, rs)],
                           ssem.at[w, s], rsem.at[w, s], (px, py, c)).start()
                cps.append(_rcopy(got[w].at[s], got[w].at[s], ssem.at[w, s], rsem.at[w, s], (px, py, c)))
        for cp in cps:
            cp.wait()

    return pl.pallas_call(
        body, name="scatter_chips",
        in_specs=[ANY] * nw, out_specs=[ANY] * nw,
        out_shape=[jax.ShapeDtypeStruct((3,) + p.shape[1:], p.dtype) for p in parts],
        scratch_shapes=[pltpu.SemaphoreType.DMA((nw, 3)), pltpu.SemaphoreType.DMA((nw, 3))],
    )(*parts)


def _sum_chips(parts, got, pos, steps):
    nw = len(parts)

    def body(pos_ref, *refs):
        for w in range(nw):
            acc = refs[w][0].astype(F32)
            for s in range(3):
                acc = acc + refs[nw + w][s].astype(F32)
            refs[2 * nw + w][...] = acc

    own = [pl.BlockSpec((1, p.shape[1] // steps, p.shape[2]), lambda i, ps: (ps[0], i, 0)) for p in parts]
    rec = [pl.BlockSpec((3, p.shape[1] // steps, p.shape[2]), lambda i, ps: (0, i, 0)) for p in parts]
    outs = [pl.BlockSpec((p.shape[1] // steps, p.shape[2]), lambda i, ps: (ps[1] * steps + i, 0)) for p in parts]
    return pl.pallas_call(
        body, name="sum_chips",
        grid_spec=pltpu.PrefetchScalarGridSpec(num_scalar_prefetch=1, grid=(steps,), in_specs=own + rec, out_specs=outs),
        out_shape=[jax.ShapeDtypeStruct((2 * p.shape[1], p.shape[2]), F32) for p in parts],
        compiler_params=_params("arbitrary"),
    )(pos, *parts, *got)


def _join_halves(shards):
    nw = len(shards)

    def body(*refs):
        outs = refs[nw:2 * nw]
        ssem, rsem = refs[2 * nw:]
        x, y, c, _ = _place()
        cps = []
        for w in range(nw):
            hr = shards[w].shape[0] // 2
            for r0, rs in _row_chunks(hr, _row_bytes(shards[w])):
                piece = outs[w].at[pl.ds(c * hr + r0, rs)]
                _rcopy(piece, piece, ssem.at[w], rsem.at[w], (x, y, 1 - c)).start()
            mine = outs[w].at[pl.ds(c * hr, hr)]
            cps.append(_rcopy(mine, mine, ssem.at[w], rsem.at[w], (x, y, 1 - c)))
        for cp in cps:
            cp.wait()

    return pl.pallas_call(
        body, name="join_halves",
        in_specs=[ANY] * nw, out_specs=[ANY] * nw,
        out_shape=[jax.ShapeDtypeStruct(s.shape, F32) for s in shards],
        input_output_aliases={w: w for w in range(nw)},
        scratch_shapes=[pltpu.SemaphoreType.DMA((nw,)), pltpu.SemaphoreType.DMA((nw,))],
    )(*shards)


def _allreduce_small(packed):
    shape = packed.shape

    def body(p_ref, out_ref, rbuf, ssem, rsem):
        x, y, c, _ = _place()
        out_ref[...] = p_ref[...]
        for st, peer in enumerate([(x, y, 1 - c), (1 - x, y, c), (x, 1 - y, c)]):
            cp = _rcopy(out_ref, rbuf.at[st], ssem.at[st], rsem.at[st], peer)
            cp.start()
            cp.wait()
            out_ref[...] = out_ref[...] + rbuf[st]

    vm = pl.BlockSpec(memory_space=pltpu.VMEM)
    return pl.pallas_call(
        body, name="allreduce_small",
        in_specs=[vm], out_specs=vm, out_shape=jax.ShapeDtypeStruct(shape, F32),
        scratch_shapes=[pltpu.VMEM((3,) + shape, F32), pltpu.SemaphoreType.DMA((3,)), pltpu.SemaphoreType.DMA((3,))],
        compiler_params=pltpu.CompilerParams(vmem_limit_bytes=VMEM_LIMIT),
    )(packed)


def _adamw_math(w, g, m, v):
    m = ADAM_B1 * m + (1.0 - ADAM_B1) * g
    v = ADAM_B2 * v + (1.0 - ADAM_B2) * (g * g)
    delta = -ADAM_LR * ((m * ADAM_C1) / (jnp.sqrt(v * ADAM_C2) + ADAM_EPS) + ADAM_WD * w)
    return delta, m, v


def _adamw(ws, gs, ms, vs, steps, name):
    nw = len(ws)

    def body(*refs):
        for k in range(nw):
            d, m, v = _adamw_math(refs[k][...], refs[nw + k][...], refs[2 * nw + k][...], refs[3 * nw + k][...])
            refs[4 * nw + k][...] = d
            refs[5 * nw + k][...] = m
            refs[6 * nw + k][...] = v

    specs = [pl.BlockSpec((a.shape[0] // steps, a.shape[1]), lambda i: (i, 0)) for a in ws]
    shapes = [jax.ShapeDtypeStruct(a.shape, F32) for a in ws]
    outs = pl.pallas_call(
        body, name=name, grid=(steps,),
        in_specs=specs * 4, out_specs=specs * 3, out_shape=shapes * 3,
        compiler_params=_params("arbitrary"),
    )(*ws, *gs, *ms, *vs)
    return outs[:nw], outs[nw:2 * nw], outs[2 * nw:]


SMALL = ["norm_mix_g", "pool_w", "pool_b", "pool_scale", "conv_b", "gate_a_w", "gate_a_b", "gate_x_w", "gate_x_b",
         "lru_L", "norm_mlp_g", "norm_ple_g", "b_ple_gate", "norm_final_g"]
BIG = ["w_in", "w_out", "w_up", "w_down", "w_ple_gate", "w_ple_proj"]
ORDER = ["norm_mix_g", "w_in", "pool_w", "pool_b", "pool_scale", "conv_w", "conv_b", "gate_a_w", "gate_a_b", "gate_x_w",
         "gate_x_b", "lru_L", "w_out", "norm_mlp_g", "w_up", "w_down", "norm_ple_g", "w_ple_gate", "b_ple_gate",
         "w_ple_proj", "norm_final_g"]
LANES = 128


def _block_diag(w):
    eye = jnp.eye(LRU_HEADS, dtype=w.dtype)
    return jnp.einsum("hij,hk->hikj", w, eye).reshape(D_LRU, D_LRU)


def _diag_blocks(full):
    f = full.reshape(LRU_HEADS, LRU_BLOCK, LRU_HEADS, LRU_BLOCK)
    return jnp.stack([f[h, :, h, :] for h in range(LRU_HEADS)])


def _rows128(a):
    return a.reshape(-1, LANES)


def _pad8(a):
    r = (-a.shape[0]) % 8
    return jnp.pad(a, ((0, r), (0, 0))) if r else a


def kernel(x, p, norm_mix_g, w_in, pool_w, pool_b, pool_scale, conv_w, conv_b, gate_a_w, gate_a_b, gate_x_w, gate_x_b, lru_L, w_out, norm_mlp_g, w_up, w_down, norm_ple_g, w_ple_gate, b_ple_gate, w_ple_proj, norm_final_g, loss_target, m_norm_mix_g, m_w_in, m_pool_w, m_pool_b, m_pool_scale, m_conv_w, m_conv_b, m_gate_a_w, m_gate_a_b, m_gate_x_w, m_gate_x_b, m_lru_L, m_w_out, m_norm_mlp_g, m_w_up, m_w_down, m_norm_ple_g, m_w_ple_gate, m_b_ple_gate, m_w_ple_proj, m_norm_final_g, v_norm_mix_g, v_w_in, v_pool_w, v_pool_b, v_pool_scale, v_conv_w, v_conv_b, v_gate_a_w, v_gate_a_b, v_gate_x_w, v_gate_x_b, v_lru_L, v_w_out, v_norm_mlp_g, v_w_up, v_w_down, v_norm_ple_g, v_w_ple_gate, v_b_ple_gate, v_w_ple_proj, v_norm_final_g):
    W = dict(norm_mix_g=norm_mix_g, w_in=w_in, pool_w=pool_w, pool_b=pool_b, pool_scale=pool_scale, conv_w=conv_w,
             conv_b=conv_b, gate_a_w=gate_a_w, gate_a_b=gate_a_b, gate_x_w=gate_x_w, gate_x_b=gate_x_b, lru_L=lru_L,
             w_out=w_out, norm_mlp_g=norm_mlp_g, w_up=w_up, w_down=w_down, norm_ple_g=norm_ple_g,
             w_ple_gate=w_ple_gate, b_ple_gate=b_ple_gate, w_ple_proj=w_ple_proj, norm_final_g=norm_final_g)
    M = dict(norm_mix_g=m_norm_mix_g, w_in=m_w_in, pool_w=m_pool_w, pool_b=m_pool_b, pool_scale=m_pool_scale,
             conv_w=m_conv_w, conv_b=m_conv_b, gate_a_w=m_gate_a_w, gate_a_b=m_gate_a_b, gate_x_w=m_gate_x_w,
             gate_x_b=m_gate_x_b, lru_L=m_lru_L, w_out=m_w_out, norm_mlp_g=m_norm_mlp_g, w_up=m_w_up, w_down=m_w_down,
             norm_ple_g=m_norm_ple_g, w_ple_gate=m_w_ple_gate, b_ple_gate=m_b_ple_gate, w_ple_proj=m_w_ple_proj,
             norm_final_g=m_norm_final_g)
    V = dict(norm_mix_g=v_norm_mix_g, w_in=v_w_in, pool_w=v_pool_w, pool_b=v_pool_b, pool_scale=v_pool_scale,
             conv_w=v_conv_w, conv_b=v_conv_b, gate_a_w=v_gate_a_w, gate_a_b=v_gate_a_b, gate_x_w=v_gate_x_w,
             gate_x_b=v_gate_x_b, lru_L=v_lru_L, w_out=v_w_out, norm_mlp_g=v_norm_mlp_g, w_up=v_w_up, w_down=v_w_down,
             norm_ple_g=v_norm_ple_g, w_ple_gate=v_w_ple_gate, b_ple_gate=v_b_ple_gate, w_ple_proj=v_w_ple_proj,
             norm_final_g=v_norm_final_g)

    s_len = x.shape[1]
    tm_mix = min(256, s_len)
    tm = min(512, s_len)
    chip = (2 * lax.axis_index("x") + lax.axis_index("y")).astype(jnp.int32)
    pos = jnp.stack([chip, lax.axis_index("c").astype(jnp.int32)])

    shards = [w_in[0], w_out[0], w_up[0], w_down[0], w_ple_gate[0], w_ple_proj[0], jnp.pad(conv_w[0], ((0, 12), (0, 0)))]
    stacks = _stack_own(shards, [BF16] * 6 + [F32], pos, 8)
    win_g, wout_g, wup_g, wdn_g, wpg_g, wpp_g, cw_g = _gather_weights(stacks)
    wout_f = wout_g.reshape(D_MODEL, D_MODEL)
    wpg_f = wpg_g.reshape(D_MODEL, D_MODEL)
    wpp_f = jnp.transpose(wpp_g, (1, 0, 2)).reshape(PLE_DIM, D_MODEL)
    cw_f = jnp.transpose(cw_g[:, :CONV_WIDTH], (1, 0, 2)).reshape(CONV_WIDTH, D_LRU)
    pw_b = pool_w[0].astype(BF16)
    wa_b = _block_diag(gate_a_w[0]).astype(BF16)
    wx_b = _block_diag(gate_x_w[0]).astype(BF16)
    pb_r = pool_b.reshape(1, D_POOL)
    ba_r = gate_a_b.reshape(1, D_LRU)
    bx_r = gate_x_b.reshape(1, D_LRU)
    g4 = norm_final_g.reshape(1, D_MODEL)
    mix_w = (pw_b, pb_r, pool_scale, cw_f, conv_b, wa_b, ba_r, wx_b, bx_r, lru_L, wout_f)

    xs, ps, ts = x[0], p[0, 0], loss_target[0]
    z1, proj, hst, cat, h1 = _fwd_mix(xs, norm_mix_g, win_g, *mix_w, tm_mix)
    z2, ru, h2 = _fwd_mlp(h1, norm_mlp_g, wup_g, wdn_g, tm)
    dh2, d_wpg, d_wpp, head_vec = _head(h2, ps, ts, norm_ple_g, wpg_f, b_ple_gate, wpp_f, g4, tm)
    dup, dh1, mlp_vec = _bwd_mlp_x(dh2, ru, h1, norm_mlp_g, wup_g, wdn_g, tm)
    d_wup, d_wdn = _bwd_mlp_w(z2, dup, ru, dh2, tm)
    dproj, d_wout, d_pw, d_wa, d_wx, mix_vec = _bwd_mix(dh1, proj, hst, cat, *mix_w, tm_mix)
    dx, d_win, in_vec = _bwd_in(dproj, z1, xs, dh1, norm_mix_g, win_g, tm)

    big = [d_win, d_wout.reshape(N_CHIPS, D_MODEL // N_CHIPS, D_MODEL), d_wup, d_wdn,
           d_wpg.reshape(N_CHIPS, D_MODEL // N_CHIPS, D_MODEL), d_wpp]
    pair = _add_pairs(big, _swap_halves(big), pos, 8)
    g_big = _join_halves(_sum_chips(pair, _scatter_chips(pair), pos, 8))

    g_small = {
        "norm_mix_g": in_vec[0:1], "pool_w": d_pw, "pool_b": mix_vec[0:1], "pool_scale": mix_vec[1:2],
        "conv_b": mix_vec[2:3], "gate_a_w": _diag_blocks(d_wa), "gate_a_b": mix_vec[3:4],
        "gate_x_w": _diag_blocks(d_wx), "gate_x_b": mix_vec[4:5], "lru_L": mix_vec[5:6], "norm_mlp_g": mlp_vec[0:1],
        "norm_ple_g": head_vec[0:1], "b_ple_gate": head_vec[2:3], "norm_final_g": head_vec[1:2],
    }
    d_cw = jnp.transpose(mix_vec[8:8 + CONV_WIDTH].reshape(CONV_WIDTH, N_CHIPS, LANES), (1, 0, 2)).reshape(-1, LANES)
    pieces = [_pad8(_rows128(g_small[k])) for k in SMALL] + [d_cw, _pad8(head_vec[3:4, :LANES])]
    offs = [0]
    for pc in pieces:
        offs.append(offs[-1] + pc.shape[0])
    red = _allreduce_small(jnp.concatenate(pieces, axis=0))
    loss = red[offs[-2], 0]
    g_cw = lax.dynamic_slice(red, (offs[len(SMALL)] + CONV_WIDTH * chip, 0), (CONV_WIDTH, LANES))

    def packed(src):
        return jnp.concatenate([_pad8(_rows128(src[k])) for k in SMALL] + [_pad8(src["conv_w"][0])], axis=0)

    n_small = offs[len(SMALL)]
    g_pack = jnp.concatenate([red[:n_small], _pad8(g_cw)], axis=0)
    (d_pack,), (m_pack,), (v_pack,) = _adamw([packed(W)], [g_pack], [packed(M)], [packed(V)], 1, "adamw_small")

    big2d = lambda src: [src[k][0] for k in BIG]
    d_big, m_big, v_big = _adamw(big2d(W), g_big, big2d(M), big2d(V), 8, "adamw_big")

    def unpack(pack, big_list):
        out = {}
        for idx, k in enumerate(SMALL):
            n_el = W[k].size
            out[k] = pack[offs[idx]:offs[idx + 1]].reshape(-1)[:n_el].reshape(W[k].shape)
        out["conv_w"] = pack[n_small:n_small + CONV_WIDTH].reshape(W["conv_w"].shape)
        for k, a in zip(BIG, big_list):
            out[k] = a.reshape(W[k].shape)
        return out

    grads = unpack(g_pack, g_big)
    deltas = unpack(d_pack, d_big)
    new_m = unpack(m_pack, m_big)
    new_v = unpack(v_pack, v_big)
    return (loss, dx[None], *[grads[k] for k in ORDER], *[deltas[k] for k in ORDER],
            *[new_m[k] for k in ORDER], *[new_v[k] for k in ORDER])
```

```python
import functools

import jax
import jax.numpy as jnp
from jax import lax
from jax.experimental import pallas as pl
from jax.experimental.pallas import tpu as pltpu

F32 = jnp.float32
BF16 = jnp.bfloat16
MESH = pl.DeviceIdType.MESH

D_MODEL = 1024
D_POOL = 512
D_LRU = 512
POOL_WINDOWS = (2, 4, 8, 16)
POOL_GROUP = 128
POOL_HALO = 16
CONV_WIDTH = 4
CONV_HALO = 8
PASS_STEPS = 4
LRU_HEADS = 8
LRU_BLOCK = 64
LRU_C = 8.0
D_FF = 4096
PLE_DIM = 256
D_IN_PROJ = 1536
RMS_EPS = 1e-6
N_CHIPS = 4
FF_BLOCK = D_FF // N_CHIPS

ADAM_LR = 0.001
ADAM_B1 = 0.9
ADAM_B2 = 0.999
ADAM_EPS = 1e-08
ADAM_WD = 0.01
ADAM_STEP = 10
ADAM_C1 = 1.0 / (1.0 - ADAM_B1 ** ADAM_STEP)
ADAM_C2 = 1.0 / (1.0 - ADAM_B2 ** ADAM_STEP)

VMEM_LIMIT = 56 * 1024 * 1024
GELU_C = 0.7978845608028654
GELU_A = 0.044715

NT = (((1,), (1,)), ((), ()))
TN = (((0,), (0,)), ((), ()))


def _dot(a, b):
    return jnp.dot(a, b, preferred_element_type=F32)


def _dot_nt(a, b):
    return lax.dot_general(a, b, NT, preferred_element_type=F32)


def _dot_tn(a, b):
    return lax.dot_general(a, b, TN, preferred_element_type=F32)


def _params(*sem):
    return pltpu.CompilerParams(dimension_semantics=sem, vmem_limit_bytes=VMEM_LIMIT)


def _full(shape):
    nd = len(shape)
    return pl.BlockSpec(shape, lambda *_: (0,) * nd)


def _rstd(x):
    return lax.rsqrt(jnp.mean(x * x, axis=-1, keepdims=True) + RMS_EPS)


def _rms_bwd(x, g, dz):
    xr = x * _rstd(x)
    r = _rstd(x)
    dyg = dz * g
    dx = r * (dyg - xr * jnp.mean(dyg * xr, axis=-1, keepdims=True))
    dg = jnp.sum(dz * xr, axis=0, keepdims=True)
    return dx, dg


def _sigmoid(x):
    return 1.0 / (1.0 + jnp.exp(-x))


def _log_sigmoid(v):
    u = jnp.exp(-jnp.abs(v))
    w = 1.0 + u
    l1p = jnp.where(w == 1.0, u, jnp.log(w) * u / jnp.where(w == 1.0, 1.0, w - 1.0))
    return jnp.minimum(v, 0.0) - l1p


def _gelu(x):
    t = jnp.tanh(GELU_C * (x + GELU_A * x * x * x))
    return 0.5 * x * (1.0 + t), t


def _gelu_grad(x, t):
    return 0.5 * (1.0 + t) + 0.5 * x * (1.0 - t * t) * GELU_C * (1.0 + 3.0 * GELU_A * x * x)


def _rows(shape, t0):
    return lax.broadcasted_iota(jnp.int32, shape, 0) + t0


def _pool_diff(u_pool, prev, t0):
    tm = u_pool.shape[0]
    rows = _rows((tm, POOL_GROUP), t0)
    outs, invs = [], []
    for g, w in enumerate(POOL_WINDOWS):
        sl = slice(POOL_GROUP * g, POOL_GROUP * (g + 1))
        ug = u_pool[:, sl]
        s = jnp.concatenate([prev[:, sl], ug], axis=0)
        k = 1
        while k < w:
            s = s + pltpu.roll(s, k, 0)
            k *= 2
        inv = 1.0 / jnp.minimum(rows + 1, w).astype(F32)
        outs.append(s[POOL_HALO:] * inv - ug)
        invs.append(inv)
    return jnp.concatenate(outs, axis=1), jnp.concatenate(invs, axis=1)


def _pool_diff_bwd(dd, inv, nxt):
    ddc = dd * inv
    outs = []
    for g, w in enumerate(POOL_WINDOWS):
        sl = slice(POOL_GROUP * g, POOL_GROUP * (g + 1))
        s = jnp.concatenate([ddc[:, sl], nxt[:, sl]], axis=0)
        n = s.shape[0]
        k = 1
        while k < w:
            s = s + pltpu.roll(s, n - k, 0)
            k *= 2
        outs.append(s[:n - POOL_HALO] - dd[:, sl])
    return jnp.concatenate(outs, axis=1), ddc


def _conv_taps(u, prev):
    ext = jnp.concatenate([prev, u], axis=0)
    return [pltpu.roll(ext, CONV_WIDTH - 1 - k, 0)[CONV_HALO:] if k < CONV_WIDTH - 1 else u for k in range(CONV_WIDTH)]


def _scan_fwd(a, b):
    tm = a.shape[0]
    rows = _rows(a.shape, 0)
    k = 1
    while k < tm:
        ar = pltpu.roll(a, k, 0)
        br = pltpu.roll(b, k, 0)
        m = rows >= k
        b = jnp.where(m, a * br + b, b)
        a = jnp.where(m, a * ar, a)
        k *= 2
    return a, b


def _scan_rev(a, b):
    tm = a.shape[0]
    rows = _rows(a.shape, 0)
    k = 1
    while k < tm:
        ar = pltpu.roll(a, tm - k, 0)
        br = pltpu.roll(b, tm - k, 0)
        m = rows < tm - k
        b = jnp.where(m, a * br + b, b)
        a = jnp.where(m, a * ar, a)
        k *= 2
    return a, b


def _lru_gates(xb, wa, ba, wx, bx, lsl8, t0):
    xbb = xb.astype(BF16)
    r = _sigmoid(_dot(xbb, wa) + ba)
    ig = _sigmoid(_dot(xbb, wx) + bx)
    a = jnp.exp(r * lsl8)
    first = _rows(xb.shape, t0) == 0
    mult = jnp.where(first, 1.0, jnp.sqrt(1.0 - a * a))
    return r, ig, a, mult, first


def _fwd_mix(x, g1, w_in, pool_w, pool_b, pool_scale, conv_w, conv_b, wa, ba, wx, bx, lru_l, w_out, tm, late):
    s_len = x.shape[0]
    n = s_len // tm
    nl = len(late)

    def body(x_ref, g1_ref, win_ref, pw_ref, pb_ref, ps_ref, cw_ref, cb_ref, wa_ref, ba_ref, wx_ref, bx_ref, l_ref,
             wout_ref, *rest):
        z1_ref, proj_ref, h_ref, cat_ref, h1_ref = rest[nl:nl + 5]
        late_ref = rest[nl + 5:2 * nl + 5]
        cpool, clru, ch, ssem, rsem = rest[2 * nl + 5:]
        i = pl.program_id(0)

        @pl.when(i == 0)
        def _():
            _gather_send(late_ref, late, ssem, rsem)
            cpool[...] = jnp.zeros_like(cpool)
            clru[...] = jnp.zeros_like(clru)
            ch[...] = jnp.zeros_like(ch)

        t0 = i * tm
        xv = x_ref[...]
        zb = (xv * _rstd(xv) * g1_ref[...]).astype(BF16)
        z1_ref[...] = zb
        proj = jnp.concatenate([_dot(zb, win_ref[j]) for j in range(N_CHIPS)], axis=1)
        proj_ref[...] = proj
        u_pool = proj[:, :D_POOL]
        u_lru = proj[:, D_POOL:D_POOL + D_LRU]
        u_gate = proj[:, D_POOL + D_LRU:]

        d, _ = _pool_diff(u_pool, cpool[...], t0)
        cpool[...] = u_pool[tm - POOL_HALO:]
        db = d.astype(BF16)
        yp = jnp.concatenate(
            [_dot(db[:, POOL_GROUP * g:POOL_GROUP * (g + 1)], pw_ref[g]) for g in range(len(POOL_WINDOWS))], axis=1)
        y_pool = (yp + pb_ref[...]) * ps_ref[...]

        taps = _conv_taps(u_lru, clru[...])
        clru[...] = u_lru[tm - CONV_HALO:]
        xb = cb_ref[...]
        for k in range(CONV_WIDTH):
            xb = xb + taps[k] * cw_ref[k:k + 1, :]
        lsl8 = LRU_C * _log_sigmoid(l_ref[...])
        _, ig, a, mult, _ = _lru_gates(xb, wa_ref[...], ba_ref[...], wx_ref[...], bx_ref[...], lsl8, t0)
        pa, hb = _scan_fwd(a, mult * (ig * xb))
        h = hb + pa * ch[7:8, :]
        ch[...] = h[tm - 8:]
        h_ref[...] = h
        gl, _ = _gelu(u_gate)
        cat = jnp.concatenate([y_pool, h * gl], axis=1).astype(BF16)
        cat_ref[...] = cat
        h1_ref[...] = xv + _dot(cat, wout_ref[...])

        @pl.when(i == max(n - PASS_STEPS, 0))
        def _():
            _gather_pass(late_ref, late, ssem, rsem)

        @pl.when(i == n - 1)
        def _():
            _gather_done(late_ref, late, ssem, rsem)

    row = lambda w: pl.BlockSpec((tm, w), lambda i: (i, 0))
    ins = [x, g1, w_in, pool_w, pool_b, pool_scale, conv_w, conv_b, wa, ba, wx, bx, lru_l, w_out]
    outs = pl.pallas_call(
        body, name="fwd_mix", grid=(n,),
        in_specs=[row(D_MODEL)] + [_full(a.shape) for a in ins[1:]] + [ANY] * nl,
        out_specs=[row(D_MODEL), row(D_IN_PROJ), row(D_LRU), row(D_MODEL), row(D_MODEL)] + [ANY] * nl,
        out_shape=[jax.ShapeDtypeStruct((s_len, D_MODEL), BF16), jax.ShapeDtypeStruct((s_len, D_IN_PROJ), F32),
                   jax.ShapeDtypeStruct((s_len, D_LRU), F32), jax.ShapeDtypeStruct((s_len, D_MODEL), BF16),
                   jax.ShapeDtypeStruct((s_len, D_MODEL), F32)]
        + [jax.ShapeDtypeStruct(a.shape, a.dtype) for a in late],
        input_output_aliases={len(ins) + k: 5 + k for k in range(nl)},
        scratch_shapes=[pltpu.VMEM((POOL_HALO, D_POOL), F32), pltpu.VMEM((CONV_HALO, D_LRU), F32),
                        pltpu.VMEM((8, D_LRU), F32)] + _gather_sems(nl),
        compiler_params=_params("arbitrary"),
    )(*ins, *late)
    return outs[:5], outs[5:]


def _fwd_mlp(h1, g2, w_up, w_down, tm):
    s_len = h1.shape[0]
    n = s_len // tm

    def body(h1_ref, g2_ref, wup_ref, wdn_ref, z2_ref, ru_ref, h2_ref, acc):
        j = pl.program_id(1)

        @pl.when(j == 0)
        def _():
            hv = h1_ref[...]
            z2_ref[...] = (hv * _rstd(hv) * g2_ref[...]).astype(BF16)
            acc[...] = hv

        ru = jnp.maximum(_dot(z2_ref[...], wup_ref[0]), 0.0)
        ru_ref[...] = ru.astype(BF16)
        acc[...] += _dot((ru * ru).astype(BF16), wdn_ref[0])

        @pl.when(j == N_CHIPS - 1)
        def _():
            h2_ref[...] = acc[...]

    return pl.pallas_call(
        body, name="fwd_mlp", grid=(n, N_CHIPS),
        in_specs=[pl.BlockSpec((tm, D_MODEL), lambda i, j: (i, 0)), _full(g2.shape),
                  pl.BlockSpec((1, D_MODEL, FF_BLOCK), lambda i, j: (j, 0, 0)),
                  pl.BlockSpec((1, FF_BLOCK, D_MODEL), lambda i, j: (j, 0, 0))],
        out_specs=[pl.BlockSpec((tm, D_MODEL), lambda i, j: (i, 0)),
                   pl.BlockSpec((tm, FF_BLOCK), lambda i, j: (i, j)),
                   pl.BlockSpec((tm, D_MODEL), lambda i, j: (i, 0))],
        out_shape=[jax.ShapeDtypeStruct((s_len, D_MODEL), BF16), jax.ShapeDtypeStruct((s_len, D_FF), BF16),
                   jax.ShapeDtypeStruct((s_len, D_MODEL), F32)],
        scratch_shapes=[pltpu.VMEM((tm, D_MODEL), F32)],
        compiler_params=_params("arbitrary", "arbitrary"),
    )(h1, g2, w_up, w_down)


def _head(h2, p, target, g3, w_pg, b_pg, w_pp, g4, tm):
    s_len = h2.shape[0]
    n = s_len // tm

    def body(h2_ref, p_ref, t_ref, g3_ref, wpg_ref, bpg_ref, wpp_ref, g4_ref,
             dh2_ref, dwpg_ref, dwpp_ref, vec_ref, a_pg, a_pp, a_vec):
        i = pl.program_id(0)

        @pl.when(i == 0)
        def _():
            a_pg[...] = jnp.zeros_like(a_pg)
            a_pp[...] = jnp.zeros_like(a_pp)
            a_vec[...] = jnp.zeros_like(a_vec)

        h2v = h2_ref[...]
        g3v = g3_ref[...]
        g4v = g4_ref[...]
        z3 = (h2v * _rstd(h2v) * g3v).astype(BF16)
        gate = _sigmoid(_dot(z3, wpg_ref[...]) + bpg_ref[...])
        pb = p_ref[...].astype(BF16)
        pp = _dot(pb, wpp_ref[...])
        h3 = h2v + gate * pp
        r4 = _rstd(h3)
        diff = h3 * r4 * g4v - t_ref[...]
        loss = 0.5 * jnp.sum(jnp.mean(diff * diff, axis=-1, keepdims=True), axis=0, keepdims=True)
        dy = diff * (1.0 / D_MODEL)
        dh3, dg4 = _rms_bwd(h3, g4v, dy)
        dpp = (dh3 * gate).astype(BF16)
        dpre = dh3 * pp * gate * (1.0 - gate)
        dpreb = dpre.astype(BF16)
        dz3 = _dot_nt(dpreb, wpg_ref[...])
        dx, dg3 = _rms_bwd(h2v, g3v, dz3)
        dh2_ref[...] = dh3 + dx
        a_pg[...] += _dot_tn(z3, dpreb)
        a_pp[...] += _dot_tn(pb, dpp)
        a_vec[0:1, :] += dg3
        a_vec[1:2, :] += dg4
        a_vec[2:3, :] += jnp.sum(dpre, axis=0, keepdims=True)
        a_vec[3:4, :] += jnp.broadcast_to(loss, (1, D_MODEL))

        @pl.when(i == n - 1)
        def _():
            dwpg_ref[...] = a_pg[...].astype(BF16)
            for j in range(N_CHIPS):
                dwpp_ref[j] = a_pp[:, PLE_DIM * j:PLE_DIM * (j + 1)].astype(BF16)
            vec_ref[...] = a_vec[...]

    row = lambda w: pl.BlockSpec((tm, w), lambda i: (i, 0))
    ins = [h2, p, target, g3, w_pg, b_pg, w_pp, g4]
    return pl.pallas_call(
        body, name="head", grid=(n,),
        in_specs=[row(D_MODEL), row(PLE_DIM), row(D_MODEL)] + [_full(a.shape) for a in ins[3:]],
        out_specs=[row(D_MODEL), _full((D_MODEL, D_MODEL)), _full((N_CHIPS, PLE_DIM, PLE_DIM)), _full((8, D_MODEL))],
        out_shape=[jax.ShapeDtypeStruct((s_len, D_MODEL), F32), jax.ShapeDtypeStruct((D_MODEL, D_MODEL), BF16),
                   jax.ShapeDtypeStruct((N_CHIPS, PLE_DIM, PLE_DIM), BF16), jax.ShapeDtypeStruct((8, D_MODEL), F32)],
        scratch_shapes=[pltpu.VMEM((D_MODEL, D_MODEL), F32), pltpu.VMEM((PLE_DIM, D_MODEL), F32),
                        pltpu.VMEM((8, D_MODEL), F32)],
        compiler_params=_params("arbitrary"),
    )(*ins)


def _bwd_mlp_x(dh2, ru, h1, g2, w_up, w_down, tm):
    s_len = dh2.shape[0]
    n = s_len // tm

    def body(dh2_ref, ru_ref, h1_ref, g2_ref, wup_ref, wdn_ref, dup_ref, dh1_ref, dg2_ref, acc, a_g):
        i = pl.program_id(0)
        j = pl.program_id(1)

        @pl.when((i == 0) & (j == 0))
        def _():
            a_g[...] = jnp.zeros_like(a_g)

        @pl.when(j == 0)
        def _():
            acc[...] = jnp.zeros_like(acc)

        dact = _dot_nt(dh2_ref[...].astype(BF16), wdn_ref[0])
        dup = (dact * (2.0 * ru_ref[...].astype(F32))).astype(BF16)
        dup_ref[...] = dup
        acc[...] += _dot_nt(dup, wup_ref[0])

        @pl.when(j == N_CHIPS - 1)
        def _():
            dx, dg = _rms_bwd(h1_ref[...], g2_ref[...], acc[...])
            dh1_ref[...] = dh2_ref[...] + dx
            a_g[0:1, :] += dg

        @pl.when((i == n - 1) & (j == N_CHIPS - 1))
        def _():
            dg2_ref[...] = a_g[...]

    tile = pl.BlockSpec((tm, D_MODEL), lambda i, j: (i, 0))
    ffb = pl.BlockSpec((tm, FF_BLOCK), lambda i, j: (i, j))
    return pl.pallas_call(
        body, name="bwd_mlp_x", grid=(n, N_CHIPS),
        in_specs=[tile, ffb, tile, _full(g2.shape),
                  pl.BlockSpec((1, D_MODEL, FF_BLOCK), lambda i, j: (j, 0, 0)),
                  pl.BlockSpec((1, FF_BLOCK, D_MODEL), lambda i, j: (j, 0, 0))],
        out_specs=[ffb, tile, _full((8, D_MODEL))],
        out_shape=[jax.ShapeDtypeStruct((s_len, D_FF), BF16), jax.ShapeDtypeStruct((s_len, D_MODEL), F32),
                   jax.ShapeDtypeStruct((8, D_MODEL), F32)],
        scratch_shapes=[pltpu.VMEM((tm, D_MODEL), F32), pltpu.VMEM((8, D_MODEL), F32)],
        compiler_params=_params("arbitrary", "arbitrary"),
    )(dh2, ru, h1, g2, w_up, w_down)


def _bwd_mlp_w(z2, dup, ru, dh2, tk):
    s_len = z2.shape[0]
    n = s_len // tk

    def body(z2_ref, dup_ref, ru_ref, dh2_ref, dwup_ref, dwdn_ref, a_up, a_dn):
        t = pl.program_id(1)

        @pl.when(t == 0)
        def _():
            a_up[...] = jnp.zeros_like(a_up)
            a_dn[...] = jnp.zeros_like(a_dn)

        ruv = ru_ref[...]
        a_up[...] += _dot_tn(z2_ref[...], dup_ref[...])
        a_dn[...] += _dot_tn(ruv * ruv, dh2_ref[...].astype(BF16))

        @pl.when(t == n - 1)
        def _():
            dwup_ref[0] = a_up[...].astype(BF16)
            dwdn_ref[0] = a_dn[...].astype(BF16)

    tile = pl.BlockSpec((tk, D_MODEL), lambda j, t: (t, 0))
    ffb = pl.BlockSpec((tk, FF_BLOCK), lambda j, t: (t, j))
    return pl.pallas_call(
        body, name="bwd_mlp_w", grid=(N_CHIPS, n),
        in_specs=[tile, ffb, ffb, tile],
        out_specs=[pl.BlockSpec((1, D_MODEL, FF_BLOCK), lambda j, t: (j, 0, 0)),
                   pl.BlockSpec((1, FF_BLOCK, D_MODEL), lambda j, t: (j, 0, 0))],
        out_shape=[jax.ShapeDtypeStruct((N_CHIPS, D_MODEL, FF_BLOCK), BF16),
                   jax.ShapeDtypeStruct((N_CHIPS, FF_BLOCK, D_MODEL), BF16)],
        scratch_shapes=[pltpu.VMEM((D_MODEL, FF_BLOCK), F32), pltpu.VMEM((FF_BLOCK, D_MODEL), F32)],
        compiler_params=_params("arbitrary", "arbitrary"),
    )(z2, dup, ru, dh2)


MIX_VEC_ROWS = 16


def _bwd_mix(dh1, proj, h, cat, pool_w, pool_b, pool_scale, conv_w, conv_b, wa, ba, wx, bx, lru_l, w_out, tm, early):
    s_len = dh1.shape[0]
    n = s_len // tm
    ng = len(POOL_WINDOWS)
    ne_ = len(early)

    def body(dh1_ref, proj_ref, h_ref, cat_ref, projh_ref, hh_ref, pw_ref, pb_ref, ps_ref, cw_ref, cb_ref,
             wa_ref, ba_ref, wx_ref, bx_ref, l_ref, wout_ref, *rest):
        early_ref = rest[:ne_]
        dproj_ref, dwout_ref, dpw_ref, dwa_ref, dwx_ref, vec_ref = rest[ne_:ne_ + 6]
        got_ref = rest[ne_ + 6:2 * ne_ + 6]
        a_out, a_pw, a_wa, a_wx, a_vec, c_g, c_dxb, c_ddc, ssem, rsem = rest[2 * ne_ + 6:]
        q = pl.program_id(0)
        i = n - 1 - q

        @pl.when(q == 0)
        def _():
            _scatter_send(early_ref, got_ref, early, ssem, rsem)
            for r in (a_out, a_pw, a_wa, a_wx, a_vec, c_g, c_dxb, c_ddc):
                r[...] = jnp.zeros_like(r)

        t0 = i * tm
        has_prev = (i > 0).astype(F32)
        dh1b = dh1_ref[...].astype(BF16)
        dcat = _dot_nt(dh1b, wout_ref[...])
        a_out[...] += _dot_tn(cat_ref[...], dh1b)
        dy_pool = dcat[:, :D_POOL]
        dy_lru = dcat[:, D_POOL:]

        proj = proj_ref[...]
        u_pool = proj[:, :D_POOL]
        u_lru = proj[:, D_POOL:D_POOL + D_LRU]
        u_gate = proj[:, D_POOL + D_LRU:]
        halo = projh_ref[...] * has_prev

        d, inv = _pool_diff(u_pool, halo[:, :D_POOL], t0)
        db = d.astype(BF16)
        ypre = jnp.concatenate(
            [_dot(db[:, POOL_GROUP * g:POOL_GROUP * (g + 1)], pw_ref[g]) for g in range(ng)], axis=1) + pb_ref[...]
        dyp = dy_pool * ps_ref[...]
        dypb = dyp.astype(BF16)
        dds = []
        for g in range(ng):
            sl = slice(POOL_GROUP * g, POOL_GROUP * (g + 1))
            a_pw[g] += _dot_tn(db[:, sl], dypb[:, sl])
            dds.append(_dot_nt(dypb[:, sl], pw_ref[g]))
        du_pool, ddc = _pool_diff_bwd(jnp.concatenate(dds, axis=1), inv, c_ddc[...])
        c_ddc[...] = ddc[:POOL_HALO]
        a_vec[0:1, :] += jnp.sum(dyp, axis=0, keepdims=True)
        a_vec[1:2, :] += jnp.sum(dy_pool * ypre, axis=0, keepdims=True)

        taps = _conv_taps(u_lru, halo[POOL_HALO - CONV_HALO:, D_POOL:D_POOL + D_LRU])
        xb = cb_ref[...]
        for k in range(CONV_WIDTH):
            xb = xb + taps[k] * cw_ref[k:k + 1, :]
        lv = l_ref[...]
        lsl8 = LRU_C * _log_sigmoid(lv)
        r, ig, a, mult, first = _lru_gates(xb, wa_ref[...], ba_ref[...], wx_ref[...], bx_ref[...], lsl8, t0)
        hv = h_ref[...]
        gl, th = _gelu(u_gate)
        du_gate = dy_lru * hv * _gelu_grad(u_gate, th)
        last = _rows(a.shape, 0) == tm - 1
        a_next = jnp.where(last, 1.0, pltpu.roll(a, tm - 1, 0))
        pa, gb = _scan_rev(a_next, dy_lru * gl)
        gh = gb + pa * c_g[0:1, :]
        c_g[...] = jnp.broadcast_to(a[0:1, :] * gh[0:1, :], c_g.shape)
        h_prev_row = hh_ref[7:8, :] * has_prev
        h_prev = jnp.where(_rows(hv.shape, 0) == 0, h_prev_row, pltpu.roll(hv, 1, 0))
        gix = gh * ig * xb
        dla = gh * h_prev * a - jnp.where(first, 0.0, gix * a * a / mult)
        dpre_r = dla * lsl8 * r * (1.0 - r)
        dpre_i = gh * mult * xb * ig * (1.0 - ig)
        dprb = dpre_r.astype(BF16)
        dpib = dpre_i.astype(BF16)
        xbb = xb.astype(BF16)
        a_wa[...] += _dot_tn(xbb, dprb)
        a_wx[...] += _dot_tn(xbb, dpib)
        dxb = gh * mult * ig + _dot_nt(dprb, wa_ref[...]) + _dot_nt(dpib, wx_ref[...])
        a_vec[2:3, :] += jnp.sum(dxb, axis=0, keepdims=True)
        a_vec[3:4, :] += jnp.sum(dpre_r, axis=0, keepdims=True)
        a_vec[4:5, :] += jnp.sum(dpre_i, axis=0, keepdims=True)
        a_vec[5:6, :] += jnp.sum(dla * r, axis=0, keepdims=True)
        ext = jnp.concatenate([dxb, c_dxb[...]], axis=0)
        c_dxb[...] = dxb[:CONV_HALO]
        ne = tm + CONV_HALO
        du_lru = dxb * cw_ref[CONV_WIDTH - 1:CONV_WIDTH, :]
        for k in range(CONV_WIDTH):
            a_vec[8 + k:9 + k, :] += jnp.sum(dxb * taps[k], axis=0, keepdims=True)
            if k < CONV_WIDTH - 1:
                du_lru = du_lru + pltpu.roll(ext, ne - (CONV_WIDTH - 1 - k), 0)[:tm] * cw_ref[k:k + 1, :]
        dproj_ref[...] = jnp.concatenate([du_pool, du_lru, du_gate], axis=1).astype(BF16)

        @pl.when(q == n - 1)
        def _():
            dwout_ref[...] = a_out[...].astype(BF16)
            dpw_ref[...] = a_pw[...]
            dwa_ref[...] = a_wa[...]
            dwx_ref[...] = a_wx[...]
            vec_ref[...] = a_vec[...]
            vec_ref[5:6, :] = a_vec[5:6, :] * (LRU_C * _sigmoid(-lv))
            _scatter_done(got_ref, early, ssem, rsem)

    rev =lambda w: pl.BlockSpec((tm, w), lambda q: (n - 1 - q, 0))
    halo_p = pl.BlockSpec((POOL_HALO, D_IN_PROJ), lambda q: (jnp.maximum((n - 1 - q) * (tm // POOL_HALO) - 1, 0), 0))
    halo_h = pl.BlockSpec((8, D_LRU), lambda q: (jnp.maximum((n - 1 - q) * (tm // 8) - 1, 0), 0))
    wts = [pool_w, pool_b, pool_scale, conv_w, conv_b, wa, ba, wx, bx, lru_l, w_out]
    outs = pl.pallas_call(
        body, name="bwd_mix", grid=(n,),
        in_specs=[rev(D_MODEL), rev(D_IN_PROJ), rev(D_LRU), rev(D_MODEL), halo_p, halo_h] + [_full(a.shape) for a in wts]
        + [ANY] * ne_,
        out_specs=[rev(D_IN_PROJ), _full((D_MODEL, D_MODEL)), _full((ng, POOL_GROUP, POOL_GROUP)),
                   _full((D_LRU, D_LRU)), _full((D_LRU, D_LRU)), _full((MIX_VEC_ROWS, D_LRU))] + [ANY] * ne_,
        out_shape=[jax.ShapeDtypeStruct((s_len, D_IN_PROJ), BF16), jax.ShapeDtypeStruct((D_MODEL, D_MODEL), BF16),
                   jax.ShapeDtypeStruct((ng, POOL_GROUP, POOL_GROUP), F32), jax.ShapeDtypeStruct((D_LRU, D_LRU), F32),
                   jax.ShapeDtypeStruct((D_LRU, D_LRU), F32), jax.ShapeDtypeStruct((MIX_VEC_ROWS, D_LRU), F32)]
        + _scatter_shapes(early),
        scratch_shapes=[pltpu.VMEM((D_MODEL, D_MODEL), F32), pltpu.VMEM((ng, POOL_GROUP, POOL_GROUP), F32),
                        pltpu.VMEM((D_LRU, D_LRU), F32), pltpu.VMEM((D_LRU, D_LRU), F32),
                        pltpu.VMEM((MIX_VEC_ROWS, D_LRU), F32), pltpu.VMEM((8, D_LRU), F32),
                        pltpu.VMEM((CONV_HALO, D_LRU), F32), pltpu.VMEM((POOL_HALO, D_POOL), F32)] + _scatter_sems(ne_),
        compiler_params=_params("arbitrary"),
    )(dh1, proj, h, cat, proj, h, *wts, *early)
    return outs[:6], outs[6:]


def _bwd_in(dproj, z1, x, dh1, g1, w_in, tm):
    s_len = x.shape[0]
    n = s_len // tm
    cb = D_IN_PROJ // N_CHIPS

    def body(dp_ref, z1_ref, x_ref, dh1_ref, g1_ref, win_ref, dx_ref, dwin_ref, dg1_ref, a_w, a_g):
        i = pl.program_id(0)

        @pl.when(i == 0)
        def _():
            a_w[...] = jnp.zeros_like(a_w)
            a_g[...] = jnp.zeros_like(a_g)

        dp = dp_ref[...]
        zb = z1_ref[...]
        dz = jnp.zeros((tm, D_MODEL), F32)
        for j in range(N_CHIPS):
            dpj = dp[:, cb * j:cb * (j + 1)]
            dz = dz + _dot_nt(dpj, win_ref[j])
            a_w[j] += _dot_tn(zb, dpj)
        dx, dg = _rms_bwd(x_ref[...], g1_ref[...], dz)
        dx_ref[...] = dh1_ref[...] + dx
        a_g[0:1, :] += dg

        @pl.when(i == n - 1)
        def _():
            dwin_ref[...] = a_w[...].astype(BF16)
            dg1_ref[...] = a_g[...]

    row = lambda w: pl.BlockSpec((tm, w), lambda i: (i, 0))
    return pl.pallas_call(
        body, name="bwd_in", grid=(n,),
        in_specs=[row(D_IN_PROJ), row(D_MODEL), row(D_MODEL), row(D_MODEL), _full(g1.shape), _full(w_in.shape)],
        out_specs=[row(D_MODEL), _full(w_in.shape), _full((8, D_MODEL))],
        out_shape=[jax.ShapeDtypeStruct((s_len, D_MODEL), F32), jax.ShapeDtypeStruct(w_in.shape, BF16),
                   jax.ShapeDtypeStruct((8, D_MODEL), F32)],
        scratch_shapes=[pltpu.VMEM(w_in.shape, F32), pltpu.VMEM((8, D_MODEL), F32)],
        compiler_params=_params("arbitrary"),
    )(dproj, z1, x, dh1, g1, w_in)


def _place():
    x, y, c = lax.axis_index("x"), lax.axis_index("y"), lax.axis_index("c")
    chips = [(1 - x, y), (x, 1 - y), (1 - x, 1 - y)]
    return x, y, c, chips


def _rcopy(src, dst, ssem, rsem, dev):
    return pltpu.make_async_remote_copy(src_ref=src, dst_ref=dst, send_sem=ssem, recv_sem=rsem,
                                        device_id=dev, device_id_type=MESH)


ANY = pl.BlockSpec(memory_space=pl.ANY)
COPY_CHUNK_BYTES = 128 * 1024
ROW_ALIGN = 16


def _row_chunks(rows, row_bytes):
    per = max(ROW_ALIGN, (COPY_CHUNK_BYTES // row_bytes) // ROW_ALIGN * ROW_ALIGN)
    return [(r0, min(per, rows - r0)) for r0 in range(0, rows, per)]


def _row_bytes(a):
    return a.shape[-1] * jnp.dtype(a.dtype).itemsize


def _stack_own(shards, dtypes, pos, steps):
    nw = len(shards)

    def body(pos_ref, *refs):
        for w in range(nw):
            refs[nw + w][0] = refs[w][...].astype(dtypes[w])

    def split(s):
        return s.shape[0] % (steps * ROW_ALIGN) == 0

    ins = [pl.BlockSpec((s.shape[0] // steps, s.shape[1]), lambda i, p: (i, 0)) if split(s)
           else pl.BlockSpec(s.shape, lambda i, p: (0, 0)) for s in shards]
    outs = [pl.BlockSpec((1, s.shape[0] // steps, s.shape[1]), lambda i, p: (p[0], i, 0)) if split(s)
            else pl.BlockSpec((1,) + s.shape, lambda i, p: (p[0], 0, 0)) for s in shards]
    return pl.pallas_call(
        body, name="stack_own",
        grid_spec=pltpu.PrefetchScalarGridSpec(num_scalar_prefetch=1, grid=(steps,), in_specs=ins, out_specs=outs),
        out_shape=[jax.ShapeDtypeStruct((N_CHIPS,) + s.shape, d) for s, d in zip(shards, dtypes)],
        compiler_params=_params("arbitrary"),
    )(pos, *shards)


def _gather_send(outs, stacks, ssem, rsem):
    x, y, c, chips = _place()
    me = 2 * x + y
    for w, st in enumerate(stacks):
        half = st.shape[1] // 2
        for s, (px, py) in enumerate(chips):
            for r0, rs in _row_chunks(half, _row_bytes(st)):
                piece = outs[w].at[me, pl.ds(c * half + r0, rs)]
                _rcopy(piece, piece, ssem.at[w, s], rsem.at[w, s], (px, py, c)).start()


def _gather_pass(outs, stacks, ssem, rsem):
    x, y, c, chips = _place()
    sib = (x, y, 1 - c)
    for w, st in enumerate(stacks):
        half = st.shape[1] // 2
        for s, (px, py) in enumerate(chips):
            blk = outs[w].at[2 * px + py, pl.ds(c * half, half)]
            _rcopy(blk, blk, ssem.at[w, s], rsem.at[w, s], sib).wait_recv()
            for r0, rs in _row_chunks(half, _row_bytes(st)):
                piece = outs[w].at[2 * px + py, pl.ds(c * half + r0, rs)]
                _rcopy(piece, piece, ssem.at[w, 3 + s], rsem.at[w, 3 + s], sib).start()


def _gather_done(outs, stacks, ssem, rsem):
    x, y, c, chips = _place()
    sib = (x, y, 1 - c)
    for w, st in enumerate(stacks):
        half = st.shape[1] // 2
        for s, (px, py) in enumerate(chips):
            blk = outs[w].at[2 * px + py, pl.ds((1 - c) * half, half)]
            _rcopy(blk, blk, ssem.at[w, 3 + s], rsem.at[w, 3 + s], sib).wait_recv()
    for w, st in enumerate(stacks):
        half = st.shape[1] // 2
        blk = outs[w].at[0, pl.ds(0, half)]
        for s in range(6):
            _rcopy(blk, blk, ssem.at[w, s], rsem.at[w, s], sib).wait_send()


def _gather_sems(nw):
    return [pltpu.SemaphoreType.DMA((nw, 6)), pltpu.SemaphoreType.DMA((nw, 6))]


def _gather_weights(stacks):
    nw = len(stacks)

    def body(*refs):
        outs = refs[nw:2 * nw]
        ssem, rsem = refs[2 * nw:]
        _gather_send(outs, stacks, ssem, rsem)
        _gather_pass(outs, stacks, ssem, rsem)
        _gather_done(outs, stacks, ssem, rsem)

    return pl.pallas_call(
        body, name="gather_weights",
        in_specs=[ANY] * nw, out_specs=[ANY] * nw,
        out_shape=[jax.ShapeDtypeStruct(s.shape, s.dtype) for s in stacks],
        input_output_aliases={w: w for w in range(nw)},
        scratch_shapes=_gather_sems(nw),
    )(*stacks)


def _swap_halves(grads, tag):
    nw = len(grads)

    def body(*refs):
        ins, got = refs[:nw], refs[nw:2 * nw]
        ssem, rsem = refs[2 * nw:]
        x, y, c, _ = _place()
        cps = []
        for w in range(nw):
            hr = grads[w].shape[1] // 2
            for k in range(N_CHIPS):
                for r0, rs in _row_chunks(hr, _row_bytes(grads[w])):
                    _rcopy(ins[w].at[k, pl.ds((1 - c) * hr + r0, rs)], got[w].at[k, pl.ds(r0, rs)],
                           ssem.at[w], rsem.at[w], (x, y, 1 - c)).start()
            cps.append(_rcopy(got[w], got[w], ssem.at[w], rsem.at[w], (x, y, 1 - c)))
        for cp in cps:
            cp.wait()

    return pl.pallas_call(
        body, name="swap_halves_" + tag,
        in_specs=[ANY] * nw, out_specs=[ANY] * nw,
        out_shape=[jax.ShapeDtypeStruct((g.shape[0], g.shape[1] // 2, g.shape[2]), g.dtype) for g in grads],
        scratch_shapes=[pltpu.SemaphoreType.DMA((nw,)), pltpu.SemaphoreType.DMA((nw,))],
    )(*grads)


def _add_pairs(grads, got, pos, steps, tag):
    nw = len(grads)

    def body(pos_ref, *refs):
        for w in range(nw):
            refs[2 * nw + w][...] = (refs[w][...].astype(F32) + refs[nw + w][...].astype(F32)).astype(BF16)

    blk = lambda a: (a.shape[0], a.shape[1] // steps, a.shape[2])
    own = [pl.BlockSpec(blk(a), lambda i, p: (0, p[1] * steps + i, 0)) for a in got]
    rec = [pl.BlockSpec(blk(a), lambda i, p: (0, i, 0)) for a in got]
    return pl.pallas_call(
        body, name="add_pairs_" + tag,
        grid_spec=pltpu.PrefetchScalarGridSpec(num_scalar_prefetch=1, grid=(steps,), in_specs=own + rec, out_specs=rec),
        out_shape=[jax.ShapeDtypeStruct(a.shape, BF16) for a in got],
        compiler_params=_params("arbitrary"),
    )(pos, *grads, *got)


def _scatter_send(ins, got, parts, ssem, rsem):
    x, y, c, chips = _place()
    for w, p in enumerate(parts):
        for s, (px, py) in enumerate(chips):
            for r0, rs in _row_chunks(p.shape[1], _row_bytes(p)):
                _rcopy(ins[w].at[2 * px + py, pl.ds(r0, rs)], got[w].at[s, pl.ds(r0, rs)],
                       ssem.at[w, s], rsem.at[w, s], (px, py, c)).start()


def _scatter_done(got, parts, ssem, rsem):
    x, y, c, chips = _place()
    for w in range(len(parts)):
        for s, (px, py) in enumerate(chips):
            _rcopy(got[w].at[s], got[w].at[s], ssem.at[w, s], rsem.at[w, s], (px, py, c)).wait()


def _scatter_sems(nw):
    return [pltpu.SemaphoreType.DMA((nw, 3)), pltpu.SemaphoreType.DMA((nw, 3))]


def _scatter_shapes(parts):
    return [jax.ShapeDtypeStruct((3,) + p.shape[1:], p.dtype) for p in parts]


def _scatter_chips(parts):
    nw = len(parts)

    def body(*refs):
        ins, got = refs[:nw], refs[nw:2 * nw]
        ssem, rsem = refs[2 * nw:]
        _scatter_send(ins, got, parts, ssem, rsem)
        _scatter_done(got, parts, ssem, rsem)

    return pl.pallas_call(
        body, name="scatter_chips",
        in_specs=[ANY] * nw, out_specs=[ANY] * nw, out_shape=_scatter_shapes(parts),
        scratch_shapes=_scatter_sems(nw),
    )(*parts)


def _sum_chips(parts, got, pos, steps):
    nw = len(parts)

    def body(pos_ref, *refs):
        for w in range(nw):
            acc = refs[w][0].astype(F32)
            for s in range(3):
                acc = acc + refs[nw + w][s].astype(F32)
            refs[2 * nw + w][...] = acc

    own = [pl.BlockSpec((1, p.shape[1] // steps, p.shape[2]), lambda i, ps: (ps[0], i, 0)) for p in parts]
    rec = [pl.BlockSpec((3, p.shape[1] // steps, p.shape[2]), lambda i, ps: (0, i, 0)) for p in parts]
    outs = [pl.BlockSpec((p.shape[1] // steps, p.shape[2]), lambda i, ps: (ps[1] * steps + i, 0)) for p in parts]
    return pl.pallas_call(
        body, name="sum_chips",
        grid_spec=pltpu.PrefetchScalarGridSpec(num_scalar_prefetch=1, grid=(steps,), in_specs=own + rec, out_specs=outs),
        out_shape=[jax.ShapeDtypeStruct((2 * p.shape[1], p.shape[2]), F32) for p in parts],
        compiler_params=_params("arbitrary"),
    )(pos, *parts, *got)


def _join_halves(shards):
    nw = len(shards)

    def body(*refs):
        outs = refs[nw:2 * nw]
        ssem, rsem = refs[2 * nw:]
        x, y, c, _ = _place()
        cps = []
        for w in range(nw):
            hr = shards[w].shape[0] // 2
            for r0, rs in _row_chunks(hr, _row_bytes(shards[w])):
                piece = outs[w].at[pl.ds(c * hr + r0, rs)]
                _rcopy(piece, piece, ssem.at[w], rsem.at[w], (x, y, 1 - c)).start()
            mine = outs[w].at[pl.ds(c * hr, hr)]
            cps.append(_rcopy(mine, mine, ssem.at[w], rsem.at[w], (x, y, 1 - c)))
        for cp in cps:
            cp.wait()

    return pl.pallas_call(
        body, name="join_halves",
        in_specs=[ANY] * nw, out_specs=[ANY] * nw,
        out_shape=[jax.ShapeDtypeStruct(s.shape, F32) for s in shards],
        input_output_aliases={w: w for w in range(nw)},
        scratch_shapes=[pltpu.SemaphoreType.DMA((nw,)), pltpu.SemaphoreType.DMA((nw,))],
    )(*shards)


def _allreduce_small(packed):
    shape = packed.shape

    def body(p_ref, out_ref, rbuf, ssem, rsem):
        x, y, c, _ = _place()
        out_ref[...] = p_ref[...]
        for st, peer in enumerate([(x, y, 1 - c), (1 - x, y, c), (x, 1 - y, c)]):
            cp = _rcopy(out_ref, rbuf.at[st], ssem.at[st], rsem.at[st], peer)
            cp.start()
            cp.wait()
            out_ref[...] = out_ref[...] + rbuf[st]

    vm = pl.BlockSpec(memory_space=pltpu.VMEM)
    return pl.pallas_call(
        body, name="allreduce_small",
        in_specs=[vm], out_specs=vm, out_shape=jax.ShapeDtypeStruct(shape, F32),
        scratch_shapes=[pltpu.VMEM((3,) + shape, F32), pltpu.SemaphoreType.DMA((3,)), pltpu.SemaphoreType.DMA((3,))],
        compiler_params=pltpu.CompilerParams(vmem_limit_bytes=VMEM_LIMIT),
    )(packed)


def _adamw_math(w, g, m, v):
    m = ADAM_B1 * m + (1.0 - ADAM_B1) * g
    v = ADAM_B2 * v + (1.0 - ADAM_B2) * (g * g)
    delta = -ADAM_LR * ((m * ADAM_C1) / (jnp.sqrt(v * ADAM_C2) + ADAM_EPS) + ADAM_WD * w)
    return delta, m, v


def _adamw(ws, gs, ms, vs, steps, name):
    nw = len(ws)

    def body(*refs):
        for k in range(nw):
            d, m, v = _adamw_math(refs[k][...], refs[nw + k][...], refs[2 * nw + k][...], refs[3 * nw + k][...])
            refs[4 * nw + k][...] = d
            refs[5 * nw + k][...] = m
            refs[6 * nw + k][...] = v

    specs = [pl.BlockSpec((a.shape[0] // steps, a.shape[1]), lambda i: (i, 0)) for a in ws]
    shapes = [jax.ShapeDtypeStruct(a.shape, F32) for a in ws]
    outs = pl.pallas_call(
        body, name=name, grid=(steps,),
        in_specs=specs * 4, out_specs=specs * 3, out_shape=shapes * 3,
        compiler_params=_params("arbitrary"),
    )(*ws, *gs, *ms, *vs)
    return outs[:nw], outs[nw:2 * nw], outs[2 * nw:]


SMALL = ["norm_mix_g", "pool_w", "pool_b", "pool_scale", "conv_b", "gate_a_w", "gate_a_b", "gate_x_w", "gate_x_b",
         "lru_L", "norm_mlp_g", "norm_ple_g", "b_ple_gate", "norm_final_g"]
BIG = ["w_in", "w_out", "w_up", "w_down", "w_ple_gate", "w_ple_proj"]
ORDER = ["norm_mix_g", "w_in", "pool_w", "pool_b", "pool_scale", "conv_w", "conv_b", "gate_a_w", "gate_a_b", "gate_x_w",
         "gate_x_b", "lru_L", "w_out", "norm_mlp_g", "w_up", "w_down", "norm_ple_g", "w_ple_gate", "b_ple_gate",
         "w_ple_proj", "norm_final_g"]
LANES = 128


def _block_diag(w):
    eye = jnp.eye(LRU_HEADS, dtype=w.dtype)
    return jnp.einsum("hij,hk->hikj", w, eye).reshape(D_LRU, D_LRU)


def _diag_blocks(full):
    f = full.reshape(LRU_HEADS, LRU_BLOCK, LRU_HEADS, LRU_BLOCK)
    return jnp.stack([f[h, :, h, :] for h in range(LRU_HEADS)])


def _rows128(a):
    return a.reshape(-1, LANES)


def _pad8(a):
    r = (-a.shape[0]) % 8
    return jnp.pad(a, ((0, r), (0, 0))) if r else a


def kernel(x, p, norm_mix_g, w_in, pool_w, pool_b, pool_scale, conv_w, conv_b, gate_a_w, gate_a_b, gate_x_w, gate_x_b, lru_L, w_out, norm_mlp_g, w_up, w_down, norm_ple_g, w_ple_gate, b_ple_gate, w_ple_proj, norm_final_g, loss_target, m_norm_mix_g, m_w_in, m_pool_w, m_pool_b, m_pool_scale, m_conv_w, m_conv_b, m_gate_a_w, m_gate_a_b, m_gate_x_w, m_gate_x_b, m_lru_L, m_w_out, m_norm_mlp_g, m_w_up, m_w_down, m_norm_ple_g, m_w_ple_gate, m_b_ple_gate, m_w_ple_proj, m_norm_final_g, v_norm_mix_g, v_w_in, v_pool_w, v_pool_b, v_pool_scale, v_conv_w, v_conv_b, v_gate_a_w, v_gate_a_b, v_gate_x_w, v_gate_x_b, v_lru_L, v_w_out, v_norm_mlp_g, v_w_up, v_w_down, v_norm_ple_g, v_w_ple_gate, v_b_ple_gate, v_w_ple_proj, v_norm_final_g):
    W = dict(norm_mix_g=norm_mix_g, w_in=w_in, pool_w=pool_w, pool_b=pool_b, pool_scale=pool_scale, conv_w=conv_w,
             conv_b=conv_b, gate_a_w=gate_a_w, gate_a_b=gate_a_b, gate_x_w=gate_x_w, gate_x_b=gate_x_b, lru_L=lru_L,
             w_out=w_out, norm_mlp_g=norm_mlp_g, w_up=w_up, w_down=w_down, norm_ple_g=norm_ple_g,
             w_ple_gate=w_ple_gate, b_ple_gate=b_ple_gate, w_ple_proj=w_ple_proj, norm_final_g=norm_final_g)
    M = dict(norm_mix_g=m_norm_mix_g, w_in=m_w_in, pool_w=m_pool_w, pool_b=m_pool_b, pool_scale=m_pool_scale,
             conv_w=m_conv_w, conv_b=m_conv_b, gate_a_w=m_gate_a_w, gate_a_b=m_gate_a_b, gate_x_w=m_gate_x_w,
             gate_x_b=m_gate_x_b, lru_L=m_lru_L, w_out=m_w_out, norm_mlp_g=m_norm_mlp_g, w_up=m_w_up, w_down=m_w_down,
             norm_ple_g=m_norm_ple_g, w_ple_gate=m_w_ple_gate, b_ple_gate=m_b_ple_gate, w_ple_proj=m_w_ple_proj,
             norm_final_g=m_norm_final_g)
    V = dict(norm_mix_g=v_norm_mix_g, w_in=v_w_in, pool_w=v_pool_w, pool_b=v_pool_b, pool_scale=v_pool_scale,
             conv_w=v_conv_w, conv_b=v_conv_b, gate_a_w=v_gate_a_w, gate_a_b=v_gate_a_b, gate_x_w=v_gate_x_w,
             gate_x_b=v_gate_x_b, lru_L=v_lru_L, w_out=v_w_out, norm_mlp_g=v_norm_mlp_g, w_up=v_w_up, w_down=v_w_down,
             norm_ple_g=v_norm_ple_g, w_ple_gate=v_w_ple_gate, b_ple_gate=v_b_ple_gate, w_ple_proj=v_w_ple_proj,
             norm_final_g=v_norm_final_g)

    s_len = x.shape[1]
    tm_mix = min(256, s_len)
    tm = min(512, s_len)
    chip = (2 * lax.axis_index("x") + lax.axis_index("y")).astype(jnp.int32)
    pos = jnp.stack([chip, lax.axis_index("c").astype(jnp.int32)])

    shards = [w_in[0], w_out[0], w_up[0], w_down[0], w_ple_gate[0], w_ple_proj[0], jnp.pad(conv_w[0], ((0, 12), (0, 0)))]
    st_in, st_out, st_up, st_dn, st_pg, st_pp, st_cw = _stack_own(shards, [BF16] * 6 + [F32], pos, 8)
    win_g, wout_g, cw_g = _gather_weights([st_in, st_out, st_cw])
    wout_f = wout_g.reshape(D_MODEL, D_MODEL)
    cw_f = jnp.transpose(cw_g[:, :CONV_WIDTH], (1, 0, 2)).reshape(CONV_WIDTH, D_LRU)
    pw_b = pool_w[0].astype(BF16)
    wa_b = _block_diag(gate_a_w[0]).astype(BF16)
    wx_b = _block_diag(gate_x_w[0]).astype(BF16)
    pb_r = pool_b.reshape(1, D_POOL)
    ba_r = gate_a_b.reshape(1, D_LRU)
    bx_r = gate_x_b.reshape(1, D_LRU)
    g4 = norm_final_g.reshape(1, D_MODEL)
    mix_w = (pw_b, pb_r, pool_scale, cw_f, conv_b, wa_b, ba_r, wx_b, bx_r, lru_L, wout_f)

    xs, ps, ts = x[0], p[0, 0], loss_target[0]
    (z1, proj, hst, cat, h1), (wup_g, wdn_g, wpg_g, wpp_g) = _fwd_mix(
        xs, norm_mix_g, win_g, *mix_w, tm_mix, [st_up, st_dn, st_pg, st_pp])
    wpg_f = wpg_g.reshape(D_MODEL, D_MODEL)
    wpp_f = jnp.transpose(wpp_g, (1, 0, 2)).reshape(PLE_DIM, D_MODEL)
    z2, ru, h2 = _fwd_mlp(h1, norm_mlp_g, wup_g, wdn_g, tm)
    dh2, d_wpg, d_wpp, head_vec = _head(h2, ps, ts, norm_ple_g, wpg_f, b_ple_gate, wpp_f, g4, tm)
    dup, dh1, mlp_vec = _bwd_mlp_x(dh2, ru, h1, norm_mlp_g, wup_g, wdn_g, tm)
    d_wup, d_wdn = _bwd_mlp_w(z2, dup, ru, dh2, tm)
    early = [d_wup, d_wdn, d_wpg.reshape(N_CHIPS, D_MODEL // N_CHIPS, D_MODEL), d_wpp]
    pair_e = _add_pairs(early, _swap_halves(early, "early"), pos, 8, "early")
    (dproj, d_wout, d_pw, d_wa, d_wx, mix_vec), got_e = _bwd_mix(dh1, proj, hst, cat, *mix_w, tm_mix, pair_e)
    dx, d_win, in_vec = _bwd_in(dproj, z1, xs, dh1, norm_mix_g, win_g, tm)

    last = [d_win, d_wout.reshape(N_CHIPS, D_MODEL // N_CHIPS, D_MODEL)]
    pair_l = _add_pairs(last, _swap_halves(last, "last"), pos, 8, "last")
    got_l = _scatter_chips(pair_l)
    g_big = _join_halves(_sum_chips(pair_l + pair_e, got_l + got_e, pos, 8))

    g_small = {
        "norm_mix_g": in_vec[0:1], "pool_w": d_pw, "pool_b": mix_vec[0:1], "pool_scale": mix_vec[1:2],
        "conv_b": mix_vec[2:3], "gate_a_w": _diag_blocks(d_wa), "gate_a_b": mix_vec[3:4],
        "gate_x_w": _diag_blocks(d_wx), "gate_x_b": mix_vec[4:5], "lru_L": mix_vec[5:6], "norm_mlp_g": mlp_vec[0:1],
        "norm_ple_g": head_vec[0:1], "b_ple_gate": head_vec[2:3], "norm_final_g": head_vec[1:2],
    }
    d_cw = jnp.transpose(mix_vec[8:8 + CONV_WIDTH].reshape(CONV_WIDTH, N_CHIPS, LANES), (1, 0, 2)).reshape(-1, LANES)
    pieces = [_pad8(_rows128(g_small[k])) for k in SMALL] + [d_cw, _pad8(head_vec[3:4, :LANES])]
    offs = [0]
    for pc in pieces:
        offs.append(offs[-1] + pc.shape[0])
    red = _allreduce_small(jnp.concatenate(pieces, axis=0))
    loss = red[offs[-2], 0]
    g_cw = lax.dynamic_slice(red, (offs[len(SMALL)] + CONV_WIDTH * chip, 0), (CONV_WIDTH, LANES))

    def packed(src):
        return jnp.concatenate([_pad8(_rows128(src[k])) for k in SMALL] + [_pad8(src["conv_w"][0])], axis=0)

    n_small = offs[len(SMALL)]
    g_pack = jnp.concatenate([red[:n_small], _pad8(g_cw)], axis=0)
    (d_pack,), (m_pack,), (v_pack,) = _adamw([packed(W)], [g_pack], [packed(M)], [packed(V)], 1, "adamw_small")

    big2d = lambda src: [src[k][0] for k in BIG]
    d_big, m_big, v_big = _adamw(big2d(W), g_big, big2d(M), big2d(V), 8, "adamw_big")

    def unpack(pack, big_list):
        out = {}
        for idx, k in enumerate(SMALL):
            n_el = W[k].size
            out[k] = pack[offs[idx]:offs[idx + 1]].reshape(-1)[:n_el].reshape(W[k].shape)
        out["conv_w"] = pack[n_small:n_small + CONV_WIDTH].reshape(W["conv_w"].shape)
        for k, a in zip(BIG, big_list):
            out[k] = a.reshape(W[k].shape)
        return out

    grads = unpack(g_pack, g_big)
    deltas = unpack(d_pack, d_big)
    new_m = unpack(m_pack, m_big)
    new_v = unpack(v_pack, v_big)
    return (loss, dx[None], *[grads[k] for k in ORDER], *[deltas[k] for k in ORDER],
            *[new_m[k] for k in ORDER], *[new_v[k] for k in ORDER])
```

```python
import functools

import jax
import jax.numpy as jnp
from jax import lax
from jax.experimental import pallas as pl
from jax.experimental.pallas import tpu as pltpu

F32 = jnp.float32
BF16 = jnp.bfloat16
MESH = pl.DeviceIdType.MESH

D_MODEL = 1024
D_POOL = 512
D_LRU = 512
POOL_WINDOWS = (2, 4, 8, 16)
POOL_GROUP = 128
POOL_HALO = 16
CONV_WIDTH = 4
CONV_HALO = 8
PASS_STEPS = 4
LRU_HEADS = 8
LRU_BLOCK = 64
LRU_C = 8.0
D_FF = 4096
PLE_DIM = 256
D_IN_PROJ = 1536
RMS_EPS = 1e-6
N_CHIPS = 4
FF_BLOCK = D_FF // N_CHIPS

ADAM_LR = 0.001
ADAM_B1 = 0.9
ADAM_B2 = 0.999
ADAM_EPS = 1e-08
ADAM_WD = 0.01
ADAM_STEP = 10
ADAM_C1 = 1.0 / (1.0 - ADAM_B1 ** ADAM_STEP)
ADAM_C2 = 1.0 / (1.0 - ADAM_B2 ** ADAM_STEP)

VMEM_LIMIT = 56 * 1024 * 1024
GELU_C = 0.7978845608028654
GELU_A = 0.044715

NT = (((1,), (1,)), ((), ()))
TN = (((0,), (0,)), ((), ()))


def _dot(a, b):
    return jnp.dot(a, b, preferred_element_type=F32)


def _dot_nt(a, b):
    return lax.dot_general(a, b, NT, preferred_element_type=F32)


def _dot_tn(a, b):
    return lax.dot_general(a, b, TN, preferred_element_type=F32)


def _params(*sem):
    return pltpu.CompilerParams(dimension_semantics=sem, vmem_limit_bytes=VMEM_LIMIT)


def _full(shape):
    nd = len(shape)
    return pl.BlockSpec(shape, lambda *_: (0,) * nd)


def _resident(shape):
    nd = len(shape)
    return pl.BlockSpec(shape, lambda *_: (0,) * nd, pipeline_mode=pl.Buffered(1))


def _rstd(x):
    return lax.rsqrt(jnp.mean(x * x, axis=-1, keepdims=True) + RMS_EPS)


def _rms_bwd(x, g, dz):
    xr = x * _rstd(x)
    r = _rstd(x)
    dyg = dz * g
    dx = r * (dyg - xr * jnp.mean(dyg * xr, axis=-1, keepdims=True))
    dg = jnp.sum(dz * xr, axis=0, keepdims=True)
    return dx, dg


def _sigmoid(x):
    return 1.0 / (1.0 + jnp.exp(-x))


def _log_sigmoid(v):
    u = jnp.exp(-jnp.abs(v))
    w = 1.0 + u
    l1p = jnp.where(w == 1.0, u, jnp.log(w) * u / jnp.where(w == 1.0, 1.0, w - 1.0))
    return jnp.minimum(v, 0.0) - l1p


def _gelu(x):
    t = jnp.tanh(GELU_C * (x + GELU_A * x * x * x))
    return 0.5 * x * (1.0 + t), t


def _gelu_grad(x, t):
    return 0.5 * (1.0 + t) + 0.5 * x * (1.0 - t * t) * GELU_C * (1.0 + 3.0 * GELU_A * x * x)


def _rows(shape, t0):
    return lax.broadcasted_iota(jnp.int32, shape, 0) + t0


def _pool_diff(u_pool, prev, t0):
    tm = u_pool.shape[0]
    rows = _rows((tm, POOL_GROUP), t0)
    outs, invs = [], []
    for g, w in enumerate(POOL_WINDOWS):
        sl = slice(POOL_GROUP * g, POOL_GROUP * (g + 1))
        ug = u_pool[:, sl]
        s = jnp.concatenate([prev[:, sl], ug], axis=0)
        k = 1
        while k < w:
            s = s + pltpu.roll(s, k, 0)
            k *= 2
        inv = 1.0 / jnp.minimum(rows + 1, w).astype(F32)
        outs.append(s[POOL_HALO:] * inv - ug)
        invs.append(inv)
    return jnp.concatenate(outs, axis=1), jnp.concatenate(invs, axis=1)


def _pool_diff_bwd(dd, inv, nxt):
    ddc = dd * inv
    outs = []
    for g, w in enumerate(POOL_WINDOWS):
        sl = slice(POOL_GROUP * g, POOL_GROUP * (g + 1))
        s = jnp.concatenate([ddc[:, sl], nxt[:, sl]], axis=0)
        n = s.shape[0]
        k = 1
        while k < w:
            s = s + pltpu.roll(s, n - k, 0)
            k *= 2
        outs.append(s[:n - POOL_HALO] - dd[:, sl])
    return jnp.concatenate(outs, axis=1), ddc


def _conv_taps(u, prev):
    ext = jnp.concatenate([prev, u], axis=0)
    return [pltpu.roll(ext, CONV_WIDTH - 1 - k, 0)[CONV_HALO:] if k < CONV_WIDTH - 1 else u for k in range(CONV_WIDTH)]


def _scan_fwd(a, b):
    tm = a.shape[0]
    rows = _rows(a.shape, 0)
    k = 1
    while k < tm:
        ar = pltpu.roll(a, k, 0)
        br = pltpu.roll(b, k, 0)
        m = rows >= k
        b = jnp.where(m, a * br + b, b)
        a = jnp.where(m, a * ar, a)
        k *= 2
    return a, b


def _scan_rev(a, b):
    tm = a.shape[0]
    rows = _rows(a.shape, 0)
    k = 1
    while k < tm:
        ar = pltpu.roll(a, tm - k, 0)
        br = pltpu.roll(b, tm - k, 0)
        m = rows < tm - k
        b = jnp.where(m, a * br + b, b)
        a = jnp.where(m, a * ar, a)
        k *= 2
    return a, b


def _lru_gates(xb, wa, ba, wx, bx, lsl8, t0):
    xbb = xb.astype(BF16)
    r = _sigmoid(_dot(xbb, wa) + ba)
    ig = _sigmoid(_dot(xbb, wx) + bx)
    a = jnp.exp(r * lsl8)
    first = _rows(xb.shape, t0) == 0
    mult = jnp.where(first, 1.0, jnp.sqrt(1.0 - a * a))
    return r, ig, a, mult, first


def _fwd_mix(x, g1, w_in, pool_w, pool_b, pool_scale, conv_w, conv_b, wa, ba, wx, bx, lru_l, w_out, tm, late):
    s_len = x.shape[0]
    n = s_len // tm
    nl = len(late)

    def body(x_ref, g1_ref, win_ref, pw_ref, pb_ref, ps_ref, cw_ref, cb_ref, wa_ref, ba_ref, wx_ref, bx_ref, l_ref,
             wout_ref, *rest):
        z1_ref, proj_ref, h_ref, cat_ref, h1_ref = rest[nl:nl + 5]
        late_ref = rest[nl + 5:2 * nl + 5]
        cpool, clru, ch, ssem, rsem = rest[2 * nl + 5:]
        i = pl.program_id(0)

        @pl.when(i == 0)
        def _():
            _gather_send(late_ref, late, ssem, rsem)
            cpool[...] = jnp.zeros_like(cpool)
            clru[...] = jnp.zeros_like(clru)
            ch[...] = jnp.zeros_like(ch)

        t0 = i * tm
        xv = x_ref[...]
        zb = (xv * _rstd(xv) * g1_ref[...]).astype(BF16)
        z1_ref[...] = zb
        proj = jnp.concatenate([_dot(zb, win_ref[j]) for j in range(N_CHIPS)], axis=1)
        proj_ref[...] = proj
        u_pool = proj[:, :D_POOL]
        u_lru = proj[:, D_POOL:D_POOL + D_LRU]
        u_gate = proj[:, D_POOL + D_LRU:]

        d, _ = _pool_diff(u_pool, cpool[...], t0)
        cpool[...] = u_pool[tm - POOL_HALO:]
        db = d.astype(BF16)
        yp = jnp.concatenate(
            [_dot(db[:, POOL_GROUP * g:POOL_GROUP * (g + 1)], pw_ref[g]) for g in range(len(POOL_WINDOWS))], axis=1)
        y_pool = (yp + pb_ref[...]) * ps_ref[...]

        taps = _conv_taps(u_lru, clru[...])
        clru[...] = u_lru[tm - CONV_HALO:]
        xb = cb_ref[...]
        for k in range(CONV_WIDTH):
            xb = xb + taps[k] * cw_ref[k:k + 1, :]
        lsl8 = LRU_C * _log_sigmoid(l_ref[...])
        _, ig, a, mult, _ = _lru_gates(xb, wa_ref[...], ba_ref[...], wx_ref[...], bx_ref[...], lsl8, t0)
        pa, hb = _scan_fwd(a, mult * (ig * xb))
        h = hb + pa * ch[7:8, :]
        ch[...] = h[tm - 8:]
        h_ref[...] = h
        gl, _ = _gelu(u_gate)
        cat = jnp.concatenate([y_pool, h * gl], axis=1).astype(BF16)
        cat_ref[...] = cat
        h1_ref[...] = xv + _dot(cat, wout_ref[...])

        @pl.when(i == max(n - PASS_STEPS, 0))
        def _():
            _gather_pass(late_ref, late, ssem, rsem)

        @pl.when(i == n - 1)
        def _():
            _gather_done(late_ref, late, ssem, rsem)

    row = lambda w: pl.BlockSpec((tm, w), lambda i: (i, 0))
    ins = [x, g1, w_in, pool_w, pool_b, pool_scale, conv_w, conv_b, wa, ba, wx, bx, lru_l, w_out]
    outs = pl.pallas_call(
        body, name="fwd_mix", grid=(n,),
        in_specs=[row(D_MODEL)] + [_full(a.shape) for a in ins[1:]] + [ANY] * nl,
        out_specs=[row(D_MODEL), row(D_IN_PROJ), row(D_LRU), row(D_MODEL), row(D_MODEL)] + [ANY] * nl,
        out_shape=[jax.ShapeDtypeStruct((s_len, D_MODEL), BF16), jax.ShapeDtypeStruct((s_len, D_IN_PROJ), F32),
                   jax.ShapeDtypeStruct((s_len, D_LRU), F32), jax.ShapeDtypeStruct((s_len, D_MODEL), BF16),
                   jax.ShapeDtypeStruct((s_len, D_MODEL), F32)]
        + [jax.ShapeDtypeStruct(a.shape, a.dtype) for a in late],
        input_output_aliases={len(ins) + k: 5 + k for k in range(nl)},
        scratch_shapes=[pltpu.VMEM((POOL_HALO, D_POOL), F32), pltpu.VMEM((CONV_HALO, D_LRU), F32),
                        pltpu.VMEM((8, D_LRU), F32)] + _gather_sems(nl),
        compiler_params=_params("arbitrary"),
    )(*ins, *late)
    return outs[:5], outs[5:]


def _fwd_mlp(h1, g2, w_up, w_down, tm):
    s_len = h1.shape[0]
    n = s_len // tm

    def body(h1_ref, g2_ref, wup_ref, wdn_ref, z2_ref, ru_ref, h2_ref):
        hv = h1_ref[...]
        zb = (hv * _rstd(hv) * g2_ref[...]).astype(BF16)
        z2_ref[...] = zb
        acc = hv
        for j in range(N_CHIPS):
            ru = jnp.maximum(_dot(zb, wup_ref[j]), 0.0)
            ru_ref[:, FF_BLOCK * j:FF_BLOCK * (j + 1)] = ru.astype(BF16)
            acc = acc + _dot((ru * ru).astype(BF16), wdn_ref[j])
        h2_ref[...] = acc

    row = lambda w: pl.BlockSpec((tm, w), lambda i: (i, 0))
    return pl.pallas_call(
        body, name="fwd_mlp", grid=(n,),
        in_specs=[row(D_MODEL), _full(g2.shape), _resident(w_up.shape), _resident(w_down.shape)],
        out_specs=[row(D_MODEL), row(D_FF), row(D_MODEL)],
        out_shape=[jax.ShapeDtypeStruct((s_len, D_MODEL), BF16), jax.ShapeDtypeStruct((s_len, D_FF), BF16),
                   jax.ShapeDtypeStruct((s_len, D_MODEL), F32)],
        compiler_params=_params("arbitrary"),
    )(h1, g2, w_up, w_down)


def _head(h2, p, target, g3, w_pg, b_pg, w_pp, g4, tm):
    s_len = h2.shape[0]
    n = s_len // tm

    def body(h2_ref, p_ref, t_ref, g3_ref, wpg_ref, bpg_ref, wpp_ref, g4_ref,
             dh2_ref, dh2b_ref, dwpg_ref, dwpp_ref, vec_ref, a_pg, a_pp, a_vec):
        i = pl.program_id(0)

        @pl.when(i == 0)
        def _():
            a_pg[...] = jnp.zeros_like(a_pg)
            a_pp[...] = jnp.zeros_like(a_pp)
            a_vec[...] = jnp.zeros_like(a_vec)

        h2v = h2_ref[...]
        g3v = g3_ref[...]
        g4v = g4_ref[...]
        z3 = (h2v * _rstd(h2v) * g3v).astype(BF16)
        gate = _sigmoid(_dot(z3, wpg_ref[...]) + bpg_ref[...])
        pb = p_ref[...].astype(BF16)
        pp = _dot(pb, wpp_ref[...])
        h3 = h2v + gate * pp
        r4 = _rstd(h3)
        diff = h3 * r4 * g4v - t_ref[...]
        loss = 0.5 * jnp.sum(jnp.mean(diff * diff, axis=-1, keepdims=True), axis=0, keepdims=True)
        dy = diff * (1.0 / D_MODEL)
        dh3, dg4 = _rms_bwd(h3, g4v, dy)
        dpp = (dh3 * gate).astype(BF16)
        dpre = dh3 * pp * gate * (1.0 - gate)
        dpreb = dpre.astype(BF16)
        dz3 = _dot_nt(dpreb, wpg_ref[...])
        dx, dg3 = _rms_bwd(h2v, g3v, dz3)
        dh2 = dh3 + dx
        dh2_ref[...] = dh2
        dh2b_ref[...] = dh2.astype(BF16)
        a_pg[...] += _dot_tn(z3, dpreb)
        a_pp[...] += _dot_tn(pb, dpp)
        a_vec[0:1, :] += dg3
        a_vec[1:2, :] += dg4
        a_vec[2:3, :] += jnp.sum(dpre, axis=0, keepdims=True)
        a_vec[3:4, :] += jnp.broadcast_to(loss, (1, D_MODEL))

        @pl.when(i == n - 1)
        def _():
            dwpg_ref[...] = a_pg[...].astype(BF16)
            for j in range(N_CHIPS):
                dwpp_ref[j] = a_pp[:, PLE_DIM * j:PLE_DIM * (j + 1)].astype(BF16)
            vec_ref[...] = a_vec[...]

    row = lambda w: pl.BlockSpec((tm, w), lambda i: (i, 0))
    ins = [h2, p, target, g3, w_pg, b_pg, w_pp, g4]
    return pl.pallas_call(
        body, name="head", grid=(n,),
        in_specs=[row(D_MODEL), row(PLE_DIM), row(D_MODEL)] + [_resident(a.shape) for a in ins[3:]],
        out_specs=[row(D_MODEL), row(D_MODEL), _full((D_MODEL, D_MODEL)), _full((N_CHIPS, PLE_DIM, PLE_DIM)),
                   _full((8, D_MODEL))],
        out_shape=[jax.ShapeDtypeStruct((s_len, D_MODEL), F32), jax.ShapeDtypeStruct((s_len, D_MODEL), BF16),
                   jax.ShapeDtypeStruct((D_MODEL, D_MODEL), BF16),
                   jax.ShapeDtypeStruct((N_CHIPS, PLE_DIM, PLE_DIM), BF16), jax.ShapeDtypeStruct((8, D_MODEL), F32)],
        scratch_shapes=[pltpu.VMEM((D_MODEL, D_MODEL), F32), pltpu.VMEM((PLE_DIM, D_MODEL), F32),
                        pltpu.VMEM((8, D_MODEL), F32)],
        compiler_params=_params("arbitrary"),
    )(*ins)


def _bwd_mlp_x(dh2, ru, h1, g2, w_up, w_down, tm):
    s_len = dh2.shape[0]
    n = s_len // tm

    def body(dh2_ref, ru_ref, h1_ref, g2_ref, wup_ref, wdn_ref, dup_ref, dh1_ref, dg2_ref, a_g):
        i = pl.program_id(0)

        @pl.when(i == 0)
        def _():
            a_g[...] = jnp.zeros_like(a_g)

        dh2v = dh2_ref[...]
        dhb = dh2v.astype(BF16)
        acc = jnp.zeros((tm, D_MODEL), F32)
        for j in range(N_CHIPS):
            sl = slice(FF_BLOCK * j, FF_BLOCK * (j + 1))
            dup = (_dot_nt(dhb, wdn_ref[j]) * (2.0 * ru_ref[:, sl].astype(F32))).astype(BF16)
            dup_ref[:, sl] = dup
            acc = acc + _dot_nt(dup, wup_ref[j])
        dx, dg = _rms_bwd(h1_ref[...], g2_ref[...], acc)
        dh1_ref[...] = dh2v + dx
        a_g[0:1, :] += dg

        @pl.when(i == n - 1)
        def _():
            dg2_ref[...] = a_g[...]

    row = lambda w: pl.BlockSpec((tm, w), lambda i: (i, 0))
    return pl.pallas_call(
        body, name="bwd_mlp_x", grid=(n,),
        in_specs=[row(D_MODEL), row(D_FF), row(D_MODEL), _full(g2.shape), _resident(w_up.shape), _resident(w_down.shape)],
        out_specs=[row(D_FF), row(D_MODEL), _full((8, D_MODEL))],
        out_shape=[jax.ShapeDtypeStruct((s_len, D_FF), BF16), jax.ShapeDtypeStruct((s_len, D_MODEL), F32),
                   jax.ShapeDtypeStruct((8, D_MODEL), F32)],
        scratch_shapes=[pltpu.VMEM((8, D_MODEL), F32)],
        compiler_params=_params("arbitrary"),
    )(dh2, ru, h1, g2, w_up, w_down)


def _bwd_mlp_w(z2, dup, ru, dh2, tk):
    s_len = z2.shape[0]
    n = s_len // tk

    def body(z2_ref, dup_ref, ru_ref, dh2_ref, dwup_ref, dwdn_ref, a_up, a_dn):
        t = pl.program_id(1)

        @pl.when(t == 0)
        def _():
            a_up[...] = jnp.zeros_like(a_up)
            a_dn[...] = jnp.zeros_like(a_dn)

        ruv = ru_ref[...]
        a_up[...] += _dot_tn(z2_ref[...], dup_ref[...])
        a_dn[...] += _dot_tn(ruv * ruv, dh2_ref[...])

        @pl.when(t == n - 1)
        def _():
            dwup_ref[0] = a_up[...].astype(BF16)
            dwdn_ref[0] = a_dn[...].astype(BF16)

    tile = pl.BlockSpec((tk, D_MODEL), lambda j, t: (t, 0))
    ffb = pl.BlockSpec((tk, FF_BLOCK), lambda j, t: (t, j))
    return pl.pallas_call(
        body, name="bwd_mlp_w", grid=(N_CHIPS, n),
        in_specs=[tile, ffb, ffb, tile],
        out_specs=[pl.BlockSpec((1, D_MODEL, FF_BLOCK), lambda j, t: (j, 0, 0)),
                   pl.BlockSpec((1, FF_BLOCK, D_MODEL), lambda j, t: (j, 0, 0))],
        out_shape=[jax.ShapeDtypeStruct((N_CHIPS, D_MODEL, FF_BLOCK), BF16),
                   jax.ShapeDtypeStruct((N_CHIPS, FF_BLOCK, D_MODEL), BF16)],
        scratch_shapes=[pltpu.VMEM((D_MODEL, FF_BLOCK), F32), pltpu.VMEM((FF_BLOCK, D_MODEL), F32)],
        compiler_params=_params("arbitrary", "arbitrary"),
    )(z2, dup, ru, dh2)


MIX_VEC_ROWS = 16


def _bwd_mix(dh1, proj, h, cat, pool_w, pool_b, pool_scale, conv_w, conv_b, wa, ba, wx, bx, lru_l, w_out, tm, early):
    s_len = dh1.shape[0]
    n = s_len // tm
    ng = len(POOL_WINDOWS)
    ne_ = len(early)

    def body(dh1_ref, proj_ref, h_ref, cat_ref, projh_ref, hh_ref, pw_ref, pb_ref, ps_ref, cw_ref, cb_ref,
             wa_ref, ba_ref, wx_ref, bx_ref, l_ref, wout_ref, *rest):
        early_ref = rest[:ne_]
        dproj_ref, dwout_ref, dpw_ref, dwa_ref, dwx_ref, vec_ref = rest[ne_:ne_ + 6]
        got_ref = rest[ne_ + 6:2 * ne_ + 6]
        a_out, a_pw, a_wa, a_wx, a_vec, c_g, c_dxb, c_ddc, ssem, rsem = rest[2 * ne_ + 6:]
        q = pl.program_id(0)
        i = n - 1 - q

        @pl.when(q == 0)
        def _():
            _scatter_send(early_ref, got_ref, early, ssem, rsem)
            for r in (a_out, a_pw, a_wa, a_wx, a_vec, c_g, c_dxb, c_ddc):
                r[...] = jnp.zeros_like(r)

        t0 = i * tm
        has_prev = (i > 0).astype(F32)
        dh1b = dh1_ref[...].astype(BF16)
        dcat = _dot_nt(dh1b, wout_ref[...])
        a_out[...] += _dot_tn(cat_ref[...], dh1b)
        dy_pool = dcat[:, :D_POOL]
        dy_lru = dcat[:, D_POOL:]

        proj = proj_ref[...]
        u_pool = proj[:, :D_POOL]
        u_lru = proj[:, D_POOL:D_POOL + D_LRU]
        u_gate = proj[:, D_POOL + D_LRU:]
        halo = projh_ref[...] * has_prev

        d, inv = _pool_diff(u_pool, halo[:, :D_POOL], t0)
        db = d.astype(BF16)
        ypre = jnp.concatenate(
            [_dot(db[:, POOL_GROUP * g:POOL_GROUP * (g + 1)], pw_ref[g]) for g in range(ng)], axis=1) + pb_ref[...]
        dyp = dy_pool * ps_ref[...]
        dypb = dyp.astype(BF16)
        dds = []
        for g in range(ng):
            sl = slice(POOL_GROUP * g, POOL_GROUP * (g + 1))
            a_pw[g] += _dot_tn(db[:, sl], dypb[:, sl])
            dds.append(_dot_nt(dypb[:, sl], pw_ref[g]))
        du_pool, ddc = _pool_diff_bwd(jnp.concatenate(dds, axis=1), inv, c_ddc[...])
        c_ddc[...] = ddc[:POOL_HALO]
        a_vec[0:1, :] += jnp.sum(dyp, axis=0, keepdims=True)
        a_vec[1:2, :] += jnp.sum(dy_pool * ypre, axis=0, keepdims=True)

        taps = _conv_taps(u_lru, halo[POOL_HALO - CONV_HALO:, D_POOL:D_POOL + D_LRU])
        xb = cb_ref[...]
        for k in range(CONV_WIDTH):
            xb = xb + taps[k] * cw_ref[k:k + 1, :]
        lv = l_ref[...]
        lsl8 = LRU_C * _log_sigmoid(lv)
        r, ig, a, mult, first = _lru_gates(xb, wa_ref[...], ba_ref[...], wx_ref[...], bx_ref[...], lsl8, t0)
        hv = h_ref[...]
        gl, th = _gelu(u_gate)
        du_gate = dy_lru * hv * _gelu_grad(u_gate, th)
        last = _rows(a.shape, 0) == tm - 1
        a_next = jnp.where(last, 1.0, pltpu.roll(a, tm - 1, 0))
        pa, gb = _scan_rev(a_next, dy_lru * gl)
        gh = gb + pa * c_g[0:1, :]
        c_g[...] = jnp.broadcast_to(a[0:1, :] * gh[0:1, :], c_g.shape)
        h_prev_row = hh_ref[7:8, :] * has_prev
        h_prev = jnp.where(_rows(hv.shape, 0) == 0, h_prev_row, pltpu.roll(hv, 1, 0))
        gix = gh * ig * xb
        dla = gh * h_prev * a - jnp.where(first, 0.0, gix * a * a / mult)
        dpre_r = dla * lsl8 * r * (1.0 - r)
        dpre_i = gh * mult * xb * ig * (1.0 - ig)
        dprb = dpre_r.astype(BF16)
        dpib = dpre_i.astype(BF16)
        xbb = xb.astype(BF16)
        a_wa[...] += _dot_tn(xbb, dprb)
        a_wx[...] += _dot_tn(xbb, dpib)
        dxb = gh * mult * ig + _dot_nt(dprb, wa_ref[...]) + _dot_nt(dpib, wx_ref[...])
        a_vec[2:3, :] += jnp.sum(dxb, axis=0, keepdims=True)
        a_vec[3:4, :] += jnp.sum(dpre_r, axis=0, keepdims=True)
        a_vec[4:5, :] += jnp.sum(dpre_i, axis=0, keepdims=True)
        a_vec[5:6, :] += jnp.sum(dla * r, axis=0, keepdims=True)
        ext = jnp.concatenate([dxb, c_dxb[...]], axis=0)
        c_dxb[...] = dxb[:CONV_HALO]
        ne = tm + CONV_HALO
        du_lru = dxb * cw_ref[CONV_WIDTH - 1:CONV_WIDTH, :]
        for k in range(CONV_WIDTH):
            a_vec[8 + k:9 + k, :] += jnp.sum(dxb * taps[k], axis=0, keepdims=True)
            if k < CONV_WIDTH - 1:
                du_lru = du_lru + pltpu.roll(ext, ne - (CONV_WIDTH - 1 - k), 0)[:tm] * cw_ref[k:k + 1, :]
        dproj_ref[...] = jnp.concatenate([du_pool, du_lru, du_gate], axis=1).astype(BF16)

        @pl.when(q == n - 1)
        def _():
            dwout_ref[...] = a_out[...].astype(BF16)
            dpw_ref[...] = a_pw[...]
            dwa_ref[...] = a_wa[...]
            dwx_ref[...] = a_wx[...]
            vec_ref[...] = a_vec[...]
            vec_ref[5:6, :] = a_vec[5:6, :] * (LRU_C * _sigmoid(-lv))
            _scatter_done(got_ref, early, ssem, rsem)

    rev =lambda w: pl.BlockSpec((tm, w), lambda q: (n - 1 - q, 0))
    halo_p = pl.BlockSpec((POOL_HALO, D_IN_PROJ), lambda q: (jnp.maximum((n - 1 - q) * (tm // POOL_HALO) - 1, 0), 0))
    halo_h = pl.BlockSpec((8, D_LRU), lambda q: (jnp.maximum((n - 1 - q) * (tm // 8) - 1, 0), 0))
    wts = [pool_w, pool_b, pool_scale, conv_w, conv_b, wa, ba, wx, bx, lru_l, w_out]
    outs = pl.pallas_call(
        body, name="bwd_mix", grid=(n,),
        in_specs=[rev(D_MODEL), rev(D_IN_PROJ), rev(D_LRU), rev(D_MODEL), halo_p, halo_h] + [_full(a.shape) for a in wts]
        + [ANY] * ne_,
        out_specs=[rev(D_IN_PROJ), _full((D_MODEL, D_MODEL)), _full((ng, POOL_GROUP, POOL_GROUP)),
                   _full((D_LRU, D_LRU)), _full((D_LRU, D_LRU)), _full((MIX_VEC_ROWS, D_LRU))] + [ANY] * ne_,
        out_shape=[jax.ShapeDtypeStruct((s_len, D_IN_PROJ), BF16), jax.ShapeDtypeStruct((D_MODEL, D_MODEL), BF16),
                   jax.ShapeDtypeStruct((ng, POOL_GROUP, POOL_GROUP), F32), jax.ShapeDtypeStruct((D_LRU, D_LRU), F32),
                   jax.ShapeDtypeStruct((D_LRU, D_LRU), F32), jax.ShapeDtypeStruct((MIX_VEC_ROWS, D_LRU), F32)]
        + _scatter_shapes(early),
        scratch_shapes=[pltpu.VMEM((D_MODEL, D_MODEL), F32), pltpu.VMEM((ng, POOL_GROUP, POOL_GROUP), F32),
                        pltpu.VMEM((D_LRU, D_LRU), F32), pltpu.VMEM((D_LRU, D_LRU), F32),
                        pltpu.VMEM((MIX_VEC_ROWS, D_LRU), F32), pltpu.VMEM((8, D_LRU), F32),
                        pltpu.VMEM((CONV_HALO, D_LRU), F32), pltpu.VMEM((POOL_HALO, D_POOL), F32)] + _scatter_sems(ne_),
        compiler_params=_params("arbitrary"),
    )(dh1, proj, h, cat, proj, h, *wts, *early)
    return outs[:6], outs[6:]


def _bwd_in(dproj, z1, x, dh1, g1, w_in, tm):
    s_len = x.shape[0]
    n = s_len // tm
    cb = D_IN_PROJ // N_CHIPS

    def body(dp_ref, z1_ref, x_ref, dh1_ref, g1_ref, win_ref, dx_ref, dwin_ref, dg1_ref, a_w, a_g):
        i = pl.program_id(0)

        @pl.when(i == 0)
        def _():
            a_w[...] = jnp.zeros_like(a_w)
            a_g[...] = jnp.zeros_like(a_g)

        dp = dp_ref[...]
        zb = z1_ref[...]
        dz = jnp.zeros((tm, D_MODEL), F32)
        for j in range(N_CHIPS):
            dpj = dp[:, cb * j:cb * (j + 1)]
            dz = dz + _dot_nt(dpj, win_ref[j])
            a_w[j] += _dot_tn(zb, dpj)
        dx, dg = _rms_bwd(x_ref[...], g1_ref[...], dz)
        dx_ref[...] = dh1_ref[...] + dx
        a_g[0:1, :] += dg

        @pl.when(i == n - 1)
        def _():
            dwin_ref[...] = a_w[...].astype(BF16)
            dg1_ref[...] = a_g[...]

    row = lambda w: pl.BlockSpec((tm, w), lambda i: (i, 0))
    return pl.pallas_call(
        body, name="bwd_in", grid=(n,),
        in_specs=[row(D_IN_PROJ), row(D_MODEL), row(D_MODEL), row(D_MODEL), _resident(g1.shape), _resident(w_in.shape)],
        out_specs=[row(D_MODEL), _full(w_in.shape), _full((8, D_MODEL))],
        out_shape=[jax.ShapeDtypeStruct((s_len, D_MODEL), F32), jax.ShapeDtypeStruct(w_in.shape, BF16),
                   jax.ShapeDtypeStruct((8, D_MODEL), F32)],
        scratch_shapes=[pltpu.VMEM(w_in.shape, F32), pltpu.VMEM((8, D_MODEL), F32)],
        compiler_params=_params("arbitrary"),
    )(dproj, z1, x, dh1, g1, w_in)


def _place():
    x, y, c = lax.axis_index("x"), lax.axis_index("y"), lax.axis_index("c")
    chips = [(1 - x, y), (x, 1 - y), (1 - x, 1 - y)]
    return x, y, c, chips


def _rcopy(src, dst, ssem, rsem, dev):
    return pltpu.make_async_remote_copy(src_ref=src, dst_ref=dst, send_sem=ssem, recv_sem=rsem,
                                        device_id=dev, device_id_type=MESH)


ANY = pl.BlockSpec(memory_space=pl.ANY)
COPY_CHUNK_BYTES = 128 * 1024
ROW_ALIGN = 16


def _row_chunks(rows, row_bytes):
    per = max(ROW_ALIGN, (COPY_CHUNK_BYTES // row_bytes) // ROW_ALIGN * ROW_ALIGN)
    return [(r0, min(per, rows - r0)) for r0 in range(0, rows, per)]


def _row_bytes(a):
    return a.shape[-1] * jnp.dtype(a.dtype).itemsize


def _stack_own(shards, dtypes, pos, steps):
    nw = len(shards)

    def body(pos_ref, *refs):
        for w in range(nw):
            refs[nw + w][0] = refs[w][...].astype(dtypes[w])

    def split(s):
        return s.shape[0] % (steps * ROW_ALIGN) == 0

    ins = [pl.BlockSpec((s.shape[0] // steps, s.shape[1]), lambda i, p: (i, 0)) if split(s)
           else pl.BlockSpec(s.shape, lambda i, p: (0, 0)) for s in shards]
    outs = [pl.BlockSpec((1, s.shape[0] // steps, s.shape[1]), lambda i, p: (p[0], i, 0)) if split(s)
            else pl.BlockSpec((1,) + s.shape, lambda i, p: (p[0], 0, 0)) for s in shards]
    return pl.pallas_call(
        body, name="stack_own",
        grid_spec=pltpu.PrefetchScalarGridSpec(num_scalar_prefetch=1, grid=(steps,), in_specs=ins, out_specs=outs),
        out_shape=[jax.ShapeDtypeStruct((N_CHIPS,) + s.shape, d) for s, d in zip(shards, dtypes)],
        compiler_params=_params("arbitrary"),
    )(pos, *shards)


def _gather_send(outs, stacks, ssem, rsem):
    x, y, c, chips = _place()
    me = 2 * x + y
    for w, st in enumerate(stacks):
        half = st.shape[1] // 2
        for s, (px, py) in enumerate(chips):
            for r0, rs in _row_chunks(half, _row_bytes(st)):
                piece = outs[w].at[me, pl.ds(c * half + r0, rs)]
                _rcopy(piece, piece, ssem.at[w, s], rsem.at[w, s], (px, py, c)).start()


def _gather_pass(outs, stacks, ssem, rsem):
    x, y, c, chips = _place()
    sib = (x, y, 1 - c)
    for w, st in enumerate(stacks):
        half = st.shape[1] // 2
        for s, (px, py) in enumerate(chips):
            blk = outs[w].at[2 * px + py, pl.ds(c * half, half)]
            _rcopy(blk, blk, ssem.at[w, s], rsem.at[w, s], sib).wait_recv()
            for r0, rs in _row_chunks(half, _row_bytes(st)):
                piece = outs[w].at[2 * px + py, pl.ds(c * half + r0, rs)]
                _rcopy(piece, piece, ssem.at[w, 3 + s], rsem.at[w, 3 + s], sib).start()


def _gather_done(outs, stacks, ssem, rsem):
    x, y, c, chips = _place()
    sib = (x, y, 1 - c)
    for w, st in enumerate(stacks):
        half = st.shape[1] // 2
        for s, (px, py) in enumerate(chips):
            blk = outs[w].at[2 * px + py, pl.ds((1 - c) * half, half)]
            _rcopy(blk, blk, ssem.at[w, 3 + s], rsem.at[w, 3 + s], sib).wait_recv()
    for w, st in enumerate(stacks):
        half = st.shape[1] // 2
        blk = outs[w].at[0, pl.ds(0, half)]
        for s in range(6):
            _rcopy(blk, blk, ssem.at[w, s], rsem.at[w, s], sib).wait_send()


def _gather_sems(nw):
    return [pltpu.SemaphoreType.DMA((nw, 6)), pltpu.SemaphoreType.DMA((nw, 6))]


def _gather_weights(stacks):
    nw = len(stacks)

    def body(*refs):
        outs = refs[nw:2 * nw]
        ssem, rsem = refs[2 * nw:]
        _gather_send(outs, stacks, ssem, rsem)
        _gather_pass(outs, stacks, ssem, rsem)
        _gather_done(outs, stacks, ssem, rsem)

    return pl.pallas_call(
        body, name="gather_weights",
        in_specs=[ANY] * nw, out_specs=[ANY] * nw,
        out_shape=[jax.ShapeDtypeStruct(s.shape, s.dtype) for s in stacks],
        input_output_aliases={w: w for w in range(nw)},
        scratch_shapes=_gather_sems(nw),
    )(*stacks)


def _swap_halves(grads, tag):
    nw = len(grads)

    def body(*refs):
        ins, got = refs[:nw], refs[nw:2 * nw]
        ssem, rsem = refs[2 * nw:]
        x, y, c, _ = _place()
        cps = []
        for w in range(nw):
            hr = grads[w].shape[1] // 2
            for k in range(N_CHIPS):
                for r0, rs in _row_chunks(hr, _row_bytes(grads[w])):
                    _rcopy(ins[w].at[k, pl.ds((1 - c) * hr + r0, rs)], got[w].at[k, pl.ds(r0, rs)],
                           ssem.at[w], rsem.at[w], (x, y, 1 - c)).start()
            cps.append(_rcopy(got[w], got[w], ssem.at[w], rsem.at[w], (x, y, 1 - c)))
        for cp in cps:
            cp.wait()

    return pl.pallas_call(
        body, name="swap_halves_" + tag,
        in_specs=[ANY] * nw, out_specs=[ANY] * nw,
        out_shape=[jax.ShapeDtypeStruct((g.shape[0], g.shape[1] // 2, g.shape[2]), g.dtype) for g in grads],
        scratch_shapes=[pltpu.SemaphoreType.DMA((nw,)), pltpu.SemaphoreType.DMA((nw,))],
    )(*grads)


def _add_pairs(grads, got, pos, steps, tag):
    nw = len(grads)

    def body(pos_ref, *refs):
        for w in range(nw):
            refs[2 * nw + w][...] = (refs[w][...].astype(F32) + refs[nw + w][...].astype(F32)).astype(BF16)

    blk = lambda a: (a.shape[0], a.shape[1] // steps, a.shape[2])
    own = [pl.BlockSpec(blk(a), lambda i, p: (0, p[1] * steps + i, 0)) for a in got]
    rec = [pl.BlockSpec(blk(a), lambda i, p: (0, i, 0)) for a in got]
    return pl.pallas_call(
        body, name="add_pairs_" + tag,
        grid_spec=pltpu.PrefetchScalarGridSpec(num_scalar_prefetch=1, grid=(steps,), in_specs=own + rec, out_specs=rec),
        out_shape=[jax.ShapeDtypeStruct(a.shape, BF16) for a in got],
        compiler_params=_params("arbitrary"),
    )(pos, *grads, *got)


def _scatter_send(ins, got, parts, ssem, rsem):
    x, y, c, chips = _place()
    for w, p in enumerate(parts):
        for s, (px, py) in enumerate(chips):
            for r0, rs in _row_chunks(p.shape[1], _row_bytes(p)):
                _rcopy(ins[w].at[2 * px + py, pl.ds(r0, rs)], got[w].at[s, pl.ds(r0, rs)],
                       ssem.at[w, s], rsem.at[w, s], (px, py, c)).start()


def _scatter_done(got, parts, ssem, rsem):
    x, y, c, chips = _place()
    for w in range(len(parts)):
        for s, (px, py) in enumerate(chips):
            _rcopy(got[w].at[s], got[w].at[s], ssem.at[w, s], rsem.at[w, s], (px, py, c)).wait()


def _scatter_sems(nw):
    return [pltpu.SemaphoreType.DMA((nw, 3)), pltpu.SemaphoreType.DMA((nw, 3))]


def _scatter_shapes(parts):
    return [jax.ShapeDtypeStruct((3,) + p.shape[1:], p.dtype) for p in parts]


def _scatter_chips(parts):
    nw = len(parts)

    def body(*refs):
        ins, got = refs[:nw], refs[nw:2 * nw]
        ssem, rsem = refs[2 * nw:]
        _scatter_send(ins, got, parts, ssem, rsem)
        _scatter_done(got, parts, ssem, rsem)

    return pl.pallas_call(
        body, name="scatter_chips",
        in_specs=[ANY] * nw, out_specs=[ANY] * nw, out_shape=_scatter_shapes(parts),
        scratch_shapes=_scatter_sems(nw),
    )(*parts)


def _sum_chips(parts, got, pos, steps):
    nw = len(parts)

    def body(pos_ref, *refs):
        for w in range(nw):
            acc = refs[w][0].astype(F32)
            for s in range(3):
                acc = acc + refs[nw + w][s].astype(F32)
            refs[2 * nw + w][...] = acc

    own = [pl.BlockSpec((1, p.shape[1] // steps, p.shape[2]), lambda i, ps: (ps[0], i, 0)) for p in parts]
    rec = [pl.BlockSpec((3, p.shape[1] // steps, p.shape[2]), lambda i, ps: (0, i, 0)) for p in parts]
    outs = [pl.BlockSpec((p.shape[1] // steps, p.shape[2]), lambda i, ps: (ps[1] * steps + i, 0)) for p in parts]
    return pl.pallas_call(
        body, name="sum_chips",
        grid_spec=pltpu.PrefetchScalarGridSpec(num_scalar_prefetch=1, grid=(steps,), in_specs=own + rec, out_specs=outs),
        out_shape=[jax.ShapeDtypeStruct((2 * p.shape[1], p.shape[2]), F32) for p in parts],
        compiler_params=_params("arbitrary"),
    )(pos, *parts, *got)


def _join_halves(shards):
    nw = len(shards)

    def body(*refs):
        outs = refs[nw:2 * nw]
        ssem, rsem = refs[2 * nw:]
        x, y, c, _ = _place()
        cps = []
        for w in range(nw):
            hr = shards[w].shape[0] // 2
            for r0, rs in _row_chunks(hr, _row_bytes(shards[w])):
                piece = outs[w].at[pl.ds(c * hr + r0, rs)]
                _rcopy(piece, piece, ssem.at[w], rsem.at[w], (x, y, 1 - c)).start()
            mine = outs[w].at[pl.ds(c * hr, hr)]
            cps.append(_rcopy(mine, mine, ssem.at[w], rsem.at[w], (x, y, 1 - c)))
        for cp in cps:
            cp.wait()

    return pl.pallas_call(
        body, name="join_halves",
        in_specs=[ANY] * nw, out_specs=[ANY] * nw,
        out_shape=[jax.ShapeDtypeStruct(s.shape, F32) for s in shards],
        input_output_aliases={w: w for w in range(nw)},
        scratch_shapes=[pltpu.SemaphoreType.DMA((nw,)), pltpu.SemaphoreType.DMA((nw,))],
    )(*shards)


def _allreduce_small(packed):
    shape = packed.shape

    def body(p_ref, out_ref, rbuf, ssem, rsem):
        x, y, c, _ = _place()
        out_ref[...] = p_ref[...]
        for st, peer in enumerate([(x, y, 1 - c), (1 - x, y, c), (x, 1 - y, c)]):
            cp = _rcopy(out_ref, rbuf.at[st], ssem.at[st], rsem.at[st], peer)
            cp.start()
            cp.wait()
            out_ref[...] = out_ref[...] + rbuf[st]

    vm = pl.BlockSpec(memory_space=pltpu.VMEM)
    return pl.pallas_call(
        body, name="allreduce_small",
        in_specs=[vm], out_specs=vm, out_shape=jax.ShapeDtypeStruct(shape, F32),
        scratch_shapes=[pltpu.VMEM((3,) + shape, F32), pltpu.SemaphoreType.DMA((3,)), pltpu.SemaphoreType.DMA((3,))],
        compiler_params=pltpu.CompilerParams(vmem_limit_bytes=VMEM_LIMIT),
    )(packed)


def _adamw_math(w, g, m, v):
    m = ADAM_B1 * m + (1.0 - ADAM_B1) * g
    v = ADAM_B2 * v + (1.0 - ADAM_B2) * (g * g)
    delta = -ADAM_LR * ((m * ADAM_C1) / (jnp.sqrt(v * ADAM_C2) + ADAM_EPS) + ADAM_WD * w)
    return delta, m, v


def _adamw(ws, gs, ms, vs, steps, name):
    nw = len(ws)

    def body(*refs):
        for k in range(nw):
            d, m, v = _adamw_math(refs[k][...], refs[nw + k][...], refs[2 * nw + k][...], refs[3 * nw + k][...])
            refs[4 * nw + k][...] = d
            refs[5 * nw + k][...] = m
            refs[6 * nw + k][...] = v

    specs = [pl.BlockSpec((a.shape[0] // steps, a.shape[1]), lambda i: (i, 0)) for a in ws]
    shapes = [jax.ShapeDtypeStruct(a.shape, F32) for a in ws]
    outs = pl.pallas_call(
        body, name=name, grid=(steps,),
        in_specs=specs * 4, out_specs=specs * 3, out_shape=shapes * 3,
        compiler_params=_params("arbitrary"),
    )(*ws, *gs, *ms, *vs)
    return outs[:nw], outs[nw:2 * nw], outs[2 * nw:]


SMALL = ["norm_mix_g", "pool_w", "pool_b", "pool_scale", "conv_b", "gate_a_w", "gate_a_b", "gate_x_w", "gate_x_b",
         "lru_L", "norm_mlp_g", "norm_ple_g", "b_ple_gate", "norm_final_g"]
BIG = ["w_in", "w_out", "w_up", "w_down", "w_ple_gate", "w_ple_proj"]
ORDER = ["norm_mix_g", "w_in", "pool_w", "pool_b", "pool_scale", "conv_w", "conv_b", "gate_a_w", "gate_a_b", "gate_x_w",
         "gate_x_b", "lru_L", "w_out", "norm_mlp_g", "w_up", "w_down", "norm_ple_g", "w_ple_gate", "b_ple_gate",
         "w_ple_proj", "norm_final_g"]
LANES = 128


def _block_diag(w):
    eye = jnp.eye(LRU_HEADS, dtype=w.dtype)
    return jnp.einsum("hij,hk->hikj", w, eye).reshape(D_LRU, D_LRU)


def _diag_blocks(full):
    f = full.reshape(LRU_HEADS, LRU_BLOCK, LRU_HEADS, LRU_BLOCK)
    return jnp.stack([f[h, :, h, :] for h in range(LRU_HEADS)])


def _rows128(a):
    return a.reshape(-1, LANES)


def _pad8(a):
    r = (-a.shape[0]) % 8
    return jnp.pad(a, ((0, r), (0, 0))) if r else a


def kernel(x, p, norm_mix_g, w_in, pool_w, pool_b, pool_scale, conv_w, conv_b, gate_a_w, gate_a_b, gate_x_w, gate_x_b, lru_L, w_out, norm_mlp_g, w_up, w_down, norm_ple_g, w_ple_gate, b_ple_gate, w_ple_proj, norm_final_g, loss_target, m_norm_mix_g, m_w_in, m_pool_w, m_pool_b, m_pool_scale, m_conv_w, m_conv_b, m_gate_a_w, m_gate_a_b, m_gate_x_w, m_gate_x_b, m_lru_L, m_w_out, m_norm_mlp_g, m_w_up, m_w_down, m_norm_ple_g, m_w_ple_gate, m_b_ple_gate, m_w_ple_proj, m_norm_final_g, v_norm_mix_g, v_w_in, v_pool_w, v_pool_b, v_pool_scale, v_conv_w, v_conv_b, v_gate_a_w, v_gate_a_b, v_gate_x_w, v_gate_x_b, v_lru_L, v_w_out, v_norm_mlp_g, v_w_up, v_w_down, v_norm_ple_g, v_w_ple_gate, v_b_ple_gate, v_w_ple_proj, v_norm_final_g):
    W = dict(norm_mix_g=norm_mix_g, w_in=w_in, pool_w=pool_w, pool_b=pool_b, pool_scale=pool_scale, conv_w=conv_w,
             conv_b=conv_b, gate_a_w=gate_a_w, gate_a_b=gate_a_b, gate_x_w=gate_x_w, gate_x_b=gate_x_b, lru_L=lru_L,
             w_out=w_out, norm_mlp_g=norm_mlp_g, w_up=w_up, w_down=w_down, norm_ple_g=norm_ple_g,
             w_ple_gate=w_ple_gate, b_ple_gate=b_ple_gate, w_ple_proj=w_ple_proj, norm_final_g=norm_final_g)
    M = dict(norm_mix_g=m_norm_mix_g, w_in=m_w_in, pool_w=m_pool_w, pool_b=m_pool_b, pool_scale=m_pool_scale,
             conv_w=m_conv_w, conv_b=m_conv_b, gate_a_w=m_gate_a_w, gate_a_b=m_gate_a_b, gate_x_w=m_gate_x_w,
             gate_x_b=m_gate_x_b, lru_L=m_lru_L, w_out=m_w_out, norm_mlp_g=m_norm_mlp_g, w_up=m_w_up, w_down=m_w_down,
             norm_ple_g=m_norm_ple_g, w_ple_gate=m_w_ple_gate, b_ple_gate=m_b_ple_gate, w_ple_proj=m_w_ple_proj,
             norm_final_g=m_norm_final_g)
    V = dict(norm_mix_g=v_norm_mix_g, w_in=v_w_in, pool_w=v_pool_w, pool_b=v_pool_b, pool_scale=v_pool_scale,
             conv_w=v_conv_w, conv_b=v_conv_b, gate_a_w=v_gate_a_w, gate_a_b=v_gate_a_b, gate_x_w=v_gate_x_w,
             gate_x_b=v_gate_x_b, lru_L=v_lru_L, w_out=v_w_out, norm_mlp_g=v_norm_mlp_g, w_up=v_w_up, w_down=v_w_down,
             norm_ple_g=v_norm_ple_g, w_ple_gate=v_w_ple_gate, b_ple_gate=v_b_ple_gate, w_ple_proj=v_w_ple_proj,
             norm_final_g=v_norm_final_g)

    s_len = x.shape[1]
    tm_mix = min(256, s_len)
    tm = min(512, s_len)
    chip = (2 * lax.axis_index("x") + lax.axis_index("y")).astype(jnp.int32)
    pos = jnp.stack([chip, lax.axis_index("c").astype(jnp.int32)])

    shards = [w_in[0], w_out[0], w_up[0], w_down[0], w_ple_gate[0], w_ple_proj[0], jnp.pad(conv_w[0], ((0, 12), (0, 0)))]
    st_in, st_out, st_up, st_dn, st_pg, st_pp, st_cw = _stack_own(shards, [BF16] * 6 + [F32], pos, 8)
    win_g, wout_g, cw_g = _gather_weights([st_in, st_out, st_cw])
    wout_f = wout_g.reshape(D_MODEL, D_MODEL)
    cw_f = jnp.transpose(cw_g[:, :CONV_WIDTH], (1, 0, 2)).reshape(CONV_WIDTH, D_LRU)
    pw_b = pool_w[0].astype(BF16)
    wa_b = _block_diag(gate_a_w[0]).astype(BF16)
    wx_b = _block_diag(gate_x_w[0]).astype(BF16)
    pb_r = pool_b.reshape(1, D_POOL)
    ba_r = gate_a_b.reshape(1, D_LRU)
    bx_r = gate_x_b.reshape(1, D_LRU)
    g4 = norm_final_g.reshape(1, D_MODEL)
    mix_w = (pw_b, pb_r, pool_scale, cw_f, conv_b, wa_b, ba_r, wx_b, bx_r, lru_L, wout_f)

    xs, ps, ts = x[0], p[0, 0], loss_target[0]
    (z1, proj, hst, cat, h1), (wup_g, wdn_g, wpg_g, wpp_g) = _fwd_mix(
        xs, norm_mix_g, win_g, *mix_w, tm_mix, [st_up, st_dn, st_pg, st_pp])
    wpg_f = wpg_g.reshape(D_MODEL, D_MODEL)
    wpp_f = jnp.transpose(wpp_g, (1, 0, 2)).reshape(PLE_DIM, D_MODEL)
    z2, ru, h2 = _fwd_mlp(h1, norm_mlp_g, wup_g, wdn_g, tm)
    dh2, dh2b, d_wpg, d_wpp, head_vec = _head(h2, ps, ts, norm_ple_g, wpg_f, b_ple_gate, wpp_f, g4, tm)
    dup, dh1, mlp_vec = _bwd_mlp_x(dh2, ru, h1, norm_mlp_g, wup_g, wdn_g, tm)
    d_wup, d_wdn = _bwd_mlp_w(z2, dup, ru, dh2b, tm)
    early = [d_wup, d_wdn, d_wpg.reshape(N_CHIPS, D_MODEL // N_CHIPS, D_MODEL), d_wpp]
    pair_e = _add_pairs(early, _swap_halves(early, "early"), pos, 8, "early")
    (dproj, d_wout, d_pw, d_wa, d_wx, mix_vec), got_e = _bwd_mix(dh1, proj, hst, cat, *mix_w, tm_mix, pair_e)
    dx, d_win, in_vec = _bwd_in(dproj, z1, xs, dh1, norm_mix_g, win_g, tm)

    last = [d_win, d_wout.reshape(N_CHIPS, D_MODEL // N_CHIPS, D_MODEL)]
    pair_l = _add_pairs(last, _swap_halves(last, "last"), pos, 8, "last")
    got_l = _scatter_chips(pair_l)
    g_big = _join_halves(_sum_chips(pair_l + pair_e, got_l + got_e, pos, 8))

    g_small = {
        "norm_mix_g": in_vec[0:1], "pool_w": d_pw, "pool_b": mix_vec[0:1], "pool_scale": mix_vec[1:2],
        "conv_b": mix_vec[2:3], "gate_a_w": _diag_blocks(d_wa), "gate_a_b": mix_vec[3:4],
        "gate_x_w": _diag_blocks(d_wx), "gate_x_b": mix_vec[4:5], "lru_L": mix_vec[5:6], "norm_mlp_g": mlp_vec[0:1],
        "norm_ple_g": head_vec[0:1], "b_ple_gate": head_vec[2:3], "norm_final_g": head_vec[1:2],
    }
    d_cw = jnp.transpose(mix_vec[8:8 + CONV_WIDTH].reshape(CONV_WIDTH, N_CHIPS, LANES), (1, 0, 2)).reshape(-1, LANES)
    pieces = [_pad8(_rows128(g_small[k])) for k in SMALL] + [d_cw, _pad8(head_vec[3:4, :LANES])]
    offs = [0]
    for pc in pieces:
        offs.append(offs[-1] + pc.shape[0])
    red = _allreduce_small(jnp.concatenate(pieces, axis=0))
    loss = red[offs[-2], 0]
    g_cw = lax.dynamic_slice(red, (offs[len(SMALL)] + CONV_WIDTH * chip, 0), (CONV_WIDTH, LANES))

    def packed(src):
        return jnp.concatenate([_pad8(_rows128(src[k])) for k in SMALL] + [_pad8(src["conv_w"][0])], axis=0)

    n_small = offs[len(SMALL)]
    g_pack = jnp.concatenate([red[:n_small], _pad8(g_cw)], axis=0)
    (d_pack,), (m_pack,), (v_pack,) = _adamw([packed(W)], [g_pack], [packed(M)], [packed(V)], 1, "adamw_small")

    big2d = lambda src: [src[k][0] for k in BIG]
    d_big, m_big, v_big = _adamw(big2d(W), g_big, big2d(M), big2d(V), 8, "adamw_big")

    def unpack(pack, big_list):
        out = {}
        for idx, k in enumerate(SMALL):
            n_el = W[k].size
            out[k] = pack[offs[idx]:offs[idx + 1]].reshape(-1)[:n_el].reshape(W[k].shape)
        out["conv_w"] = pack[n_small:n_small + CONV_WIDTH].reshape(W["conv_w"].shape)
        for k, a in zip(BIG, big_list):
            out[k] = a.reshape(W[k].shape)
        return out

    grads = unpack(g_pack, g_big)
    deltas = unpack(d_pack, d_big)
    new_m = unpack(m_pack, m_big)
    new_v = unpack(v_pack, v_big)
    return (loss, dx[None], *[grads[k] for k in ORDER], *[deltas[k] for k in ORDER],
            *[new_m[k] for k in ORDER], *[new_v[k] for k in ORDER])
```

```python
import functools

import jax
import jax.numpy as jnp
from jax import lax
from jax.experimental import pallas as pl
from jax.experimental.pallas import tpu as pltpu

F32 = jnp.float32
BF16 = jnp.bfloat16
MESH = pl.DeviceIdType.MESH

D_MODEL = 1024
D_POOL = 512
D_LRU = 512
POOL_WINDOWS = (2, 4, 8, 16)
POOL_GROUP = 128
POOL_HALO = 16
CONV_WIDTH = 4
CONV_HALO = 8
PASS_STEPS = 2
LRU_HEADS = 8
LRU_BLOCK = 64
LRU_C = 8.0
D_FF = 4096
PLE_DIM = 256
D_IN_PROJ = 1536
RMS_EPS = 1e-6
N_CHIPS = 4
FF_BLOCK = D_FF // N_CHIPS

ADAM_LR = 0.001
ADAM_B1 = 0.9
ADAM_B2 = 0.999
ADAM_EPS = 1e-08
ADAM_WD = 0.01
ADAM_STEP = 10
ADAM_C1 = 1.0 / (1.0 - ADAM_B1 ** ADAM_STEP)
ADAM_C2 = 1.0 / (1.0 - ADAM_B2 ** ADAM_STEP)

VMEM_LIMIT = 56 * 1024 * 1024
GELU_C = 0.7978845608028654
GELU_A = 0.044715

NT = (((1,), (1,)), ((), ()))
TN = (((0,), (0,)), ((), ()))


def _dot(a, b):
    return jnp.dot(a, b, preferred_element_type=F32)


def _dot_nt(a, b):
    return lax.dot_general(a, b, NT, preferred_element_type=F32)


def _dot_tn(a, b):
    return lax.dot_general(a, b, TN, preferred_element_type=F32)


def _params(*sem):
    return pltpu.CompilerParams(dimension_semantics=sem, vmem_limit_bytes=VMEM_LIMIT)


def _full(shape):
    nd = len(shape)
    return pl.BlockSpec(shape, lambda *_: (0,) * nd)


def _resident(shape):
    nd = len(shape)
    return pl.BlockSpec(shape, lambda *_: (0,) * nd, pipeline_mode=pl.Buffered(1))


def _rstd(x):
    return lax.rsqrt(jnp.mean(x * x, axis=-1, keepdims=True) + RMS_EPS)


def _rms_bwd(x, g, dz):
    xr = x * _rstd(x)
    r = _rstd(x)
    dyg = dz * g
    dx = r * (dyg - xr * jnp.mean(dyg * xr, axis=-1, keepdims=True))
    dg = jnp.sum(dz * xr, axis=0, keepdims=True)
    return dx, dg


def _sigmoid(x):
    return 1.0 / (1.0 + jnp.exp(-x))


def _log_sigmoid(v):
    u = jnp.exp(-jnp.abs(v))
    w = 1.0 + u
    l1p = jnp.where(w == 1.0, u, jnp.log(w) * u / jnp.where(w == 1.0, 1.0, w - 1.0))
    return jnp.minimum(v, 0.0) - l1p


def _gelu(x):
    t = jnp.tanh(GELU_C * (x + GELU_A * x * x * x))
    return 0.5 * x * (1.0 + t), t


def _gelu_grad(x, t):
    return 0.5 * (1.0 + t) + 0.5 * x * (1.0 - t * t) * GELU_C * (1.0 + 3.0 * GELU_A * x * x)


def _rows(shape, t0):
    return lax.broadcasted_iota(jnp.int32, shape, 0) + t0


def _pool_diff(u_pool, prev, t0):
    tm = u_pool.shape[0]
    rows = _rows((tm, POOL_GROUP), t0)
    outs, invs = [], []
    for g, w in enumerate(POOL_WINDOWS):
        sl = slice(POOL_GROUP * g, POOL_GROUP * (g + 1))
        ug = u_pool[:, sl]
        s = jnp.concatenate([prev[:, sl], ug], axis=0)
        k = 1
        while k < w:
            s = s + pltpu.roll(s, k, 0)
            k *= 2
        inv = 1.0 / jnp.minimum(rows + 1, w).astype(F32)
        outs.append(s[POOL_HALO:] * inv - ug)
        invs.append(inv)
    return jnp.concatenate(outs, axis=1), jnp.concatenate(invs, axis=1)


def _pool_diff_bwd(dd, inv, nxt):
    ddc = dd * inv
    outs = []
    for g, w in enumerate(POOL_WINDOWS):
        sl = slice(POOL_GROUP * g, POOL_GROUP * (g + 1))
        s = jnp.concatenate([ddc[:, sl], nxt[:, sl]], axis=0)
        n = s.shape[0]
        k = 1
        while k < w:
            s = s + pltpu.roll(s, n - k, 0)
            k *= 2
        outs.append(s[:n - POOL_HALO] - dd[:, sl])
    return jnp.concatenate(outs, axis=1), ddc


def _conv_taps(u, prev):
    ext = jnp.concatenate([prev, u], axis=0)
    return [pltpu.roll(ext, CONV_WIDTH - 1 - k, 0)[CONV_HALO:] if k < CONV_WIDTH - 1 else u for k in range(CONV_WIDTH)]


def _scan_fwd(a, b):
    tm = a.shape[0]
    rows = _rows(a.shape, 0)
    k = 1
    while k < tm:
        ar = pltpu.roll(a, k, 0)
        br = pltpu.roll(b, k, 0)
        m = rows >= k
        b = jnp.where(m, a * br + b, b)
        a = jnp.where(m, a * ar, a)
        k *= 2
    return a, b


def _scan_rev(a, b):
    tm = a.shape[0]
    rows = _rows(a.shape, 0)
    k = 1
    while k < tm:
        ar = pltpu.roll(a, tm - k, 0)
        br = pltpu.roll(b, tm - k, 0)
        m = rows < tm - k
        b = jnp.where(m, a * br + b, b)
        a = jnp.where(m, a * ar, a)
        k *= 2
    return a, b


def _lru_gates(xb, wa, ba, wx, bx, lsl8, t0):
    xbb = xb.astype(BF16)
    r = _sigmoid(_dot(xbb, wa) + ba)
    ig = _sigmoid(_dot(xbb, wx) + bx)
    a = jnp.exp(r * lsl8)
    first = _rows(xb.shape, t0) == 0
    mult = jnp.where(first, 1.0, jnp.sqrt(1.0 - a * a))
    return r, ig, a, mult, first


def _fwd_mix(x, g1, w_in, pool_w, pool_b, pool_scale, conv_w, conv_b, wa, ba, wx, bx, lru_l, w_out, tm, sub, late):
    s_len = x.shape[0]
    n = s_len // tm
    nl = len(late)

    def body(x_ref, g1_ref, win_ref, pw_ref, pb_ref, ps_ref, cw_ref, cb_ref, wa_ref, ba_ref, wx_ref, bx_ref, l_ref,
             wout_ref, *rest):
        z1_ref, proj_ref, h_ref, cat_ref, h1_ref = rest[nl:nl + 5]
        late_ref = rest[nl + 5:2 * nl + 5]
        cpool, clru, ch, ssem, rsem = rest[2 * nl + 5:]
        i = pl.program_id(0)

        @pl.when(i == 0)
        def _():
            _gather_send(late_ref, late, ssem, rsem)
            cpool[...] = jnp.zeros_like(cpool)
            clru[...] = jnp.zeros_like(clru)
            ch[...] = jnp.zeros_like(ch)

        lsl8 = LRU_C * _log_sigmoid(l_ref[...])
        cp, cl, hc = cpool[...], clru[...], ch[7:8, :]
        def in_proj(k):
            rs = slice(k * sub, (k + 1) * sub)
            xv = x_ref[rs, :]
            zb = (xv * _rstd(xv) * g1_ref[...]).astype(BF16)
            z1_ref[rs, :] = zb
            proj = jnp.concatenate([_dot(zb, win_ref[j]) for j in range(N_CHIPS)], axis=1)
            proj_ref[rs, :] = proj
            return xv, proj

        nxt = in_proj(0)
        for k in range(tm // sub):
            rs = slice(k * sub, (k + 1) * sub)
            t0 = i * tm + k * sub
            xv, proj = nxt
            if k + 1 < tm // sub:
                nxt = in_proj(k + 1)
            u_pool = proj[:, :D_POOL]
            u_lru = proj[:, D_POOL:D_POOL + D_LRU]
            u_gate = proj[:, D_POOL + D_LRU:]

            d, _ = _pool_diff(u_pool, cp, t0)
            cp = u_pool[sub - POOL_HALO:]
            db = d.astype(BF16)
            yp = jnp.concatenate(
                [_dot(db[:, POOL_GROUP * g:POOL_GROUP * (g + 1)], pw_ref[g]) for g in range(len(POOL_WINDOWS))], axis=1)
            y_pool = (yp + pb_ref[...]) * ps_ref[...]

            taps = _conv_taps(u_lru, cl)
            cl = u_lru[sub - CONV_HALO:]
            xb = cb_ref[...]
            for q in range(CONV_WIDTH):
                xb = xb + taps[q] * cw_ref[q:q + 1, :]
            _, ig, a, mult, _ = _lru_gates(xb, wa_ref[...], ba_ref[...], wx_ref[...], bx_ref[...], lsl8, t0)
            pa, hb = _scan_fwd(a, mult * (ig * xb))
            h = hb + pa * hc
            hc = h[sub - 1:sub, :]
            h_ref[rs, :] = h
            gl, _ = _gelu(u_gate)
            cat = jnp.concatenate([y_pool, h * gl], axis=1).astype(BF16)
            cat_ref[rs, :] = cat
            h1_ref[rs, :] = xv + _dot(cat, wout_ref[...])
        cpool[...] = cp
        clru[...] = cl
        ch[...] = jnp.broadcast_to(hc, ch.shape)

        @pl.when(i == max(n - PASS_STEPS, 0))
        def _():
            _gather_pass(late_ref, late, ssem, rsem)

        @pl.when(i == n - 1)
        def _():
            _gather_done(late_ref, late, ssem, rsem)

    row = lambda w: pl.BlockSpec((tm, w), lambda i: (i, 0))
    ins = [x, g1, w_in, pool_w, pool_b, pool_scale, conv_w, conv_b, wa, ba, wx, bx, lru_l, w_out]
    outs = pl.pallas_call(
        body, name="fwd_mix", grid=(n,),
        in_specs=[row(D_MODEL)] + [_resident(a.shape) for a in ins[1:]] + [ANY] * nl,
        out_specs=[row(D_MODEL), row(D_IN_PROJ), row(D_LRU), row(D_MODEL), row(D_MODEL)] + [ANY] * nl,
        out_shape=[jax.ShapeDtypeStruct((s_len, D_MODEL), BF16), jax.ShapeDtypeStruct((s_len, D_IN_PROJ), F32),
                   jax.ShapeDtypeStruct((s_len, D_LRU), F32), jax.ShapeDtypeStruct((s_len, D_MODEL), BF16),
                   jax.ShapeDtypeStruct((s_len, D_MODEL), F32)]
        + [jax.ShapeDtypeStruct(a.shape, a.dtype) for a in late],
        input_output_aliases={len(ins) + k: 5 + k for k in range(nl)},
        scratch_shapes=[pltpu.VMEM((POOL_HALO, D_POOL), F32), pltpu.VMEM((CONV_HALO, D_LRU), F32),
                        pltpu.VMEM((8, D_LRU), F32)] + _gather_sems(nl),
        compiler_params=_params("arbitrary"),
    )(*ins, *late)
    return outs[:5], outs[5:]


def _fwd_mlp(h1, g2, w_up, w_down, tm):
    s_len = h1.shape[0]
    n = s_len // tm

    def body(h1_ref, g2_ref, wup_ref, wdn_ref, z2_ref, ru_ref, h2_ref):
        hv = h1_ref[...]
        zb = (hv * _rstd(hv) * g2_ref[...]).astype(BF16)
        z2_ref[...] = zb
        acc = hv
        for j in range(N_CHIPS):
            ru = jnp.maximum(_dot(zb, wup_ref[j]), 0.0)
            ru_ref[:, FF_BLOCK * j:FF_BLOCK * (j + 1)] = ru.astype(BF16)
            acc = acc + _dot((ru * ru).astype(BF16), wdn_ref[j])
        h2_ref[...] = acc

    row = lambda w: pl.BlockSpec((tm, w), lambda i: (i, 0))
    return pl.pallas_call(
        body, name="fwd_mlp", grid=(n,),
        in_specs=[row(D_MODEL), _full(g2.shape), _resident(w_up.shape), _resident(w_down.shape)],
        out_specs=[row(D_MODEL), row(D_FF), row(D_MODEL)],
        out_shape=[jax.ShapeDtypeStruct((s_len, D_MODEL), BF16), jax.ShapeDtypeStruct((s_len, D_FF), BF16),
                   jax.ShapeDtypeStruct((s_len, D_MODEL), F32)],
        compiler_params=_params("arbitrary"),
    )(h1, g2, w_up, w_down)


def _head(h2, p, target, g3, w_pg, b_pg, w_pp, g4, tm):
    s_len = h2.shape[0]
    n = s_len // tm

    def body(h2_ref, p_ref, t_ref, g3_ref, wpg_ref, bpg_ref, wpp_ref, g4_ref,
             dh2_ref, dh2b_ref, dwpg_ref, dwpp_ref, vec_ref, a_pg, a_pp, a_vec):
        i = pl.program_id(0)

        @pl.when(i == 0)
        def _():
            a_pg[...] = jnp.zeros_like(a_pg)
            a_pp[...] = jnp.zeros_like(a_pp)
            a_vec[...] = jnp.zeros_like(a_vec)

        h2v = h2_ref[...]
        g3v = g3_ref[...]
        g4v = g4_ref[...]
        z3 = (h2v * _rstd(h2v) * g3v).astype(BF16)
        gate = _sigmoid(_dot(z3, wpg_ref[...]) + bpg_ref[...])
        pb = p_ref[...].astype(BF16)
        pp = _dot(pb, wpp_ref[...])
        h3 = h2v + gate * pp
        r4 = _rstd(h3)
        diff = h3 * r4 * g4v - t_ref[...]
        loss = 0.5 * jnp.sum(jnp.mean(diff * diff, axis=-1, keepdims=True), axis=0, keepdims=True)
        dy = diff * (1.0 / D_MODEL)
        dh3, dg4 = _rms_bwd(h3, g4v, dy)
        dpp = (dh3 * gate).astype(BF16)
        dpre = dh3 * pp * gate * (1.0 - gate)
        dpreb = dpre.astype(BF16)
        dz3 = _dot_nt(dpreb, wpg_ref[...])
        dx, dg3 = _rms_bwd(h2v, g3v, dz3)
        dh2 = dh3 + dx
        dh2_ref[...] = dh2
        dh2b_ref[...] = dh2.astype(BF16)
        a_pg[...] += _dot_tn(z3, dpreb)
        a_pp[...] += _dot_tn(pb, dpp)
        a_vec[0:1, :] += dg3
        a_vec[1:2, :] += dg4
        a_vec[2:3, :] += jnp.sum(dpre, axis=0, keepdims=True)
        a_vec[3:4, :] += jnp.broadcast_to(loss, (1, D_MODEL))

        @pl.when(i == n - 1)
        def _():
            dwpg_ref[...] = a_pg[...].astype(BF16)
            for j in range(N_CHIPS):
                dwpp_ref[j] = a_pp[:, PLE_DIM * j:PLE_DIM * (j + 1)].astype(BF16)
            vec_ref[...] = a_vec[...]

    row = lambda w: pl.BlockSpec((tm, w), lambda i: (i, 0))
    ins = [h2, p, target, g3, w_pg, b_pg, w_pp, g4]
    return pl.pallas_call(
        body, name="head", grid=(n,),
        in_specs=[row(D_MODEL), row(PLE_DIM), row(D_MODEL)] + [_resident(a.shape) for a in ins[3:]],
        out_specs=[row(D_MODEL), row(D_MODEL), _full((D_MODEL, D_MODEL)), _full((N_CHIPS, PLE_DIM, PLE_DIM)),
                   _full((8, D_MODEL))],
        out_shape=[jax.ShapeDtypeStruct((s_len, D_MODEL), F32), jax.ShapeDtypeStruct((s_len, D_MODEL), BF16),
                   jax.ShapeDtypeStruct((D_MODEL, D_MODEL), BF16),
                   jax.ShapeDtypeStruct((N_CHIPS, PLE_DIM, PLE_DIM), BF16), jax.ShapeDtypeStruct((8, D_MODEL), F32)],
        scratch_shapes=[pltpu.VMEM((D_MODEL, D_MODEL), F32), pltpu.VMEM((PLE_DIM, D_MODEL), F32),
                        pltpu.VMEM((8, D_MODEL), F32)],
        compiler_params=_params("arbitrary"),
    )(*ins)


def _bwd_mlp_x(dh2, ru, h1, g2, w_up, w_down, tm):
    s_len = dh2.shape[0]
    n = s_len // tm

    def body(dh2_ref, ru_ref, h1_ref, g2_ref, wup_ref, wdn_ref, dup_ref, dh1_ref, dg2_ref, a_g):
        i = pl.program_id(0)

        @pl.when(i == 0)
        def _():
            a_g[...] = jnp.zeros_like(a_g)

        dh2v = dh2_ref[...]
        dhb = dh2v.astype(BF16)
        acc = jnp.zeros((tm, D_MODEL), F32)
        for j in range(N_CHIPS):
            sl = slice(FF_BLOCK * j, FF_BLOCK * (j + 1))
            dup = (_dot_nt(dhb, wdn_ref[j]) * (2.0 * ru_ref[:, sl].astype(F32))).astype(BF16)
            dup_ref[:, sl] = dup
            acc = acc + _dot_nt(dup, wup_ref[j])
        dx, dg = _rms_bwd(h1_ref[...], g2_ref[...], acc)
        dh1_ref[...] = dh2v + dx
        a_g[0:1, :] += dg

        @pl.when(i == n - 1)
        def _():
            dg2_ref[...] = a_g[...]

    row = lambda w: pl.BlockSpec((tm, w), lambda i: (i, 0))
    return pl.pallas_call(
        body, name="bwd_mlp_x", grid=(n,),
        in_specs=[row(D_MODEL), row(D_FF), row(D_MODEL), _full(g2.shape), _resident(w_up.shape), _resident(w_down.shape)],
        out_specs=[row(D_FF), row(D_MODEL), _full((8, D_MODEL))],
        out_shape=[jax.ShapeDtypeStruct((s_len, D_FF), BF16), jax.ShapeDtypeStruct((s_len, D_MODEL), F32),
                   jax.ShapeDtypeStruct((8, D_MODEL), F32)],
        scratch_shapes=[pltpu.VMEM((8, D_MODEL), F32)],
        compiler_params=_params("arbitrary"),
    )(dh2, ru, h1, g2, w_up, w_down)


def _bwd_mlp_w(z2, dup, ru, dh2, tk):
    s_len = z2.shape[0]
    n = s_len // tk

    def body(z2_ref, dup_ref, ru_ref, dh2_ref, dwup_ref, dwdn_ref, a_up, a_dn):
        t = pl.program_id(1)

        @pl.when(t == 0)
        def _():
            a_up[...] = jnp.zeros_like(a_up)
            a_dn[...] = jnp.zeros_like(a_dn)

        ruv = ru_ref[...]
        a_up[...] += _dot_tn(z2_ref[...], dup_ref[...])
        a_dn[...] += _dot_tn(ruv * ruv, dh2_ref[...])

        @pl.when(t == n - 1)
        def _():
            dwup_ref[0] = a_up[...].astype(BF16)
            dwdn_ref[0] = a_dn[...].astype(BF16)

    tile = pl.BlockSpec((tk, D_MODEL), lambda j, t: (t, 0))
    ffb = pl.BlockSpec((tk, FF_BLOCK), lambda j, t: (t, j))
    return pl.pallas_call(
        body, name="bwd_mlp_w", grid=(N_CHIPS, n),
        in_specs=[tile, ffb, ffb, tile],
        out_specs=[pl.BlockSpec((1, D_MODEL, FF_BLOCK), lambda j, t: (j, 0, 0)),
                   pl.BlockSpec((1, FF_BLOCK, D_MODEL), lambda j, t: (j, 0, 0))],
        out_shape=[jax.ShapeDtypeStruct((N_CHIPS, D_MODEL, FF_BLOCK), BF16),
                   jax.ShapeDtypeStruct((N_CHIPS, FF_BLOCK, D_MODEL), BF16)],
        scratch_shapes=[pltpu.VMEM((D_MODEL, FF_BLOCK), F32), pltpu.VMEM((FF_BLOCK, D_MODEL), F32)],
        compiler_params=_params("arbitrary", "arbitrary"),
    )(z2, dup, ru, dh2)


MIX_VEC_ROWS = 16


def _bwd_mix(dh1, proj, h, cat, pool_w, pool_b, pool_scale, conv_w, conv_b, wa, ba, wx, bx, lru_l, w_out, tm, sub, early):
    s_len = dh1.shape[0]
    n = s_len // tm
    ng = len(POOL_WINDOWS)
    ne_ = len(early)

    def body(dh1_ref, proj_ref, h_ref, cat_ref, projh_ref, hh_ref, pw_ref, pb_ref, ps_ref, cw_ref, cb_ref,
             wa_ref, ba_ref, wx_ref, bx_ref, l_ref, wout_ref, *rest):
        early_ref = rest[:ne_]
        dproj_ref, dwout_ref, dpw_ref, dwa_ref, dwx_ref, vec_ref = rest[ne_:ne_ + 6]
        got_ref = rest[ne_ + 6:2 * ne_ + 6]
        a_out, a_pw, a_wa, a_wx, a_vec, c_g, c_dxb, c_ddc, ssem, rsem = rest[2 * ne_ + 6:]
        q = pl.program_id(0)
        i = n - 1 - q

        @pl.when(q == 0)
        def _():
            _scatter_send(early_ref, got_ref, early, ssem, rsem)
            for r in (a_out, a_pw, a_wa, a_wx, a_vec, c_g, c_dxb, c_ddc):
                r[...] = jnp.zeros_like(r)

        has_prev = (i > 0).astype(F32)
        lv = l_ref[...]
        lsl8 = LRU_C * _log_sigmoid(lv)
        a_out[...] += _dot_tn(cat_ref[...], dh1_ref[...].astype(BF16))
        cg, cdxb, cddc = c_g[0:1, :], c_dxb[...], c_ddc[...]
        vec = {}

        def add(row, v):
            vec[row] = v if row not in vec else vec[row] + v

        for k in reversed(range(tm // sub)):
            rs = slice(k * sub, (k + 1) * sub)
            t0 = i * tm + k * sub
            dcat = _dot_nt(dh1_ref[rs, :].astype(BF16), wout_ref[...])
            dy_pool = dcat[:, :D_POOL]
            dy_lru = dcat[:, D_POOL:]

            proj = proj_ref[rs, :]
            u_pool = proj[:, :D_POOL]
            u_lru = proj[:, D_POOL:D_POOL + D_LRU]
            u_gate = proj[:, D_POOL + D_LRU:]
            if k > 0:
                halo = proj_ref[k * sub - POOL_HALO:k * sub, :]
                h_prev_row = h_ref[k * sub - 1:k * sub, :]
            else:
                halo = projh_ref[...] * has_prev
                h_prev_row = hh_ref[7:8, :] * has_prev

            d, inv = _pool_diff(u_pool, halo[:, :D_POOL], t0)
            db = d.astype(BF16)
            ypre = jnp.concatenate(
                [_dot(db[:, POOL_GROUP * g:POOL_GROUP * (g + 1)], pw_ref[g]) for g in range(ng)], axis=1) + pb_ref[...]
            dyp = dy_pool * ps_ref[...]
            dypb = dyp.astype(BF16)
            dds = []
            for g in range(ng):
                sl = slice(POOL_GROUP * g, POOL_GROUP * (g + 1))
                a_pw[g] += _dot_tn(db[:, sl], dypb[:, sl])
                dds.append(_dot_nt(dypb[:, sl], pw_ref[g]))
            du_pool, ddc = _pool_diff_bwd(jnp.concatenate(dds, axis=1), inv, cddc)
            cddc = ddc[:POOL_HALO]
            add(0, jnp.sum(dyp, axis=0, keepdims=True))
            add(1, jnp.sum(dy_pool * ypre, axis=0, keepdims=True))

            taps = _conv_taps(u_lru, halo[POOL_HALO - CONV_HALO:, D_POOL:D_POOL + D_LRU])
            xb = cb_ref[...]
            for c in range(CONV_WIDTH):
                xb = xb + taps[c] * cw_ref[c:c + 1, :]
            r, ig, a, mult, first = _lru_gates(xb, wa_ref[...], ba_ref[...], wx_ref[...], bx_ref[...], lsl8, t0)
            hv = h_ref[rs, :]
            gl, th = _gelu(u_gate)
            du_gate = dy_lru * hv * _gelu_grad(u_gate, th)
            last = _rows(a.shape, 0) == sub - 1
            a_next = jnp.where(last, 1.0, pltpu.roll(a, sub - 1, 0))
            pa, gb = _scan_rev(a_next, dy_lru * gl)
            gh = gb + pa * cg
            cg = a[0:1, :] * gh[0:1, :]
            h_prev = jnp.where(_rows(hv.shape, 0) == 0, h_prev_row, pltpu.roll(hv, 1, 0))
            gix = gh * ig * xb
            dla = gh * h_prev * a - jnp.where(first, 0.0, gix * a * a / mult)
            dpre_r = dla * lsl8 * r * (1.0 - r)
            dpre_i = gh * mult * xb * ig * (1.0 - ig)
            dprb = dpre_r.astype(BF16)
            dpib = dpre_i.astype(BF16)
            xbb = xb.astype(BF16)
            a_wa[...] += _dot_tn(xbb, dprb)
            a_wx[...] += _dot_tn(xbb, dpib)
            dxb = gh * mult * ig + _dot_nt(dprb, wa_ref[...]) + _dot_nt(dpib, wx_ref[...])
            add(2, jnp.sum(dxb, axis=0, keepdims=True))
            add(3, jnp.sum(dpre_r, axis=0, keepdims=True))
            add(4, jnp.sum(dpre_i, axis=0, keepdims=True))
            add(5, jnp.sum(dla * r, axis=0, keepdims=True))
            ext = jnp.concatenate([dxb, cdxb], axis=0)
            cdxb = dxb[:CONV_HALO]
            ne = sub + CONV_HALO
            du_lru = dxb * cw_ref[CONV_WIDTH - 1:CONV_WIDTH, :]
            for c in range(CONV_WIDTH):
                add(8 + c, jnp.sum(dxb * taps[c], axis=0, keepdims=True))
                if c < CONV_WIDTH - 1:
                    du_lru = du_lru + pltpu.roll(ext, ne - (CONV_WIDTH - 1 - c), 0)[:sub] * cw_ref[c:c + 1, :]
            dproj_ref[rs, :] = jnp.concatenate([du_pool, du_lru, du_gate], axis=1).astype(BF16)
        c_g[...] = jnp.broadcast_to(cg, c_g.shape)
        c_dxb[...] = cdxb
        c_ddc[...] = cddc
        for row, v in vec.items():
            a_vec[row:row + 1, :] += v

        @pl.when(q == n - 1)
        def _():
            dwout_ref[...] = a_out[...].astype(BF16)
            dpw_ref[...] = a_pw[...]
            dwa_ref[...] = a_wa[...]
            dwx_ref[...] = a_wx[...]
            vec_ref[...] = a_vec[...]
            vec_ref[5:6, :] = a_vec[5:6, :] * (LRU_C * _sigmoid(-lv))
            _scatter_done(got_ref, early, ssem, rsem)

    rev =lambda w: pl.BlockSpec((tm, w), lambda q: (n - 1 - q, 0))
    halo_p = pl.BlockSpec((POOL_HALO, D_IN_PROJ), lambda q: (jnp.maximum((n - 1 - q) * (tm // POOL_HALO) - 1, 0), 0))
    halo_h = pl.BlockSpec((8, D_LRU), lambda q: (jnp.maximum((n - 1 - q) * (tm // 8) - 1, 0), 0))
    wts = [pool_w, pool_b, pool_scale, conv_w, conv_b, wa, ba, wx, bx, lru_l, w_out]
    outs = pl.pallas_call(
        body, name="bwd_mix", grid=(n,),
        in_specs=[rev(D_MODEL), rev(D_IN_PROJ), rev(D_LRU), rev(D_MODEL), halo_p, halo_h] + [_resident(a.shape) for a in wts]
        + [ANY] * ne_,
        out_specs=[rev(D_IN_PROJ), _full((D_MODEL, D_MODEL)), _full((ng, POOL_GROUP, POOL_GROUP)),
                   _full((D_LRU, D_LRU)), _full((D_LRU, D_LRU)), _full((MIX_VEC_ROWS, D_LRU))] + [ANY] * ne_,
        out_shape=[jax.ShapeDtypeStruct((s_len, D_IN_PROJ), BF16), jax.ShapeDtypeStruct((D_MODEL, D_MODEL), BF16),
                   jax.ShapeDtypeStruct((ng, POOL_GROUP, POOL_GROUP), F32), jax.ShapeDtypeStruct((D_LRU, D_LRU), F32),
                   jax.ShapeDtypeStruct((D_LRU, D_LRU), F32), jax.ShapeDtypeStruct((MIX_VEC_ROWS, D_LRU), F32)]
        + _scatter_shapes(early),
        scratch_shapes=[pltpu.VMEM((D_MODEL, D_MODEL), F32), pltpu.VMEM((ng, POOL_GROUP, POOL_GROUP), F32),
                        pltpu.VMEM((D_LRU, D_LRU), F32), pltpu.VMEM((D_LRU, D_LRU), F32),
                        pltpu.VMEM((MIX_VEC_ROWS, D_LRU), F32), pltpu.VMEM((8, D_LRU), F32),
                        pltpu.VMEM((CONV_HALO, D_LRU), F32), pltpu.VMEM((POOL_HALO, D_POOL), F32)] + _scatter_sems(ne_),
        compiler_params=_params("arbitrary"),
    )(dh1, proj, h, cat, proj, h, *wts, *early)
    return outs[:6], outs[6:]


def _bwd_in(dproj, z1, x, dh1, g1, w_in, tm):
    s_len = x.shape[0]
    n = s_len // tm
    cb = D_IN_PROJ // N_CHIPS

    def body(dp_ref, z1_ref, x_ref, dh1_ref, g1_ref, win_ref, dx_ref, dwin_ref, dg1_ref, a_w, a_g):
        i = pl.program_id(0)

        @pl.when(i == 0)
        def _():
            a_w[...] = jnp.zeros_like(a_w)
            a_g[...] = jnp.zeros_like(a_g)

        dp = dp_ref[...]
        zb = z1_ref[...]
        dz = jnp.zeros((tm, D_MODEL), F32)
        for j in range(N_CHIPS):
            dpj = dp[:, cb * j:cb * (j + 1)]
            dz = dz + _dot_nt(dpj, win_ref[j])
            a_w[j] += _dot_tn(zb, dpj)
        dx, dg = _rms_bwd(x_ref[...], g1_ref[...], dz)
        dx_ref[...] = dh1_ref[...] + dx
        a_g[0:1, :] += dg

        @pl.when(i == n - 1)
        def _():
            dwin_ref[...] = a_w[...].astype(BF16)
            dg1_ref[...] = a_g[...]

    row = lambda w: pl.BlockSpec((tm, w), lambda i: (i, 0))
    return pl.pallas_call(
        body, name="bwd_in", grid=(n,),
        in_specs=[row(D_IN_PROJ), row(D_MODEL), row(D_MODEL), row(D_MODEL), _resident(g1.shape), _resident(w_in.shape)],
        out_specs=[row(D_MODEL), _full(w_in.shape), _full((8, D_MODEL))],
        out_shape=[jax.ShapeDtypeStruct((s_len, D_MODEL), F32), jax.ShapeDtypeStruct(w_in.shape, BF16),
                   jax.ShapeDtypeStruct((8, D_MODEL), F32)],
        scratch_shapes=[pltpu.VMEM(w_in.shape, F32), pltpu.VMEM((8, D_MODEL), F32)],
        compiler_params=_params("arbitrary"),
    )(dproj, z1, x, dh1, g1, w_in)


def _place():
    x, y, c = lax.axis_index("x"), lax.axis_index("y"), lax.axis_index("c")
    chips = [(1 - x, y), (x, 1 - y), (1 - x, 1 - y)]
    return x, y, c, chips


def _rcopy(src, dst, ssem, rsem, dev):
    return pltpu.make_async_remote_copy(src_ref=src, dst_ref=dst, send_sem=ssem, recv_sem=rsem,
                                        device_id=dev, device_id_type=MESH)


ANY = pl.BlockSpec(memory_space=pl.ANY)
COPY_CHUNK_BYTES = 128 * 1024
ROW_ALIGN = 16


def _row_chunks(rows, row_bytes):
    per = max(ROW_ALIGN, (COPY_CHUNK_BYTES // row_bytes) // ROW_ALIGN * ROW_ALIGN)
    return [(r0, min(per, rows - r0)) for r0 in range(0, rows, per)]


def _row_bytes(a):
    return a.shape[-1] * jnp.dtype(a.dtype).itemsize


def _stack_own(shards, dtypes, pos, steps):
    nw = len(shards)

    def body(pos_ref, *refs):
        for w in range(nw):
            refs[nw + w][0] = refs[w][...].astype(dtypes[w])

    def split(s):
        return s.shape[0] % (steps * ROW_ALIGN) == 0

    ins = [pl.BlockSpec((s.shape[0] // steps, s.shape[1]), lambda i, p: (i, 0)) if split(s)
           else pl.BlockSpec(s.shape, lambda i, p: (0, 0)) for s in shards]
    outs = [pl.BlockSpec((1, s.shape[0] // steps, s.shape[1]), lambda i, p: (p[0], i, 0)) if split(s)
            else pl.BlockSpec((1,) + s.shape, lambda i, p: (p[0], 0, 0)) for s in shards]
    return pl.pallas_call(
        body, name="stack_own",
        grid_spec=pltpu.PrefetchScalarGridSpec(num_scalar_prefetch=1, grid=(steps,), in_specs=ins, out_specs=outs),
        out_shape=[jax.ShapeDtypeStruct((N_CHIPS,) + s.shape, d) for s, d in zip(shards, dtypes)],
        compiler_params=_params("arbitrary"),
    )(pos, *shards)


def _gather_send(outs, stacks, ssem, rsem):
    x, y, c, chips = _place()
    me = 2 * x + y
    for w, st in enumerate(stacks):
        half = st.shape[1] // 2
        for s, (px, py) in enumerate(chips):
            for r0, rs in _row_chunks(half, _row_bytes(st)):
                piece = outs[w].at[me, pl.ds(c * half + r0, rs)]
                _rcopy(piece, piece, ssem.at[w, s], rsem.at[w, s], (px, py, c)).start()


def _gather_pass(outs, stacks, ssem, rsem):
    x, y, c, chips = _place()
    sib = (x, y, 1 - c)
    for w, st in enumerate(stacks):
        half = st.shape[1] // 2
        for s, (px, py) in enumerate(chips):
            blk = outs[w].at[2 * px + py, pl.ds(c * half, half)]
            _rcopy(blk, blk, ssem.at[w, s], rsem.at[w, s], sib).wait_recv()
            for r0, rs in _row_chunks(half, _row_bytes(st)):
                piece = outs[w].at[2 * px + py, pl.ds(c * half + r0, rs)]
                _rcopy(piece, piece, ssem.at[w, 3 + s], rsem.at[w, 3 + s], sib).start()


def _gather_done(outs, stacks, ssem, rsem):
    x, y, c, chips = _place()
    sib = (x, y, 1 - c)
    for w, st in enumerate(stacks):
        half = st.shape[1] // 2
        for s, (px, py) in enumerate(chips):
            blk = outs[w].at[2 * px + py, pl.ds((1 - c) * half, half)]
            _rcopy(blk, blk, ssem.at[w, 3 + s], rsem.at[w, 3 + s], sib).wait_recv()
    for w, st in enumerate(stacks):
        half = st.shape[1] // 2
        blk = outs[w].at[0, pl.ds(0, half)]
        for s in range(6):
            _rcopy(blk, blk, ssem.at[w, s], rsem.at[w, s], sib).wait_send()


def _gather_sems(nw):
    return [pltpu.SemaphoreType.DMA((nw, 6)), pltpu.SemaphoreType.DMA((nw, 6))]


def _gather_weights(stacks):
    nw = len(stacks)

    def body(*refs):
        outs = refs[nw:2 * nw]
        ssem, rsem = refs[2 * nw:]
        _gather_send(outs, stacks, ssem, rsem)
        _gather_pass(outs, stacks, ssem, rsem)
        _gather_done(outs, stacks, ssem, rsem)

    return pl.pallas_call(
        body, name="gather_weights",
        in_specs=[ANY] * nw, out_specs=[ANY] * nw,
        out_shape=[jax.ShapeDtypeStruct(s.shape, s.dtype) for s in stacks],
        input_output_aliases={w: w for w in range(nw)},
        scratch_shapes=_gather_sems(nw),
    )(*stacks)


def _swap_halves(grads, tag):
    nw = len(grads)

    def body(*refs):
        ins, got = refs[:nw], refs[nw:2 * nw]
        ssem, rsem = refs[2 * nw:]
        x, y, c, _ = _place()
        cps = []
        for w in range(nw):
            hr = grads[w].shape[1] // 2
            for k in range(N_CHIPS):
                for r0, rs in _row_chunks(hr, _row_bytes(grads[w])):
                    _rcopy(ins[w].at[k, pl.ds((1 - c) * hr + r0, rs)], got[w].at[k, pl.ds(r0, rs)],
                           ssem.at[w], rsem.at[w], (x, y, 1 - c)).start()
            cps.append(_rcopy(got[w], got[w], ssem.at[w], rsem.at[w], (x, y, 1 - c)))
        for cp in cps:
            cp.wait()

    return pl.pallas_call(
        body, name="swap_halves_" + tag,
        in_specs=[ANY] * nw, out_specs=[ANY] * nw,
        out_shape=[jax.ShapeDtypeStruct((g.shape[0], g.shape[1] // 2, g.shape[2]), g.dtype) for g in grads],
        scratch_shapes=[pltpu.SemaphoreType.DMA((nw,)), pltpu.SemaphoreType.DMA((nw,))],
    )(*grads)


def _add_pairs(grads, got, pos, steps, tag):
    nw = len(grads)

    def body(pos_ref, *refs):
        for w in range(nw):
            refs[2 * nw + w][...] = (refs[w][...].astype(F32) + refs[nw + w][...].astype(F32)).astype(BF16)

    blk = lambda a: (a.shape[0], a.shape[1] // steps, a.shape[2])
    own = [pl.BlockSpec(blk(a), lambda i, p: (0, p[1] * steps + i, 0)) for a in got]
    rec = [pl.BlockSpec(blk(a), lambda i, p: (0, i, 0)) for a in got]
    return pl.pallas_call(
        body, name="add_pairs_" + tag,
        grid_spec=pltpu.PrefetchScalarGridSpec(num_scalar_prefetch=1, grid=(steps,), in_specs=own + rec, out_specs=rec),
        out_shape=[jax.ShapeDtypeStruct(a.shape, BF16) for a in got],
        compiler_params=_params("arbitrary"),
    )(pos, *grads, *got)


def _scatter_send(ins, got, parts, ssem, rsem):
    x, y, c, chips = _place()
    for w, p in enumerate(parts):
        for s, (px, py) in enumerate(chips):
            for r0, rs in _row_chunks(p.shape[1], _row_bytes(p)):
                _rcopy(ins[w].at[2 * px + py, pl.ds(r0, rs)], got[w].at[s, pl.ds(r0, rs)],
                       ssem.at[w, s], rsem.at[w, s], (px, py, c)).start()


def _scatter_done(got, parts, ssem, rsem):
    x, y, c, chips = _place()
    for w in range(len(parts)):
        for s, (px, py) in enumerate(chips):
            _rcopy(got[w].at[s], got[w].at[s], ssem.at[w, s], rsem.at[w, s], (px, py, c)).wait()


def _scatter_sems(nw):
    return [pltpu.SemaphoreType.DMA((nw, 3)), pltpu.SemaphoreType.DMA((nw, 3))]


def _scatter_shapes(parts):
    return [jax.ShapeDtypeStruct((3,) + p.shape[1:], p.dtype) for p in parts]


def _scatter_chips(parts):
    nw = len(parts)

    def body(*refs):
        ins, got = refs[:nw], refs[nw:2 * nw]
        ssem, rsem = refs[2 * nw:]
        _scatter_send(ins, got, parts, ssem, rsem)
        _scatter_done(got, parts, ssem, rsem)

    return pl.pallas_call(
        body, name="scatter_chips",
        in_specs=[ANY] * nw, out_specs=[ANY] * nw, out_shape=_scatter_shapes(parts),
        scratch_shapes=_scatter_sems(nw),
    )(*parts)


def _sum_chips(parts, got, pos, steps):
    nw = len(parts)

    def body(pos_ref, *refs):
        for w in range(nw):
            acc = refs[w][0].astype(F32)
            for s in range(3):
                acc = acc + refs[nw + w][s].astype(F32)
            refs[2 * nw + w][...] = acc

    own = [pl.BlockSpec((1, p.shape[1] // steps, p.shape[2]), lambda i, ps: (ps[0], i, 0)) for p in parts]
    rec = [pl.BlockSpec((3, p.shape[1] // steps, p.shape[2]), lambda i, ps: (0, i, 0)) for p in parts]
    outs = [pl.BlockSpec((p.shape[1] // steps, p.shape[2]), lambda i, ps: (ps[1] * steps + i, 0)) for p in parts]
    return pl.pallas_call(
        body, name="sum_chips",
        grid_spec=pltpu.PrefetchScalarGridSpec(num_scalar_prefetch=1, grid=(steps,), in_specs=own + rec, out_specs=outs),
        out_shape=[jax.ShapeDtypeStruct((2 * p.shape[1], p.shape[2]), F32) for p in parts],
        compiler_params=_params("arbitrary"),
    )(pos, *parts, *got)


def _join_halves(shards):
    nw = len(shards)

    def body(*refs):
        outs = refs[nw:2 * nw]
        ssem, rsem = refs[2 * nw:]
        x, y, c, _ = _place()
        cps = []
        for w in range(nw):
            hr = shards[w].shape[0] // 2
            for r0, rs in _row_chunks(hr, _row_bytes(shards[w])):
                piece = outs[w].at[pl.ds(c * hr + r0, rs)]
                _rcopy(piece, piece, ssem.at[w], rsem.at[w], (x, y, 1 - c)).start()
            mine = outs[w].at[pl.ds(c * hr, hr)]
            cps.append(_rcopy(mine, mine, ssem.at[w], rsem.at[w], (x, y, 1 - c)))
        for cp in cps:
            cp.wait()

    return pl.pallas_call(
        body, name="join_halves",
        in_specs=[ANY] * nw, out_specs=[ANY] * nw,
        out_shape=[jax.ShapeDtypeStruct(s.shape, F32) for s in shards],
        input_output_aliases={w: w for w in range(nw)},
        scratch_shapes=[pltpu.SemaphoreType.DMA((nw,)), pltpu.SemaphoreType.DMA((nw,))],
    )(*shards)


def _allreduce_small(packed):
    shape = packed.shape

    def body(p_ref, out_ref, rbuf, ssem, rsem):
        x, y, c, _ = _place()
        out_ref[...] = p_ref[...]
        for st, peer in enumerate([(x, y, 1 - c), (1 - x, y, c), (x, 1 - y, c)]):
            cp = _rcopy(out_ref, rbuf.at[st], ssem.at[st], rsem.at[st], peer)
            cp.start()
            cp.wait()
            out_ref[...] = out_ref[...] + rbuf[st]

    vm = pl.BlockSpec(memory_space=pltpu.VMEM)
    return pl.pallas_call(
        body, name="allreduce_small",
        in_specs=[vm], out_specs=vm, out_shape=jax.ShapeDtypeStruct(shape, F32),
        scratch_shapes=[pltpu.VMEM((3,) + shape, F32), pltpu.SemaphoreType.DMA((3,)), pltpu.SemaphoreType.DMA((3,))],
        compiler_params=pltpu.CompilerParams(vmem_limit_bytes=VMEM_LIMIT),
    )(packed)


def _adamw_math(w, g, m, v):
    m = ADAM_B1 * m + (1.0 - ADAM_B1) * g
    v = ADAM_B2 * v + (1.0 - ADAM_B2) * (g * g)
    delta = -ADAM_LR * ((m * ADAM_C1) / (jnp.sqrt(v * ADAM_C2) + ADAM_EPS) + ADAM_WD * w)
    return delta, m, v


def _adamw(ws, gs, ms, vs, steps, name):
    nw = len(ws)

    def body(*refs):
        for k in range(nw):
            d, m, v = _adamw_math(refs[k][...], refs[nw + k][...], refs[2 * nw + k][...], refs[3 * nw + k][...])
            refs[4 * nw + k][...] = d
            refs[5 * nw + k][...] = m
            refs[6 * nw + k][...] = v

    specs = [pl.BlockSpec((a.shape[0] // steps, a.shape[1]), lambda i: (i, 0)) for a in ws]
    shapes = [jax.ShapeDtypeStruct(a.shape, F32) for a in ws]
    outs = pl.pallas_call(
        body, name=name, grid=(steps,),
        in_specs=specs * 4, out_specs=specs * 3, out_shape=shapes * 3,
        compiler_params=_params("arbitrary"),
    )(*ws, *gs, *ms, *vs)
    return outs[:nw], outs[nw:2 * nw], outs[2 * nw:]


SMALL = ["norm_mix_g", "pool_w", "pool_b", "pool_scale", "conv_b", "gate_a_w", "gate_a_b", "gate_x_w", "gate_x_b",
         "lru_L", "norm_mlp_g", "norm_ple_g", "b_ple_gate", "norm_final_g"]
BIG = ["w_in", "w_out", "w_up", "w_down", "w_ple_gate", "w_ple_proj"]
ORDER = ["norm_mix_g", "w_in", "pool_w", "pool_b", "pool_scale", "conv_w", "conv_b", "gate_a_w", "gate_a_b", "gate_x_w",
         "gate_x_b", "lru_L", "w_out", "norm_mlp_g", "w_up", "w_down", "norm_ple_g", "w_ple_gate", "b_ple_gate",
         "w_ple_proj", "norm_final_g"]
LANES = 128


def _block_diag(w):
    eye = jnp.eye(LRU_HEADS, dtype=w.dtype)
    return jnp.einsum("hij,hk->hikj", w, eye).reshape(D_LRU, D_LRU)


def _diag_blocks(full):
    f = full.reshape(LRU_HEADS, LRU_BLOCK, LRU_HEADS, LRU_BLOCK)
    return jnp.stack([f[h, :, h, :] for h in range(LRU_HEADS)])


def _rows128(a):
    return a.reshape(-1, LANES)


def _pad8(a):
    r = (-a.shape[0]) % 8
    return jnp.pad(a, ((0, r), (0, 0))) if r else a


def kernel(x, p, norm_mix_g, w_in, pool_w, pool_b, pool_scale, conv_w, conv_b, gate_a_w, gate_a_b, gate_x_w, gate_x_b, lru_L, w_out, norm_mlp_g, w_up, w_down, norm_ple_g, w_ple_gate, b_ple_gate, w_ple_proj, norm_final_g, loss_target, m_norm_mix_g, m_w_in, m_pool_w, m_pool_b, m_pool_scale, m_conv_w, m_conv_b, m_gate_a_w, m_gate_a_b, m_gate_x_w, m_gate_x_b, m_lru_L, m_w_out, m_norm_mlp_g, m_w_up, m_w_down, m_norm_ple_g, m_w_ple_gate, m_b_ple_gate, m_w_ple_proj, m_norm_final_g, v_norm_mix_g, v_w_in, v_pool_w, v_pool_b, v_pool_scale, v_conv_w, v_conv_b, v_gate_a_w, v_gate_a_b, v_gate_x_w, v_gate_x_b, v_lru_L, v_w_out, v_norm_mlp_g, v_w_up, v_w_down, v_norm_ple_g, v_w_ple_gate, v_b_ple_gate, v_w_ple_proj, v_norm_final_g):
    W = dict(norm_mix_g=norm_mix_g, w_in=w_in, pool_w=pool_w, pool_b=pool_b, pool_scale=pool_scale, conv_w=conv_w,
             conv_b=conv_b, gate_a_w=gate_a_w, gate_a_b=gate_a_b, gate_x_w=gate_x_w, gate_x_b=gate_x_b, lru_L=lru_L,
             w_out=w_out, norm_mlp_g=norm_mlp_g, w_up=w_up, w_down=w_down, norm_ple_g=norm_ple_g,
             w_ple_gate=w_ple_gate, b_ple_gate=b_ple_gate, w_ple_proj=w_ple_proj, norm_final_g=norm_final_g)
    M = dict(norm_mix_g=m_norm_mix_g, w_in=m_w_in, pool_w=m_pool_w, pool_b=m_pool_b, pool_scale=m_pool_scale,
             conv_w=m_conv_w, conv_b=m_conv_b, gate_a_w=m_gate_a_w, gate_a_b=m_gate_a_b, gate_x_w=m_gate_x_w,
             gate_x_b=m_gate_x_b, lru_L=m_lru_L, w_out=m_w_out, norm_mlp_g=m_norm_mlp_g, w_up=m_w_up, w_down=m_w_down,
             norm_ple_g=m_norm_ple_g, w_ple_gate=m_w_ple_gate, b_ple_gate=m_b_ple_gate, w_ple_proj=m_w_ple_proj,
             norm_final_g=m_norm_final_g)
    V = dict(norm_mix_g=v_norm_mix_g, w_in=v_w_in, pool_w=v_pool_w, pool_b=v_pool_b, pool_scale=v_pool_scale,
             conv_w=v_conv_w, conv_b=v_conv_b, gate_a_w=v_gate_a_w, gate_a_b=v_gate_a_b, gate_x_w=v_gate_x_w,
             gate_x_b=v_gate_x_b, lru_L=v_lru_L, w_out=v_w_out, norm_mlp_g=v_norm_mlp_g, w_up=v_w_up, w_down=v_w_down,
             norm_ple_g=v_norm_ple_g, w_ple_gate=v_w_ple_gate, b_ple_gate=v_b_ple_gate, w_ple_proj=v_w_ple_proj,
             norm_final_g=v_norm_final_g)

    s_len = x.shape[1]
    sub_mix = min(256, s_len)
    tm = min(512, s_len)
    chip = (2 * lax.axis_index("x") + lax.axis_index("y")).astype(jnp.int32)
    pos = jnp.stack([chip, lax.axis_index("c").astype(jnp.int32)])

    shards = [w_in[0], w_out[0], w_up[0], w_down[0], w_ple_gate[0], w_ple_proj[0], jnp.pad(conv_w[0], ((0, 12), (0, 0)))]
    st_in, st_out, st_up, st_dn, st_pg, st_pp, st_cw = _stack_own(shards, [BF16] * 6 + [F32], pos, 8)
    win_g, wout_g, cw_g = _gather_weights([st_in, st_out, st_cw])
    wout_f = wout_g.reshape(D_MODEL, D_MODEL)
    cw_f = jnp.transpose(cw_g[:, :CONV_WIDTH], (1, 0, 2)).reshape(CONV_WIDTH, D_LRU)
    pw_b = pool_w[0].astype(BF16)
    wa_b = _block_diag(gate_a_w[0]).astype(BF16)
    wx_b = _block_diag(gate_x_w[0]).astype(BF16)
    pb_r = pool_b.reshape(1, D_POOL)
    ba_r = gate_a_b.reshape(1, D_LRU)
    bx_r = gate_x_b.reshape(1, D_LRU)
    g4 = norm_final_g.reshape(1, D_MODEL)
    mix_w = (pw_b, pb_r, pool_scale, cw_f, conv_b, wa_b, ba_r, wx_b, bx_r, lru_L, wout_f)

    xs, ps, ts = x[0], p[0, 0], loss_target[0]
    (z1, proj, hst, cat, h1), (wup_g, wdn_g, wpg_g, wpp_g) = _fwd_mix(
        xs, norm_mix_g, win_g, *mix_w, tm, sub_mix, [st_up, st_dn, st_pg, st_pp])
    wpg_f = wpg_g.reshape(D_MODEL, D_MODEL)
    wpp_f = jnp.transpose(wpp_g, (1, 0, 2)).reshape(PLE_DIM, D_MODEL)
    z2, ru, h2 = _fwd_mlp(h1, norm_mlp_g, wup_g, wdn_g, tm)
    dh2, dh2b, d_wpg, d_wpp, head_vec = _head(h2, ps, ts, norm_ple_g, wpg_f, b_ple_gate, wpp_f, g4, tm)
    dup, dh1, mlp_vec = _bwd_mlp_x(dh2, ru, h1, norm_mlp_g, wup_g, wdn_g, tm)
    d_wup, d_wdn = _bwd_mlp_w(z2, dup, ru, dh2b, tm)
    early = [d_wup, d_wdn, d_wpg.reshape(N_CHIPS, D_MODEL // N_CHIPS, D_MODEL), d_wpp]
    pair_e = _add_pairs(early, _swap_halves(early, "early"), pos, 8, "early")
    (dproj, d_wout, d_pw, d_wa, d_wx, mix_vec), got_e = _bwd_mix(dh1, proj, hst, cat, *mix_w, tm, sub_mix, pair_e)
    dx, d_win, in_vec = _bwd_in(dproj, z1, xs, dh1, norm_mix_g, win_g, tm)

    last = [d_win, d_wout.reshape(N_CHIPS, D_MODEL // N_CHIPS, D_MODEL)]
    pair_l = _add_pairs(last, _swap_halves(last, "last"), pos, 8, "last")
    got_l = _scatter_chips(pair_l)
    g_big = _join_halves(_sum_chips(pair_l + pair_e, got_l + got_e, pos, 8))

    g_small = {
        "norm_mix_g": in_vec[0:1], "pool_w": d_pw, "pool_b": mix_vec[0:1], "pool_scale": mix_vec[1:2],
        "conv_b": mix_vec[2:3], "gate_a_w": _diag_blocks(d_wa), "gate_a_b": mix_vec[3:4],
        "gate_x_w": _diag_blocks(d_wx), "gate_x_b": mix_vec[4:5], "lru_L": mix_vec[5:6], "norm_mlp_g": mlp_vec[0:1],
        "norm_ple_g": head_vec[0:1], "b_ple_gate": head_vec[2:3], "norm_final_g": head_vec[1:2],
    }
    d_cw = jnp.transpose(mix_vec[8:8 + CONV_WIDTH].reshape(CONV_WIDTH, N_CHIPS, LANES), (1, 0, 2)).reshape(-1, LANES)
    pieces = [_pad8(_rows128(g_small[k])) for k in SMALL] + [d_cw, _pad8(head_vec[3:4, :LANES])]
    offs = [0]
    for pc in pieces:
        offs.append(offs[-1] + pc.shape[0])
    red = _allreduce_small(jnp.concatenate(pieces, axis=0))
    loss = red[offs[-2], 0]
    g_cw = lax.dynamic_slice(red, (offs[len(SMALL)] + CONV_WIDTH * chip, 0), (CONV_WIDTH, LANES))

    def packed(src):
        return jnp.concatenate([_pad8(_rows128(src[k])) for k in SMALL] + [_pad8(src["conv_w"][0])], axis=0)

    n_small = offs[len(SMALL)]
    g_pack = jnp.concatenate([red[:n_small], _pad8(g_cw)], axis=0)
    (d_pack,), (m_pack,), (v_pack,) = _adamw([packed(W)], [g_pack], [packed(M)], [packed(V)], 1, "adamw_small")

    big2d = lambda src: [src[k][0] for k in BIG]
    d_big, m_big, v_big = _adamw(big2d(W), g_big, big2d(M), big2d(V), 8, "adamw_big")

    def unpack(pack, big_list):
        out = {}
        for idx, k in enumerate(SMALL):
            n_el = W[k].size
            out[k] = pack[offs[idx]:offs[idx + 1]].reshape(-1)[:n_el].reshape(W[k].shape)
        out["conv_w"] = pack[n_small:n_small + CONV_WIDTH].reshape(W["conv_w"].shape)
        for k, a in zip(BIG, big_list):
            out[k] = a.reshape(W[k].shape)
        return out

    grads = unpack(g_pack, g_big)
    deltas = unpack(d_pack, d_big)
    new_m = unpack(m_pack, m_big)
    new_v = unpack(v_pack, v_big)
    return (loss, dx[None], *[grads[k] for k in ORDER], *[deltas[k] for k in ORDER],
            *[new_m[k] for k in ORDER], *[new_v[k] for k in ORDER])
```

```python
import functools

import jax
import jax.numpy as jnp
from jax import lax
from jax.experimental import pallas as pl
from jax.experimental.pallas import tpu as pltpu

F32 = jnp.float32
BF16 = jnp.bfloat16
MESH = pl.DeviceIdType.MESH

D_MODEL = 1024
D_POOL = 512
D_LRU = 512
POOL_WINDOWS = (2, 4, 8, 16)
POOL_GROUP = 128
POOL_HALO = 16
CONV_WIDTH = 4
CONV_HALO = 8
PASS_STEPS = 2
LRU_HEADS = 8
LRU_BLOCK = 64
LRU_C = 8.0
D_FF = 4096
PLE_DIM = 256
D_IN_PROJ = 1536
RMS_EPS = 1e-6
N_CHIPS = 4
FF_BLOCK = D_FF // N_CHIPS

ADAM_LR = 0.001
ADAM_B1 = 0.9
ADAM_B2 = 0.999
ADAM_EPS = 1e-08
ADAM_WD = 0.01
ADAM_STEP = 10
ADAM_C1 = 1.0 / (1.0 - ADAM_B1 ** ADAM_STEP)
ADAM_C2 = 1.0 / (1.0 - ADAM_B2 ** ADAM_STEP)

VMEM_LIMIT = 56 * 1024 * 1024
GELU_C = 0.7978845608028654
GELU_A = 0.044715

NT = (((1,), (1,)), ((), ()))
TN = (((0,), (0,)), ((), ()))


def _dot(a, b):
    return jnp.dot(a, b, preferred_element_type=F32)


def _dot_nt(a, b):
    return lax.dot_general(a, b, NT, preferred_element_type=F32)


def _dot_tn(a, b):
    return lax.dot_general(a, b, TN, preferred_element_type=F32)


def _params(*sem):
    return pltpu.CompilerParams(dimension_semantics=sem, vmem_limit_bytes=VMEM_LIMIT)


def _full(shape):
    nd = len(shape)
    return pl.BlockSpec(shape, lambda *_: (0,) * nd)


def _resident(shape):
    nd = len(shape)
    return pl.BlockSpec(shape, lambda *_: (0,) * nd, pipeline_mode=pl.Buffered(1))


def _rstd(x):
    return lax.rsqrt(jnp.mean(x * x, axis=-1, keepdims=True) + RMS_EPS)


def _rms_bwd(x, g, dz):
    xr = x * _rstd(x)
    r = _rstd(x)
    dyg = dz * g
    dx = r * (dyg - xr * jnp.mean(dyg * xr, axis=-1, keepdims=True))
    dg = jnp.sum(dz * xr, axis=0, keepdims=True)
    return dx, dg


def _sigmoid(x):
    return 1.0 / (1.0 + jnp.exp(-x))


def _log_sigmoid(v):
    u = jnp.exp(-jnp.abs(v))
    w = 1.0 + u
    l1p = jnp.where(w == 1.0, u, jnp.log(w) * u / jnp.where(w == 1.0, 1.0, w - 1.0))
    return jnp.minimum(v, 0.0) - l1p


def _gelu(x):
    t = jnp.tanh(GELU_C * (x + GELU_A * x * x * x))
    return 0.5 * x * (1.0 + t), t


def _gelu_grad(x, t):
    return 0.5 * (1.0 + t) + 0.5 * x * (1.0 - t * t) * GELU_C * (1.0 + 3.0 * GELU_A * x * x)


def _rows(shape, t0):
    return lax.broadcasted_iota(jnp.int32, shape, 0) + t0


def _pool_diff(u_pool, prev, t0):
    tm = u_pool.shape[0]
    rows = _rows((tm, POOL_GROUP), t0)
    outs, invs = [], []
    for g, w in enumerate(POOL_WINDOWS):
        sl = slice(POOL_GROUP * g, POOL_GROUP * (g + 1))
        ug = u_pool[:, sl]
        s = jnp.concatenate([prev[:, sl], ug], axis=0)
        k = 1
        while k < w:
            s = s + pltpu.roll(s, k, 0)
            k *= 2
        inv = 1.0 / jnp.minimum(rows + 1, w).astype(F32)
        outs.append(s[POOL_HALO:] * inv - ug)
        invs.append(inv)
    return jnp.concatenate(outs, axis=1), jnp.concatenate(invs, axis=1)


def _pool_diff_bwd(dd, inv, nxt):
    ddc = dd * inv
    outs = []
    for g, w in enumerate(POOL_WINDOWS):
        sl = slice(POOL_GROUP * g, POOL_GROUP * (g + 1))
        s = jnp.concatenate([ddc[:, sl], nxt[:, sl]], axis=0)
        n = s.shape[0]
        k = 1
        while k < w:
            s = s + pltpu.roll(s, n - k, 0)
            k *= 2
        outs.append(s[:n - POOL_HALO] - dd[:, sl])
    return jnp.concatenate(outs, axis=1), ddc


def _conv_taps(u, prev):
    ext = jnp.concatenate([prev, u], axis=0)
    return [pltpu.roll(ext, CONV_WIDTH - 1 - k, 0)[CONV_HALO:] if k < CONV_WIDTH - 1 else u for k in range(CONV_WIDTH)]


def _scan_fwd(a, b):
    tm = a.shape[0]
    rows = _rows(a.shape, 0)
    k = 1
    while k < tm:
        ar = pltpu.roll(a, k, 0)
        br = pltpu.roll(b, k, 0)
        m = rows >= k
        b = jnp.where(m, a * br + b, b)
        a = jnp.where(m, a * ar, a)
        k *= 2
    return a, b


def _scan_rev(a, b):
    tm = a.shape[0]
    rows = _rows(a.shape, 0)
    k = 1
    while k < tm:
        ar = pltpu.roll(a, tm - k, 0)
        br = pltpu.roll(b, tm - k, 0)
        m = rows < tm - k
        b = jnp.where(m, a * br + b, b)
        a = jnp.where(m, a * ar, a)
        k *= 2
    return a, b


def _lru_gates(xb, wa, ba, wx, bx, lsl8, t0):
    xbb = xb.astype(BF16)
    r = _sigmoid(_dot(xbb, wa) + ba)
    ig = _sigmoid(_dot(xbb, wx) + bx)
    a = jnp.exp(r * lsl8)
    first = _rows(xb.shape, t0) == 0
    mult = jnp.where(first, 1.0, jnp.sqrt(1.0 - a * a))
    return r, ig, a, mult, first


def _fwd_mix(x, g1, w_in, pool_w, pool_b, pool_scale, conv_w, conv_b, wa, ba, wx, bx, lru_l, w_out, tm, sub, late):
    s_len = x.shape[0]
    n = s_len // tm
    nl = len(late)

    def body(x_ref, g1_ref, win_ref, pw_ref, pb_ref, ps_ref, cw_ref, cb_ref, wa_ref, ba_ref, wx_ref, bx_ref, l_ref,
             wout_ref, *rest):
        z1_ref, proj_ref, h_ref, cat_ref, h1_ref = rest[nl:nl + 5]
        late_ref = rest[nl + 5:2 * nl + 5]
        cpool, clru, ch, ssem, rsem = rest[2 * nl + 5:]
        i = pl.program_id(0)

        @pl.when(i == 0)
        def _():
            _gather_send(late_ref, late, ssem, rsem)
            cpool[...] = jnp.zeros_like(cpool)
            clru[...] = jnp.zeros_like(clru)
            ch[...] = jnp.zeros_like(ch)

        lsl8 = LRU_C * _log_sigmoid(l_ref[...])
        cp, cl, hc = cpool[...], clru[...], ch[7:8, :]
        def in_proj(k):
            rs = slice(k * sub, (k + 1) * sub)
            xv = x_ref[rs, :]
            zb = (xv * _rstd(xv) * g1_ref[...]).astype(BF16)
            z1_ref[rs, :] = zb
            proj = jnp.concatenate([_dot(zb, win_ref[j]) for j in range(N_CHIPS)], axis=1)
            proj_ref[rs, :] = proj
            return xv, proj

        nxt = in_proj(0)
        for k in range(tm // sub):
            rs = slice(k * sub, (k + 1) * sub)
            t0 = i * tm + k * sub
            xv, proj = nxt
            if k + 1 < tm // sub:
                nxt = in_proj(k + 1)
            u_pool = proj[:, :D_POOL]
            u_lru = proj[:, D_POOL:D_POOL + D_LRU]
            u_gate = proj[:, D_POOL + D_LRU:]

            d, _ = _pool_diff(u_pool, cp, t0)
            cp = u_pool[sub - POOL_HALO:]
            db = d.astype(BF16)
            yp = jnp.concatenate(
                [_dot(db[:, POOL_GROUP * g:POOL_GROUP * (g + 1)], pw_ref[g]) for g in range(len(POOL_WINDOWS))], axis=1)
            y_pool = (yp + pb_ref[...]) * ps_ref[...]

            taps = _conv_taps(u_lru, cl)
            cl = u_lru[sub - CONV_HALO:]
            xb = cb_ref[...]
            for q in range(CONV_WIDTH):
                xb = xb + taps[q] * cw_ref[q:q + 1, :]
            _, ig, a, mult, _ = _lru_gates(xb, wa_ref[...], ba_ref[...], wx_ref[...], bx_ref[...], lsl8, t0)
            pa, hb = _scan_fwd(a, mult * (ig * xb))
            h = hb + pa * hc
            hc = h[sub - 1:sub, :]
            h_ref[rs, :] = h
            gl, _ = _gelu(u_gate)
            cat = jnp.concatenate([y_pool, h * gl], axis=1).astype(BF16)
            cat_ref[rs, :] = cat
            h1_ref[rs, :] = xv + _dot(cat, wout_ref[...])
        cpool[...] = cp
        clru[...] = cl
        ch[...] = jnp.broadcast_to(hc, ch.shape)

        @pl.when(i == max(n - PASS_STEPS, 0))
        def _():
            _gather_pass(late_ref, late, ssem, rsem)

        @pl.when(i == n - 1)
        def _():
            _gather_done(late_ref, late, ssem, rsem)

    row = lambda w: pl.BlockSpec((tm, w), lambda i: (i, 0))
    ins = [x, g1, w_in, pool_w, pool_b, pool_scale, conv_w, conv_b, wa, ba, wx, bx, lru_l, w_out]
    outs = pl.pallas_call(
        body, name="fwd_mix", grid=(n,),
        in_specs=[row(D_MODEL)] + [_resident(a.shape) for a in ins[1:]] + [ANY] * nl,
        out_specs=[row(D_MODEL), row(D_IN_PROJ), row(D_LRU), row(D_MODEL), row(D_MODEL)] + [ANY] * nl,
        out_shape=[jax.ShapeDtypeStruct((s_len, D_MODEL), BF16), jax.ShapeDtypeStruct((s_len, D_IN_PROJ), F32),
                   jax.ShapeDtypeStruct((s_len, D_LRU), F32), jax.ShapeDtypeStruct((s_len, D_MODEL), BF16),
                   jax.ShapeDtypeStruct((s_len, D_MODEL), F32)]
        + [jax.ShapeDtypeStruct(a.shape, a.dtype) for a in late],
        input_output_aliases={len(ins) + k: 5 + k for k in range(nl)},
        scratch_shapes=[pltpu.VMEM((POOL_HALO, D_POOL), F32), pltpu.VMEM((CONV_HALO, D_LRU), F32),
                        pltpu.VMEM((8, D_LRU), F32)] + _gather_sems(nl),
        compiler_params=_params("arbitrary"),
    )(*ins, *late)
    return outs[:5], outs[5:]


def _fwd_mlp(h1, g2, w_up, w_down, tm, late):
    s_len = h1.shape[0]
    n = s_len // tm
    nl = len(late)

    def body(h1_ref, g2_ref, wup_ref, wdn_ref, *rest):
        z2_ref, ru_ref, h2_ref = rest[nl:nl + 3]
        late_ref = rest[nl + 3:2 * nl + 3]
        ssem, rsem = rest[2 * nl + 3:]
        i = pl.program_id(0)

        @pl.when(i == 0)
        def _():
            _gather_send(late_ref, late, ssem, rsem)

        hv = h1_ref[...]
        zb = (hv * _rstd(hv) * g2_ref[...]).astype(BF16)
        z2_ref[...] = zb
        acc = hv
        for j in range(N_CHIPS):
            ru = jnp.maximum(_dot(zb, wup_ref[j]), 0.0)
            ru_ref[:, FF_BLOCK * j:FF_BLOCK * (j + 1)] = ru.astype(BF16)
            acc = acc + _dot((ru * ru).astype(BF16), wdn_ref[j])
        h2_ref[...] = acc

        @pl.when(i == max(n - PASS_STEPS, 0))
        def _():
            _gather_pass(late_ref, late, ssem, rsem)

        @pl.when(i == n - 1)
        def _():
            _gather_done(late_ref, late, ssem, rsem)

    row = lambda w: pl.BlockSpec((tm, w), lambda i: (i, 0))
    outs = pl.pallas_call(
        body, name="fwd_mlp", grid=(n,),
        in_specs=[row(D_MODEL), _full(g2.shape), _resident(w_up.shape), _resident(w_down.shape)] + [ANY] * nl,
        out_specs=[row(D_MODEL), row(D_FF), row(D_MODEL)] + [ANY] * nl,
        out_shape=[jax.ShapeDtypeStruct((s_len, D_MODEL), BF16), jax.ShapeDtypeStruct((s_len, D_FF), BF16),
                   jax.ShapeDtypeStruct((s_len, D_MODEL), F32)] + [jax.ShapeDtypeStruct(a.shape, a.dtype) for a in late],
        input_output_aliases={4 + k: 3 + k for k in range(nl)},
        scratch_shapes=_gather_sems(nl),
        compiler_params=_params("arbitrary"),
    )(h1, g2, w_up, w_down, *late)
    return outs[:3], outs[3:]


def _head(h2, p, target, g3, w_pg, b_pg, w_pp, g4, tm):
    s_len = h2.shape[0]
    n = s_len // tm

    def body(h2_ref, p_ref, t_ref, g3_ref, wpg_ref, bpg_ref, wpp_ref, g4_ref,
             dh2_ref, dh2b_ref, dwpg_ref, dwpp_ref, vec_ref, a_pg, a_pp, a_vec):
        i = pl.program_id(0)

        @pl.when(i == 0)
        def _():
            a_pg[...] = jnp.zeros_like(a_pg)
            a_pp[...] = jnp.zeros_like(a_pp)
            a_vec[...] = jnp.zeros_like(a_vec)

        h2v = h2_ref[...]
        g3v = g3_ref[...]
        g4v = g4_ref[...]
        z3 = (h2v * _rstd(h2v) * g3v).astype(BF16)
        gate = _sigmoid(_dot(z3, wpg_ref[...]) + bpg_ref[...])
        pb = p_ref[...].astype(BF16)
        pp = _dot(pb, wpp_ref[...])
        h3 = h2v + gate * pp
        r4 = _rstd(h3)
        diff = h3 * r4 * g4v - t_ref[...]
        loss = 0.5 * jnp.sum(jnp.mean(diff * diff, axis=-1, keepdims=True), axis=0, keepdims=True)
        dy = diff * (1.0 / D_MODEL)
        dh3, dg4 = _rms_bwd(h3, g4v, dy)
        dpp = (dh3 * gate).astype(BF16)
        dpre = dh3 * pp * gate * (1.0 - gate)
        dpreb = dpre.astype(BF16)
        dz3 = _dot_nt(dpreb, wpg_ref[...])
        dx, dg3 = _rms_bwd(h2v, g3v, dz3)
        dh2 = dh3 + dx
        dh2_ref[...] = dh2
        dh2b_ref[...] = dh2.astype(BF16)
        a_pg[...] += _dot_tn(z3, dpreb)
        a_pp[...] += _dot_tn(pb, dpp)
        a_vec[0:1, :] += dg3
        a_vec[1:2, :] += dg4
        a_vec[2:3, :] += jnp.sum(dpre, axis=0, keepdims=True)
        a_vec[3:4, :] += jnp.broadcast_to(loss, (1, D_MODEL))

        @pl.when(i == n - 1)
        def _():
            dwpg_ref[...] = a_pg[...].astype(BF16)
            for j in range(N_CHIPS):
                dwpp_ref[j] = a_pp[:, PLE_DIM * j:PLE_DIM * (j + 1)].astype(BF16)
            vec_ref[...] = a_vec[...]

    row = lambda w: pl.BlockSpec((tm, w), lambda i: (i, 0))
    ins = [h2, p, target, g3, w_pg, b_pg, w_pp, g4]
    return pl.pallas_call(
        body, name="head", grid=(n,),
        in_specs=[row(D_MODEL), row(PLE_DIM), row(D_MODEL)] + [_resident(a.shape) for a in ins[3:]],
        out_specs=[row(D_MODEL), row(D_MODEL), _full((D_MODEL, D_MODEL)), _full((N_CHIPS, PLE_DIM, PLE_DIM)),
                   _full((8, D_MODEL))],
        out_shape=[jax.ShapeDtypeStruct((s_len, D_MODEL), F32), jax.ShapeDtypeStruct((s_len, D_MODEL), BF16),
                   jax.ShapeDtypeStruct((D_MODEL, D_MODEL), BF16),
                   jax.ShapeDtypeStruct((N_CHIPS, PLE_DIM, PLE_DIM), BF16), jax.ShapeDtypeStruct((8, D_MODEL), F32)],
        scratch_shapes=[pltpu.VMEM((D_MODEL, D_MODEL), F32), pltpu.VMEM((PLE_DIM, D_MODEL), F32),
                        pltpu.VMEM((8, D_MODEL), F32)],
        compiler_params=_params("arbitrary"),
    )(*ins)


def _bwd_mlp_x(dh2, ru, h1, g2, w_up, w_down, tm):
    s_len = dh2.shape[0]
    n = s_len // tm

    def body(dh2_ref, ru_ref, h1_ref, g2_ref, wup_ref, wdn_ref, dup_ref, dh1_ref, dg2_ref, a_g):
        i = pl.program_id(0)

        @pl.when(i == 0)
        def _():
            a_g[...] = jnp.zeros_like(a_g)

        dh2v = dh2_ref[...]
        dhb = dh2v.astype(BF16)
        acc = jnp.zeros((tm, D_MODEL), F32)
        for j in range(N_CHIPS):
            sl = slice(FF_BLOCK * j, FF_BLOCK * (j + 1))
            dup = (_dot_nt(dhb, wdn_ref[j]) * (2.0 * ru_ref[:, sl].astype(F32))).astype(BF16)
            dup_ref[:, sl] = dup
            acc = acc + _dot_nt(dup, wup_ref[j])
        dx, dg = _rms_bwd(h1_ref[...], g2_ref[...], acc)
        dh1_ref[...] = dh2v + dx
        a_g[0:1, :] += dg

        @pl.when(i == n - 1)
        def _():
            dg2_ref[...] = a_g[...]

    row = lambda w: pl.BlockSpec((tm, w), lambda i: (i, 0))
    return pl.pallas_call(
        body, name="bwd_mlp_x", grid=(n,),
        in_specs=[row(D_MODEL), row(D_FF), row(D_MODEL), _full(g2.shape), _resident(w_up.shape), _resident(w_down.shape)],
        out_specs=[row(D_FF), row(D_MODEL), _full((8, D_MODEL))],
        out_shape=[jax.ShapeDtypeStruct((s_len, D_FF), BF16), jax.ShapeDtypeStruct((s_len, D_MODEL), F32),
                   jax.ShapeDtypeStruct((8, D_MODEL), F32)],
        scratch_shapes=[pltpu.VMEM((8, D_MODEL), F32)],
        compiler_params=_params("arbitrary"),
    )(dh2, ru, h1, g2, w_up, w_down)


def _bwd_mlp_w(z2, dup, ru, dh2, tk):
    s_len = z2.shape[0]
    n = s_len // tk

    def body(z2_ref, dup_ref, ru_ref, dh2_ref, dwup_ref, dwdn_ref, a_up, a_dn):
        t = pl.program_id(1)

        @pl.when(t == 0)
        def _():
            a_up[...] = jnp.zeros_like(a_up)
            a_dn[...] = jnp.zeros_like(a_dn)

        ruv = ru_ref[...]
        a_up[...] += _dot_tn(z2_ref[...], dup_ref[...])
        a_dn[...] += _dot_tn(ruv * ruv, dh2_ref[...])

        @pl.when(t == n - 1)
        def _():
            dwup_ref[0] = a_up[...].astype(BF16)
            dwdn_ref[0] = a_dn[...].astype(BF16)

    tile = pl.BlockSpec((tk, D_MODEL), lambda j, t: (t, 0))
    ffb = pl.BlockSpec((tk, FF_BLOCK), lambda j, t: (t, j))
    return pl.pallas_call(
        body, name="bwd_mlp_w", grid=(N_CHIPS, n),
        in_specs=[tile, ffb, ffb, tile],
        out_specs=[pl.BlockSpec((1, D_MODEL, FF_BLOCK), lambda j, t: (j, 0, 0)),
                   pl.BlockSpec((1, FF_BLOCK, D_MODEL), lambda j, t: (j, 0, 0))],
        out_shape=[jax.ShapeDtypeStruct((N_CHIPS, D_MODEL, FF_BLOCK), BF16),
                   jax.ShapeDtypeStruct((N_CHIPS, FF_BLOCK, D_MODEL), BF16)],
        scratch_shapes=[pltpu.VMEM((D_MODEL, FF_BLOCK), F32), pltpu.VMEM((FF_BLOCK, D_MODEL), F32)],
        compiler_params=_params("arbitrary", "arbitrary"),
    )(z2, dup, ru, dh2)


MIX_VEC_ROWS = 16


def _bwd_mix(dh1, proj, h, cat, pool_w, pool_b, pool_scale, conv_w, conv_b, wa, ba, wx, bx, lru_l, w_out, tm, sub, early):
    s_len = dh1.shape[0]
    n = s_len // tm
    ng = len(POOL_WINDOWS)
    ne_ = len(early)

    def body(dh1_ref, proj_ref, h_ref, cat_ref, projh_ref, hh_ref, pw_ref, pb_ref, ps_ref, cw_ref, cb_ref,
             wa_ref, ba_ref, wx_ref, bx_ref, l_ref, wout_ref, *rest):
        early_ref = rest[:ne_]
        dproj_ref, dwout_ref, dpw_ref, dwa_ref, dwx_ref, vec_ref = rest[ne_:ne_ + 6]
        got_ref = rest[ne_ + 6:2 * ne_ + 6]
        a_out, a_pw, a_wa, a_wx, a_vec, c_g, c_dxb, c_ddc, ssem, rsem = rest[2 * ne_ + 6:]
        q = pl.program_id(0)
        i = n - 1 - q

        @pl.when(q == 0)
        def _():
            _scatter_send(early_ref, got_ref, early, ssem, rsem)
            for r in (a_out, a_pw, a_wa, a_wx, a_vec, c_g, c_dxb, c_ddc):
                r[...] = jnp.zeros_like(r)

        has_prev = (i > 0).astype(F32)
        lv = l_ref[...]
        lsl8 = LRU_C * _log_sigmoid(lv)
        a_out[...] += _dot_tn(cat_ref[...], dh1_ref[...].astype(BF16))
        cg, cdxb, cddc = c_g[0:1, :], c_dxb[...], c_ddc[...]
        vec = {}

        def add(row, v):
            vec[row] = v if row not in vec else vec[row] + v

        for k in reversed(range(tm // sub)):
            rs = slice(k * sub, (k + 1) * sub)
            t0 = i * tm + k * sub
            dcat = _dot_nt(dh1_ref[rs, :].astype(BF16), wout_ref[...])
            dy_pool = dcat[:, :D_POOL]
            dy_lru = dcat[:, D_POOL:]

            proj = proj_ref[rs, :]
            u_pool = proj[:, :D_POOL]
            u_lru = proj[:, D_POOL:D_POOL + D_LRU]
            u_gate = proj[:, D_POOL + D_LRU:]
            if k > 0:
                halo = proj_ref[k * sub - POOL_HALO:k * sub, :]
                h_prev_row = h_ref[k * sub - 1:k * sub, :]
            else:
                halo = projh_ref[...] * has_prev
                h_prev_row = hh_ref[7:8, :] * has_prev

            d, inv = _pool_diff(u_pool, halo[:, :D_POOL], t0)
            db = d.astype(BF16)
            ypre = jnp.concatenate(
                [_dot(db[:, POOL_GROUP * g:POOL_GROUP * (g + 1)], pw_ref[g]) for g in range(ng)], axis=1) + pb_ref[...]
            dyp = dy_pool * ps_ref[...]
            dypb = dyp.astype(BF16)
            dds = []
            for g in range(ng):
                sl = slice(POOL_GROUP * g, POOL_GROUP * (g + 1))
                a_pw[g] += _dot_tn(db[:, sl], dypb[:, sl])
                dds.append(_dot_nt(dypb[:, sl], pw_ref[g]))
            du_pool, ddc = _pool_diff_bwd(jnp.concatenate(dds, axis=1), inv, cddc)
            cddc = ddc[:POOL_HALO]
            add(0, jnp.sum(dyp, axis=0, keepdims=True))
            add(1, jnp.sum(dy_pool * ypre, axis=0, keepdims=True))

            taps = _conv_taps(u_lru, halo[POOL_HALO - CONV_HALO:, D_POOL:D_POOL + D_LRU])
            xb = cb_ref[...]
            for c in range(CONV_WIDTH):
                xb = xb + taps[c] * cw_ref[c:c + 1, :]
            r, ig, a, mult, first = _lru_gates(xb, wa_ref[...], ba_ref[...], wx_ref[...], bx_ref[...], lsl8, t0)
            hv = h_ref[rs, :]
            gl, th = _gelu(u_gate)
            du_gate = dy_lru * hv * _gelu_grad(u_gate, th)
            last = _rows(a.shape, 0) == sub - 1
            a_next = jnp.where(last, 1.0, pltpu.roll(a, sub - 1, 0))
            pa, gb = _scan_rev(a_next, dy_lru * gl)
            gh = gb + pa * cg
            cg = a[0:1, :] * gh[0:1, :]
            h_prev = jnp.where(_rows(hv.shape, 0) == 0, h_prev_row, pltpu.roll(hv, 1, 0))
            gix = gh * ig * xb
            dla = gh * h_prev * a - jnp.where(first, 0.0, gix * a * a / mult)
            dpre_r = dla * lsl8 * r * (1.0 - r)
            dpre_i = gh * mult * xb * ig * (1.0 - ig)
            dprb = dpre_r.astype(BF16)
            dpib = dpre_i.astype(BF16)
            xbb = xb.astype(BF16)
            a_wa[...] += _dot_tn(xbb, dprb)
            a_wx[...] += _dot_tn(xbb, dpib)
            dxb = gh * mult * ig + _dot_nt(dprb, wa_ref[...]) + _dot_nt(dpib, wx_ref[...])
            add(2, jnp.sum(dxb, axis=0, keepdims=True))
            add(3, jnp.sum(dpre_r, axis=0, keepdims=True))
            add(4, jnp.sum(dpre_i, axis=0, keepdims=True))
            add(5, jnp.sum(dla * r, axis=0, keepdims=True))
            ext = jnp.concatenate([dxb, cdxb], axis=0)
            cdxb = dxb[:CONV_HALO]
            ne = sub + CONV_HALO
            du_lru = dxb * cw_ref[CONV_WIDTH - 1:CONV_WIDTH, :]
            for c in range(CONV_WIDTH):
                add(8 + c, jnp.sum(dxb * taps[c], axis=0, keepdims=True))
                if c < CONV_WIDTH - 1:
                    du_lru = du_lru + pltpu.roll(ext, ne - (CONV_WIDTH - 1 - c), 0)[:sub] * cw_ref[c:c + 1, :]
            dproj_ref[rs, :] = jnp.concatenate([du_pool, du_lru, du_gate], axis=1).astype(BF16)
        c_g[...] = jnp.broadcast_to(cg, c_g.shape)
        c_dxb[...] = cdxb
        c_ddc[...] = cddc
        for row, v in vec.items():
            a_vec[row:row + 1, :] += v

        @pl.when(q == n - 1)
        def _():
            dwout_ref[...] = a_out[...].astype(BF16)
            dpw_ref[...] = a_pw[...]
            dwa_ref[...] = a_wa[...]
            dwx_ref[...] = a_wx[...]
            vec_ref[...] = a_vec[...]
            vec_ref[5:6, :] = a_vec[5:6, :] * (LRU_C * _sigmoid(-lv))
            _scatter_done(got_ref, early, ssem, rsem)

    rev =lambda w: pl.BlockSpec((tm, w), lambda q: (n - 1 - q, 0))
    halo_p = pl.BlockSpec((POOL_HALO, D_IN_PROJ), lambda q: (jnp.maximum((n - 1 - q) * (tm // POOL_HALO) - 1, 0), 0))
    halo_h = pl.BlockSpec((8, D_LRU), lambda q: (jnp.maximum((n - 1 - q) * (tm // 8) - 1, 0), 0))
    wts = [pool_w, pool_b, pool_scale, conv_w, conv_b, wa, ba, wx, bx, lru_l, w_out]
    outs = pl.pallas_call(
        body, name="bwd_mix", grid=(n,),
        in_specs=[rev(D_MODEL), rev(D_IN_PROJ), rev(D_LRU), rev(D_MODEL), halo_p, halo_h] + [_resident(a.shape) for a in wts]
        + [ANY] * ne_,
        out_specs=[rev(D_IN_PROJ), _full((D_MODEL, D_MODEL)), _full((ng, POOL_GROUP, POOL_GROUP)),
                   _full((D_LRU, D_LRU)), _full((D_LRU, D_LRU)), _full((MIX_VEC_ROWS, D_LRU))] + [ANY] * ne_,
        out_shape=[jax.ShapeDtypeStruct((s_len, D_IN_PROJ), BF16), jax.ShapeDtypeStruct((D_MODEL, D_MODEL), BF16),
                   jax.ShapeDtypeStruct((ng, POOL_GROUP, POOL_GROUP), F32), jax.ShapeDtypeStruct((D_LRU, D_LRU), F32),
                   jax.ShapeDtypeStruct((D_LRU, D_LRU), F32), jax.ShapeDtypeStruct((MIX_VEC_ROWS, D_LRU), F32)]
        + _scatter_shapes(early),
        scratch_shapes=[pltpu.VMEM((D_MODEL, D_MODEL), F32), pltpu.VMEM((ng, POOL_GROUP, POOL_GROUP), F32),
                        pltpu.VMEM((D_LRU, D_LRU), F32), pltpu.VMEM((D_LRU, D_LRU), F32),
                        pltpu.VMEM((MIX_VEC_ROWS, D_LRU), F32), pltpu.VMEM((8, D_LRU), F32),
                        pltpu.VMEM((CONV_HALO, D_LRU), F32), pltpu.VMEM((POOL_HALO, D_POOL), F32)] + _scatter_sems(ne_),
        compiler_params=_params("arbitrary"),
    )(dh1, proj, h, cat, proj, h, *wts, *early)
    return outs[:6], outs[6:]


def _bwd_in(dproj, z1, x, dh1, g1, w_in, tm):
    s_len = x.shape[0]
    n = s_len // tm
    cb = D_IN_PROJ // N_CHIPS

    def body(dp_ref, z1_ref, x_ref, dh1_ref, g1_ref, win_ref, dx_ref, dwin_ref, dg1_ref, a_w, a_g):
        i = pl.program_id(0)

        @pl.when(i == 0)
        def _():
            a_w[...] = jnp.zeros_like(a_w)
            a_g[...] = jnp.zeros_like(a_g)

        dp = dp_ref[...]
        zb = z1_ref[...]
        dz = jnp.zeros((tm, D_MODEL), F32)
        for j in range(N_CHIPS):
            dpj = dp[:, cb * j:cb * (j + 1)]
            dz = dz + _dot_nt(dpj, win_ref[j])
            a_w[j] += _dot_tn(zb, dpj)
        dx, dg = _rms_bwd(x_ref[...], g1_ref[...], dz)
        dx_ref[...] = dh1_ref[...] + dx
        a_g[0:1, :] += dg

        @pl.when(i == n - 1)
        def _():
            dwin_ref[...] = a_w[...].astype(BF16)
            dg1_ref[...] = a_g[...]

    row = lambda w: pl.BlockSpec((tm, w), lambda i: (i, 0))
    return pl.pallas_call(
        body, name="bwd_in", grid=(n,),
        in_specs=[row(D_IN_PROJ), row(D_MODEL), row(D_MODEL), row(D_MODEL), _resident(g1.shape), _resident(w_in.shape)],
        out_specs=[row(D_MODEL), _full(w_in.shape), _full((8, D_MODEL))],
        out_shape=[jax.ShapeDtypeStruct((s_len, D_MODEL), F32), jax.ShapeDtypeStruct(w_in.shape, BF16),
                   jax.ShapeDtypeStruct((8, D_MODEL), F32)],
        scratch_shapes=[pltpu.VMEM(w_in.shape, F32), pltpu.VMEM((8, D_MODEL), F32)],
        compiler_params=_params("arbitrary"),
    )(dproj, z1, x, dh1, g1, w_in)


def _place():
    x, y, c = lax.axis_index("x"), lax.axis_index("y"), lax.axis_index("c")
    chips = [(1 - x, y), (x, 1 - y), (1 - x, 1 - y)]
    return x, y, c, chips


def _rcopy(src, dst, ssem, rsem, dev):
    return pltpu.make_async_remote_copy(src_ref=src, dst_ref=dst, send_sem=ssem, recv_sem=rsem,
                                        device_id=dev, device_id_type=MESH)


ANY = pl.BlockSpec(memory_space=pl.ANY)
COPY_CHUNK_BYTES = 128 * 1024
ROW_ALIGN = 16


def _row_chunks(rows, row_bytes):
    per = max(ROW_ALIGN, (COPY_CHUNK_BYTES // row_bytes) // ROW_ALIGN * ROW_ALIGN)
    return [(r0, min(per, rows - r0)) for r0 in range(0, rows, per)]


def _row_bytes(a):
    return a.shape[-1] * jnp.dtype(a.dtype).itemsize


def _stack_own(shards, dtypes, pos, steps):
    nw = len(shards)

    def body(pos_ref, *refs):
        for w in range(nw):
            refs[nw + w][0] = refs[w][...].astype(dtypes[w])

    def split(s):
        return s.shape[0] % (steps * ROW_ALIGN) == 0

    ins = [pl.BlockSpec((s.shape[0] // steps, s.shape[1]), lambda i, p: (i, 0)) if split(s)
           else pl.BlockSpec(s.shape, lambda i, p: (0, 0)) for s in shards]
    outs = [pl.BlockSpec((1, s.shape[0] // steps, s.shape[1]), lambda i, p: (p[0], i, 0)) if split(s)
            else pl.BlockSpec((1,) + s.shape, lambda i, p: (p[0], 0, 0)) for s in shards]
    return pl.pallas_call(
        body, name="stack_own",
        grid_spec=pltpu.PrefetchScalarGridSpec(num_scalar_prefetch=1, grid=(steps,), in_specs=ins, out_specs=outs),
        out_shape=[jax.ShapeDtypeStruct((N_CHIPS,) + s.shape, d) for s, d in zip(shards, dtypes)],
        compiler_params=_params("arbitrary"),
    )(pos, *shards)


def _gather_send(outs, stacks, ssem, rsem):
    x, y, c, chips = _place()
    me = 2 * x + y
    for w, st in enumerate(stacks):
        half = st.shape[1] // 2
        for s, (px, py) in enumerate(chips):
            for r0, rs in _row_chunks(half, _row_bytes(st)):
                piece = outs[w].at[me, pl.ds(c * half + r0, rs)]
                _rcopy(piece, piece, ssem.at[w, s], rsem.at[w, s], (px, py, c)).start()


def _gather_pass(outs, stacks, ssem, rsem):
    x, y, c, chips = _place()
    sib = (x, y, 1 - c)
    for w, st in enumerate(stacks):
        half = st.shape[1] // 2
        for s, (px, py) in enumerate(chips):
            blk = outs[w].at[2 * px + py, pl.ds(c * half, half)]
            _rcopy(blk, blk, ssem.at[w, s], rsem.at[w, s], sib).wait_recv()
            for r0, rs in _row_chunks(half, _row_bytes(st)):
                piece = outs[w].at[2 * px + py, pl.ds(c * half + r0, rs)]
                _rcopy(piece, piece, ssem.at[w, 3 + s], rsem.at[w, 3 + s], sib).start()


def _gather_done(outs, stacks, ssem, rsem):
    x, y, c, chips = _place()
    sib = (x, y, 1 - c)
    for w, st in enumerate(stacks):
        half = st.shape[1] // 2
        for s, (px, py) in enumerate(chips):
            blk = outs[w].at[2 * px + py, pl.ds((1 - c) * half, half)]
            _rcopy(blk, blk, ssem.at[w, 3 + s], rsem.at[w, 3 + s], sib).wait_recv()
    for w, st in enumerate(stacks):
        half = st.shape[1] // 2
        blk = outs[w].at[0, pl.ds(0, half)]
        for s in range(6):
            _rcopy(blk, blk, ssem.at[w, s], rsem.at[w, s], sib).wait_send()


def _gather_sems(nw):
    return [pltpu.SemaphoreType.DMA((nw, 6)), pltpu.SemaphoreType.DMA((nw, 6))]


def _gather_weights(stacks):
    nw = len(stacks)

    def body(*refs):
        outs = refs[nw:2 * nw]
        ssem, rsem = refs[2 * nw:]
        _gather_send(outs, stacks, ssem, rsem)
        _gather_pass(outs, stacks, ssem, rsem)
        _gather_done(outs, stacks, ssem, rsem)

    return pl.pallas_call(
        body, name="gather_weights",
        in_specs=[ANY] * nw, out_specs=[ANY] * nw,
        out_shape=[jax.ShapeDtypeStruct(s.shape, s.dtype) for s in stacks],
        input_output_aliases={w: w for w in range(nw)},
        scratch_shapes=_gather_sems(nw),
    )(*stacks)


def _swap_halves(grads, tag):
    nw = len(grads)

    def body(*refs):
        ins, got = refs[:nw], refs[nw:2 * nw]
        ssem, rsem = refs[2 * nw:]
        x, y, c, _ = _place()
        cps = []
        for w in range(nw):
            hr = grads[w].shape[1] // 2
            for k in range(N_CHIPS):
                for r0, rs in _row_chunks(hr, _row_bytes(grads[w])):
                    _rcopy(ins[w].at[k, pl.ds((1 - c) * hr + r0, rs)], got[w].at[k, pl.ds(r0, rs)],
                           ssem.at[w], rsem.at[w], (x, y, 1 - c)).start()
            cps.append(_rcopy(got[w], got[w], ssem.at[w], rsem.at[w], (x, y, 1 - c)))
        for cp in cps:
            cp.wait()

    return pl.pallas_call(
        body, name="swap_halves_" + tag,
        in_specs=[ANY] * nw, out_specs=[ANY] * nw,
        out_shape=[jax.ShapeDtypeStruct((g.shape[0], g.shape[1] // 2, g.shape[2]), g.dtype) for g in grads],
        scratch_shapes=[pltpu.SemaphoreType.DMA((nw,)), pltpu.SemaphoreType.DMA((nw,))],
    )(*grads)


def _add_pairs(grads, got, pos, steps, tag):
    nw = len(grads)

    def body(pos_ref, *refs):
        for w in range(nw):
            refs[2 * nw + w][...] = (refs[w][...].astype(F32) + refs[nw + w][...].astype(F32)).astype(BF16)

    blk = lambda a: (a.shape[0], a.shape[1] // steps, a.shape[2])
    own = [pl.BlockSpec(blk(a), lambda i, p: (0, p[1] * steps + i, 0)) for a in got]
    rec = [pl.BlockSpec(blk(a), lambda i, p: (0, i, 0)) for a in got]
    return pl.pallas_call(
        body, name="add_pairs_" + tag,
        grid_spec=pltpu.PrefetchScalarGridSpec(num_scalar_prefetch=1, grid=(steps,), in_specs=own + rec, out_specs=rec),
        out_shape=[jax.ShapeDtypeStruct(a.shape, BF16) for a in got],
        compiler_params=_params("arbitrary"),
    )(pos, *grads, *got)


def _scatter_send(ins, got, parts, ssem, rsem):
    x, y, c, chips = _place()
    for w, p in enumerate(parts):
        for s, (px, py) in enumerate(chips):
            for r0, rs in _row_chunks(p.shape[1], _row_bytes(p)):
                _rcopy(ins[w].at[2 * px + py, pl.ds(r0, rs)], got[w].at[s, pl.ds(r0, rs)],
                       ssem.at[w, s], rsem.at[w, s], (px, py, c)).start()


def _scatter_done(got, parts, ssem, rsem):
    x, y, c, chips = _place()
    for w in range(len(parts)):
        for s, (px, py) in enumerate(chips):
            _rcopy(got[w].at[s], got[w].at[s], ssem.at[w, s], rsem.at[w, s], (px, py, c)).wait()


def _scatter_sems(nw):
    return [pltpu.SemaphoreType.DMA((nw, 3)), pltpu.SemaphoreType.DMA((nw, 3))]


def _scatter_shapes(parts):
    return [jax.ShapeDtypeStruct((3,) + p.shape[1:], p.dtype) for p in parts]


def _scatter_chips(parts):
    nw = len(parts)

    def body(*refs):
        ins, got = refs[:nw], refs[nw:2 * nw]
        ssem, rsem = refs[2 * nw:]
        _scatter_send(ins, got, parts, ssem, rsem)
        _scatter_done(got, parts, ssem, rsem)

    return pl.pallas_call(
        body, name="scatter_chips",
        in_specs=[ANY] * nw, out_specs=[ANY] * nw, out_shape=_scatter_shapes(parts),
        scratch_shapes=_scatter_sems(nw),
    )(*parts)


def _sum_chips(parts, got, pos, steps):
    nw = len(parts)

    def body(pos_ref, *refs):
        for w in range(nw):
            acc = refs[w][0].astype(F32)
            for s in range(3):
                acc = acc + refs[nw + w][s].astype(F32)
            refs[2 * nw + w][...] = acc

    own = [pl.BlockSpec((1, p.shape[1] // steps, p.shape[2]), lambda i, ps: (ps[0], i, 0)) for p in parts]
    rec = [pl.BlockSpec((3, p.shape[1] // steps, p.shape[2]), lambda i, ps: (0, i, 0)) for p in parts]
    outs = [pl.BlockSpec((p.shape[1] // steps, p.shape[2]), lambda i, ps: (ps[1] * steps + i, 0)) for p in parts]
    return pl.pallas_call(
        body, name="sum_chips",
        grid_spec=pltpu.PrefetchScalarGridSpec(num_scalar_prefetch=1, grid=(steps,), in_specs=own + rec, out_specs=outs),
        out_shape=[jax.ShapeDtypeStruct((2 * p.shape[1], p.shape[2]), F32) for p in parts],
        compiler_params=_params("arbitrary"),
    )(pos, *parts, *got)


def _join_halves(shards):
    nw = len(shards)

    def body(*refs):
        outs = refs[nw:2 * nw]
        ssem, rsem = refs[2 * nw:]
        x, y, c, _ = _place()
        cps = []
        for w in range(nw):
            hr = shards[w].shape[0] // 2
            for r0, rs in _row_chunks(hr, _row_bytes(shards[w])):
                piece = outs[w].at[pl.ds(c * hr + r0, rs)]
                _rcopy(piece, piece, ssem.at[w], rsem.at[w], (x, y, 1 - c)).start()
            mine = outs[w].at[pl.ds(c * hr, hr)]
            cps.append(_rcopy(mine, mine, ssem.at[w], rsem.at[w], (x, y, 1 - c)))
        for cp in cps:
            cp.wait()

    return pl.pallas_call(
        body, name="join_halves",
        in_specs=[ANY] * nw, out_specs=[ANY] * nw,
        out_shape=[jax.ShapeDtypeStruct(s.shape, F32) for s in shards],
        input_output_aliases={w: w for w in range(nw)},
        scratch_shapes=[pltpu.SemaphoreType.DMA((nw,)), pltpu.SemaphoreType.DMA((nw,))],
    )(*shards)


def _scatter_and_allreduce(parts, packed):
    nw = len(parts)
    shape = packed.shape

    def body(*refs):
        ins, p_ref = refs[:nw], refs[nw]
        got, out_ref = refs[nw + 1:2 * nw + 1], refs[2 * nw + 1]
        rbuf, ssem, rsem, bsem_s, bsem_r = refs[2 * nw + 2:]
        x, y, c, _ = _place()
        _scatter_send(ins, got, parts, ssem, rsem)
        out_ref[...] = p_ref[...]
        for st, peer in enumerate([(x, y, 1 - c), (1 - x, y, c), (x, 1 - y, c)]):
            cp = _rcopy(out_ref, rbuf.at[st], bsem_s.at[st], bsem_r.at[st], peer)
            cp.start()
            cp.wait()
            out_ref[...] = out_ref[...] + rbuf[st]
        _scatter_done(got, parts, ssem, rsem)

    vm = pl.BlockSpec(memory_space=pltpu.VMEM)
    outs = pl.pallas_call(
        body, name="scatter_and_allreduce",
        in_specs=[ANY] * nw + [vm], out_specs=[ANY] * nw + [vm],
        out_shape=_scatter_shapes(parts) + [jax.ShapeDtypeStruct(shape, F32)],
        scratch_shapes=[pltpu.VMEM((3,) + shape, F32)] + _scatter_sems(nw)
        + [pltpu.SemaphoreType.DMA((3,)), pltpu.SemaphoreType.DMA((3,))],
        compiler_params=pltpu.CompilerParams(vmem_limit_bytes=VMEM_LIMIT),
    )(*parts, packed)
    return outs[:nw], outs[nw]


def _adamw_math(w, g, m, v):
    m = ADAM_B1 * m + (1.0 - ADAM_B1) * g
    v = ADAM_B2 * v + (1.0 - ADAM_B2) * (g * g)
    delta = -ADAM_LR * ((m * ADAM_C1) / (jnp.sqrt(v * ADAM_C2) + ADAM_EPS) + ADAM_WD * w)
    return delta, m, v


def _adamw(ws, gs, ms, vs, steps, name):
    nw = len(ws)

    def body(*refs):
        for k in range(nw):
            d, m, v = _adamw_math(refs[k][...], refs[nw + k][...], refs[2 * nw + k][...], refs[3 * nw + k][...])
            refs[4 * nw + k][...] = d
            refs[5 * nw + k][...] = m
            refs[6 * nw + k][...] = v

    specs = [pl.BlockSpec((a.shape[0] // steps, a.shape[1]), lambda i: (i, 0)) for a in ws]
    shapes = [jax.ShapeDtypeStruct(a.shape, F32) for a in ws]
    outs = pl.pallas_call(
        body, name=name, grid=(steps,),
        in_specs=specs * 4, out_specs=specs * 3, out_shape=shapes * 3,
        compiler_params=_params("arbitrary"),
    )(*ws, *gs, *ms, *vs)
    return outs[:nw], outs[nw:2 * nw], outs[2 * nw:]


SMALL = ["norm_mix_g", "pool_w", "pool_b", "pool_scale", "conv_b", "gate_a_w", "gate_a_b", "gate_x_w", "gate_x_b",
         "lru_L", "norm_mlp_g", "norm_ple_g", "b_ple_gate", "norm_final_g"]
BIG = ["w_in", "w_out", "w_up", "w_down", "w_ple_gate", "w_ple_proj"]
ORDER = ["norm_mix_g", "w_in", "pool_w", "pool_b", "pool_scale", "conv_w", "conv_b", "gate_a_w", "gate_a_b", "gate_x_w",
         "gate_x_b", "lru_L", "w_out", "norm_mlp_g", "w_up", "w_down", "norm_ple_g", "w_ple_gate", "b_ple_gate",
         "w_ple_proj", "norm_final_g"]
LANES = 128


def _block_diag(w):
    eye = jnp.eye(LRU_HEADS, dtype=w.dtype)
    return jnp.einsum("hij,hk->hikj", w, eye).reshape(D_LRU, D_LRU)


def _diag_blocks(full):
    f = full.reshape(LRU_HEADS, LRU_BLOCK, LRU_HEADS, LRU_BLOCK)
    return jnp.stack([f[h, :, h, :] for h in range(LRU_HEADS)])


def _rows128(a):
    return a.reshape(-1, LANES)


def _pad8(a):
    r = (-a.shape[0]) % 8
    return jnp.pad(a, ((0, r), (0, 0))) if r else a


def kernel(x, p, norm_mix_g, w_in, pool_w, pool_b, pool_scale, conv_w, conv_b, gate_a_w, gate_a_b, gate_x_w, gate_x_b, lru_L, w_out, norm_mlp_g, w_up, w_down, norm_ple_g, w_ple_gate, b_ple_gate, w_ple_proj, norm_final_g, loss_target, m_norm_mix_g, m_w_in, m_pool_w, m_pool_b, m_pool_scale, m_conv_w, m_conv_b, m_gate_a_w, m_gate_a_b, m_gate_x_w, m_gate_x_b, m_lru_L, m_w_out, m_norm_mlp_g, m_w_up, m_w_down, m_norm_ple_g, m_w_ple_gate, m_b_ple_gate, m_w_ple_proj, m_norm_final_g, v_norm_mix_g, v_w_in, v_pool_w, v_pool_b, v_pool_scale, v_conv_w, v_conv_b, v_gate_a_w, v_gate_a_b, v_gate_x_w, v_gate_x_b, v_lru_L, v_w_out, v_norm_mlp_g, v_w_up, v_w_down, v_norm_ple_g, v_w_ple_gate, v_b_ple_gate, v_w_ple_proj, v_norm_final_g):
    W = dict(norm_mix_g=norm_mix_g, w_in=w_in, pool_w=pool_w, pool_b=pool_b, pool_scale=pool_scale, conv_w=conv_w,
             conv_b=conv_b, gate_a_w=gate_a_w, gate_a_b=gate_a_b, gate_x_w=gate_x_w, gate_x_b=gate_x_b, lru_L=lru_L,
             w_out=w_out, norm_mlp_g=norm_mlp_g, w_up=w_up, w_down=w_down, norm_ple_g=norm_ple_g,
             w_ple_gate=w_ple_gate, b_ple_gate=b_ple_gate, w_ple_proj=w_ple_proj, norm_final_g=norm_final_g)
    M = dict(norm_mix_g=m_norm_mix_g, w_in=m_w_in, pool_w=m_pool_w, pool_b=m_pool_b, pool_scale=m_pool_scale,
             conv_w=m_conv_w, conv_b=m_conv_b, gate_a_w=m_gate_a_w, gate_a_b=m_gate_a_b, gate_x_w=m_gate_x_w,
             gate_x_b=m_gate_x_b, lru_L=m_lru_L, w_out=m_w_out, norm_mlp_g=m_norm_mlp_g, w_up=m_w_up, w_down=m_w_down,
             norm_ple_g=m_norm_ple_g, w_ple_gate=m_w_ple_gate, b_ple_gate=m_b_ple_gate, w_ple_proj=m_w_ple_proj,
             norm_final_g=m_norm_final_g)
    V = dict(norm_mix_g=v_norm_mix_g, w_in=v_w_in, pool_w=v_pool_w, pool_b=v_pool_b, pool_scale=v_pool_scale,
             conv_w=v_conv_w, conv_b=v_conv_b, gate_a_w=v_gate_a_w, gate_a_b=v_gate_a_b, gate_x_w=v_gate_x_w,
             gate_x_b=v_gate_x_b, lru_L=v_lru_L, w_out=v_w_out, norm_mlp_g=v_norm_mlp_g, w_up=v_w_up, w_down=v_w_down,
             norm_ple_g=v_norm_ple_g, w_ple_gate=v_w_ple_gate, b_ple_gate=v_b_ple_gate, w_ple_proj=v_w_ple_proj,
             norm_final_g=v_norm_final_g)

    s_len = x.shape[1]
    sub_mix = min(256, s_len)
    tm = min(512, s_len)
    chip = (2 * lax.axis_index("x") + lax.axis_index("y")).astype(jnp.int32)
    pos = jnp.stack([chip, lax.axis_index("c").astype(jnp.int32)])

    shards = [w_in[0], w_out[0], w_up[0], w_down[0], w_ple_gate[0], w_ple_proj[0], jnp.pad(conv_w[0], ((0, 12), (0, 0)))]
    st_in, st_out, st_up, st_dn, st_pg, st_pp, st_cw = _stack_own(shards, [BF16] * 6 + [F32], pos, 8)
    win_g, wout_g, cw_g = _gather_weights([st_in, st_out, st_cw])
    wout_f = wout_g.reshape(D_MODEL, D_MODEL)
    cw_f = jnp.transpose(cw_g[:, :CONV_WIDTH], (1, 0, 2)).reshape(CONV_WIDTH, D_LRU)
    pw_b = pool_w[0].astype(BF16)
    wa_b = _block_diag(gate_a_w[0]).astype(BF16)
    wx_b = _block_diag(gate_x_w[0]).astype(BF16)
    pb_r = pool_b.reshape(1, D_POOL)
    ba_r = gate_a_b.reshape(1, D_LRU)
    bx_r = gate_x_b.reshape(1, D_LRU)
    g4 = norm_final_g.reshape(1, D_MODEL)
    mix_w = (pw_b, pb_r, pool_scale, cw_f, conv_b, wa_b, ba_r, wx_b, bx_r, lru_L, wout_f)

    xs, ps, ts = x[0], p[0, 0], loss_target[0]
    (z1, proj, hst, cat, h1), (wup_g, wdn_g) = _fwd_mix(xs, norm_mix_g, win_g, *mix_w, tm, sub_mix, [st_up, st_dn])
    (z2, ru, h2), (wpg_g, wpp_g) = _fwd_mlp(h1, norm_mlp_g, wup_g, wdn_g, tm, [st_pg, st_pp])
    wpg_f = wpg_g.reshape(D_MODEL, D_MODEL)
    wpp_f = jnp.transpose(wpp_g, (1, 0, 2)).reshape(PLE_DIM, D_MODEL)
    dh2, dh2b, d_wpg, d_wpp, head_vec = _head(h2, ps, ts, norm_ple_g, wpg_f, b_ple_gate, wpp_f, g4, tm)
    dup, dh1, mlp_vec = _bwd_mlp_x(dh2, ru, h1, norm_mlp_g, wup_g, wdn_g, tm)
    d_wup, d_wdn = _bwd_mlp_w(z2, dup, ru, dh2b, tm)
    early = [d_wup, d_wdn, d_wpg.reshape(N_CHIPS, D_MODEL // N_CHIPS, D_MODEL), d_wpp]
    pair_e = _add_pairs(early, _swap_halves(early, "early"), pos, 8, "early")
    (dproj, d_wout, d_pw, d_wa, d_wx, mix_vec), got_e = _bwd_mix(dh1, proj, hst, cat, *mix_w, tm, sub_mix, pair_e)
    dx, d_win, in_vec = _bwd_in(dproj, z1, xs, dh1, norm_mix_g, win_g, tm)

    last = [d_win, d_wout.reshape(N_CHIPS, D_MODEL // N_CHIPS, D_MODEL)]
    pair_l = _add_pairs(last, _swap_halves(last, "last"), pos, 8, "last")

    g_small = {
        "norm_mix_g": in_vec[0:1], "pool_w": d_pw, "pool_b": mix_vec[0:1], "pool_scale": mix_vec[1:2],
        "conv_b": mix_vec[2:3], "gate_a_w": _diag_blocks(d_wa), "gate_a_b": mix_vec[3:4],
        "gate_x_w": _diag_blocks(d_wx), "gate_x_b": mix_vec[4:5], "lru_L": mix_vec[5:6], "norm_mlp_g": mlp_vec[0:1],
        "norm_ple_g": head_vec[0:1], "b_ple_gate": head_vec[2:3], "norm_final_g": head_vec[1:2],
    }
    d_cw = jnp.transpose(mix_vec[8:8 + CONV_WIDTH].reshape(CONV_WIDTH, N_CHIPS, LANES), (1, 0, 2)).reshape(-1, LANES)
    pieces = [_pad8(_rows128(g_small[k])) for k in SMALL] + [d_cw, _pad8(head_vec[3:4, :LANES])]
    offs = [0]
    for pc in pieces:
        offs.append(offs[-1] + pc.shape[0])
    got_l, red = _scatter_and_allreduce(pair_l, jnp.concatenate(pieces, axis=0))
    g_big = _join_halves(_sum_chips(pair_l + pair_e, got_l + got_e, pos, 8))
    loss = red[offs[-2], 0]
    g_cw = lax.dynamic_slice(red, (offs[len(SMALL)] + CONV_WIDTH * chip, 0), (CONV_WIDTH, LANES))

    def packed(src):
        return jnp.concatenate([_pad8(_rows128(src[k])) for k in SMALL] + [_pad8(src["conv_w"][0])], axis=0)

    n_small = offs[len(SMALL)]
    g_pack = jnp.concatenate([red[:n_small], _pad8(g_cw)], axis=0)
    (d_pack,), (m_pack,), (v_pack,) = _adamw([packed(W)], [g_pack], [packed(M)], [packed(V)], 1, "adamw_small")

    big2d = lambda src: [src[k][0] for k in BIG]
    d_big, m_big, v_big = _adamw(big2d(W), g_big, big2d(M), big2d(V), 8, "adamw_big")

    def unpack(pack, big_list):
        out = {}
        for idx, k in enumerate(SMALL):
            n_el = W[k].size
            out[k] = pack[offs[idx]:offs[idx + 1]].reshape(-1)[:n_el].reshape(W[k].shape)
        out["conv_w"] = pack[n_small:n_small + CONV_WIDTH].reshape(W["conv_w"].shape)
        for k, a in zip(BIG, big_list):
            out[k] = a.reshape(W[k].shape)
        return out

    grads = unpack(g_pack, g_big)
    deltas = unpack(d_pack, d_big)
    new_m = unpack(m_pack, m_big)
    new_v = unpack(v_pack, v_big)
    return (loss, dx[None], *[grads[k] for k in ORDER], *[deltas[k] for k in ORDER],
            *[new_m[k] for k in ORDER], *[new_v[k] for k in ORDER])
```

```python
import functools

import jax
import jax.numpy as jnp
from jax import lax
from jax.experimental import pallas as pl
from jax.experimental.pallas import tpu as pltpu

F32 = jnp.float32
BF16 = jnp.bfloat16
MESH = pl.DeviceIdType.MESH

D_MODEL = 1024
D_POOL = 512
D_LRU = 512
POOL_WINDOWS = (2, 4, 8, 16)
POOL_GROUP = 128
POOL_HALO = 16
CONV_WIDTH = 4
CONV_HALO = 8
PASS_STEPS = 2
LRU_HEADS = 8
LRU_BLOCK = 64
LRU_C = 8.0
D_FF = 4096
PLE_DIM = 256
D_IN_PROJ = 1536
RMS_EPS = 1e-6
N_CHIPS = 4
FF_BLOCK = D_FF // N_CHIPS
FF_PAIR = 2

ADAM_LR = 0.001
ADAM_B1 = 0.9
ADAM_B2 = 0.999
ADAM_EPS = 1e-08
ADAM_WD = 0.01
ADAM_STEP = 10
ADAM_C1 = 1.0 / (1.0 - ADAM_B1 ** ADAM_STEP)
ADAM_C2 = 1.0 / (1.0 - ADAM_B2 ** ADAM_STEP)

VMEM_LIMIT = 56 * 1024 * 1024
GELU_C = 0.7978845608028654
GELU_A = 0.044715

NT = (((1,), (1,)), ((), ()))
TN = (((0,), (0,)), ((), ()))


def _dot(a, b):
    return jnp.dot(a, b, preferred_element_type=F32)


def _dot_nt(a, b):
    return lax.dot_general(a, b, NT, preferred_element_type=F32)


def _dot_tn(a, b):
    return lax.dot_general(a, b, TN, preferred_element_type=F32)


def _params(*sem):
    return pltpu.CompilerParams(dimension_semantics=sem, vmem_limit_bytes=VMEM_LIMIT)


def _full(shape):
    nd = len(shape)
    return pl.BlockSpec(shape, lambda *_: (0,) * nd)


def _resident(shape):
    nd = len(shape)
    return pl.BlockSpec(shape, lambda *_: (0,) * nd, pipeline_mode=pl.Buffered(1))


def _rstd(x):
    return lax.rsqrt(jnp.mean(x * x, axis=-1, keepdims=True) + RMS_EPS)


def _rms_bwd(x, g, dz):
    xr = x * _rstd(x)
    r = _rstd(x)
    dyg = dz * g
    dx = r * (dyg - xr * jnp.mean(dyg * xr, axis=-1, keepdims=True))
    dg = jnp.sum(dz * xr, axis=0, keepdims=True)
    return dx, dg


def _sigmoid(x):
    return 1.0 / (1.0 + jnp.exp(-x))


def _log_sigmoid(v):
    u = jnp.exp(-jnp.abs(v))
    w = 1.0 + u
    l1p = jnp.where(w == 1.0, u, jnp.log(w) * u / jnp.where(w == 1.0, 1.0, w - 1.0))
    return jnp.minimum(v, 0.0) - l1p


def _gelu(x):
    t = jnp.tanh(GELU_C * (x + GELU_A * x * x * x))
    return 0.5 * x * (1.0 + t), t


def _gelu_grad(x, t):
    return 0.5 * (1.0 + t) + 0.5 * x * (1.0 - t * t) * GELU_C * (1.0 + 3.0 * GELU_A * x * x)


def _rows(shape, t0):
    return lax.broadcasted_iota(jnp.int32, shape, 0) + t0


def _pool_diff(u_pool, prev, t0):
    tm = u_pool.shape[0]
    rows = _rows((tm, POOL_GROUP), t0)
    outs, invs = [], []
    for g, w in enumerate(POOL_WINDOWS):
        sl = slice(POOL_GROUP * g, POOL_GROUP * (g + 1))
        ug = u_pool[:, sl]
        s = jnp.concatenate([prev[:, sl], ug], axis=0)
        k = 1
        while k < w:
            s = s + pltpu.roll(s, k, 0)
            k *= 2
        inv = 1.0 / jnp.minimum(rows + 1, w).astype(F32)
        outs.append(s[POOL_HALO:] * inv - ug)
        invs.append(inv)
    return jnp.concatenate(outs, axis=1), jnp.concatenate(invs, axis=1)


def _pool_diff_bwd(dd, inv, nxt):
    ddc = dd * inv
    outs = []
    for g, w in enumerate(POOL_WINDOWS):
        sl = slice(POOL_GROUP * g, POOL_GROUP * (g + 1))
        s = jnp.concatenate([ddc[:, sl], nxt[:, sl]], axis=0)
        n = s.shape[0]
        k = 1
        while k < w:
            s = s + pltpu.roll(s, n - k, 0)
            k *= 2
        outs.append(s[:n - POOL_HALO] - dd[:, sl])
    return jnp.concatenate(outs, axis=1), ddc


def _conv_taps(u, prev):
    ext = jnp.concatenate([prev, u], axis=0)
    return [pltpu.roll(ext, CONV_WIDTH - 1 - k, 0)[CONV_HALO:] if k < CONV_WIDTH - 1 else u for k in range(CONV_WIDTH)]


def _scan_fwd(a, b):
    tm = a.shape[0]
    rows = _rows(a.shape, 0)
    k = 1
    while k < tm:
        ar = pltpu.roll(a, k, 0)
        br = pltpu.roll(b, k, 0)
        m = rows >= k
        b = jnp.where(m, a * br + b, b)
        a = jnp.where(m, a * ar, a)
        k *= 2
    return a, b


def _scan_rev(a, b):
    tm = a.shape[0]
    rows = _rows(a.shape, 0)
    k = 1
    while k < tm:
        ar = pltpu.roll(a, tm - k, 0)
        br = pltpu.roll(b, tm - k, 0)
        m = rows < tm - k
        b = jnp.where(m, a * br + b, b)
        a = jnp.where(m, a * ar, a)
        k *= 2
    return a, b


def _lru_gates(xb, wa, ba, wx, bx, lsl8, t0):
    xbb = xb.astype(BF16)
    r = _sigmoid(_dot(xbb, wa) + ba)
    ig = _sigmoid(_dot(xbb, wx) + bx)
    a = jnp.exp(r * lsl8)
    first = _rows(xb.shape, t0) == 0
    mult = jnp.where(first, 1.0, jnp.sqrt(1.0 - a * a))
    return r, ig, a, mult, first


def _fwd_mix(x, g1, w_in, pool_w, pool_b, pool_scale, conv_w, conv_b, wa, ba, wx, bx, lru_l, w_out, tm, sub, late):
    s_len = x.shape[0]
    n = s_len // tm
    nl = len(late)

    def body(x_ref, g1_ref, win_ref, pw_ref, pb_ref, ps_ref, cw_ref, cb_ref, wa_ref, ba_ref, wx_ref, bx_ref, l_ref,
             wout_ref, *rest):
        proj_ref, h_ref, h1_ref = rest[nl:nl + 3]
        late_ref = rest[nl + 3:2 * nl + 3]
        cpool, clru, ch, ssem, rsem = rest[2 * nl + 3:]
        i = pl.program_id(0)

        @pl.when(i == 0)
        def _():
            _gather_send(late_ref, late, ssem, rsem)
            cpool[...] = jnp.zeros_like(cpool)
            clru[...] = jnp.zeros_like(clru)
            ch[...] = jnp.zeros_like(ch)

        lsl8 = LRU_C * _log_sigmoid(l_ref[...])
        cp, cl, hc = cpool[...], clru[...], ch[7:8, :]
        def in_proj(k):
            rs = slice(k * sub, (k + 1) * sub)
            xv = x_ref[rs, :]
            zb = (xv * _rstd(xv) * g1_ref[...]).astype(BF16)
            proj = jnp.concatenate([_dot(zb, win_ref[j]) for j in range(N_CHIPS)], axis=1)
            proj_ref[rs, :] = proj.astype(BF16)
            return xv, proj

        nxt = in_proj(0)
        for k in range(tm // sub):
            rs = slice(k * sub, (k + 1) * sub)
            t0 = i * tm + k * sub
            xv, proj = nxt
            if k + 1 < tm // sub:
                nxt = in_proj(k + 1)
            u_pool = proj[:, :D_POOL]
            u_lru = proj[:, D_POOL:D_POOL + D_LRU]
            u_gate = proj[:, D_POOL + D_LRU:]

            d, _ = _pool_diff(u_pool, cp, t0)
            cp = u_pool[sub - POOL_HALO:]
            db = d.astype(BF16)
            yp = jnp.concatenate(
                [_dot(db[:, POOL_GROUP * g:POOL_GROUP * (g + 1)], pw_ref[g]) for g in range(len(POOL_WINDOWS))], axis=1)
            y_pool = (yp + pb_ref[...]) * ps_ref[...]

            taps = _conv_taps(u_lru, cl)
            cl = u_lru[sub - CONV_HALO:]
            xb = cb_ref[...]
            for q in range(CONV_WIDTH):
                xb = xb + taps[q] * cw_ref[q:q + 1, :]
            _, ig, a, mult, _ = _lru_gates(xb, wa_ref[...], ba_ref[...], wx_ref[...], bx_ref[...], lsl8, t0)
            pa, hb = _scan_fwd(a, mult * (ig * xb))
            h = hb + pa * hc
            hc = h[sub - 1:sub, :]
            h_ref[rs, :] = h
            gl, _ = _gelu(u_gate)
            cat = jnp.concatenate([y_pool, h * gl], axis=1).astype(BF16)
            h1_ref[rs, :] = xv + _dot(cat, wout_ref[...])
        cpool[...] = cp
        clru[...] = cl
        ch[...] = jnp.broadcast_to(hc, ch.shape)

        @pl.when(i == max(n - PASS_STEPS, 0))
        def _():
            _gather_pass(late_ref, late, ssem, rsem)

        @pl.when(i == n - 1)
        def _():
            _gather_done(late_ref, late, ssem, rsem)

    row = lambda w: pl.BlockSpec((tm, w), lambda i: (i, 0))
    ins = [x, g1, w_in, pool_w, pool_b, pool_scale, conv_w, conv_b, wa, ba, wx, bx, lru_l, w_out]
    outs = pl.pallas_call(
        body, name="fwd_mix", grid=(n,),
        in_specs=[row(D_MODEL)] + [_resident(a.shape) for a in ins[1:]] + [ANY] * nl,
        out_specs=[row(D_IN_PROJ), row(D_LRU), row(D_MODEL)] + [ANY] * nl,
        out_shape=[jax.ShapeDtypeStruct((s_len, D_IN_PROJ), BF16), jax.ShapeDtypeStruct((s_len, D_LRU), F32),
                   jax.ShapeDtypeStruct((s_len, D_MODEL), F32)]
        + [jax.ShapeDtypeStruct(a.shape, a.dtype) for a in late],
        input_output_aliases={len(ins) + k: 3 + k for k in range(nl)},
        scratch_shapes=[pltpu.VMEM((POOL_HALO, D_POOL), F32), pltpu.VMEM((CONV_HALO, D_LRU), F32),
                        pltpu.VMEM((8, D_LRU), F32)] + _gather_sems(nl),
        compiler_params=_params("arbitrary"),
    )(*ins, *late)
    return outs[:3], outs[3:]


def _fwd_mlp(h1, g2, w_up, w_down, tm, late):
    s_len = h1.shape[0]
    n = s_len // tm
    nl = len(late)

    def body(h1_ref, g2_ref, wup_ref, wdn_ref, *rest):
        z2_ref, ru_ref, h2_ref = rest[nl:nl + 3]
        late_ref = rest[nl + 3:2 * nl + 3]
        ssem, rsem = rest[2 * nl + 3:]
        i = pl.program_id(0)

        @pl.when(i == 0)
        def _():
            _gather_send(late_ref, late, ssem, rsem)

        hv = h1_ref[...]
        zb = (hv * _rstd(hv) * g2_ref[...]).astype(BF16)
        z2_ref[...] = zb
        acc = hv
        for j in range(N_CHIPS):
            ru = jnp.maximum(_dot(zb, wup_ref[j]), 0.0)
            ru_ref[:, FF_BLOCK * j:FF_BLOCK * (j + 1)] = ru.astype(BF16)
            acc = acc + _dot((ru * ru).astype(BF16), wdn_ref[j])
        h2_ref[...] = acc

        @pl.when(i == max(n - PASS_STEPS, 0))
        def _():
            _gather_pass(late_ref, late, ssem, rsem)

        @pl.when(i == n - 1)
        def _():
            _gather_done(late_ref, late, ssem, rsem)

    row = lambda w: pl.BlockSpec((tm, w), lambda i: (i, 0))
    outs = pl.pallas_call(
        body, name="fwd_mlp", grid=(n,),
        in_specs=[row(D_MODEL), _full(g2.shape), _resident(w_up.shape), _resident(w_down.shape)] + [ANY] * nl,
        out_specs=[row(D_MODEL), row(D_FF), row(D_MODEL)] + [ANY] * nl,
        out_shape=[jax.ShapeDtypeStruct((s_len, D_MODEL), BF16), jax.ShapeDtypeStruct((s_len, D_FF), BF16),
                   jax.ShapeDtypeStruct((s_len, D_MODEL), F32)] + [jax.ShapeDtypeStruct(a.shape, a.dtype) for a in late],
        input_output_aliases={4 + k: 3 + k for k in range(nl)},
        scratch_shapes=_gather_sems(nl),
        compiler_params=_params("arbitrary"),
    )(h1, g2, w_up, w_down, *late)
    return outs[:3], outs[3:]


def _head(h2, p, target, g3, w_pg, b_pg, w_pp, g4, tm):
    s_len = h2.shape[0]
    n = s_len // tm

    def body(h2_ref, p_ref, t_ref, g3_ref, wpg_ref, bpg_ref, wpp_ref, g4_ref,
             dh2_ref, dh2b_ref, dwpg_ref, dwpp_ref, vec_ref, a_pg, a_pp, a_vec):
        i = pl.program_id(0)

        @pl.when(i == 0)
        def _():
            a_pg[...] = jnp.zeros_like(a_pg)
            a_pp[...] = jnp.zeros_like(a_pp)
            a_vec[...] = jnp.zeros_like(a_vec)

        h2v = h2_ref[...]
        g3v = g3_ref[...]
        g4v = g4_ref[...]
        z3 = (h2v * _rstd(h2v) * g3v).astype(BF16)
        gate = _sigmoid(_dot(z3, wpg_ref[...]) + bpg_ref[...])
        pb = p_ref[...].astype(BF16)
        pp = _dot(pb, wpp_ref[...])
        h3 = h2v + gate * pp
        r4 = _rstd(h3)
        diff = h3 * r4 * g4v - t_ref[...]
        loss = 0.5 * jnp.sum(jnp.mean(diff * diff, axis=-1, keepdims=True), axis=0, keepdims=True)
        dy = diff * (1.0 / D_MODEL)
        dh3, dg4 = _rms_bwd(h3, g4v, dy)
        dpp = (dh3 * gate).astype(BF16)
        dpre = dh3 * pp * gate * (1.0 - gate)
        dpreb = dpre.astype(BF16)
        dz3 = _dot_nt(dpreb, wpg_ref[...])
        dx, dg3 = _rms_bwd(h2v, g3v, dz3)
        dh2 = dh3 + dx
        dh2_ref[...] = dh2
        dh2b_ref[...] = dh2.astype(BF16)
        a_pg[...] += _dot_tn(z3, dpreb)
        a_pp[...] += _dot_tn(pb, dpp)
        a_vec[0:1, :] += dg3
        a_vec[1:2, :] += dg4
        a_vec[2:3, :] += jnp.sum(dpre, axis=0, keepdims=True)
        a_vec[3:4, :] += jnp.broadcast_to(loss, (1, D_MODEL))

        @pl.when(i == n - 1)
        def _():
            dwpg_ref[...] = a_pg[...].astype(BF16)
            for j in range(N_CHIPS):
                dwpp_ref[j] = a_pp[:, PLE_DIM * j:PLE_DIM * (j + 1)].astype(BF16)
            vec_ref[...] = a_vec[...]

    row = lambda w: pl.BlockSpec((tm, w), lambda i: (i, 0))
    ins = [h2, p, target, g3, w_pg, b_pg, w_pp, g4]
    return pl.pallas_call(
        body, name="head", grid=(n,),
        in_specs=[row(D_MODEL), row(PLE_DIM), row(D_MODEL)] + [_resident(a.shape) for a in ins[3:]],
        out_specs=[row(D_MODEL), row(D_MODEL), _full((D_MODEL, D_MODEL)), _full((N_CHIPS, PLE_DIM, PLE_DIM)),
                   _full((8, D_MODEL))],
        out_shape=[jax.ShapeDtypeStruct((s_len, D_MODEL), F32), jax.ShapeDtypeStruct((s_len, D_MODEL), BF16),
                   jax.ShapeDtypeStruct((D_MODEL, D_MODEL), BF16),
                   jax.ShapeDtypeStruct((N_CHIPS, PLE_DIM, PLE_DIM), BF16), jax.ShapeDtypeStruct((8, D_MODEL), F32)],
        scratch_shapes=[pltpu.VMEM((D_MODEL, D_MODEL), F32), pltpu.VMEM((PLE_DIM, D_MODEL), F32),
                        pltpu.VMEM((8, D_MODEL), F32)],
        compiler_params=_params("arbitrary"),
    )(*ins)


def _bwd_mlp_x(dh2, ru, h1, g2, w_up, w_down, tm):
    s_len = dh2.shape[0]
    n = s_len // tm

    def body(dh2_ref, ru_ref, h1_ref, g2_ref, wup_ref, wdn_ref, dup_ref, dh1_ref, dg2_ref, a_g):
        i = pl.program_id(0)

        @pl.when(i == 0)
        def _():
            a_g[...] = jnp.zeros_like(a_g)

        dh2v = dh2_ref[...]
        dhb = dh2v.astype(BF16)
        acc = jnp.zeros((tm, D_MODEL), F32)
        for j in range(N_CHIPS):
            sl = slice(FF_BLOCK * j, FF_BLOCK * (j + 1))
            dup = (_dot_nt(dhb, wdn_ref[j]) * (2.0 * ru_ref[:, sl].astype(F32))).astype(BF16)
            dup_ref[:, sl] = dup
            acc = acc + _dot_nt(dup, wup_ref[j])
        dx, dg = _rms_bwd(h1_ref[...], g2_ref[...], acc)
        dh1_ref[...] = dh2v + dx
        a_g[0:1, :] += dg

        @pl.when(i == n - 1)
        def _():
            dg2_ref[...] = a_g[...]

    row = lambda w: pl.BlockSpec((tm, w), lambda i: (i, 0))
    return pl.pallas_call(
        body, name="bwd_mlp_x", grid=(n,),
        in_specs=[row(D_MODEL), row(D_FF), row(D_MODEL), _full(g2.shape), _resident(w_up.shape), _resident(w_down.shape)],
        out_specs=[row(D_FF), row(D_MODEL), _full((8, D_MODEL))],
        out_shape=[jax.ShapeDtypeStruct((s_len, D_FF), BF16), jax.ShapeDtypeStruct((s_len, D_MODEL), F32),
                   jax.ShapeDtypeStruct((8, D_MODEL), F32)],
        scratch_shapes=[pltpu.VMEM((8, D_MODEL), F32)],
        compiler_params=_params("arbitrary"),
    )(dh2, ru, h1, g2, w_up, w_down)


def _bwd_mlp_w(z2, dup, ru, dh2, tk):
    s_len = z2.shape[0]
    n = s_len // tk

    def body(z2_ref, dup_ref, ru_ref, dh2_ref, dwup_ref, dwdn_ref, a_up, a_dn):
        t = pl.program_id(1)

        @pl.when(t == 0)
        def _():
            a_up[...] = jnp.zeros_like(a_up)
            a_dn[...] = jnp.zeros_like(a_dn)

        for b in range(FF_PAIR):
            sl = slice(FF_BLOCK * b, FF_BLOCK * (b + 1))
            ruv = ru_ref[:, sl]
            a_up[b] += _dot_tn(z2_ref[...], dup_ref[:, sl])
            a_dn[b] += _dot_tn(ruv * ruv, dh2_ref[...])

        @pl.when(t == n - 1)
        def _():
            dwup_ref[...] = a_up[...].astype(BF16)
            dwdn_ref[...] = a_dn[...].astype(BF16)

    tile = pl.BlockSpec((tk, D_MODEL), lambda j, t: (t, 0))
    ffb = pl.BlockSpec((tk, FF_PAIR * FF_BLOCK), lambda j, t: (t, j))
    return pl.pallas_call(
        body, name="bwd_mlp_w", grid=(N_CHIPS // FF_PAIR, n),
        in_specs=[tile, ffb, ffb, tile],
        out_specs=[pl.BlockSpec((FF_PAIR, D_MODEL, FF_BLOCK), lambda j, t: (j, 0, 0)),
                   pl.BlockSpec((FF_PAIR, FF_BLOCK, D_MODEL), lambda j, t: (j, 0, 0))],
        out_shape=[jax.ShapeDtypeStruct((N_CHIPS, D_MODEL, FF_BLOCK), BF16),
                   jax.ShapeDtypeStruct((N_CHIPS, FF_BLOCK, D_MODEL), BF16)],
        scratch_shapes=[pltpu.VMEM((FF_PAIR, D_MODEL, FF_BLOCK), F32), pltpu.VMEM((FF_PAIR, FF_BLOCK, D_MODEL), F32)],
        compiler_params=_params("arbitrary", "arbitrary"),
    )(z2, dup, ru, dh2)


MIX_VEC_ROWS = 16


def _bwd_mix(dh1, proj, h, pool_w, pool_b, pool_scale, conv_w, conv_b, wa, ba, wx, bx, lru_l, w_out, tm, sub, early):
    s_len = dh1.shape[0]
    n = s_len // tm
    ng = len(POOL_WINDOWS)
    ne_ = len(early)

    def body(dh1_ref, proj_ref, h_ref, projh_ref, hh_ref, pw_ref, pb_ref, ps_ref, cw_ref, cb_ref,
             wa_ref, ba_ref, wx_ref, bx_ref, l_ref, wout_ref, *rest):
        early_ref = rest[:ne_]
        dproj_ref, dwout_ref, dpw_ref, dwa_ref, dwx_ref, vec_ref = rest[ne_:ne_ + 6]
        got_ref = rest[ne_ + 6:2 * ne_ + 6]
        a_out, a_pw, a_wa, a_wx, a_vec, c_g, c_dxb, c_ddc, ssem, rsem = rest[2 * ne_ + 6:]
        q = pl.program_id(0)
        i = n - 1 - q

        @pl.when(q == 0)
        def _():
            _scatter_send(early_ref, got_ref, early, ssem, rsem)
            for r in (a_out, a_pw, a_wa, a_wx, a_vec, c_g, c_dxb, c_ddc):
                r[...] = jnp.zeros_like(r)

        has_prev = (i > 0).astype(F32)
        lv = l_ref[...]
        lsl8 = LRU_C * _log_sigmoid(lv)
        cg, cdxb, cddc = c_g[0:1, :], c_dxb[...], c_ddc[...]
        vec, cats, dhbs = {}, [], []

        def add(row, v):
            vec[row] = v if row not in vec else vec[row] + v

        for k in reversed(range(tm // sub)):
            rs = slice(k * sub, (k + 1) * sub)
            t0 = i * tm + k * sub
            dh1b = dh1_ref[rs, :].astype(BF16)
            dcat = _dot_nt(dh1b, wout_ref[...])
            dy_pool = dcat[:, :D_POOL]
            dy_lru = dcat[:, D_POOL:]

            proj = proj_ref[rs, :].astype(F32)
            u_pool = proj[:, :D_POOL]
            u_lru = proj[:, D_POOL:D_POOL + D_LRU]
            u_gate = proj[:, D_POOL + D_LRU:]
            if k > 0:
                halo = proj_ref[k * sub - POOL_HALO:k * sub, :].astype(F32)
                h_prev_row = h_ref[k * sub - 1:k * sub, :]
            else:
                halo = projh_ref[...].astype(F32) * has_prev
                h_prev_row = hh_ref[7:8, :] * has_prev

            d, inv = _pool_diff(u_pool, halo[:, :D_POOL], t0)
            db = d.astype(BF16)
            ypre = jnp.concatenate(
                [_dot(db[:, POOL_GROUP * g:POOL_GROUP * (g + 1)], pw_ref[g]) for g in range(ng)], axis=1) + pb_ref[...]
            dyp = dy_pool * ps_ref[...]
            dypb = dyp.astype(BF16)
            dds = []
            for g in range(ng):
                sl = slice(POOL_GROUP * g, POOL_GROUP * (g + 1))
                a_pw[g] += _dot_tn(db[:, sl], dypb[:, sl])
                dds.append(_dot_nt(dypb[:, sl], pw_ref[g]))
            du_pool, ddc = _pool_diff_bwd(jnp.concatenate(dds, axis=1), inv, cddc)
            cddc = ddc[:POOL_HALO]
            add(0, jnp.sum(dyp, axis=0, keepdims=True))
            add(1, jnp.sum(dy_pool * ypre, axis=0, keepdims=True))

            taps = _conv_taps(u_lru, halo[POOL_HALO - CONV_HALO:, D_POOL:D_POOL + D_LRU])
            xb = cb_ref[...]
            for c in range(CONV_WIDTH):
                xb = xb + taps[c] * cw_ref[c:c + 1, :]
            r, ig, a, mult, first = _lru_gates(xb, wa_ref[...], ba_ref[...], wx_ref[...], bx_ref[...], lsl8, t0)
            hv = h_ref[rs, :]
            gl, th = _gelu(u_gate)
            du_gate = dy_lru * hv * _gelu_grad(u_gate, th)
            cats.insert(0, jnp.concatenate([ypre * ps_ref[...], hv * gl], axis=1).astype(BF16))
            dhbs.insert(0, dh1b)
            last = _rows(a.shape, 0) == sub - 1
            a_next = jnp.where(last, 1.0, pltpu.roll(a, sub - 1, 0))
            pa, gb = _scan_rev(a_next, dy_lru * gl)
            gh = gb + pa * cg
            cg = a[0:1, :] * gh[0:1, :]
            h_prev = jnp.where(_rows(hv.shape, 0) == 0, h_prev_row, pltpu.roll(hv, 1, 0))
            gix = gh * ig * xb
            dla = gh * h_prev * a - jnp.where(first, 0.0, gix * a * a / mult)
            dpre_r = dla * lsl8 * r * (1.0 - r)
            dpre_i = gh * mult * xb * ig * (1.0 - ig)
            dprb = dpre_r.astype(BF16)
            dpib = dpre_i.astype(BF16)
            xbb = xb.astype(BF16)
            a_wa[...] += _dot_tn(xbb, dprb)
            a_wx[...] += _dot_tn(xbb, dpib)
            dxb = gh * mult * ig + _dot_nt(dprb, wa_ref[...]) + _dot_nt(dpib, wx_ref[...])
            add(2, jnp.sum(dxb, axis=0, keepdims=True))
            add(3, jnp.sum(dpre_r, axis=0, keepdims=True))
            add(4, jnp.sum(dpre_i, axis=0, keepdims=True))
            add(5, jnp.sum(dla * r, axis=0, keepdims=True))
            ext = jnp.concatenate([dxb, cdxb], axis=0)
            cdxb = dxb[:CONV_HALO]
            ne = sub + CONV_HALO
            du_lru = dxb * cw_ref[CONV_WIDTH - 1:CONV_WIDTH, :]
            for c in range(CONV_WIDTH):
                add(8 + c, jnp.sum(dxb * taps[c], axis=0, keepdims=True))
                if c < CONV_WIDTH - 1:
                    du_lru = du_lru + pltpu.roll(ext, ne - (CONV_WIDTH - 1 - c), 0)[:sub] * cw_ref[c:c + 1, :]
            dproj_ref[rs, :] = jnp.concatenate([du_pool, du_lru, du_gate], axis=1).astype(BF16)
        a_out[...] += _dot_tn(jnp.concatenate(cats, axis=0), jnp.concatenate(dhbs, axis=0))
        c_g[...] = jnp.broadcast_to(cg, c_g.shape)
        c_dxb[...] = cdxb
        c_ddc[...] = cddc
        for row, v in vec.items():
            a_vec[row:row + 1, :] += v

        @pl.when(q == n - 1)
        def _():
            dwout_ref[...] = a_out[...].astype(BF16)
            dpw_ref[...] = a_pw[...]
            dwa_ref[...] = a_wa[...]
            dwx_ref[...] = a_wx[...]
            vec_ref[...] = a_vec[...]
            vec_ref[5:6, :] = a_vec[5:6, :] * (LRU_C * _sigmoid(-lv))
            _scatter_done(got_ref, early, ssem, rsem)

    rev =lambda w: pl.BlockSpec((tm, w), lambda q: (n - 1 - q, 0))
    halo_p = pl.BlockSpec((POOL_HALO, D_IN_PROJ), lambda q: (jnp.maximum((n - 1 - q) * (tm // POOL_HALO) - 1, 0), 0))
    halo_h = pl.BlockSpec((8, D_LRU), lambda q: (jnp.maximum((n - 1 - q) * (tm // 8) - 1, 0), 0))
    wts = [pool_w, pool_b, pool_scale, conv_w, conv_b, wa, ba, wx, bx, lru_l, w_out]
    outs = pl.pallas_call(
        body, name="bwd_mix", grid=(n,),
        in_specs=[rev(D_MODEL), rev(D_IN_PROJ), rev(D_LRU), halo_p, halo_h] + [_resident(a.shape) for a in wts]
        + [ANY] * ne_,
        out_specs=[rev(D_IN_PROJ), _full((D_MODEL, D_MODEL)), _full((ng, POOL_GROUP, POOL_GROUP)),
                   _full((D_LRU, D_LRU)), _full((D_LRU, D_LRU)), _full((MIX_VEC_ROWS, D_LRU))] + [ANY] * ne_,
        out_shape=[jax.ShapeDtypeStruct((s_len, D_IN_PROJ), BF16), jax.ShapeDtypeStruct((D_MODEL, D_MODEL), BF16),
                   jax.ShapeDtypeStruct((ng, POOL_GROUP, POOL_GROUP), F32), jax.ShapeDtypeStruct((D_LRU, D_LRU), F32),
                   jax.ShapeDtypeStruct((D_LRU, D_LRU), F32), jax.ShapeDtypeStruct((MIX_VEC_ROWS, D_LRU), F32)]
        + _scatter_shapes(early),
        scratch_shapes=[pltpu.VMEM((D_MODEL, D_MODEL), F32), pltpu.VMEM((ng, POOL_GROUP, POOL_GROUP), F32),
                        pltpu.VMEM((D_LRU, D_LRU), F32), pltpu.VMEM((D_LRU, D_LRU), F32),
                        pltpu.VMEM((MIX_VEC_ROWS, D_LRU), F32), pltpu.VMEM((8, D_LRU), F32),
                        pltpu.VMEM((CONV_HALO, D_LRU), F32), pltpu.VMEM((POOL_HALO, D_POOL), F32)] + _scatter_sems(ne_),
        compiler_params=_params("arbitrary"),
    )(dh1, proj, h, proj, h, *wts, *early)
    return outs[:6], outs[6:]


def _bwd_in(dproj, x, dh1, g1, w_in, tm):
    s_len = x.shape[0]
    n = s_len // tm
    cb = D_IN_PROJ // N_CHIPS

    def body(dp_ref, x_ref, dh1_ref, g1_ref, win_ref, dx_ref, dwin_ref, dg1_ref, a_w, a_g):
        i = pl.program_id(0)

        @pl.when(i == 0)
        def _():
            a_w[...] = jnp.zeros_like(a_w)
            a_g[...] = jnp.zeros_like(a_g)

        dp = dp_ref[...]
        xv = x_ref[...]
        zb = (xv * _rstd(xv) * g1_ref[...]).astype(BF16)
        dz = jnp.zeros((tm, D_MODEL), F32)
        for j in range(N_CHIPS):
            dpj = dp[:, cb * j:cb * (j + 1)]
            dz = dz + _dot_nt(dpj, win_ref[j])
            a_w[j] += _dot_tn(zb, dpj)
        dx, dg = _rms_bwd(xv, g1_ref[...], dz)
        dx_ref[...] = dh1_ref[...] + dx
        a_g[0:1, :] += dg

        @pl.when(i == n - 1)
        def _():
            dwin_ref[...] = a_w[...].astype(BF16)
            dg1_ref[...] = a_g[...]

    row = lambda w: pl.BlockSpec((tm, w), lambda i: (i, 0))
    return pl.pallas_call(
        body, name="bwd_in", grid=(n,),
        in_specs=[row(D_IN_PROJ), row(D_MODEL), row(D_MODEL), _resident(g1.shape), _resident(w_in.shape)],
        out_specs=[row(D_MODEL), _full(w_in.shape), _full((8, D_MODEL))],
        out_shape=[jax.ShapeDtypeStruct((s_len, D_MODEL), F32), jax.ShapeDtypeStruct(w_in.shape, BF16),
                   jax.ShapeDtypeStruct((8, D_MODEL), F32)],
        scratch_shapes=[pltpu.VMEM(w_in.shape, F32), pltpu.VMEM((8, D_MODEL), F32)],
        compiler_params=_params("arbitrary"),
    )(dproj, x, dh1, g1, w_in)


def _place():
    x, y, c = lax.axis_index("x"), lax.axis_index("y"), lax.axis_index("c")
    chips = [(1 - x, y), (x, 1 - y), (1 - x, 1 - y)]
    return x, y, c, chips


def _rcopy(src, dst, ssem, rsem, dev):
    return pltpu.make_async_remote_copy(src_ref=src, dst_ref=dst, send_sem=ssem, recv_sem=rsem,
                                        device_id=dev, device_id_type=MESH)


ANY = pl.BlockSpec(memory_space=pl.ANY)
COPY_CHUNK_BYTES = 128 * 1024
ROW_ALIGN = 16


def _row_chunks(rows, row_bytes):
    per = max(ROW_ALIGN, (COPY_CHUNK_BYTES // row_bytes) // ROW_ALIGN * ROW_ALIGN)
    return [(r0, min(per, rows - r0)) for r0 in range(0, rows, per)]


def _row_bytes(a):
    return a.shape[-1] * jnp.dtype(a.dtype).itemsize


def _stack_own(shards, dtypes, pos, steps):
    nw = len(shards)

    def body(pos_ref, *refs):
        for w in range(nw):
            refs[nw + w][0] = refs[w][...].astype(dtypes[w])

    def split(s):
        return s.shape[0] % (steps * ROW_ALIGN) == 0

    ins = [pl.BlockSpec((s.shape[0] // steps, s.shape[1]), lambda i, p: (i, 0)) if split(s)
           else pl.BlockSpec(s.shape, lambda i, p: (0, 0)) for s in shards]
    outs = [pl.BlockSpec((1, s.shape[0] // steps, s.shape[1]), lambda i, p: (p[0], i, 0)) if split(s)
            else pl.BlockSpec((1,) + s.shape, lambda i, p: (p[0], 0, 0)) for s in shards]
    return pl.pallas_call(
        body, name="stack_own",
        grid_spec=pltpu.PrefetchScalarGridSpec(num_scalar_prefetch=1, grid=(steps,), in_specs=ins, out_specs=outs),
        out_shape=[jax.ShapeDtypeStruct((N_CHIPS,) + s.shape, d) for s, d in zip(shards, dtypes)],
        compiler_params=_params("arbitrary"),
    )(pos, *shards)


def _gather_send(outs, stacks, ssem, rsem):
    x, y, c, chips = _place()
    me = 2 * x + y
    for w, st in enumerate(stacks):
        half = st.shape[1] // 2
        for s, (px, py) in enumerate(chips):
            for r0, rs in _row_chunks(half, _row_bytes(st)):
                piece = outs[w].at[me, pl.ds(c * half + r0, rs)]
                _rcopy(piece, piece, ssem.at[w, s], rsem.at[w, s], (px, py, c)).start()


def _gather_pass(outs, stacks, ssem, rsem):
    x, y, c, chips = _place()
    sib = (x, y, 1 - c)
    for w, st in enumerate(stacks):
        half = st.shape[1] // 2
        for s, (px, py) in enumerate(chips):
            blk = outs[w].at[2 * px + py, pl.ds(c * half, half)]
            _rcopy(blk, blk, ssem.at[w, s], rsem.at[w, s], sib).wait_recv()
            for r0, rs in _row_chunks(half, _row_bytes(st)):
                piece = outs[w].at[2 * px + py, pl.ds(c * half + r0, rs)]
                _rcopy(piece, piece, ssem.at[w, 3 + s], rsem.at[w, 3 + s], sib).start()


def _gather_done(outs, stacks, ssem, rsem):
    x, y, c, chips = _place()
    sib = (x, y, 1 - c)
    for w, st in enumerate(stacks):
        half = st.shape[1] // 2
        for s, (px, py) in enumerate(chips):
            blk = outs[w].at[2 * px + py, pl.ds((1 - c) * half, half)]
            _rcopy(blk, blk, ssem.at[w, 3 + s], rsem.at[w, 3 + s], sib).wait_recv()
    for w, st in enumerate(stacks):
        half = st.shape[1] // 2
        blk = outs[w].at[0, pl.ds(0, half)]
        for s in range(6):
            _rcopy(blk, blk, ssem.at[w, s], rsem.at[w, s], sib).wait_send()


def _gather_sems(nw):
    return [pltpu.SemaphoreType.DMA((nw, 6)), pltpu.SemaphoreType.DMA((nw, 6))]


def _gather_weights(stacks):
    nw = len(stacks)

    def body(*refs):
        outs = refs[nw:2 * nw]
        ssem, rsem = refs[2 * nw:]
        _gather_send(outs, stacks, ssem, rsem)
        _gather_pass(outs, stacks, ssem, rsem)
        _gather_done(outs, stacks, ssem, rsem)

    return pl.pallas_call(
        body, name="gather_weights",
        in_specs=[ANY] * nw, out_specs=[ANY] * nw,
        out_shape=[jax.ShapeDtypeStruct(s.shape, s.dtype) for s in stacks],
        input_output_aliases={w: w for w in range(nw)},
        scratch_shapes=_gather_sems(nw),
    )(*stacks)


def _swap_halves(grads, tag):
    nw = len(grads)

    def body(*refs):
        ins, got = refs[:nw], refs[nw:2 * nw]
        ssem, rsem = refs[2 * nw:]
        x, y, c, _ = _place()
        cps = []
        for w in range(nw):
            hr = grads[w].shape[1] // 2
            for k in range(N_CHIPS):
                for r0, rs in _row_chunks(hr, _row_bytes(grads[w])):
                    _rcopy(ins[w].at[k, pl.ds((1 - c) * hr + r0, rs)], got[w].at[k, pl.ds(r0, rs)],
                           ssem.at[w], rsem.at[w], (x, y, 1 - c)).start()
            cps.append(_rcopy(got[w], got[w], ssem.at[w], rsem.at[w], (x, y, 1 - c)))
        for cp in cps:
            cp.wait()

    return pl.pallas_call(
        body, name="swap_halves_" + tag,
        in_specs=[ANY] * nw, out_specs=[ANY] * nw,
        out_shape=[jax.ShapeDtypeStruct((g.shape[0], g.shape[1] // 2, g.shape[2]), g.dtype) for g in grads],
        scratch_shapes=[pltpu.SemaphoreType.DMA((nw,)), pltpu.SemaphoreType.DMA((nw,))],
    )(*grads)


def _add_pairs(grads, got, pos, steps, tag):
    nw = len(grads)

    def body(pos_ref, *refs):
        for w in range(nw):
            refs[2 * nw + w][...] = (refs[w][...].astype(F32) + refs[nw + w][...].astype(F32)).astype(BF16)

    blk = lambda a: (a.shape[0], a.shape[1] // steps, a.shape[2])
    own = [pl.BlockSpec(blk(a), lambda i, p: (0, p[1] * steps + i, 0)) for a in got]
    rec = [pl.BlockSpec(blk(a), lambda i, p: (0, i, 0)) for a in got]
    return pl.pallas_call(
        body, name="add_pairs_" + tag,
        grid_spec=pltpu.PrefetchScalarGridSpec(num_scalar_prefetch=1, grid=(steps,), in_specs=own + rec, out_specs=rec),
        out_shape=[jax.ShapeDtypeStruct(a.shape, BF16) for a in got],
        compiler_params=_params("arbitrary"),
    )(pos, *grads, *got)


def _scatter_send(ins, got, parts, ssem, rsem):
    x, y, c, chips = _place()
    for w, p in enumerate(parts):
        for s, (px, py) in enumerate(chips):
            for r0, rs in _row_chunks(p.shape[1], _row_bytes(p)):
                _rcopy(ins[w].at[2 * px + py, pl.ds(r0, rs)], got[w].at[s, pl.ds(r0, rs)],
                       ssem.at[w, s], rsem.at[w, s], (px, py, c)).start()


def _scatter_done(got, parts, ssem, rsem):
    x, y, c, chips = _place()
    for w in range(len(parts)):
        for s, (px, py) in enumerate(chips):
            _rcopy(got[w].at[s], got[w].at[s], ssem.at[w, s], rsem.at[w, s], (px, py, c)).wait()


def _scatter_sems(nw):
    return [pltpu.SemaphoreType.DMA((nw, 3)), pltpu.SemaphoreType.DMA((nw, 3))]


def _scatter_shapes(parts):
    return [jax.ShapeDtypeStruct((3,) + p.shape[1:], p.dtype) for p in parts]


def _scatter_chips(parts):
    nw = len(parts)

    def body(*refs):
        ins, got = refs[:nw], refs[nw:2 * nw]
        ssem, rsem = refs[2 * nw:]
        _scatter_send(ins, got, parts, ssem, rsem)
        _scatter_done(got, parts, ssem, rsem)

    return pl.pallas_call(
        body, name="scatter_chips",
        in_specs=[ANY] * nw, out_specs=[ANY] * nw, out_shape=_scatter_shapes(parts),
        scratch_shapes=_scatter_sems(nw),
    )(*parts)


def _sum_chips(parts, got, pos, steps):
    nw = len(parts)

    def body(pos_ref, *refs):
        for w in range(nw):
            acc = refs[w][0].astype(F32)
            for s in range(3):
                acc = acc + refs[nw + w][s].astype(F32)
            refs[2 * nw + w][...] = acc

    own = [pl.BlockSpec((1, p.shape[1] // steps, p.shape[2]), lambda i, ps: (ps[0], i, 0)) for p in parts]
    rec = [pl.BlockSpec((3, p.shape[1] // steps, p.shape[2]), lambda i, ps: (0, i, 0)) for p in parts]
    outs = [pl.BlockSpec((p.shape[1] // steps, p.shape[2]), lambda i, ps: (ps[1] * steps + i, 0)) for p in parts]
    return pl.pallas_call(
        body, name="sum_chips",
        grid_spec=pltpu.PrefetchScalarGridSpec(num_scalar_prefetch=1, grid=(steps,), in_specs=own + rec, out_specs=outs),
        out_shape=[jax.ShapeDtypeStruct((2 * p.shape[1], p.shape[2]), F32) for p in parts],
        compiler_params=_params("arbitrary"),
    )(pos, *parts, *got)


def _join_halves(shards):
    nw = len(shards)

    def body(*refs):
        outs = refs[nw:2 * nw]
        ssem, rsem = refs[2 * nw:]
        x, y, c, _ = _place()
        cps = []
        for w in range(nw):
            hr = shards[w].shape[0] // 2
            for r0, rs in _row_chunks(hr, _row_bytes(shards[w])):
                piece = outs[w].at[pl.ds(c * hr + r0, rs)]
                _rcopy(piece, piece, ssem.at[w], rsem.at[w], (x, y, 1 - c)).start()
            mine = outs[w].at[pl.ds(c * hr, hr)]
            cps.append(_rcopy(mine, mine, ssem.at[w], rsem.at[w], (x, y, 1 - c)))
        for cp in cps:
            cp.wait()

    return pl.pallas_call(
        body, name="join_halves",
        in_specs=[ANY] * nw, out_specs=[ANY] * nw,
        out_shape=[jax.ShapeDtypeStruct(s.shape, F32) for s in shards],
        input_output_aliases={w: w for w in range(nw)},
        scratch_shapes=[pltpu.SemaphoreType.DMA((nw,)), pltpu.SemaphoreType.DMA((nw,))],
    )(*shards)


def _scatter_and_allreduce(parts, packed):
    nw = len(parts)
    shape = packed.shape

    def body(*refs):
        ins, p_ref = refs[:nw], refs[nw]
        got, out_ref = refs[nw + 1:2 * nw + 1], refs[2 * nw + 1]
        rbuf, ssem, rsem, bsem_s, bsem_r = refs[2 * nw + 2:]
        x, y, c, _ = _place()
        _scatter_send(ins, got, parts, ssem, rsem)
        out_ref[...] = p_ref[...]
        for st, peer in enumerate([(x, y, 1 - c), (1 - x, y, c), (x, 1 - y, c)]):
            cp = _rcopy(out_ref, rbuf.at[st], bsem_s.at[st], bsem_r.at[st], peer)
            cp.start()
            cp.wait()
            out_ref[...] = out_ref[...] + rbuf[st]
        _scatter_done(got, parts, ssem, rsem)

    vm = pl.BlockSpec(memory_space=pltpu.VMEM)
    outs = pl.pallas_call(
        body, name="scatter_and_allreduce",
        in_specs=[ANY] * nw + [vm], out_specs=[ANY] * nw + [vm],
        out_shape=_scatter_shapes(parts) + [jax.ShapeDtypeStruct(shape, F32)],
        scratch_shapes=[pltpu.VMEM((3,) + shape, F32)] + _scatter_sems(nw)
        + [pltpu.SemaphoreType.DMA((3,)), pltpu.SemaphoreType.DMA((3,))],
        compiler_params=pltpu.CompilerParams(vmem_limit_bytes=VMEM_LIMIT),
    )(*parts, packed)
    return outs[:nw], outs[nw]


def _adamw_math(w, g, m, v):
    m = ADAM_B1 * m + (1.0 - ADAM_B1) * g
    v = ADAM_B2 * v + (1.0 - ADAM_B2) * (g * g)
    delta = -ADAM_LR * ((m * ADAM_C1) / (jnp.sqrt(v * ADAM_C2) + ADAM_EPS) + ADAM_WD * w)
    return delta, m, v


def _adamw(ws, gs, ms, vs, steps, name):
    nw = len(ws)

    def body(*refs):
        for k in range(nw):
            d, m, v = _adamw_math(refs[k][...], refs[nw + k][...], refs[2 * nw + k][...], refs[3 * nw + k][...])
            refs[4 * nw + k][...] = d
            refs[5 * nw + k][...] = m
            refs[6 * nw + k][...] = v

    specs = [pl.BlockSpec((a.shape[0] // steps, a.shape[1]), lambda i: (i, 0)) for a in ws]
    shapes = [jax.ShapeDtypeStruct(a.shape, F32) for a in ws]
    outs = pl.pallas_call(
        body, name=name, grid=(steps,),
        in_specs=specs * 4, out_specs=specs * 3, out_shape=shapes * 3,
        compiler_params=_params("arbitrary"),
    )(*ws, *gs, *ms, *vs)
    return outs[:nw], outs[nw:2 * nw], outs[2 * nw:]


SMALL = ["norm_mix_g", "pool_w", "pool_b", "pool_scale", "conv_b", "gate_a_w", "gate_a_b", "gate_x_w", "gate_x_b",
         "lru_L", "norm_mlp_g", "norm_ple_g", "b_ple_gate", "norm_final_g"]
BIG = ["w_in", "w_out", "w_up", "w_down", "w_ple_gate", "w_ple_proj"]
ORDER = ["norm_mix_g", "w_in", "pool_w", "pool_b", "pool_scale", "conv_w", "conv_b", "gate_a_w", "gate_a_b", "gate_x_w",
         "gate_x_b", "lru_L", "w_out", "norm_mlp_g", "w_up", "w_down", "norm_ple_g", "w_ple_gate", "b_ple_gate",
         "w_ple_proj", "norm_final_g"]
LANES = 128


def _block_diag(w):
    eye = jnp.eye(LRU_HEADS, dtype=w.dtype)
    return jnp.einsum("hij,hk->hikj", w, eye).reshape(D_LRU, D_LRU)


def _diag_blocks(full):
    f = full.reshape(LRU_HEADS, LRU_BLOCK, LRU_HEADS, LRU_BLOCK)
    return jnp.stack([f[h, :, h, :] for h in range(LRU_HEADS)])


def _rows128(a):
    return a.reshape(-1, LANES)


def _pad8(a):
    r = (-a.shape[0]) % 8
    return jnp.pad(a, ((0, r), (0, 0))) if r else a


def kernel(x, p, norm_mix_g, w_in, pool_w, pool_b, pool_scale, conv_w, conv_b, gate_a_w, gate_a_b, gate_x_w, gate_x_b, lru_L, w_out, norm_mlp_g, w_up, w_down, norm_ple_g, w_ple_gate, b_ple_gate, w_ple_proj, norm_final_g, loss_target, m_norm_mix_g, m_w_in, m_pool_w, m_pool_b, m_pool_scale, m_conv_w, m_conv_b, m_gate_a_w, m_gate_a_b, m_gate_x_w, m_gate_x_b, m_lru_L, m_w_out, m_norm_mlp_g, m_w_up, m_w_down, m_norm_ple_g, m_w_ple_gate, m_b_ple_gate, m_w_ple_proj, m_norm_final_g, v_norm_mix_g, v_w_in, v_pool_w, v_pool_b, v_pool_scale, v_conv_w, v_conv_b, v_gate_a_w, v_gate_a_b, v_gate_x_w, v_gate_x_b, v_lru_L, v_w_out, v_norm_mlp_g, v_w_up, v_w_down, v_norm_ple_g, v_w_ple_gate, v_b_ple_gate, v_w_ple_proj, v_norm_final_g):
    W = dict(norm_mix_g=norm_mix_g, w_in=w_in, pool_w=pool_w, pool_b=pool_b, pool_scale=pool_scale, conv_w=conv_w,
             conv_b=conv_b, gate_a_w=gate_a_w, gate_a_b=gate_a_b, gate_x_w=gate_x_w, gate_x_b=gate_x_b, lru_L=lru_L,
             w_out=w_out, norm_mlp_g=norm_mlp_g, w_up=w_up, w_down=w_down, norm_ple_g=norm_ple_g,
             w_ple_gate=w_ple_gate, b_ple_gate=b_ple_gate, w_ple_proj=w_ple_proj, norm_final_g=norm_final_g)
    M = dict(norm_mix_g=m_norm_mix_g, w_in=m_w_in, pool_w=m_pool_w, pool_b=m_pool_b, pool_scale=m_pool_scale,
             conv_w=m_conv_w, conv_b=m_conv_b, gate_a_w=m_gate_a_w, gate_a_b=m_gate_a_b, gate_x_w=m_gate_x_w,
             gate_x_b=m_gate_x_b, lru_L=m_lru_L, w_out=m_w_out, norm_mlp_g=m_norm_mlp_g, w_up=m_w_up, w_down=m_w_down,
             norm_ple_g=m_norm_ple_g, w_ple_gate=m_w_ple_gate, b_ple_gate=m_b_ple_gate, w_ple_proj=m_w_ple_proj,
             norm_final_g=m_norm_final_g)
    V = dict(norm_mix_g=v_norm_mix_g, w_in=v_w_in, pool_w=v_pool_w, pool_b=v_pool_b, pool_scale=v_pool_scale,
             conv_w=v_conv_w, conv_b=v_conv_b, gate_a_w=v_gate_a_w, gate_a_b=v_gate_a_b, gate_x_w=v_gate_x_w,
             gate_x_b=v_gate_x_b, lru_L=v_lru_L, w_out=v_w_out, norm_mlp_g=v_norm_mlp_g, w_up=v_w_up, w_down=v_w_down,
             norm_ple_g=v_norm_ple_g, w_ple_gate=v_w_ple_gate, b_ple_gate=v_b_ple_gate, w_ple_proj=v_w_ple_proj,
             norm_final_g=v_norm_final_g)

    s_len = x.shape[1]
    sub_mix = min(256, s_len)
    tm = min(512, s_len)
    chip = (2 * lax.axis_index("x") + lax.axis_index("y")).astype(jnp.int32)
    pos = jnp.stack([chip, lax.axis_index("c").astype(jnp.int32)])

    shards = [w_in[0], w_out[0], w_up[0], w_down[0], w_ple_gate[0], w_ple_proj[0], jnp.pad(conv_w[0], ((0, 12), (0, 0)))]
    st_in, st_out, st_up, st_dn, st_pg, st_pp, st_cw = _stack_own(shards, [BF16] * 6 + [F32], pos, 8)
    win_g, wout_g, cw_g = _gather_weights([st_in, st_out, st_cw])
    wout_f = wout_g.reshape(D_MODEL, D_MODEL)
    cw_f = jnp.transpose(cw_g[:, :CONV_WIDTH], (1, 0, 2)).reshape(CONV_WIDTH, D_LRU)
    pw_b = pool_w[0].astype(BF16)
    wa_b = _block_diag(gate_a_w[0]).astype(BF16)
    wx_b = _block_diag(gate_x_w[0]).astype(BF16)
    pb_r = pool_b.reshape(1, D_POOL)
    ba_r = gate_a_b.reshape(1, D_LRU)
    bx_r = gate_x_b.reshape(1, D_LRU)
    g4 = norm_final_g.reshape(1, D_MODEL)
    mix_w = (pw_b, pb_r, pool_scale, cw_f, conv_b, wa_b, ba_r, wx_b, bx_r, lru_L, wout_f)

    xs, ps, ts = x[0], p[0, 0], loss_target[0]
    (proj, hst, h1), (wup_g, wdn_g) = _fwd_mix(xs, norm_mix_g, win_g, *mix_w, tm, sub_mix, [st_up, st_dn])
    (z2, ru, h2), (wpg_g, wpp_g) = _fwd_mlp(h1, norm_mlp_g, wup_g, wdn_g, tm, [st_pg, st_pp])
    wpg_f = wpg_g.reshape(D_MODEL, D_MODEL)
    wpp_f = jnp.transpose(wpp_g, (1, 0, 2)).reshape(PLE_DIM, D_MODEL)
    dh2, dh2b, d_wpg, d_wpp, head_vec = _head(h2, ps, ts, norm_ple_g, wpg_f, b_ple_gate, wpp_f, g4, tm)
    dup, dh1, mlp_vec = _bwd_mlp_x(dh2, ru, h1, norm_mlp_g, wup_g, wdn_g, tm)
    d_wup, d_wdn = _bwd_mlp_w(z2, dup, ru, dh2b, tm)
    early = [d_wup, d_wdn, d_wpg.reshape(N_CHIPS, D_MODEL // N_CHIPS, D_MODEL), d_wpp]
    pair_e = _add_pairs(early, _swap_halves(early, "early"), pos, 8, "early")
    (dproj, d_wout, d_pw, d_wa, d_wx, mix_vec), got_e = _bwd_mix(dh1, proj, hst, *mix_w, tm, sub_mix, pair_e)
    dx, d_win, in_vec = _bwd_in(dproj, xs, dh1, norm_mix_g, win_g, tm)

    last = [d_win, d_wout.reshape(N_CHIPS, D_MODEL // N_CHIPS, D_MODEL)]
    pair_l = _add_pairs(last, _swap_halves(last, "last"), pos, 8, "last")

    g_small = {
        "norm_mix_g": in_vec[0:1], "pool_w": d_pw, "pool_b": mix_vec[0:1], "pool_scale": mix_vec[1:2],
        "conv_b": mix_vec[2:3], "gate_a_w": _diag_blocks(d_wa), "gate_a_b": mix_vec[3:4],
        "gate_x_w": _diag_blocks(d_wx), "gate_x_b": mix_vec[4:5], "lru_L": mix_vec[5:6], "norm_mlp_g": mlp_vec[0:1],
        "norm_ple_g": head_vec[0:1], "b_ple_gate": head_vec[2:3], "norm_final_g": head_vec[1:2],
    }
    d_cw = jnp.transpose(mix_vec[8:8 + CONV_WIDTH].reshape(CONV_WIDTH, N_CHIPS, LANES), (1, 0, 2)).reshape(-1, LANES)
    pieces = [_pad8(_rows128(g_small[k])) for k in SMALL] + [d_cw, _pad8(head_vec[3:4, :LANES])]
    offs = [0]
    for pc in pieces:
        offs.append(offs[-1] + pc.shape[0])
    got_l, red = _scatter_and_allreduce(pair_l, jnp.concatenate(pieces, axis=0))
    g_big = _join_halves(_sum_chips(pair_l + pair_e, got_l + got_e, pos, 8))
    loss = red[offs[-2], 0]
    g_cw = lax.dynamic_slice(red, (offs[len(SMALL)] + CONV_WIDTH * chip, 0), (CONV_WIDTH, LANES))

    def packed(src):
        return jnp.concatenate([_pad8(_rows128(src[k])) for k in SMALL] + [_pad8(src["conv_w"][0])], axis=0)

    n_small = offs[len(SMALL)]
    g_pack = jnp.concatenate([red[:n_small], _pad8(g_cw)], axis=0)
    (d_pack,), (m_pack,), (v_pack,) = _adamw([packed(W)], [g_pack], [packed(M)], [packed(V)], 1, "adamw_small")

    big2d = lambda src: [src[k][0] for k in BIG]
    d_big, m_big, v_big = _adamw(big2d(W), g_big, big2d(M), big2d(V), 8, "adamw_big")

    def unpack(pack, big_list):
        out = {}
        for idx, k in enumerate(SMALL):
            n_el = W[k].size
            out[k] = pack[offs[idx]:offs[idx + 1]].reshape(-1)[:n_el].reshape(W[k].shape)
        out["conv_w"] = pack[n_small:n_small + CONV_WIDTH].reshape(W["conv_w"].shape)
        for k, a in zip(BIG, big_list):
            out[k] = a.reshape(W[k].shape)
        return out

    grads = unpack(g_pack, g_big)
    deltas = unpack(d_pack, d_big)
    new_m = unpack(m_pack, m_big)
    new_v = unpack(v_pack, v_big)
    return (loss, dx[None], *[grads[k] for k in ORDER], *[deltas[k] for k in ORDER],
            *[new_m[k] for k in ORDER], *[new_v[k] for k in ORDER])
```

```python
import functools

import jax
import jax.numpy as jnp
from jax import lax
from jax.experimental import pallas as pl
from jax.experimental.pallas import tpu as pltpu

F32 = jnp.float32
BF16 = jnp.bfloat16
MESH = pl.DeviceIdType.MESH

D_MODEL = 1024
D_POOL = 512
D_LRU = 512
POOL_WINDOWS = (2, 4, 8, 16)
POOL_GROUP = 128
POOL_HALO = 16
CONV_WIDTH = 4
CONV_HALO = 8
PASS_STEPS = 2
LRU_HEADS = 8
LRU_BLOCK = 64
LRU_C = 8.0
D_FF = 4096
PLE_DIM = 256
D_IN_PROJ = 1536
RMS_EPS = 1e-6
N_CHIPS = 4
FF_BLOCK = D_FF // N_CHIPS
FF_PAIR = 2

ADAM_LR = 0.001
ADAM_B1 = 0.9
ADAM_B2 = 0.999
ADAM_EPS = 1e-08
ADAM_WD = 0.01
ADAM_STEP = 10
ADAM_C1 = 1.0 / (1.0 - ADAM_B1 ** ADAM_STEP)
ADAM_C2 = 1.0 / (1.0 - ADAM_B2 ** ADAM_STEP)

VMEM_LIMIT = 56 * 1024 * 1024
GELU_C = 0.7978845608028654
GELU_A = 0.044715

NT = (((1,), (1,)), ((), ()))
TN = (((0,), (0,)), ((), ()))


def _dot(a, b):
    return jnp.dot(a, b, preferred_element_type=F32)


def _dot_nt(a, b):
    return lax.dot_general(a, b, NT, preferred_element_type=F32)


def _dot_tn(a, b):
    return lax.dot_general(a, b, TN, preferred_element_type=F32)


def _params(*sem):
    return pltpu.CompilerParams(dimension_semantics=sem, vmem_limit_bytes=VMEM_LIMIT)


def _full(shape):
    nd = len(shape)
    return pl.BlockSpec(shape, lambda *_: (0,) * nd)


def _resident(shape):
    nd = len(shape)
    return pl.BlockSpec(shape, lambda *_: (0,) * nd, pipeline_mode=pl.Buffered(1))


def _rstd(x):
    return lax.rsqrt(jnp.mean(x * x, axis=-1, keepdims=True) + RMS_EPS)


def _rms_bwd(x, g, dz):
    xr = x * _rstd(x)
    r = _rstd(x)
    dyg = dz * g
    dx = r * (dyg - xr * jnp.mean(dyg * xr, axis=-1, keepdims=True))
    dg = jnp.sum(dz * xr, axis=0, keepdims=True)
    return dx, dg


def _sigmoid(x):
    return 1.0 / (1.0 + jnp.exp(-x))


def _log_sigmoid(v):
    u = jnp.exp(-jnp.abs(v))
    w = 1.0 + u
    l1p = jnp.where(w == 1.0, u, jnp.log(w) * u / jnp.where(w == 1.0, 1.0, w - 1.0))
    return jnp.minimum(v, 0.0) - l1p


def _gelu(x):
    t = jnp.tanh(GELU_C * (x + GELU_A * x * x * x))
    return 0.5 * x * (1.0 + t), t


def _gelu_grad(x, t):
    return 0.5 * (1.0 + t) + 0.5 * x * (1.0 - t * t) * GELU_C * (1.0 + 3.0 * GELU_A * x * x)


def _rows(shape, t0):
    return lax.broadcasted_iota(jnp.int32, shape, 0) + t0


def _pool_diff(u_pool, prev, t0):
    tm = u_pool.shape[0]
    rows = _rows((tm, POOL_GROUP), t0)
    outs, invs = [], []
    for g, w in enumerate(POOL_WINDOWS):
        sl = slice(POOL_GROUP * g, POOL_GROUP * (g + 1))
        ug = u_pool[:, sl]
        s = jnp.concatenate([prev[:, sl], ug], axis=0)
        k = 1
        while k < w:
            s = s + pltpu.roll(s, k, 0)
            k *= 2
        inv = 1.0 / jnp.minimum(rows + 1, w).astype(F32)
        outs.append(s[POOL_HALO:] * inv - ug)
        invs.append(inv)
    return jnp.concatenate(outs, axis=1), jnp.concatenate(invs, axis=1)


def _pool_diff_bwd(dd, inv, nxt):
    ddc = dd * inv
    outs = []
    for g, w in enumerate(POOL_WINDOWS):
        sl = slice(POOL_GROUP * g, POOL_GROUP * (g + 1))
        s = jnp.concatenate([ddc[:, sl], nxt[:, sl]], axis=0)
        n = s.shape[0]
        k = 1
        while k < w:
            s = s + pltpu.roll(s, n - k, 0)
            k *= 2
        outs.append(s[:n - POOL_HALO] - dd[:, sl])
    return jnp.concatenate(outs, axis=1), ddc


def _conv_taps(u, prev):
    ext = jnp.concatenate([prev, u], axis=0)
    return [pltpu.roll(ext, CONV_WIDTH - 1 - k, 0)[CONV_HALO:] if k < CONV_WIDTH - 1 else u for k in range(CONV_WIDTH)]


def _scan_fwd(a, b):
    tm = a.shape[0]
    rows = _rows(a.shape, 0)
    k = 1
    while k < tm:
        ar = pltpu.roll(a, k, 0)
        br = pltpu.roll(b, k, 0)
        m = rows >= k
        b = jnp.where(m, a * br + b, b)
        a = jnp.where(m, a * ar, a)
        k *= 2
    return a, b


def _scan_rev(a, b):
    tm = a.shape[0]
    rows = _rows(a.shape, 0)
    k = 1
    while k < tm:
        ar = pltpu.roll(a, tm - k, 0)
        br = pltpu.roll(b, tm - k, 0)
        m = rows < tm - k
        b = jnp.where(m, a * br + b, b)
        a = jnp.where(m, a * ar, a)
        k *= 2
    return a, b


def _lru_gates(xb, wa, ba, wx, bx, lsl8, t0):
    xbb = xb.astype(BF16)
    r = _sigmoid(_dot(xbb, wa) + ba)
    ig = _sigmoid(_dot(xbb, wx) + bx)
    a = jnp.exp(r * lsl8)
    first = _rows(xb.shape, t0) == 0
    mult = jnp.where(first, 1.0, jnp.sqrt(1.0 - a * a))
    return r, ig, a, mult, first


def _fwd_mix(x, g1, w_in, pool_w, pool_b, pool_scale, conv_w, conv_b, wa, ba, wx, bx, lru_l, w_out, tm, sub, late):
    s_len = x.shape[0]
    n = s_len // tm
    nl = len(late)

    def body(x_ref, g1_ref, win_ref, pw_ref, pb_ref, ps_ref, cw_ref, cb_ref, wa_ref, ba_ref, wx_ref, bx_ref, l_ref,
             wout_ref, *rest):
        proj_ref, h_ref, h1_ref = rest[nl:nl + 3]
        late_ref = rest[nl + 3:2 * nl + 3]
        cpool, clru, ch, ssem, rsem = rest[2 * nl + 3:]
        i = pl.program_id(0)

        @pl.when(i == 0)
        def _():
            _gather_send(late_ref, late, ssem, rsem)
            cpool[...] = jnp.zeros_like(cpool)
            clru[...] = jnp.zeros_like(clru)
            ch[...] = jnp.zeros_like(ch)

        lsl8 = LRU_C * _log_sigmoid(l_ref[...])
        cp, cl, hc = cpool[...], clru[...], ch[7:8, :]
        def in_proj(k):
            rs = slice(k * sub, (k + 1) * sub)
            xv = x_ref[rs, :]
            zb = (xv * _rstd(xv) * g1_ref[...]).astype(BF16)
            proj = jnp.concatenate([_dot(zb, win_ref[j]) for j in range(N_CHIPS)], axis=1)
            proj_ref[rs, :] = proj.astype(BF16)
            return xv, proj

        nxt = in_proj(0)
        for k in range(tm // sub):
            rs = slice(k * sub, (k + 1) * sub)
            t0 = i * tm + k * sub
            xv, proj = nxt
            if k + 1 < tm // sub:
                nxt = in_proj(k + 1)
            u_pool = proj[:, :D_POOL]
            u_lru = proj[:, D_POOL:D_POOL + D_LRU]
            u_gate = proj[:, D_POOL + D_LRU:]

            d, _ = _pool_diff(u_pool, cp, t0)
            cp = u_pool[sub - POOL_HALO:]
            db = d.astype(BF16)
            yp = jnp.concatenate(
                [_dot(db[:, POOL_GROUP * g:POOL_GROUP * (g + 1)], pw_ref[g]) for g in range(len(POOL_WINDOWS))], axis=1)
            y_pool = (yp + pb_ref[...]) * ps_ref[...]

            taps = _conv_taps(u_lru, cl)
            cl = u_lru[sub - CONV_HALO:]
            xb = cb_ref[...]
            for q in range(CONV_WIDTH):
                xb = xb + taps[q] * cw_ref[q:q + 1, :]
            _, ig, a, mult, _ = _lru_gates(xb, wa_ref[...], ba_ref[...], wx_ref[...], bx_ref[...], lsl8, t0)
            pa, hb = _scan_fwd(a, mult * (ig * xb))
            h = hb + pa * hc
            hc = h[sub - 1:sub, :]
            h_ref[rs, :] = h
            gl, _ = _gelu(u_gate)
            cat = jnp.concatenate([y_pool, h * gl], axis=1).astype(BF16)
            h1_ref[rs, :] = xv + _dot(cat, wout_ref[...])
        cpool[...] = cp
        clru[...] = cl
        ch[...] = jnp.broadcast_to(hc, ch.shape)

        @pl.when(i == max(n - PASS_STEPS, 0))
        def _():
            _gather_pass(late_ref, late, ssem, rsem)

        @pl.when(i == n - 1)
        def _():
            _gather_done(late_ref, late, ssem, rsem)

    row = lambda w: pl.BlockSpec((tm, w), lambda i: (i, 0))
    ins = [x, g1, w_in, pool_w, pool_b, pool_scale, conv_w, conv_b, wa, ba, wx, bx, lru_l, w_out]
    outs = pl.pallas_call(
        body, name="fwd_mix", grid=(n,),
        in_specs=[row(D_MODEL)] + [_resident(a.shape) for a in ins[1:]] + [ANY] * nl,
        out_specs=[row(D_IN_PROJ), row(D_LRU), row(D_MODEL)] + [ANY] * nl,
        out_shape=[jax.ShapeDtypeStruct((s_len, D_IN_PROJ), BF16), jax.ShapeDtypeStruct((s_len, D_LRU), F32),
                   jax.ShapeDtypeStruct((s_len, D_MODEL), F32)]
        + [jax.ShapeDtypeStruct(a.shape, a.dtype) for a in late],
        input_output_aliases={len(ins) + k: 3 + k for k in range(nl)},
        scratch_shapes=[pltpu.VMEM((POOL_HALO, D_POOL), F32), pltpu.VMEM((CONV_HALO, D_LRU), F32),
                        pltpu.VMEM((8, D_LRU), F32)] + _gather_sems(nl),
        compiler_params=_params("arbitrary"),
    )(*ins, *late)
    return outs[:3], outs[3:]


def _fwd_mlp(h1, g2, w_up, w_down, tm, late):
    s_len = h1.shape[0]
    n = s_len // tm
    nl = len(late)

    def body(h1_ref, g2_ref, wup_ref, wdn_ref, *rest):
        z2_ref, ru_ref, h2_ref = rest[nl:nl + 3]
        late_ref = rest[nl + 3:2 * nl + 3]
        ssem, rsem = rest[2 * nl + 3:]
        i = pl.program_id(0)

        @pl.when(i == 0)
        def _():
            _gather_send(late_ref, late, ssem, rsem)

        hv = h1_ref[...]
        zb = (hv * _rstd(hv) * g2_ref[...]).astype(BF16)
        z2_ref[...] = zb
        acc = hv
        for j in range(N_CHIPS):
            ru = jnp.maximum(_dot(zb, wup_ref[j]), 0.0)
            ru_ref[:, FF_BLOCK * j:FF_BLOCK * (j + 1)] = ru.astype(BF16)
            acc = acc + _dot((ru * ru).astype(BF16), wdn_ref[j])
        h2_ref[...] = acc

        @pl.when(i == max(n - PASS_STEPS, 0))
        def _():
            _gather_pass(late_ref, late, ssem, rsem)

        @pl.when(i == n - 1)
        def _():
            _gather_done(late_ref, late, ssem, rsem)

    row = lambda w: pl.BlockSpec((tm, w), lambda i: (i, 0))
    outs = pl.pallas_call(
        body, name="fwd_mlp", grid=(n,),
        in_specs=[row(D_MODEL), _full(g2.shape), _resident(w_up.shape), _resident(w_down.shape)] + [ANY] * nl,
        out_specs=[row(D_MODEL), row(D_FF), row(D_MODEL)] + [ANY] * nl,
        out_shape=[jax.ShapeDtypeStruct((s_len, D_MODEL), BF16), jax.ShapeDtypeStruct((s_len, D_FF), BF16),
                   jax.ShapeDtypeStruct((s_len, D_MODEL), F32)] + [jax.ShapeDtypeStruct(a.shape, a.dtype) for a in late],
        input_output_aliases={4 + k: 3 + k for k in range(nl)},
        scratch_shapes=_gather_sems(nl),
        compiler_params=_params("arbitrary"),
    )(h1, g2, w_up, w_down, *late)
    return outs[:3], outs[3:]


def _head(h2, p, target, g3, w_pg, b_pg, w_pp, g4, tm):
    s_len = h2.shape[0]
    n = s_len // tm

    def body(h2_ref, p_ref, t_ref, g3_ref, wpg_ref, bpg_ref, wpp_ref, g4_ref,
             dh2_ref, dh2b_ref, dwpg_ref, dwpp_ref, vec_ref, a_pg, a_pp, a_vec):
        i = pl.program_id(0)

        @pl.when(i == 0)
        def _():
            a_pg[...] = jnp.zeros_like(a_pg)
            a_pp[...] = jnp.zeros_like(a_pp)
            a_vec[...] = jnp.zeros_like(a_vec)

        h2v = h2_ref[...]
        g3v = g3_ref[...]
        g4v = g4_ref[...]
        z3 = (h2v * _rstd(h2v) * g3v).astype(BF16)
        gate = _sigmoid(_dot(z3, wpg_ref[...]) + bpg_ref[...])
        pb = p_ref[...].astype(BF16)
        pp = _dot(pb, wpp_ref[...])
        h3 = h2v + gate * pp
        r4 = _rstd(h3)
        diff = h3 * r4 * g4v - t_ref[...]
        loss = 0.5 * jnp.sum(jnp.mean(diff * diff, axis=-1, keepdims=True), axis=0, keepdims=True)
        dy = diff * (1.0 / D_MODEL)
        dh3, dg4 = _rms_bwd(h3, g4v, dy)
        dpp = (dh3 * gate).astype(BF16)
        dpre = dh3 * pp * gate * (1.0 - gate)
        dpreb = dpre.astype(BF16)
        dz3 = _dot_nt(dpreb, wpg_ref[...])
        dx, dg3 = _rms_bwd(h2v, g3v, dz3)
        dh2 = dh3 + dx
        dh2_ref[...] = dh2
        dh2b_ref[...] = dh2.astype(BF16)
        a_pg[...] += _dot_tn(z3, dpreb)
        a_pp[...] += _dot_tn(pb, dpp)
        a_vec[0:1, :] += dg3
        a_vec[1:2, :] += dg4
        a_vec[2:3, :] += jnp.sum(dpre, axis=0, keepdims=True)
        a_vec[3:4, :] += jnp.broadcast_to(loss, (1, D_MODEL))

        @pl.when(i == n - 1)
        def _():
            dwpg_ref[...] = a_pg[...].astype(BF16)
            for j in range(N_CHIPS):
                dwpp_ref[j] = a_pp[:, PLE_DIM * j:PLE_DIM * (j + 1)].astype(BF16)
            vec_ref[...] = a_vec[...]

    row = lambda w: pl.BlockSpec((tm, w), lambda i: (i, 0))
    ins = [h2, p, target, g3, w_pg, b_pg, w_pp, g4]
    return pl.pallas_call(
        body, name="head", grid=(n,),
        in_specs=[row(D_MODEL), row(PLE_DIM), row(D_MODEL)] + [_resident(a.shape) for a in ins[3:]],
        out_specs=[row(D_MODEL), row(D_MODEL), _full((D_MODEL, D_MODEL)), _full((N_CHIPS, PLE_DIM, PLE_DIM)),
                   _full((8, D_MODEL))],
        out_shape=[jax.ShapeDtypeStruct((s_len, D_MODEL), F32), jax.ShapeDtypeStruct((s_len, D_MODEL), BF16),
                   jax.ShapeDtypeStruct((D_MODEL, D_MODEL), BF16),
                   jax.ShapeDtypeStruct((N_CHIPS, PLE_DIM, PLE_DIM), BF16), jax.ShapeDtypeStruct((8, D_MODEL), F32)],
        scratch_shapes=[pltpu.VMEM((D_MODEL, D_MODEL), F32), pltpu.VMEM((PLE_DIM, D_MODEL), F32),
                        pltpu.VMEM((8, D_MODEL), F32)],
        compiler_params=_params("arbitrary"),
    )(*ins)


def _bwd_mlp_x(dh2, ru, h1, g2, w_up, w_down, tm):
    s_len = dh2.shape[0]
    n = s_len // tm

    def body(dh2_ref, ru_ref, h1_ref, g2_ref, wup_ref, wdn_ref, dup_ref, dh1_ref, dg2_ref, a_g):
        i = pl.program_id(0)

        @pl.when(i == 0)
        def _():
            a_g[...] = jnp.zeros_like(a_g)

        dh2v = dh2_ref[...]
        dhb = dh2v.astype(BF16)
        acc = jnp.zeros((tm, D_MODEL), F32)
        for j in range(N_CHIPS):
            sl = slice(FF_BLOCK * j, FF_BLOCK * (j + 1))
            dup = (_dot_nt(dhb, wdn_ref[j]) * (2.0 * ru_ref[:, sl].astype(F32))).astype(BF16)
            dup_ref[:, sl] = dup
            acc = acc + _dot_nt(dup, wup_ref[j])
        dx, dg = _rms_bwd(h1_ref[...], g2_ref[...], acc)
        dh1_ref[...] = dh2v + dx
        a_g[0:1, :] += dg

        @pl.when(i == n - 1)
        def _():
            dg2_ref[...] = a_g[...]

    row = lambda w: pl.BlockSpec((tm, w), lambda i: (i, 0))
    return pl.pallas_call(
        body, name="bwd_mlp_x", grid=(n,),
        in_specs=[row(D_MODEL), row(D_FF), row(D_MODEL), _full(g2.shape), _resident(w_up.shape), _resident(w_down.shape)],
        out_specs=[row(D_FF), row(D_MODEL), _full((8, D_MODEL))],
        out_shape=[jax.ShapeDtypeStruct((s_len, D_FF), BF16), jax.ShapeDtypeStruct((s_len, D_MODEL), F32),
                   jax.ShapeDtypeStruct((8, D_MODEL), F32)],
        scratch_shapes=[pltpu.VMEM((8, D_MODEL), F32)],
        compiler_params=_params("arbitrary"),
    )(dh2, ru, h1, g2, w_up, w_down)


def _bwd_mlp_w(z2, dup, ru, dh2, tk):
    s_len = z2.shape[0]
    n = s_len // tk

    def body(z2_ref, dup_ref, ru_ref, dh2_ref, dwup_ref, dwdn_ref, a_up, a_dn):
        t = pl.program_id(1)

        @pl.when(t == 0)
        def _():
            a_up[...] = jnp.zeros_like(a_up)
            a_dn[...] = jnp.zeros_like(a_dn)

        for b in range(FF_PAIR):
            sl = slice(FF_BLOCK * b, FF_BLOCK * (b + 1))
            ruv = ru_ref[:, sl]
            a_up[b] += _dot_tn(z2_ref[...], dup_ref[:, sl])
            a_dn[b] += _dot_tn(ruv * ruv, dh2_ref[...])

        @pl.when(t == n - 1)
        def _():
            dwup_ref[...] = a_up[...].astype(BF16)
            dwdn_ref[...] = a_dn[...].astype(BF16)

    tile = pl.BlockSpec((tk, D_MODEL), lambda j, t: (t, 0))
    ffb = pl.BlockSpec((tk, FF_PAIR * FF_BLOCK), lambda j, t: (t, j))
    return pl.pallas_call(
        body, name="bwd_mlp_w", grid=(N_CHIPS // FF_PAIR, n),
        in_specs=[tile, ffb, ffb, tile],
        out_specs=[pl.BlockSpec((FF_PAIR, D_MODEL, FF_BLOCK), lambda j, t: (j, 0, 0)),
                   pl.BlockSpec((FF_PAIR, FF_BLOCK, D_MODEL), lambda j, t: (j, 0, 0))],
        out_shape=[jax.ShapeDtypeStruct((N_CHIPS, D_MODEL, FF_BLOCK), BF16),
                   jax.ShapeDtypeStruct((N_CHIPS, FF_BLOCK, D_MODEL), BF16)],
        scratch_shapes=[pltpu.VMEM((FF_PAIR, D_MODEL, FF_BLOCK), F32), pltpu.VMEM((FF_PAIR, FF_BLOCK, D_MODEL), F32)],
        compiler_params=_params("arbitrary", "arbitrary"),
    )(z2, dup, ru, dh2)


MIX_VEC_ROWS = 16


def _bwd_mix(dh1, proj, h, pool_w, pool_b, pool_scale, conv_w, conv_b, wa, ba, wx, bx, lru_l, w_out, tm, sub, early):
    s_len = dh1.shape[0]
    n = s_len // tm
    ng = len(POOL_WINDOWS)
    ne_ = len(early)

    def body(dh1_ref, proj_ref, h_ref, projh_ref, hh_ref, pw_ref, pb_ref, ps_ref, cw_ref, cb_ref,
             wa_ref, ba_ref, wx_ref, bx_ref, l_ref, wout_ref, *rest):
        early_ref = rest[:ne_]
        dproj_ref, dwout_ref, dpw_ref, dwa_ref, dwx_ref, vec_ref = rest[ne_:ne_ + 6]
        got_ref = rest[ne_ + 6:2 * ne_ + 6]
        a_out, a_pw, a_wa, a_wx, a_vec, c_g, c_dxb, c_ddc, ssem, rsem = rest[2 * ne_ + 6:]
        q = pl.program_id(0)
        i = n - 1 - q

        @pl.when(q == 0)
        def _():
            _scatter_send(early_ref, got_ref, early, ssem, rsem)
            for r in (a_out, a_pw, a_wa, a_wx, a_vec, c_g, c_dxb, c_ddc):
                r[...] = jnp.zeros_like(r)

        has_prev = (i > 0).astype(F32)
        lv = l_ref[...]
        lsl8 = LRU_C * _log_sigmoid(lv)
        cg, cdxb, cddc = c_g[0:1, :], c_dxb[...], c_ddc[...]
        vec, cats, dhbs = {}, [], []

        def add(row, v):
            vec[row] = v if row not in vec else vec[row] + v

        for k in reversed(range(tm // sub)):
            rs = slice(k * sub, (k + 1) * sub)
            t0 = i * tm + k * sub
            dh1b = dh1_ref[rs, :].astype(BF16)
            dcat = _dot_nt(dh1b, wout_ref[...])
            dy_pool = dcat[:, :D_POOL]
            dy_lru = dcat[:, D_POOL:]

            proj = proj_ref[rs, :].astype(F32)
            u_pool = proj[:, :D_POOL]
            u_lru = proj[:, D_POOL:D_POOL + D_LRU]
            u_gate = proj[:, D_POOL + D_LRU:]
            if k > 0:
                halo = proj_ref[k * sub - POOL_HALO:k * sub, :].astype(F32)
                h_prev_row = h_ref[k * sub - 1:k * sub, :]
            else:
                halo = projh_ref[...].astype(F32) * has_prev
                h_prev_row = hh_ref[7:8, :] * has_prev

            d, inv = _pool_diff(u_pool, halo[:, :D_POOL], t0)
            db = d.astype(BF16)
            ypre = jnp.concatenate(
                [_dot(db[:, POOL_GROUP * g:POOL_GROUP * (g + 1)], pw_ref[g]) for g in range(ng)], axis=1) + pb_ref[...]
            dyp = dy_pool * ps_ref[...]
            dypb = dyp.astype(BF16)
            dds = []
            for g in range(ng):
                sl = slice(POOL_GROUP * g, POOL_GROUP * (g + 1))
                a_pw[g] += _dot_tn(db[:, sl], dypb[:, sl])
                dds.append(_dot_nt(dypb[:, sl], pw_ref[g]))
            du_pool, ddc = _pool_diff_bwd(jnp.concatenate(dds, axis=1), inv, cddc)
            cddc = ddc[:POOL_HALO]
            add(0, jnp.sum(dyp, axis=0, keepdims=True))
            add(1, jnp.sum(dy_pool * ypre, axis=0, keepdims=True))

            taps = _conv_taps(u_lru, halo[POOL_HALO - CONV_HALO:, D_POOL:D_POOL + D_LRU])
            xb = cb_ref[...]
            for c in range(CONV_WIDTH):
                xb = xb + taps[c] * cw_ref[c:c + 1, :]
            r, ig, a, mult, first = _lru_gates(xb, wa_ref[...], ba_ref[...], wx_ref[...], bx_ref[...], lsl8, t0)
            hv = h_ref[rs, :]
            gl, th = _gelu(u_gate)
            du_gate = dy_lru * hv * _gelu_grad(u_gate, th)
            cats.insert(0, jnp.concatenate([ypre * ps_ref[...], hv * gl], axis=1).astype(BF16))
            dhbs.insert(0, dh1b)
            last = _rows(a.shape, 0) == sub - 1
            a_next = jnp.where(last, 1.0, pltpu.roll(a, sub - 1, 0))
            pa, gb = _scan_rev(a_next, dy_lru * gl)
            gh = gb + pa * cg
            cg = a[0:1, :] * gh[0:1, :]
            h_prev = jnp.where(_rows(hv.shape, 0) == 0, h_prev_row, pltpu.roll(hv, 1, 0))
            gix = gh * ig * xb
            dla = gh * h_prev * a - jnp.where(first, 0.0, gix * a * a / mult)
            dpre_r = dla * lsl8 * r * (1.0 - r)
            dpre_i = gh * mult * xb * ig * (1.0 - ig)
            dprb = dpre_r.astype(BF16)
            dpib = dpre_i.astype(BF16)
            xbb = xb.astype(BF16)
            a_wa[...] += _dot_tn(xbb, dprb)
            a_wx[...] += _dot_tn(xbb, dpib)
            dxb = gh * mult * ig + _dot_nt(dprb, wa_ref[...]) + _dot_nt(dpib, wx_ref[...])
            add(2, jnp.sum(dxb, axis=0, keepdims=True))
            add(3, jnp.sum(dpre_r, axis=0, keepdims=True))
            add(4, jnp.sum(dpre_i, axis=0, keepdims=True))
            add(5, jnp.sum(dla * r, axis=0, keepdims=True))
            ext = jnp.concatenate([dxb, cdxb], axis=0)
            cdxb = dxb[:CONV_HALO]
            ne = sub + CONV_HALO
            du_lru = dxb * cw_ref[CONV_WIDTH - 1:CONV_WIDTH, :]
            for c in range(CONV_WIDTH):
                add(8 + c, jnp.sum(dxb * taps[c], axis=0, keepdims=True))
                if c < CONV_WIDTH - 1:
                    du_lru = du_lru + pltpu.roll(ext, ne - (CONV_WIDTH - 1 - c), 0)[:sub] * cw_ref[c:c + 1, :]
            dproj_ref[rs, :] = jnp.concatenate([du_pool, du_lru, du_gate], axis=1).astype(BF16)
        a_out[...] += _dot_tn(jnp.concatenate(cats, axis=0), jnp.concatenate(dhbs, axis=0))
        c_g[...] = jnp.broadcast_to(cg, c_g.shape)
        c_dxb[...] = cdxb
        c_ddc[...] = cddc
        for row, v in vec.items():
            a_vec[row:row + 1, :] += v

        @pl.when(q == n - 1)
        def _():
            dwout_ref[...] = a_out[...].astype(BF16)
            dpw_ref[...] = a_pw[...]
            dwa_ref[...] = a_wa[...]
            dwx_ref[...] = a_wx[...]
            vec_ref[...] = a_vec[...]
            vec_ref[5:6, :] = a_vec[5:6, :] * (LRU_C * _sigmoid(-lv))
            _scatter_done(got_ref, early, ssem, rsem)

    rev =lambda w: pl.BlockSpec((tm, w), lambda q: (n - 1 - q, 0))
    halo_p = pl.BlockSpec((POOL_HALO, D_IN_PROJ), lambda q: (jnp.maximum((n - 1 - q) * (tm // POOL_HALO) - 1, 0), 0))
    halo_h = pl.BlockSpec((8, D_LRU), lambda q: (jnp.maximum((n - 1 - q) * (tm // 8) - 1, 0), 0))
    wts = [pool_w, pool_b, pool_scale, conv_w, conv_b, wa, ba, wx, bx, lru_l, w_out]
    outs = pl.pallas_call(
        body, name="bwd_mix", grid=(n,),
        in_specs=[rev(D_MODEL), rev(D_IN_PROJ), rev(D_LRU), halo_p, halo_h] + [_resident(a.shape) for a in wts]
        + [ANY] * ne_,
        out_specs=[rev(D_IN_PROJ), _full((D_MODEL, D_MODEL)), _full((ng, POOL_GROUP, POOL_GROUP)),
                   _full((D_LRU, D_LRU)), _full((D_LRU, D_LRU)), _full((MIX_VEC_ROWS, D_LRU))] + [ANY] * ne_,
        out_shape=[jax.ShapeDtypeStruct((s_len, D_IN_PROJ), BF16), jax.ShapeDtypeStruct((D_MODEL, D_MODEL), BF16),
                   jax.ShapeDtypeStruct((ng, POOL_GROUP, POOL_GROUP), F32), jax.ShapeDtypeStruct((D_LRU, D_LRU), F32),
                   jax.ShapeDtypeStruct((D_LRU, D_LRU), F32), jax.ShapeDtypeStruct((MIX_VEC_ROWS, D_LRU), F32)]
        + _scatter_shapes(early),
        scratch_shapes=[pltpu.VMEM((D_MODEL, D_MODEL), F32), pltpu.VMEM((ng, POOL_GROUP, POOL_GROUP), F32),
                        pltpu.VMEM((D_LRU, D_LRU), F32), pltpu.VMEM((D_LRU, D_LRU), F32),
                        pltpu.VMEM((MIX_VEC_ROWS, D_LRU), F32), pltpu.VMEM((8, D_LRU), F32),
                        pltpu.VMEM((CONV_HALO, D_LRU), F32), pltpu.VMEM((POOL_HALO, D_POOL), F32)] + _scatter_sems(ne_),
        compiler_params=_params("arbitrary"),
    )(dh1, proj, h, proj, h, *wts, *early)
    return outs[:6], outs[6:]


def _bwd_in(dproj, x, dh1, g1, w_in, tm, packed):
    s_len = x.shape[0]
    n = s_len // tm
    cb = D_IN_PROJ // N_CHIPS
    marks = [(k * n) // 4 for k in range(4)]

    def body(dp_ref, x_ref, dh1_ref, g1_ref, win_ref, pk_ref, dx_ref, dwin_ref, dg1_ref, red_ref,
             a_w, a_g, acc, rbuf, ssem, rsem):
        i = pl.program_id(0)
        px, py, pc, _ = _place()
        peers = [(px, py, 1 - pc), (1 - px, py, pc), (px, 1 - py, pc)]

        def exchange(st):
            return _rcopy(acc, rbuf.at[st], ssem.at[st], rsem.at[st], peers[st])

        for k in range(4):
            @pl.when(i == marks[k])
            def _(k=k):
                if k == 0:
                    acc[...] = pk_ref[...]
                else:
                    exchange(k - 1).wait()
                    acc[...] = acc[...] + rbuf[k - 1]
                if k < 3:
                    exchange(k).start()
                else:
                    red_ref[...] = acc[...]

        @pl.when(i == 0)
        def _():
            a_w[...] = jnp.zeros_like(a_w)
            a_g[...] = jnp.zeros_like(a_g)

        dp = dp_ref[...]
        xv = x_ref[...]
        zb = (xv * _rstd(xv) * g1_ref[...]).astype(BF16)
        dz = jnp.zeros((tm, D_MODEL), F32)
        for j in range(N_CHIPS):
            dpj = dp[:, cb * j:cb * (j + 1)]
            dz = dz + _dot_nt(dpj, win_ref[j])
            a_w[j] += _dot_tn(zb, dpj)
        dx, dg = _rms_bwd(xv, g1_ref[...], dz)
        dx_ref[...] = dh1_ref[...] + dx
        a_g[0:1, :] += dg

        @pl.when(i == n - 1)
        def _():
            dwin_ref[...] = a_w[...].astype(BF16)
            dg1_ref[...] = a_g[...]

    row = lambda w: pl.BlockSpec((tm, w), lambda i: (i, 0))
    return pl.pallas_call(
        body, name="bwd_in", grid=(n,),
        in_specs=[row(D_IN_PROJ), row(D_MODEL), row(D_MODEL), _resident(g1.shape), _resident(w_in.shape),
                  _resident(packed.shape)],
        out_specs=[row(D_MODEL), _full(w_in.shape), _full((8, D_MODEL)), _full(packed.shape)],
        out_shape=[jax.ShapeDtypeStruct((s_len, D_MODEL), F32), jax.ShapeDtypeStruct(w_in.shape, BF16),
                   jax.ShapeDtypeStruct((8, D_MODEL), F32), jax.ShapeDtypeStruct(packed.shape, F32)],
        scratch_shapes=[pltpu.VMEM(w_in.shape, F32), pltpu.VMEM((8, D_MODEL), F32), pltpu.VMEM(packed.shape, F32),
                        pltpu.VMEM((3,) + packed.shape, F32), pltpu.SemaphoreType.DMA((3,)), pltpu.SemaphoreType.DMA((3,))],
        compiler_params=_params("arbitrary"),
    )(dproj, x, dh1, g1, w_in, packed)


def _place():
    x, y, c = lax.axis_index("x"), lax.axis_index("y"), lax.axis_index("c")
    chips = [(1 - x, y), (x, 1 - y), (1 - x, 1 - y)]
    return x, y, c, chips


def _rcopy(src, dst, ssem, rsem, dev):
    return pltpu.make_async_remote_copy(src_ref=src, dst_ref=dst, send_sem=ssem, recv_sem=rsem,
                                        device_id=dev, device_id_type=MESH)


ANY = pl.BlockSpec(memory_space=pl.ANY)
COPY_CHUNK_BYTES = 128 * 1024
ROW_ALIGN = 16


def _row_chunks(rows, row_bytes):
    per = max(ROW_ALIGN, (COPY_CHUNK_BYTES // row_bytes) // ROW_ALIGN * ROW_ALIGN)
    return [(r0, min(per, rows - r0)) for r0 in range(0, rows, per)]


def _row_bytes(a):
    return a.shape[-1] * jnp.dtype(a.dtype).itemsize


def _stack_own(shards, dtypes, pos, steps, tag, late=()):
    nw = len(shards)
    nl = len(late)

    def body(pos_ref, *refs):
        outs = refs[nw + nl:2 * nw + nl]
        late_ref = refs[2 * nw + nl:2 * nw + 2 * nl]
        i = pl.program_id(0)
        if nl:
            ssem, rsem = refs[2 * nw + 2 * nl:]

            @pl.when(i == 0)
            def _():
                _gather_send(late_ref, late, ssem, rsem)

        for w in range(nw):
            outs[w][0] = refs[w][...].astype(dtypes[w])

        if nl:
            @pl.when(i == max(steps - PASS_STEPS, 0))
            def _():
                _gather_pass(late_ref, late, ssem, rsem)

            @pl.when(i == steps - 1)
            def _():
                _gather_done(late_ref, late, ssem, rsem)

    def split(s):
        return s.shape[0] % (steps * ROW_ALIGN) == 0

    ins = [pl.BlockSpec((s.shape[0] // steps, s.shape[1]), lambda i, p: (i, 0)) if split(s)
           else pl.BlockSpec(s.shape, lambda i, p: (0, 0)) for s in shards]
    outs = [pl.BlockSpec((1, s.shape[0] // steps, s.shape[1]), lambda i, p: (p[0], i, 0)) if split(s)
            else pl.BlockSpec((1,) + s.shape, lambda i, p: (p[0], 0, 0)) for s in shards]
    res = pl.pallas_call(
        body, name="stack_own_" + tag,
        grid_spec=pltpu.PrefetchScalarGridSpec(
            num_scalar_prefetch=1, grid=(steps,), in_specs=ins + [ANY] * nl, out_specs=outs + [ANY] * nl,
            scratch_shapes=_gather_sems(nl) if nl else []),
        out_shape=[jax.ShapeDtypeStruct((N_CHIPS,) + s.shape, d) for s, d in zip(shards, dtypes)]
        + [jax.ShapeDtypeStruct(a.shape, a.dtype) for a in late],
        input_output_aliases={1 + nw + k: nw + k for k in range(nl)},
        compiler_params=_params("arbitrary"),
    )(pos, *shards, *late)
    return res[:nw], res[nw:]


def _gather_send(outs, stacks, ssem, rsem):
    x, y, c, chips = _place()
    me = 2 * x + y
    for w, st in enumerate(stacks):
        half = st.shape[1] // 2
        for s, (px, py) in enumerate(chips):
            for r0, rs in _row_chunks(half, _row_bytes(st)):
                piece = outs[w].at[me, pl.ds(c * half + r0, rs)]
                _rcopy(piece, piece, ssem.at[w, s], rsem.at[w, s], (px, py, c)).start()


def _gather_pass(outs, stacks, ssem, rsem):
    x, y, c, chips = _place()
    sib = (x, y, 1 - c)
    for w, st in enumerate(stacks):
        half = st.shape[1] // 2
        for s, (px, py) in enumerate(chips):
            blk = outs[w].at[2 * px + py, pl.ds(c * half, half)]
            _rcopy(blk, blk, ssem.at[w, s], rsem.at[w, s], sib).wait_recv()
            for r0, rs in _row_chunks(half, _row_bytes(st)):
                piece = outs[w].at[2 * px + py, pl.ds(c * half + r0, rs)]
                _rcopy(piece, piece, ssem.at[w, 3 + s], rsem.at[w, 3 + s], sib).start()


def _gather_done(outs, stacks, ssem, rsem):
    x, y, c, chips = _place()
    sib = (x, y, 1 - c)
    for w, st in enumerate(stacks):
        half = st.shape[1] // 2
        for s, (px, py) in enumerate(chips):
            blk = outs[w].at[2 * px + py, pl.ds((1 - c) * half, half)]
            _rcopy(blk, blk, ssem.at[w, 3 + s], rsem.at[w, 3 + s], sib).wait_recv()
    for w, st in enumerate(stacks):
        half = st.shape[1] // 2
        blk = outs[w].at[0, pl.ds(0, half)]
        for s in range(6):
            _rcopy(blk, blk, ssem.at[w, s], rsem.at[w, s], sib).wait_send()


def _gather_sems(nw):
    return [pltpu.SemaphoreType.DMA((nw, 6)), pltpu.SemaphoreType.DMA((nw, 6))]


def _swap_halves(grads, tag):
    nw = len(grads)

    def body(*refs):
        ins, got = refs[:nw], refs[nw:2 * nw]
        ssem, rsem = refs[2 * nw:]
        x, y, c, _ = _place()
        cps = []
        for w in range(nw):
            hr = grads[w].shape[1] // 2
            for k in range(N_CHIPS):
                for r0, rs in _row_chunks(hr, _row_bytes(grads[w])):
                    _rcopy(ins[w].at[k, pl.ds((1 - c) * hr + r0, rs)], got[w].at[k, pl.ds(r0, rs)],
                           ssem.at[w], rsem.at[w], (x, y, 1 - c)).start()
            cps.append(_rcopy(got[w], got[w], ssem.at[w], rsem.at[w], (x, y, 1 - c)))
        for cp in cps:
            cp.wait()

    return pl.pallas_call(
        body, name="swap_halves_" + tag,
        in_specs=[ANY] * nw, out_specs=[ANY] * nw,
        out_shape=[jax.ShapeDtypeStruct((g.shape[0], g.shape[1] // 2, g.shape[2]), g.dtype) for g in grads],
        scratch_shapes=[pltpu.SemaphoreType.DMA((nw,)), pltpu.SemaphoreType.DMA((nw,))],
    )(*grads)


def _add_pairs(grads, got, pos, steps, tag):
    nw = len(grads)

    def body(pos_ref, *refs):
        for w in range(nw):
            refs[2 * nw + w][...] = (refs[w][...].astype(F32) + refs[nw + w][...].astype(F32)).astype(BF16)

    blk = lambda a: (a.shape[0], a.shape[1] // steps, a.shape[2])
    own = [pl.BlockSpec(blk(a), lambda i, p: (0, p[1] * steps + i, 0)) for a in got]
    rec = [pl.BlockSpec(blk(a), lambda i, p: (0, i, 0)) for a in got]
    return pl.pallas_call(
        body, name="add_pairs_" + tag,
        grid_spec=pltpu.PrefetchScalarGridSpec(num_scalar_prefetch=1, grid=(steps,), in_specs=own + rec, out_specs=rec),
        out_shape=[jax.ShapeDtypeStruct(a.shape, BF16) for a in got],
        compiler_params=_params("arbitrary"),
    )(pos, *grads, *got)


def _scatter_send(ins, got, parts, ssem, rsem):
    x, y, c, chips = _place()
    for w, p in enumerate(parts):
        for s, (px, py) in enumerate(chips):
            for r0, rs in _row_chunks(p.shape[1], _row_bytes(p)):
                _rcopy(ins[w].at[2 * px + py, pl.ds(r0, rs)], got[w].at[s, pl.ds(r0, rs)],
                       ssem.at[w, s], rsem.at[w, s], (px, py, c)).start()


def _scatter_done(got, parts, ssem, rsem):
    x, y, c, chips = _place()
    for w in range(len(parts)):
        for s, (px, py) in enumerate(chips):
            _rcopy(got[w].at[s], got[w].at[s], ssem.at[w, s], rsem.at[w, s], (px, py, c)).wait()


def _scatter_sems(nw):
    return [pltpu.SemaphoreType.DMA((nw, 3)), pltpu.SemaphoreType.DMA((nw, 3))]


def _scatter_shapes(parts):
    return [jax.ShapeDtypeStruct((3,) + p.shape[1:], p.dtype) for p in parts]


def _scatter_chips(parts):
    nw = len(parts)

    def body(*refs):
        ins, got = refs[:nw], refs[nw:2 * nw]
        ssem, rsem = refs[2 * nw:]
        _scatter_send(ins, got, parts, ssem, rsem)
        _scatter_done(got, parts, ssem, rsem)

    return pl.pallas_call(
        body, name="scatter_chips",
        in_specs=[ANY] * nw, out_specs=[ANY] * nw, out_shape=_scatter_shapes(parts),
        scratch_shapes=_scatter_sems(nw),
    )(*parts)


def _sum_chips(parts, got, pos, steps):
    nw = len(parts)

    def body(pos_ref, *refs):
        for w in range(nw):
            acc = refs[w][0].astype(F32)
            for s in range(3):
                acc = acc + refs[nw + w][s].astype(F32)
            refs[2 * nw + w][...] = acc

    own = [pl.BlockSpec((1, p.shape[1] // steps, p.shape[2]), lambda i, ps: (ps[0], i, 0)) for p in parts]
    rec = [pl.BlockSpec((3, p.shape[1] // steps, p.shape[2]), lambda i, ps: (0, i, 0)) for p in parts]
    outs = [pl.BlockSpec((p.shape[1] // steps, p.shape[2]), lambda i, ps: (ps[1] * steps + i, 0)) for p in parts]
    return pl.pallas_call(
        body, name="sum_chips",
        grid_spec=pltpu.PrefetchScalarGridSpec(num_scalar_prefetch=1, grid=(steps,), in_specs=own + rec, out_specs=outs),
        out_shape=[jax.ShapeDtypeStruct((2 * p.shape[1], p.shape[2]), F32) for p in parts],
        compiler_params=_params("arbitrary"),
    )(pos, *parts, *got)


def _join_halves(shards):
    nw = len(shards)

    def body(*refs):
        outs = refs[nw:2 * nw]
        ssem, rsem = refs[2 * nw:]
        x, y, c, _ = _place()
        cps = []
        for w in range(nw):
            hr = shards[w].shape[0] // 2
            for r0, rs in _row_chunks(hr, _row_bytes(shards[w])):
                piece = outs[w].at[pl.ds(c * hr + r0, rs)]
                _rcopy(piece, piece, ssem.at[w], rsem.at[w], (x, y, 1 - c)).start()
            mine = outs[w].at[pl.ds(c * hr, hr)]
            cps.append(_rcopy(mine, mine, ssem.at[w], rsem.at[w], (x, y, 1 - c)))
        for cp in cps:
            cp.wait()

    return pl.pallas_call(
        body, name="join_halves",
        in_specs=[ANY] * nw, out_specs=[ANY] * nw,
        out_shape=[jax.ShapeDtypeStruct(s.shape, F32) for s in shards],
        input_output_aliases={w: w for w in range(nw)},
        scratch_shapes=[pltpu.SemaphoreType.DMA((nw,)), pltpu.SemaphoreType.DMA((nw,))],
    )(*shards)


def _scatter_and_allreduce(parts, packed):
    nw = len(parts)
    shape = packed.shape

    def body(*refs):
        ins, p_ref = refs[:nw], refs[nw]
        got, out_ref = refs[nw + 1:2 * nw + 1], refs[2 * nw + 1]
        rbuf, ssem, rsem, bsem_s, bsem_r = refs[2 * nw + 2:]
        x, y, c, _ = _place()
        _scatter_send(ins, got, parts, ssem, rsem)
        out_ref[...] = p_ref[...]
        for st, peer in enumerate([(x, y, 1 - c), (1 - x, y, c), (x, 1 - y, c)]):
            cp = _rcopy(out_ref, rbuf.at[st], bsem_s.at[st], bsem_r.at[st], peer)
            cp.start()
            cp.wait()
            out_ref[...] = out_ref[...] + rbuf[st]
        _scatter_done(got, parts, ssem, rsem)

    vm = pl.BlockSpec(memory_space=pltpu.VMEM)
    outs = pl.pallas_call(
        body, name="scatter_and_allreduce",
        in_specs=[ANY] * nw + [vm], out_specs=[ANY] * nw + [vm],
        out_shape=_scatter_shapes(parts) + [jax.ShapeDtypeStruct(shape, F32)],
        scratch_shapes=[pltpu.VMEM((3,) + shape, F32)] + _scatter_sems(nw)
        + [pltpu.SemaphoreType.DMA((3,)), pltpu.SemaphoreType.DMA((3,))],
        compiler_params=pltpu.CompilerParams(vmem_limit_bytes=VMEM_LIMIT),
    )(*parts, packed)
    return outs[:nw], outs[nw]


def _adamw_math(w, g, m, v):
    m = ADAM_B1 * m + (1.0 - ADAM_B1) * g
    v = ADAM_B2 * v + (1.0 - ADAM_B2) * (g * g)
    delta = -ADAM_LR * ((m * ADAM_C1) / (jnp.sqrt(v * ADAM_C2) + ADAM_EPS) + ADAM_WD * w)
    return delta, m, v


def _adamw(ws, gs, ms, vs, steps, name):
    nw = len(ws)

    def body(*refs):
        for k in range(nw):
            g = refs[nw + k][...]
            d, m, v = _adamw_math(refs[k][...], g, refs[2 * nw + k][...], refs[3 * nw + k][...])
            refs[4 * nw + k][...] = d
            refs[5 * nw + k][...] = m
            refs[6 * nw + k][...] = v
            refs[7 * nw + k][...] = g

    specs = [pl.BlockSpec((a.shape[0] // steps, a.shape[1]), lambda i: (i, 0)) for a in ws]
    shapes = [jax.ShapeDtypeStruct(a.shape, F32) for a in ws]
    outs = pl.pallas_call(
        body, name=name, grid=(steps,),
        in_specs=specs * 4, out_specs=specs * 4, out_shape=shapes * 4,
        compiler_params=_params("arbitrary"),
    )(*ws, *gs, *ms, *vs)
    return outs[:nw], outs[nw:2 * nw], outs[2 * nw:3 * nw], outs[3 * nw:]


SMALL = ["norm_mix_g", "pool_w", "pool_b", "pool_scale", "conv_b", "gate_a_w", "gate_a_b", "gate_x_w", "gate_x_b",
         "lru_L", "norm_mlp_g", "norm_ple_g", "b_ple_gate", "norm_final_g"]
BIG = ["w_in", "w_out", "w_up", "w_down", "w_ple_gate", "w_ple_proj"]
ORDER = ["norm_mix_g", "w_in", "pool_w", "pool_b", "pool_scale", "conv_w", "conv_b", "gate_a_w", "gate_a_b", "gate_x_w",
         "gate_x_b", "lru_L", "w_out", "norm_mlp_g", "w_up", "w_down", "norm_ple_g", "w_ple_gate", "b_ple_gate",
         "w_ple_proj", "norm_final_g"]
LANES = 128


def _block_diag(w):
    eye = jnp.eye(LRU_HEADS, dtype=w.dtype)
    return jnp.einsum("hij,hk->hikj", w, eye).reshape(D_LRU, D_LRU)


def _diag_blocks(full):
    f = full.reshape(LRU_HEADS, LRU_BLOCK, LRU_HEADS, LRU_BLOCK)
    return jnp.stack([f[h, :, h, :] for h in range(LRU_HEADS)])


def _rows128(a):
    return a.reshape(-1, LANES)


def _pad8(a):
    r = (-a.shape[0]) % 8
    return jnp.pad(a, ((0, r), (0, 0))) if r else a


def kernel(x, p, norm_mix_g, w_in, pool_w, pool_b, pool_scale, conv_w, conv_b, gate_a_w, gate_a_b, gate_x_w, gate_x_b, lru_L, w_out, norm_mlp_g, w_up, w_down, norm_ple_g, w_ple_gate, b_ple_gate, w_ple_proj, norm_final_g, loss_target, m_norm_mix_g, m_w_in, m_pool_w, m_pool_b, m_pool_scale, m_conv_w, m_conv_b, m_gate_a_w, m_gate_a_b, m_gate_x_w, m_gate_x_b, m_lru_L, m_w_out, m_norm_mlp_g, m_w_up, m_w_down, m_norm_ple_g, m_w_ple_gate, m_b_ple_gate, m_w_ple_proj, m_norm_final_g, v_norm_mix_g, v_w_in, v_pool_w, v_pool_b, v_pool_scale, v_conv_w, v_conv_b, v_gate_a_w, v_gate_a_b, v_gate_x_w, v_gate_x_b, v_lru_L, v_w_out, v_norm_mlp_g, v_w_up, v_w_down, v_norm_ple_g, v_w_ple_gate, v_b_ple_gate, v_w_ple_proj, v_norm_final_g):
    W = dict(norm_mix_g=norm_mix_g, w_in=w_in, pool_w=pool_w, pool_b=pool_b, pool_scale=pool_scale, conv_w=conv_w,
             conv_b=conv_b, gate_a_w=gate_a_w, gate_a_b=gate_a_b, gate_x_w=gate_x_w, gate_x_b=gate_x_b, lru_L=lru_L,
             w_out=w_out, norm_mlp_g=norm_mlp_g, w_up=w_up, w_down=w_down, norm_ple_g=norm_ple_g,
             w_ple_gate=w_ple_gate, b_ple_gate=b_ple_gate, w_ple_proj=w_ple_proj, norm_final_g=norm_final_g)
    M = dict(norm_mix_g=m_norm_mix_g, w_in=m_w_in, pool_w=m_pool_w, pool_b=m_pool_b, pool_scale=m_pool_scale,
             conv_w=m_conv_w, conv_b=m_conv_b, gate_a_w=m_gate_a_w, gate_a_b=m_gate_a_b, gate_x_w=m_gate_x_w,
             gate_x_b=m_gate_x_b, lru_L=m_lru_L, w_out=m_w_out, norm_mlp_g=m_norm_mlp_g, w_up=m_w_up, w_down=m_w_down,
             norm_ple_g=m_norm_ple_g, w_ple_gate=m_w_ple_gate, b_ple_gate=m_b_ple_gate, w_ple_proj=m_w_ple_proj,
             norm_final_g=m_norm_final_g)
    V = dict(norm_mix_g=v_norm_mix_g, w_in=v_w_in, pool_w=v_pool_w, pool_b=v_pool_b, pool_scale=v_pool_scale,
             conv_w=v_conv_w, conv_b=v_conv_b, gate_a_w=v_gate_a_w, gate_a_b=v_gate_a_b, gate_x_w=v_gate_x_w,
             gate_x_b=v_gate_x_b, lru_L=v_lru_L, w_out=v_w_out, norm_mlp_g=v_norm_mlp_g, w_up=v_w_up, w_down=v_w_down,
             norm_ple_g=v_norm_ple_g, w_ple_gate=v_w_ple_gate, b_ple_gate=v_b_ple_gate, w_ple_proj=v_w_ple_proj,
             norm_final_g=v_norm_final_g)

    s_len = x.shape[1]
    sub_mix = min(256, s_len)
    tm = min(512, s_len)
    chip = (2 * lax.axis_index("x") + lax.axis_index("y")).astype(jnp.int32)
    pos = jnp.stack([chip, lax.axis_index("c").astype(jnp.int32)])

    shards = [w_in[0], w_out[0], w_up[0], w_down[0], w_ple_gate[0], w_ple_proj[0], jnp.pad(conv_w[0], ((0, 12), (0, 0)))]
    first, _ = _stack_own([shards[0], shards[1], shards[6]], [BF16, BF16, F32], pos, 8, "first")
    (st_up, st_dn, st_pg, st_pp), (win_g, wout_g, cw_g) = _stack_own(shards[2:6], [BF16] * 4, pos, 8, "rest", first)
    wout_f = wout_g.reshape(D_MODEL, D_MODEL)
    cw_f = jnp.transpose(cw_g[:, :CONV_WIDTH], (1, 0, 2)).reshape(CONV_WIDTH, D_LRU)
    pw_b = pool_w[0].astype(BF16)
    wa_b = _block_diag(gate_a_w[0]).astype(BF16)
    wx_b = _block_diag(gate_x_w[0]).astype(BF16)
    pb_r = pool_b.reshape(1, D_POOL)
    ba_r = gate_a_b.reshape(1, D_LRU)
    bx_r = gate_x_b.reshape(1, D_LRU)
    g4 = norm_final_g.reshape(1, D_MODEL)
    mix_w = (pw_b, pb_r, pool_scale, cw_f, conv_b, wa_b, ba_r, wx_b, bx_r, lru_L, wout_f)

    xs, ps, ts = x[0], p[0, 0], loss_target[0]
    (proj, hst, h1), (wup_g, wdn_g) = _fwd_mix(xs, norm_mix_g, win_g, *mix_w, tm, sub_mix, [st_up, st_dn])
    (z2, ru, h2), (wpg_g, wpp_g) = _fwd_mlp(h1, norm_mlp_g, wup_g, wdn_g, tm, [st_pg, st_pp])
    wpg_f = wpg_g.reshape(D_MODEL, D_MODEL)
    wpp_f = jnp.transpose(wpp_g, (1, 0, 2)).reshape(PLE_DIM, D_MODEL)
    dh2, dh2b, d_wpg, d_wpp, head_vec = _head(h2, ps, ts, norm_ple_g, wpg_f, b_ple_gate, wpp_f, g4, tm)
    dup, dh1, mlp_vec = _bwd_mlp_x(dh2, ru, h1, norm_mlp_g, wup_g, wdn_g, tm)
    d_wup, d_wdn = _bwd_mlp_w(z2, dup, ru, dh2b, tm)
    early = [d_wup, d_wdn, d_wpg.reshape(N_CHIPS, D_MODEL // N_CHIPS, D_MODEL), d_wpp]
    pair_e = _add_pairs(early, _swap_halves(early, "early"), pos, 8, "early")
    (dproj, d_wout, d_pw, d_wa, d_wx, mix_vec), got_e = _bwd_mix(dh1, proj, hst, *mix_w, tm, sub_mix, pair_e)

    g_small = {
        "norm_mix_g": jnp.zeros((1, D_MODEL), F32), "pool_w": d_pw, "pool_b": mix_vec[0:1], "pool_scale": mix_vec[1:2],
        "conv_b": mix_vec[2:3], "gate_a_w": _diag_blocks(d_wa), "gate_a_b": mix_vec[3:4],
        "gate_x_w": _diag_blocks(d_wx), "gate_x_b": mix_vec[4:5], "lru_L": mix_vec[5:6], "norm_mlp_g": mlp_vec[0:1],
        "norm_ple_g": head_vec[0:1], "b_ple_gate": head_vec[2:3], "norm_final_g": head_vec[1:2],
    }
    d_cw = jnp.transpose(mix_vec[8:8 + CONV_WIDTH].reshape(CONV_WIDTH, N_CHIPS, LANES), (1, 0, 2)).reshape(-1, LANES)
    pieces = [_pad8(_rows128(g_small[k])) for k in SMALL] + [d_cw, _pad8(head_vec[3:4, :LANES])]
    offs = [0]
    for pc in pieces:
        offs.append(offs[-1] + pc.shape[0])
    dx, d_win, in_vec, red = _bwd_in(dproj, xs, dh1, norm_mix_g, win_g, tm, jnp.concatenate(pieces, axis=0))

    last = [d_win, d_wout.reshape(N_CHIPS, D_MODEL // N_CHIPS, D_MODEL)]
    pair_l = _add_pairs(last, _swap_halves(last, "last"), pos, 8, "last")
    got_l, red_in = _scatter_and_allreduce(pair_l, in_vec)
    g_big = _join_halves(_sum_chips(pair_l + pair_e, got_l + got_e, pos, 8))
    loss = red[offs[-2], 0]
    g_cw = lax.dynamic_slice(red, (offs[len(SMALL)] + CONV_WIDTH * chip, 0), (CONV_WIDTH, LANES))

    def packed(src):
        return jnp.concatenate([_pad8(_rows128(src[k])) for k in SMALL] + [_pad8(src["conv_w"][0])], axis=0)

    n_small = offs[len(SMALL)]
    g_pack = jnp.concatenate([_rows128(red_in[0:1]), red[offs[1]:n_small], _pad8(g_cw)], axis=0)
    (d_pack,), (m_pack,), (v_pack,), _ = _adamw([packed(W)], [g_pack], [packed(M)], [packed(V)], 1, "adamw_small")

    big2d = lambda src: [src[k][0] for k in BIG]
    d_big, m_big, v_big, g_big = _adamw(big2d(W), g_big, big2d(M), big2d(V), 8, "adamw_big")

    def unpack(pack, big_list):
        out = {}
        for idx, k in enumerate(SMALL):
            n_el = W[k].size
            out[k] = pack[offs[idx]:offs[idx + 1]].reshape(-1)[:n_el].reshape(W[k].shape)
        out["conv_w"] = pack[n_small:n_small + CONV_WIDTH].reshape(W["conv_w"].shape)
        for k, a in zip(BIG, big_list):
            out[k] = a.reshape(W[k].shape)
        return out

    grads = unpack(g_pack, g_big)
    deltas = unpack(d_pack, d_big)
    new_m = unpack(m_pack, m_big)
    new_v = unpack(v_pack, v_big)
    return (loss, dx[None], *[grads[k] for k in ORDER], *[deltas[k] for k in ORDER],
            *[new_m[k] for k in ORDER], *[new_v[k] for k in ORDER])
```

```python
import functools

import jax
import jax.numpy as jnp
from jax import lax
from jax.experimental import pallas as pl
from jax.experimental.pallas import tpu as pltpu

F32 = jnp.float32
BF16 = jnp.bfloat16
MESH = pl.DeviceIdType.MESH

D_MODEL = 1024
D_POOL = 512
D_LRU = 512
POOL_WINDOWS = (2, 4, 8, 16)
POOL_GROUP = 128
POOL_HALO = 16
CONV_WIDTH = 4
CONV_HALO = 8
PASS_STEPS = 2
LRU_HEADS = 8
LRU_BLOCK = 64
LRU_C = 8.0
D_FF = 4096
PLE_DIM = 256
D_IN_PROJ = 1536
RMS_EPS = 1e-6
N_CHIPS = 4
FF_BLOCK = D_FF // N_CHIPS
FF_PAIR = 2

ADAM_LR = 0.001
ADAM_B1 = 0.9
ADAM_B2 = 0.999
ADAM_EPS = 1e-08
ADAM_WD = 0.01
ADAM_STEP = 10
ADAM_C1 = 1.0 / (1.0 - ADAM_B1 ** ADAM_STEP)
ADAM_C2 = 1.0 / (1.0 - ADAM_B2 ** ADAM_STEP)

VMEM_LIMIT = 56 * 1024 * 1024
GELU_C = 0.7978845608028654
GELU_A = 0.044715

NT = (((1,), (1,)), ((), ()))
TN = (((0,), (0,)), ((), ()))


def _dot(a, b):
    return jnp.dot(a, b, preferred_element_type=F32)


def _dot_nt(a, b):
    return lax.dot_general(a, b, NT, preferred_element_type=F32)


def _dot_tn(a, b):
    return lax.dot_general(a, b, TN, preferred_element_type=F32)


def _params(*sem):
    return pltpu.CompilerParams(dimension_semantics=sem, vmem_limit_bytes=VMEM_LIMIT)


def _full(shape):
    nd = len(shape)
    return pl.BlockSpec(shape, lambda *_: (0,) * nd)


def _resident(shape):
    nd = len(shape)
    return pl.BlockSpec(shape, lambda *_: (0,) * nd, pipeline_mode=pl.Buffered(1))


def _rstd(x):
    return lax.rsqrt(jnp.mean(x * x, axis=-1, keepdims=True) + RMS_EPS)


def _rms_bwd(x, g, dz):
    xr = x * _rstd(x)
    r = _rstd(x)
    dyg = dz * g
    dx = r * (dyg - xr * jnp.mean(dyg * xr, axis=-1, keepdims=True))
    dg = jnp.sum(dz * xr, axis=0, keepdims=True)
    return dx, dg


def _sigmoid(x):
    return 1.0 / (1.0 + jnp.exp(-x))


def _log_sigmoid(v):
    u = jnp.exp(-jnp.abs(v))
    w = 1.0 + u
    l1p = jnp.where(w == 1.0, u, jnp.log(w) * u / jnp.where(w == 1.0, 1.0, w - 1.0))
    return jnp.minimum(v, 0.0) - l1p


def _gelu(x):
    t = jnp.tanh(GELU_C * (x + GELU_A * x * x * x))
    return 0.5 * x * (1.0 + t), t


def _gelu_grad(x, t):
    return 0.5 * (1.0 + t) + 0.5 * x * (1.0 - t * t) * GELU_C * (1.0 + 3.0 * GELU_A * x * x)


def _rows(shape, t0):
    return lax.broadcasted_iota(jnp.int32, shape, 0) + t0


def _pool_diff(u_pool, prev, t0):
    tm = u_pool.shape[0]
    rows = _rows((tm, POOL_GROUP), t0)
    outs, invs = [], []
    for g, w in enumerate(POOL_WINDOWS):
        sl = slice(POOL_GROUP * g, POOL_GROUP * (g + 1))
        ug = u_pool[:, sl]
        s = jnp.concatenate([prev[:, sl], ug], axis=0)
        k = 1
        while k < w:
            s = s + pltpu.roll(s, k, 0)
            k *= 2
        inv = 1.0 / jnp.minimum(rows + 1, w).astype(F32)
        outs.append(s[POOL_HALO:] * inv - ug)
        invs.append(inv)
    return jnp.concatenate(outs, axis=1), jnp.concatenate(invs, axis=1)


def _pool_diff_bwd(dd, inv, nxt):
    ddc = dd * inv
    outs = []
    for g, w in enumerate(POOL_WINDOWS):
        sl = slice(POOL_GROUP * g, POOL_GROUP * (g + 1))
        s = jnp.concatenate([ddc[:, sl], nxt[:, sl]], axis=0)
        n = s.shape[0]
        k = 1
        while k < w:
            s = s + pltpu.roll(s, n - k, 0)
            k *= 2
        outs.append(s[:n - POOL_HALO] - dd[:, sl])
    return jnp.concatenate(outs, axis=1), ddc


def _conv_taps(u, prev):
    ext = jnp.concatenate([prev, u], axis=0)
    return [pltpu.roll(ext, CONV_WIDTH - 1 - k, 0)[CONV_HALO:] if k < CONV_WIDTH - 1 else u for k in range(CONV_WIDTH)]


def _scan_fwd(a, b):
    tm = a.shape[0]
    rows = _rows(a.shape, 0)
    k = 1
    while k < tm:
        ar = pltpu.roll(a, k, 0)
        br = pltpu.roll(b, k, 0)
        m = rows >= k
        b = jnp.where(m, a * br + b, b)
        a = jnp.where(m, a * ar, a)
        k *= 2
    return a, b


def _scan_rev(a, b):
    tm = a.shape[0]
    rows = _rows(a.shape, 0)
    k = 1
    while k < tm:
        ar = pltpu.roll(a, tm - k, 0)
        br = pltpu.roll(b, tm - k, 0)
        m = rows < tm - k
        b = jnp.where(m, a * br + b, b)
        a = jnp.where(m, a * ar, a)
        k *= 2
    return a, b


def _lru_gates(xb, wa, ba, wx, bx, lsl8, t0):
    xbb = xb.astype(BF16)
    r = _sigmoid(_dot(xbb, wa) + ba)
    ig = _sigmoid(_dot(xbb, wx) + bx)
    a = jnp.exp(r * lsl8)
    first = _rows(xb.shape, t0) == 0
    mult = jnp.where(first, 1.0, jnp.sqrt(1.0 - a * a))
    return r, ig, a, mult, first


def _fwd_mix(x, g1, w_in, pool_w, pool_b, pool_scale, conv_w, conv_b, wa, ba, wx, bx, lru_l, w_out, tm, sub, late):
    s_len = x.shape[0]
    n = s_len // tm
    nl = len(late)

    def body(x_ref, g1_ref, win_ref, pw_ref, pb_ref, ps_ref, cw_ref, cb_ref, wa_ref, ba_ref, wx_ref, bx_ref, l_ref,
             wout_ref, *rest):
        proj_ref, h_ref, h1_ref = rest[nl:nl + 3]
        late_ref = rest[nl + 3:2 * nl + 3]
        cpool, clru, ch, ssem, rsem = rest[2 * nl + 3:]
        i = pl.program_id(0)

        @pl.when(i == 0)
        def _():
            _gather_send(late_ref, late, ssem, rsem)
            cpool[...] = jnp.zeros_like(cpool)
            clru[...] = jnp.zeros_like(clru)
            ch[...] = jnp.zeros_like(ch)

        lsl8 = LRU_C * _log_sigmoid(l_ref[...])
        cp, cl, hc = cpool[...], clru[...], ch[7:8, :]
        def in_proj(k):
            rs = slice(k * sub, (k + 1) * sub)
            xv = x_ref[rs, :]
            zb = (xv * _rstd(xv) * g1_ref[...]).astype(BF16)
            proj = jnp.concatenate([_dot(zb, win_ref[j]) for j in range(N_CHIPS)], axis=1)
            proj_ref[rs, :] = proj.astype(BF16)
            return xv, proj

        nxt = in_proj(0)
        for k in range(tm // sub):
            rs = slice(k * sub, (k + 1) * sub)
            t0 = i * tm + k * sub
            xv, proj = nxt
            if k + 1 < tm // sub:
                nxt = in_proj(k + 1)
            u_pool = proj[:, :D_POOL]
            u_lru = proj[:, D_POOL:D_POOL + D_LRU]
            u_gate = proj[:, D_POOL + D_LRU:]

            d, _ = _pool_diff(u_pool, cp, t0)
            cp = u_pool[sub - POOL_HALO:]
            db = d.astype(BF16)
            yp = jnp.concatenate(
                [_dot(db[:, POOL_GROUP * g:POOL_GROUP * (g + 1)], pw_ref[g]) for g in range(len(POOL_WINDOWS))], axis=1)
            y_pool = (yp + pb_ref[...]) * ps_ref[...]

            taps = _conv_taps(u_lru, cl)
            cl = u_lru[sub - CONV_HALO:]
            xb = cb_ref[...]
            for q in range(CONV_WIDTH):
                xb = xb + taps[q] * cw_ref[q:q + 1, :]
            _, ig, a, mult, _ = _lru_gates(xb, wa_ref[...], ba_ref[...], wx_ref[...], bx_ref[...], lsl8, t0)
            pa, hb = _scan_fwd(a, mult * (ig * xb))
            h = hb + pa * hc
            hc = h[sub - 1:sub, :]
            h_ref[rs, :] = h
            gl, _ = _gelu(u_gate)
            cat = jnp.concatenate([y_pool, h * gl], axis=1).astype(BF16)
            h1_ref[rs, :] = xv + _dot(cat, wout_ref[...])
        cpool[...] = cp
        clru[...] = cl
        ch[...] = jnp.broadcast_to(hc, ch.shape)

        @pl.when(i == max(n - PASS_STEPS, 0))
        def _():
            _gather_pass(late_ref, late, ssem, rsem)

        @pl.when(i == n - 1)
        def _():
            _gather_done(late_ref, late, ssem, rsem)

    row = lambda w: pl.BlockSpec((tm, w), lambda i: (i, 0))
    ins = [x, g1, w_in, pool_w, pool_b, pool_scale, conv_w, conv_b, wa, ba, wx, bx, lru_l, w_out]
    outs = pl.pallas_call(
        body, name="fwd_mix", grid=(n,),
        in_specs=[row(D_MODEL)] + [_resident(a.shape) for a in ins[1:]] + [ANY] * nl,
        out_specs=[row(D_IN_PROJ), row(D_LRU), row(D_MODEL)] + [ANY] * nl,
        out_shape=[jax.ShapeDtypeStruct((s_len, D_IN_PROJ), BF16), jax.ShapeDtypeStruct((s_len, D_LRU), F32),
                   jax.ShapeDtypeStruct((s_len, D_MODEL), F32)]
        + [jax.ShapeDtypeStruct(a.shape, a.dtype) for a in late],
        input_output_aliases={len(ins) + k: 3 + k for k in range(nl)},
        scratch_shapes=[pltpu.VMEM((POOL_HALO, D_POOL), F32), pltpu.VMEM((CONV_HALO, D_LRU), F32),
                        pltpu.VMEM((8, D_LRU), F32)] + _gather_sems(nl),
        compiler_params=_params("arbitrary"),
    )(*ins, *late)
    return outs[:3], outs[3:]


def _fwd_mlp(h1, g2, w_up, w_down, tm, late):
    s_len = h1.shape[0]
    n = s_len // tm
    nl = len(late)

    def body(h1_ref, g2_ref, wup_ref, wdn_ref, *rest):
        z2_ref, ru_ref, h2_ref = rest[nl:nl + 3]
        late_ref = rest[nl + 3:2 * nl + 3]
        ssem, rsem = rest[2 * nl + 3:]
        i = pl.program_id(0)

        @pl.when(i == 0)
        def _():
            _gather_send(late_ref, late, ssem, rsem)

        hv = h1_ref[...]
        zb = (hv * _rstd(hv) * g2_ref[...]).astype(BF16)
        z2_ref[...] = zb
        acc = hv
        for j in range(N_CHIPS):
            ru = jnp.maximum(_dot(zb, wup_ref[j]), 0.0)
            ru_ref[:, FF_BLOCK * j:FF_BLOCK * (j + 1)] = ru.astype(BF16)
            acc = acc + _dot((ru * ru).astype(BF16), wdn_ref[j])
        h2_ref[...] = acc

        @pl.when(i == max(n - PASS_STEPS, 0))
        def _():
            _gather_pass(late_ref, late, ssem, rsem)

        @pl.when(i == n - 1)
        def _():
            _gather_done(late_ref, late, ssem, rsem)

    row = lambda w: pl.BlockSpec((tm, w), lambda i: (i, 0))
    outs = pl.pallas_call(
        body, name="fwd_mlp", grid=(n,),
        in_specs=[row(D_MODEL), _full(g2.shape), _resident(w_up.shape), _resident(w_down.shape)] + [ANY] * nl,
        out_specs=[row(D_MODEL), row(D_FF), row(D_MODEL)] + [ANY] * nl,
        out_shape=[jax.ShapeDtypeStruct((s_len, D_MODEL), BF16), jax.ShapeDtypeStruct((s_len, D_FF), BF16),
                   jax.ShapeDtypeStruct((s_len, D_MODEL), F32)] + [jax.ShapeDtypeStruct(a.shape, a.dtype) for a in late],
        input_output_aliases={4 + k: 3 + k for k in range(nl)},
        scratch_shapes=_gather_sems(nl),
        compiler_params=_params("arbitrary"),
    )(h1, g2, w_up, w_down, *late)
    return outs[:3], outs[3:]


def _head(h2, p, target, g3, w_pg, b_pg, w_pp, g4, tm):
    s_len = h2.shape[0]
    n = s_len // tm

    def body(h2_ref, p_ref, t_ref, g3_ref, wpg_ref, bpg_ref, wpp_ref, g4_ref,
             dh2_ref, dh2b_ref, dwpg_ref, dwpp_ref, vec_ref, a_pg, a_pp, a_vec):
        i = pl.program_id(0)

        @pl.when(i == 0)
        def _():
            a_pg[...] = jnp.zeros_like(a_pg)
            a_pp[...] = jnp.zeros_like(a_pp)
            a_vec[...] = jnp.zeros_like(a_vec)

        h2v = h2_ref[...]
        g3v = g3_ref[...]
        g4v = g4_ref[...]
        z3 = (h2v * _rstd(h2v) * g3v).astype(BF16)
        gate = _sigmoid(_dot(z3, wpg_ref[...]) + bpg_ref[...])
        pb = p_ref[...].astype(BF16)
        pp = _dot(pb, wpp_ref[...])
        h3 = h2v + gate * pp
        r4 = _rstd(h3)
        diff = h3 * r4 * g4v - t_ref[...]
        loss = 0.5 * jnp.sum(jnp.mean(diff * diff, axis=-1, keepdims=True), axis=0, keepdims=True)
        dy = diff * (1.0 / D_MODEL)
        dh3, dg4 = _rms_bwd(h3, g4v, dy)
        dpp = (dh3 * gate).astype(BF16)
        dpre = dh3 * pp * gate * (1.0 - gate)
        dpreb = dpre.astype(BF16)
        dz3 = _dot_nt(dpreb, wpg_ref[...])
        dx, dg3 = _rms_bwd(h2v, g3v, dz3)
        dh2 = dh3 + dx
        dh2_ref[...] = dh2
        dh2b_ref[...] = dh2.astype(BF16)
        a_pg[...] += _dot_tn(z3, dpreb)
        a_pp[...] += _dot_tn(pb, dpp)
        a_vec[0:1, :] += dg3
        a_vec[1:2, :] += dg4
        a_vec[2:3, :] += jnp.sum(dpre, axis=0, keepdims=True)
        a_vec[3:4, :] += jnp.broadcast_to(loss, (1, D_MODEL))

        @pl.when(i == n - 1)
        def _():
            dwpg_ref[...] = a_pg[...].astype(BF16)
            for j in range(N_CHIPS):
                dwpp_ref[j] = a_pp[:, PLE_DIM * j:PLE_DIM * (j + 1)].astype(BF16)
            vec_ref[...] = a_vec[...]

    row = lambda w: pl.BlockSpec((tm, w), lambda i: (i, 0))
    ins = [h2, p, target, g3, w_pg, b_pg, w_pp, g4]
    return pl.pallas_call(
        body, name="head", grid=(n,),
        in_specs=[row(D_MODEL), row(PLE_DIM), row(D_MODEL)] + [_resident(a.shape) for a in ins[3:]],
        out_specs=[row(D_MODEL), row(D_MODEL), _full((D_MODEL, D_MODEL)), _full((N_CHIPS, PLE_DIM, PLE_DIM)),
                   _full((8, D_MODEL))],
        out_shape=[jax.ShapeDtypeStruct((s_len, D_MODEL), F32), jax.ShapeDtypeStruct((s_len, D_MODEL), BF16),
                   jax.ShapeDtypeStruct((D_MODEL, D_MODEL), BF16),
                   jax.ShapeDtypeStruct((N_CHIPS, PLE_DIM, PLE_DIM), BF16), jax.ShapeDtypeStruct((8, D_MODEL), F32)],
        scratch_shapes=[pltpu.VMEM((D_MODEL, D_MODEL), F32), pltpu.VMEM((PLE_DIM, D_MODEL), F32),
                        pltpu.VMEM((8, D_MODEL), F32)],
        compiler_params=_params("arbitrary"),
    )(*ins)


def _bwd_mlp_x(dh2, ru, h1, g2, w_up, w_down, tm):
    s_len = dh2.shape[0]
    n = s_len // tm

    def body(dh2_ref, ru_ref, h1_ref, g2_ref, wup_ref, wdn_ref, dup_ref, dh1_ref, dg2_ref, a_g):
        i = pl.program_id(0)

        @pl.when(i == 0)
        def _():
            a_g[...] = jnp.zeros_like(a_g)

        dh2v = dh2_ref[...]
        dhb = dh2v.astype(BF16)
        acc = jnp.zeros((tm, D_MODEL), F32)
        for j in range(N_CHIPS):
            sl = slice(FF_BLOCK * j, FF_BLOCK * (j + 1))
            dup = (_dot_nt(dhb, wdn_ref[j]) * (2.0 * ru_ref[:, sl].astype(F32))).astype(BF16)
            dup_ref[:, sl] = dup
            acc = acc + _dot_nt(dup, wup_ref[j])
        dx, dg = _rms_bwd(h1_ref[...], g2_ref[...], acc)
        dh1_ref[...] = dh2v + dx
        a_g[0:1, :] += dg

        @pl.when(i == n - 1)
        def _():
            dg2_ref[...] = a_g[...]

    row = lambda w: pl.BlockSpec((tm, w), lambda i: (i, 0))
    return pl.pallas_call(
        body, name="bwd_mlp_x", grid=(n,),
        in_specs=[row(D_MODEL), row(D_FF), row(D_MODEL), _full(g2.shape), _resident(w_up.shape), _resident(w_down.shape)],
        out_specs=[row(D_FF), row(D_MODEL), _full((8, D_MODEL))],
        out_shape=[jax.ShapeDtypeStruct((s_len, D_FF), BF16), jax.ShapeDtypeStruct((s_len, D_MODEL), F32),
                   jax.ShapeDtypeStruct((8, D_MODEL), F32)],
        scratch_shapes=[pltpu.VMEM((8, D_MODEL), F32)],
        compiler_params=_params("arbitrary"),
    )(dh2, ru, h1, g2, w_up, w_down)


def _bwd_mlp_w(z2, dup, ru, dh2, tk):
    s_len = z2.shape[0]
    n = s_len // tk

    def body(z2_ref, dup_ref, ru_ref, dh2_ref, dwup_ref, dwdn_ref, a_up, a_dn):
        t = pl.program_id(1)

        @pl.when(t == 0)
        def _():
            a_up[...] = jnp.zeros_like(a_up)
            a_dn[...] = jnp.zeros_like(a_dn)

        for b in range(FF_PAIR):
            sl = slice(FF_BLOCK * b, FF_BLOCK * (b + 1))
            ruv = ru_ref[:, sl]
            a_up[b] += _dot_tn(z2_ref[...], dup_ref[:, sl])
            a_dn[b] += _dot_tn(ruv * ruv, dh2_ref[...])

        @pl.when(t == n - 1)
        def _():
            dwup_ref[...] = a_up[...].astype(BF16)
            dwdn_ref[...] = a_dn[...].astype(BF16)

    tile = pl.BlockSpec((tk, D_MODEL), lambda j, t: (t, 0))
    ffb = pl.BlockSpec((tk, FF_PAIR * FF_BLOCK), lambda j, t: (t, j))
    return pl.pallas_call(
        body, name="bwd_mlp_w", grid=(N_CHIPS // FF_PAIR, n),
        in_specs=[tile, ffb, ffb, tile],
        out_specs=[pl.BlockSpec((FF_PAIR, D_MODEL, FF_BLOCK), lambda j, t: (j, 0, 0)),
                   pl.BlockSpec((FF_PAIR, FF_BLOCK, D_MODEL), lambda j, t: (j, 0, 0))],
        out_shape=[jax.ShapeDtypeStruct((N_CHIPS, D_MODEL, FF_BLOCK), BF16),
                   jax.ShapeDtypeStruct((N_CHIPS, FF_BLOCK, D_MODEL), BF16)],
        scratch_shapes=[pltpu.VMEM((FF_PAIR, D_MODEL, FF_BLOCK), F32), pltpu.VMEM((FF_PAIR, FF_BLOCK, D_MODEL), F32)],
        compiler_params=_params("arbitrary", "arbitrary"),
    )(z2, dup, ru, dh2)


MIX_VEC_ROWS = 16


def _bwd_mix(dh1, proj, h, pool_w, pool_b, pool_scale, conv_w, conv_b, wa, ba, wx, bx, lru_l, w_out, tm, sub, early):
    s_len = dh1.shape[0]
    n = s_len // tm
    ng = len(POOL_WINDOWS)
    ne_ = len(early)

    def body(dh1_ref, proj_ref, h_ref, projh_ref, hh_ref, pw_ref, pb_ref, ps_ref, cw_ref, cb_ref,
             wa_ref, ba_ref, wx_ref, bx_ref, l_ref, wout_ref, *rest):
        early_ref = rest[:ne_]
        dproj_ref, dwout_ref, dpw_ref, dwa_ref, dwx_ref, vec_ref = rest[ne_:ne_ + 6]
        got_ref = rest[ne_ + 6:2 * ne_ + 6]
        a_out, a_pw, a_wa, a_wx, a_vec, c_g, c_dxb, c_ddc, ssem, rsem = rest[2 * ne_ + 6:]
        q = pl.program_id(0)
        i = n - 1 - q

        @pl.when(q == 0)
        def _():
            _scatter_send(early_ref, got_ref, early, ssem, rsem)
            for r in (a_out, a_pw, a_wa, a_wx, a_vec, c_g, c_dxb, c_ddc):
                r[...] = jnp.zeros_like(r)

        has_prev = (i > 0).astype(F32)
        lv = l_ref[...]
        lsl8 = LRU_C * _log_sigmoid(lv)
        cg, cdxb, cddc = c_g[0:1, :], c_dxb[...], c_ddc[...]
        vec, cats, dhbs = {}, [], []

        def add(row, v):
            vec[row] = v if row not in vec else vec[row] + v

        for k in reversed(range(tm // sub)):
            rs = slice(k * sub, (k + 1) * sub)
            t0 = i * tm + k * sub
            dh1b = dh1_ref[rs, :].astype(BF16)
            dcat = _dot_nt(dh1b, wout_ref[...])
            dy_pool = dcat[:, :D_POOL]
            dy_lru = dcat[:, D_POOL:]

            proj = proj_ref[rs, :].astype(F32)
            u_pool = proj[:, :D_POOL]
            u_lru = proj[:, D_POOL:D_POOL + D_LRU]
            u_gate = proj[:, D_POOL + D_LRU:]
            if k > 0:
                halo = proj_ref[k * sub - POOL_HALO:k * sub, :].astype(F32)
                h_prev_row = h_ref[k * sub - 1:k * sub, :]
            else:
                halo = projh_ref[...].astype(F32) * has_prev
                h_prev_row = hh_ref[7:8, :] * has_prev

            d, inv = _pool_diff(u_pool, halo[:, :D_POOL], t0)
            db = d.astype(BF16)
            ypre = jnp.concatenate(
                [_dot(db[:, POOL_GROUP * g:POOL_GROUP * (g + 1)], pw_ref[g]) for g in range(ng)], axis=1) + pb_ref[...]
            dyp = dy_pool * ps_ref[...]
            dypb = dyp.astype(BF16)
            dds = []
            for g in range(ng):
                sl = slice(POOL_GROUP * g, POOL_GROUP * (g + 1))
                a_pw[g] += _dot_tn(db[:, sl], dypb[:, sl])
                dds.append(_dot_nt(dypb[:, sl], pw_ref[g]))
            du_pool, ddc = _pool_diff_bwd(jnp.concatenate(dds, axis=1), inv, cddc)
            cddc = ddc[:POOL_HALO]
            add(0, jnp.sum(dyp, axis=0, keepdims=True))
            add(1, jnp.sum(dy_pool * ypre, axis=0, keepdims=True))

            taps = _conv_taps(u_lru, halo[POOL_HALO - CONV_HALO:, D_POOL:D_POOL + D_LRU])
            xb = cb_ref[...]
            for c in range(CONV_WIDTH):
                xb = xb + taps[c] * cw_ref[c:c + 1, :]
            r, ig, a, mult, first = _lru_gates(xb, wa_ref[...], ba_ref[...], wx_ref[...], bx_ref[...], lsl8, t0)
            hv = h_ref[rs, :]
            gl, th = _gelu(u_gate)
            du_gate = dy_lru * hv * _gelu_grad(u_gate, th)
            cats.insert(0, jnp.concatenate([ypre * ps_ref[...], hv * gl], axis=1).astype(BF16))
            dhbs.insert(0, dh1b)
            last = _rows(a.shape, 0) == sub - 1
            a_next = jnp.where(last, 1.0, pltpu.roll(a, sub - 1, 0))
            pa, gb = _scan_rev(a_next, dy_lru * gl)
            gh = gb + pa * cg
            cg = a[0:1, :] * gh[0:1, :]
            h_prev = jnp.where(_rows(hv.shape, 0) == 0, h_prev_row, pltpu.roll(hv, 1, 0))
            gix = gh * ig * xb
            dla = gh * h_prev * a - jnp.where(first, 0.0, gix * a * a / mult)
            dpre_r = dla * lsl8 * r * (1.0 - r)
            dpre_i = gh * mult * xb * ig * (1.0 - ig)
            dprb = dpre_r.astype(BF16)
            dpib = dpre_i.astype(BF16)
            xbb = xb.astype(BF16)
            a_wa[...] += _dot_tn(xbb, dprb)
            a_wx[...] += _dot_tn(xbb, dpib)
            dxb = gh * mult * ig + _dot_nt(dprb, wa_ref[...]) + _dot_nt(dpib, wx_ref[...])
            add(2, jnp.sum(dxb, axis=0, keepdims=True))
            add(3, jnp.sum(dpre_r, axis=0, keepdims=True))
            add(4, jnp.sum(dpre_i, axis=0, keepdims=True))
            add(5, jnp.sum(dla * r, axis=0, keepdims=True))
            ext = jnp.concatenate([dxb, cdxb], axis=0)
            cdxb = dxb[:CONV_HALO]
            ne = sub + CONV_HALO
            du_lru = dxb * cw_ref[CONV_WIDTH - 1:CONV_WIDTH, :]
            for c in range(CONV_WIDTH):
                add(8 + c, jnp.sum(dxb * taps[c], axis=0, keepdims=True))
                if c < CONV_WIDTH - 1:
                    du_lru = du_lru + pltpu.roll(ext, ne - (CONV_WIDTH - 1 - c), 0)[:sub] * cw_ref[c:c + 1, :]
            dproj_ref[rs, :] = jnp.concatenate([du_pool, du_lru, du_gate], axis=1).astype(BF16)
        a_out[...] += _dot_tn(jnp.concatenate(cats, axis=0), jnp.concatenate(dhbs, axis=0))
        c_g[...] = jnp.broadcast_to(cg, c_g.shape)
        c_dxb[...] = cdxb
        c_ddc[...] = cddc
        for row, v in vec.items():
            a_vec[row:row + 1, :] += v

        @pl.when(q == n - 1)
        def _():
            dwout_ref[...] = a_out[...].astype(BF16)
            dpw_ref[...] = a_pw[...]
            dwa_ref[...] = a_wa[...]
            dwx_ref[...] = a_wx[...]
            vec_ref[...] = a_vec[...]
            vec_ref[5:6, :] = a_vec[5:6, :] * (LRU_C * _sigmoid(-lv))
            _scatter_done(got_ref, early, ssem, rsem)

    rev =lambda w: pl.BlockSpec((tm, w), lambda q: (n - 1 - q, 0))
    halo_p = pl.BlockSpec((POOL_HALO, D_IN_PROJ), lambda q: (jnp.maximum((n - 1 - q) * (tm // POOL_HALO) - 1, 0), 0))
    halo_h = pl.BlockSpec((8, D_LRU), lambda q: (jnp.maximum((n - 1 - q) * (tm // 8) - 1, 0), 0))
    wts = [pool_w, pool_b, pool_scale, conv_w, conv_b, wa, ba, wx, bx, lru_l, w_out]
    outs = pl.pallas_call(
        body, name="bwd_mix", grid=(n,),
        in_specs=[rev(D_MODEL), rev(D_IN_PROJ), rev(D_LRU), halo_p, halo_h] + [_resident(a.shape) for a in wts]
        + [ANY] * ne_,
        out_specs=[rev(D_IN_PROJ), _full((D_MODEL, D_MODEL)), _full((ng, POOL_GROUP, POOL_GROUP)),
                   _full((D_LRU, D_LRU)), _full((D_LRU, D_LRU)), _full((MIX_VEC_ROWS, D_LRU))] + [ANY] * ne_,
        out_shape=[jax.ShapeDtypeStruct((s_len, D_IN_PROJ), BF16), jax.ShapeDtypeStruct((D_MODEL, D_MODEL), BF16),
                   jax.ShapeDtypeStruct((ng, POOL_GROUP, POOL_GROUP), F32), jax.ShapeDtypeStruct((D_LRU, D_LRU), F32),
                   jax.ShapeDtypeStruct((D_LRU, D_LRU), F32), jax.ShapeDtypeStruct((MIX_VEC_ROWS, D_LRU), F32)]
        + _scatter_shapes(early),
        scratch_shapes=[pltpu.VMEM((D_MODEL, D_MODEL), F32), pltpu.VMEM((ng, POOL_GROUP, POOL_GROUP), F32),
                        pltpu.VMEM((D_LRU, D_LRU), F32), pltpu.VMEM((D_LRU, D_LRU), F32),
                        pltpu.VMEM((MIX_VEC_ROWS, D_LRU), F32), pltpu.VMEM((8, D_LRU), F32),
                        pltpu.VMEM((CONV_HALO, D_LRU), F32), pltpu.VMEM((POOL_HALO, D_POOL), F32)] + _scatter_sems(ne_),
        compiler_params=_params("arbitrary"),
    )(dh1, proj, h, proj, h, *wts, *early)
    return outs[:6], outs[6:]


def _bwd_in(dproj, x, dh1, g1, w_in, tm):
    s_len = x.shape[0]
    n = s_len // tm
    cb = D_IN_PROJ // N_CHIPS

    def body(dp_ref, x_ref, dh1_ref, g1_ref, win_ref, dx_ref, dwin_ref, dg1_ref, a_w, a_g):
        i = pl.program_id(0)

        @pl.when(i == 0)
        def _():
            a_w[...] = jnp.zeros_like(a_w)
            a_g[...] = jnp.zeros_like(a_g)

        dp = dp_ref[...]
        xv = x_ref[...]
        zb = (xv * _rstd(xv) * g1_ref[...]).astype(BF16)
        dz = jnp.zeros((tm, D_MODEL), F32)
        for j in range(N_CHIPS):
            dpj = dp[:, cb * j:cb * (j + 1)]
            dz = dz + _dot_nt(dpj, win_ref[j])
            a_w[j] += _dot_tn(zb, dpj)
        dx, dg = _rms_bwd(xv, g1_ref[...], dz)
        dx_ref[...] = dh1_ref[...] + dx
        a_g[0:1, :] += dg

        @pl.when(i == n - 1)
        def _():
            dwin_ref[...] = a_w[...].astype(BF16)
            dg1_ref[...] = a_g[...]

    row = lambda w: pl.BlockSpec((tm, w), lambda i: (i, 0))
    return pl.pallas_call(
        body, name="bwd_in", grid=(n,),
        in_specs=[row(D_IN_PROJ), row(D_MODEL), row(D_MODEL), _resident(g1.shape), _resident(w_in.shape)],
        out_specs=[row(D_MODEL), _full(w_in.shape), _full((8, D_MODEL))],
        out_shape=[jax.ShapeDtypeStruct((s_len, D_MODEL), F32), jax.ShapeDtypeStruct(w_in.shape, BF16),
                   jax.ShapeDtypeStruct((8, D_MODEL), F32)],
        scratch_shapes=[pltpu.VMEM(w_in.shape, F32), pltpu.VMEM((8, D_MODEL), F32)],
        compiler_params=_params("arbitrary"),
    )(dproj, x, dh1, g1, w_in)


def _place():
    x, y, c = lax.axis_index("x"), lax.axis_index("y"), lax.axis_index("c")
    chips = [(1 - x, y), (x, 1 - y), (1 - x, 1 - y)]
    return x, y, c, chips


def _rcopy(src, dst, ssem, rsem, dev):
    return pltpu.make_async_remote_copy(src_ref=src, dst_ref=dst, send_sem=ssem, recv_sem=rsem,
                                        device_id=dev, device_id_type=MESH)


ANY = pl.BlockSpec(memory_space=pl.ANY)
COPY_CHUNK_BYTES = 128 * 1024
ROW_ALIGN = 16


def _row_chunks(rows, row_bytes):
    per = max(ROW_ALIGN, (COPY_CHUNK_BYTES // row_bytes) // ROW_ALIGN * ROW_ALIGN)
    return [(r0, min(per, rows - r0)) for r0 in range(0, rows, per)]


def _row_bytes(a):
    return a.shape[-1] * jnp.dtype(a.dtype).itemsize


def _stack_own(shards, dtypes, pos, steps, tag, late=()):
    nw = len(shards)
    nl = len(late)

    def body(pos_ref, *refs):
        outs = refs[nw + nl:2 * nw + nl]
        late_ref = refs[2 * nw + nl:2 * nw + 2 * nl]
        i = pl.program_id(0)
        if nl:
            ssem, rsem = refs[2 * nw + 2 * nl:]

            @pl.when(i == 0)
            def _():
                _gather_send(late_ref, late, ssem, rsem)

        for w in range(nw):
            outs[w][0] = refs[w][...].astype(dtypes[w])

        if nl:
            @pl.when(i == max(steps - PASS_STEPS, 0))
            def _():
                _gather_pass(late_ref, late, ssem, rsem)

            @pl.when(i == steps - 1)
            def _():
                _gather_done(late_ref, late, ssem, rsem)

    def split(s):
        return s.shape[0] % (steps * ROW_ALIGN) == 0

    ins = [pl.BlockSpec((s.shape[0] // steps, s.shape[1]), lambda i, p: (i, 0)) if split(s)
           else pl.BlockSpec(s.shape, lambda i, p: (0, 0)) for s in shards]
    outs = [pl.BlockSpec((1, s.shape[0] // steps, s.shape[1]), lambda i, p: (p[0], i, 0)) if split(s)
            else pl.BlockSpec((1,) + s.shape, lambda i, p: (p[0], 0, 0)) for s in shards]
    res = pl.pallas_call(
        body, name="stack_own_" + tag,
        grid_spec=pltpu.PrefetchScalarGridSpec(
            num_scalar_prefetch=1, grid=(steps,), in_specs=ins + [ANY] * nl, out_specs=outs + [ANY] * nl,
            scratch_shapes=_gather_sems(nl) if nl else []),
        out_shape=[jax.ShapeDtypeStruct((N_CHIPS,) + s.shape, d) for s, d in zip(shards, dtypes)]
        + [jax.ShapeDtypeStruct(a.shape, a.dtype) for a in late],
        input_output_aliases={1 + nw + k: nw + k for k in range(nl)},
        compiler_params=_params("arbitrary"),
    )(pos, *shards, *late)
    return res[:nw], res[nw:]


def _gather_send(outs, stacks, ssem, rsem):
    x, y, c, chips = _place()
    me = 2 * x + y
    for w, st in enumerate(stacks):
        half = st.shape[1] // 2
        for s, (px, py) in enumerate(chips):
            for r0, rs in _row_chunks(half, _row_bytes(st)):
                piece = outs[w].at[me, pl.ds(c * half + r0, rs)]
                _rcopy(piece, piece, ssem.at[w, s], rsem.at[w, s], (px, py, c)).start()


def _gather_pass(outs, stacks, ssem, rsem):
    x, y, c, chips = _place()
    sib = (x, y, 1 - c)
    for w, st in enumerate(stacks):
        half = st.shape[1] // 2
        for s, (px, py) in enumerate(chips):
            blk = outs[w].at[2 * px + py, pl.ds(c * half, half)]
            _rcopy(blk, blk, ssem.at[w, s], rsem.at[w, s], sib).wait_recv()
            for r0, rs in _row_chunks(half, _row_bytes(st)):
                piece = outs[w].at[2 * px + py, pl.ds(c * half + r0, rs)]
                _rcopy(piece, piece, ssem.at[w, 3 + s], rsem.at[w, 3 + s], sib).start()


def _gather_done(outs, stacks, ssem, rsem):
    x, y, c, chips = _place()
    sib = (x, y, 1 - c)
    for w, st in enumerate(stacks):
        half = st.shape[1] // 2
        for s, (px, py) in enumerate(chips):
            blk = outs[w].at[2 * px + py, pl.ds((1 - c) * half, half)]
            _rcopy(blk, blk, ssem.at[w, 3 + s], rsem.at[w, 3 + s], sib).wait_recv()
    for w, st in enumerate(stacks):
        half = st.shape[1] // 2
        blk = outs[w].at[0, pl.ds(0, half)]
        for s in range(6):
            _rcopy(blk, blk, ssem.at[w, s], rsem.at[w, s], sib).wait_send()


def _gather_sems(nw):
    return [pltpu.SemaphoreType.DMA((nw, 6)), pltpu.SemaphoreType.DMA((nw, 6))]


def _swap_halves(grads, tag):
    nw = len(grads)

    def body(*refs):
        ins, got = refs[:nw], refs[nw:2 * nw]
        ssem, rsem = refs[2 * nw:]
        x, y, c, _ = _place()
        cps = []
        for w in range(nw):
            hr = grads[w].shape[1] // 2
            for k in range(N_CHIPS):
                for r0, rs in _row_chunks(hr, _row_bytes(grads[w])):
                    _rcopy(ins[w].at[k, pl.ds((1 - c) * hr + r0, rs)], got[w].at[k, pl.ds(r0, rs)],
                           ssem.at[w], rsem.at[w], (x, y, 1 - c)).start()
            cps.append(_rcopy(got[w], got[w], ssem.at[w], rsem.at[w], (x, y, 1 - c)))
        for cp in cps:
            cp.wait()

    return pl.pallas_call(
        body, name="swap_halves_" + tag,
        in_specs=[ANY] * nw, out_specs=[ANY] * nw,
        out_shape=[jax.ShapeDtypeStruct((g.shape[0], g.shape[1] // 2, g.shape[2]), g.dtype) for g in grads],
        scratch_shapes=[pltpu.SemaphoreType.DMA((nw,)), pltpu.SemaphoreType.DMA((nw,))],
    )(*grads)


def _add_pairs(grads, got, pos, steps, tag):
    nw = len(grads)

    def body(pos_ref, *refs):
        for w in range(nw):
            refs[2 * nw + w][...] = (refs[w][...].astype(F32) + refs[nw + w][...].astype(F32)).astype(BF16)

    blk = lambda a: (a.shape[0], a.shape[1] // steps, a.shape[2])
    own = [pl.BlockSpec(blk(a), lambda i, p: (0, p[1] * steps + i, 0)) for a in got]
    rec = [pl.BlockSpec(blk(a), lambda i, p: (0, i, 0)) for a in got]
    return pl.pallas_call(
        body, name="add_pairs_" + tag,
        grid_spec=pltpu.PrefetchScalarGridSpec(num_scalar_prefetch=1, grid=(steps,), in_specs=own + rec, out_specs=rec),
        out_shape=[jax.ShapeDtypeStruct(a.shape, BF16) for a in got],
        compiler_params=_params("arbitrary"),
    )(pos, *grads, *got)


def _scatter_send(ins, got, parts, ssem, rsem):
    x, y, c, chips = _place()
    for w, p in enumerate(parts):
        for s, (px, py) in enumerate(chips):
            for r0, rs in _row_chunks(p.shape[1], _row_bytes(p)):
                _rcopy(ins[w].at[2 * px + py, pl.ds(r0, rs)], got[w].at[s, pl.ds(r0, rs)],
                       ssem.at[w, s], rsem.at[w, s], (px, py, c)).start()


def _scatter_done(got, parts, ssem, rsem):
    x, y, c, chips = _place()
    for w in range(len(parts)):
        for s, (px, py) in enumerate(chips):
            _rcopy(got[w].at[s], got[w].at[s], ssem.at[w, s], rsem.at[w, s], (px, py, c)).wait()


def _scatter_sems(nw):
    return [pltpu.SemaphoreType.DMA((nw, 3)), pltpu.SemaphoreType.DMA((nw, 3))]


def _scatter_shapes(parts):
    return [jax.ShapeDtypeStruct((3,) + p.shape[1:], p.dtype) for p in parts]


def _scatter_chips(parts):
    nw = len(parts)

    def body(*refs):
        ins, got = refs[:nw], refs[nw:2 * nw]
        ssem, rsem = refs[2 * nw:]
        _scatter_send(ins, got, parts, ssem, rsem)
        _scatter_done(got, parts, ssem, rsem)

    return pl.pallas_call(
        body, name="scatter_chips",
        in_specs=[ANY] * nw, out_specs=[ANY] * nw, out_shape=_scatter_shapes(parts),
        scratch_shapes=_scatter_sems(nw),
    )(*parts)


def _sum_chips(parts, got, pos, steps):
    nw = len(parts)

    def body(pos_ref, *refs):
        for w in range(nw):
            acc = refs[w][0].astype(F32)
            for s in range(3):
                acc = acc + refs[nw + w][s].astype(F32)
            refs[2 * nw + w][...] = acc

    own = [pl.BlockSpec((1, p.shape[1] // steps, p.shape[2]), lambda i, ps: (ps[0], i, 0)) for p in parts]
    rec = [pl.BlockSpec((3, p.shape[1] // steps, p.shape[2]), lambda i, ps: (0, i, 0)) for p in parts]
    outs = [pl.BlockSpec((p.shape[1] // steps, p.shape[2]), lambda i, ps: (ps[1] * steps + i, 0)) for p in parts]
    return pl.pallas_call(
        body, name="sum_chips",
        grid_spec=pltpu.PrefetchScalarGridSpec(num_scalar_prefetch=1, grid=(steps,), in_specs=own + rec, out_specs=outs),
        out_shape=[jax.ShapeDtypeStruct((2 * p.shape[1], p.shape[2]), F32) for p in parts],
        compiler_params=_params("arbitrary"),
    )(pos, *parts, *got)


def _join_halves(shards):
    nw = len(shards)

    def body(*refs):
        outs = refs[nw:2 * nw]
        ssem, rsem = refs[2 * nw:]
        x, y, c, _ = _place()
        cps = []
        for w in range(nw):
            hr = shards[w].shape[0] // 2
            for r0, rs in _row_chunks(hr, _row_bytes(shards[w])):
                piece = outs[w].at[pl.ds(c * hr + r0, rs)]
                _rcopy(piece, piece, ssem.at[w], rsem.at[w], (x, y, 1 - c)).start()
            mine = outs[w].at[pl.ds(c * hr, hr)]
            cps.append(_rcopy(mine, mine, ssem.at[w], rsem.at[w], (x, y, 1 - c)))
        for cp in cps:
            cp.wait()

    return pl.pallas_call(
        body, name="join_halves",
        in_specs=[ANY] * nw, out_specs=[ANY] * nw,
        out_shape=[jax.ShapeDtypeStruct(s.shape, F32) for s in shards],
        input_output_aliases={w: w for w in range(nw)},
        scratch_shapes=[pltpu.SemaphoreType.DMA((nw,)), pltpu.SemaphoreType.DMA((nw,))],
    )(*shards)


def _scatter_and_allreduce(parts, packed):
    nw = len(parts)
    rows = packed.shape[0]
    half = rows // 2

    def body(*refs):
        ins, p_ref = refs[:nw], refs[nw]
        got, out_ref = refs[nw + 1:2 * nw + 1], refs[2 * nw + 1]
        rfull, rhalf, ssem, rsem, bsem_s, bsem_r = refs[2 * nw + 2:]
        x, y, c, _ = _place()
        sib = (x, y, 1 - c)
        _scatter_send(ins, got, parts, ssem, rsem)
        out_ref[...] = p_ref[...]
        cp = _rcopy(out_ref, rfull, bsem_s.at[0], bsem_r.at[0], sib)
        cp.start()
        cp.wait()
        out_ref[...] = out_ref[...] + rfull[...]
        mine = pl.ds(pl.multiple_of(c * half, 8), half)
        other = pl.ds(pl.multiple_of((1 - c) * half, 8), half)
        for st, peer in enumerate([(1 - x, y, c), (x, 1 - y, c)]):
            cp = _rcopy(out_ref.at[mine], rhalf.at[st], bsem_s.at[1 + st], bsem_r.at[1 + st], peer)
            cp.start()
            cp.wait()
            out_ref[mine, :] = out_ref[mine, :] + rhalf[st]
        cp = _rcopy(out_ref.at[mine], rhalf.at[2], bsem_s.at[3], bsem_r.at[3], sib)
        cp.start()
        cp.wait()
        out_ref[other, :] = rhalf[2]
        _scatter_done(got, parts, ssem, rsem)

    vm = pl.BlockSpec(memory_space=pltpu.VMEM)
    outs = pl.pallas_call(
        body, name="scatter_and_allreduce",
        in_specs=[ANY] * nw + [vm], out_specs=[ANY] * nw + [vm],
        out_shape=_scatter_shapes(parts) + [jax.ShapeDtypeStruct(packed.shape, F32)],
        scratch_shapes=[pltpu.VMEM(packed.shape, F32), pltpu.VMEM((3, half, packed.shape[1]), F32)] + _scatter_sems(nw)
        + [pltpu.SemaphoreType.DMA((4,)), pltpu.SemaphoreType.DMA((4,))],
        compiler_params=pltpu.CompilerParams(vmem_limit_bytes=VMEM_LIMIT),
    )(*parts, packed)
    return outs[:nw], outs[nw]


def _adamw_math(w, g, m, v):
    m = ADAM_B1 * m + (1.0 - ADAM_B1) * g
    v = ADAM_B2 * v + (1.0 - ADAM_B2) * (g * g)
    delta = -ADAM_LR * ((m * ADAM_C1) / (jnp.sqrt(v * ADAM_C2) + ADAM_EPS) + ADAM_WD * w)
    return delta, m, v


def _adamw(ws, gs, ms, vs, steps, name):
    nw = len(ws)

    def body(*refs):
        for k in range(nw):
            g = refs[nw + k][...]
            d, m, v = _adamw_math(refs[k][...], g, refs[2 * nw + k][...], refs[3 * nw + k][...])
            refs[4 * nw + k][...] = d
            refs[5 * nw + k][...] = m
            refs[6 * nw + k][...] = v
            refs[7 * nw + k][...] = g

    specs = [pl.BlockSpec((a.shape[0] // steps, a.shape[1]), lambda i: (i, 0)) for a in ws]
    shapes = [jax.ShapeDtypeStruct(a.shape, F32) for a in ws]
    outs = pl.pallas_call(
        body, name=name, grid=(steps,),
        in_specs=specs * 4, out_specs=specs * 4, out_shape=shapes * 4,
        compiler_params=_params("arbitrary"),
    )(*ws, *gs, *ms, *vs)
    return outs[:nw], outs[nw:2 * nw], outs[2 * nw:3 * nw], outs[3 * nw:]


SMALL = ["norm_mix_g", "pool_w", "pool_b", "pool_scale", "conv_b", "gate_a_w", "gate_a_b", "gate_x_w", "gate_x_b",
         "lru_L", "norm_mlp_g", "norm_ple_g", "b_ple_gate", "norm_final_g"]
BIG = ["w_in", "w_out", "w_up", "w_down", "w_ple_gate", "w_ple_proj"]
ORDER = ["norm_mix_g", "w_in", "pool_w", "pool_b", "pool_scale", "conv_w", "conv_b", "gate_a_w", "gate_a_b", "gate_x_w",
         "gate_x_b", "lru_L", "w_out", "norm_mlp_g", "w_up", "w_down", "norm_ple_g", "w_ple_gate", "b_ple_gate",
         "w_ple_proj", "norm_final_g"]
LANES = 128


def _block_diag(w):
    eye = jnp.eye(LRU_HEADS, dtype=w.dtype)
    return jnp.einsum("hij,hk->hikj", w, eye).reshape(D_LRU, D_LRU)


def _diag_blocks(full):
    f = full.reshape(LRU_HEADS, LRU_BLOCK, LRU_HEADS, LRU_BLOCK)
    return jnp.stack([f[h, :, h, :] for h in range(LRU_HEADS)])


def _rows128(a):
    return a.reshape(-1, LANES)


def _pad8(a):
    r = (-a.shape[0]) % 8
    return jnp.pad(a, ((0, r), (0, 0))) if r else a


def kernel(x, p, norm_mix_g, w_in, pool_w, pool_b, pool_scale, conv_w, conv_b, gate_a_w, gate_a_b, gate_x_w, gate_x_b, lru_L, w_out, norm_mlp_g, w_up, w_down, norm_ple_g, w_ple_gate, b_ple_gate, w_ple_proj, norm_final_g, loss_target, m_norm_mix_g, m_w_in, m_pool_w, m_pool_b, m_pool_scale, m_conv_w, m_conv_b, m_gate_a_w, m_gate_a_b, m_gate_x_w, m_gate_x_b, m_lru_L, m_w_out, m_norm_mlp_g, m_w_up, m_w_down, m_norm_ple_g, m_w_ple_gate, m_b_ple_gate, m_w_ple_proj, m_norm_final_g, v_norm_mix_g, v_w_in, v_pool_w, v_pool_b, v_pool_scale, v_conv_w, v_conv_b, v_gate_a_w, v_gate_a_b, v_gate_x_w, v_gate_x_b, v_lru_L, v_w_out, v_norm_mlp_g, v_w_up, v_w_down, v_norm_ple_g, v_w_ple_gate, v_b_ple_gate, v_w_ple_proj, v_norm_final_g):
    W = dict(norm_mix_g=norm_mix_g, w_in=w_in, pool_w=pool_w, pool_b=pool_b, pool_scale=pool_scale, conv_w=conv_w,
             conv_b=conv_b, gate_a_w=gate_a_w, gate_a_b=gate_a_b, gate_x_w=gate_x_w, gate_x_b=gate_x_b, lru_L=lru_L,
             w_out=w_out, norm_mlp_g=norm_mlp_g, w_up=w_up, w_down=w_down, norm_ple_g=norm_ple_g,
             w_ple_gate=w_ple_gate, b_ple_gate=b_ple_gate, w_ple_proj=w_ple_proj, norm_final_g=norm_final_g)
    M = dict(norm_mix_g=m_norm_mix_g, w_in=m_w_in, pool_w=m_pool_w, pool_b=m_pool_b, pool_scale=m_pool_scale,
             conv_w=m_conv_w, conv_b=m_conv_b, gate_a_w=m_gate_a_w, gate_a_b=m_gate_a_b, gate_x_w=m_gate_x_w,
             gate_x_b=m_gate_x_b, lru_L=m_lru_L, w_out=m_w_out, norm_mlp_g=m_norm_mlp_g, w_up=m_w_up, w_down=m_w_down,
             norm_ple_g=m_norm_ple_g, w_ple_gate=m_w_ple_gate, b_ple_gate=m_b_ple_gate, w_ple_proj=m_w_ple_proj,
             norm_final_g=m_norm_final_g)
    V = dict(norm_mix_g=v_norm_mix_g, w_in=v_w_in, pool_w=v_pool_w, pool_b=v_pool_b, pool_scale=v_pool_scale,
             conv_w=v_conv_w, conv_b=v_conv_b, gate_a_w=v_gate_a_w, gate_a_b=v_gate_a_b, gate_x_w=v_gate_x_w,
             gate_x_b=v_gate_x_b, lru_L=v_lru_L, w_out=v_w_out, norm_mlp_g=v_norm_mlp_g, w_up=v_w_up, w_down=v_w_down,
             norm_ple_g=v_norm_ple_g, w_ple_gate=v_w_ple_gate, b_ple_gate=v_b_ple_gate, w_ple_proj=v_w_ple_proj,
             norm_final_g=v_norm_final_g)

    s_len = x.shape[1]
    sub_mix = min(256, s_len)
    tm = min(512, s_len)
    chip = (2 * lax.axis_index("x") + lax.axis_index("y")).astype(jnp.int32)
    pos = jnp.stack([chip, lax.axis_index("c").astype(jnp.int32)])

    shards = [w_in[0], w_out[0], w_up[0], w_down[0], w_ple_gate[0], w_ple_proj[0], jnp.pad(conv_w[0], ((0, 12), (0, 0)))]
    first, _ = _stack_own([shards[0], shards[1], shards[6]], [BF16, BF16, F32], pos, 8, "first")
    (st_up, st_dn, st_pg, st_pp), (win_g, wout_g, cw_g) = _stack_own(shards[2:6], [BF16] * 4, pos, 8, "rest", first)
    wout_f = wout_g.reshape(D_MODEL, D_MODEL)
    cw_f = jnp.transpose(cw_g[:, :CONV_WIDTH], (1, 0, 2)).reshape(CONV_WIDTH, D_LRU)
    pw_b = pool_w[0].astype(BF16)
    wa_b = _block_diag(gate_a_w[0]).astype(BF16)
    wx_b = _block_diag(gate_x_w[0]).astype(BF16)
    pb_r = pool_b.reshape(1, D_POOL)
    ba_r = gate_a_b.reshape(1, D_LRU)
    bx_r = gate_x_b.reshape(1, D_LRU)
    g4 = norm_final_g.reshape(1, D_MODEL)
    mix_w = (pw_b, pb_r, pool_scale, cw_f, conv_b, wa_b, ba_r, wx_b, bx_r, lru_L, wout_f)

    xs, ps, ts = x[0], p[0, 0], loss_target[0]
    (proj, hst, h1), (wup_g, wdn_g) = _fwd_mix(xs, norm_mix_g, win_g, *mix_w, tm, sub_mix, [st_up, st_dn])
    (z2, ru, h2), (wpg_g, wpp_g) = _fwd_mlp(h1, norm_mlp_g, wup_g, wdn_g, tm, [st_pg, st_pp])
    wpg_f = wpg_g.reshape(D_MODEL, D_MODEL)
    wpp_f = jnp.transpose(wpp_g, (1, 0, 2)).reshape(PLE_DIM, D_MODEL)
    dh2, dh2b, d_wpg, d_wpp, head_vec = _head(h2, ps, ts, norm_ple_g, wpg_f, b_ple_gate, wpp_f, g4, tm)
    dup, dh1, mlp_vec = _bwd_mlp_x(dh2, ru, h1, norm_mlp_g, wup_g, wdn_g, tm)
    d_wup, d_wdn = _bwd_mlp_w(z2, dup, ru, dh2b, tm)
    early = [d_wup, d_wdn, d_wpg.reshape(N_CHIPS, D_MODEL // N_CHIPS, D_MODEL), d_wpp]
    pair_e = _add_pairs(early, _swap_halves(early, "early"), pos, 8, "early")
    (dproj, d_wout, d_pw, d_wa, d_wx, mix_vec), got_e = _bwd_mix(dh1, proj, hst, *mix_w, tm, sub_mix, pair_e)

    dx, d_win, in_vec = _bwd_in(dproj, xs, dh1, norm_mix_g, win_g, tm)

    last = [d_win, d_wout.reshape(N_CHIPS, D_MODEL // N_CHIPS, D_MODEL)]
    pair_l = _add_pairs(last, _swap_halves(last, "last"), pos, 8, "last")

    g_small = {
        "norm_mix_g": in_vec[0:1], "pool_w": d_pw, "pool_b": mix_vec[0:1], "pool_scale": mix_vec[1:2],
        "conv_b": mix_vec[2:3], "gate_a_w": _diag_blocks(d_wa), "gate_a_b": mix_vec[3:4],
        "gate_x_w": _diag_blocks(d_wx), "gate_x_b": mix_vec[4:5], "lru_L": mix_vec[5:6], "norm_mlp_g": mlp_vec[0:1],
        "norm_ple_g": head_vec[0:1], "b_ple_gate": head_vec[2:3], "norm_final_g": head_vec[1:2],
    }
    d_cw = jnp.transpose(mix_vec[8:8 + CONV_WIDTH].reshape(CONV_WIDTH, N_CHIPS, LANES), (1, 0, 2)).reshape(-1, LANES)
    pieces = [_pad8(_rows128(g_small[k])) for k in SMALL] + [d_cw, _pad8(head_vec[3:4, :LANES])]
    offs = [0]
    for pc in pieces:
        offs.append(offs[-1] + pc.shape[0])
    if offs[-1] % 16:
        pieces.append(jnp.zeros((8, LANES), F32))
    got_l, red = _scatter_and_allreduce(pair_l, jnp.concatenate(pieces, axis=0))
    g_big = _join_halves(_sum_chips(pair_l + pair_e, got_l + got_e, pos, 8))
    loss = red[offs[-2], 0]
    g_cw = lax.dynamic_slice(red, (offs[len(SMALL)] + CONV_WIDTH * chip, 0), (CONV_WIDTH, LANES))

    def packed(src):
        return jnp.concatenate([_pad8(_rows128(src[k])) for k in SMALL] + [_pad8(src["conv_w"][0])], axis=0)

    n_small = offs[len(SMALL)]
    g_pack = jnp.concatenate([red[:n_small], _pad8(g_cw)], axis=0)
    (d_pack,), (m_pack,), (v_pack,), _ = _adamw([packed(W)], [g_pack], [packed(M)], [packed(V)], 1, "adamw_small")

    big2d = lambda src: [src[k][0] for k in BIG]
    d_big, m_big, v_big, g_big = _adamw(big2d(W), g_big, big2d(M), big2d(V), 8, "adamw_big")

    def unpack(pack, big_list):
        out = {}
        for idx, k in enumerate(SMALL):
            n_el = W[k].size
            out[k] = pack[offs[idx]:offs[idx + 1]].reshape(-1)[:n_el].reshape(W[k].shape)
        out["conv_w"] = pack[n_small:n_small + CONV_WIDTH].reshape(W["conv_w"].shape)
        for k, a in zip(BIG, big_list):
            out[k] = a.reshape(W[k].shape)
        return out

    grads = unpack(g_pack, g_big)
    deltas = unpack(d_pack, d_big)
    new_m = unpack(m_pack, m_big)
    new_v = unpack(v_pack, v_big)
    return (loss, dx[None], *[grads[k] for k in ORDER], *[deltas[k] for k in ORDER],
            *[new_m[k] for k in ORDER], *[new_v[k] for k in ORDER])
```

```python
import functools

import jax
import jax.numpy as jnp
from jax import lax
from jax.experimental import pallas as pl
from jax.experimental.pallas import tpu as pltpu

F32 = jnp.float32
BF16 = jnp.bfloat16
MESH = pl.DeviceIdType.MESH

D_MODEL = 1024
D_POOL = 512
D_LRU = 512
POOL_WINDOWS = (2, 4, 8, 16)
POOL_GROUP = 128
POOL_HALO = 16
CONV_WIDTH = 4
CONV_HALO = 8
PASS_STEPS = 2
LRU_HEADS = 8
LRU_BLOCK = 64
LRU_C = 8.0
LRU_UNROLL = 4
D_FF = 4096
PLE_DIM = 256
D_IN_PROJ = 1536
RMS_EPS = 1e-6
N_CHIPS = 4
FF_BLOCK = D_FF // N_CHIPS
FF_PAIR = 2

ADAM_LR = 0.001
ADAM_B1 = 0.9
ADAM_B2 = 0.999
ADAM_EPS = 1e-08
ADAM_WD = 0.01
ADAM_STEP = 10
ADAM_C1 = 1.0 / (1.0 - ADAM_B1 ** ADAM_STEP)
ADAM_C2 = 1.0 / (1.0 - ADAM_B2 ** ADAM_STEP)

VMEM_LIMIT = 56 * 1024 * 1024
GELU_C = 0.7978845608028654
GELU_A = 0.044715

NT = (((1,), (1,)), ((), ()))
TN = (((0,), (0,)), ((), ()))


def _dot(a, b):
    return jnp.dot(a, b, preferred_element_type=F32)


def _dot_nt(a, b):
    return lax.dot_general(a, b, NT, preferred_element_type=F32)


def _dot_tn(a, b):
    return lax.dot_general(a, b, TN, preferred_element_type=F32)


def _params(*sem):
    return pltpu.CompilerParams(dimension_semantics=sem, vmem_limit_bytes=VMEM_LIMIT)


def _full(shape):
    nd = len(shape)
    return pl.BlockSpec(shape, lambda *_: (0,) * nd)


def _resident(shape):
    nd = len(shape)
    return pl.BlockSpec(shape, lambda *_: (0,) * nd, pipeline_mode=pl.Buffered(1))


def _rstd(x):
    return lax.rsqrt(jnp.mean(x * x, axis=-1, keepdims=True) + RMS_EPS)


def _rms_bwd(x, g, dz):
    xr = x * _rstd(x)
    r = _rstd(x)
    dyg = dz * g
    dx = r * (dyg - xr * jnp.mean(dyg * xr, axis=-1, keepdims=True))
    dg = jnp.sum(dz * xr, axis=0, keepdims=True)
    return dx, dg


def _sigmoid(x):
    return 1.0 / (1.0 + jnp.exp(-x))


def _log_sigmoid(v):
    u = jnp.exp(-jnp.abs(v))
    w = 1.0 + u
    l1p = jnp.where(w == 1.0, u, jnp.log(w) * u / jnp.where(w == 1.0, 1.0, w - 1.0))
    return jnp.minimum(v, 0.0) - l1p


def _gelu(x):
    t = jnp.tanh(GELU_C * (x + GELU_A * x * x * x))
    return 0.5 * x * (1.0 + t), t


def _gelu_grad(x, t):
    return 0.5 * (1.0 + t) + 0.5 * x * (1.0 - t * t) * GELU_C * (1.0 + 3.0 * GELU_A * x * x)


def _rows(shape, t0):
    return lax.broadcasted_iota(jnp.int32, shape, 0) + t0


def _pool_diff(u_pool, prev, t0):
    tm = u_pool.shape[0]
    rows = _rows((tm, POOL_GROUP), t0)
    outs, invs = [], []
    for g, w in enumerate(POOL_WINDOWS):
        sl = slice(POOL_GROUP * g, POOL_GROUP * (g + 1))
        ug = u_pool[:, sl]
        s = jnp.concatenate([prev[:, sl], ug], axis=0)
        k = 1
        while k < w:
            s = s + pltpu.roll(s, k, 0)
            k *= 2
        inv = 1.0 / jnp.minimum(rows + 1, w).astype(F32)
        outs.append(s[POOL_HALO:] * inv - ug)
        invs.append(inv)
    return jnp.concatenate(outs, axis=1), jnp.concatenate(invs, axis=1)


def _pool_diff_bwd(dd, inv, nxt):
    ddc = dd * inv
    outs = []
    for g, w in enumerate(POOL_WINDOWS):
        sl = slice(POOL_GROUP * g, POOL_GROUP * (g + 1))
        s = jnp.concatenate([ddc[:, sl], nxt[:, sl]], axis=0)
        n = s.shape[0]
        k = 1
        while k < w:
            s = s + pltpu.roll(s, n - k, 0)
            k *= 2
        outs.append(s[:n - POOL_HALO] - dd[:, sl])
    return jnp.concatenate(outs, axis=1), ddc


def _conv_taps(u, prev):
    ext = jnp.concatenate([prev, u], axis=0)
    return [pltpu.roll(ext, CONV_WIDTH - 1 - k, 0)[CONV_HALO:] if k < CONV_WIDTH - 1 else u for k in range(CONV_WIDTH)]


def _scan_rev(a, b, carry):
    tm = a.shape[0]
    r8 = _rows(a.shape, 0) & 7
    k = 1
    while k < 8:
        ar = pltpu.roll(a, tm - k, 0)
        br = pltpu.roll(b, tm - k, 0)
        m = r8 < 8 - k
        b = jnp.where(m, a * br + b, b)
        a = jnp.where(m, a * ar, a)
        k *= 2
    outs = []
    c = jnp.broadcast_to(carry, (8, a.shape[1]))
    for g in reversed(range(tm // 8)):
        gg = b[8 * g:8 * g + 8] + a[8 * g:8 * g + 8] * c
        outs.insert(0, gg)
        c = jnp.broadcast_to(gg[0:1], c.shape)
    return jnp.concatenate(outs, axis=0)


def _lru_gates(xb, wa, ba, wx, bx, lsl8, t0):
    xbb = xb.astype(BF16)
    r = _sigmoid(_dot(xbb, wa) + ba)
    ig = _sigmoid(_dot(xbb, wx) + bx)
    a = jnp.exp(r * lsl8)
    first = _rows(xb.shape, t0) == 0
    m2 = 1.0 - a * a
    rs = lax.rsqrt(jnp.maximum(m2, jnp.finfo(F32).tiny))
    mult = jnp.where(first, 1.0, m2 * rs)
    return r, ig, a, mult, rs, first


def _lru_chunks_fwd(r_ref, i_ref, xb_ref, ug_ref, h_ref, y_ref, row0, n_rows, lsl8, t0, hc):
    lsl = jnp.broadcast_to(lsl8, (8, D_LRU))
    sub8 = lax.broadcasted_iota(jnp.int32, (8, D_LRU), 0)

    def chunk(j, hc):
        o = j * 8
        rows = pl.ds(o, 8)
        r = _sigmoid(r_ref[rows, :])
        ig = _sigmoid(i_ref[rows, :])
        a = jnp.exp(r * lsl)
        m2 = 1.0 - a * a
        mult = jnp.where(sub8 + (t0 + o) == 0, 1.0, m2 * lax.rsqrt(jnp.maximum(m2, jnp.finfo(F32).tiny)))
        b = mult * (ig * xb_ref[rows, :])
        k = 1
        while k < 8:
            m = sub8 >= k
            b = jnp.where(m, a * pltpu.roll(b, k, 0) + b, b)
            a = jnp.where(m, a * pltpu.roll(a, k, 0), a)
            k *= 2
        h = b + a * hc
        h_ref[pl.ds(row0 + o, 8), :] = h
        gl, _ = _gelu(ug_ref[rows, :])
        y_ref[rows, :] = h * gl
        return jnp.broadcast_to(h[7:8, :], (8, D_LRU))

    for j in range(n_rows // 8):
        hc = chunk(j, hc)
    return hc


def _fwd_mix(x, g1, w_in, pool_w, pool_b, pool_scale, conv_w, conv_b, wa, ba, wx, bx, lru_l, w_out, tm, sub, late):
    s_len = x.shape[0]
    n = s_len // tm
    nl = len(late)

    def body(x_ref, g1_ref, win_ref, pw_ref, pb_ref, ps_ref, cw_ref, cb_ref, wa_ref, ba_ref, wx_ref, bx_ref, l_ref,
             wout_ref, *rest):
        proj_ref, h_ref, h1_ref = rest[nl:nl + 3]
        late_ref = rest[nl + 3:2 * nl + 3]
        cpool, clru, ch, s_r, s_i, s_xb, s_ug, s_y, ssem, rsem = rest[2 * nl + 3:]
        i = pl.program_id(0)

        @pl.when(i == 0)
        def _():
            _gather_send(late_ref, late, ssem, rsem)
            cpool[...] = jnp.zeros_like(cpool)
            clru[...] = jnp.zeros_like(clru)
            ch[...] = jnp.zeros_like(ch)

        lsl8 = LRU_C * _log_sigmoid(l_ref[...])
        cp, cl, hc = cpool[...], clru[...], ch[...]

        def in_proj(k):
            rs = slice(k * sub, (k + 1) * sub)
            xv = x_ref[rs, :]
            zb = (xv * _rstd(xv) * g1_ref[...]).astype(BF16)
            proj = jnp.concatenate([_dot(zb, win_ref[j]) for j in range(N_CHIPS)], axis=1)
            proj_ref[rs, :] = proj.astype(BF16)
            return xv, proj

        nxt = in_proj(0)
        for k in range(tm // sub):
            rs = slice(k * sub, (k + 1) * sub)
            t0 = i * tm + k * sub
            xv, proj = nxt
            if k + 1 < tm // sub:
                nxt = in_proj(k + 1)
            u_pool = proj[:, :D_POOL]
            u_lru = proj[:, D_POOL:D_POOL + D_LRU]
            u_gate = proj[:, D_POOL + D_LRU:]

            d, _ = _pool_diff(u_pool, cp, t0)
            cp = u_pool[sub - POOL_HALO:]
            db = d.astype(BF16)
            yp = jnp.concatenate(
                [_dot(db[:, POOL_GROUP * g:POOL_GROUP * (g + 1)], pw_ref[g]) for g in range(len(POOL_WINDOWS))], axis=1)
            y_pool = (yp + pb_ref[...]) * ps_ref[...]

            taps = _conv_taps(u_lru, cl)
            cl = u_lru[sub - CONV_HALO:]
            xb = cb_ref[...]
            for q in range(CONV_WIDTH):
                xb = xb + taps[q] * cw_ref[q:q + 1, :]
            xbb = xb.astype(BF16)
            s_r[...] = _dot(xbb, wa_ref[...]) + ba_ref[...]
            s_i[...] = _dot(xbb, wx_ref[...]) + bx_ref[...]
            s_xb[...] = xb
            s_ug[...] = u_gate
            hc = _lru_chunks_fwd(s_r, s_i, s_xb, s_ug, h_ref, s_y, k * sub, sub, lsl8, t0, hc)
            cat = jnp.concatenate([y_pool, s_y[...]], axis=1).astype(BF16)
            h1_ref[rs, :] = xv + _dot(cat, wout_ref[...])
        cpool[...] = cp
        clru[...] = cl
        ch[...] = hc

        @pl.when(i == max(n - PASS_STEPS, 0))
        def _():
            _gather_pass(late_ref, late, ssem, rsem)

        @pl.when(i == n - 1)
        def _():
            _gather_done(late_ref, late, ssem, rsem)

    row = lambda w: pl.BlockSpec((tm, w), lambda i: (i, 0))
    ins = [x, g1, w_in, pool_w, pool_b, pool_scale, conv_w, conv_b, wa, ba, wx, bx, lru_l, w_out]
    outs = pl.pallas_call(
        body, name="fwd_mix", grid=(n,),
        in_specs=[row(D_MODEL)] + [_resident(a.shape) for a in ins[1:]] + [ANY] * nl,
        out_specs=[row(D_IN_PROJ), row(D_LRU), row(D_MODEL)] + [ANY] * nl,
        out_shape=[jax.ShapeDtypeStruct((s_len, D_IN_PROJ), BF16), jax.ShapeDtypeStruct((s_len, D_LRU), F32),
                   jax.ShapeDtypeStruct((s_len, D_MODEL), F32)]
        + [jax.ShapeDtypeStruct(a.shape, a.dtype) for a in late],
        input_output_aliases={len(ins) + k: 3 + k for k in range(nl)},
        scratch_shapes=[pltpu.VMEM((POOL_HALO, D_POOL), F32), pltpu.VMEM((CONV_HALO, D_LRU), F32),
                        pltpu.VMEM((8, D_LRU), F32)] + [pltpu.VMEM((sub, D_LRU), F32)] * 5 + _gather_sems(nl),
        compiler_params=_params("arbitrary"),
    )(*ins, *late)
    return outs[:3], outs[3:]


def _fwd_mlp(h1, g2, w_up, w_down, tm, late):
    s_len = h1.shape[0]
    n = s_len // tm
    nl = len(late)

    def body(h1_ref, g2_ref, wup_ref, wdn_ref, *rest):
        z2_ref, ru_ref, h2_ref = rest[nl:nl + 3]
        late_ref = rest[nl + 3:2 * nl + 3]
        ssem, rsem = rest[2 * nl + 3:]
        i = pl.program_id(0)

        @pl.when(i == 0)
        def _():
            _gather_send(late_ref, late, ssem, rsem)

        hv = h1_ref[...]
        zb = (hv * _rstd(hv) * g2_ref[...]).astype(BF16)
        z2_ref[...] = zb
        acc = hv
        for j in range(N_CHIPS):
            ru = jnp.maximum(_dot(zb, wup_ref[j]), 0.0)
            ru_ref[:, FF_BLOCK * j:FF_BLOCK * (j + 1)] = ru.astype(BF16)
            acc = acc + _dot((ru * ru).astype(BF16), wdn_ref[j])
        h2_ref[...] = acc

        @pl.when(i == max(n - PASS_STEPS, 0))
        def _():
            _gather_pass(late_ref, late, ssem, rsem)

        @pl.when(i == n - 1)
        def _():
            _gather_done(late_ref, late, ssem, rsem)

    row = lambda w: pl.BlockSpec((tm, w), lambda i: (i, 0))
    outs = pl.pallas_call(
        body, name="fwd_mlp", grid=(n,),
        in_specs=[row(D_MODEL), _full(g2.shape), _resident(w_up.shape), _resident(w_down.shape)] + [ANY] * nl,
        out_specs=[row(D_MODEL), row(D_FF), row(D_MODEL)] + [ANY] * nl,
        out_shape=[jax.ShapeDtypeStruct((s_len, D_MODEL), BF16), jax.ShapeDtypeStruct((s_len, D_FF), BF16),
                   jax.ShapeDtypeStruct((s_len, D_MODEL), F32)] + [jax.ShapeDtypeStruct(a.shape, a.dtype) for a in late],
        input_output_aliases={4 + k: 3 + k for k in range(nl)},
        scratch_shapes=_gather_sems(nl),
        compiler_params=_params("arbitrary"),
    )(h1, g2, w_up, w_down, *late)
    return outs[:3], outs[3:]


def _head(h2, p, target, g3, w_pg, b_pg, w_pp, g4, tm):
    s_len = h2.shape[0]
    n = s_len // tm

    def body(h2_ref, p_ref, t_ref, g3_ref, wpg_ref, bpg_ref, wpp_ref, g4_ref,
             dh2_ref, dh2b_ref, dwpg_ref, dwpp_ref, vec_ref, a_pg, a_pp, a_vec):
        i = pl.program_id(0)

        @pl.when(i == 0)
        def _():
            a_pg[...] = jnp.zeros_like(a_pg)
            a_pp[...] = jnp.zeros_like(a_pp)
            a_vec[...] = jnp.zeros_like(a_vec)

        h2v = h2_ref[...]
        g3v = g3_ref[...]
        g4v = g4_ref[...]
        z3 = (h2v * _rstd(h2v) * g3v).astype(BF16)
        gate = _sigmoid(_dot(z3, wpg_ref[...]) + bpg_ref[...])
        pb = p_ref[...].astype(BF16)
        pp = _dot(pb, wpp_ref[...])
        h3 = h2v + gate * pp
        r4 = _rstd(h3)
        diff = h3 * r4 * g4v - t_ref[...]
        loss = 0.5 * jnp.sum(jnp.mean(diff * diff, axis=-1, keepdims=True), axis=0, keepdims=True)
        dy = diff * (1.0 / D_MODEL)
        dh3, dg4 = _rms_bwd(h3, g4v, dy)
        dpp = (dh3 * gate).astype(BF16)
        dpre = dh3 * pp * gate * (1.0 - gate)
        dpreb = dpre.astype(BF16)
        dz3 = _dot_nt(dpreb, wpg_ref[...])
        dx, dg3 = _rms_bwd(h2v, g3v, dz3)
        dh2 = dh3 + dx
        dh2_ref[...] = dh2
        dh2b_ref[...] = dh2.astype(BF16)
        a_pg[...] += _dot_tn(z3, dpreb)
        a_pp[...] += _dot_tn(pb, dpp)
        a_vec[0:1, :] += dg3
        a_vec[1:2, :] += dg4
        a_vec[2:3, :] += jnp.sum(dpre, axis=0, keepdims=True)
        a_vec[3:4, :] += jnp.broadcast_to(loss, (1, D_MODEL))

        @pl.when(i == n - 1)
        def _():
            dwpg_ref[...] = a_pg[...].astype(BF16)
            for j in range(N_CHIPS):
                dwpp_ref[j] = a_pp[:, PLE_DIM * j:PLE_DIM * (j + 1)].astype(BF16)
            vec_ref[...] = a_vec[...]

    row = lambda w: pl.BlockSpec((tm, w), lambda i: (i, 0))
    ins = [h2, p, target, g3, w_pg, b_pg, w_pp, g4]
    return pl.pallas_call(
        body, name="head", grid=(n,),
        in_specs=[row(D_MODEL), row(PLE_DIM), row(D_MODEL)] + [_resident(a.shape) for a in ins[3:]],
        out_specs=[row(D_MODEL), row(D_MODEL), _full((D_MODEL, D_MODEL)), _full((N_CHIPS, PLE_DIM, PLE_DIM)),
                   _full((8, D_MODEL))],
        out_shape=[jax.ShapeDtypeStruct((s_len, D_MODEL), F32), jax.ShapeDtypeStruct((s_len, D_MODEL), BF16),
                   jax.ShapeDtypeStruct((D_MODEL, D_MODEL), BF16),
                   jax.ShapeDtypeStruct((N_CHIPS, PLE_DIM, PLE_DIM), BF16), jax.ShapeDtypeStruct((8, D_MODEL), F32)],
        scratch_shapes=[pltpu.VMEM((D_MODEL, D_MODEL), F32), pltpu.VMEM((PLE_DIM, D_MODEL), F32),
                        pltpu.VMEM((8, D_MODEL), F32)],
        compiler_params=_params("arbitrary"),
    )(*ins)


def _bwd_mlp_x(dh2, ru, h1, g2, w_up, w_down, tm):
    s_len = dh2.shape[0]
    n = s_len // tm

    def body(dh2_ref, ru_ref, h1_ref, g2_ref, wup_ref, wdn_ref, dup_ref, dh1_ref, dg2_ref, a_g):
        i = pl.program_id(0)

        @pl.when(i == 0)
        def _():
            a_g[...] = jnp.zeros_like(a_g)

        dh2v = dh2_ref[...]
        dhb = dh2v.astype(BF16)
        acc = jnp.zeros((tm, D_MODEL), F32)
        for j in range(N_CHIPS):
            sl = slice(FF_BLOCK * j, FF_BLOCK * (j + 1))
            dup = (_dot_nt(dhb, wdn_ref[j]) * (2.0 * ru_ref[:, sl].astype(F32))).astype(BF16)
            dup_ref[:, sl] = dup
            acc = acc + _dot_nt(dup, wup_ref[j])
        dx, dg = _rms_bwd(h1_ref[...], g2_ref[...], acc)
        dh1_ref[...] = dh2v + dx
        a_g[0:1, :] += dg

        @pl.when(i == n - 1)
        def _():
            dg2_ref[...] = a_g[...]

    row = lambda w: pl.BlockSpec((tm, w), lambda i: (i, 0))
    return pl.pallas_call(
        body, name="bwd_mlp_x", grid=(n,),
        in_specs=[row(D_MODEL), row(D_FF), row(D_MODEL), _full(g2.shape), _resident(w_up.shape), _resident(w_down.shape)],
        out_specs=[row(D_FF), row(D_MODEL), _full((8, D_MODEL))],
        out_shape=[jax.ShapeDtypeStruct((s_len, D_FF), BF16), jax.ShapeDtypeStruct((s_len, D_MODEL), F32),
                   jax.ShapeDtypeStruct((8, D_MODEL), F32)],
        scratch_shapes=[pltpu.VMEM((8, D_MODEL), F32)],
        compiler_params=_params("arbitrary"),
    )(dh2, ru, h1, g2, w_up, w_down)


def _bwd_mlp_w(z2, dup, ru, dh2, tk):
    s_len = z2.shape[0]
    n = s_len // tk

    def body(z2_ref, dup_ref, ru_ref, dh2_ref, dwup_ref, dwdn_ref, a_up, a_dn):
        t = pl.program_id(1)

        @pl.when(t == 0)
        def _():
            a_up[...] = jnp.zeros_like(a_up)
            a_dn[...] = jnp.zeros_like(a_dn)

        for b in range(FF_PAIR):
            sl = slice(FF_BLOCK * b, FF_BLOCK * (b + 1))
            ruv = ru_ref[:, sl]
            a_up[b] += _dot_tn(z2_ref[...], dup_ref[:, sl])
            a_dn[b] += _dot_tn(ruv * ruv, dh2_ref[...])

        @pl.when(t == n - 1)
        def _():
            dwup_ref[...] = a_up[...].astype(BF16)
            dwdn_ref[...] = a_dn[...].astype(BF16)

    tile = pl.BlockSpec((tk, D_MODEL), lambda j, t: (t, 0))
    ffb = pl.BlockSpec((tk, FF_PAIR * FF_BLOCK), lambda j, t: (t, j))
    return pl.pallas_call(
        body, name="bwd_mlp_w", grid=(N_CHIPS // FF_PAIR, n),
        in_specs=[tile, ffb, ffb, tile],
        out_specs=[pl.BlockSpec((FF_PAIR, D_MODEL, FF_BLOCK), lambda j, t: (j, 0, 0)),
                   pl.BlockSpec((FF_PAIR, FF_BLOCK, D_MODEL), lambda j, t: (j, 0, 0))],
        out_shape=[jax.ShapeDtypeStruct((N_CHIPS, D_MODEL, FF_BLOCK), BF16),
                   jax.ShapeDtypeStruct((N_CHIPS, FF_BLOCK, D_MODEL), BF16)],
        scratch_shapes=[pltpu.VMEM((FF_PAIR, D_MODEL, FF_BLOCK), F32), pltpu.VMEM((FF_PAIR, FF_BLOCK, D_MODEL), F32)],
        compiler_params=_params("arbitrary", "arbitrary"),
    )(z2, dup, ru, dh2)


MIX_VEC_ROWS = 16


def _bwd_mix(dh1, proj, h, pool_w, pool_b, pool_scale, conv_w, conv_b, wa, ba, wx, bx, lru_l, w_out, tm, sub, early):
    s_len = dh1.shape[0]
    n = s_len // tm
    ng = len(POOL_WINDOWS)
    ne_ = len(early)

    def body(dh1_ref, proj_ref, h_ref, projh_ref, hh_ref, pw_ref, pb_ref, ps_ref, cw_ref, cb_ref,
             wa_ref, ba_ref, wx_ref, bx_ref, l_ref, wout_ref, *rest):
        early_ref = rest[:ne_]
        dproj_ref, dwout_ref, dpw_ref, dwa_ref, dwx_ref, vec_ref = rest[ne_:ne_ + 6]
        got_ref = rest[ne_ + 6:2 * ne_ + 6]
        a_out, a_pw, a_wa, a_wx, a_vec, c_g, c_dxb, c_ddc, ssem, rsem = rest[2 * ne_ + 6:]
        q = pl.program_id(0)
        i = n - 1 - q

        @pl.when(q == 0)
        def _():
            _scatter_send(early_ref, got_ref, early, ssem, rsem)
            for r in (a_out, a_pw, a_wa, a_wx, a_vec, c_g, c_dxb, c_ddc):
                r[...] = jnp.zeros_like(r)

        has_prev = (i > 0).astype(F32)
        lv = l_ref[...]
        lsl8 = LRU_C * _log_sigmoid(lv)
        cg, cdxb, cddc = c_g[0:1, :], c_dxb[...], c_ddc[...]
        vec, cats, dhbs = {}, [], []

        def add(row, v):
            vec[row] = v if row not in vec else vec[row] + v

        for k in reversed(range(tm // sub)):
            rs = slice(k * sub, (k + 1) * sub)
            t0 = i * tm + k * sub
            dh1b = dh1_ref[rs, :].astype(BF16)
            dcat = _dot_nt(dh1b, wout_ref[...])
            dy_pool = dcat[:, :D_POOL]
            dy_lru = dcat[:, D_POOL:]

            proj = proj_ref[rs, :].astype(F32)
            u_pool = proj[:, :D_POOL]
            u_lru = proj[:, D_POOL:D_POOL + D_LRU]
            u_gate = proj[:, D_POOL + D_LRU:]
            if k > 0:
                halo = proj_ref[k * sub - POOL_HALO:k * sub, :].astype(F32)
                h_prev_row = h_ref[k * sub - 1:k * sub, :]
            else:
                halo = projh_ref[...].astype(F32) * has_prev
                h_prev_row = hh_ref[7:8, :] * has_prev

            d, inv = _pool_diff(u_pool, halo[:, :D_POOL], t0)
            db = d.astype(BF16)
            ypre = jnp.concatenate(
                [_dot(db[:, POOL_GROUP * g:POOL_GROUP * (g + 1)], pw_ref[g]) for g in range(ng)], axis=1) + pb_ref[...]
            dyp = dy_pool * ps_ref[...]
            dypb = dyp.astype(BF16)
            dds = []
            for g in range(ng):
                sl = slice(POOL_GROUP * g, POOL_GROUP * (g + 1))
                a_pw[g] += _dot_tn(db[:, sl], dypb[:, sl])
                dds.append(_dot_nt(dypb[:, sl], pw_ref[g]))
            du_pool, ddc = _pool_diff_bwd(jnp.concatenate(dds, axis=1), inv, cddc)
            cddc = ddc[:POOL_HALO]
            add(0, jnp.sum(dyp, axis=0, keepdims=True))
            add(1, jnp.sum(dy_pool * ypre, axis=0, keepdims=True))

            taps = _conv_taps(u_lru, halo[POOL_HALO - CONV_HALO:, D_POOL:D_POOL + D_LRU])
            xb = cb_ref[...]
            for c in range(CONV_WIDTH):
                xb = xb + taps[c] * cw_ref[c:c + 1, :]
            r, ig, a, mult, inv_mult, first = _lru_gates(xb, wa_ref[...], ba_ref[...], wx_ref[...], bx_ref[...], lsl8, t0)
            hv = h_ref[rs, :]
            gl, th = _gelu(u_gate)
            du_gate = dy_lru * hv * _gelu_grad(u_gate, th)
            cats.insert(0, jnp.concatenate([ypre * ps_ref[...], hv * gl], axis=1).astype(BF16))
            dhbs.insert(0, dh1b)
            last = _rows(a.shape, 0) == sub - 1
            a_next = jnp.where(last, 1.0, pltpu.roll(a, sub - 1, 0))
            gh = _scan_rev(a_next, dy_lru * gl, cg)
            cg = a[0:1, :] * gh[0:1, :]
            h_prev = jnp.where(_rows(hv.shape, 0) == 0, h_prev_row, pltpu.roll(hv, 1, 0))
            gix = gh * ig * xb
            dla = gh * h_prev * a - jnp.where(first, 0.0, gix * a * a * inv_mult)
            dpre_r = dla * lsl8 * r * (1.0 - r)
            dpre_i = gh * mult * xb * ig * (1.0 - ig)
            dprb = dpre_r.astype(BF16)
            dpib = dpre_i.astype(BF16)
            xbb = xb.astype(BF16)
            a_wa[...] += _dot_tn(xbb, dprb)
            a_wx[...] += _dot_tn(xbb, dpib)
            dxb = gh * mult * ig + _dot_nt(dprb, wa_ref[...]) + _dot_nt(dpib, wx_ref[...])
            add(2, jnp.sum(dxb, axis=0, keepdims=True))
            add(3, jnp.sum(dpre_r, axis=0, keepdims=True))
            add(4, jnp.sum(dpre_i, axis=0, keepdims=True))
            add(5, jnp.sum(dla * r, axis=0, keepdims=True))
            ext = jnp.concatenate([dxb, cdxb], axis=0)
            cdxb = dxb[:CONV_HALO]
            ne = sub + CONV_HALO
            du_lru = dxb * cw_ref[CONV_WIDTH - 1:CONV_WIDTH, :]
            for c in range(CONV_WIDTH):
                add(8 + c, jnp.sum(dxb * taps[c], axis=0, keepdims=True))
                if c < CONV_WIDTH - 1:
                    du_lru = du_lru + pltpu.roll(ext, ne - (CONV_WIDTH - 1 - c), 0)[:sub] * cw_ref[c:c + 1, :]
            dproj_ref[rs, :] = jnp.concatenate([du_pool, du_lru, du_gate], axis=1).astype(BF16)
        a_out[...] += _dot_tn(jnp.concatenate(cats, axis=0), jnp.concatenate(dhbs, axis=0))
        c_g[...] = jnp.broadcast_to(cg, c_g.shape)
        c_dxb[...] = cdxb
        c_ddc[...] = cddc
        for row, v in vec.items():
            a_vec[row:row + 1, :] += v

        @pl.when(q == n - 1)
        def _():
            dwout_ref[...] = a_out[...].astype(BF16)
            dpw_ref[...] = a_pw[...]
            dwa_ref[...] = a_wa[...]
            dwx_ref[...] = a_wx[...]
            vec_ref[...] = a_vec[...]
            vec_ref[5:6, :] = a_vec[5:6, :] * (LRU_C * _sigmoid(-lv))
            _scatter_done(got_ref, early, ssem, rsem)

    rev =lambda w: pl.BlockSpec((tm, w), lambda q: (n - 1 - q, 0))
    halo_p = pl.BlockSpec((POOL_HALO, D_IN_PROJ), lambda q: (jnp.maximum((n - 1 - q) * (tm // POOL_HALO) - 1, 0), 0))
    halo_h = pl.BlockSpec((8, D_LRU), lambda q: (jnp.maximum((n - 1 - q) * (tm // 8) - 1, 0), 0))
    wts = [pool_w, pool_b, pool_scale, conv_w, conv_b, wa, ba, wx, bx, lru_l, w_out]
    outs = pl.pallas_call(
        body, name="bwd_mix", grid=(n,),
        in_specs=[rev(D_MODEL), rev(D_IN_PROJ), rev(D_LRU), halo_p, halo_h] + [_resident(a.shape) for a in wts]
        + [ANY] * ne_,
        out_specs=[rev(D_IN_PROJ), _full((D_MODEL, D_MODEL)), _full((ng, POOL_GROUP, POOL_GROUP)),
                   _full((D_LRU, D_LRU)), _full((D_LRU, D_LRU)), _full((MIX_VEC_ROWS, D_LRU))] + [ANY] * ne_,
        out_shape=[jax.ShapeDtypeStruct((s_len, D_IN_PROJ), BF16), jax.ShapeDtypeStruct((D_MODEL, D_MODEL), BF16),
                   jax.ShapeDtypeStruct((ng, POOL_GROUP, POOL_GROUP), F32), jax.ShapeDtypeStruct((D_LRU, D_LRU), F32),
                   jax.ShapeDtypeStruct((D_LRU, D_LRU), F32), jax.ShapeDtypeStruct((MIX_VEC_ROWS, D_LRU), F32)]
        + _scatter_shapes(early),
        scratch_shapes=[pltpu.VMEM((D_MODEL, D_MODEL), F32), pltpu.VMEM((ng, POOL_GROUP, POOL_GROUP), F32),
                        pltpu.VMEM((D_LRU, D_LRU), F32), pltpu.VMEM((D_LRU, D_LRU), F32),
                        pltpu.VMEM((MIX_VEC_ROWS, D_LRU), F32), pltpu.VMEM((8, D_LRU), F32),
                        pltpu.VMEM((CONV_HALO, D_LRU), F32), pltpu.VMEM((POOL_HALO, D_POOL), F32)] + _scatter_sems(ne_),
        compiler_params=_params("arbitrary"),
    )(dh1, proj, h, proj, h, *wts, *early)
    return outs[:6], outs[6:]


def _bwd_in(dproj, x, dh1, g1, w_in, tm):
    s_len = x.shape[0]
    n = s_len // tm
    cb = D_IN_PROJ // N_CHIPS

    def body(dp_ref, x_ref, dh1_ref, g1_ref, win_ref, dx_ref, dwin_ref, dg1_ref, a_w, a_g):
        i = pl.program_id(0)

        @pl.when(i == 0)
        def _():
            a_w[...] = jnp.zeros_like(a_w)
            a_g[...] = jnp.zeros_like(a_g)

        dp = dp_ref[...]
        xv = x_ref[...]
        zb = (xv * _rstd(xv) * g1_ref[...]).astype(BF16)
        dz = jnp.zeros((tm, D_MODEL), F32)
        for j in range(N_CHIPS):
            dpj = dp[:, cb * j:cb * (j + 1)]
            dz = dz + _dot_nt(dpj, win_ref[j])
            a_w[j] += _dot_tn(zb, dpj)
        dx, dg = _rms_bwd(xv, g1_ref[...], dz)
        dx_ref[...] = dh1_ref[...] + dx
        a_g[0:1, :] += dg

        @pl.when(i == n - 1)
        def _():
            dwin_ref[...] = a_w[...].astype(BF16)
            dg1_ref[...] = a_g[...]

    row = lambda w: pl.BlockSpec((tm, w), lambda i: (i, 0))
    return pl.pallas_call(
        body, name="bwd_in", grid=(n,),
        in_specs=[row(D_IN_PROJ), row(D_MODEL), row(D_MODEL), _resident(g1.shape), _resident(w_in.shape)],
        out_specs=[row(D_MODEL), _full(w_in.shape), _full((8, D_MODEL))],
        out_shape=[jax.ShapeDtypeStruct((s_len, D_MODEL), F32), jax.ShapeDtypeStruct(w_in.shape, BF16),
                   jax.ShapeDtypeStruct((8, D_MODEL), F32)],
        scratch_shapes=[pltpu.VMEM(w_in.shape, F32), pltpu.VMEM((8, D_MODEL), F32)],
        compiler_params=_params("arbitrary"),
    )(dproj, x, dh1, g1, w_in)


def _place():
    x, y, c = lax.axis_index("x"), lax.axis_index("y"), lax.axis_index("c")
    chips = [(1 - x, y), (x, 1 - y), (1 - x, 1 - y)]
    return x, y, c, chips


def _rcopy(src, dst, ssem, rsem, dev):
    return pltpu.make_async_remote_copy(src_ref=src, dst_ref=dst, send_sem=ssem, recv_sem=rsem,
                                        device_id=dev, device_id_type=MESH)


ANY = pl.BlockSpec(memory_space=pl.ANY)
COPY_CHUNK_BYTES = 128 * 1024
ROW_ALIGN = 16


def _row_chunks(rows, row_bytes):
    per = max(ROW_ALIGN, (COPY_CHUNK_BYTES // row_bytes) // ROW_ALIGN * ROW_ALIGN)
    return [(r0, min(per, rows - r0)) for r0 in range(0, rows, per)]


def _row_bytes(a):
    return a.shape[-1] * jnp.dtype(a.dtype).itemsize


def _stack_own(shards, dtypes, pos, steps, tag, late=()):
    nw = len(shards)
    nl = len(late)

    def body(pos_ref, *refs):
        outs = refs[nw + nl:2 * nw + nl]
        late_ref = refs[2 * nw + nl:2 * nw + 2 * nl]
        i = pl.program_id(0)
        if nl:
            ssem, rsem = refs[2 * nw + 2 * nl:]

            @pl.when(i == 0)
            def _():
                _gather_send(late_ref, late, ssem, rsem)

        for w in range(nw):
            outs[w][0] = refs[w][...].astype(dtypes[w])

        if nl:
            @pl.when(i == max(steps - PASS_STEPS, 0))
            def _():
                _gather_pass(late_ref, late, ssem, rsem)

            @pl.when(i == steps - 1)
            def _():
                _gather_done(late_ref, late, ssem, rsem)

    def split(s):
        return s.shape[0] % (steps * ROW_ALIGN) == 0

    ins = [pl.BlockSpec((s.shape[0] // steps, s.shape[1]), lambda i, p: (i, 0)) if split(s)
           else pl.BlockSpec(s.shape, lambda i, p: (0, 0)) for s in shards]
    outs = [pl.BlockSpec((1, s.shape[0] // steps, s.shape[1]), lambda i, p: (p[0], i, 0)) if split(s)
            else pl.BlockSpec((1,) + s.shape, lambda i, p: (p[0], 0, 0)) for s in shards]
    res = pl.pallas_call(
        body, name="stack_own_" + tag,
        grid_spec=pltpu.PrefetchScalarGridSpec(
            num_scalar_prefetch=1, grid=(steps,), in_specs=ins + [ANY] * nl, out_specs=outs + [ANY] * nl,
            scratch_shapes=_gather_sems(nl) if nl else []),
        out_shape=[jax.ShapeDtypeStruct((N_CHIPS,) + s.shape, d) for s, d in zip(shards, dtypes)]
        + [jax.ShapeDtypeStruct(a.shape, a.dtype) for a in late],
        input_output_aliases={1 + nw + k: nw + k for k in range(nl)},
        compiler_params=_params("arbitrary"),
    )(pos, *shards, *late)
    return res[:nw], res[nw:]


def _gather_send(outs, stacks, ssem, rsem):
    x, y, c, chips = _place()
    me = 2 * x + y
    for w, st in enumerate(stacks):
        half = st.shape[1] // 2
        for s, (px, py) in enumerate(chips):
            for r0, rs in _row_chunks(half, _row_bytes(st)):
                piece = outs[w].at[me, pl.ds(c * half + r0, rs)]
                _rcopy(piece, piece, ssem.at[w, s], rsem.at[w, s], (px, py, c)).start()


def _gather_pass(outs, stacks, ssem, rsem):
    x, y, c, chips = _place()
    sib = (x, y, 1 - c)
    for w, st in enumerate(stacks):
        half = st.shape[1] // 2
        for s, (px, py) in enumerate(chips):
            blk = outs[w].at[2 * px + py, pl.ds(c * half, half)]
            _rcopy(blk, blk, ssem.at[w, s], rsem.at[w, s], sib).wait_recv()
            for r0, rs in _row_chunks(half, _row_bytes(st)):
                piece = outs[w].at[2 * px + py, pl.ds(c * half + r0, rs)]
                _rcopy(piece, piece, ssem.at[w, 3 + s], rsem.at[w, 3 + s], sib).start()


def _gather_done(outs, stacks, ssem, rsem):
    x, y, c, chips = _place()
    sib = (x, y, 1 - c)
    for w, st in enumerate(stacks):
        half = st.shape[1] // 2
        for s, (px, py) in enumerate(chips):
            blk = outs[w].at[2 * px + py, pl.ds((1 - c) * half, half)]
            _rcopy(blk, blk, ssem.at[w, 3 + s], rsem.at[w, 3 + s], sib).wait_recv()
    for w, st in enumerate(stacks):
        half = st.shape[1] // 2
        blk = outs[w].at[0, pl.ds(0, half)]
        for s in range(6):
            _rcopy(blk, blk, ssem.at[w, s], rsem.at[w, s], sib).wait_send()


def _gather_sems(nw):
    return [pltpu.SemaphoreType.DMA((nw, 6)), pltpu.SemaphoreType.DMA((nw, 6))]


def _swap_halves(grads, tag):
    nw = len(grads)

    def body(*refs):
        ins, got = refs[:nw], refs[nw:2 * nw]
        ssem, rsem = refs[2 * nw:]
        x, y, c, _ = _place()
        cps = []
        for w in range(nw):
            hr = grads[w].shape[1] // 2
            for k in range(N_CHIPS):
                for r0, rs in _row_chunks(hr, _row_bytes(grads[w])):
                    _rcopy(ins[w].at[k, pl.ds((1 - c) * hr + r0, rs)], got[w].at[k, pl.ds(r0, rs)],
                           ssem.at[w], rsem.at[w], (x, y, 1 - c)).start()
            cps.append(_rcopy(got[w], got[w], ssem.at[w], rsem.at[w], (x, y, 1 - c)))
        for cp in cps:
            cp.wait()

    return pl.pallas_call(
        body, name="swap_halves_" + tag,
        in_specs=[ANY] * nw, out_specs=[ANY] * nw,
        out_shape=[jax.ShapeDtypeStruct((g.shape[0], g.shape[1] // 2, g.shape[2]), g.dtype) for g in grads],
        scratch_shapes=[pltpu.SemaphoreType.DMA((nw,)), pltpu.SemaphoreType.DMA((nw,))],
    )(*grads)


def _add_pairs(grads, got, pos, steps, tag):
    nw = len(grads)

    def body(pos_ref, *refs):
        for w in range(nw):
            refs[2 * nw + w][...] = (refs[w][...].astype(F32) + refs[nw + w][...].astype(F32)).astype(BF16)

    blk = lambda a: (a.shape[0], a.shape[1] // steps, a.shape[2])
    own = [pl.BlockSpec(blk(a), lambda i, p: (0, p[1] * steps + i, 0)) for a in got]
    rec = [pl.BlockSpec(blk(a), lambda i, p: (0, i, 0)) for a in got]
    return pl.pallas_call(
        body, name="add_pairs_" + tag,
        grid_spec=pltpu.PrefetchScalarGridSpec(num_scalar_prefetch=1, grid=(steps,), in_specs=own + rec, out_specs=rec),
        out_shape=[jax.ShapeDtypeStruct(a.shape, BF16) for a in got],
        compiler_params=_params("arbitrary"),
    )(pos, *grads, *got)


def _scatter_send(ins, got, parts, ssem, rsem):
    x, y, c, chips = _place()
    for w, p in enumerate(parts):
        for s, (px, py) in enumerate(chips):
            for r0, rs in _row_chunks(p.shape[1], _row_bytes(p)):
                _rcopy(ins[w].at[2 * px + py, pl.ds(r0, rs)], got[w].at[s, pl.ds(r0, rs)],
                       ssem.at[w, s], rsem.at[w, s], (px, py, c)).start()


def _scatter_done(got, parts, ssem, rsem):
    x, y, c, chips = _place()
    for w in range(len(parts)):
        for s, (px, py) in enumerate(chips):
            _rcopy(got[w].at[s], got[w].at[s], ssem.at[w, s], rsem.at[w, s], (px, py, c)).wait()


def _scatter_sems(nw):
    return [pltpu.SemaphoreType.DMA((nw, 3)), pltpu.SemaphoreType.DMA((nw, 3))]


def _scatter_shapes(parts):
    return [jax.ShapeDtypeStruct((3,) + p.shape[1:], p.dtype) for p in parts]


def _scatter_chips(parts):
    nw = len(parts)

    def body(*refs):
        ins, got = refs[:nw], refs[nw:2 * nw]
        ssem, rsem = refs[2 * nw:]
        _scatter_send(ins, got, parts, ssem, rsem)
        _scatter_done(got, parts, ssem, rsem)

    return pl.pallas_call(
        body, name="scatter_chips",
        in_specs=[ANY] * nw, out_specs=[ANY] * nw, out_shape=_scatter_shapes(parts),
        scratch_shapes=_scatter_sems(nw),
    )(*parts)


def _sum_chips(parts, got, pos, steps):
    nw = len(parts)

    def body(pos_ref, *refs):
        for w in range(nw):
            acc = refs[w][0].astype(F32)
            for s in range(3):
                acc = acc + refs[nw + w][s].astype(F32)
            refs[2 * nw + w][...] = acc

    own = [pl.BlockSpec((1, p.shape[1] // steps, p.shape[2]), lambda i, ps: (ps[0], i, 0)) for p in parts]
    rec = [pl.BlockSpec((3, p.shape[1] // steps, p.shape[2]), lambda i, ps: (0, i, 0)) for p in parts]
    outs = [pl.BlockSpec((p.shape[1] // steps, p.shape[2]), lambda i, ps: (ps[1] * steps + i, 0)) for p in parts]
    return pl.pallas_call(
        body, name="sum_chips",
        grid_spec=pltpu.PrefetchScalarGridSpec(num_scalar_prefetch=1, grid=(steps,), in_specs=own + rec, out_specs=outs),
        out_shape=[jax.ShapeDtypeStruct((2 * p.shape[1], p.shape[2]), F32) for p in parts],
        compiler_params=_params("arbitrary"),
    )(pos, *parts, *got)


def _join_halves(shards):
    nw = len(shards)

    def body(*refs):
        outs = refs[nw:2 * nw]
        ssem, rsem = refs[2 * nw:]
        x, y, c, _ = _place()
        cps = []
        for w in range(nw):
            hr = shards[w].shape[0] // 2
            for r0, rs in _row_chunks(hr, _row_bytes(shards[w])):
                piece = outs[w].at[pl.ds(c * hr + r0, rs)]
                _rcopy(piece, piece, ssem.at[w], rsem.at[w], (x, y, 1 - c)).start()
            mine = outs[w].at[pl.ds(c * hr, hr)]
            cps.append(_rcopy(mine, mine, ssem.at[w], rsem.at[w], (x, y, 1 - c)))
        for cp in cps:
            cp.wait()

    return pl.pallas_call(
        body, name="join_halves",
        in_specs=[ANY] * nw, out_specs=[ANY] * nw,
        out_shape=[jax.ShapeDtypeStruct(s.shape, F32) for s in shards],
        input_output_aliases={w: w for w in range(nw)},
        scratch_shapes=[pltpu.SemaphoreType.DMA((nw,)), pltpu.SemaphoreType.DMA((nw,))],
    )(*shards)


def _scatter_and_allreduce(parts, packed):
    nw = len(parts)
    rows = packed.shape[0]
    half = rows // 2

    def body(*refs):
        ins, p_ref = refs[:nw], refs[nw]
        got, out_ref = refs[nw + 1:2 * nw + 1], refs[2 * nw + 1]
        rfull, rhalf, ssem, rsem, bsem_s, bsem_r = refs[2 * nw + 2:]
        x, y, c, _ = _place()
        sib = (x, y, 1 - c)
        _scatter_send(ins, got, parts, ssem, rsem)
        out_ref[...] = p_ref[...]
        cp = _rcopy(out_ref, rfull, bsem_s.at[0], bsem_r.at[0], sib)
        cp.start()
        cp.wait()
        out_ref[...] = out_ref[...] + rfull[...]
        mine = pl.ds(pl.multiple_of(c * half, 8), half)
        other = pl.ds(pl.multiple_of((1 - c) * half, 8), half)
        for st, peer in enumerate([(1 - x, y, c), (x, 1 - y, c)]):
            cp = _rcopy(out_ref.at[mine], rhalf.at[st], bsem_s.at[1 + st], bsem_r.at[1 + st], peer)
            cp.start()
            cp.wait()
            out_ref[mine, :] = out_ref[mine, :] + rhalf[st]
        cp = _rcopy(out_ref.at[mine], rhalf.at[2], bsem_s.at[3], bsem_r.at[3], sib)
        cp.start()
        cp.wait()
        out_ref[other, :] = rhalf[2]
        _scatter_done(got, parts, ssem, rsem)

    vm = pl.BlockSpec(memory_space=pltpu.VMEM)
    outs = pl.pallas_call(
        body, name="scatter_and_allreduce",
        in_specs=[ANY] * nw + [vm], out_specs=[ANY] * nw + [vm],
        out_shape=_scatter_shapes(parts) + [jax.ShapeDtypeStruct(packed.shape, F32)],
        scratch_shapes=[pltpu.VMEM(packed.shape, F32), pltpu.VMEM((3, half, packed.shape[1]), F32)] + _scatter_sems(nw)
        + [pltpu.SemaphoreType.DMA((4,)), pltpu.SemaphoreType.DMA((4,))],
        compiler_params=pltpu.CompilerParams(vmem_limit_bytes=VMEM_LIMIT),
    )(*parts, packed)
    return outs[:nw], outs[nw]


def _adamw_math(w, g, m, v):
    m = ADAM_B1 * m + (1.0 - ADAM_B1) * g
    v = ADAM_B2 * v + (1.0 - ADAM_B2) * (g * g)
    delta = -ADAM_LR * ((m * ADAM_C1) / (jnp.sqrt(v * ADAM_C2) + ADAM_EPS) + ADAM_WD * w)
    return delta, m, v


def _adamw(ws, gs, ms, vs, steps, name):
    nw = len(ws)

    def body(*refs):
        for k in range(nw):
            g = refs[nw + k][...]
            d, m, v = _adamw_math(refs[k][...], g, refs[2 * nw + k][...], refs[3 * nw + k][...])
            refs[4 * nw + k][...] = d
            refs[5 * nw + k][...] = m
            refs[6 * nw + k][...] = v
            refs[7 * nw + k][...] = g

    specs = [pl.BlockSpec((a.shape[0] // steps, a.shape[1]), lambda i: (i, 0)) for a in ws]
    shapes = [jax.ShapeDtypeStruct(a.shape, F32) for a in ws]
    outs = pl.pallas_call(
        body, name=name, grid=(steps,),
        in_specs=specs * 4, out_specs=specs * 4, out_shape=shapes * 4,
        compiler_params=_params("arbitrary"),
    )(*ws, *gs, *ms, *vs)
    return outs[:nw], outs[nw:2 * nw], outs[2 * nw:3 * nw], outs[3 * nw:]


SMALL = ["norm_mix_g", "pool_w", "pool_b", "pool_scale", "conv_b", "gate_a_w", "gate_a_b", "gate_x_w", "gate_x_b",
         "lru_L", "norm_mlp_g", "norm_ple_g", "b_ple_gate", "norm_final_g"]
BIG = ["w_in", "w_out", "w_up", "w_down", "w_ple_gate", "w_ple_proj"]
ORDER = ["norm_mix_g", "w_in", "pool_w", "pool_b", "pool_scale", "conv_w", "conv_b", "gate_a_w", "gate_a_b", "gate_x_w",
         "gate_x_b", "lru_L", "w_out", "norm_mlp_g", "w_up", "w_down", "norm_ple_g", "w_ple_gate", "b_ple_gate",
         "w_ple_proj", "norm_final_g"]
LANES = 128


def _block_diag(w):
    eye = jnp.eye(LRU_HEADS, dtype=w.dtype)
    return jnp.einsum("hij,hk->hikj", w, eye).reshape(D_LRU, D_LRU)


def _diag_blocks(full):
    f = full.reshape(LRU_HEADS, LRU_BLOCK, LRU_HEADS, LRU_BLOCK)
    return jnp.stack([f[h, :, h, :] for h in range(LRU_HEADS)])


def _rows128(a):
    return a.reshape(-1, LANES)


def _pad8(a):
    r = (-a.shape[0]) % 8
    return jnp.pad(a, ((0, r), (0, 0))) if r else a


def kernel(x, p, norm_mix_g, w_in, pool_w, pool_b, pool_scale, conv_w, conv_b, gate_a_w, gate_a_b, gate_x_w, gate_x_b, lru_L, w_out, norm_mlp_g, w_up, w_down, norm_ple_g, w_ple_gate, b_ple_gate, w_ple_proj, norm_final_g, loss_target, m_norm_mix_g, m_w_in, m_pool_w, m_pool_b, m_pool_scale, m_conv_w, m_conv_b, m_gate_a_w, m_gate_a_b, m_gate_x_w, m_gate_x_b, m_lru_L, m_w_out, m_norm_mlp_g, m_w_up, m_w_down, m_norm_ple_g, m_w_ple_gate, m_b_ple_gate, m_w_ple_proj, m_norm_final_g, v_norm_mix_g, v_w_in, v_pool_w, v_pool_b, v_pool_scale, v_conv_w, v_conv_b, v_gate_a_w, v_gate_a_b, v_gate_x_w, v_gate_x_b, v_lru_L, v_w_out, v_norm_mlp_g, v_w_up, v_w_down, v_norm_ple_g, v_w_ple_gate, v_b_ple_gate, v_w_ple_proj, v_norm_final_g):
    W = dict(norm_mix_g=norm_mix_g, w_in=w_in, pool_w=pool_w, pool_b=pool_b, pool_scale=pool_scale, conv_w=conv_w,
             conv_b=conv_b, gate_a_w=gate_a_w, gate_a_b=gate_a_b, gate_x_w=gate_x_w, gate_x_b=gate_x_b, lru_L=lru_L,
             w_out=w_out, norm_mlp_g=norm_mlp_g, w_up=w_up, w_down=w_down, norm_ple_g=norm_ple_g,
             w_ple_gate=w_ple_gate, b_ple_gate=b_ple_gate, w_ple_proj=w_ple_proj, norm_final_g=norm_final_g)
    M = dict(norm_mix_g=m_norm_mix_g, w_in=m_w_in, pool_w=m_pool_w, pool_b=m_pool_b, pool_scale=m_pool_scale,
             conv_w=m_conv_w, conv_b=m_conv_b, gate_a_w=m_gate_a_w, gate_a_b=m_gate_a_b, gate_x_w=m_gate_x_w,
             gate_x_b=m_gate_x_b, lru_L=m_lru_L, w_out=m_w_out, norm_mlp_g=m_norm_mlp_g, w_up=m_w_up, w_down=m_w_down,
             norm_ple_g=m_norm_ple_g, w_ple_gate=m_w_ple_gate, b_ple_gate=m_b_ple_gate, w_ple_proj=m_w_ple_proj,
             norm_final_g=m_norm_final_g)
    V = dict(norm_mix_g=v_norm_mix_g, w_in=v_w_in, pool_w=v_pool_w, pool_b=v_pool_b, pool_scale=v_pool_scale,
             conv_w=v_conv_w, conv_b=v_conv_b, gate_a_w=v_gate_a_w, gate_a_b=v_gate_a_b, gate_x_w=v_gate_x_w,
             gate_x_b=v_gate_x_b, lru_L=v_lru_L, w_out=v_w_out, norm_mlp_g=v_norm_mlp_g, w_up=v_w_up, w_down=v_w_down,
             norm_ple_g=v_norm_ple_g, w_ple_gate=v_w_ple_gate, b_ple_gate=v_b_ple_gate, w_ple_proj=v_w_ple_proj,
             norm_final_g=v_norm_final_g)

    s_len = x.shape[1]
    sub_mix = min(256, s_len)
    tm = min(512, s_len)
    chip = (2 * lax.axis_index("x") + lax.axis_index("y")).astype(jnp.int32)
    pos = jnp.stack([chip, lax.axis_index("c").astype(jnp.int32)])

    shards = [w_in[0], w_out[0], w_up[0], w_down[0], w_ple_gate[0], w_ple_proj[0], jnp.pad(conv_w[0], ((0, 12), (0, 0)))]
    first, _ = _stack_own([shards[0], shards[1], shards[6]], [BF16, BF16, F32], pos, 8, "first")
    (st_up, st_dn, st_pg, st_pp), (win_g, wout_g, cw_g) = _stack_own(shards[2:6], [BF16] * 4, pos, 8, "rest", first)
    wout_f = wout_g.reshape(D_MODEL, D_MODEL)
    cw_f = jnp.transpose(cw_g[:, :CONV_WIDTH], (1, 0, 2)).reshape(CONV_WIDTH, D_LRU)
    pw_b = pool_w[0].astype(BF16)
    wa_b = _block_diag(gate_a_w[0]).astype(BF16)
    wx_b = _block_diag(gate_x_w[0]).astype(BF16)
    pb_r = pool_b.reshape(1, D_POOL)
    ba_r = gate_a_b.reshape(1, D_LRU)
    bx_r = gate_x_b.reshape(1, D_LRU)
    g4 = norm_final_g.reshape(1, D_MODEL)
    mix_w = (pw_b, pb_r, pool_scale, cw_f, conv_b, wa_b, ba_r, wx_b, bx_r, lru_L, wout_f)

    xs, ps, ts = x[0], p[0, 0], loss_target[0]
    (proj, hst, h1), (wup_g, wdn_g) = _fwd_mix(xs, norm_mix_g, win_g, *mix_w, tm, sub_mix, [st_up, st_dn])
    (z2, ru, h2), (wpg_g, wpp_g) = _fwd_mlp(h1, norm_mlp_g, wup_g, wdn_g, tm, [st_pg, st_pp])
    wpg_f = wpg_g.reshape(D_MODEL, D_MODEL)
    wpp_f = jnp.transpose(wpp_g, (1, 0, 2)).reshape(PLE_DIM, D_MODEL)
    dh2, dh2b, d_wpg, d_wpp, head_vec = _head(h2, ps, ts, norm_ple_g, wpg_f, b_ple_gate, wpp_f, g4, tm)
    dup, dh1, mlp_vec = _bwd_mlp_x(dh2, ru, h1, norm_mlp_g, wup_g, wdn_g, tm)
    d_wup, d_wdn = _bwd_mlp_w(z2, dup, ru, dh2b, tm)
    early = [d_wup, d_wdn, d_wpg.reshape(N_CHIPS, D_MODEL // N_CHIPS, D_MODEL), d_wpp]
    pair_e = _add_pairs(early, _swap_halves(early, "early"), pos, 8, "early")
    (dproj, d_wout, d_pw, d_wa, d_wx, mix_vec), got_e = _bwd_mix(dh1, proj, hst, *mix_w, tm, sub_mix, pair_e)

    dx, d_win, in_vec = _bwd_in(dproj, xs, dh1, norm_mix_g, win_g, tm)

    last = [d_win, d_wout.reshape(N_CHIPS, D_MODEL // N_CHIPS, D_MODEL)]
    pair_l = _add_pairs(last, _swap_halves(last, "last"), pos, 8, "last")

    g_small = {
        "norm_mix_g": in_vec[0:1], "pool_w": d_pw, "pool_b": mix_vec[0:1], "pool_scale": mix_vec[1:2],
        "conv_b": mix_vec[2:3], "gate_a_w": _diag_blocks(d_wa), "gate_a_b": mix_vec[3:4],
        "gate_x_w": _diag_blocks(d_wx), "gate_x_b": mix_vec[4:5], "lru_L": mix_vec[5:6], "norm_mlp_g": mlp_vec[0:1],
        "norm_ple_g": head_vec[0:1], "b_ple_gate": head_vec[2:3], "norm_final_g": head_vec[1:2],
    }
    d_cw = jnp.transpose(mix_vec[8:8 + CONV_WIDTH].reshape(CONV_WIDTH, N_CHIPS, LANES), (1, 0, 2)).reshape(-1, LANES)
    pieces = [_pad8(_rows128(g_small[k])) for k in SMALL] + [d_cw, _pad8(head_vec[3:4, :LANES])]
    offs = [0]
    for pc in pieces:
        offs.append(offs[-1] + pc.shape[0])
    if offs[-1] % 16:
        pieces.append(jnp.zeros((8, LANES), F32))
    got_l, red = _scatter_and_allreduce(pair_l, jnp.concatenate(pieces, axis=0))
    g_big = _join_halves(_sum_chips(pair_l + pair_e, got_l + got_e, pos, 8))
    loss = red[offs[-2], 0]
    g_cw = lax.dynamic_slice(red, (offs[len(SMALL)] + CONV_WIDTH * chip, 0), (CONV_WIDTH, LANES))

    def packed(src):
        return jnp.concatenate([_pad8(_rows128(src[k])) for k in SMALL] + [_pad8(src["conv_w"][0])], axis=0)

    n_small = offs[len(SMALL)]
    g_pack = jnp.concatenate([red[:n_small], _pad8(g_cw)], axis=0)
    (d_pack,), (m_pack,), (v_pack,), _ = _adamw([packed(W)], [g_pack], [packed(M)], [packed(V)], 1, "adamw_small")

    big2d = lambda src: [src[k][0] for k in BIG]
    d_big, m_big, v_big, g_big = _adamw(big2d(W), g_big, big2d(M), big2d(V), 8, "adamw_big")

    def unpack(pack, big_list):
        out = {}
        for idx, k in enumerate(SMALL):
            n_el = W[k].size
            out[k] = pack[offs[idx]:offs[idx + 1]].reshape(-1)[:n_el].reshape(W[k].shape)
        out["conv_w"] = pack[n_small:n_small + CONV_WIDTH].reshape(W["conv_w"].shape)
        for k, a in zip(BIG, big_list):
            out[k] = a.reshape(W[k].shape)
        return out

    grads = unpack(g_pack, g_big)
    deltas = unpack(d_pack, d_big)
    new_m = unpack(m_pack, m_big)
    new_v = unpack(v_pack, v_big)
    return (loss, dx[None], *[grads[k] for k in ORDER], *[deltas[k] for k in ORDER],
            *[new_m[k] for k in ORDER], *[new_v[k] for k in ORDER])
```

```python
import functools

import jax
import jax.numpy as jnp
from jax import lax
from jax.experimental import pallas as pl
from jax.experimental.pallas import tpu as pltpu

F32 = jnp.float32
BF16 = jnp.bfloat16
MESH = pl.DeviceIdType.MESH

D_MODEL = 1024
D_POOL = 512
D_LRU = 512
POOL_WINDOWS = (2, 4, 8, 16)
POOL_GROUP = 128
POOL_HALO = 16
CONV_WIDTH = 4
CONV_HALO = 8
PASS_STEPS = 2
LRU_HEADS = 8
LRU_BLOCK = 64
GATE_PAIRS = 4
LRU_C = 8.0
LRU_UNROLL = 4
D_FF = 4096
PLE_DIM = 256
D_IN_PROJ = 1536
RMS_EPS = 1e-6
N_CHIPS = 4
FF_BLOCK = D_FF // N_CHIPS
FF_PAIR = 2

ADAM_LR = 0.001
ADAM_B1 = 0.9
ADAM_B2 = 0.999
ADAM_EPS = 1e-08
ADAM_WD = 0.01
ADAM_STEP = 10
ADAM_C1 = 1.0 / (1.0 - ADAM_B1 ** ADAM_STEP)
ADAM_C2 = 1.0 / (1.0 - ADAM_B2 ** ADAM_STEP)

VMEM_LIMIT = 56 * 1024 * 1024
GELU_C = 0.7978845608028654
GELU_A = 0.044715

NT = (((1,), (1,)), ((), ()))
TN = (((0,), (0,)), ((), ()))


def _dot(a, b):
    return jnp.dot(a, b, preferred_element_type=F32)


def _dot_nt(a, b):
    return lax.dot_general(a, b, NT, preferred_element_type=F32)


def _dot_tn(a, b):
    return lax.dot_general(a, b, TN, preferred_element_type=F32)


def _params(*sem):
    return pltpu.CompilerParams(dimension_semantics=sem, vmem_limit_bytes=VMEM_LIMIT)


def _full(shape):
    nd = len(shape)
    return pl.BlockSpec(shape, lambda *_: (0,) * nd)


def _resident(shape):
    nd = len(shape)
    return pl.BlockSpec(shape, lambda *_: (0,) * nd, pipeline_mode=pl.Buffered(1))


def _rstd(x):
    return lax.rsqrt(jnp.mean(x * x, axis=-1, keepdims=True) + RMS_EPS)


def _rms_bwd(x, g, dz):
    xr = x * _rstd(x)
    r = _rstd(x)
    dyg = dz * g
    dx = r * (dyg - xr * jnp.mean(dyg * xr, axis=-1, keepdims=True))
    dg = jnp.sum(dz * xr, axis=0, keepdims=True)
    return dx, dg


def _sigmoid(x):
    return 1.0 / (1.0 + jnp.exp(-x))


def _log_sigmoid(v):
    u = jnp.exp(-jnp.abs(v))
    w = 1.0 + u
    l1p = jnp.where(w == 1.0, u, jnp.log(w) * u / jnp.where(w == 1.0, 1.0, w - 1.0))
    return jnp.minimum(v, 0.0) - l1p


def _gelu(x):
    t = jnp.tanh(GELU_C * (x + GELU_A * x * x * x))
    return 0.5 * x * (1.0 + t), t


def _gelu_grad(x, t):
    return 0.5 * (1.0 + t) + 0.5 * x * (1.0 - t * t) * GELU_C * (1.0 + 3.0 * GELU_A * x * x)


def _rows(shape, t0):
    return lax.broadcasted_iota(jnp.int32, shape, 0) + t0


def _pool_diff(u_pool, prev, t0):
    tm = u_pool.shape[0]
    rows = _rows((tm, POOL_GROUP), t0)
    outs, invs = [], []
    for g, w in enumerate(POOL_WINDOWS):
        sl = slice(POOL_GROUP * g, POOL_GROUP * (g + 1))
        ug = u_pool[:, sl]
        s = jnp.concatenate([prev[:, sl], ug], axis=0)
        k = 1
        while k < w:
            s = s + pltpu.roll(s, k, 0)
            k *= 2
        inv = 1.0 / jnp.minimum(rows + 1, w).astype(F32)
        outs.append(s[POOL_HALO:] * inv - ug)
        invs.append(inv)
    return jnp.concatenate(outs, axis=1), jnp.concatenate(invs, axis=1)


def _pool_diff_bwd(dd, inv, nxt):
    ddc = dd * inv
    outs = []
    for g, w in enumerate(POOL_WINDOWS):
        sl = slice(POOL_GROUP * g, POOL_GROUP * (g + 1))
        s = jnp.concatenate([ddc[:, sl], nxt[:, sl]], axis=0)
        n = s.shape[0]
        k = 1
        while k < w:
            s = s + pltpu.roll(s, n - k, 0)
            k *= 2
        outs.append(s[:n - POOL_HALO] - dd[:, sl])
    return jnp.concatenate(outs, axis=1), ddc


def _conv_taps(u, prev):
    ext = jnp.concatenate([prev, u], axis=0)
    return [pltpu.roll(ext, CONV_WIDTH - 1 - k, 0)[CONV_HALO:] if k < CONV_WIDTH - 1 else u for k in range(CONV_WIDTH)]


def _scan_rev(a, b, carry):
    tm = a.shape[0]
    r8 = _rows(a.shape, 0) & 7
    k = 1
    while k < 8:
        ar = pltpu.roll(a, tm - k, 0)
        br = pltpu.roll(b, tm - k, 0)
        m = r8 < 8 - k
        b = jnp.where(m, a * br + b, b)
        a = jnp.where(m, a * ar, a)
        k *= 2
    outs = []
    c = jnp.broadcast_to(carry, (8, a.shape[1]))
    for g in reversed(range(tm // 8)):
        gg = b[8 * g:8 * g + 8] + a[8 * g:8 * g + 8] * c
        outs.insert(0, gg)
        c = jnp.broadcast_to(gg[0:1], c.shape)
    return jnp.concatenate(outs, axis=0)


def _gate_dot(xbb, w_ref):
    return jnp.concatenate(
        [_dot(xbb[:, POOL_GROUP * q:POOL_GROUP * (q + 1)], w_ref[q]) for q in range(GATE_PAIRS)], axis=1)


def _gate_dot_nt(db, w_ref):
    return jnp.concatenate(
        [_dot_nt(db[:, POOL_GROUP * q:POOL_GROUP * (q + 1)], w_ref[q]) for q in range(GATE_PAIRS)], axis=1)


def _lru_gates(xb, wa_ref, ba, wx_ref, bx, lsl8, t0):
    xbb = xb.astype(BF16)
    r = _sigmoid(_gate_dot(xbb, wa_ref) + ba)
    ig = _sigmoid(_gate_dot(xbb, wx_ref) + bx)
    a = jnp.exp(r * lsl8)
    first = _rows(xb.shape, t0) == 0
    m2 = 1.0 - a * a
    rs = lax.rsqrt(jnp.maximum(m2, jnp.finfo(F32).tiny))
    mult = jnp.where(first, 1.0, m2 * rs)
    return r, ig, a, mult, rs, first


def _lru_chunks_fwd(r_ref, i_ref, xb_ref, ug_ref, h_ref, y_ref, row0, n_rows, lsl8, t0, hc):
    lsl = jnp.broadcast_to(lsl8, (8, D_LRU))
    sub8 = lax.broadcasted_iota(jnp.int32, (8, D_LRU), 0)

    def chunk(j, hc):
        o = j * 8
        rows = pl.ds(o, 8)
        r = _sigmoid(r_ref[rows, :])
        ig = _sigmoid(i_ref[rows, :])
        a = jnp.exp(r * lsl)
        m2 = 1.0 - a * a
        mult = jnp.where(sub8 + (t0 + o) == 0, 1.0, m2 * lax.rsqrt(jnp.maximum(m2, jnp.finfo(F32).tiny)))
        b = mult * (ig * xb_ref[rows, :])
        k = 1
        while k < 8:
            m = sub8 >= k
            b = jnp.where(m, a * pltpu.roll(b, k, 0) + b, b)
            a = jnp.where(m, a * pltpu.roll(a, k, 0), a)
            k *= 2
        h = b + a * hc
        h_ref[pl.ds(row0 + o, 8), :] = h
        gl, _ = _gelu(ug_ref[rows, :])
        y_ref[rows, :] = h * gl
        return jnp.broadcast_to(h[7:8, :], (8, D_LRU))

    for j in range(n_rows // 8):
        hc = chunk(j, hc)
    return hc


def _fwd_mix(x, g1, w_in, pool_w, pool_b, pool_scale, conv_w, conv_b, wa, ba, wx, bx, lru_l, w_out, tm, sub, late):
    s_len = x.shape[0]
    n = s_len // tm
    nl = len(late)

    def body(x_ref, g1_ref, win_ref, pw_ref, pb_ref, ps_ref, cw_ref, cb_ref, wa_ref, ba_ref, wx_ref, bx_ref, l_ref,
             wout_ref, *rest):
        proj_ref, h_ref, h1_ref = rest[nl:nl + 3]
        late_ref = rest[nl + 3:2 * nl + 3]
        cpool, clru, ch, s_r, s_i, s_xb, s_ug, s_y, ssem, rsem = rest[2 * nl + 3:]
        i = pl.program_id(0)

        @pl.when(i == 0)
        def _():
            _gather_send(late_ref, late, ssem, rsem)
            cpool[...] = jnp.zeros_like(cpool)
            clru[...] = jnp.zeros_like(clru)
            ch[...] = jnp.zeros_like(ch)

        lsl8 = LRU_C * _log_sigmoid(l_ref[...])
        cp, cl, hc = cpool[...], clru[...], ch[...]

        def in_proj(k):
            rs = slice(k * sub, (k + 1) * sub)
            xv = x_ref[rs, :]
            zb = (xv * _rstd(xv) * g1_ref[...]).astype(BF16)
            proj = jnp.concatenate([_dot(zb, win_ref[j]) for j in range(N_CHIPS)], axis=1)
            proj_ref[rs, :] = proj.astype(BF16)
            return xv, proj

        nxt = in_proj(0)
        for k in range(tm // sub):
            rs = slice(k * sub, (k + 1) * sub)
            t0 = i * tm + k * sub
            xv, proj = nxt
            if k + 1 < tm // sub:
                nxt = in_proj(k + 1)
            u_pool = proj[:, :D_POOL]
            u_lru = proj[:, D_POOL:D_POOL + D_LRU]
            u_gate = proj[:, D_POOL + D_LRU:]

            d, _ = _pool_diff(u_pool, cp, t0)
            cp = u_pool[sub - POOL_HALO:]
            db = d.astype(BF16)
            yp = jnp.concatenate(
                [_dot(db[:, POOL_GROUP * g:POOL_GROUP * (g + 1)], pw_ref[g]) for g in range(len(POOL_WINDOWS))], axis=1)
            y_pool = (yp + pb_ref[...]) * ps_ref[...]

            taps = _conv_taps(u_lru, cl)
            cl = u_lru[sub - CONV_HALO:]
            xb = cb_ref[...]
            for q in range(CONV_WIDTH):
                xb = xb + taps[q] * cw_ref[q:q + 1, :]
            xbb = xb.astype(BF16)
            s_r[...] = _gate_dot(xbb, wa_ref) + ba_ref[...]
            s_i[...] = _gate_dot(xbb, wx_ref) + bx_ref[...]
            s_xb[...] = xb
            s_ug[...] = u_gate
            hc = _lru_chunks_fwd(s_r, s_i, s_xb, s_ug, h_ref, s_y, k * sub, sub, lsl8, t0, hc)
            cat = jnp.concatenate([y_pool, s_y[...]], axis=1).astype(BF16)
            h1_ref[rs, :] = xv + _dot(cat, wout_ref[...])
        cpool[...] = cp
        clru[...] = cl
        ch[...] = hc

        @pl.when(i == max(n - PASS_STEPS, 0))
        def _():
            _gather_pass(late_ref, late, ssem, rsem)

        @pl.when(i == n - 1)
        def _():
            _gather_done(late_ref, late, ssem, rsem)

    row = lambda w: pl.BlockSpec((tm, w), lambda i: (i, 0))
    ins = [x, g1, w_in, pool_w, pool_b, pool_scale, conv_w, conv_b, wa, ba, wx, bx, lru_l, w_out]
    outs = pl.pallas_call(
        body, name="fwd_mix", grid=(n,),
        in_specs=[row(D_MODEL)] + [_resident(a.shape) for a in ins[1:]] + [ANY] * nl,
        out_specs=[row(D_IN_PROJ), row(D_LRU), row(D_MODEL)] + [ANY] * nl,
        out_shape=[jax.ShapeDtypeStruct((s_len, D_IN_PROJ), BF16), jax.ShapeDtypeStruct((s_len, D_LRU), F32),
                   jax.ShapeDtypeStruct((s_len, D_MODEL), F32)]
        + [jax.ShapeDtypeStruct(a.shape, a.dtype) for a in late],
        input_output_aliases={len(ins) + k: 3 + k for k in range(nl)},
        scratch_shapes=[pltpu.VMEM((POOL_HALO, D_POOL), F32), pltpu.VMEM((CONV_HALO, D_LRU), F32),
                        pltpu.VMEM((8, D_LRU), F32)] + [pltpu.VMEM((sub, D_LRU), F32)] * 5 + _gather_sems(nl),
        compiler_params=_params("arbitrary"),
    )(*ins, *late)
    return outs[:3], outs[3:]


def _fwd_mlp(h1, g2, w_up, w_down, tm, late):
    s_len = h1.shape[0]
    n = s_len // tm
    nl = len(late)

    def body(h1_ref, g2_ref, wup_ref, wdn_ref, *rest):
        z2_ref, ru_ref, h2_ref = rest[nl:nl + 3]
        late_ref = rest[nl + 3:2 * nl + 3]
        ssem, rsem = rest[2 * nl + 3:]
        i = pl.program_id(0)

        @pl.when(i == 0)
        def _():
            _gather_send(late_ref, late, ssem, rsem)

        hv = h1_ref[...]
        zb = (hv * _rstd(hv) * g2_ref[...]).astype(BF16)
        z2_ref[...] = zb
        acc = hv
        for j in range(N_CHIPS):
            ru = jnp.maximum(_dot(zb, wup_ref[j]), 0.0)
            ru_ref[:, FF_BLOCK * j:FF_BLOCK * (j + 1)] = ru.astype(BF16)
            acc = acc + _dot((ru * ru).astype(BF16), wdn_ref[j])
        h2_ref[...] = acc

        @pl.when(i == max(n - PASS_STEPS, 0))
        def _():
            _gather_pass(late_ref, late, ssem, rsem)

        @pl.when(i == n - 1)
        def _():
            _gather_done(late_ref, late, ssem, rsem)

    row = lambda w: pl.BlockSpec((tm, w), lambda i: (i, 0))
    outs = pl.pallas_call(
        body, name="fwd_mlp", grid=(n,),
        in_specs=[row(D_MODEL), _full(g2.shape), _resident(w_up.shape), _resident(w_down.shape)] + [ANY] * nl,
        out_specs=[row(D_MODEL), row(D_FF), row(D_MODEL)] + [ANY] * nl,
        out_shape=[jax.ShapeDtypeStruct((s_len, D_MODEL), BF16), jax.ShapeDtypeStruct((s_len, D_FF), BF16),
                   jax.ShapeDtypeStruct((s_len, D_MODEL), F32)] + [jax.ShapeDtypeStruct(a.shape, a.dtype) for a in late],
        input_output_aliases={4 + k: 3 + k for k in range(nl)},
        scratch_shapes=_gather_sems(nl),
        compiler_params=_params("arbitrary"),
    )(h1, g2, w_up, w_down, *late)
    return outs[:3], outs[3:]


def _head(h2, p, target, g3, w_pg, b_pg, w_pp, g4, tm):
    s_len = h2.shape[0]
    n = s_len // tm

    def body(h2_ref, p_ref, t_ref, g3_ref, wpg_ref, bpg_ref, wpp_ref, g4_ref,
             dh2_ref, dh2b_ref, dwpg_ref, dwpp_ref, vec_ref, a_pg, a_pp, a_vec):
        i = pl.program_id(0)

        @pl.when(i == 0)
        def _():
            a_pg[...] = jnp.zeros_like(a_pg)
            a_pp[...] = jnp.zeros_like(a_pp)
            a_vec[...] = jnp.zeros_like(a_vec)

        h2v = h2_ref[...]
        g3v = g3_ref[...]
        g4v = g4_ref[...]
        z3 = (h2v * _rstd(h2v) * g3v).astype(BF16)
        gate = _sigmoid(_dot(z3, wpg_ref[...]) + bpg_ref[...])
        pb = p_ref[...].astype(BF16)
        pp = _dot(pb, wpp_ref[...])
        h3 = h2v + gate * pp
        r4 = _rstd(h3)
        diff = h3 * r4 * g4v - t_ref[...]
        loss = 0.5 * jnp.sum(jnp.mean(diff * diff, axis=-1, keepdims=True), axis=0, keepdims=True)
        dy = diff * (1.0 / D_MODEL)
        dh3, dg4 = _rms_bwd(h3, g4v, dy)
        dpp = (dh3 * gate).astype(BF16)
        dpre = dh3 * pp * gate * (1.0 - gate)
        dpreb = dpre.astype(BF16)
        dz3 = _dot_nt(dpreb, wpg_ref[...])
        dx, dg3 = _rms_bwd(h2v, g3v, dz3)
        dh2 = dh3 + dx
        dh2_ref[...] = dh2
        dh2b_ref[...] = dh2.astype(BF16)
        a_pg[...] += _dot_tn(z3, dpreb)
        a_pp[...] += _dot_tn(pb, dpp)
        a_vec[0:1, :] += dg3
        a_vec[1:2, :] += dg4
        a_vec[2:3, :] += jnp.sum(dpre, axis=0, keepdims=True)
        a_vec[3:4, :] += jnp.broadcast_to(loss, (1, D_MODEL))

        @pl.when(i == n - 1)
        def _():
            dwpg_ref[...] = a_pg[...].astype(BF16)
            for j in range(N_CHIPS):
                dwpp_ref[j] = a_pp[:, PLE_DIM * j:PLE_DIM * (j + 1)].astype(BF16)
            vec_ref[...] = a_vec[...]

    row = lambda w: pl.BlockSpec((tm, w), lambda i: (i, 0))
    ins = [h2, p, target, g3, w_pg, b_pg, w_pp, g4]
    return pl.pallas_call(
        body, name="head", grid=(n,),
        in_specs=[row(D_MODEL), row(PLE_DIM), row(D_MODEL)] + [_resident(a.shape) for a in ins[3:]],
        out_specs=[row(D_MODEL), row(D_MODEL), _full((D_MODEL, D_MODEL)), _full((N_CHIPS, PLE_DIM, PLE_DIM)),
                   _full((8, D_MODEL))],
        out_shape=[jax.ShapeDtypeStruct((s_len, D_MODEL), F32), jax.ShapeDtypeStruct((s_len, D_MODEL), BF16),
                   jax.ShapeDtypeStruct((D_MODEL, D_MODEL), BF16),
                   jax.ShapeDtypeStruct((N_CHIPS, PLE_DIM, PLE_DIM), BF16), jax.ShapeDtypeStruct((8, D_MODEL), F32)],
        scratch_shapes=[pltpu.VMEM((D_MODEL, D_MODEL), F32), pltpu.VMEM((PLE_DIM, D_MODEL), F32),
                        pltpu.VMEM((8, D_MODEL), F32)],
        compiler_params=_params("arbitrary"),
    )(*ins)


def _bwd_mlp_x(dh2, ru, h1, g2, w_up, w_down, tm):
    s_len = dh2.shape[0]
    n = s_len // tm

    def body(dh2_ref, ru_ref, h1_ref, g2_ref, wup_ref, wdn_ref, dup_ref, dh1_ref, dg2_ref, a_g):
        i = pl.program_id(0)

        @pl.when(i == 0)
        def _():
            a_g[...] = jnp.zeros_like(a_g)

        dh2v = dh2_ref[...]
        dhb = dh2v.astype(BF16)
        acc = jnp.zeros((tm, D_MODEL), F32)
        for j in range(N_CHIPS):
            sl = slice(FF_BLOCK * j, FF_BLOCK * (j + 1))
            dup = (_dot_nt(dhb, wdn_ref[j]) * (2.0 * ru_ref[:, sl].astype(F32))).astype(BF16)
            dup_ref[:, sl] = dup
            acc = acc + _dot_nt(dup, wup_ref[j])
        dx, dg = _rms_bwd(h1_ref[...], g2_ref[...], acc)
        dh1_ref[...] = dh2v + dx
        a_g[0:1, :] += dg

        @pl.when(i == n - 1)
        def _():
            dg2_ref[...] = a_g[...]

    row = lambda w: pl.BlockSpec((tm, w), lambda i: (i, 0))
    return pl.pallas_call(
        body, name="bwd_mlp_x", grid=(n,),
        in_specs=[row(D_MODEL), row(D_FF), row(D_MODEL), _full(g2.shape), _resident(w_up.shape), _resident(w_down.shape)],
        out_specs=[row(D_FF), row(D_MODEL), _full((8, D_MODEL))],
        out_shape=[jax.ShapeDtypeStruct((s_len, D_FF), BF16), jax.ShapeDtypeStruct((s_len, D_MODEL), F32),
                   jax.ShapeDtypeStruct((8, D_MODEL), F32)],
        scratch_shapes=[pltpu.VMEM((8, D_MODEL), F32)],
        compiler_params=_params("arbitrary"),
    )(dh2, ru, h1, g2, w_up, w_down)


def _bwd_mlp_w(z2, dup, ru, dh2, tk):
    s_len = z2.shape[0]
    n = s_len // tk

    def body(z2_ref, dup_ref, ru_ref, dh2_ref, dwup_ref, dwdn_ref, a_up, a_dn):
        t = pl.program_id(1)

        @pl.when(t == 0)
        def _():
            a_up[...] = jnp.zeros_like(a_up)
            a_dn[...] = jnp.zeros_like(a_dn)

        for b in range(FF_PAIR):
            sl = slice(FF_BLOCK * b, FF_BLOCK * (b + 1))
            ruv = ru_ref[:, sl]
            a_up[b] += _dot_tn(z2_ref[...], dup_ref[:, sl])
            a_dn[b] += _dot_tn(ruv * ruv, dh2_ref[...])

        @pl.when(t == n - 1)
        def _():
            dwup_ref[...] = a_up[...].astype(BF16)
            dwdn_ref[...] = a_dn[...].astype(BF16)

    tile = pl.BlockSpec((tk, D_MODEL), lambda j, t: (t, 0))
    ffb = pl.BlockSpec((tk, FF_PAIR * FF_BLOCK), lambda j, t: (t, j))
    return pl.pallas_call(
        body, name="bwd_mlp_w", grid=(N_CHIPS // FF_PAIR, n),
        in_specs=[tile, ffb, ffb, tile],
        out_specs=[pl.BlockSpec((FF_PAIR, D_MODEL, FF_BLOCK), lambda j, t: (j, 0, 0)),
                   pl.BlockSpec((FF_PAIR, FF_BLOCK, D_MODEL), lambda j, t: (j, 0, 0))],
        out_shape=[jax.ShapeDtypeStruct((N_CHIPS, D_MODEL, FF_BLOCK), BF16),
                   jax.ShapeDtypeStruct((N_CHIPS, FF_BLOCK, D_MODEL), BF16)],
        scratch_shapes=[pltpu.VMEM((FF_PAIR, D_MODEL, FF_BLOCK), F32), pltpu.VMEM((FF_PAIR, FF_BLOCK, D_MODEL), F32)],
        compiler_params=_params("arbitrary", "arbitrary"),
    )(z2, dup, ru, dh2)


MIX_VEC_ROWS = 16


def _bwd_mix(dh1, proj, h, pool_w, pool_b, pool_scale, conv_w, conv_b, wa, ba, wx, bx, lru_l, w_out, tm, sub, early):
    s_len = dh1.shape[0]
    n = s_len // tm
    ng = len(POOL_WINDOWS)
    ne_ = len(early)

    def body(dh1_ref, proj_ref, h_ref, projh_ref, hh_ref, pw_ref, pb_ref, ps_ref, cw_ref, cb_ref,
             wa_ref, ba_ref, wx_ref, bx_ref, l_ref, wout_ref, *rest):
        early_ref = rest[:ne_]
        dproj_ref, dwout_ref, dpw_ref, dwa_ref, dwx_ref, vec_ref = rest[ne_:ne_ + 6]
        got_ref = rest[ne_ + 6:2 * ne_ + 6]
        a_out, a_pw, a_wa, a_wx, a_vec, c_g, c_dxb, c_ddc, ssem, rsem = rest[2 * ne_ + 6:]
        q = pl.program_id(0)
        i = n - 1 - q

        @pl.when(q == 0)
        def _():
            _scatter_send(early_ref, got_ref, early, ssem, rsem)
            for r in (a_out, a_pw, a_wa, a_wx, a_vec, c_g, c_dxb, c_ddc):
                r[...] = jnp.zeros_like(r)

        has_prev = (i > 0).astype(F32)
        lv = l_ref[...]
        lsl8 = LRU_C * _log_sigmoid(lv)
        cg, cdxb, cddc = c_g[0:1, :], c_dxb[...], c_ddc[...]
        vec, cats, dhbs = {}, [], []

        def add(row, v):
            vec[row] = v if row not in vec else vec[row] + v

        for k in reversed(range(tm // sub)):
            rs = slice(k * sub, (k + 1) * sub)
            t0 = i * tm + k * sub
            dh1b = dh1_ref[rs, :].astype(BF16)
            dcat = _dot_nt(dh1b, wout_ref[...])
            dy_pool = dcat[:, :D_POOL]
            dy_lru = dcat[:, D_POOL:]

            proj = proj_ref[rs, :].astype(F32)
            u_pool = proj[:, :D_POOL]
            u_lru = proj[:, D_POOL:D_POOL + D_LRU]
            u_gate = proj[:, D_POOL + D_LRU:]
            if k > 0:
                halo = proj_ref[k * sub - POOL_HALO:k * sub, :].astype(F32)
                h_prev_row = h_ref[k * sub - 1:k * sub, :]
            else:
                halo = projh_ref[...].astype(F32) * has_prev
                h_prev_row = hh_ref[7:8, :] * has_prev

            d, inv = _pool_diff(u_pool, halo[:, :D_POOL], t0)
            db = d.astype(BF16)
            ypre = jnp.concatenate(
                [_dot(db[:, POOL_GROUP * g:POOL_GROUP * (g + 1)], pw_ref[g]) for g in range(ng)], axis=1) + pb_ref[...]
            dyp = dy_pool * ps_ref[...]
            dypb = dyp.astype(BF16)
            dds = []
            for g in range(ng):
                sl = slice(POOL_GROUP * g, POOL_GROUP * (g + 1))
                a_pw[g] += _dot_tn(db[:, sl], dypb[:, sl])
                dds.append(_dot_nt(dypb[:, sl], pw_ref[g]))
            du_pool, ddc = _pool_diff_bwd(jnp.concatenate(dds, axis=1), inv, cddc)
            cddc = ddc[:POOL_HALO]
            add(0, jnp.sum(dyp, axis=0, keepdims=True))
            add(1, jnp.sum(dy_pool * ypre, axis=0, keepdims=True))

            taps = _conv_taps(u_lru, halo[POOL_HALO - CONV_HALO:, D_POOL:D_POOL + D_LRU])
            xb = cb_ref[...]
            for c in range(CONV_WIDTH):
                xb = xb + taps[c] * cw_ref[c:c + 1, :]
            r, ig, a, mult, inv_mult, first = _lru_gates(xb, wa_ref, ba_ref[...], wx_ref, bx_ref[...], lsl8, t0)
            hv = h_ref[rs, :]
            gl, th = _gelu(u_gate)
            du_gate = dy_lru * hv * _gelu_grad(u_gate, th)
            cats.insert(0, jnp.concatenate([ypre * ps_ref[...], hv * gl], axis=1).astype(BF16))
            dhbs.insert(0, dh1b)
            last = _rows(a.shape, 0) == sub - 1
            a_next = jnp.where(last, 1.0, pltpu.roll(a, sub - 1, 0))
            gh = _scan_rev(a_next, dy_lru * gl, cg)
            cg = a[0:1, :] * gh[0:1, :]
            h_prev = jnp.where(_rows(hv.shape, 0) == 0, h_prev_row, pltpu.roll(hv, 1, 0))
            gix = gh * ig * xb
            dla = gh * h_prev * a - jnp.where(first, 0.0, gix * a * a * inv_mult)
            dpre_r = dla * lsl8 * r * (1.0 - r)
            dpre_i = gh * mult * xb * ig * (1.0 - ig)
            dprb = dpre_r.astype(BF16)
            dpib = dpre_i.astype(BF16)
            xbb = xb.astype(BF16)
            for pair in range(GATE_PAIRS):
                sl = slice(POOL_GROUP * pair, POOL_GROUP * (pair + 1))
                a_wa[pair] += _dot_tn(xbb[:, sl], dprb[:, sl])
                a_wx[pair] += _dot_tn(xbb[:, sl], dpib[:, sl])
            dxb = gh * mult * ig + _gate_dot_nt(dprb, wa_ref) + _gate_dot_nt(dpib, wx_ref)
            add(2, jnp.sum(dxb, axis=0, keepdims=True))
            add(3, jnp.sum(dpre_r, axis=0, keepdims=True))
            add(4, jnp.sum(dpre_i, axis=0, keepdims=True))
            add(5, jnp.sum(dla * r, axis=0, keepdims=True))
            ext = jnp.concatenate([dxb, cdxb], axis=0)
            cdxb = dxb[:CONV_HALO]
            ne = sub + CONV_HALO
            du_lru = dxb * cw_ref[CONV_WIDTH - 1:CONV_WIDTH, :]
            for c in range(CONV_WIDTH):
                add(8 + c, jnp.sum(dxb * taps[c], axis=0, keepdims=True))
                if c < CONV_WIDTH - 1:
                    du_lru = du_lru + pltpu.roll(ext, ne - (CONV_WIDTH - 1 - c), 0)[:sub] * cw_ref[c:c + 1, :]
            dproj_ref[rs, :] = jnp.concatenate([du_pool, du_lru, du_gate], axis=1).astype(BF16)
        a_out[...] += _dot_tn(jnp.concatenate(cats, axis=0), jnp.concatenate(dhbs, axis=0))
        c_g[...] = jnp.broadcast_to(cg, c_g.shape)
        c_dxb[...] = cdxb
        c_ddc[...] = cddc
        for row, v in vec.items():
            a_vec[row:row + 1, :] += v

        @pl.when(q == n - 1)
        def _():
            dwout_ref[...] = a_out[...].astype(BF16)
            dpw_ref[...] = a_pw[...]
            dwa_ref[...] = a_wa[...]
            dwx_ref[...] = a_wx[...]
            vec_ref[...] = a_vec[...]
            vec_ref[5:6, :] = a_vec[5:6, :] * (LRU_C * _sigmoid(-lv))
            _scatter_done(got_ref, early, ssem, rsem)

    rev =lambda w: pl.BlockSpec((tm, w), lambda q: (n - 1 - q, 0))
    halo_p = pl.BlockSpec((POOL_HALO, D_IN_PROJ), lambda q: (jnp.maximum((n - 1 - q) * (tm // POOL_HALO) - 1, 0), 0))
    halo_h = pl.BlockSpec((8, D_LRU), lambda q: (jnp.maximum((n - 1 - q) * (tm // 8) - 1, 0), 0))
    wts = [pool_w, pool_b, pool_scale, conv_w, conv_b, wa, ba, wx, bx, lru_l, w_out]
    outs = pl.pallas_call(
        body, name="bwd_mix", grid=(n,),
        in_specs=[rev(D_MODEL), rev(D_IN_PROJ), rev(D_LRU), halo_p, halo_h] + [_resident(a.shape) for a in wts]
        + [ANY] * ne_,
        out_specs=[rev(D_IN_PROJ), _full((D_MODEL, D_MODEL)), _full((ng, POOL_GROUP, POOL_GROUP)),
                   _full(wa.shape), _full(wa.shape), _full((MIX_VEC_ROWS, D_LRU))] + [ANY] * ne_,
        out_shape=[jax.ShapeDtypeStruct((s_len, D_IN_PROJ), BF16), jax.ShapeDtypeStruct((D_MODEL, D_MODEL), BF16),
                   jax.ShapeDtypeStruct((ng, POOL_GROUP, POOL_GROUP), F32), jax.ShapeDtypeStruct(wa.shape, F32),
                   jax.ShapeDtypeStruct(wa.shape, F32), jax.ShapeDtypeStruct((MIX_VEC_ROWS, D_LRU), F32)]
        + _scatter_shapes(early),
        scratch_shapes=[pltpu.VMEM((D_MODEL, D_MODEL), F32), pltpu.VMEM((ng, POOL_GROUP, POOL_GROUP), F32),
                        pltpu.VMEM(wa.shape, F32), pltpu.VMEM(wa.shape, F32),
                        pltpu.VMEM((MIX_VEC_ROWS, D_LRU), F32), pltpu.VMEM((8, D_LRU), F32),
                        pltpu.VMEM((CONV_HALO, D_LRU), F32), pltpu.VMEM((POOL_HALO, D_POOL), F32)] + _scatter_sems(ne_),
        compiler_params=_params("arbitrary"),
    )(dh1, proj, h, proj, h, *wts, *early)
    return outs[:6], outs[6:]


def _bwd_in(dproj, x, dh1, g1, w_in, tm):
    s_len = x.shape[0]
    n = s_len // tm
    cb = D_IN_PROJ // N_CHIPS

    def body(dp_ref, x_ref, dh1_ref, g1_ref, win_ref, dx_ref, dwin_ref, dg1_ref, a_w, a_g):
        i = pl.program_id(0)

        @pl.when(i == 0)
        def _():
            a_w[...] = jnp.zeros_like(a_w)
            a_g[...] = jnp.zeros_like(a_g)

        dp = dp_ref[...]
        xv = x_ref[...]
        zb = (xv * _rstd(xv) * g1_ref[...]).astype(BF16)
        dz = jnp.zeros((tm, D_MODEL), F32)
        for j in range(N_CHIPS):
            dpj = dp[:, cb * j:cb * (j + 1)]
            dz = dz + _dot_nt(dpj, win_ref[j])
            a_w[j] += _dot_tn(zb, dpj)
        dx, dg = _rms_bwd(xv, g1_ref[...], dz)
        dx_ref[...] = dh1_ref[...] + dx
        a_g[0:1, :] += dg

        @pl.when(i == n - 1)
        def _():
            dwin_ref[...] = a_w[...].astype(BF16)
            dg1_ref[...] = a_g[...]

    row = lambda w: pl.BlockSpec((tm, w), lambda i: (i, 0))
    return pl.pallas_call(
        body, name="bwd_in", grid=(n,),
        in_specs=[row(D_IN_PROJ), row(D_MODEL), row(D_MODEL), _resident(g1.shape), _resident(w_in.shape)],
        out_specs=[row(D_MODEL), _full(w_in.shape), _full((8, D_MODEL))],
        out_shape=[jax.ShapeDtypeStruct((s_len, D_MODEL), F32), jax.ShapeDtypeStruct(w_in.shape, BF16),
                   jax.ShapeDtypeStruct((8, D_MODEL), F32)],
        scratch_shapes=[pltpu.VMEM(w_in.shape, F32), pltpu.VMEM((8, D_MODEL), F32)],
        compiler_params=_params("arbitrary"),
    )(dproj, x, dh1, g1, w_in)


def _place():
    x, y, c = lax.axis_index("x"), lax.axis_index("y"), lax.axis_index("c")
    chips = [(1 - x, y), (x, 1 - y), (1 - x, 1 - y)]
    return x, y, c, chips


def _rcopy(src, dst, ssem, rsem, dev):
    return pltpu.make_async_remote_copy(src_ref=src, dst_ref=dst, send_sem=ssem, recv_sem=rsem,
                                        device_id=dev, device_id_type=MESH)


ANY = pl.BlockSpec(memory_space=pl.ANY)
COPY_CHUNK_BYTES = 128 * 1024
ROW_ALIGN = 16


def _row_chunks(rows, row_bytes):
    per = max(ROW_ALIGN, (COPY_CHUNK_BYTES // row_bytes) // ROW_ALIGN * ROW_ALIGN)
    return [(r0, min(per, rows - r0)) for r0 in range(0, rows, per)]


def _row_bytes(a):
    return a.shape[-1] * jnp.dtype(a.dtype).itemsize


def _stack_own(shards, dtypes, pos, steps, tag, late=()):
    nw = len(shards)
    nl = len(late)

    def body(pos_ref, *refs):
        outs = refs[nw + nl:2 * nw + nl]
        late_ref = refs[2 * nw + nl:2 * nw + 2 * nl]
        i = pl.program_id(0)
        if nl:
            ssem, rsem = refs[2 * nw + 2 * nl:]

            @pl.when(i == 0)
            def _():
                _gather_send(late_ref, late, ssem, rsem)

        for w in range(nw):
            outs[w][0] = refs[w][...].astype(dtypes[w])

        if nl:
            @pl.when(i == max(steps - PASS_STEPS, 0))
            def _():
                _gather_pass(late_ref, late, ssem, rsem)

            @pl.when(i == steps - 1)
            def _():
                _gather_done(late_ref, late, ssem, rsem)

    def split(s):
        return s.shape[0] % (steps * ROW_ALIGN) == 0

    ins = [pl.BlockSpec((s.shape[0] // steps, s.shape[1]), lambda i, p: (i, 0)) if split(s)
           else pl.BlockSpec(s.shape, lambda i, p: (0, 0)) for s in shards]
    outs = [pl.BlockSpec((1, s.shape[0] // steps, s.shape[1]), lambda i, p: (p[0], i, 0)) if split(s)
            else pl.BlockSpec((1,) + s.shape, lambda i, p: (p[0], 0, 0)) for s in shards]
    res = pl.pallas_call(
        body, name="stack_own_" + tag,
        grid_spec=pltpu.PrefetchScalarGridSpec(
            num_scalar_prefetch=1, grid=(steps,), in_specs=ins + [ANY] * nl, out_specs=outs + [ANY] * nl,
            scratch_shapes=_gather_sems(nl) if nl else []),
        out_shape=[jax.ShapeDtypeStruct((N_CHIPS,) + s.shape, d) for s, d in zip(shards, dtypes)]
        + [jax.ShapeDtypeStruct(a.shape, a.dtype) for a in late],
        input_output_aliases={1 + nw + k: nw + k for k in range(nl)},
        compiler_params=_params("arbitrary"),
    )(pos, *shards, *late)
    return res[:nw], res[nw:]


def _gather_send(outs, stacks, ssem, rsem):
    x, y, c, chips = _place()
    me = 2 * x + y
    for w, st in enumerate(stacks):
        half = st.shape[1] // 2
        for s, (px, py) in enumerate(chips):
            for r0, rs in _row_chunks(half, _row_bytes(st)):
                piece = outs[w].at[me, pl.ds(c * half + r0, rs)]
                _rcopy(piece, piece, ssem.at[w, s], rsem.at[w, s], (px, py, c)).start()


def _gather_pass(outs, stacks, ssem, rsem):
    x, y, c, chips = _place()
    sib = (x, y, 1 - c)
    for w, st in enumerate(stacks):
        half = st.shape[1] // 2
        for s, (px, py) in enumerate(chips):
            blk = outs[w].at[2 * px + py, pl.ds(c * half, half)]
            _rcopy(blk, blk, ssem.at[w, s], rsem.at[w, s], sib).wait_recv()
            for r0, rs in _row_chunks(half, _row_bytes(st)):
                piece = outs[w].at[2 * px + py, pl.ds(c * half + r0, rs)]
                _rcopy(piece, piece, ssem.at[w, 3 + s], rsem.at[w, 3 + s], sib).start()


def _gather_done(outs, stacks, ssem, rsem):
    x, y, c, chips = _place()
    sib = (x, y, 1 - c)
    for w, st in enumerate(stacks):
        half = st.shape[1] // 2
        for s, (px, py) in enumerate(chips):
            blk = outs[w].at[2 * px + py, pl.ds((1 - c) * half, half)]
            _rcopy(blk, blk, ssem.at[w, 3 + s], rsem.at[w, 3 + s], sib).wait_recv()
    for w, st in enumerate(stacks):
        half = st.shape[1] // 2
        blk = outs[w].at[0, pl.ds(0, half)]
        for s in range(6):
            _rcopy(blk, blk, ssem.at[w, s], rsem.at[w, s], sib).wait_send()


def _gather_sems(nw):
    return [pltpu.SemaphoreType.DMA((nw, 6)), pltpu.SemaphoreType.DMA((nw, 6))]


def _swap_halves(grads, tag):
    nw = len(grads)

    def body(*refs):
        ins, got = refs[:nw], refs[nw:2 * nw]
        ssem, rsem = refs[2 * nw:]
        x, y, c, _ = _place()
        cps = []
        for w in range(nw):
            hr = grads[w].shape[1] // 2
            for k in range(N_CHIPS):
                for r0, rs in _row_chunks(hr, _row_bytes(grads[w])):
                    _rcopy(ins[w].at[k, pl.ds((1 - c) * hr + r0, rs)], got[w].at[k, pl.ds(r0, rs)],
                           ssem.at[w], rsem.at[w], (x, y, 1 - c)).start()
            cps.append(_rcopy(got[w], got[w], ssem.at[w], rsem.at[w], (x, y, 1 - c)))
        for cp in cps:
            cp.wait()

    return pl.pallas_call(
        body, name="swap_halves_" + tag,
        in_specs=[ANY] * nw, out_specs=[ANY] * nw,
        out_shape=[jax.ShapeDtypeStruct((g.shape[0], g.shape[1] // 2, g.shape[2]), g.dtype) for g in grads],
        scratch_shapes=[pltpu.SemaphoreType.DMA((nw,)), pltpu.SemaphoreType.DMA((nw,))],
    )(*grads)


def _add_pairs(grads, got, pos, steps, tag):
    nw = len(grads)

    def body(pos_ref, *refs):
        for w in range(nw):
            refs[2 * nw + w][...] = (refs[w][...].astype(F32) + refs[nw + w][...].astype(F32)).astype(BF16)

    blk = lambda a: (a.shape[0], a.shape[1] // steps, a.shape[2])
    own = [pl.BlockSpec(blk(a), lambda i, p: (0, p[1] * steps + i, 0)) for a in got]
    rec = [pl.BlockSpec(blk(a), lambda i, p: (0, i, 0)) for a in got]
    return pl.pallas_call(
        body, name="add_pairs_" + tag,
        grid_spec=pltpu.PrefetchScalarGridSpec(num_scalar_prefetch=1, grid=(steps,), in_specs=own + rec, out_specs=rec),
        out_shape=[jax.ShapeDtypeStruct(a.shape, BF16) for a in got],
        compiler_params=_params("arbitrary"),
    )(pos, *grads, *got)


def _scatter_send(ins, got, parts, ssem, rsem):
    x, y, c, chips = _place()
    for w, p in enumerate(parts):
        for s, (px, py) in enumerate(chips):
            for r0, rs in _row_chunks(p.shape[1], _row_bytes(p)):
                _rcopy(ins[w].at[2 * px + py, pl.ds(r0, rs)], got[w].at[s, pl.ds(r0, rs)],
                       ssem.at[w, s], rsem.at[w, s], (px, py, c)).start()


def _scatter_done(got, parts, ssem, rsem):
    x, y, c, chips = _place()
    for w in range(len(parts)):
        for s, (px, py) in enumerate(chips):
            _rcopy(got[w].at[s], got[w].at[s], ssem.at[w, s], rsem.at[w, s], (px, py, c)).wait()


def _scatter_sems(nw):
    return [pltpu.SemaphoreType.DMA((nw, 3)), pltpu.SemaphoreType.DMA((nw, 3))]


def _scatter_shapes(parts):
    return [jax.ShapeDtypeStruct((3,) + p.shape[1:], p.dtype) for p in parts]


def _scatter_chips(parts):
    nw = len(parts)

    def body(*refs):
        ins, got = refs[:nw], refs[nw:2 * nw]
        ssem, rsem = refs[2 * nw:]
        _scatter_send(ins, got, parts, ssem, rsem)
        _scatter_done(got, parts, ssem, rsem)

    return pl.pallas_call(
        body, name="scatter_chips",
        in_specs=[ANY] * nw, out_specs=[ANY] * nw, out_shape=_scatter_shapes(parts),
        scratch_shapes=_scatter_sems(nw),
    )(*parts)


def _sum_chips(parts, got, pos, steps):
    nw = len(parts)

    def body(pos_ref, *refs):
        for w in range(nw):
            acc = refs[w][0].astype(F32)
            for s in range(3):
                acc = acc + refs[nw + w][s].astype(F32)
            refs[2 * nw + w][...] = acc

    own = [pl.BlockSpec((1, p.shape[1] // steps, p.shape[2]), lambda i, ps: (ps[0], i, 0)) for p in parts]
    rec = [pl.BlockSpec((3, p.shape[1] // steps, p.shape[2]), lambda i, ps: (0, i, 0)) for p in parts]
    outs = [pl.BlockSpec((p.shape[1] // steps, p.shape[2]), lambda i, ps: (ps[1] * steps + i, 0)) for p in parts]
    return pl.pallas_call(
        body, name="sum_chips",
        grid_spec=pltpu.PrefetchScalarGridSpec(num_scalar_prefetch=1, grid=(steps,), in_specs=own + rec, out_specs=outs),
        out_shape=[jax.ShapeDtypeStruct((2 * p.shape[1], p.shape[2]), F32) for p in parts],
        compiler_params=_params("arbitrary"),
    )(pos, *parts, *got)


def _join_halves(shards):
    nw = len(shards)

    def body(*refs):
        outs = refs[nw:2 * nw]
        ssem, rsem = refs[2 * nw:]
        x, y, c, _ = _place()
        cps = []
        for w in range(nw):
            hr = shards[w].shape[0] // 2
            for r0, rs in _row_chunks(hr, _row_bytes(shards[w])):
                piece = outs[w].at[pl.ds(c * hr + r0, rs)]
                _rcopy(piece, piece, ssem.at[w], rsem.at[w], (x, y, 1 - c)).start()
            mine = outs[w].at[pl.ds(c * hr, hr)]
            cps.append(_rcopy(mine, mine, ssem.at[w], rsem.at[w], (x, y, 1 - c)))
        for cp in cps:
            cp.wait()

    return pl.pallas_call(
        body, name="join_halves",
        in_specs=[ANY] * nw, out_specs=[ANY] * nw,
        out_shape=[jax.ShapeDtypeStruct(s.shape, F32) for s in shards],
        input_output_aliases={w: w for w in range(nw)},
        scratch_shapes=[pltpu.SemaphoreType.DMA((nw,)), pltpu.SemaphoreType.DMA((nw,))],
    )(*shards)


def _scatter_and_allreduce(parts, packed):
    nw = len(parts)
    rows = packed.shape[0]
    half = rows // 2

    def body(*refs):
        ins, p_ref = refs[:nw], refs[nw]
        got, out_ref = refs[nw + 1:2 * nw + 1], refs[2 * nw + 1]
        rfull, rhalf, ssem, rsem, bsem_s, bsem_r = refs[2 * nw + 2:]
        x, y, c, _ = _place()
        sib = (x, y, 1 - c)
        _scatter_send(ins, got, parts, ssem, rsem)
        out_ref[...] = p_ref[...]
        cp = _rcopy(out_ref, rfull, bsem_s.at[0], bsem_r.at[0], sib)
        cp.start()
        cp.wait()
        out_ref[...] = out_ref[...] + rfull[...]
        mine = pl.ds(pl.multiple_of(c * half, 8), half)
        other = pl.ds(pl.multiple_of((1 - c) * half, 8), half)
        for st, peer in enumerate([(1 - x, y, c), (x, 1 - y, c)]):
            cp = _rcopy(out_ref.at[mine], rhalf.at[st], bsem_s.at[1 + st], bsem_r.at[1 + st], peer)
            cp.start()
            cp.wait()
            out_ref[mine, :] = out_ref[mine, :] + rhalf[st]
        cp = _rcopy(out_ref.at[mine], rhalf.at[2], bsem_s.at[3], bsem_r.at[3], sib)
        cp.start()
        cp.wait()
        out_ref[other, :] = rhalf[2]
        _scatter_done(got, parts, ssem, rsem)

    vm = pl.BlockSpec(memory_space=pltpu.VMEM)
    outs = pl.pallas_call(
        body, name="scatter_and_allreduce",
        in_specs=[ANY] * nw + [vm], out_specs=[ANY] * nw + [vm],
        out_shape=_scatter_shapes(parts) + [jax.ShapeDtypeStruct(packed.shape, F32)],
        scratch_shapes=[pltpu.VMEM(packed.shape, F32), pltpu.VMEM((3, half, packed.shape[1]), F32)] + _scatter_sems(nw)
        + [pltpu.SemaphoreType.DMA((4,)), pltpu.SemaphoreType.DMA((4,))],
        compiler_params=pltpu.CompilerParams(vmem_limit_bytes=VMEM_LIMIT),
    )(*parts, packed)
    return outs[:nw], outs[nw]


def _adamw_math(w, g, m, v):
    m = ADAM_B1 * m + (1.0 - ADAM_B1) * g
    v = ADAM_B2 * v + (1.0 - ADAM_B2) * (g * g)
    delta = -ADAM_LR * ((m * ADAM_C1) / (jnp.sqrt(v * ADAM_C2) + ADAM_EPS) + ADAM_WD * w)
    return delta, m, v


def _adamw(ws, gs, ms, vs, steps, name):
    nw = len(ws)

    def body(*refs):
        for k in range(nw):
            g = refs[nw + k][...]
            d, m, v = _adamw_math(refs[k][...], g, refs[2 * nw + k][...], refs[3 * nw + k][...])
            refs[4 * nw + k][...] = d
            refs[5 * nw + k][...] = m
            refs[6 * nw + k][...] = v
            refs[7 * nw + k][...] = g

    specs = [pl.BlockSpec((a.shape[0] // steps, a.shape[1]), lambda i: (i, 0)) for a in ws]
    shapes = [jax.ShapeDtypeStruct(a.shape, F32) for a in ws]
    outs = pl.pallas_call(
        body, name=name, grid=(steps,),
        in_specs=specs * 4, out_specs=specs * 4, out_shape=shapes * 4,
        compiler_params=_params("arbitrary"),
    )(*ws, *gs, *ms, *vs)
    return outs[:nw], outs[nw:2 * nw], outs[2 * nw:3 * nw], outs[3 * nw:]


SMALL = ["norm_mix_g", "pool_w", "pool_b", "pool_scale", "conv_b", "gate_a_w", "gate_a_b", "gate_x_w", "gate_x_b",
         "lru_L", "norm_mlp_g", "norm_ple_g", "b_ple_gate", "norm_final_g"]
BIG = ["w_in", "w_out", "w_up", "w_down", "w_ple_gate", "w_ple_proj"]
ORDER = ["norm_mix_g", "w_in", "pool_w", "pool_b", "pool_scale", "conv_w", "conv_b", "gate_a_w", "gate_a_b", "gate_x_w",
         "gate_x_b", "lru_L", "w_out", "norm_mlp_g", "w_up", "w_down", "norm_ple_g", "w_ple_gate", "b_ple_gate",
         "w_ple_proj", "norm_final_g"]
LANES = 128


def _pair_blocks(w):
    eye = jnp.eye(2, dtype=w.dtype)
    w4 = w.reshape(GATE_PAIRS, 2, LRU_BLOCK, LRU_BLOCK)
    return jnp.einsum("qaij,ab->qaibj", w4, eye).reshape(GATE_PAIRS, POOL_GROUP, POOL_GROUP)


def _diag_blocks(pairs):
    f = pairs.reshape(GATE_PAIRS, 2, LRU_BLOCK, 2, LRU_BLOCK)
    return jnp.stack([f[h // 2, h % 2, :, h % 2, :] for h in range(LRU_HEADS)])


def _rows128(a):
    return a.reshape(-1, LANES)


def _pad8(a):
    r = (-a.shape[0]) % 8
    return jnp.pad(a, ((0, r), (0, 0))) if r else a


def kernel(x, p, norm_mix_g, w_in, pool_w, pool_b, pool_scale, conv_w, conv_b, gate_a_w, gate_a_b, gate_x_w, gate_x_b, lru_L, w_out, norm_mlp_g, w_up, w_down, norm_ple_g, w_ple_gate, b_ple_gate, w_ple_proj, norm_final_g, loss_target, m_norm_mix_g, m_w_in, m_pool_w, m_pool_b, m_pool_scale, m_conv_w, m_conv_b, m_gate_a_w, m_gate_a_b, m_gate_x_w, m_gate_x_b, m_lru_L, m_w_out, m_norm_mlp_g, m_w_up, m_w_down, m_norm_ple_g, m_w_ple_gate, m_b_ple_gate, m_w_ple_proj, m_norm_final_g, v_norm_mix_g, v_w_in, v_pool_w, v_pool_b, v_pool_scale, v_conv_w, v_conv_b, v_gate_a_w, v_gate_a_b, v_gate_x_w, v_gate_x_b, v_lru_L, v_w_out, v_norm_mlp_g, v_w_up, v_w_down, v_norm_ple_g, v_w_ple_gate, v_b_ple_gate, v_w_ple_proj, v_norm_final_g):
    W = dict(norm_mix_g=norm_mix_g, w_in=w_in, pool_w=pool_w, pool_b=pool_b, pool_scale=pool_scale, conv_w=conv_w,
             conv_b=conv_b, gate_a_w=gate_a_w, gate_a_b=gate_a_b, gate_x_w=gate_x_w, gate_x_b=gate_x_b, lru_L=lru_L,
             w_out=w_out, norm_mlp_g=norm_mlp_g, w_up=w_up, w_down=w_down, norm_ple_g=norm_ple_g,
             w_ple_gate=w_ple_gate, b_ple_gate=b_ple_gate, w_ple_proj=w_ple_proj, norm_final_g=norm_final_g)
    M = dict(norm_mix_g=m_norm_mix_g, w_in=m_w_in, pool_w=m_pool_w, pool_b=m_pool_b, pool_scale=m_pool_scale,
             conv_w=m_conv_w, conv_b=m_conv_b, gate_a_w=m_gate_a_w, gate_a_b=m_gate_a_b, gate_x_w=m_gate_x_w,
             gate_x_b=m_gate_x_b, lru_L=m_lru_L, w_out=m_w_out, norm_mlp_g=m_norm_mlp_g, w_up=m_w_up, w_down=m_w_down,
             norm_ple_g=m_norm_ple_g, w_ple_gate=m_w_ple_gate, b_ple_gate=m_b_ple_gate, w_ple_proj=m_w_ple_proj,
             norm_final_g=m_norm_final_g)
    V = dict(norm_mix_g=v_norm_mix_g, w_in=v_w_in, pool_w=v_pool_w, pool_b=v_pool_b, pool_scale=v_pool_scale,
             conv_w=v_conv_w, conv_b=v_conv_b, gate_a_w=v_gate_a_w, gate_a_b=v_gate_a_b, gate_x_w=v_gate_x_w,
             gate_x_b=v_gate_x_b, lru_L=v_lru_L, w_out=v_w_out, norm_mlp_g=v_norm_mlp_g, w_up=v_w_up, w_down=v_w_down,
             norm_ple_g=v_norm_ple_g, w_ple_gate=v_w_ple_gate, b_ple_gate=v_b_ple_gate, w_ple_proj=v_w_ple_proj,
             norm_final_g=v_norm_final_g)

    s_len = x.shape[1]
    sub_mix = min(256, s_len)
    tm = min(512, s_len)
    chip = (2 * lax.axis_index("x") + lax.axis_index("y")).astype(jnp.int32)
    pos = jnp.stack([chip, lax.axis_index("c").astype(jnp.int32)])

    shards = [w_in[0], w_out[0], w_up[0], w_down[0], w_ple_gate[0], w_ple_proj[0], jnp.pad(conv_w[0], ((0, 12), (0, 0)))]
    first, _ = _stack_own([shards[0], shards[1], shards[6]], [BF16, BF16, F32], pos, 8, "first")
    (st_up, st_dn, st_pg, st_pp), (win_g, wout_g, cw_g) = _stack_own(shards[2:6], [BF16] * 4, pos, 8, "rest", first)
    wout_f = wout_g.reshape(D_MODEL, D_MODEL)
    cw_f = jnp.transpose(cw_g[:, :CONV_WIDTH], (1, 0, 2)).reshape(CONV_WIDTH, D_LRU)
    pw_b = pool_w[0].astype(BF16)
    wa_b = _pair_blocks(gate_a_w[0]).astype(BF16)
    wx_b = _pair_blocks(gate_x_w[0]).astype(BF16)
    pb_r = pool_b.reshape(1, D_POOL)
    ba_r = gate_a_b.reshape(1, D_LRU)
    bx_r = gate_x_b.reshape(1, D_LRU)
    g4 = norm_final_g.reshape(1, D_MODEL)
    mix_w = (pw_b, pb_r, pool_scale, cw_f, conv_b, wa_b, ba_r, wx_b, bx_r, lru_L, wout_f)

    xs, ps, ts = x[0], p[0, 0], loss_target[0]
    (proj, hst, h1), (wup_g, wdn_g) = _fwd_mix(xs, norm_mix_g, win_g, *mix_w, tm, sub_mix, [st_up, st_dn])
    (z2, ru, h2), (wpg_g, wpp_g) = _fwd_mlp(h1, norm_mlp_g, wup_g, wdn_g, tm, [st_pg, st_pp])
    wpg_f = wpg_g.reshape(D_MODEL, D_MODEL)
    wpp_f = jnp.transpose(wpp_g, (1, 0, 2)).reshape(PLE_DIM, D_MODEL)
    dh2, dh2b, d_wpg, d_wpp, head_vec = _head(h2, ps, ts, norm_ple_g, wpg_f, b_ple_gate, wpp_f, g4, tm)
    dup, dh1, mlp_vec = _bwd_mlp_x(dh2, ru, h1, norm_mlp_g, wup_g, wdn_g, tm)
    d_wup, d_wdn = _bwd_mlp_w(z2, dup, ru, dh2b, tm)
    early = [d_wup, d_wdn, d_wpg.reshape(N_CHIPS, D_MODEL // N_CHIPS, D_MODEL), d_wpp]
    pair_e = _add_pairs(early, _swap_halves(early, "early"), pos, 8, "early")
    (dproj, d_wout, d_pw, d_wa, d_wx, mix_vec), got_e = _bwd_mix(dh1, proj, hst, *mix_w, tm, sub_mix, pair_e)

    dx, d_win, in_vec = _bwd_in(dproj, xs, dh1, norm_mix_g, win_g, tm)

    last = [d_win, d_wout.reshape(N_CHIPS, D_MODEL // N_CHIPS, D_MODEL)]
    pair_l = _add_pairs(last, _swap_halves(last, "last"), pos, 8, "last")

    g_small = {
        "norm_mix_g": in_vec[0:1], "pool_w": d_pw, "pool_b": mix_vec[0:1], "pool_scale": mix_vec[1:2],
        "conv_b": mix_vec[2:3], "gate_a_w": _diag_blocks(d_wa), "gate_a_b": mix_vec[3:4],
        "gate_x_w": _diag_blocks(d_wx), "gate_x_b": mix_vec[4:5], "lru_L": mix_vec[5:6], "norm_mlp_g": mlp_vec[0:1],
        "norm_ple_g": head_vec[0:1], "b_ple_gate": head_vec[2:3], "norm_final_g": head_vec[1:2],
    }
    d_cw = jnp.transpose(mix_vec[8:8 + CONV_WIDTH].reshape(CONV_WIDTH, N_CHIPS, LANES), (1, 0, 2)).reshape(-1, LANES)
    pieces = [_pad8(_rows128(g_small[k])) for k in SMALL] + [d_cw, _pad8(head_vec[3:4, :LANES])]
    offs = [0]
    for pc in pieces:
        offs.append(offs[-1] + pc.shape[0])
    if offs[-1] % 16:
        pieces.append(jnp.zeros((8, LANES), F32))
    got_l, red = _scatter_and_allreduce(pair_l, jnp.concatenate(pieces, axis=0))
    g_big = _join_halves(_sum_chips(pair_l + pair_e, got_l + got_e, pos, 8))
    loss = red[offs[-2], 0]
    g_cw = lax.dynamic_slice(red, (offs[len(SMALL)] + CONV_WIDTH * chip, 0), (CONV_WIDTH, LANES))

    def packed(src):
        return jnp.concatenate([_pad8(_rows128(src[k])) for k in SMALL] + [_pad8(src["conv_w"][0])], axis=0)

    n_small = offs[len(SMALL)]
    g_pack = jnp.concatenate([red[:n_small], _pad8(g_cw)], axis=0)
    (d_pack,), (m_pack,), (v_pack,), _ = _adamw([packed(W)], [g_pack], [packed(M)], [packed(V)], 1, "adamw_small")

    big2d = lambda src: [src[k][0] for k in BIG]
    d_big, m_big, v_big, g_big = _adamw(big2d(W), g_big, big2d(M), big2d(V), 8, "adamw_big")

    def unpack(pack, big_list):
        out = {}
        for idx, k in enumerate(SMALL):
            n_el = W[k].size
            out[k] = pack[offs[idx]:offs[idx + 1]].reshape(-1)[:n_el].reshape(W[k].shape)
        out["conv_w"] = pack[n_small:n_small + CONV_WIDTH].reshape(W["conv_w"].shape)
        for k, a in zip(BIG, big_list):
            out[k] = a.reshape(W[k].shape)
        return out

    grads = unpack(g_pack, g_big)
    deltas = unpack(d_pack, d_big)
    new_m = unpack(m_pack, m_big)
    new_v = unpack(v_pack, v_big)
    return (loss, dx[None], *[grads[k] for k in ORDER], *[deltas[k] for k in ORDER],
            *[new_m[k] for k in ORDER], *[new_v[k] for k in ORDER])
```

```python
import functools

import jax
import jax.numpy as jnp
from jax import lax
from jax.experimental import pallas as pl
from jax.experimental.pallas import tpu as pltpu

F32 = jnp.float32
BF16 = jnp.bfloat16
MESH = pl.DeviceIdType.MESH

D_MODEL = 1024
D_POOL = 512
D_LRU = 512
POOL_WINDOWS = (2, 4, 8, 16)
POOL_GROUP = 128
POOL_HALO = 16
CONV_WIDTH = 4
CONV_HALO = 8
PASS_STEPS = 2
LRU_HEADS = 8
LRU_BLOCK = 64
GATE_PAIRS = 4
LRU_C = 8.0
LRU_UNROLL = 4
D_FF = 4096
PLE_DIM = 256
D_IN_PROJ = 1536
RMS_EPS = 1e-6
N_CHIPS = 4
FF_BLOCK = D_FF // N_CHIPS
FF_PAIR = 2

ADAM_LR = 0.001
ADAM_B1 = 0.9
ADAM_B2 = 0.999
ADAM_EPS = 1e-08
ADAM_WD = 0.01
ADAM_STEP = 10
ADAM_C1 = 1.0 / (1.0 - ADAM_B1 ** ADAM_STEP)
ADAM_C2 = 1.0 / (1.0 - ADAM_B2 ** ADAM_STEP)

VMEM_LIMIT = 56 * 1024 * 1024
GELU_C = 0.7978845608028654
GELU_A = 0.044715

NT = (((1,), (1,)), ((), ()))
TN = (((0,), (0,)), ((), ()))


def _dot(a, b):
    return jnp.dot(a, b, preferred_element_type=F32)


def _dot_nt(a, b):
    return lax.dot_general(a, b, NT, preferred_element_type=F32)


def _dot_tn(a, b):
    return lax.dot_general(a, b, TN, preferred_element_type=F32)


def _params(*sem):
    return pltpu.CompilerParams(dimension_semantics=sem, vmem_limit_bytes=VMEM_LIMIT)


def _full(shape):
    nd = len(shape)
    return pl.BlockSpec(shape, lambda *_: (0,) * nd)


def _resident(shape):
    nd = len(shape)
    return pl.BlockSpec(shape, lambda *_: (0,) * nd, pipeline_mode=pl.Buffered(1))


def _rstd(x):
    return lax.rsqrt(jnp.mean(x * x, axis=-1, keepdims=True) + RMS_EPS)


def _rms_bwd(x, g, dz):
    xr = x * _rstd(x)
    r = _rstd(x)
    dyg = dz * g
    dx = r * (dyg - xr * jnp.mean(dyg * xr, axis=-1, keepdims=True))
    dg = jnp.sum(dz * xr, axis=0, keepdims=True)
    return dx, dg


def _sigmoid(x):
    return 1.0 / (1.0 + jnp.exp(-x))


def _log_sigmoid(v):
    u = jnp.exp(-jnp.abs(v))
    w = 1.0 + u
    l1p = jnp.where(w == 1.0, u, jnp.log(w) * u / jnp.where(w == 1.0, 1.0, w - 1.0))
    return jnp.minimum(v, 0.0) - l1p


def _gelu(x):
    t = jnp.tanh(GELU_C * (x + GELU_A * x * x * x))
    return 0.5 * x * (1.0 + t), t


def _gelu_grad(x, t):
    return 0.5 * (1.0 + t) + 0.5 * x * (1.0 - t * t) * GELU_C * (1.0 + 3.0 * GELU_A * x * x)


def _rows(shape, t0):
    return lax.broadcasted_iota(jnp.int32, shape, 0) + t0


def _pool_diff(u_pool, prev, t0):
    tm = u_pool.shape[0]
    rows = _rows((tm, POOL_GROUP), t0)
    outs, invs = [], []
    for g, w in enumerate(POOL_WINDOWS):
        sl = slice(POOL_GROUP * g, POOL_GROUP * (g + 1))
        ug = u_pool[:, sl]
        s = jnp.concatenate([prev[:, sl], ug], axis=0)
        k = 1
        while k < w:
            s = s + pltpu.roll(s, k, 0)
            k *= 2
        inv = 1.0 / jnp.minimum(rows + 1, w).astype(F32)
        outs.append(s[POOL_HALO:] * inv - ug)
        invs.append(inv)
    return jnp.concatenate(outs, axis=1), jnp.concatenate(invs, axis=1)


def _pool_diff_bwd(dd, inv, nxt):
    ddc = dd * inv
    outs = []
    for g, w in enumerate(POOL_WINDOWS):
        sl = slice(POOL_GROUP * g, POOL_GROUP * (g + 1))
        s = jnp.concatenate([ddc[:, sl], nxt[:, sl]], axis=0)
        n = s.shape[0]
        k = 1
        while k < w:
            s = s + pltpu.roll(s, n - k, 0)
            k *= 2
        outs.append(s[:n - POOL_HALO] - dd[:, sl])
    return jnp.concatenate(outs, axis=1), ddc


def _conv_taps(u, prev):
    ext = jnp.concatenate([prev, u], axis=0)
    return [pltpu.roll(ext, CONV_WIDTH - 1 - k, 0)[CONV_HALO:] if k < CONV_WIDTH - 1 else u for k in range(CONV_WIDTH)]


def _scan_rev(a, b, carry):
    tm = a.shape[0]
    r8 = _rows(a.shape, 0) & 7
    k = 1
    while k < 8:
        ar = pltpu.roll(a, tm - k, 0)
        br = pltpu.roll(b, tm - k, 0)
        m = r8 < 8 - k
        b = jnp.where(m, a * br + b, b)
        a = jnp.where(m, a * ar, a)
        k *= 2
    outs = []
    c = jnp.broadcast_to(carry, (8, a.shape[1]))
    for g in reversed(range(tm // 8)):
        gg = b[8 * g:8 * g + 8] + a[8 * g:8 * g + 8] * c
        outs.insert(0, gg)
        c = jnp.broadcast_to(gg[0:1], c.shape)
    return jnp.concatenate(outs, axis=0)


def _gate_dot(xbb, w_ref):
    return jnp.concatenate(
        [_dot(xbb[:, POOL_GROUP * q:POOL_GROUP * (q + 1)], w_ref[q]) for q in range(GATE_PAIRS)], axis=1)


def _gate_dot_nt(db, w_ref):
    return jnp.concatenate(
        [_dot_nt(db[:, POOL_GROUP * q:POOL_GROUP * (q + 1)], w_ref[q]) for q in range(GATE_PAIRS)], axis=1)


def _lru_decay(r, lsl8, t0):
    a = jnp.exp(r * lsl8)
    first = _rows(r.shape, t0) == 0
    m2 = 1.0 - a * a
    rs = lax.rsqrt(jnp.maximum(m2, jnp.finfo(F32).tiny))
    mult = jnp.where(first, 1.0, m2 * rs)
    return a, mult, rs, first


def _lru_chunks_fwd(r_ref, i_ref, xb_ref, ug_ref, h_ref, y_ref, row0, n_rows, lsl8, t0, hc):
    lsl = jnp.broadcast_to(lsl8, (8, D_LRU))
    sub8 = lax.broadcasted_iota(jnp.int32, (8, D_LRU), 0)

    def chunk(j, hc):
        o = j * 8
        rows = pl.ds(o, 8)
        r = _sigmoid(r_ref[rows, :])
        ig = _sigmoid(i_ref[rows, :])
        r_ref[rows, :] = r
        i_ref[rows, :] = ig
        a = jnp.exp(r * lsl)
        m2 = 1.0 - a * a
        mult = jnp.where(sub8 + (t0 + o) == 0, 1.0, m2 * lax.rsqrt(jnp.maximum(m2, jnp.finfo(F32).tiny)))
        b = mult * (ig * xb_ref[rows, :])
        k = 1
        while k < 8:
            m = sub8 >= k
            b = jnp.where(m, a * pltpu.roll(b, k, 0) + b, b)
            a = jnp.where(m, a * pltpu.roll(a, k, 0), a)
            k *= 2
        h = b + a * hc
        h_ref[pl.ds(row0 + o, 8), :] = h
        gl, _ = _gelu(ug_ref[rows, :])
        y_ref[rows, :] = h * gl
        return jnp.broadcast_to(h[7:8, :], (8, D_LRU))

    for j in range(n_rows // 8):
        hc = chunk(j, hc)
    return hc


def _fwd_mix(x, g1, w_in, pool_w, pool_b, pool_scale, conv_w, conv_b, wa, ba, wx, bx, lru_l, w_out, tm, sub, late):
    s_len = x.shape[0]
    n = s_len // tm
    nl = len(late)

    def body(x_ref, g1_ref, win_ref, pw_ref, pb_ref, ps_ref, cw_ref, cb_ref, wa_ref, ba_ref, wx_ref, bx_ref, l_ref,
             wout_ref, *rest):
        proj_ref, h_ref, h1_ref, rg_ref = rest[nl:nl + 4]
        late_ref = rest[nl + 4:2 * nl + 4]
        cpool, clru, ch, s_r, s_i, s_xb, s_ug, s_y, ssem, rsem = rest[2 * nl + 4:]
        i = pl.program_id(0)

        @pl.when(i == 0)
        def _():
            _gather_send(late_ref, late, ssem, rsem)
            cpool[...] = jnp.zeros_like(cpool)
            clru[...] = jnp.zeros_like(clru)
            ch[...] = jnp.zeros_like(ch)

        lsl8 = LRU_C * _log_sigmoid(l_ref[...])
        cp, cl, hc = cpool[...], clru[...], ch[...]

        def in_proj(k):
            rs = slice(k * sub, (k + 1) * sub)
            xv = x_ref[rs, :]
            zb = (xv * _rstd(xv) * g1_ref[...]).astype(BF16)
            proj = jnp.concatenate([_dot(zb, win_ref[j]) for j in range(N_CHIPS)], axis=1)
            proj_ref[rs, :] = proj.astype(BF16)
            return xv, proj

        nxt = in_proj(0)
        for k in range(tm // sub):
            rs = slice(k * sub, (k + 1) * sub)
            t0 = i * tm + k * sub
            xv, proj = nxt
            if k + 1 < tm // sub:
                nxt = in_proj(k + 1)
            u_pool = proj[:, :D_POOL]
            u_lru = proj[:, D_POOL:D_POOL + D_LRU]
            u_gate = proj[:, D_POOL + D_LRU:]

            d, _ = _pool_diff(u_pool, cp, t0)
            cp = u_pool[sub - POOL_HALO:]
            db = d.astype(BF16)
            yp = jnp.concatenate(
                [_dot(db[:, POOL_GROUP * g:POOL_GROUP * (g + 1)], pw_ref[g]) for g in range(len(POOL_WINDOWS))], axis=1)
            y_pool = (yp + pb_ref[...]) * ps_ref[...]

            taps = _conv_taps(u_lru, cl)
            cl = u_lru[sub - CONV_HALO:]
            xb = cb_ref[...]
            for q in range(CONV_WIDTH):
                xb = xb + taps[q] * cw_ref[q:q + 1, :]
            xbb = xb.astype(BF16)
            s_r[...] = _gate_dot(xbb, wa_ref) + ba_ref[...]
            s_i[...] = _gate_dot(xbb, wx_ref) + bx_ref[...]
            s_xb[...] = xb
            s_ug[...] = u_gate
            hc = _lru_chunks_fwd(s_r, s_i, s_xb, s_ug, h_ref, s_y, k * sub, sub, lsl8, t0, hc)
            rg_ref[rs, :] = jnp.concatenate([s_r[...], s_i[...]], axis=1).astype(BF16)
            cat = jnp.concatenate([y_pool, s_y[...]], axis=1).astype(BF16)
            h1_ref[rs, :] = xv + _dot(cat, wout_ref[...])
        cpool[...] = cp
        clru[...] = cl
        ch[...] = hc

        @pl.when(i == max(n - PASS_STEPS, 0))
        def _():
            _gather_pass(late_ref, late, ssem, rsem)

        @pl.when(i == n - 1)
        def _():
            _gather_done(late_ref, late, ssem, rsem)

    row = lambda w: pl.BlockSpec((tm, w), lambda i: (i, 0))
    ins = [x, g1, w_in, pool_w, pool_b, pool_scale, conv_w, conv_b, wa, ba, wx, bx, lru_l, w_out]
    outs = pl.pallas_call(
        body, name="fwd_mix", grid=(n,),
        in_specs=[row(D_MODEL)] + [_resident(a.shape) for a in ins[1:]] + [ANY] * nl,
        out_specs=[row(D_IN_PROJ), row(D_LRU), row(D_MODEL), row(2 * D_LRU)] + [ANY] * nl,
        out_shape=[jax.ShapeDtypeStruct((s_len, D_IN_PROJ), BF16), jax.ShapeDtypeStruct((s_len, D_LRU), F32),
                   jax.ShapeDtypeStruct((s_len, D_MODEL), F32), jax.ShapeDtypeStruct((s_len, 2 * D_LRU), BF16)]
        + [jax.ShapeDtypeStruct(a.shape, a.dtype) for a in late],
        input_output_aliases={len(ins) + k: 4 + k for k in range(nl)},
        scratch_shapes=[pltpu.VMEM((POOL_HALO, D_POOL), F32), pltpu.VMEM((CONV_HALO, D_LRU), F32),
                        pltpu.VMEM((8, D_LRU), F32)] + [pltpu.VMEM((sub, D_LRU), F32)] * 5 + _gather_sems(nl),
        compiler_params=_params("arbitrary"),
    )(*ins, *late)
    return outs[:4], outs[4:]


def _fwd_mlp(h1, g2, w_up, w_down, tm, late):
    s_len = h1.shape[0]
    n = s_len // tm
    nl = len(late)

    def body(h1_ref, g2_ref, wup_ref, wdn_ref, *rest):
        z2_ref, ru_ref, h2_ref = rest[nl:nl + 3]
        late_ref = rest[nl + 3:2 * nl + 3]
        ssem, rsem = rest[2 * nl + 3:]
        i = pl.program_id(0)

        @pl.when(i == 0)
        def _():
            _gather_send(late_ref, late, ssem, rsem)

        hv = h1_ref[...]
        zb = (hv * _rstd(hv) * g2_ref[...]).astype(BF16)
        z2_ref[...] = zb
        acc = hv
        for j in range(N_CHIPS):
            ru = jnp.maximum(_dot(zb, wup_ref[j]), 0.0)
            ru_ref[:, FF_BLOCK * j:FF_BLOCK * (j + 1)] = ru.astype(BF16)
            acc = acc + _dot((ru * ru).astype(BF16), wdn_ref[j])
        h2_ref[...] = acc

        @pl.when(i == max(n - PASS_STEPS, 0))
        def _():
            _gather_pass(late_ref, late, ssem, rsem)

        @pl.when(i == n - 1)
        def _():
            _gather_done(late_ref, late, ssem, rsem)

    row = lambda w: pl.BlockSpec((tm, w), lambda i: (i, 0))
    outs = pl.pallas_call(
        body, name="fwd_mlp", grid=(n,),
        in_specs=[row(D_MODEL), _full(g2.shape), _resident(w_up.shape), _resident(w_down.shape)] + [ANY] * nl,
        out_specs=[row(D_MODEL), row(D_FF), row(D_MODEL)] + [ANY] * nl,
        out_shape=[jax.ShapeDtypeStruct((s_len, D_MODEL), BF16), jax.ShapeDtypeStruct((s_len, D_FF), BF16),
                   jax.ShapeDtypeStruct((s_len, D_MODEL), F32)] + [jax.ShapeDtypeStruct(a.shape, a.dtype) for a in late],
        input_output_aliases={4 + k: 3 + k for k in range(nl)},
        scratch_shapes=_gather_sems(nl),
        compiler_params=_params("arbitrary"),
    )(h1, g2, w_up, w_down, *late)
    return outs[:3], outs[3:]


def _head(h2, p, target, g3, w_pg, b_pg, w_pp, g4, tm):
    s_len = h2.shape[0]
    n = s_len // tm

    def body(h2_ref, p_ref, t_ref, g3_ref, wpg_ref, bpg_ref, wpp_ref, g4_ref,
             dh2_ref, dh2b_ref, dwpg_ref, dwpp_ref, vec_ref, a_pg, a_pp, a_vec):
        i = pl.program_id(0)

        @pl.when(i == 0)
        def _():
            a_pg[...] = jnp.zeros_like(a_pg)
            a_pp[...] = jnp.zeros_like(a_pp)
            a_vec[...] = jnp.zeros_like(a_vec)

        h2v = h2_ref[...]
        g3v = g3_ref[...]
        g4v = g4_ref[...]
        z3 = (h2v * _rstd(h2v) * g3v).astype(BF16)
        gate = _sigmoid(_dot(z3, wpg_ref[...]) + bpg_ref[...])
        pb = p_ref[...].astype(BF16)
        pp = _dot(pb, wpp_ref[...])
        h3 = h2v + gate * pp
        r4 = _rstd(h3)
        diff = h3 * r4 * g4v - t_ref[...]
        loss = 0.5 * jnp.sum(jnp.mean(diff * diff, axis=-1, keepdims=True), axis=0, keepdims=True)
        dy = diff * (1.0 / D_MODEL)
        dh3, dg4 = _rms_bwd(h3, g4v, dy)
        dpp = (dh3 * gate).astype(BF16)
        dpre = dh3 * pp * gate * (1.0 - gate)
        dpreb = dpre.astype(BF16)
        dz3 = _dot_nt(dpreb, wpg_ref[...])
        dx, dg3 = _rms_bwd(h2v, g3v, dz3)
        dh2 = dh3 + dx
        dh2_ref[...] = dh2
        dh2b_ref[...] = dh2.astype(BF16)
        a_pg[...] += _dot_tn(z3, dpreb)
        a_pp[...] += _dot_tn(pb, dpp)
        a_vec[0:1, :] += dg3
        a_vec[1:2, :] += dg4
        a_vec[2:3, :] += jnp.sum(dpre, axis=0, keepdims=True)
        a_vec[3:4, :] += jnp.broadcast_to(loss, (1, D_MODEL))

        @pl.when(i == n - 1)
        def _():
            dwpg_ref[...] = a_pg[...].astype(BF16)
            for j in range(N_CHIPS):
                dwpp_ref[j] = a_pp[:, PLE_DIM * j:PLE_DIM * (j + 1)].astype(BF16)
            vec_ref[...] = a_vec[...]

    row = lambda w: pl.BlockSpec((tm, w), lambda i: (i, 0))
    ins = [h2, p, target, g3, w_pg, b_pg, w_pp, g4]
    return pl.pallas_call(
        body, name="head", grid=(n,),
        in_specs=[row(D_MODEL), row(PLE_DIM), row(D_MODEL)] + [_resident(a.shape) for a in ins[3:]],
        out_specs=[row(D_MODEL), row(D_MODEL), _full((D_MODEL, D_MODEL)), _full((N_CHIPS, PLE_DIM, PLE_DIM)),
                   _full((8, D_MODEL))],
        out_shape=[jax.ShapeDtypeStruct((s_len, D_MODEL), F32), jax.ShapeDtypeStruct((s_len, D_MODEL), BF16),
                   jax.ShapeDtypeStruct((D_MODEL, D_MODEL), BF16),
                   jax.ShapeDtypeStruct((N_CHIPS, PLE_DIM, PLE_DIM), BF16), jax.ShapeDtypeStruct((8, D_MODEL), F32)],
        scratch_shapes=[pltpu.VMEM((D_MODEL, D_MODEL), F32), pltpu.VMEM((PLE_DIM, D_MODEL), F32),
                        pltpu.VMEM((8, D_MODEL), F32)],
        compiler_params=_params("arbitrary"),
    )(*ins)


def _bwd_mlp_x(dh2, ru, h1, g2, w_up, w_down, tm):
    s_len = dh2.shape[0]
    n = s_len // tm

    def body(dh2_ref, ru_ref, h1_ref, g2_ref, wup_ref, wdn_ref, dup_ref, dh1_ref, dg2_ref, a_g):
        i = pl.program_id(0)

        @pl.when(i == 0)
        def _():
            a_g[...] = jnp.zeros_like(a_g)

        dh2v = dh2_ref[...]
        dhb = dh2v.astype(BF16)
        acc = jnp.zeros((tm, D_MODEL), F32)
        for j in range(N_CHIPS):
            sl = slice(FF_BLOCK * j, FF_BLOCK * (j + 1))
            dup = (_dot_nt(dhb, wdn_ref[j]) * (2.0 * ru_ref[:, sl].astype(F32))).astype(BF16)
            dup_ref[:, sl] = dup
            acc = acc + _dot_nt(dup, wup_ref[j])
        dx, dg = _rms_bwd(h1_ref[...], g2_ref[...], acc)
        dh1_ref[...] = dh2v + dx
        a_g[0:1, :] += dg

        @pl.when(i == n - 1)
        def _():
            dg2_ref[...] = a_g[...]

    row = lambda w: pl.BlockSpec((tm, w), lambda i: (i, 0))
    return pl.pallas_call(
        body, name="bwd_mlp_x", grid=(n,),
        in_specs=[row(D_MODEL), row(D_FF), row(D_MODEL), _full(g2.shape), _resident(w_up.shape), _resident(w_down.shape)],
        out_specs=[row(D_FF), row(D_MODEL), _full((8, D_MODEL))],
        out_shape=[jax.ShapeDtypeStruct((s_len, D_FF), BF16), jax.ShapeDtypeStruct((s_len, D_MODEL), F32),
                   jax.ShapeDtypeStruct((8, D_MODEL), F32)],
        scratch_shapes=[pltpu.VMEM((8, D_MODEL), F32)],
        compiler_params=_params("arbitrary"),
    )(dh2, ru, h1, g2, w_up, w_down)


def _bwd_mlp_w(z2, dup, ru, dh2, tk):
    s_len = z2.shape[0]
    n = s_len // tk

    def body(z2_ref, dup_ref, ru_ref, dh2_ref, dwup_ref, dwdn_ref, a_up, a_dn):
        t = pl.program_id(1)

        @pl.when(t == 0)
        def _():
            a_up[...] = jnp.zeros_like(a_up)
            a_dn[...] = jnp.zeros_like(a_dn)

        for b in range(FF_PAIR):
            sl = slice(FF_BLOCK * b, FF_BLOCK * (b + 1))
            ruv = ru_ref[:, sl]
            a_up[b] += _dot_tn(z2_ref[...], dup_ref[:, sl])
            a_dn[b] += _dot_tn(ruv * ruv, dh2_ref[...])

        @pl.when(t == n - 1)
        def _():
            dwup_ref[...] = a_up[...].astype(BF16)
            dwdn_ref[...] = a_dn[...].astype(BF16)

    tile = pl.BlockSpec((tk, D_MODEL), lambda j, t: (t, 0))
    ffb = pl.BlockSpec((tk, FF_PAIR * FF_BLOCK), lambda j, t: (t, j))
    return pl.pallas_call(
        body, name="bwd_mlp_w", grid=(N_CHIPS // FF_PAIR, n),
        in_specs=[tile, ffb, ffb, tile],
        out_specs=[pl.BlockSpec((FF_PAIR, D_MODEL, FF_BLOCK), lambda j, t: (j, 0, 0)),
                   pl.BlockSpec((FF_PAIR, FF_BLOCK, D_MODEL), lambda j, t: (j, 0, 0))],
        out_shape=[jax.ShapeDtypeStruct((N_CHIPS, D_MODEL, FF_BLOCK), BF16),
                   jax.ShapeDtypeStruct((N_CHIPS, FF_BLOCK, D_MODEL), BF16)],
        scratch_shapes=[pltpu.VMEM((FF_PAIR, D_MODEL, FF_BLOCK), F32), pltpu.VMEM((FF_PAIR, FF_BLOCK, D_MODEL), F32)],
        compiler_params=_params("arbitrary", "arbitrary"),
    )(z2, dup, ru, dh2)


MIX_VEC_ROWS = 16


def _bwd_mix(dh1, proj, h, rg, pool_w, pool_b, pool_scale, conv_w, conv_b, wa, ba, wx, bx, lru_l, w_out, tm, sub, early):
    s_len = dh1.shape[0]
    n = s_len // tm
    ng = len(POOL_WINDOWS)
    ne_ = len(early)

    def body(dh1_ref, proj_ref, h_ref, rg_ref, projh_ref, hh_ref, pw_ref, pb_ref, ps_ref, cw_ref, cb_ref,
             wa_ref, ba_ref, wx_ref, bx_ref, l_ref, wout_ref, *rest):
        early_ref = rest[:ne_]
        dproj_ref, dwout_ref, dpw_ref, dwa_ref, dwx_ref, vec_ref = rest[ne_:ne_ + 6]
        got_ref = rest[ne_ + 6:2 * ne_ + 6]
        a_out, a_pw, a_wa, a_wx, a_vec, c_g, c_dxb, c_ddc, ssem, rsem = rest[2 * ne_ + 6:]
        q = pl.program_id(0)
        i = n - 1 - q

        @pl.when(q == 0)
        def _():
            _scatter_send(early_ref, got_ref, early, ssem, rsem)
            for r in (a_out, a_pw, a_wa, a_wx, a_vec, c_g, c_dxb, c_ddc):
                r[...] = jnp.zeros_like(r)

        has_prev = (i > 0).astype(F32)
        lv = l_ref[...]
        lsl8 = LRU_C * _log_sigmoid(lv)
        cg, cdxb, cddc = c_g[0:1, :], c_dxb[...], c_ddc[...]
        vec, cats, dhbs = {}, [], []

        def add(row, v):
            vec[row] = v if row not in vec else vec[row] + v

        for k in reversed(range(tm // sub)):
            rs = slice(k * sub, (k + 1) * sub)
            t0 = i * tm + k * sub
            dh1b = dh1_ref[rs, :].astype(BF16)
            dcat = _dot_nt(dh1b, wout_ref[...])
            dy_pool = dcat[:, :D_POOL]
            dy_lru = dcat[:, D_POOL:]

            proj = proj_ref[rs, :].astype(F32)
            u_pool = proj[:, :D_POOL]
            u_lru = proj[:, D_POOL:D_POOL + D_LRU]
            u_gate = proj[:, D_POOL + D_LRU:]
            if k > 0:
                halo = proj_ref[k * sub - POOL_HALO:k * sub, :].astype(F32)
                h_prev_row = h_ref[k * sub - 1:k * sub, :]
            else:
                halo = projh_ref[...].astype(F32) * has_prev
                h_prev_row = hh_ref[7:8, :] * has_prev

            d, inv = _pool_diff(u_pool, halo[:, :D_POOL], t0)
            db = d.astype(BF16)
            ypre = jnp.concatenate(
                [_dot(db[:, POOL_GROUP * g:POOL_GROUP * (g + 1)], pw_ref[g]) for g in range(ng)], axis=1) + pb_ref[...]
            dyp = dy_pool * ps_ref[...]
            dypb = dyp.astype(BF16)
            dds = []
            for g in range(ng):
                sl = slice(POOL_GROUP * g, POOL_GROUP * (g + 1))
                a_pw[g] += _dot_tn(db[:, sl], dypb[:, sl])
                dds.append(_dot_nt(dypb[:, sl], pw_ref[g]))
            du_pool, ddc = _pool_diff_bwd(jnp.concatenate(dds, axis=1), inv, cddc)
            cddc = ddc[:POOL_HALO]
            add(0, jnp.sum(dyp, axis=0, keepdims=True))
            add(1, jnp.sum(dy_pool * ypre, axis=0, keepdims=True))

            taps = _conv_taps(u_lru, halo[POOL_HALO - CONV_HALO:, D_POOL:D_POOL + D_LRU])
            xb = cb_ref[...]
            for c in range(CONV_WIDTH):
                xb = xb + taps[c] * cw_ref[c:c + 1, :]
            r = rg_ref[rs, :D_LRU].astype(F32)
            ig = rg_ref[rs, D_LRU:].astype(F32)
            a, mult, inv_mult, first = _lru_decay(r, lsl8, t0)
            hv = h_ref[rs, :]
            gl, th = _gelu(u_gate)
            du_gate = dy_lru * hv * _gelu_grad(u_gate, th)
            cats.insert(0, jnp.concatenate([ypre * ps_ref[...], hv * gl], axis=1).astype(BF16))
            dhbs.insert(0, dh1b)
            last = _rows(a.shape, 0) == sub - 1
            a_next = jnp.where(last, 1.0, pltpu.roll(a, sub - 1, 0))
            gh = _scan_rev(a_next, dy_lru * gl, cg)
            cg = a[0:1, :] * gh[0:1, :]
            h_prev = jnp.where(_rows(hv.shape, 0) == 0, h_prev_row, pltpu.roll(hv, 1, 0))
            gix = gh * ig * xb
            dla = gh * h_prev * a - jnp.where(first, 0.0, gix * a * a * inv_mult)
            dpre_r = dla * lsl8 * r * (1.0 - r)
            dpre_i = gh * mult * xb * ig * (1.0 - ig)
            dprb = dpre_r.astype(BF16)
            dpib = dpre_i.astype(BF16)
            xbb = xb.astype(BF16)
            for pair in range(GATE_PAIRS):
                sl = slice(POOL_GROUP * pair, POOL_GROUP * (pair + 1))
                a_wa[pair] += _dot_tn(xbb[:, sl], dprb[:, sl])
                a_wx[pair] += _dot_tn(xbb[:, sl], dpib[:, sl])
            dxb = gh * mult * ig + _gate_dot_nt(dprb, wa_ref) + _gate_dot_nt(dpib, wx_ref)
            add(2, jnp.sum(dxb, axis=0, keepdims=True))
            add(3, jnp.sum(dpre_r, axis=0, keepdims=True))
            add(4, jnp.sum(dpre_i, axis=0, keepdims=True))
            add(5, jnp.sum(dla * r, axis=0, keepdims=True))
            ext = jnp.concatenate([dxb, cdxb], axis=0)
            cdxb = dxb[:CONV_HALO]
            ne = sub + CONV_HALO
            du_lru = dxb * cw_ref[CONV_WIDTH - 1:CONV_WIDTH, :]
            for c in range(CONV_WIDTH):
                add(8 + c, jnp.sum(dxb * taps[c], axis=0, keepdims=True))
                if c < CONV_WIDTH - 1:
                    du_lru = du_lru + pltpu.roll(ext, ne - (CONV_WIDTH - 1 - c), 0)[:sub] * cw_ref[c:c + 1, :]
            dproj_ref[rs, :] = jnp.concatenate([du_pool, du_lru, du_gate], axis=1).astype(BF16)
        a_out[...] += _dot_tn(jnp.concatenate(cats, axis=0), jnp.concatenate(dhbs, axis=0))
        c_g[...] = jnp.broadcast_to(cg, c_g.shape)
        c_dxb[...] = cdxb
        c_ddc[...] = cddc
        for row, v in vec.items():
            a_vec[row:row + 1, :] += v

        @pl.when(q == n - 1)
        def _():
            dwout_ref[...] = a_out[...].astype(BF16)
            dpw_ref[...] = a_pw[...]
            dwa_ref[...] = a_wa[...]
            dwx_ref[...] = a_wx[...]
            vec_ref[...] = a_vec[...]
            vec_ref[5:6, :] = a_vec[5:6, :] * (LRU_C * _sigmoid(-lv))
            _scatter_done(got_ref, early, ssem, rsem)

    rev =lambda w: pl.BlockSpec((tm, w), lambda q: (n - 1 - q, 0))
    halo_p = pl.BlockSpec((POOL_HALO, D_IN_PROJ), lambda q: (jnp.maximum((n - 1 - q) * (tm // POOL_HALO) - 1, 0), 0))
    halo_h = pl.BlockSpec((8, D_LRU), lambda q: (jnp.maximum((n - 1 - q) * (tm // 8) - 1, 0), 0))
    wts = [pool_w, pool_b, pool_scale, conv_w, conv_b, wa, ba, wx, bx, lru_l, w_out]
    outs = pl.pallas_call(
        body, name="bwd_mix", grid=(n,),
        in_specs=[rev(D_MODEL), rev(D_IN_PROJ), rev(D_LRU), rev(2 * D_LRU), halo_p, halo_h] + [_resident(a.shape) for a in wts]
        + [ANY] * ne_,
        out_specs=[rev(D_IN_PROJ), _full((D_MODEL, D_MODEL)), _full((ng, POOL_GROUP, POOL_GROUP)),
                   _full(wa.shape), _full(wa.shape), _full((MIX_VEC_ROWS, D_LRU))] + [ANY] * ne_,
        out_shape=[jax.ShapeDtypeStruct((s_len, D_IN_PROJ), BF16), jax.ShapeDtypeStruct((D_MODEL, D_MODEL), BF16),
                   jax.ShapeDtypeStruct((ng, POOL_GROUP, POOL_GROUP), F32), jax.ShapeDtypeStruct(wa.shape, F32),
                   jax.ShapeDtypeStruct(wa.shape, F32), jax.ShapeDtypeStruct((MIX_VEC_ROWS, D_LRU), F32)]
        + _scatter_shapes(early),
        scratch_shapes=[pltpu.VMEM((D_MODEL, D_MODEL), F32), pltpu.VMEM((ng, POOL_GROUP, POOL_GROUP), F32),
                        pltpu.VMEM(wa.shape, F32), pltpu.VMEM(wa.shape, F32),
                        pltpu.VMEM((MIX_VEC_ROWS, D_LRU), F32), pltpu.VMEM((8, D_LRU), F32),
                        pltpu.VMEM((CONV_HALO, D_LRU), F32), pltpu.VMEM((POOL_HALO, D_POOL), F32)] + _scatter_sems(ne_),
        compiler_params=_params("arbitrary"),
    )(dh1, proj, h, rg, proj, h, *wts, *early)
    return outs[:6], outs[6:]


def _bwd_in(dproj, x, dh1, g1, w_in, tm):
    s_len = x.shape[0]
    n = s_len // tm
    cb = D_IN_PROJ // N_CHIPS

    def body(dp_ref, x_ref, dh1_ref, g1_ref, win_ref, dx_ref, dwin_ref, dg1_ref, a_w, a_g):
        i = pl.program_id(0)

        @pl.when(i == 0)
        def _():
            a_w[...] = jnp.zeros_like(a_w)
            a_g[...] = jnp.zeros_like(a_g)

        dp = dp_ref[...]
        xv = x_ref[...]
        zb = (xv * _rstd(xv) * g1_ref[...]).astype(BF16)
        dz = jnp.zeros((tm, D_MODEL), F32)
        for j in range(N_CHIPS):
            dpj = dp[:, cb * j:cb * (j + 1)]
            dz = dz + _dot_nt(dpj, win_ref[j])
            a_w[j] += _dot_tn(zb, dpj)
        dx, dg = _rms_bwd(xv, g1_ref[...], dz)
        dx_ref[...] = dh1_ref[...] + dx
        a_g[0:1, :] += dg

        @pl.when(i == n - 1)
        def _():
            dwin_ref[...] = a_w[...].astype(BF16)
            dg1_ref[...] = a_g[...]

    row = lambda w: pl.BlockSpec((tm, w), lambda i: (i, 0))
    return pl.pallas_call(
        body, name="bwd_in", grid=(n,),
        in_specs=[row(D_IN_PROJ), row(D_MODEL), row(D_MODEL), _resident(g1.shape), _resident(w_in.shape)],
        out_specs=[row(D_MODEL), _full(w_in.shape), _full((8, D_MODEL))],
        out_shape=[jax.ShapeDtypeStruct((s_len, D_MODEL), F32), jax.ShapeDtypeStruct(w_in.shape, BF16),
                   jax.ShapeDtypeStruct((8, D_MODEL), F32)],
        scratch_shapes=[pltpu.VMEM(w_in.shape, F32), pltpu.VMEM((8, D_MODEL), F32)],
        compiler_params=_params("arbitrary"),
    )(dproj, x, dh1, g1, w_in)


def _place():
    x, y, c = lax.axis_index("x"), lax.axis_index("y"), lax.axis_index("c")
    chips = [(1 - x, y), (x, 1 - y), (1 - x, 1 - y)]
    return x, y, c, chips


def _rcopy(src, dst, ssem, rsem, dev):
    return pltpu.make_async_remote_copy(src_ref=src, dst_ref=dst, send_sem=ssem, recv_sem=rsem,
                                        device_id=dev, device_id_type=MESH)


ANY = pl.BlockSpec(memory_space=pl.ANY)
COPY_CHUNK_BYTES = 128 * 1024
ROW_ALIGN = 16


def _row_chunks(rows, row_bytes):
    per = max(ROW_ALIGN, (COPY_CHUNK_BYTES // row_bytes) // ROW_ALIGN * ROW_ALIGN)
    return [(r0, min(per, rows - r0)) for r0 in range(0, rows, per)]


def _row_bytes(a):
    return a.shape[-1] * jnp.dtype(a.dtype).itemsize


def _stack_own(shards, dtypes, pos, steps, tag, late=()):
    nw = len(shards)
    nl = len(late)

    def body(pos_ref, *refs):
        outs = refs[nw + nl:2 * nw + nl]
        late_ref = refs[2 * nw + nl:2 * nw + 2 * nl]
        i = pl.program_id(0)
        if nl:
            ssem, rsem = refs[2 * nw + 2 * nl:]

            @pl.when(i == 0)
            def _():
                _gather_send(late_ref, late, ssem, rsem)

        for w in range(nw):
            outs[w][0] = refs[w][...].astype(dtypes[w])

        if nl:
            @pl.when(i == max(steps - PASS_STEPS, 0))
            def _():
                _gather_pass(late_ref, late, ssem, rsem)

            @pl.when(i == steps - 1)
            def _():
                _gather_done(late_ref, late, ssem, rsem)

    def split(s):
        return s.shape[0] % (steps * ROW_ALIGN) == 0

    ins = [pl.BlockSpec((s.shape[0] // steps, s.shape[1]), lambda i, p: (i, 0)) if split(s)
           else pl.BlockSpec(s.shape, lambda i, p: (0, 0)) for s in shards]
    outs = [pl.BlockSpec((1, s.shape[0] // steps, s.shape[1]), lambda i, p: (p[0], i, 0)) if split(s)
            else pl.BlockSpec((1,) + s.shape, lambda i, p: (p[0], 0, 0)) for s in shards]
    res = pl.pallas_call(
        body, name="stack_own_" + tag,
        grid_spec=pltpu.PrefetchScalarGridSpec(
            num_scalar_prefetch=1, grid=(steps,), in_specs=ins + [ANY] * nl, out_specs=outs + [ANY] * nl,
            scratch_shapes=_gather_sems(nl) if nl else []),
        out_shape=[jax.ShapeDtypeStruct((N_CHIPS,) + s.shape, d) for s, d in zip(shards, dtypes)]
        + [jax.ShapeDtypeStruct(a.shape, a.dtype) for a in late],
        input_output_aliases={1 + nw + k: nw + k for k in range(nl)},
        compiler_params=_params("arbitrary"),
    )(pos, *shards, *late)
    return res[:nw], res[nw:]


def _gather_send(outs, stacks, ssem, rsem):
    x, y, c, chips = _place()
    me = 2 * x + y
    for w, st in enumerate(stacks):
        half = st.shape[1] // 2
        for s, (px, py) in enumerate(chips):
            for r0, rs in _row_chunks(half, _row_bytes(st)):
                piece = outs[w].at[me, pl.ds(c * half + r0, rs)]
                _rcopy(piece, piece, ssem.at[w, s], rsem.at[w, s], (px, py, c)).start()


def _gather_pass(outs, stacks, ssem, rsem):
    x, y, c, chips = _place()
    sib = (x, y, 1 - c)
    for w, st in enumerate(stacks):
        half = st.shape[1] // 2
        for s, (px, py) in enumerate(chips):
            blk = outs[w].at[2 * px + py, pl.ds(c * half, half)]
            _rcopy(blk, blk, ssem.at[w, s], rsem.at[w, s], sib).wait_recv()
            for r0, rs in _row_chunks(half, _row_bytes(st)):
                piece = outs[w].at[2 * px + py, pl.ds(c * half + r0, rs)]
                _rcopy(piece, piece, ssem.at[w, 3 + s], rsem.at[w, 3 + s], sib).start()


def _gather_done(outs, stacks, ssem, rsem):
    x, y, c, chips = _place()
    sib = (x, y, 1 - c)
    for w, st in enumerate(stacks):
        half = st.shape[1] // 2
        for s, (px, py) in enumerate(chips):
            blk = outs[w].at[2 * px + py, pl.ds((1 - c) * half, half)]
            _rcopy(blk, blk, ssem.at[w, 3 + s], rsem.at[w, 3 + s], sib).wait_recv()
    for w, st in enumerate(stacks):
        half = st.shape[1] // 2
        blk = outs[w].at[0, pl.ds(0, half)]
        for s in range(6):
            _rcopy(blk, blk, ssem.at[w, s], rsem.at[w, s], sib).wait_send()


def _gather_sems(nw):
    return [pltpu.SemaphoreType.DMA((nw, 6)), pltpu.SemaphoreType.DMA((nw, 6))]


def _swap_halves(grads, tag):
    nw = len(grads)

    def body(*refs):
        ins, got = refs[:nw], refs[nw:2 * nw]
        ssem, rsem = refs[2 * nw:]
        x, y, c, _ = _place()
        cps = []
        for w in range(nw):
            hr = grads[w].shape[1] // 2
            for k in range(N_CHIPS):
                for r0, rs in _row_chunks(hr, _row_bytes(grads[w])):
                    _rcopy(ins[w].at[k, pl.ds((1 - c) * hr + r0, rs)], got[w].at[k, pl.ds(r0, rs)],
                           ssem.at[w], rsem.at[w], (x, y, 1 - c)).start()
            cps.append(_rcopy(got[w], got[w], ssem.at[w], rsem.at[w], (x, y, 1 - c)))
        for cp in cps:
            cp.wait()

    return pl.pallas_call(
        body, name="swap_halves_" + tag,
        in_specs=[ANY] * nw, out_specs=[ANY] * nw,
        out_shape=[jax.ShapeDtypeStruct((g.shape[0], g.shape[1] // 2, g.shape[2]), g.dtype) for g in grads],
        scratch_shapes=[pltpu.SemaphoreType.DMA((nw,)), pltpu.SemaphoreType.DMA((nw,))],
    )(*grads)


def _add_pairs(grads, got, pos, steps, tag):
    nw = len(grads)

    def body(pos_ref, *refs):
        for w in range(nw):
            refs[2 * nw + w][...] = (refs[w][...].astype(F32) + refs[nw + w][...].astype(F32)).astype(BF16)

    blk = lambda a: (a.shape[0], a.shape[1] // steps, a.shape[2])
    own = [pl.BlockSpec(blk(a), lambda i, p: (0, p[1] * steps + i, 0)) for a in got]
    rec = [pl.BlockSpec(blk(a), lambda i, p: (0, i, 0)) for a in got]
    return pl.pallas_call(
        body, name="add_pairs_" + tag,
        grid_spec=pltpu.PrefetchScalarGridSpec(num_scalar_prefetch=1, grid=(steps,), in_specs=own + rec, out_specs=rec),
        out_shape=[jax.ShapeDtypeStruct(a.shape, BF16) for a in got],
        compiler_params=_params("arbitrary"),
    )(pos, *grads, *got)


def _scatter_send(ins, got, parts, ssem, rsem):
    x, y, c, chips = _place()
    for w, p in enumerate(parts):
        for s, (px, py) in enumerate(chips):
            for r0, rs in _row_chunks(p.shape[1], _row_bytes(p)):
                _rcopy(ins[w].at[2 * px + py, pl.ds(r0, rs)], got[w].at[s, pl.ds(r0, rs)],
                       ssem.at[w, s], rsem.at[w, s], (px, py, c)).start()


def _scatter_done(got, parts, ssem, rsem):
    x, y, c, chips = _place()
    for w in range(len(parts)):
        for s, (px, py) in enumerate(chips):
            _rcopy(got[w].at[s], got[w].at[s], ssem.at[w, s], rsem.at[w, s], (px, py, c)).wait()


def _scatter_sems(nw):
    return [pltpu.SemaphoreType.DMA((nw, 3)), pltpu.SemaphoreType.DMA((nw, 3))]


def _scatter_shapes(parts):
    return [jax.ShapeDtypeStruct((3,) + p.shape[1:], p.dtype) for p in parts]


def _scatter_chips(parts):
    nw = len(parts)

    def body(*refs):
        ins, got = refs[:nw], refs[nw:2 * nw]
        ssem, rsem = refs[2 * nw:]
        _scatter_send(ins, got, parts, ssem, rsem)
        _scatter_done(got, parts, ssem, rsem)

    return pl.pallas_call(
        body, name="scatter_chips",
        in_specs=[ANY] * nw, out_specs=[ANY] * nw, out_shape=_scatter_shapes(parts),
        scratch_shapes=_scatter_sems(nw),
    )(*parts)


def _sum_chips(parts, got, pos, steps):
    nw = len(parts)

    def body(pos_ref, *refs):
        for w in range(nw):
            acc = refs[w][0].astype(F32)
            for s in range(3):
                acc = acc + refs[nw + w][s].astype(F32)
            refs[2 * nw + w][...] = acc

    own = [pl.BlockSpec((1, p.shape[1] // steps, p.shape[2]), lambda i, ps: (ps[0], i, 0)) for p in parts]
    rec = [pl.BlockSpec((3, p.shape[1] // steps, p.shape[2]), lambda i, ps: (0, i, 0)) for p in parts]
    outs = [pl.BlockSpec((p.shape[1] // steps, p.shape[2]), lambda i, ps: (ps[1] * steps + i, 0)) for p in parts]
    return pl.pallas_call(
        body, name="sum_chips",
        grid_spec=pltpu.PrefetchScalarGridSpec(num_scalar_prefetch=1, grid=(steps,), in_specs=own + rec, out_specs=outs),
        out_shape=[jax.ShapeDtypeStruct((2 * p.shape[1], p.shape[2]), F32) for p in parts],
        compiler_params=_params("arbitrary"),
    )(pos, *parts, *got)


def _join_halves(shards):
    nw = len(shards)

    def body(*refs):
        outs = refs[nw:2 * nw]
        ssem, rsem = refs[2 * nw:]
        x, y, c, _ = _place()
        cps = []
        for w in range(nw):
            hr = shards[w].shape[0] // 2
            for r0, rs in _row_chunks(hr, _row_bytes(shards[w])):
                piece = outs[w].at[pl.ds(c * hr + r0, rs)]
                _rcopy(piece, piece, ssem.at[w], rsem.at[w], (x, y, 1 - c)).start()
            mine = outs[w].at[pl.ds(c * hr, hr)]
            cps.append(_rcopy(mine, mine, ssem.at[w], rsem.at[w], (x, y, 1 - c)))
        for cp in cps:
            cp.wait()

    return pl.pallas_call(
        body, name="join_halves",
        in_specs=[ANY] * nw, out_specs=[ANY] * nw,
        out_shape=[jax.ShapeDtypeStruct(s.shape, F32) for s in shards],
        input_output_aliases={w: w for w in range(nw)},
        scratch_shapes=[pltpu.SemaphoreType.DMA((nw,)), pltpu.SemaphoreType.DMA((nw,))],
    )(*shards)


def _scatter_and_allreduce(parts, packed):
    nw = len(parts)
    rows = packed.shape[0]
    half = rows // 2

    def body(*refs):
        ins, p_ref = refs[:nw], refs[nw]
        got, out_ref = refs[nw + 1:2 * nw + 1], refs[2 * nw + 1]
        rfull, rhalf, ssem, rsem, bsem_s, bsem_r = refs[2 * nw + 2:]
        x, y, c, _ = _place()
        sib = (x, y, 1 - c)
        _scatter_send(ins, got, parts, ssem, rsem)
        out_ref[...] = p_ref[...]
        cp = _rcopy(out_ref, rfull, bsem_s.at[0], bsem_r.at[0], sib)
        cp.start()
        cp.wait()
        out_ref[...] = out_ref[...] + rfull[...]
        mine = pl.ds(pl.multiple_of(c * half, 8), half)
        other = pl.ds(pl.multiple_of((1 - c) * half, 8), half)
        for st, peer in enumerate([(1 - x, y, c), (x, 1 - y, c)]):
            cp = _rcopy(out_ref.at[mine], rhalf.at[st], bsem_s.at[1 + st], bsem_r.at[1 + st], peer)
            cp.start()
            cp.wait()
            out_ref[mine, :] = out_ref[mine, :] + rhalf[st]
        cp = _rcopy(out_ref.at[mine], rhalf.at[2], bsem_s.at[3], bsem_r.at[3], sib)
        cp.start()
        cp.wait()
        out_ref[other, :] = rhalf[2]
        _scatter_done(got, parts, ssem, rsem)

    vm = pl.BlockSpec(memory_space=pltpu.VMEM)
    outs = pl.pallas_call(
        body, name="scatter_and_allreduce",
        in_specs=[ANY] * nw + [vm], out_specs=[ANY] * nw + [vm],
        out_shape=_scatter_shapes(parts) + [jax.ShapeDtypeStruct(packed.shape, F32)],
        scratch_shapes=[pltpu.VMEM(packed.shape, F32), pltpu.VMEM((3, half, packed.shape[1]), F32)] + _scatter_sems(nw)
        + [pltpu.SemaphoreType.DMA((4,)), pltpu.SemaphoreType.DMA((4,))],
        compiler_params=pltpu.CompilerParams(vmem_limit_bytes=VMEM_LIMIT),
    )(*parts, packed)
    return outs[:nw], outs[nw]


def _adamw_math(w, g, m, v):
    m = ADAM_B1 * m + (1.0 - ADAM_B1) * g
    v = ADAM_B2 * v + (1.0 - ADAM_B2) * (g * g)
    delta = -ADAM_LR * ((m * ADAM_C1) / (jnp.sqrt(v * ADAM_C2) + ADAM_EPS) + ADAM_WD * w)
    return delta, m, v


def _adamw(ws, gs, ms, vs, steps, name):
    nw = len(ws)

    def body(*refs):
        for k in range(nw):
            g = refs[nw + k][...]
            d, m, v = _adamw_math(refs[k][...], g, refs[2 * nw + k][...], refs[3 * nw + k][...])
            refs[4 * nw + k][...] = d
            refs[5 * nw + k][...] = m
            refs[6 * nw + k][...] = v
            refs[7 * nw + k][...] = g

    specs = [pl.BlockSpec((a.shape[0] // steps, a.shape[1]), lambda i: (i, 0)) for a in ws]
    shapes = [jax.ShapeDtypeStruct(a.shape, F32) for a in ws]
    outs = pl.pallas_call(
        body, name=name, grid=(steps,),
        in_specs=specs * 4, out_specs=specs * 4, out_shape=shapes * 4,
        compiler_params=_params("arbitrary"),
    )(*ws, *gs, *ms, *vs)
    return outs[:nw], outs[nw:2 * nw], outs[2 * nw:3 * nw], outs[3 * nw:]


SMALL = ["norm_mix_g", "pool_w", "pool_b", "pool_scale", "conv_b", "gate_a_w", "gate_a_b", "gate_x_w", "gate_x_b",
         "lru_L", "norm_mlp_g", "norm_ple_g", "b_ple_gate", "norm_final_g"]
BIG = ["w_in", "w_out", "w_up", "w_down", "w_ple_gate", "w_ple_proj"]
ORDER = ["norm_mix_g", "w_in", "pool_w", "pool_b", "pool_scale", "conv_w", "conv_b", "gate_a_w", "gate_a_b", "gate_x_w",
         "gate_x_b", "lru_L", "w_out", "norm_mlp_g", "w_up", "w_down", "norm_ple_g", "w_ple_gate", "b_ple_gate",
         "w_ple_proj", "norm_final_g"]
LANES = 128


def _pair_blocks(w):
    eye = jnp.eye(2, dtype=w.dtype)
    w4 = w.reshape(GATE_PAIRS, 2, LRU_BLOCK, LRU_BLOCK)
    return jnp.einsum("qaij,ab->qaibj", w4, eye).reshape(GATE_PAIRS, POOL_GROUP, POOL_GROUP)


def _diag_blocks(pairs):
    f = pairs.reshape(GATE_PAIRS, 2, LRU_BLOCK, 2, LRU_BLOCK)
    return jnp.stack([f[h // 2, h % 2, :, h % 2, :] for h in range(LRU_HEADS)])


def _rows128(a):
    return a.reshape(-1, LANES)


def _pad8(a):
    r = (-a.shape[0]) % 8
    return jnp.pad(a, ((0, r), (0, 0))) if r else a


def kernel(x, p, norm_mix_g, w_in, pool_w, pool_b, pool_scale, conv_w, conv_b, gate_a_w, gate_a_b, gate_x_w, gate_x_b, lru_L, w_out, norm_mlp_g, w_up, w_down, norm_ple_g, w_ple_gate, b_ple_gate, w_ple_proj, norm_final_g, loss_target, m_norm_mix_g, m_w_in, m_pool_w, m_pool_b, m_pool_scale, m_conv_w, m_conv_b, m_gate_a_w, m_gate_a_b, m_gate_x_w, m_gate_x_b, m_lru_L, m_w_out, m_norm_mlp_g, m_w_up, m_w_down, m_norm_ple_g, m_w_ple_gate, m_b_ple_gate, m_w_ple_proj, m_norm_final_g, v_norm_mix_g, v_w_in, v_pool_w, v_pool_b, v_pool_scale, v_conv_w, v_conv_b, v_gate_a_w, v_gate_a_b, v_gate_x_w, v_gate_x_b, v_lru_L, v_w_out, v_norm_mlp_g, v_w_up, v_w_down, v_norm_ple_g, v_w_ple_gate, v_b_ple_gate, v_w_ple_proj, v_norm_final_g):
    W = dict(norm_mix_g=norm_mix_g, w_in=w_in, pool_w=pool_w, pool_b=pool_b, pool_scale=pool_scale, conv_w=conv_w,
             conv_b=conv_b, gate_a_w=gate_a_w, gate_a_b=gate_a_b, gate_x_w=gate_x_w, gate_x_b=gate_x_b, lru_L=lru_L,
             w_out=w_out, norm_mlp_g=norm_mlp_g, w_up=w_up, w_down=w_down, norm_ple_g=norm_ple_g,
             w_ple_gate=w_ple_gate, b_ple_gate=b_ple_gate, w_ple_proj=w_ple_proj, norm_final_g=norm_final_g)
    M = dict(norm_mix_g=m_norm_mix_g, w_in=m_w_in, pool_w=m_pool_w, pool_b=m_pool_b, pool_scale=m_pool_scale,
             conv_w=m_conv_w, conv_b=m_conv_b, gate_a_w=m_gate_a_w, gate_a_b=m_gate_a_b, gate_x_w=m_gate_x_w,
             gate_x_b=m_gate_x_b, lru_L=m_lru_L, w_out=m_w_out, norm_mlp_g=m_norm_mlp_g, w_up=m_w_up, w_down=m_w_down,
             norm_ple_g=m_norm_ple_g, w_ple_gate=m_w_ple_gate, b_ple_gate=m_b_ple_gate, w_ple_proj=m_w_ple_proj,
             norm_final_g=m_norm_final_g)
    V = dict(norm_mix_g=v_norm_mix_g, w_in=v_w_in, pool_w=v_pool_w, pool_b=v_pool_b, pool_scale=v_pool_scale,
             conv_w=v_conv_w, conv_b=v_conv_b, gate_a_w=v_gate_a_w, gate_a_b=v_gate_a_b, gate_x_w=v_gate_x_w,
             gate_x_b=v_gate_x_b, lru_L=v_lru_L, w_out=v_w_out, norm_mlp_g=v_norm_mlp_g, w_up=v_w_up, w_down=v_w_down,
             norm_ple_g=v_norm_ple_g, w_ple_gate=v_w_ple_gate, b_ple_gate=v_b_ple_gate, w_ple_proj=v_w_ple_proj,
             norm_final_g=v_norm_final_g)

    s_len = x.shape[1]
    sub_mix = min(256, s_len)
    tm = min(512, s_len)
    chip = (2 * lax.axis_index("x") + lax.axis_index("y")).astype(jnp.int32)
    pos = jnp.stack([chip, lax.axis_index("c").astype(jnp.int32)])

    shards = [w_in[0], w_out[0], w_up[0], w_down[0], w_ple_gate[0], w_ple_proj[0], jnp.pad(conv_w[0], ((0, 12), (0, 0)))]
    first, _ = _stack_own([shards[0], shards[1], shards[6]], [BF16, BF16, F32], pos, 8, "first")
    (st_up, st_dn, st_pg, st_pp), (win_g, wout_g, cw_g) = _stack_own(shards[2:6], [BF16] * 4, pos, 8, "rest", first)
    wout_f = wout_g.reshape(D_MODEL, D_MODEL)
    cw_f = jnp.transpose(cw_g[:, :CONV_WIDTH], (1, 0, 2)).reshape(CONV_WIDTH, D_LRU)
    pw_b = pool_w[0].astype(BF16)
    wa_b = _pair_blocks(gate_a_w[0]).astype(BF16)
    wx_b = _pair_blocks(gate_x_w[0]).astype(BF16)
    pb_r = pool_b.reshape(1, D_POOL)
    ba_r = gate_a_b.reshape(1, D_LRU)
    bx_r = gate_x_b.reshape(1, D_LRU)
    g4 = norm_final_g.reshape(1, D_MODEL)
    mix_w = (pw_b, pb_r, pool_scale, cw_f, conv_b, wa_b, ba_r, wx_b, bx_r, lru_L, wout_f)

    xs, ps, ts = x[0], p[0, 0], loss_target[0]
    (proj, hst, h1, rg), (wup_g, wdn_g) = _fwd_mix(xs, norm_mix_g, win_g, *mix_w, tm, sub_mix, [st_up, st_dn])
    (z2, ru, h2), (wpg_g, wpp_g) = _fwd_mlp(h1, norm_mlp_g, wup_g, wdn_g, tm, [st_pg, st_pp])
    wpg_f = wpg_g.reshape(D_MODEL, D_MODEL)
    wpp_f = jnp.transpose(wpp_g, (1, 0, 2)).reshape(PLE_DIM, D_MODEL)
    dh2, dh2b, d_wpg, d_wpp, head_vec = _head(h2, ps, ts, norm_ple_g, wpg_f, b_ple_gate, wpp_f, g4, tm)
    dup, dh1, mlp_vec = _bwd_mlp_x(dh2, ru, h1, norm_mlp_g, wup_g, wdn_g, tm)
    d_wup, d_wdn = _bwd_mlp_w(z2, dup, ru, dh2b, tm)
    early = [d_wup, d_wdn, d_wpg.reshape(N_CHIPS, D_MODEL // N_CHIPS, D_MODEL), d_wpp]
    pair_e = _add_pairs(early, _swap_halves(early, "early"), pos, 8, "early")
    (dproj, d_wout, d_pw, d_wa, d_wx, mix_vec), got_e = _bwd_mix(dh1, proj, hst, rg, *mix_w, tm, sub_mix, pair_e)

    dx, d_win, in_vec = _bwd_in(dproj, xs, dh1, norm_mix_g, win_g, tm)

    last = [d_win, d_wout.reshape(N_CHIPS, D_MODEL // N_CHIPS, D_MODEL)]
    pair_l = _add_pairs(last, _swap_halves(last, "last"), pos, 8, "last")

    g_small = {
        "norm_mix_g": in_vec[0:1], "pool_w": d_pw, "pool_b": mix_vec[0:1], "pool_scale": mix_vec[1:2],
        "conv_b": mix_vec[2:3], "gate_a_w": _diag_blocks(d_wa), "gate_a_b": mix_vec[3:4],
        "gate_x_w": _diag_blocks(d_wx), "gate_x_b": mix_vec[4:5], "lru_L": mix_vec[5:6], "norm_mlp_g": mlp_vec[0:1],
        "norm_ple_g": head_vec[0:1], "b_ple_gate": head_vec[2:3], "norm_final_g": head_vec[1:2],
    }
    d_cw = jnp.transpose(mix_vec[8:8 + CONV_WIDTH].reshape(CONV_WIDTH, N_CHIPS, LANES), (1, 0, 2)).reshape(-1, LANES)
    pieces = [_pad8(_rows128(g_small[k])) for k in SMALL] + [d_cw, _pad8(head_vec[3:4, :LANES])]
    offs = [0]
    for pc in pieces:
        offs.append(offs[-1] + pc.shape[0])
    if offs[-1] % 16:
        pieces.append(jnp.zeros((8, LANES), F32))
    got_l, red = _scatter_and_allreduce(pair_l, jnp.concatenate(pieces, axis=0))
    g_big = _join_halves(_sum_chips(pair_l + pair_e, got_l + got_e, pos, 8))
    loss = red[offs[-2], 0]
    g_cw = lax.dynamic_slice(red, (offs[len(SMALL)] + CONV_WIDTH * chip, 0), (CONV_WIDTH, LANES))

    def packed(src):
        return jnp.concatenate([_pad8(_rows128(src[k])) for k in SMALL] + [_pad8(src["conv_w"][0])], axis=0)

    n_small = offs[len(SMALL)]
    g_pack = jnp.concatenate([red[:n_small], _pad8(g_cw)], axis=0)
    (d_pack,), (m_pack,), (v_pack,), _ = _adamw([packed(W)], [g_pack], [packed(M)], [packed(V)], 1, "adamw_small")

    big2d = lambda src: [src[k][0] for k in BIG]
    d_big, m_big, v_big, g_big = _adamw(big2d(W), g_big, big2d(M), big2d(V), 8, "adamw_big")

    def unpack(pack, big_list):
        out = {}
        for idx, k in enumerate(SMALL):
            n_el = W[k].size
            out[k] = pack[offs[idx]:offs[idx + 1]].reshape(-1)[:n_el].reshape(W[k].shape)
        out["conv_w"] = pack[n_small:n_small + CONV_WIDTH].reshape(W["conv_w"].shape)
        for k, a in zip(BIG, big_list):
            out[k] = a.reshape(W[k].shape)
        return out

    grads = unpack(g_pack, g_big)
    deltas = unpack(d_pack, d_big)
    new_m = unpack(m_pack, m_big)
    new_v = unpack(v_pack, v_big)
    return (loss, dx[None], *[grads[k] for k in ORDER], *[deltas[k] for k in ORDER],
            *[new_m[k] for k in ORDER], *[new_v[k] for k in ORDER])
```

```python
import functools

import jax
import jax.numpy as jnp
from jax import lax
from jax.experimental import pallas as pl
from jax.experimental.pallas import tpu as pltpu

F32 = jnp.float32
BF16 = jnp.bfloat16
MESH = pl.DeviceIdType.MESH

D_MODEL = 1024
D_POOL = 512
D_LRU = 512
POOL_WINDOWS = (2, 4, 8, 16)
POOL_GROUP = 128
POOL_HALO = 16
CONV_WIDTH = 4
CONV_HALO = 8
PASS_STEPS = 2
LRU_HEADS = 8
LRU_BLOCK = 64
GATE_PAIRS = 4
LRU_C = 8.0
LRU_UNROLL = 4
D_FF = 4096
PLE_DIM = 256
D_IN_PROJ = 1536
RMS_EPS = 1e-6
N_CHIPS = 4
FF_BLOCK = D_FF // N_CHIPS
FF_PAIR = 2

ADAM_LR = 0.001
ADAM_B1 = 0.9
ADAM_B2 = 0.999
ADAM_EPS = 1e-08
ADAM_WD = 0.01
ADAM_STEP = 10
ADAM_C1 = 1.0 / (1.0 - ADAM_B1 ** ADAM_STEP)
ADAM_C2 = 1.0 / (1.0 - ADAM_B2 ** ADAM_STEP)

VMEM_LIMIT = 56 * 1024 * 1024
GELU_C = 0.7978845608028654
GELU_A = 0.044715

NT = (((1,), (1,)), ((), ()))
TN = (((0,), (0,)), ((), ()))


def _dot(a, b):
    return jnp.dot(a, b, preferred_element_type=F32)


def _dot_nt(a, b):
    return lax.dot_general(a, b, NT, preferred_element_type=F32)


def _dot_tn(a, b):
    return lax.dot_general(a, b, TN, preferred_element_type=F32)


def _params(*sem):
    return pltpu.CompilerParams(dimension_semantics=sem, vmem_limit_bytes=VMEM_LIMIT)


def _full(shape):
    nd = len(shape)
    return pl.BlockSpec(shape, lambda *_: (0,) * nd)


def _resident(shape):
    nd = len(shape)
    return pl.BlockSpec(shape, lambda *_: (0,) * nd, pipeline_mode=pl.Buffered(1))


def _rstd(x):
    return lax.rsqrt(jnp.mean(x * x, axis=-1, keepdims=True) + RMS_EPS)


def _rms_bwd(x, g, dz):
    xr = x * _rstd(x)
    r = _rstd(x)
    dyg = dz * g
    dx = r * (dyg - xr * jnp.mean(dyg * xr, axis=-1, keepdims=True))
    dg = jnp.sum(dz * xr, axis=0, keepdims=True)
    return dx, dg


def _sigmoid(x):
    return 1.0 / (1.0 + jnp.exp(-x))


def _log_sigmoid(v):
    u = jnp.exp(-jnp.abs(v))
    w = 1.0 + u
    l1p = jnp.where(w == 1.0, u, jnp.log(w) * u / jnp.where(w == 1.0, 1.0, w - 1.0))
    return jnp.minimum(v, 0.0) - l1p


def _gelu(x):
    t = jnp.tanh(GELU_C * (x + GELU_A * x * x * x))
    return 0.5 * x * (1.0 + t), t


def _gelu_grad(x, t):
    return 0.5 * (1.0 + t) + 0.5 * x * (1.0 - t * t) * GELU_C * (1.0 + 3.0 * GELU_A * x * x)


def _rows(shape, t0):
    return lax.broadcasted_iota(jnp.int32, shape, 0) + t0


def _pool_inv(tm, t0):
    rows = _rows((tm, POOL_GROUP), t0)
    return jnp.concatenate([1.0 / jnp.minimum(rows + 1, w).astype(F32) for w in POOL_WINDOWS], axis=1)


def _pool_diff(u_pool, prev, t0):
    tm = u_pool.shape[0]
    rows = _rows((tm, POOL_GROUP), t0)
    outs, invs = [], []
    for g, w in enumerate(POOL_WINDOWS):
        sl = slice(POOL_GROUP * g, POOL_GROUP * (g + 1))
        ug = u_pool[:, sl]
        s = jnp.concatenate([prev[:, sl], ug], axis=0)
        k = 1
        while k < w:
            s = s + pltpu.roll(s, k, 0)
            k *= 2
        inv = 1.0 / jnp.minimum(rows + 1, w).astype(F32)
        outs.append(s[POOL_HALO:] * inv - ug)
        invs.append(inv)
    return jnp.concatenate(outs, axis=1), jnp.concatenate(invs, axis=1)


def _pool_diff_bwd(dd, inv, nxt):
    ddc = dd * inv
    outs = []
    for g, w in enumerate(POOL_WINDOWS):
        sl = slice(POOL_GROUP * g, POOL_GROUP * (g + 1))
        s = jnp.concatenate([ddc[:, sl], nxt[:, sl]], axis=0)
        n = s.shape[0]
        k = 1
        while k < w:
            s = s + pltpu.roll(s, n - k, 0)
            k *= 2
        outs.append(s[:n - POOL_HALO] - dd[:, sl])
    return jnp.concatenate(outs, axis=1), ddc


def _conv_taps(u, prev):
    ext = jnp.concatenate([prev, u], axis=0)
    return [pltpu.roll(ext, CONV_WIDTH - 1 - k, 0)[CONV_HALO:] if k < CONV_WIDTH - 1 else u for k in range(CONV_WIDTH)]


def _scan_rev(a, b, carry):
    tm = a.shape[0]
    r8 = _rows(a.shape, 0) & 7
    k = 1
    while k < 8:
        ar = pltpu.roll(a, tm - k, 0)
        br = pltpu.roll(b, tm - k, 0)
        m = r8 < 8 - k
        b = jnp.where(m, a * br + b, b)
        a = jnp.where(m, a * ar, a)
        k *= 2
    outs = []
    c = jnp.broadcast_to(carry, (8, a.shape[1]))
    for g in reversed(range(tm // 8)):
        gg = b[8 * g:8 * g + 8] + a[8 * g:8 * g + 8] * c
        outs.insert(0, gg)
        c = jnp.broadcast_to(gg[0:1], c.shape)
    return jnp.concatenate(outs, axis=0)


def _gate_dot(xbb, w_ref):
    return jnp.concatenate(
        [_dot(xbb[:, POOL_GROUP * q:POOL_GROUP * (q + 1)], w_ref[q]) for q in range(GATE_PAIRS)], axis=1)


def _gate_dot_nt(db, w_ref):
    return jnp.concatenate(
        [_dot_nt(db[:, POOL_GROUP * q:POOL_GROUP * (q + 1)], w_ref[q]) for q in range(GATE_PAIRS)], axis=1)


def _lru_decay(r, lsl8, t0):
    a = jnp.exp(r * lsl8)
    first = _rows(r.shape, t0) == 0
    m2 = 1.0 - a * a
    rs = lax.rsqrt(jnp.maximum(m2, jnp.finfo(F32).tiny))
    mult = jnp.where(first, 1.0, m2 * rs)
    return a, mult, rs, first


def _lru_chunks_fwd(r_ref, i_ref, xb_ref, ug_ref, gp_ref, h_ref, y_ref, row0, n_rows, lsl8, t0, hc):
    lsl = jnp.broadcast_to(lsl8, (8, D_LRU))
    sub8 = lax.broadcasted_iota(jnp.int32, (8, D_LRU), 0)

    def chunk(j, hc):
        o = j * 8
        rows = pl.ds(o, 8)
        r = _sigmoid(r_ref[rows, :])
        ig = _sigmoid(i_ref[rows, :])
        r_ref[rows, :] = r
        i_ref[rows, :] = ig
        a = jnp.exp(r * lsl)
        m2 = 1.0 - a * a
        mult = jnp.where(sub8 + (t0 + o) == 0, 1.0, m2 * lax.rsqrt(jnp.maximum(m2, jnp.finfo(F32).tiny)))
        b = mult * (ig * xb_ref[rows, :])
        k = 1
        while k < 8:
            m = sub8 >= k
            b = jnp.where(m, a * pltpu.roll(b, k, 0) + b, b)
            a = jnp.where(m, a * pltpu.roll(a, k, 0), a)
            k *= 2
        h = b + a * hc
        h_ref[pl.ds(row0 + o, 8), :] = h
        ug = ug_ref[rows, :]
        gl, th = _gelu(ug)
        y_ref[rows, :] = h * gl
        ug_ref[rows, :] = gl
        gp_ref[rows, :] = _gelu_grad(ug, th)
        return jnp.broadcast_to(h[7:8, :], (8, D_LRU))

    for j in range(n_rows // 8):
        hc = chunk(j, hc)
    return hc


def _fwd_mix(x, g1, w_in, pool_w, pool_b, pool_scale, conv_w, conv_b, wa, ba, wx, bx, lru_l, w_out, tm, sub, late):
    s_len = x.shape[0]
    n = s_len // tm
    nl = len(late)

    def body(x_ref, g1_ref, win_ref, pw_ref, pb_ref, ps_ref, cw_ref, cb_ref, wa_ref, ba_ref, wx_ref, bx_ref, l_ref,
             wout_ref, *rest):
        ulru_ref, d_ref, gg_ref, rg_ref, h_ref, h1_ref = rest[nl:nl + 6]
        late_ref = rest[nl + 6:2 * nl + 6]
        cpool, clru, ch, s_r, s_i, s_xb, s_ug, s_gp, s_y, ssem, rsem = rest[2 * nl + 6:]
        i = pl.program_id(0)

        @pl.when(i == 0)
        def _():
            _gather_send(late_ref, late, ssem, rsem)
            cpool[...] = jnp.zeros_like(cpool)
            clru[...] = jnp.zeros_like(clru)
            ch[...] = jnp.zeros_like(ch)

        lsl8 = LRU_C * _log_sigmoid(l_ref[...])
        cp, cl, hc = cpool[...], clru[...], ch[...]

        def in_proj(k):
            rs = slice(k * sub, (k + 1) * sub)
            xv = x_ref[rs, :]
            zb = (xv * _rstd(xv) * g1_ref[...]).astype(BF16)
            proj = jnp.concatenate([_dot(zb, win_ref[j]) for j in range(N_CHIPS)], axis=1)
            ulru_ref[rs, :] = proj[:, D_POOL:D_POOL + D_LRU].astype(BF16)
            return xv, proj

        nxt = in_proj(0)
        for k in range(tm // sub):
            rs = slice(k * sub, (k + 1) * sub)
            t0 = i * tm + k * sub
            xv, proj = nxt
            if k + 1 < tm // sub:
                nxt = in_proj(k + 1)
            u_pool = proj[:, :D_POOL]
            u_lru = proj[:, D_POOL:D_POOL + D_LRU]
            u_gate = proj[:, D_POOL + D_LRU:]

            d, _ = _pool_diff(u_pool, cp, t0)
            cp = u_pool[sub - POOL_HALO:]
            db = d.astype(BF16)
            d_ref[rs, :] = db
            yp = jnp.concatenate(
                [_dot(db[:, POOL_GROUP * g:POOL_GROUP * (g + 1)], pw_ref[g]) for g in range(len(POOL_WINDOWS))], axis=1)
            y_pool = (yp + pb_ref[...]) * ps_ref[...]

            taps = _conv_taps(u_lru, cl)
            cl = u_lru[sub - CONV_HALO:]
            xb = cb_ref[...]
            for q in range(CONV_WIDTH):
                xb = xb + taps[q] * cw_ref[q:q + 1, :]
            xbb = xb.astype(BF16)
            s_r[...] = _gate_dot(xbb, wa_ref) + ba_ref[...]
            s_i[...] = _gate_dot(xbb, wx_ref) + bx_ref[...]
            s_xb[...] = xb
            s_ug[...] = u_gate
            hc = _lru_chunks_fwd(s_r, s_i, s_xb, s_ug, s_gp, h_ref, s_y, k * sub, sub, lsl8, t0, hc)
            rg_ref[rs, :] = jnp.concatenate([s_r[...], s_i[...]], axis=1).astype(BF16)
            gg_ref[rs, :] = jnp.concatenate([s_ug[...], s_gp[...]], axis=1).astype(BF16)
            cat = jnp.concatenate([y_pool, s_y[...]], axis=1).astype(BF16)
            h1_ref[rs, :] = xv + _dot(cat, wout_ref[...])
        cpool[...] = cp
        clru[...] = cl
        ch[...] = hc

        @pl.when(i == max(n - PASS_STEPS, 0))
        def _():
            _gather_pass(late_ref, late, ssem, rsem)

        @pl.when(i == n - 1)
        def _():
            _gather_done(late_ref, late, ssem, rsem)

    row = lambda w: pl.BlockSpec((tm, w), lambda i: (i, 0))
    ins = [x, g1, w_in, pool_w, pool_b, pool_scale, conv_w, conv_b, wa, ba, wx, bx, lru_l, w_out]
    outs = pl.pallas_call(
        body, name="fwd_mix", grid=(n,),
        in_specs=[row(D_MODEL)] + [_resident(a.shape) for a in ins[1:]] + [ANY] * nl,
        out_specs=[row(D_LRU), row(D_POOL), row(2 * D_LRU), row(2 * D_LRU), row(D_LRU), row(D_MODEL)] + [ANY] * nl,
        out_shape=[jax.ShapeDtypeStruct((s_len, D_LRU), BF16), jax.ShapeDtypeStruct((s_len, D_POOL), BF16),
                   jax.ShapeDtypeStruct((s_len, 2 * D_LRU), BF16), jax.ShapeDtypeStruct((s_len, 2 * D_LRU), BF16),
                   jax.ShapeDtypeStruct((s_len, D_LRU), F32), jax.ShapeDtypeStruct((s_len, D_MODEL), F32)]
        + [jax.ShapeDtypeStruct(a.shape, a.dtype) for a in late],
        input_output_aliases={len(ins) + k: 6 + k for k in range(nl)},
        scratch_shapes=[pltpu.VMEM((POOL_HALO, D_POOL), F32), pltpu.VMEM((CONV_HALO, D_LRU), F32),
                        pltpu.VMEM((8, D_LRU), F32)] + [pltpu.VMEM((sub, D_LRU), F32)] * 6 + _gather_sems(nl),
        compiler_params=_params("arbitrary"),
    )(*ins, *late)
    return outs[:6], outs[6:]


def _fwd_mlp(h1, g2, w_up, w_down, tm, late):
    s_len = h1.shape[0]
    n = s_len // tm
    nl = len(late)

    def body(h1_ref, g2_ref, wup_ref, wdn_ref, *rest):
        z2_ref, ru_ref, h2_ref = rest[nl:nl + 3]
        late_ref = rest[nl + 3:2 * nl + 3]
        ssem, rsem = rest[2 * nl + 3:]
        i = pl.program_id(0)

        @pl.when(i == 0)
        def _():
            _gather_send(late_ref, late, ssem, rsem)

        hv = h1_ref[...]
        zb = (hv * _rstd(hv) * g2_ref[...]).astype(BF16)
        z2_ref[...] = zb
        acc = hv
        for j in range(N_CHIPS):
            ru = jnp.maximum(_dot(zb, wup_ref[j]), 0.0)
            ru_ref[:, FF_BLOCK * j:FF_BLOCK * (j + 1)] = ru.astype(BF16)
            acc = acc + _dot((ru * ru).astype(BF16), wdn_ref[j])
        h2_ref[...] = acc

        @pl.when(i == max(n - PASS_STEPS, 0))
        def _():
            _gather_pass(late_ref, late, ssem, rsem)

        @pl.when(i == n - 1)
        def _():
            _gather_done(late_ref, late, ssem, rsem)

    row = lambda w: pl.BlockSpec((tm, w), lambda i: (i, 0))
    outs = pl.pallas_call(
        body, name="fwd_mlp", grid=(n,),
        in_specs=[row(D_MODEL), _full(g2.shape), _resident(w_up.shape), _resident(w_down.shape)] + [ANY] * nl,
        out_specs=[row(D_MODEL), row(D_FF), row(D_MODEL)] + [ANY] * nl,
        out_shape=[jax.ShapeDtypeStruct((s_len, D_MODEL), BF16), jax.ShapeDtypeStruct((s_len, D_FF), BF16),
                   jax.ShapeDtypeStruct((s_len, D_MODEL), F32)] + [jax.ShapeDtypeStruct(a.shape, a.dtype) for a in late],
        input_output_aliases={4 + k: 3 + k for k in range(nl)},
        scratch_shapes=_gather_sems(nl),
        compiler_params=_params("arbitrary"),
    )(h1, g2, w_up, w_down, *late)
    return outs[:3], outs[3:]


def _head(h2, p, target, g3, w_pg, b_pg, w_pp, g4, tm):
    s_len = h2.shape[0]
    n = s_len // tm

    def body(h2_ref, p_ref, t_ref, g3_ref, wpg_ref, bpg_ref, wpp_ref, g4_ref,
             dh2_ref, dh2b_ref, dwpg_ref, dwpp_ref, vec_ref, a_pg, a_pp, a_vec):
        i = pl.program_id(0)

        @pl.when(i == 0)
        def _():
            a_pg[...] = jnp.zeros_like(a_pg)
            a_pp[...] = jnp.zeros_like(a_pp)
            a_vec[...] = jnp.zeros_like(a_vec)

        h2v = h2_ref[...]
        g3v = g3_ref[...]
        g4v = g4_ref[...]
        z3 = (h2v * _rstd(h2v) * g3v).astype(BF16)
        gate = _sigmoid(_dot(z3, wpg_ref[...]) + bpg_ref[...])
        pb = p_ref[...].astype(BF16)
        pp = _dot(pb, wpp_ref[...])
        h3 = h2v + gate * pp
        r4 = _rstd(h3)
        diff = h3 * r4 * g4v - t_ref[...]
        loss = 0.5 * jnp.sum(jnp.mean(diff * diff, axis=-1, keepdims=True), axis=0, keepdims=True)
        dy = diff * (1.0 / D_MODEL)
        dh3, dg4 = _rms_bwd(h3, g4v, dy)
        dpp = (dh3 * gate).astype(BF16)
        dpre = dh3 * pp * gate * (1.0 - gate)
        dpreb = dpre.astype(BF16)
        dz3 = _dot_nt(dpreb, wpg_ref[...])
        dx, dg3 = _rms_bwd(h2v, g3v, dz3)
        dh2 = dh3 + dx
        dh2_ref[...] = dh2
        dh2b_ref[...] = dh2.astype(BF16)
        a_pg[...] += _dot_tn(z3, dpreb)
        a_pp[...] += _dot_tn(pb, dpp)
        a_vec[0:1, :] += dg3
        a_vec[1:2, :] += dg4
        a_vec[2:3, :] += jnp.sum(dpre, axis=0, keepdims=True)
        a_vec[3:4, :] += jnp.broadcast_to(loss, (1, D_MODEL))

        @pl.when(i == n - 1)
        def _():
            dwpg_ref[...] = a_pg[...].astype(BF16)
            for j in range(N_CHIPS):
                dwpp_ref[j] = a_pp[:, PLE_DIM * j:PLE_DIM * (j + 1)].astype(BF16)
            vec_ref[...] = a_vec[...]

    row = lambda w: pl.BlockSpec((tm, w), lambda i: (i, 0))
    ins = [h2, p, target, g3, w_pg, b_pg, w_pp, g4]
    return pl.pallas_call(
        body, name="head", grid=(n,),
        in_specs=[row(D_MODEL), row(PLE_DIM), row(D_MODEL)] + [_resident(a.shape) for a in ins[3:]],
        out_specs=[row(D_MODEL), row(D_MODEL), _full((D_MODEL, D_MODEL)), _full((N_CHIPS, PLE_DIM, PLE_DIM)),
                   _full((8, D_MODEL))],
        out_shape=[jax.ShapeDtypeStruct((s_len, D_MODEL), F32), jax.ShapeDtypeStruct((s_len, D_MODEL), BF16),
                   jax.ShapeDtypeStruct((D_MODEL, D_MODEL), BF16),
                   jax.ShapeDtypeStruct((N_CHIPS, PLE_DIM, PLE_DIM), BF16), jax.ShapeDtypeStruct((8, D_MODEL), F32)],
        scratch_shapes=[pltpu.VMEM((D_MODEL, D_MODEL), F32), pltpu.VMEM((PLE_DIM, D_MODEL), F32),
                        pltpu.VMEM((8, D_MODEL), F32)],
        compiler_params=_params("arbitrary"),
    )(*ins)


def _bwd_mlp_x(dh2, ru, h1, g2, w_up, w_down, tm):
    s_len = dh2.shape[0]
    n = s_len // tm

    def body(dh2_ref, ru_ref, h1_ref, g2_ref, wup_ref, wdn_ref, dup_ref, dh1_ref, dg2_ref, a_g):
        i = pl.program_id(0)

        @pl.when(i == 0)
        def _():
            a_g[...] = jnp.zeros_like(a_g)

        dh2v = dh2_ref[...]
        dhb = dh2v.astype(BF16)
        acc = jnp.zeros((tm, D_MODEL), F32)
        for j in range(N_CHIPS):
            sl = slice(FF_BLOCK * j, FF_BLOCK * (j + 1))
            dup = (_dot_nt(dhb, wdn_ref[j]) * (2.0 * ru_ref[:, sl].astype(F32))).astype(BF16)
            dup_ref[:, sl] = dup
            acc = acc + _dot_nt(dup, wup_ref[j])
        dx, dg = _rms_bwd(h1_ref[...], g2_ref[...], acc)
        dh1_ref[...] = dh2v + dx
        a_g[0:1, :] += dg

        @pl.when(i == n - 1)
        def _():
            dg2_ref[...] = a_g[...]

    row = lambda w: pl.BlockSpec((tm, w), lambda i: (i, 0))
    return pl.pallas_call(
        body, name="bwd_mlp_x", grid=(n,),
        in_specs=[row(D_MODEL), row(D_FF), row(D_MODEL), _full(g2.shape), _resident(w_up.shape), _resident(w_down.shape)],
        out_specs=[row(D_FF), row(D_MODEL), _full((8, D_MODEL))],
        out_shape=[jax.ShapeDtypeStruct((s_len, D_FF), BF16), jax.ShapeDtypeStruct((s_len, D_MODEL), F32),
                   jax.ShapeDtypeStruct((8, D_MODEL), F32)],
        scratch_shapes=[pltpu.VMEM((8, D_MODEL), F32)],
        compiler_params=_params("arbitrary"),
    )(dh2, ru, h1, g2, w_up, w_down)


def _bwd_mlp_w(z2, dup, ru, dh2, tk):
    s_len = z2.shape[0]
    n = s_len // tk

    def body(z2_ref, dup_ref, ru_ref, dh2_ref, dwup_ref, dwdn_ref, a_up, a_dn):
        t = pl.program_id(1)

        @pl.when(t == 0)
        def _():
            a_up[...] = jnp.zeros_like(a_up)
            a_dn[...] = jnp.zeros_like(a_dn)

        for b in range(FF_PAIR):
            sl = slice(FF_BLOCK * b, FF_BLOCK * (b + 1))
            ruv = ru_ref[:, sl]
            a_up[b] += _dot_tn(z2_ref[...], dup_ref[:, sl])
            a_dn[b] += _dot_tn(ruv * ruv, dh2_ref[...])

        @pl.when(t == n - 1)
        def _():
            dwup_ref[...] = a_up[...].astype(BF16)
            dwdn_ref[...] = a_dn[...].astype(BF16)

    tile = pl.BlockSpec((tk, D_MODEL), lambda j, t: (t, 0))
    ffb = pl.BlockSpec((tk, FF_PAIR * FF_BLOCK), lambda j, t: (t, j))
    return pl.pallas_call(
        body, name="bwd_mlp_w", grid=(N_CHIPS // FF_PAIR, n),
        in_specs=[tile, ffb, ffb, tile],
        out_specs=[pl.BlockSpec((FF_PAIR, D_MODEL, FF_BLOCK), lambda j, t: (j, 0, 0)),
                   pl.BlockSpec((FF_PAIR, FF_BLOCK, D_MODEL), lambda j, t: (j, 0, 0))],
        out_shape=[jax.ShapeDtypeStruct((N_CHIPS, D_MODEL, FF_BLOCK), BF16),
                   jax.ShapeDtypeStruct((N_CHIPS, FF_BLOCK, D_MODEL), BF16)],
        scratch_shapes=[pltpu.VMEM((FF_PAIR, D_MODEL, FF_BLOCK), F32), pltpu.VMEM((FF_PAIR, FF_BLOCK, D_MODEL), F32)],
        compiler_params=_params("arbitrary", "arbitrary"),
    )(z2, dup, ru, dh2)


MIX_VEC_ROWS = 16


def _bwd_mix(dh1, ulru, dpl, gg, rg, h, pool_w, pool_b, pool_scale, conv_w, conv_b, wa, ba, wx, bx, lru_l, w_out, tm, sub, early):
    s_len = dh1.shape[0]
    n = s_len // tm
    ng = len(POOL_WINDOWS)
    ne_ = len(early)

    def body(dh1_ref, ulru_ref, d_ref, gg_ref, rg_ref, h_ref, ulruh_ref, hh_ref, pw_ref, pb_ref, ps_ref, cw_ref, cb_ref,
             wa_ref, ba_ref, wx_ref, bx_ref, l_ref, wout_ref, *rest):
        early_ref = rest[:ne_]
        dproj_ref, dwout_ref, dpw_ref, dwa_ref, dwx_ref, vec_ref = rest[ne_:ne_ + 6]
        got_ref = rest[ne_ + 6:2 * ne_ + 6]
        a_out, a_pw, a_wa, a_wx, a_vec, c_g, c_dxb, c_ddc, ssem, rsem = rest[2 * ne_ + 6:]
        q = pl.program_id(0)
        i = n - 1 - q

        @pl.when(q == 0)
        def _():
            _scatter_send(early_ref, got_ref, early, ssem, rsem)
            for r in (a_out, a_pw, a_wa, a_wx, a_vec, c_g, c_dxb, c_ddc):
                r[...] = jnp.zeros_like(r)

        has_prev = (i > 0).astype(F32)
        lv = l_ref[...]
        lsl8 = LRU_C * _log_sigmoid(lv)
        cg, cdxb, cddc = c_g[0:1, :], c_dxb[...], c_ddc[...]
        vec, cats, dhbs = {}, [], []

        def add(row, v):
            vec[row] = v if row not in vec else vec[row] + v

        for k in reversed(range(tm // sub)):
            rs = slice(k * sub, (k + 1) * sub)
            t0 = i * tm + k * sub
            dh1b = dh1_ref[rs, :].astype(BF16)
            dcat = _dot_nt(dh1b, wout_ref[...])
            dy_pool = dcat[:, :D_POOL]
            dy_lru = dcat[:, D_POOL:]

            u_lru = ulru_ref[rs, :].astype(F32)
            if k > 0:
                halo = ulru_ref[k * sub - POOL_HALO:k * sub, :].astype(F32)
                h_prev_row = h_ref[k * sub - 1:k * sub, :]
            else:
                halo = ulruh_ref[...].astype(F32) * has_prev
                h_prev_row = hh_ref[7:8, :] * has_prev

            inv = _pool_inv(sub, t0)
            db = d_ref[rs, :]
            ypre = jnp.concatenate(
                [_dot(db[:, POOL_GROUP * g:POOL_GROUP * (g + 1)], pw_ref[g]) for g in range(ng)], axis=1) + pb_ref[...]
            dyp = dy_pool * ps_ref[...]
            dypb = dyp.astype(BF16)
            dds = []
            for g in range(ng):
                sl = slice(POOL_GROUP * g, POOL_GROUP * (g + 1))
                a_pw[g] += _dot_tn(db[:, sl], dypb[:, sl])
                dds.append(_dot_nt(dypb[:, sl], pw_ref[g]))
            du_pool, ddc = _pool_diff_bwd(jnp.concatenate(dds, axis=1), inv, cddc)
            cddc = ddc[:POOL_HALO]
            add(0, jnp.sum(dyp, axis=0, keepdims=True))
            add(1, jnp.sum(dy_pool * ypre, axis=0, keepdims=True))

            taps = _conv_taps(u_lru, halo[POOL_HALO - CONV_HALO:])
            xb = cb_ref[...]
            for c in range(CONV_WIDTH):
                xb = xb + taps[c] * cw_ref[c:c + 1, :]
            r = rg_ref[rs, :D_LRU].astype(F32)
            ig = rg_ref[rs, D_LRU:].astype(F32)
            a, mult, inv_mult, first = _lru_decay(r, lsl8, t0)
            hv = h_ref[rs, :]
            gl = gg_ref[rs, :D_LRU].astype(F32)
            du_gate = dy_lru * hv * gg_ref[rs, D_LRU:].astype(F32)
            cats.insert(0, jnp.concatenate([ypre * ps_ref[...], hv * gl], axis=1).astype(BF16))
            dhbs.insert(0, dh1b)
            last = _rows(a.shape, 0) == sub - 1
            a_next = jnp.where(last, 1.0, pltpu.roll(a, sub - 1, 0))
            gh = _scan_rev(a_next, dy_lru * gl, cg)
            cg = a[0:1, :] * gh[0:1, :]
            h_prev = jnp.where(_rows(hv.shape, 0) == 0, h_prev_row, pltpu.roll(hv, 1, 0))
            gix = gh * ig * xb
            dla = gh * h_prev * a - jnp.where(first, 0.0, gix * a * a * inv_mult)
            dpre_r = dla * lsl8 * r * (1.0 - r)
            dpre_i = gh * mult * xb * ig * (1.0 - ig)
            dprb = dpre_r.astype(BF16)
            dpib = dpre_i.astype(BF16)
            xbb = xb.astype(BF16)
            for pair in range(GATE_PAIRS):
                sl = slice(POOL_GROUP * pair, POOL_GROUP * (pair + 1))
                a_wa[pair] += _dot_tn(xbb[:, sl], dprb[:, sl])
                a_wx[pair] += _dot_tn(xbb[:, sl], dpib[:, sl])
            dxb = gh * mult * ig + _gate_dot_nt(dprb, wa_ref) + _gate_dot_nt(dpib, wx_ref)
            add(2, jnp.sum(dxb, axis=0, keepdims=True))
            add(3, jnp.sum(dpre_r, axis=0, keepdims=True))
            add(4, jnp.sum(dpre_i, axis=0, keepdims=True))
            add(5, jnp.sum(dla * r, axis=0, keepdims=True))
            ext = jnp.concatenate([dxb, cdxb], axis=0)
            cdxb = dxb[:CONV_HALO]
            ne = sub + CONV_HALO
            du_lru = dxb * cw_ref[CONV_WIDTH - 1:CONV_WIDTH, :]
            for c in range(CONV_WIDTH):
                add(8 + c, jnp.sum(dxb * taps[c], axis=0, keepdims=True))
                if c < CONV_WIDTH - 1:
                    du_lru = du_lru + pltpu.roll(ext, ne - (CONV_WIDTH - 1 - c), 0)[:sub] * cw_ref[c:c + 1, :]
            dproj_ref[rs, :] = jnp.concatenate([du_pool, du_lru, du_gate], axis=1).astype(BF16)
        a_out[...] += _dot_tn(jnp.concatenate(cats, axis=0), jnp.concatenate(dhbs, axis=0))
        c_g[...] = jnp.broadcast_to(cg, c_g.shape)
        c_dxb[...] = cdxb
        c_ddc[...] = cddc
        for row, v in vec.items():
            a_vec[row:row + 1, :] += v

        @pl.when(q == n - 1)
        def _():
            dwout_ref[...] = a_out[...].astype(BF16)
            dpw_ref[...] = a_pw[...]
            dwa_ref[...] = a_wa[...]
            dwx_ref[...] = a_wx[...]
            vec_ref[...] = a_vec[...]
            vec_ref[5:6, :] = a_vec[5:6, :] * (LRU_C * _sigmoid(-lv))
            _scatter_done(got_ref, early, ssem, rsem)

    rev =lambda w: pl.BlockSpec((tm, w), lambda q: (n - 1 - q, 0))
    halo_p = pl.BlockSpec((POOL_HALO, D_LRU), lambda q: (jnp.maximum((n - 1 - q) * (tm // POOL_HALO) - 1, 0), 0))
    halo_h = pl.BlockSpec((8, D_LRU), lambda q: (jnp.maximum((n - 1 - q) * (tm // 8) - 1, 0), 0))
    wts = [pool_w, pool_b, pool_scale, conv_w, conv_b, wa, ba, wx, bx, lru_l, w_out]
    outs = pl.pallas_call(
        body, name="bwd_mix", grid=(n,),
        in_specs=[rev(D_MODEL), rev(D_LRU), rev(D_POOL), rev(2 * D_LRU), rev(2 * D_LRU), rev(D_LRU), halo_p, halo_h]
        + [_resident(a.shape) for a in wts]
        + [ANY] * ne_,
        out_specs=[rev(D_IN_PROJ), _full((D_MODEL, D_MODEL)), _full((ng, POOL_GROUP, POOL_GROUP)),
                   _full(wa.shape), _full(wa.shape), _full((MIX_VEC_ROWS, D_LRU))] + [ANY] * ne_,
        out_shape=[jax.ShapeDtypeStruct((s_len, D_IN_PROJ), BF16), jax.ShapeDtypeStruct((D_MODEL, D_MODEL), BF16),
                   jax.ShapeDtypeStruct((ng, POOL_GROUP, POOL_GROUP), F32), jax.ShapeDtypeStruct(wa.shape, F32),
                   jax.ShapeDtypeStruct(wa.shape, F32), jax.ShapeDtypeStruct((MIX_VEC_ROWS, D_LRU), F32)]
        + _scatter_shapes(early),
        scratch_shapes=[pltpu.VMEM((D_MODEL, D_MODEL), F32), pltpu.VMEM((ng, POOL_GROUP, POOL_GROUP), F32),
                        pltpu.VMEM(wa.shape, F32), pltpu.VMEM(wa.shape, F32),
                        pltpu.VMEM((MIX_VEC_ROWS, D_LRU), F32), pltpu.VMEM((8, D_LRU), F32),
                        pltpu.VMEM((CONV_HALO, D_LRU), F32), pltpu.VMEM((POOL_HALO, D_POOL), F32)] + _scatter_sems(ne_),
        compiler_params=_params("arbitrary"),
    )(dh1, ulru, dpl, gg, rg, h, ulru, h, *wts, *early)
    return outs[:6], outs[6:]


def _bwd_in(dproj, x, dh1, g1, w_in, tm):
    s_len = x.shape[0]
    n = s_len // tm
    cb = D_IN_PROJ // N_CHIPS

    def body(dp_ref, x_ref, dh1_ref, g1_ref, win_ref, dx_ref, dwin_ref, dg1_ref, a_w, a_g):
        i = pl.program_id(0)

        @pl.when(i == 0)
        def _():
            a_w[...] = jnp.zeros_like(a_w)
            a_g[...] = jnp.zeros_like(a_g)

        dp = dp_ref[...]
        xv = x_ref[...]
        zb = (xv * _rstd(xv) * g1_ref[...]).astype(BF16)
        dz = jnp.zeros((tm, D_MODEL), F32)
        for j in range(N_CHIPS):
            dpj = dp[:, cb * j:cb * (j + 1)]
            dz = dz + _dot_nt(dpj, win_ref[j])
            a_w[j] += _dot_tn(zb, dpj)
        dx, dg = _rms_bwd(xv, g1_ref[...], dz)
        dx_ref[...] = dh1_ref[...] + dx
        a_g[0:1, :] += dg

        @pl.when(i == n - 1)
        def _():
            dwin_ref[...] = a_w[...].astype(BF16)
            dg1_ref[...] = a_g[...]

    row = lambda w: pl.BlockSpec((tm, w), lambda i: (i, 0))
    return pl.pallas_call(
        body, name="bwd_in", grid=(n,),
        in_specs=[row(D_IN_PROJ), row(D_MODEL), row(D_MODEL), _resident(g1.shape), _resident(w_in.shape)],
        out_specs=[row(D_MODEL), _full(w_in.shape), _full((8, D_MODEL))],
        out_shape=[jax.ShapeDtypeStruct((s_len, D_MODEL), F32), jax.ShapeDtypeStruct(w_in.shape, BF16),
                   jax.ShapeDtypeStruct((8, D_MODEL), F32)],
        scratch_shapes=[pltpu.VMEM(w_in.shape, F32), pltpu.VMEM((8, D_MODEL), F32)],
        compiler_params=_params("arbitrary"),
    )(dproj, x, dh1, g1, w_in)


def _place():
    x, y, c = lax.axis_index("x"), lax.axis_index("y"), lax.axis_index("c")
    chips = [(1 - x, y), (x, 1 - y), (1 - x, 1 - y)]
    return x, y, c, chips


def _rcopy(src, dst, ssem, rsem, dev):
    return pltpu.make_async_remote_copy(src_ref=src, dst_ref=dst, send_sem=ssem, recv_sem=rsem,
                                        device_id=dev, device_id_type=MESH)


ANY = pl.BlockSpec(memory_space=pl.ANY)
COPY_CHUNK_BYTES = 128 * 1024
ROW_ALIGN = 16


def _row_chunks(rows, row_bytes):
    per = max(ROW_ALIGN, (COPY_CHUNK_BYTES // row_bytes) // ROW_ALIGN * ROW_ALIGN)
    return [(r0, min(per, rows - r0)) for r0 in range(0, rows, per)]


def _row_bytes(a):
    return a.shape[-1] * jnp.dtype(a.dtype).itemsize


def _stack_own(shards, dtypes, pos, steps, tag, late=()):
    nw = len(shards)
    nl = len(late)

    def body(pos_ref, *refs):
        outs = refs[nw + nl:2 * nw + nl]
        late_ref = refs[2 * nw + nl:2 * nw + 2 * nl]
        i = pl.program_id(0)
        if nl:
            ssem, rsem = refs[2 * nw + 2 * nl:]

            @pl.when(i == 0)
            def _():
                _gather_send(late_ref, late, ssem, rsem)

        for w in range(nw):
            outs[w][0] = refs[w][...].astype(dtypes[w])

        if nl:
            @pl.when(i == max(steps - PASS_STEPS, 0))
            def _():
                _gather_pass(late_ref, late, ssem, rsem)

            @pl.when(i == steps - 1)
            def _():
                _gather_done(late_ref, late, ssem, rsem)

    def split(s):
        return s.shape[0] % (steps * ROW_ALIGN) == 0

    ins = [pl.BlockSpec((s.shape[0] // steps, s.shape[1]), lambda i, p: (i, 0)) if split(s)
           else pl.BlockSpec(s.shape, lambda i, p: (0, 0)) for s in shards]
    outs = [pl.BlockSpec((1, s.shape[0] // steps, s.shape[1]), lambda i, p: (p[0], i, 0)) if split(s)
            else pl.BlockSpec((1,) + s.shape, lambda i, p: (p[0], 0, 0)) for s in shards]
    res = pl.pallas_call(
        body, name="stack_own_" + tag,
        grid_spec=pltpu.PrefetchScalarGridSpec(
            num_scalar_prefetch=1, grid=(steps,), in_specs=ins + [ANY] * nl, out_specs=outs + [ANY] * nl,
            scratch_shapes=_gather_sems(nl) if nl else []),
        out_shape=[jax.ShapeDtypeStruct((N_CHIPS,) + s.shape, d) for s, d in zip(shards, dtypes)]
        + [jax.ShapeDtypeStruct(a.shape, a.dtype) for a in late],
        input_output_aliases={1 + nw + k: nw + k for k in range(nl)},
        compiler_params=_params("arbitrary"),
    )(pos, *shards, *late)
    return res[:nw], res[nw:]


def _gather_send(outs, stacks, ssem, rsem):
    x, y, c, chips = _place()
    me = 2 * x + y
    for w, st in enumerate(stacks):
        half = st.shape[1] // 2
        for s, (px, py) in enumerate(chips):
            for r0, rs in _row_chunks(half, _row_bytes(st)):
                piece = outs[w].at[me, pl.ds(c * half + r0, rs)]
                _rcopy(piece, piece, ssem.at[w, s], rsem.at[w, s], (px, py, c)).start()


def _gather_pass(outs, stacks, ssem, rsem):
    x, y, c, chips = _place()
    sib = (x, y, 1 - c)
    for w, st in enumerate(stacks):
        half = st.shape[1] // 2
        for s, (px, py) in enumerate(chips):
            blk = outs[w].at[2 * px + py, pl.ds(c * half, half)]
            _rcopy(blk, blk, ssem.at[w, s], rsem.at[w, s], sib).wait_recv()
            for r0, rs in _row_chunks(half, _row_bytes(st)):
                piece = outs[w].at[2 * px + py, pl.ds(c * half + r0, rs)]
                _rcopy(piece, piece, ssem.at[w, 3 + s], rsem.at[w, 3 + s], sib).start()


def _gather_done(outs, stacks, ssem, rsem):
    x, y, c, chips = _place()
    sib = (x, y, 1 - c)
    for w, st in enumerate(stacks):
        half = st.shape[1] // 2
        for s, (px, py) in enumerate(chips):
            blk = outs[w].at[2 * px + py, pl.ds((1 - c) * half, half)]
            _rcopy(blk, blk, ssem.at[w, 3 + s], rsem.at[w, 3 + s], sib).wait_recv()
    for w, st in enumerate(stacks):
        half = st.shape[1] // 2
        blk = outs[w].at[0, pl.ds(0, half)]
        for s in range(6):
            _rcopy(blk, blk, ssem.at[w, s], rsem.at[w, s], sib).wait_send()


def _gather_sems(nw):
    return [pltpu.SemaphoreType.DMA((nw, 6)), pltpu.SemaphoreType.DMA((nw, 6))]


def _swap_halves(grads, tag):
    nw = len(grads)

    def body(*refs):
        ins, got = refs[:nw], refs[nw:2 * nw]
        ssem, rsem = refs[2 * nw:]
        x, y, c, _ = _place()
        cps = []
        for w in range(nw):
            hr = grads[w].shape[1] // 2
            for k in range(N_CHIPS):
                for r0, rs in _row_chunks(hr, _row_bytes(grads[w])):
                    _rcopy(ins[w].at[k, pl.ds((1 - c) * hr + r0, rs)], got[w].at[k, pl.ds(r0, rs)],
                           ssem.at[w], rsem.at[w], (x, y, 1 - c)).start()
            cps.append(_rcopy(got[w], got[w], ssem.at[w], rsem.at[w], (x, y, 1 - c)))
        for cp in cps:
            cp.wait()

    return pl.pallas_call(
        body, name="swap_halves_" + tag,
        in_specs=[ANY] * nw, out_specs=[ANY] * nw,
        out_shape=[jax.ShapeDtypeStruct((g.shape[0], g.shape[1] // 2, g.shape[2]), g.dtype) for g in grads],
        scratch_shapes=[pltpu.SemaphoreType.DMA((nw,)), pltpu.SemaphoreType.DMA((nw,))],
    )(*grads)


def _add_pairs(grads, got, pos, steps, tag):
    nw = len(grads)

    def body(pos_ref, *refs):
        for w in range(nw):
            refs[2 * nw + w][...] = (refs[w][...].astype(F32) + refs[nw + w][...].astype(F32)).astype(BF16)

    blk = lambda a: (a.shape[0], a.shape[1] // steps, a.shape[2])
    own = [pl.BlockSpec(blk(a), lambda i, p: (0, p[1] * steps + i, 0)) for a in got]
    rec = [pl.BlockSpec(blk(a), lambda i, p: (0, i, 0)) for a in got]
    return pl.pallas_call(
        body, name="add_pairs_" + tag,
        grid_spec=pltpu.PrefetchScalarGridSpec(num_scalar_prefetch=1, grid=(steps,), in_specs=own + rec, out_specs=rec),
        out_shape=[jax.ShapeDtypeStruct(a.shape, BF16) for a in got],
        compiler_params=_params("arbitrary"),
    )(pos, *grads, *got)


def _scatter_send(ins, got, parts, ssem, rsem):
    x, y, c, chips = _place()
    for w, p in enumerate(parts):
        for s, (px, py) in enumerate(chips):
            for r0, rs in _row_chunks(p.shape[1], _row_bytes(p)):
                _rcopy(ins[w].at[2 * px + py, pl.ds(r0, rs)], got[w].at[s, pl.ds(r0, rs)],
                       ssem.at[w, s], rsem.at[w, s], (px, py, c)).start()


def _scatter_done(got, parts, ssem, rsem):
    x, y, c, chips = _place()
    for w in range(len(parts)):
        for s, (px, py) in enumerate(chips):
            _rcopy(got[w].at[s], got[w].at[s], ssem.at[w, s], rsem.at[w, s], (px, py, c)).wait()


def _scatter_sems(nw):
    return [pltpu.SemaphoreType.DMA((nw, 3)), pltpu.SemaphoreType.DMA((nw, 3))]


def _scatter_shapes(parts):
    return [jax.ShapeDtypeStruct((3,) + p.shape[1:], p.dtype) for p in parts]


def _scatter_chips(parts):
    nw = len(parts)

    def body(*refs):
        ins, got = refs[:nw], refs[nw:2 * nw]
        ssem, rsem = refs[2 * nw:]
        _scatter_send(ins, got, parts, ssem, rsem)
        _scatter_done(got, parts, ssem, rsem)

    return pl.pallas_call(
        body, name="scatter_chips",
        in_specs=[ANY] * nw, out_specs=[ANY] * nw, out_shape=_scatter_shapes(parts),
        scratch_shapes=_scatter_sems(nw),
    )(*parts)


def _sum_chips(parts, got, pos, steps):
    nw = len(parts)

    def body(pos_ref, *refs):
        for w in range(nw):
            acc = refs[w][0].astype(F32)
            for s in range(3):
                acc = acc + refs[nw + w][s].astype(F32)
            refs[2 * nw + w][...] = acc

    own = [pl.BlockSpec((1, p.shape[1] // steps, p.shape[2]), lambda i, ps: (ps[0], i, 0)) for p in parts]
    rec = [pl.BlockSpec((3, p.shape[1] // steps, p.shape[2]), lambda i, ps: (0, i, 0)) for p in parts]
    outs = [pl.BlockSpec((p.shape[1] // steps, p.shape[2]), lambda i, ps: (ps[1] * steps + i, 0)) for p in parts]
    return pl.pallas_call(
        body, name="sum_chips",
        grid_spec=pltpu.PrefetchScalarGridSpec(num_scalar_prefetch=1, grid=(steps,), in_specs=own + rec, out_specs=outs),
        out_shape=[jax.ShapeDtypeStruct((2 * p.shape[1], p.shape[2]), F32) for p in parts],
        compiler_params=_params("arbitrary"),
    )(pos, *parts, *got)


def _join_halves(shards):
    nw = len(shards)

    def body(*refs):
        outs = refs[nw:2 * nw]
        ssem, rsem = refs[2 * nw:]
        x, y, c, _ = _place()
        cps = []
        for w in range(nw):
            hr = shards[w].shape[0] // 2
            for r0, rs in _row_chunks(hr, _row_bytes(shards[w])):
                piece = outs[w].at[pl.ds(c * hr + r0, rs)]
                _rcopy(piece, piece, ssem.at[w], rsem.at[w], (x, y, 1 - c)).start()
            mine = outs[w].at[pl.ds(c * hr, hr)]
            cps.append(_rcopy(mine, mine, ssem.at[w], rsem.at[w], (x, y, 1 - c)))
        for cp in cps:
            cp.wait()

    return pl.pallas_call(
        body, name="join_halves",
        in_specs=[ANY] * nw, out_specs=[ANY] * nw,
        out_shape=[jax.ShapeDtypeStruct(s.shape, F32) for s in shards],
        input_output_aliases={w: w for w in range(nw)},
        scratch_shapes=[pltpu.SemaphoreType.DMA((nw,)), pltpu.SemaphoreType.DMA((nw,))],
    )(*shards)


def _scatter_and_allreduce(parts, packed):
    nw = len(parts)
    rows = packed.shape[0]
    half = rows // 2

    def body(*refs):
        ins, p_ref = refs[:nw], refs[nw]
        got, out_ref = refs[nw + 1:2 * nw + 1], refs[2 * nw + 1]
        rfull, rhalf, ssem, rsem, bsem_s, bsem_r = refs[2 * nw + 2:]
        x, y, c, _ = _place()
        sib = (x, y, 1 - c)
        _scatter_send(ins, got, parts, ssem, rsem)
        out_ref[...] = p_ref[...]
        cp = _rcopy(out_ref, rfull, bsem_s.at[0], bsem_r.at[0], sib)
        cp.start()
        cp.wait()
        out_ref[...] = out_ref[...] + rfull[...]
        mine = pl.ds(pl.multiple_of(c * half, 8), half)
        other = pl.ds(pl.multiple_of((1 - c) * half, 8), half)
        for st, peer in enumerate([(1 - x, y, c), (x, 1 - y, c)]):
            cp = _rcopy(out_ref.at[mine], rhalf.at[st], bsem_s.at[1 + st], bsem_r.at[1 + st], peer)
            cp.start()
            cp.wait()
            out_ref[mine, :] = out_ref[mine, :] + rhalf[st]
        cp = _rcopy(out_ref.at[mine], rhalf.at[2], bsem_s.at[3], bsem_r.at[3], sib)
        cp.start()
        cp.wait()
        out_ref[other, :] = rhalf[2]
        _scatter_done(got, parts, ssem, rsem)

    vm = pl.BlockSpec(memory_space=pltpu.VMEM)
    outs = pl.pallas_call(
        body, name="scatter_and_allreduce",
        in_specs=[ANY] * nw + [vm], out_specs=[ANY] * nw + [vm],
        out_shape=_scatter_shapes(parts) + [jax.ShapeDtypeStruct(packed.shape, F32)],
        scratch_shapes=[pltpu.VMEM(packed.shape, F32), pltpu.VMEM((3, half, packed.shape[1]), F32)] + _scatter_sems(nw)
        + [pltpu.SemaphoreType.DMA((4,)), pltpu.SemaphoreType.DMA((4,))],
        compiler_params=pltpu.CompilerParams(vmem_limit_bytes=VMEM_LIMIT),
    )(*parts, packed)
    return outs[:nw], outs[nw]


def _adamw_math(w, g, m, v):
    m = ADAM_B1 * m + (1.0 - ADAM_B1) * g
    v = ADAM_B2 * v + (1.0 - ADAM_B2) * (g * g)
    delta = -ADAM_LR * ((m * ADAM_C1) / (jnp.sqrt(v * ADAM_C2) + ADAM_EPS) + ADAM_WD * w)
    return delta, m, v


def _adamw(ws, gs, ms, vs, steps, name):
    nw = len(ws)

    def body(*refs):
        for k in range(nw):
            g = refs[nw + k][...]
            d, m, v = _adamw_math(refs[k][...], g, refs[2 * nw + k][...], refs[3 * nw + k][...])
            refs[4 * nw + k][...] = d
            refs[5 * nw + k][...] = m
            refs[6 * nw + k][...] = v
            refs[7 * nw + k][...] = g

    specs = [pl.BlockSpec((a.shape[0] // steps, a.shape[1]), lambda i: (i, 0)) for a in ws]
    shapes = [jax.ShapeDtypeStruct(a.shape, F32) for a in ws]
    outs = pl.pallas_call(
        body, name=name, grid=(steps,),
        in_specs=specs * 4, out_specs=specs * 4, out_shape=shapes * 4,
        compiler_params=_params("arbitrary"),
    )(*ws, *gs, *ms, *vs)
    return outs[:nw], outs[nw:2 * nw], outs[2 * nw:3 * nw], outs[3 * nw:]


SMALL = ["norm_mix_g", "pool_w", "pool_b", "pool_scale", "conv_b", "gate_a_w", "gate_a_b", "gate_x_w", "gate_x_b",
         "lru_L", "norm_mlp_g", "norm_ple_g", "b_ple_gate", "norm_final_g"]
BIG = ["w_in", "w_out", "w_up", "w_down", "w_ple_gate", "w_ple_proj"]
ORDER = ["norm_mix_g", "w_in", "pool_w", "pool_b", "pool_scale", "conv_w", "conv_b", "gate_a_w", "gate_a_b", "gate_x_w",
         "gate_x_b", "lru_L", "w_out", "norm_mlp_g", "w_up", "w_down", "norm_ple_g", "w_ple_gate", "b_ple_gate",
         "w_ple_proj", "norm_final_g"]
LANES = 128


def _pair_blocks(w):
    eye = jnp.eye(2, dtype=w.dtype)
    w4 = w.reshape(GATE_PAIRS, 2, LRU_BLOCK, LRU_BLOCK)
    return jnp.einsum("qaij,ab->qaibj", w4, eye).reshape(GATE_PAIRS, POOL_GROUP, POOL_GROUP)


def _diag_blocks(pairs):
    f = pairs.reshape(GATE_PAIRS, 2, LRU_BLOCK, 2, LRU_BLOCK)
    return jnp.stack([f[h // 2, h % 2, :, h % 2, :] for h in range(LRU_HEADS)])


def _rows128(a):
    return a.reshape(-1, LANES)


def _pad8(a):
    r = (-a.shape[0]) % 8
    return jnp.pad(a, ((0, r), (0, 0))) if r else a


def kernel(x, p, norm_mix_g, w_in, pool_w, pool_b, pool_scale, conv_w, conv_b, gate_a_w, gate_a_b, gate_x_w, gate_x_b, lru_L, w_out, norm_mlp_g, w_up, w_down, norm_ple_g, w_ple_gate, b_ple_gate, w_ple_proj, norm_final_g, loss_target, m_norm_mix_g, m_w_in, m_pool_w, m_pool_b, m_pool_scale, m_conv_w, m_conv_b, m_gate_a_w, m_gate_a_b, m_gate_x_w, m_gate_x_b, m_lru_L, m_w_out, m_norm_mlp_g, m_w_up, m_w_down, m_norm_ple_g, m_w_ple_gate, m_b_ple_gate, m_w_ple_proj, m_norm_final_g, v_norm_mix_g, v_w_in, v_pool_w, v_pool_b, v_pool_scale, v_conv_w, v_conv_b, v_gate_a_w, v_gate_a_b, v_gate_x_w, v_gate_x_b, v_lru_L, v_w_out, v_norm_mlp_g, v_w_up, v_w_down, v_norm_ple_g, v_w_ple_gate, v_b_ple_gate, v_w_ple_proj, v_norm_final_g):
    W = dict(norm_mix_g=norm_mix_g, w_in=w_in, pool_w=pool_w, pool_b=pool_b, pool_scale=pool_scale, conv_w=conv_w,
             conv_b=conv_b, gate_a_w=gate_a_w, gate_a_b=gate_a_b, gate_x_w=gate_x_w, gate_x_b=gate_x_b, lru_L=lru_L,
             w_out=w_out, norm_mlp_g=norm_mlp_g, w_up=w_up, w_down=w_down, norm_ple_g=norm_ple_g,
             w_ple_gate=w_ple_gate, b_ple_gate=b_ple_gate, w_ple_proj=w_ple_proj, norm_final_g=norm_final_g)
    M = dict(norm_mix_g=m_norm_mix_g, w_in=m_w_in, pool_w=m_pool_w, pool_b=m_pool_b, pool_scale=m_pool_scale,
             conv_w=m_conv_w, conv_b=m_conv_b, gate_a_w=m_gate_a_w, gate_a_b=m_gate_a_b, gate_x_w=m_gate_x_w,
             gate_x_b=m_gate_x_b, lru_L=m_lru_L, w_out=m_w_out, norm_mlp_g=m_norm_mlp_g, w_up=m_w_up, w_down=m_w_down,
             norm_ple_g=m_norm_ple_g, w_ple_gate=m_w_ple_gate, b_ple_gate=m_b_ple_gate, w_ple_proj=m_w_ple_proj,
             norm_final_g=m_norm_final_g)
    V = dict(norm_mix_g=v_norm_mix_g, w_in=v_w_in, pool_w=v_pool_w, pool_b=v_pool_b, pool_scale=v_pool_scale,
             conv_w=v_conv_w, conv_b=v_conv_b, gate_a_w=v_gate_a_w, gate_a_b=v_gate_a_b, gate_x_w=v_gate_x_w,
             gate_x_b=v_gate_x_b, lru_L=v_lru_L, w_out=v_w_out, norm_mlp_g=v_norm_mlp_g, w_up=v_w_up, w_down=v_w_down,
             norm_ple_g=v_norm_ple_g, w_ple_gate=v_w_ple_gate, b_ple_gate=v_b_ple_gate, w_ple_proj=v_w_ple_proj,
             norm_final_g=v_norm_final_g)

    s_len = x.shape[1]
    sub_mix = min(256, s_len)
    tm = min(512, s_len)
    chip = (2 * lax.axis_index("x") + lax.axis_index("y")).astype(jnp.int32)
    pos = jnp.stack([chip, lax.axis_index("c").astype(jnp.int32)])

    shards = [w_in[0], w_out[0], w_up[0], w_down[0], w_ple_gate[0], w_ple_proj[0], jnp.pad(conv_w[0], ((0, 12), (0, 0)))]
    first, _ = _stack_own([shards[0], shards[1], shards[6]], [BF16, BF16, F32], pos, 8, "first")
    (st_up, st_dn, st_pg, st_pp), (win_g, wout_g, cw_g) = _stack_own(shards[2:6], [BF16] * 4, pos, 8, "rest", first)
    wout_f = wout_g.reshape(D_MODEL, D_MODEL)
    cw_f = jnp.transpose(cw_g[:, :CONV_WIDTH], (1, 0, 2)).reshape(CONV_WIDTH, D_LRU)
    pw_b = pool_w[0].astype(BF16)
    wa_b = _pair_blocks(gate_a_w[0]).astype(BF16)
    wx_b = _pair_blocks(gate_x_w[0]).astype(BF16)
    pb_r = pool_b.reshape(1, D_POOL)
    ba_r = gate_a_b.reshape(1, D_LRU)
    bx_r = gate_x_b.reshape(1, D_LRU)
    g4 = norm_final_g.reshape(1, D_MODEL)
    mix_w = (pw_b, pb_r, pool_scale, cw_f, conv_b, wa_b, ba_r, wx_b, bx_r, lru_L, wout_f)

    xs, ps, ts = x[0], p[0, 0], loss_target[0]
    (ulru, dpl, gg, rg, hst, h1), (wup_g, wdn_g) = _fwd_mix(xs, norm_mix_g, win_g, *mix_w, tm, sub_mix, [st_up, st_dn])
    (z2, ru, h2), (wpg_g, wpp_g) = _fwd_mlp(h1, norm_mlp_g, wup_g, wdn_g, tm, [st_pg, st_pp])
    wpg_f = wpg_g.reshape(D_MODEL, D_MODEL)
    wpp_f = jnp.transpose(wpp_g, (1, 0, 2)).reshape(PLE_DIM, D_MODEL)
    dh2, dh2b, d_wpg, d_wpp, head_vec = _head(h2, ps, ts, norm_ple_g, wpg_f, b_ple_gate, wpp_f, g4, tm)
    dup, dh1, mlp_vec = _bwd_mlp_x(dh2, ru, h1, norm_mlp_g, wup_g, wdn_g, tm)
    d_wup, d_wdn = _bwd_mlp_w(z2, dup, ru, dh2b, tm)
    early = [d_wup, d_wdn, d_wpg.reshape(N_CHIPS, D_MODEL // N_CHIPS, D_MODEL), d_wpp]
    pair_e = _add_pairs(early, _swap_halves(early, "early"), pos, 8, "early")
    (dproj, d_wout, d_pw, d_wa, d_wx, mix_vec), got_e = _bwd_mix(dh1, ulru, dpl, gg, rg, hst, *mix_w, tm, sub_mix, pair_e)

    dx, d_win, in_vec = _bwd_in(dproj, xs, dh1, norm_mix_g, win_g, tm)

    last = [d_win, d_wout.reshape(N_CHIPS, D_MODEL // N_CHIPS, D_MODEL)]
    pair_l = _add_pairs(last, _swap_halves(last, "last"), pos, 8, "last")

    g_small = {
        "norm_mix_g": in_vec[0:1], "pool_w": d_pw, "pool_b": mix_vec[0:1], "pool_scale": mix_vec[1:2],
        "conv_b": mix_vec[2:3], "gate_a_w": _diag_blocks(d_wa), "gate_a_b": mix_vec[3:4],
        "gate_x_w": _diag_blocks(d_wx), "gate_x_b": mix_vec[4:5], "lru_L": mix_vec[5:6], "norm_mlp_g": mlp_vec[0:1],
        "norm_ple_g": head_vec[0:1], "b_ple_gate": head_vec[2:3], "norm_final_g": head_vec[1:2],
    }
    d_cw = jnp.transpose(mix_vec[8:8 + CONV_WIDTH].reshape(CONV_WIDTH, N_CHIPS, LANES), (1, 0, 2)).reshape(-1, LANES)
    pieces = [_pad8(_rows128(g_small[k])) for k in SMALL] + [d_cw, _pad8(head_vec[3:4, :LANES])]
    offs = [0]
    for pc in pieces:
        offs.append(offs[-1] + pc.shape[0])
    if offs[-1] % 16:
        pieces.append(jnp.zeros((8, LANES), F32))
    got_l, red = _scatter_and_allreduce(pair_l, jnp.concatenate(pieces, axis=0))
    g_big = _join_halves(_sum_chips(pair_l + pair_e, got_l + got_e, pos, 8))
    loss = red[offs[-2], 0]
    g_cw = lax.dynamic_slice(red, (offs[len(SMALL)] + CONV_WIDTH * chip, 0), (CONV_WIDTH, LANES))

    def packed(src):
        return jnp.concatenate([_pad8(_rows128(src[k])) for k in SMALL] + [_pad8(src["conv_w"][0])], axis=0)

    n_small = offs[len(SMALL)]
    g_pack = jnp.concatenate([red[:n_small], _pad8(g_cw)], axis=0)
    (d_pack,), (m_pack,), (v_pack,), _ = _adamw([packed(W)], [g_pack], [packed(M)], [packed(V)], 1, "adamw_small")

    big2d = lambda src: [src[k][0] for k in BIG]
    d_big, m_big, v_big, g_big = _adamw(big2d(W), g_big, big2d(M), big2d(V), 8, "adamw_big")

    def unpack(pack, big_list):
        out = {}
        for idx, k in enumerate(SMALL):
            n_el = W[k].size
            out[k] = pack[offs[idx]:offs[idx + 1]].reshape(-1)[:n_el].reshape(W[k].shape)
        out["conv_w"] = pack[n_small:n_small + CONV_WIDTH].reshape(W["conv_w"].shape)
        for k, a in zip(BIG, big_list):
            out[k] = a.reshape(W[k].shape)
        return out

    grads = unpack(g_pack, g_big)
    deltas = unpack(d_pack, d_big)
    new_m = unpack(m_pack, m_big)
    new_v = unpack(v_pack, v_big)
    return (loss, dx[None], *[grads[k] for k in ORDER], *[deltas[k] for k in ORDER],
            *[new_m[k] for k in ORDER], *[new_v[k] for k in ORDER])
```

```python
import functools

import jax
import jax.numpy as jnp
from jax import lax
from jax.experimental import pallas as pl
from jax.experimental.pallas import tpu as pltpu

F32 = jnp.float32
BF16 = jnp.bfloat16
MESH = pl.DeviceIdType.MESH

D_MODEL = 1024
D_POOL = 512
D_LRU = 512
POOL_WINDOWS = (2, 4, 8, 16)
POOL_GROUP = 128
POOL_HALO = 16
CONV_WIDTH = 4
CONV_HALO = 8
PASS_STEPS = 2
LRU_HEADS = 8
LRU_BLOCK = 64
GATE_PAIRS = 4
LRU_C = 8.0
D_FF = 4096
PLE_DIM = 256
D_IN_PROJ = 1536
RMS_EPS = 1e-6
N_CHIPS = 4
FF_BLOCK = D_FF // N_CHIPS
FF_PAIR = 2

ADAM_LR = 0.001
ADAM_B1 = 0.9
ADAM_B2 = 0.999
ADAM_EPS = 1e-08
ADAM_WD = 0.01
ADAM_STEP = 10
ADAM_C1 = 1.0 / (1.0 - ADAM_B1 ** ADAM_STEP)
ADAM_C2 = 1.0 / (1.0 - ADAM_B2 ** ADAM_STEP)

VMEM_LIMIT = 56 * 1024 * 1024
GELU_C = 0.7978845608028654
GELU_A = 0.044715

NT = (((1,), (1,)), ((), ()))
TN = (((0,), (0,)), ((), ()))


def _dot(a, b):
    return jnp.dot(a, b, preferred_element_type=F32)


def _dot_nt(a, b):
    return lax.dot_general(a, b, NT, preferred_element_type=F32)


def _dot_tn(a, b):
    return lax.dot_general(a, b, TN, preferred_element_type=F32)


def _params(*sem):
    return pltpu.CompilerParams(dimension_semantics=sem, vmem_limit_bytes=VMEM_LIMIT)


def _full(shape):
    nd = len(shape)
    return pl.BlockSpec(shape, lambda *_: (0,) * nd)


def _resident(shape):
    nd = len(shape)
    return pl.BlockSpec(shape, lambda *_: (0,) * nd, pipeline_mode=pl.Buffered(1))


def _rstd(x):
    return lax.rsqrt(jnp.mean(x * x, axis=-1, keepdims=True) + RMS_EPS)


def _rms_bwd(x, g, dz):
    xr = x * _rstd(x)
    r = _rstd(x)
    dyg = dz * g
    dx = r * (dyg - xr * jnp.mean(dyg * xr, axis=-1, keepdims=True))
    dg = jnp.sum(dz * xr, axis=0, keepdims=True)
    return dx, dg


def _sigmoid(x):
    return 1.0 / (1.0 + jnp.exp(-x))


def _log_sigmoid(v):
    u = jnp.exp(-jnp.abs(v))
    w = 1.0 + u
    l1p = jnp.where(w == 1.0, u, jnp.log(w) * u / jnp.where(w == 1.0, 1.0, w - 1.0))
    return jnp.minimum(v, 0.0) - l1p


def _gelu(x):
    t = jnp.tanh(GELU_C * (x + GELU_A * x * x * x))
    return 0.5 * x * (1.0 + t), t


def _gelu_grad(x, t):
    return 0.5 * (1.0 + t) + 0.5 * x * (1.0 - t * t) * GELU_C * (1.0 + 3.0 * GELU_A * x * x)


def _rows(shape, t0):
    return lax.broadcasted_iota(jnp.int32, shape, 0) + t0


def _pool_inv(tm, t0):
    rows = _rows((tm, POOL_GROUP), t0)
    return jnp.concatenate([1.0 / jnp.minimum(rows + 1, w).astype(F32) for w in POOL_WINDOWS], axis=1)


def _pool_diff(u_pool, prev, t0):
    tm = u_pool.shape[0]
    rows = _rows((tm, POOL_GROUP), t0)
    outs, invs = [], []
    for g, w in enumerate(POOL_WINDOWS):
        sl = slice(POOL_GROUP * g, POOL_GROUP * (g + 1))
        ug = u_pool[:, sl]
        s = jnp.concatenate([prev[:, sl], ug], axis=0)
        k = 1
        while k < w:
            s = s + pltpu.roll(s, k, 0)
            k *= 2
        inv = 1.0 / jnp.minimum(rows + 1, w).astype(F32)
        outs.append(s[POOL_HALO:] * inv - ug)
        invs.append(inv)
    return jnp.concatenate(outs, axis=1), jnp.concatenate(invs, axis=1)


def _pool_diff_bwd(dd, inv, nxt):
    ddc = dd * inv
    outs = []
    for g, w in enumerate(POOL_WINDOWS):
        sl = slice(POOL_GROUP * g, POOL_GROUP * (g + 1))
        s = jnp.concatenate([ddc[:, sl], nxt[:, sl]], axis=0)
        n = s.shape[0]
        k = 1
        while k < w:
            s = s + pltpu.roll(s, n - k, 0)
            k *= 2
        outs.append(s[:n - POOL_HALO] - dd[:, sl])
    return jnp.concatenate(outs, axis=1), ddc


def _conv_taps(u, prev):
    ext = jnp.concatenate([prev, u], axis=0)
    return [pltpu.roll(ext, CONV_WIDTH - 1 - k, 0)[CONV_HALO:] if k < CONV_WIDTH - 1 else u for k in range(CONV_WIDTH)]


def _scan_rev(a, b, carry):
    tm = a.shape[0]
    r8 = _rows(a.shape, 0) & 7
    k = 1
    while k < 8:
        ar = pltpu.roll(a, tm - k, 0)
        br = pltpu.roll(b, tm - k, 0)
        m = r8 < 8 - k
        b = jnp.where(m, a * br + b, b)
        a = jnp.where(m, a * ar, a)
        k *= 2
    outs = []
    c = jnp.broadcast_to(carry, (8, a.shape[1]))
    for g in reversed(range(tm // 8)):
        gg = b[8 * g:8 * g + 8] + a[8 * g:8 * g + 8] * c
        outs.insert(0, gg)
        c = jnp.broadcast_to(gg[0:1], c.shape)
    return jnp.concatenate(outs, axis=0)


def _gate_dot(xbb, w_ref):
    return jnp.concatenate(
        [_dot(xbb[:, POOL_GROUP * q:POOL_GROUP * (q + 1)], w_ref[q]) for q in range(GATE_PAIRS)], axis=1)


def _gate_dot_nt(db, w_ref):
    return jnp.concatenate(
        [_dot_nt(db[:, POOL_GROUP * q:POOL_GROUP * (q + 1)], w_ref[q]) for q in range(GATE_PAIRS)], axis=1)


def _lru_decay(r, lsl8, t0):
    a = jnp.exp(r * lsl8)
    first = _rows(r.shape, t0) == 0
    m2 = 1.0 - a * a
    rs = lax.rsqrt(jnp.maximum(m2, jnp.finfo(F32).tiny))
    mult = jnp.where(first, 1.0, m2 * rs)
    return a, mult, rs, first


def _lru_chunks_fwd(r_ref, i_ref, xb_ref, ug_ref, gp_ref, h_ref, y_ref, row0, n_rows, lsl8, t0, hc):
    lsl = jnp.broadcast_to(lsl8, (8, D_LRU))
    sub8 = lax.broadcasted_iota(jnp.int32, (8, D_LRU), 0)

    def chunk(j, hc):
        o = j * 8
        rows = pl.ds(o, 8)
        r = _sigmoid(r_ref[rows, :])
        ig = _sigmoid(i_ref[rows, :])
        r_ref[rows, :] = r
        i_ref[rows, :] = ig
        a = jnp.exp(r * lsl)
        m2 = 1.0 - a * a
        mult = jnp.where(sub8 + (t0 + o) == 0, 1.0, m2 * lax.rsqrt(jnp.maximum(m2, jnp.finfo(F32).tiny)))
        b = mult * (ig * xb_ref[rows, :])
        k = 1
        while k < 8:
            m = sub8 >= k
            b = jnp.where(m, a * pltpu.roll(b, k, 0) + b, b)
            a = jnp.where(m, a * pltpu.roll(a, k, 0), a)
            k *= 2
        h = b + a * hc
        h_ref[pl.ds(row0 + o, 8), :] = h
        ug = ug_ref[rows, :]
        gl, th = _gelu(ug)
        y_ref[rows, :] = h * gl
        ug_ref[rows, :] = gl
        gp_ref[rows, :] = _gelu_grad(ug, th)
        return jnp.broadcast_to(h[7:8, :], (8, D_LRU))

    for j in range(n_rows // 8):
        hc = chunk(j, hc)
    return hc


def _fwd_mix(x, g1, w_in, pool_w, pool_b, pool_scale, conv_w, conv_b, wa, ba, wx, bx, lru_l, w_out, tm, sub, late):
    s_len = x.shape[0]
    n = s_len // tm
    nl = len(late)

    def body(x_ref, g1_ref, win_ref, pw_ref, pb_ref, ps_ref, cw_ref, cb_ref, wa_ref, ba_ref, wx_ref, bx_ref, l_ref,
             wout_ref, *rest):
        ulru_ref, d_ref, gg_ref, rg_ref, h_ref, h1_ref = rest[nl:nl + 6]
        late_ref = rest[nl + 6:2 * nl + 6]
        cpool, clru, ch, s_r, s_i, s_xb, s_ug, s_gp, s_y, ssem, rsem = rest[2 * nl + 6:]
        i = pl.program_id(0)

        @pl.when(i == 0)
        def _():
            _gather_send(late_ref, late, ssem, rsem)
            cpool[...] = jnp.zeros_like(cpool)
            clru[...] = jnp.zeros_like(clru)
            ch[...] = jnp.zeros_like(ch)

        lsl8 = LRU_C * _log_sigmoid(l_ref[...])
        cp, cl, hc = cpool[...], clru[...], ch[...]

        def in_proj(k):
            rs = slice(k * sub, (k + 1) * sub)
            xv = x_ref[rs, :]
            zb = (xv * _rstd(xv) * g1_ref[...]).astype(BF16)
            proj = jnp.concatenate([_dot(zb, win_ref[j]) for j in range(N_CHIPS)], axis=1)
            ulru_ref[rs, :] = proj[:, D_POOL:D_POOL + D_LRU].astype(BF16)
            return xv, proj

        nxt = in_proj(0)
        for k in range(tm // sub):
            rs = slice(k * sub, (k + 1) * sub)
            t0 = i * tm + k * sub
            xv, proj = nxt
            if k + 1 < tm // sub:
                nxt = in_proj(k + 1)
            u_pool = proj[:, :D_POOL]
            u_lru = proj[:, D_POOL:D_POOL + D_LRU]
            u_gate = proj[:, D_POOL + D_LRU:]

            d, _ = _pool_diff(u_pool, cp, t0)
            cp = u_pool[sub - POOL_HALO:]
            db = d.astype(BF16)
            d_ref[rs, :] = db
            yp = jnp.concatenate(
                [_dot(db[:, POOL_GROUP * g:POOL_GROUP * (g + 1)], pw_ref[g]) for g in range(len(POOL_WINDOWS))], axis=1)
            y_pool = (yp + pb_ref[...]) * ps_ref[...]

            taps = _conv_taps(u_lru, cl)
            cl = u_lru[sub - CONV_HALO:]
            xb = cb_ref[...]
            for q in range(CONV_WIDTH):
                xb = xb + taps[q] * cw_ref[q:q + 1, :]
            xbb = xb.astype(BF16)
            s_r[...] = _gate_dot(xbb, wa_ref) + ba_ref[...]
            s_i[...] = _gate_dot(xbb, wx_ref) + bx_ref[...]
            s_xb[...] = xb
            s_ug[...] = u_gate
            hc = _lru_chunks_fwd(s_r, s_i, s_xb, s_ug, s_gp, h_ref, s_y, k * sub, sub, lsl8, t0, hc)
            rg_ref[rs, :] = jnp.concatenate([s_r[...], s_i[...]], axis=1).astype(BF16)
            gg_ref[rs, :] = jnp.concatenate([s_ug[...], s_gp[...]], axis=1).astype(BF16)
            cat = jnp.concatenate([y_pool, s_y[...]], axis=1).astype(BF16)
            h1_ref[rs, :] = xv + _dot(cat, wout_ref[...])
        cpool[...] = cp
        clru[...] = cl
        ch[...] = hc

        @pl.when(i == max(n - PASS_STEPS, 0))
        def _():
            _gather_pass(late_ref, late, ssem, rsem)

        @pl.when(i == n - 1)
        def _():
            _gather_done(late_ref, late, ssem, rsem)

    row = lambda w: pl.BlockSpec((tm, w), lambda i: (i, 0))
    ins = [x, g1, w_in, pool_w, pool_b, pool_scale, conv_w, conv_b, wa, ba, wx, bx, lru_l, w_out]
    outs = pl.pallas_call(
        body, name="fwd_mix", grid=(n,),
        in_specs=[row(D_MODEL)] + [_resident(a.shape) for a in ins[1:]] + [ANY] * nl,
        out_specs=[row(D_LRU), row(D_POOL), row(2 * D_LRU), row(2 * D_LRU), row(D_LRU), row(D_MODEL)] + [ANY] * nl,
        out_shape=[jax.ShapeDtypeStruct((s_len, D_LRU), BF16), jax.ShapeDtypeStruct((s_len, D_POOL), BF16),
                   jax.ShapeDtypeStruct((s_len, 2 * D_LRU), BF16), jax.ShapeDtypeStruct((s_len, 2 * D_LRU), BF16),
                   jax.ShapeDtypeStruct((s_len, D_LRU), F32), jax.ShapeDtypeStruct((s_len, D_MODEL), F32)]
        + [jax.ShapeDtypeStruct(a.shape, a.dtype) for a in late],
        input_output_aliases={len(ins) + k: 6 + k for k in range(nl)},
        scratch_shapes=[pltpu.VMEM((POOL_HALO, D_POOL), F32), pltpu.VMEM((CONV_HALO, D_LRU), F32),
                        pltpu.VMEM((8, D_LRU), F32)] + [pltpu.VMEM((sub, D_LRU), F32)] * 6 + _gather_sems(nl),
        compiler_params=_params("arbitrary"),
    )(*ins, *late)
    return outs[:6], outs[6:]


def _fwd_mlp(h1, g2, w_up, w_down, tm, late):
    s_len = h1.shape[0]
    n = s_len // tm
    nl = len(late)

    def body(h1_ref, g2_ref, wup_ref, wdn_ref, *rest):
        z2_ref, ru_ref, h2_ref = rest[nl:nl + 3]
        late_ref = rest[nl + 3:2 * nl + 3]
        ssem, rsem = rest[2 * nl + 3:]
        i = pl.program_id(0)

        @pl.when(i == 0)
        def _():
            _gather_send(late_ref, late, ssem, rsem)

        hv = h1_ref[...]
        zb = (hv * _rstd(hv) * g2_ref[...]).astype(BF16)
        z2_ref[...] = zb
        acc = hv
        for j in range(N_CHIPS):
            ru = jnp.maximum(_dot(zb, wup_ref[j]), 0.0)
            ru_ref[:, FF_BLOCK * j:FF_BLOCK * (j + 1)] = ru.astype(BF16)
            acc = acc + _dot((ru * ru).astype(BF16), wdn_ref[j])
        h2_ref[...] = acc

        @pl.when(i == max(n - PASS_STEPS, 0))
        def _():
            _gather_pass(late_ref, late, ssem, rsem)

        @pl.when(i == n - 1)
        def _():
            _gather_done(late_ref, late, ssem, rsem)

    row = lambda w: pl.BlockSpec((tm, w), lambda i: (i, 0))
    outs = pl.pallas_call(
        body, name="fwd_mlp", grid=(n,),
        in_specs=[row(D_MODEL), _full(g2.shape), _resident(w_up.shape), _resident(w_down.shape)] + [ANY] * nl,
        out_specs=[row(D_MODEL), row(D_FF), row(D_MODEL)] + [ANY] * nl,
        out_shape=[jax.ShapeDtypeStruct((s_len, D_MODEL), BF16), jax.ShapeDtypeStruct((s_len, D_FF), BF16),
                   jax.ShapeDtypeStruct((s_len, D_MODEL), F32)] + [jax.ShapeDtypeStruct(a.shape, a.dtype) for a in late],
        input_output_aliases={4 + k: 3 + k for k in range(nl)},
        scratch_shapes=_gather_sems(nl),
        compiler_params=_params("arbitrary"),
    )(h1, g2, w_up, w_down, *late)
    return outs[:3], outs[3:]


def _head(h2, p, target, g3, w_pg, b_pg, w_pp, g4, tm):
    s_len = h2.shape[0]
    n = s_len // tm

    def body(h2_ref, p_ref, t_ref, g3_ref, wpg_ref, bpg_ref, wpp_ref, g4_ref,
             dh2_ref, dh2b_ref, dwpg_ref, dwpp_ref, vec_ref, a_pg, a_pp, a_vec):
        i = pl.program_id(0)

        @pl.when(i == 0)
        def _():
            a_pg[...] = jnp.zeros_like(a_pg)
            a_pp[...] = jnp.zeros_like(a_pp)
            a_vec[...] = jnp.zeros_like(a_vec)

        h2v = h2_ref[...]
        g3v = g3_ref[...]
        g4v = g4_ref[...]
        z3 = (h2v * _rstd(h2v) * g3v).astype(BF16)
        gate = _sigmoid(_dot(z3, wpg_ref[...]) + bpg_ref[...])
        pb = p_ref[...].astype(BF16)
        pp = _dot(pb, wpp_ref[...])
        h3 = h2v + gate * pp
        r4 = _rstd(h3)
        diff = h3 * r4 * g4v - t_ref[...]
        loss = 0.5 * jnp.sum(jnp.mean(diff * diff, axis=-1, keepdims=True), axis=0, keepdims=True)
        dy = diff * (1.0 / D_MODEL)
        dh3, dg4 = _rms_bwd(h3, g4v, dy)
        dpp = (dh3 * gate).astype(BF16)
        dpre = dh3 * pp * gate * (1.0 - gate)
        dpreb = dpre.astype(BF16)
        dz3 = _dot_nt(dpreb, wpg_ref[...])
        dx, dg3 = _rms_bwd(h2v, g3v, dz3)
        dh2 = dh3 + dx
        dh2_ref[...] = dh2
        dh2b_ref[...] = dh2.astype(BF16)
        a_pg[...] += _dot_tn(z3, dpreb)
        a_pp[...] += _dot_tn(pb, dpp)
        a_vec[0:1, :] += dg3
        a_vec[1:2, :] += dg4
        a_vec[2:3, :] += jnp.sum(dpre, axis=0, keepdims=True)
        a_vec[3:4, :] += jnp.broadcast_to(loss, (1, D_MODEL))

        @pl.when(i == n - 1)
        def _():
            dwpg_ref[...] = a_pg[...].astype(BF16)
            for j in range(N_CHIPS):
                dwpp_ref[j] = a_pp[:, PLE_DIM * j:PLE_DIM * (j + 1)].astype(BF16)
            vec_ref[...] = a_vec[...]

    row = lambda w: pl.BlockSpec((tm, w), lambda i: (i, 0))
    ins = [h2, p, target, g3, w_pg, b_pg, w_pp, g4]
    return pl.pallas_call(
        body, name="head", grid=(n,),
        in_specs=[row(D_MODEL), row(PLE_DIM), row(D_MODEL)] + [_resident(a.shape) for a in ins[3:]],
        out_specs=[row(D_MODEL), row(D_MODEL), _full((D_MODEL, D_MODEL)), _full((N_CHIPS, PLE_DIM, PLE_DIM)),
                   _full((8, D_MODEL))],
        out_shape=[jax.ShapeDtypeStruct((s_len, D_MODEL), F32), jax.ShapeDtypeStruct((s_len, D_MODEL), BF16),
                   jax.ShapeDtypeStruct((D_MODEL, D_MODEL), BF16),
                   jax.ShapeDtypeStruct((N_CHIPS, PLE_DIM, PLE_DIM), BF16), jax.ShapeDtypeStruct((8, D_MODEL), F32)],
        scratch_shapes=[pltpu.VMEM((D_MODEL, D_MODEL), F32), pltpu.VMEM((PLE_DIM, D_MODEL), F32),
                        pltpu.VMEM((8, D_MODEL), F32)],
        compiler_params=_params("arbitrary"),
    )(*ins)


def _bwd_mlp_x(dh2, ru, h1, g2, w_up, w_down, tm):
    s_len = dh2.shape[0]
    n = s_len // tm

    def body(dh2_ref, ru_ref, h1_ref, g2_ref, wup_ref, wdn_ref, dup_ref, dh1_ref, dg2_ref, a_g):
        i = pl.program_id(0)

        @pl.when(i == 0)
        def _():
            a_g[...] = jnp.zeros_like(a_g)

        dh2v = dh2_ref[...]
        dhb = dh2v.astype(BF16)
        acc = jnp.zeros((tm, D_MODEL), F32)
        for j in range(N_CHIPS):
            sl = slice(FF_BLOCK * j, FF_BLOCK * (j + 1))
            dup = (_dot_nt(dhb, wdn_ref[j]) * (2.0 * ru_ref[:, sl].astype(F32))).astype(BF16)
            dup_ref[:, sl] = dup
            acc = acc + _dot_nt(dup, wup_ref[j])
        dx, dg = _rms_bwd(h1_ref[...], g2_ref[...], acc)
        dh1_ref[...] = dh2v + dx
        a_g[0:1, :] += dg

        @pl.when(i == n - 1)
        def _():
            dg2_ref[...] = a_g[...]

    row = lambda w: pl.BlockSpec((tm, w), lambda i: (i, 0))
    return pl.pallas_call(
        body, name="bwd_mlp_x", grid=(n,),
        in_specs=[row(D_MODEL), row(D_FF), row(D_MODEL), _full(g2.shape), _resident(w_up.shape), _resident(w_down.shape)],
        out_specs=[row(D_FF), row(D_MODEL), _full((8, D_MODEL))],
        out_shape=[jax.ShapeDtypeStruct((s_len, D_FF), BF16), jax.ShapeDtypeStruct((s_len, D_MODEL), F32),
                   jax.ShapeDtypeStruct((8, D_MODEL), F32)],
        scratch_shapes=[pltpu.VMEM((8, D_MODEL), F32)],
        compiler_params=_params("arbitrary"),
    )(dh2, ru, h1, g2, w_up, w_down)


def _bwd_mlp_w(z2, dup, ru, dh2, tk):
    s_len = z2.shape[0]
    n = s_len // tk

    def body(z2_ref, dup_ref, ru_ref, dh2_ref, dwup_ref, dwdn_ref, a_up, a_dn):
        t = pl.program_id(1)

        @pl.when(t == 0)
        def _():
            a_up[...] = jnp.zeros_like(a_up)
            a_dn[...] = jnp.zeros_like(a_dn)

        for b in range(FF_PAIR):
            sl = slice(FF_BLOCK * b, FF_BLOCK * (b + 1))
            ruv = ru_ref[:, sl]
            a_up[b] += _dot_tn(z2_ref[...], dup_ref[:, sl])
            a_dn[b] += _dot_tn(ruv * ruv, dh2_ref[...])

        @pl.when(t == n - 1)
        def _():
            dwup_ref[...] = a_up[...].astype(BF16)
            dwdn_ref[...] = a_dn[...].astype(BF16)

    tile = pl.BlockSpec((tk, D_MODEL), lambda j, t: (t, 0))
    ffb = pl.BlockSpec((tk, FF_PAIR * FF_BLOCK), lambda j, t: (t, j))
    return pl.pallas_call(
        body, name="bwd_mlp_w", grid=(N_CHIPS // FF_PAIR, n),
        in_specs=[tile, ffb, ffb, tile],
        out_specs=[pl.BlockSpec((FF_PAIR, D_MODEL, FF_BLOCK), lambda j, t: (j, 0, 0)),
                   pl.BlockSpec((FF_PAIR, FF_BLOCK, D_MODEL), lambda j, t: (j, 0, 0))],
        out_shape=[jax.ShapeDtypeStruct((N_CHIPS, D_MODEL, FF_BLOCK), BF16),
                   jax.ShapeDtypeStruct((N_CHIPS, FF_BLOCK, D_MODEL), BF16)],
        scratch_shapes=[pltpu.VMEM((FF_PAIR, D_MODEL, FF_BLOCK), F32), pltpu.VMEM((FF_PAIR, FF_BLOCK, D_MODEL), F32)],
        compiler_params=_params("arbitrary", "arbitrary"),
    )(z2, dup, ru, dh2)


MIX_VEC_ROWS = 16


def _bwd_mix(dh1, ulru, dpl, gg, rg, h, pool_w, pool_b, pool_scale, conv_w, conv_b, wa, ba, wx, bx, lru_l, w_out, tm, sub, early):
    s_len = dh1.shape[0]
    n = s_len // tm
    ng = len(POOL_WINDOWS)
    ne_ = len(early)

    def body(dh1_ref, ulru_ref, d_ref, gg_ref, rg_ref, h_ref, ulruh_ref, hh_ref, pw_ref, pb_ref, ps_ref, cw_ref, cb_ref,
             wa_ref, ba_ref, wx_ref, bx_ref, l_ref, wout_ref, *rest):
        early_ref = rest[:ne_]
        dproj_ref, dwout_ref, dpw_ref, dwa_ref, dwx_ref, vec_ref = rest[ne_:ne_ + 6]
        got_ref = rest[ne_ + 6:2 * ne_ + 6]
        a_out, a_pw, a_wa, a_wx, a_vec, c_g, c_dxb, c_ddc, ssem, rsem = rest[2 * ne_ + 6:]
        q = pl.program_id(0)
        i = n - 1 - q

        @pl.when(q == 0)
        def _():
            _scatter_send(early_ref, got_ref, early, ssem, rsem)
            for r in (a_out, a_pw, a_wa, a_wx, a_vec, c_g, c_dxb, c_ddc):
                r[...] = jnp.zeros_like(r)

        has_prev = (i > 0).astype(F32)
        lv = l_ref[...]
        lsl8 = LRU_C * _log_sigmoid(lv)
        cg, cdxb, cddc = c_g[0:1, :], c_dxb[...], c_ddc[...]
        vec, cats, dhbs = {}, [], []

        def add(row, v):
            vec[row] = v if row not in vec else vec[row] + v

        def first_matmuls(k):
            dh1b = dh1_ref[k * sub:(k + 1) * sub, :].astype(BF16)
            db = d_ref[k * sub:(k + 1) * sub, :]
            ypre = jnp.concatenate(
                [_dot(db[:, POOL_GROUP * g:POOL_GROUP * (g + 1)], pw_ref[g]) for g in range(ng)], axis=1) + pb_ref[...]
            return dh1b, _dot_nt(dh1b, wout_ref[...]), db, ypre

        nxt = first_matmuls(tm // sub - 1)
        for k in reversed(range(tm // sub)):
            rs = slice(k * sub, (k + 1) * sub)
            t0 = i * tm + k * sub
            dh1b, dcat, db, ypre = nxt
            if k > 0:
                nxt = first_matmuls(k - 1)
            dy_pool = dcat[:, :D_POOL]
            dy_lru = dcat[:, D_POOL:]

            u_lru = ulru_ref[rs, :].astype(F32)
            if k > 0:
                halo = ulru_ref[k * sub - POOL_HALO:k * sub, :].astype(F32)
                h_prev_row = h_ref[k * sub - 1:k * sub, :]
            else:
                halo = ulruh_ref[...].astype(F32) * has_prev
                h_prev_row = hh_ref[7:8, :] * has_prev

            inv = _pool_inv(sub, t0)
            dyp = dy_pool * ps_ref[...]
            dypb = dyp.astype(BF16)
            dds = []
            for g in range(ng):
                sl = slice(POOL_GROUP * g, POOL_GROUP * (g + 1))
                a_pw[g] += _dot_tn(db[:, sl], dypb[:, sl])
                dds.append(_dot_nt(dypb[:, sl], pw_ref[g]))
            du_pool, ddc = _pool_diff_bwd(jnp.concatenate(dds, axis=1), inv, cddc)
            cddc = ddc[:POOL_HALO]
            add(0, jnp.sum(dyp, axis=0, keepdims=True))
            add(1, jnp.sum(dy_pool * ypre, axis=0, keepdims=True))

            taps = _conv_taps(u_lru, halo[POOL_HALO - CONV_HALO:])
            xb = cb_ref[...]
            for c in range(CONV_WIDTH):
                xb = xb + taps[c] * cw_ref[c:c + 1, :]
            r = rg_ref[rs, :D_LRU].astype(F32)
            ig = rg_ref[rs, D_LRU:].astype(F32)
            a, mult, inv_mult, first = _lru_decay(r, lsl8, t0)
            hv = h_ref[rs, :]
            gl = gg_ref[rs, :D_LRU].astype(F32)
            du_gate = dy_lru * hv * gg_ref[rs, D_LRU:].astype(F32)
            cats.insert(0, jnp.concatenate([ypre * ps_ref[...], hv * gl], axis=1).astype(BF16))
            dhbs.insert(0, dh1b)
            last = _rows(a.shape, 0) == sub - 1
            a_next = jnp.where(last, 1.0, pltpu.roll(a, sub - 1, 0))
            gh = _scan_rev(a_next, dy_lru * gl, cg)
            cg = a[0:1, :] * gh[0:1, :]
            h_prev = jnp.where(_rows(hv.shape, 0) == 0, h_prev_row, pltpu.roll(hv, 1, 0))
            gix = gh * ig * xb
            dla = gh * h_prev * a - jnp.where(first, 0.0, gix * a * a * inv_mult)
            dpre_r = dla * lsl8 * r * (1.0 - r)
            dpre_i = gh * mult * xb * ig * (1.0 - ig)
            dprb = dpre_r.astype(BF16)
            dpib = dpre_i.astype(BF16)
            xbb = xb.astype(BF16)
            for pair in range(GATE_PAIRS):
                sl = slice(POOL_GROUP * pair, POOL_GROUP * (pair + 1))
                a_wa[pair] += _dot_tn(xbb[:, sl], dprb[:, sl])
                a_wx[pair] += _dot_tn(xbb[:, sl], dpib[:, sl])
            dxb = gh * mult * ig + _gate_dot_nt(dprb, wa_ref) + _gate_dot_nt(dpib, wx_ref)
            add(2, jnp.sum(dxb, axis=0, keepdims=True))
            add(3, jnp.sum(dpre_r, axis=0, keepdims=True))
            add(4, jnp.sum(dpre_i, axis=0, keepdims=True))
            add(5, jnp.sum(dla * r, axis=0, keepdims=True))
            ext = jnp.concatenate([dxb, cdxb], axis=0)
            cdxb = dxb[:CONV_HALO]
            ne = sub + CONV_HALO
            du_lru = dxb * cw_ref[CONV_WIDTH - 1:CONV_WIDTH, :]
            for c in range(CONV_WIDTH):
                add(8 + c, jnp.sum(dxb * taps[c], axis=0, keepdims=True))
                if c < CONV_WIDTH - 1:
                    du_lru = du_lru + pltpu.roll(ext, ne - (CONV_WIDTH - 1 - c), 0)[:sub] * cw_ref[c:c + 1, :]
            dproj_ref[rs, :] = jnp.concatenate([du_pool, du_lru, du_gate], axis=1).astype(BF16)
        a_out[...] += _dot_tn(jnp.concatenate(cats, axis=0), jnp.concatenate(dhbs, axis=0))
        c_g[...] = jnp.broadcast_to(cg, c_g.shape)
        c_dxb[...] = cdxb
        c_ddc[...] = cddc
        for row, v in vec.items():
            a_vec[row:row + 1, :] += v

        @pl.when(q == n - 1)
        def _():
            dwout_ref[...] = a_out[...].astype(BF16)
            dpw_ref[...] = a_pw[...]
            dwa_ref[...] = a_wa[...]
            dwx_ref[...] = a_wx[...]
            vec_ref[...] = a_vec[...]
            vec_ref[5:6, :] = a_vec[5:6, :] * (LRU_C * _sigmoid(-lv))
            _scatter_done(got_ref, early, ssem, rsem)

    rev =lambda w: pl.BlockSpec((tm, w), lambda q: (n - 1 - q, 0))
    halo_p = pl.BlockSpec((POOL_HALO, D_LRU), lambda q: (jnp.maximum((n - 1 - q) * (tm // POOL_HALO) - 1, 0), 0))
    halo_h = pl.BlockSpec((8, D_LRU), lambda q: (jnp.maximum((n - 1 - q) * (tm // 8) - 1, 0), 0))
    wts = [pool_w, pool_b, pool_scale, conv_w, conv_b, wa, ba, wx, bx, lru_l, w_out]
    outs = pl.pallas_call(
        body, name="bwd_mix", grid=(n,),
        in_specs=[rev(D_MODEL), rev(D_LRU), rev(D_POOL), rev(2 * D_LRU), rev(2 * D_LRU), rev(D_LRU), halo_p, halo_h]
        + [_resident(a.shape) for a in wts]
        + [ANY] * ne_,
        out_specs=[rev(D_IN_PROJ), _full((D_MODEL, D_MODEL)), _full((ng, POOL_GROUP, POOL_GROUP)),
                   _full(wa.shape), _full(wa.shape), _full((MIX_VEC_ROWS, D_LRU))] + [ANY] * ne_,
        out_shape=[jax.ShapeDtypeStruct((s_len, D_IN_PROJ), BF16), jax.ShapeDtypeStruct((D_MODEL, D_MODEL), BF16),
                   jax.ShapeDtypeStruct((ng, POOL_GROUP, POOL_GROUP), F32), jax.ShapeDtypeStruct(wa.shape, F32),
                   jax.ShapeDtypeStruct(wa.shape, F32), jax.ShapeDtypeStruct((MIX_VEC_ROWS, D_LRU), F32)]
        + _scatter_shapes(early),
        scratch_shapes=[pltpu.VMEM((D_MODEL, D_MODEL), F32), pltpu.VMEM((ng, POOL_GROUP, POOL_GROUP), F32),
                        pltpu.VMEM(wa.shape, F32), pltpu.VMEM(wa.shape, F32),
                        pltpu.VMEM((MIX_VEC_ROWS, D_LRU), F32), pltpu.VMEM((8, D_LRU), F32),
                        pltpu.VMEM((CONV_HALO, D_LRU), F32), pltpu.VMEM((POOL_HALO, D_POOL), F32)] + _scatter_sems(ne_),
        compiler_params=_params("arbitrary"),
    )(dh1, ulru, dpl, gg, rg, h, ulru, h, *wts, *early)
    return outs[:6], outs[6:]


def _bwd_in(dproj, x, dh1, g1, w_in, tm):
    s_len = x.shape[0]
    n = s_len // tm
    cb = D_IN_PROJ // N_CHIPS

    def body(dp_ref, x_ref, dh1_ref, g1_ref, win_ref, dx_ref, dwin_ref, dg1_ref, a_w, a_g):
        i = pl.program_id(0)

        @pl.when(i == 0)
        def _():
            a_w[...] = jnp.zeros_like(a_w)
            a_g[...] = jnp.zeros_like(a_g)

        dp = dp_ref[...]
        xv = x_ref[...]
        zb = (xv * _rstd(xv) * g1_ref[...]).astype(BF16)
        dz = jnp.zeros((tm, D_MODEL), F32)
        for j in range(N_CHIPS):
            dpj = dp[:, cb * j:cb * (j + 1)]
            dz = dz + _dot_nt(dpj, win_ref[j])
            a_w[j] += _dot_tn(zb, dpj)
        dx, dg = _rms_bwd(xv, g1_ref[...], dz)
        dx_ref[...] = dh1_ref[...] + dx
        a_g[0:1, :] += dg

        @pl.when(i == n - 1)
        def _():
            dwin_ref[...] = a_w[...].astype(BF16)
            dg1_ref[...] = a_g[...]

    row = lambda w: pl.BlockSpec((tm, w), lambda i: (i, 0))
    return pl.pallas_call(
        body, name="bwd_in", grid=(n,),
        in_specs=[row(D_IN_PROJ), row(D_MODEL), row(D_MODEL), _resident(g1.shape), _resident(w_in.shape)],
        out_specs=[row(D_MODEL), _full(w_in.shape), _full((8, D_MODEL))],
        out_shape=[jax.ShapeDtypeStruct((s_len, D_MODEL), F32), jax.ShapeDtypeStruct(w_in.shape, BF16),
                   jax.ShapeDtypeStruct((8, D_MODEL), F32)],
        scratch_shapes=[pltpu.VMEM(w_in.shape, F32), pltpu.VMEM((8, D_MODEL), F32)],
        compiler_params=_params("arbitrary"),
    )(dproj, x, dh1, g1, w_in)


def _place():
    x, y, c = lax.axis_index("x"), lax.axis_index("y"), lax.axis_index("c")
    chips = [(1 - x, y), (x, 1 - y), (1 - x, 1 - y)]
    return x, y, c, chips


def _rcopy(src, dst, ssem, rsem, dev):
    return pltpu.make_async_remote_copy(src_ref=src, dst_ref=dst, send_sem=ssem, recv_sem=rsem,
                                        device_id=dev, device_id_type=MESH)


ANY = pl.BlockSpec(memory_space=pl.ANY)
COPY_CHUNK_BYTES = 128 * 1024
ROW_ALIGN = 16


def _row_chunks(rows, row_bytes):
    per = max(ROW_ALIGN, (COPY_CHUNK_BYTES // row_bytes) // ROW_ALIGN * ROW_ALIGN)
    return [(r0, min(per, rows - r0)) for r0 in range(0, rows, per)]


def _row_bytes(a):
    return a.shape[-1] * jnp.dtype(a.dtype).itemsize


def _stack_own(shards, dtypes, pos, steps, tag, late=()):
    nw = len(shards)
    nl = len(late)

    def body(pos_ref, *refs):
        outs = refs[nw + nl:2 * nw + nl]
        late_ref = refs[2 * nw + nl:2 * nw + 2 * nl]
        i = pl.program_id(0)
        if nl:
            ssem, rsem = refs[2 * nw + 2 * nl:]

            @pl.when(i == 0)
            def _():
                _gather_send(late_ref, late, ssem, rsem)

        for w in range(nw):
            outs[w][0] = refs[w][...].astype(dtypes[w])

        if nl:
            @pl.when(i == max(steps - PASS_STEPS, 0))
            def _():
                _gather_pass(late_ref, late, ssem, rsem)

            @pl.when(i == steps - 1)
            def _():
                _gather_done(late_ref, late, ssem, rsem)

    def split(s):
        return s.shape[0] % (steps * ROW_ALIGN) == 0

    ins = [pl.BlockSpec((s.shape[0] // steps, s.shape[1]), lambda i, p: (i, 0)) if split(s)
           else pl.BlockSpec(s.shape, lambda i, p: (0, 0)) for s in shards]
    outs = [pl.BlockSpec((1, s.shape[0] // steps, s.shape[1]), lambda i, p: (p[0], i, 0)) if split(s)
            else pl.BlockSpec((1,) + s.shape, lambda i, p: (p[0], 0, 0)) for s in shards]
    res = pl.pallas_call(
        body, name="stack_own_" + tag,
        grid_spec=pltpu.PrefetchScalarGridSpec(
            num_scalar_prefetch=1, grid=(steps,), in_specs=ins + [ANY] * nl, out_specs=outs + [ANY] * nl,
            scratch_shapes=_gather_sems(nl) if nl else []),
        out_shape=[jax.ShapeDtypeStruct((N_CHIPS,) + s.shape, d) for s, d in zip(shards, dtypes)]
        + [jax.ShapeDtypeStruct(a.shape, a.dtype) for a in late],
        input_output_aliases={1 + nw + k: nw + k for k in range(nl)},
        compiler_params=_params("arbitrary"),
    )(pos, *shards, *late)
    return res[:nw], res[nw:]


def _gather_send(outs, stacks, ssem, rsem):
    x, y, c, chips = _place()
    me = 2 * x + y
    for w, st in enumerate(stacks):
        half = st.shape[1] // 2
        for s, (px, py) in enumerate(chips):
            for r0, rs in _row_chunks(half, _row_bytes(st)):
                piece = outs[w].at[me, pl.ds(c * half + r0, rs)]
                _rcopy(piece, piece, ssem.at[w, s], rsem.at[w, s], (px, py, c)).start()


def _gather_pass(outs, stacks, ssem, rsem):
    x, y, c, chips = _place()
    sib = (x, y, 1 - c)
    for w, st in enumerate(stacks):
        half = st.shape[1] // 2
        for s, (px, py) in enumerate(chips):
            blk = outs[w].at[2 * px + py, pl.ds(c * half, half)]
            _rcopy(blk, blk, ssem.at[w, s], rsem.at[w, s], sib).wait_recv()
            for r0, rs in _row_chunks(half, _row_bytes(st)):
                piece = outs[w].at[2 * px + py, pl.ds(c * half + r0, rs)]
                _rcopy(piece, piece, ssem.at[w, 3 + s], rsem.at[w, 3 + s], sib).start()


def _gather_done(outs, stacks, ssem, rsem):
    x, y, c, chips = _place()
    sib = (x, y, 1 - c)
    for w, st in enumerate(stacks):
        half = st.shape[1] // 2
        for s, (px, py) in enumerate(chips):
            blk = outs[w].at[2 * px + py, pl.ds((1 - c) * half, half)]
            _rcopy(blk, blk, ssem.at[w, 3 + s], rsem.at[w, 3 + s], sib).wait_recv()
    for w, st in enumerate(stacks):
        half = st.shape[1] // 2
        blk = outs[w].at[0, pl.ds(0, half)]
        for s in range(6):
            _rcopy(blk, blk, ssem.at[w, s], rsem.at[w, s], sib).wait_send()


def _gather_sems(nw):
    return [pltpu.SemaphoreType.DMA((nw, 6)), pltpu.SemaphoreType.DMA((nw, 6))]


def _swap_halves(grads, tag):
    nw = len(grads)

    def body(*refs):
        ins, got = refs[:nw], refs[nw:2 * nw]
        ssem, rsem = refs[2 * nw:]
        x, y, c, _ = _place()
        cps = []
        for w in range(nw):
            hr = grads[w].shape[1] // 2
            for k in range(N_CHIPS):
                for r0, rs in _row_chunks(hr, _row_bytes(grads[w])):
                    _rcopy(ins[w].at[k, pl.ds((1 - c) * hr + r0, rs)], got[w].at[k, pl.ds(r0, rs)],
                           ssem.at[w], rsem.at[w], (x, y, 1 - c)).start()
            cps.append(_rcopy(got[w], got[w], ssem.at[w], rsem.at[w], (x, y, 1 - c)))
        for cp in cps:
            cp.wait()

    return pl.pallas_call(
        body, name="swap_halves_" + tag,
        in_specs=[ANY] * nw, out_specs=[ANY] * nw,
        out_shape=[jax.ShapeDtypeStruct((g.shape[0], g.shape[1] // 2, g.shape[2]), g.dtype) for g in grads],
        scratch_shapes=[pltpu.SemaphoreType.DMA((nw,)), pltpu.SemaphoreType.DMA((nw,))],
    )(*grads)


def _add_pairs(grads, got, pos, steps, tag):
    nw = len(grads)

    def body(pos_ref, *refs):
        for w in range(nw):
            refs[2 * nw + w][...] = (refs[w][...].astype(F32) + refs[nw + w][...].astype(F32)).astype(BF16)

    blk = lambda a: (a.shape[0], a.shape[1] // steps, a.shape[2])
    own = [pl.BlockSpec(blk(a), lambda i, p: (0, p[1] * steps + i, 0)) for a in got]
    rec = [pl.BlockSpec(blk(a), lambda i, p: (0, i, 0)) for a in got]
    return pl.pallas_call(
        body, name="add_pairs_" + tag,
        grid_spec=pltpu.PrefetchScalarGridSpec(num_scalar_prefetch=1, grid=(steps,), in_specs=own + rec, out_specs=rec),
        out_shape=[jax.ShapeDtypeStruct(a.shape, BF16) for a in got],
        compiler_params=_params("arbitrary"),
    )(pos, *grads, *got)


def _scatter_send(ins, got, parts, ssem, rsem):
    x, y, c, chips = _place()
    for w, p in enumerate(parts):
        for s, (px, py) in enumerate(chips):
            for r0, rs in _row_chunks(p.shape[1], _row_bytes(p)):
                _rcopy(ins[w].at[2 * px + py, pl.ds(r0, rs)], got[w].at[s, pl.ds(r0, rs)],
                       ssem.at[w, s], rsem.at[w, s], (px, py, c)).start()


def _scatter_done(got, parts, ssem, rsem):
    x, y, c, chips = _place()
    for w in range(len(parts)):
        for s, (px, py) in enumerate(chips):
            _rcopy(got[w].at[s], got[w].at[s], ssem.at[w, s], rsem.at[w, s], (px, py, c)).wait()


def _scatter_sems(nw):
    return [pltpu.SemaphoreType.DMA((nw, 3)), pltpu.SemaphoreType.DMA((nw, 3))]


def _scatter_shapes(parts):
    return [jax.ShapeDtypeStruct((3,) + p.shape[1:], p.dtype) for p in parts]


def _sum_chips(parts, got, pos, steps):
    nw = len(parts)

    def body(pos_ref, *refs):
        for w in range(nw):
            acc = refs[w][0].astype(F32)
            for s in range(3):
                acc = acc + refs[nw + w][s].astype(F32)
            refs[2 * nw + w][...] = acc

    own = [pl.BlockSpec((1, p.shape[1] // steps, p.shape[2]), lambda i, ps: (ps[0], i, 0)) for p in parts]
    rec = [pl.BlockSpec((3, p.shape[1] // steps, p.shape[2]), lambda i, ps: (0, i, 0)) for p in parts]
    outs = [pl.BlockSpec((p.shape[1] // steps, p.shape[2]), lambda i, ps: (ps[1] * steps + i, 0)) for p in parts]
    return pl.pallas_call(
        body, name="sum_chips",
        grid_spec=pltpu.PrefetchScalarGridSpec(num_scalar_prefetch=1, grid=(steps,), in_specs=own + rec, out_specs=outs),
        out_shape=[jax.ShapeDtypeStruct((2 * p.shape[1], p.shape[2]), F32) for p in parts],
        compiler_params=_params("arbitrary"),
    )(pos, *parts, *got)


def _join_halves(shards):
    nw = len(shards)

    def body(*refs):
        outs = refs[nw:2 * nw]
        ssem, rsem = refs[2 * nw:]
        x, y, c, _ = _place()
        cps = []
        for w in range(nw):
            hr = shards[w].shape[0] // 2
            for r0, rs in _row_chunks(hr, _row_bytes(shards[w])):
                piece = outs[w].at[pl.ds(c * hr + r0, rs)]
                _rcopy(piece, piece, ssem.at[w], rsem.at[w], (x, y, 1 - c)).start()
            mine = outs[w].at[pl.ds(c * hr, hr)]
            cps.append(_rcopy(mine, mine, ssem.at[w], rsem.at[w], (x, y, 1 - c)))
        for cp in cps:
            cp.wait()

    return pl.pallas_call(
        body, name="join_halves",
        in_specs=[ANY] * nw, out_specs=[ANY] * nw,
        out_shape=[jax.ShapeDtypeStruct(s.shape, F32) for s in shards],
        input_output_aliases={w: w for w in range(nw)},
        scratch_shapes=[pltpu.SemaphoreType.DMA((nw,)), pltpu.SemaphoreType.DMA((nw,))],
    )(*shards)


def _scatter_and_allreduce(parts, packed):
    nw = len(parts)
    rows = packed.shape[0]
    half = rows // 2

    def body(*refs):
        ins, p_ref = refs[:nw], refs[nw]
        got, out_ref = refs[nw + 1:2 * nw + 1], refs[2 * nw + 1]
        rfull, rhalf, ssem, rsem, bsem_s, bsem_r = refs[2 * nw + 2:]
        x, y, c, _ = _place()
        sib = (x, y, 1 - c)
        _scatter_send(ins, got, parts, ssem, rsem)
        out_ref[...] = p_ref[...]
        cp = _rcopy(out_ref, rfull, bsem_s.at[0], bsem_r.at[0], sib)
        cp.start()
        cp.wait()
        out_ref[...] = out_ref[...] + rfull[...]
        mine = pl.ds(pl.multiple_of(c * half, 8), half)
        other = pl.ds(pl.multiple_of((1 - c) * half, 8), half)
        for st, peer in enumerate([(1 - x, y, c), (x, 1 - y, c)]):
            cp = _rcopy(out_ref.at[mine], rhalf.at[st], bsem_s.at[1 + st], bsem_r.at[1 + st], peer)
            cp.start()
            cp.wait()
            out_ref[mine, :] = out_ref[mine, :] + rhalf[st]
        cp = _rcopy(out_ref.at[mine], rhalf.at[2], bsem_s.at[3], bsem_r.at[3], sib)
        cp.start()
        cp.wait()
        out_ref[other, :] = rhalf[2]
        _scatter_done(got, parts, ssem, rsem)

    vm = pl.BlockSpec(memory_space=pltpu.VMEM)
    outs = pl.pallas_call(
        body, name="scatter_and_allreduce",
        in_specs=[ANY] * nw + [vm], out_specs=[ANY] * nw + [vm],
        out_shape=_scatter_shapes(parts) + [jax.ShapeDtypeStruct(packed.shape, F32)],
        scratch_shapes=[pltpu.VMEM(packed.shape, F32), pltpu.VMEM((3, half, packed.shape[1]), F32)] + _scatter_sems(nw)
        + [pltpu.SemaphoreType.DMA((4,)), pltpu.SemaphoreType.DMA((4,))],
        compiler_params=pltpu.CompilerParams(vmem_limit_bytes=VMEM_LIMIT),
    )(*parts, packed)
    return outs[:nw], outs[nw]


def _adamw_math(w, g, m, v):
    m = ADAM_B1 * m + (1.0 - ADAM_B1) * g
    v = ADAM_B2 * v + (1.0 - ADAM_B2) * (g * g)
    delta = -ADAM_LR * ((m * ADAM_C1) / (jnp.sqrt(v * ADAM_C2) + ADAM_EPS) + ADAM_WD * w)
    return delta, m, v


def _adamw(ws, gs, ms, vs, steps, name):
    nw = len(ws)

    def body(*refs):
        for k in range(nw):
            g = refs[nw + k][...]
            d, m, v = _adamw_math(refs[k][...], g, refs[2 * nw + k][...], refs[3 * nw + k][...])
            refs[4 * nw + k][...] = d
            refs[5 * nw + k][...] = m
            refs[6 * nw + k][...] = v
            refs[7 * nw + k][...] = g

    specs = [pl.BlockSpec((a.shape[0] // steps, a.shape[1]), lambda i: (i, 0)) for a in ws]
    shapes = [jax.ShapeDtypeStruct(a.shape, F32) for a in ws]
    outs = pl.pallas_call(
        body, name=name, grid=(steps,),
        in_specs=specs * 4, out_specs=specs * 4, out_shape=shapes * 4,
        compiler_params=_params("arbitrary"),
    )(*ws, *gs, *ms, *vs)
    return outs[:nw], outs[nw:2 * nw], outs[2 * nw:3 * nw], outs[3 * nw:]


SMALL = ["norm_mix_g", "pool_w", "pool_b", "pool_scale", "conv_b", "gate_a_w", "gate_a_b", "gate_x_w", "gate_x_b",
         "lru_L", "norm_mlp_g", "norm_ple_g", "b_ple_gate", "norm_final_g"]
BIG = ["w_in", "w_out", "w_up", "w_down", "w_ple_gate", "w_ple_proj"]
ORDER = ["norm_mix_g", "w_in", "pool_w", "pool_b", "pool_scale", "conv_w", "conv_b", "gate_a_w", "gate_a_b", "gate_x_w",
         "gate_x_b", "lru_L", "w_out", "norm_mlp_g", "w_up", "w_down", "norm_ple_g", "w_ple_gate", "b_ple_gate",
         "w_ple_proj", "norm_final_g"]
LANES = 128


def _pair_blocks(w):
    eye = jnp.eye(2, dtype=w.dtype)
    w4 = w.reshape(GATE_PAIRS, 2, LRU_BLOCK, LRU_BLOCK)
    return jnp.einsum("qaij,ab->qaibj", w4, eye).reshape(GATE_PAIRS, POOL_GROUP, POOL_GROUP)


def _diag_blocks(pairs):
    f = pairs.reshape(GATE_PAIRS, 2, LRU_BLOCK, 2, LRU_BLOCK)
    return jnp.stack([f[h // 2, h % 2, :, h % 2, :] for h in range(LRU_HEADS)])


def _rows128(a):
    return a.reshape(-1, LANES)


def _pad8(a):
    r = (-a.shape[0]) % 8
    return jnp.pad(a, ((0, r), (0, 0))) if r else a


def kernel(x, p, norm_mix_g, w_in, pool_w, pool_b, pool_scale, conv_w, conv_b, gate_a_w, gate_a_b, gate_x_w, gate_x_b, lru_L, w_out, norm_mlp_g, w_up, w_down, norm_ple_g, w_ple_gate, b_ple_gate, w_ple_proj, norm_final_g, loss_target, m_norm_mix_g, m_w_in, m_pool_w, m_pool_b, m_pool_scale, m_conv_w, m_conv_b, m_gate_a_w, m_gate_a_b, m_gate_x_w, m_gate_x_b, m_lru_L, m_w_out, m_norm_mlp_g, m_w_up, m_w_down, m_norm_ple_g, m_w_ple_gate, m_b_ple_gate, m_w_ple_proj, m_norm_final_g, v_norm_mix_g, v_w_in, v_pool_w, v_pool_b, v_pool_scale, v_conv_w, v_conv_b, v_gate_a_w, v_gate_a_b, v_gate_x_w, v_gate_x_b, v_lru_L, v_w_out, v_norm_mlp_g, v_w_up, v_w_down, v_norm_ple_g, v_w_ple_gate, v_b_ple_gate, v_w_ple_proj, v_norm_final_g):
    W = dict(norm_mix_g=norm_mix_g, w_in=w_in, pool_w=pool_w, pool_b=pool_b, pool_scale=pool_scale, conv_w=conv_w,
             conv_b=conv_b, gate_a_w=gate_a_w, gate_a_b=gate_a_b, gate_x_w=gate_x_w, gate_x_b=gate_x_b, lru_L=lru_L,
             w_out=w_out, norm_mlp_g=norm_mlp_g, w_up=w_up, w_down=w_down, norm_ple_g=norm_ple_g,
             w_ple_gate=w_ple_gate, b_ple_gate=b_ple_gate, w_ple_proj=w_ple_proj, norm_final_g=norm_final_g)
    M = dict(norm_mix_g=m_norm_mix_g, w_in=m_w_in, pool_w=m_pool_w, pool_b=m_pool_b, pool_scale=m_pool_scale,
             conv_w=m_conv_w, conv_b=m_conv_b, gate_a_w=m_gate_a_w, gate_a_b=m_gate_a_b, gate_x_w=m_gate_x_w,
             gate_x_b=m_gate_x_b, lru_L=m_lru_L, w_out=m_w_out, norm_mlp_g=m_norm_mlp_g, w_up=m_w_up, w_down=m_w_down,
             norm_ple_g=m_norm_ple_g, w_ple_gate=m_w_ple_gate, b_ple_gate=m_b_ple_gate, w_ple_proj=m_w_ple_proj,
             norm_final_g=m_norm_final_g)
    V = dict(norm_mix_g=v_norm_mix_g, w_in=v_w_in, pool_w=v_pool_w, pool_b=v_pool_b, pool_scale=v_pool_scale,
             conv_w=v_conv_w, conv_b=v_conv_b, gate_a_w=v_gate_a_w, gate_a_b=v_gate_a_b, gate_x_w=v_gate_x_w,
             gate_x_b=v_gate_x_b, lru_L=v_lru_L, w_out=v_w_out, norm_mlp_g=v_norm_mlp_g, w_up=v_w_up, w_down=v_w_down,
             norm_ple_g=v_norm_ple_g, w_ple_gate=v_w_ple_gate, b_ple_gate=v_b_ple_gate, w_ple_proj=v_w_ple_proj,
             norm_final_g=v_norm_final_g)

    s_len = x.shape[1]
    sub_mix = min(256, s_len)
    tm = min(512, s_len)
    chip = (2 * lax.axis_index("x") + lax.axis_index("y")).astype(jnp.int32)
    pos = jnp.stack([chip, lax.axis_index("c").astype(jnp.int32)])

    shards = [w_in[0], w_out[0], w_up[0], w_down[0], w_ple_gate[0], w_ple_proj[0], jnp.pad(conv_w[0], ((0, 12), (0, 0)))]
    first, _ = _stack_own([shards[0], shards[1], shards[6]], [BF16, BF16, F32], pos, 8, "first")
    (st_up, st_dn, st_pg, st_pp), (win_g, wout_g, cw_g) = _stack_own(shards[2:6], [BF16] * 4, pos, 8, "rest", first)
    wout_f = wout_g.reshape(D_MODEL, D_MODEL)
    cw_f = jnp.transpose(cw_g[:, :CONV_WIDTH], (1, 0, 2)).reshape(CONV_WIDTH, D_LRU)
    pw_b = pool_w[0].astype(BF16)
    wa_b = _pair_blocks(gate_a_w[0]).astype(BF16)
    wx_b = _pair_blocks(gate_x_w[0]).astype(BF16)
    pb_r = pool_b.reshape(1, D_POOL)
    ba_r = gate_a_b.reshape(1, D_LRU)
    bx_r = gate_x_b.reshape(1, D_LRU)
    g4 = norm_final_g.reshape(1, D_MODEL)
    mix_w = (pw_b, pb_r, pool_scale, cw_f, conv_b, wa_b, ba_r, wx_b, bx_r, lru_L, wout_f)

    xs, ps, ts = x[0], p[0, 0], loss_target[0]
    (ulru, dpl, gg, rg, hst, h1), (wup_g, wdn_g) = _fwd_mix(xs, norm_mix_g, win_g, *mix_w, tm, sub_mix, [st_up, st_dn])
    (z2, ru, h2), (wpg_g, wpp_g) = _fwd_mlp(h1, norm_mlp_g, wup_g, wdn_g, tm, [st_pg, st_pp])
    wpg_f = wpg_g.reshape(D_MODEL, D_MODEL)
    wpp_f = jnp.transpose(wpp_g, (1, 0, 2)).reshape(PLE_DIM, D_MODEL)
    dh2, dh2b, d_wpg, d_wpp, head_vec = _head(h2, ps, ts, norm_ple_g, wpg_f, b_ple_gate, wpp_f, g4, tm)
    dup, dh1, mlp_vec = _bwd_mlp_x(dh2, ru, h1, norm_mlp_g, wup_g, wdn_g, tm)
    d_wup, d_wdn = _bwd_mlp_w(z2, dup, ru, dh2b, tm)
    early = [d_wup, d_wdn, d_wpg.reshape(N_CHIPS, D_MODEL // N_CHIPS, D_MODEL), d_wpp]
    pair_e = _add_pairs(early, _swap_halves(early, "early"), pos, 8, "early")
    (dproj, d_wout, d_pw, d_wa, d_wx, mix_vec), got_e = _bwd_mix(dh1, ulru, dpl, gg, rg, hst, *mix_w, tm, sub_mix, pair_e)

    dx, d_win, in_vec = _bwd_in(dproj, xs, dh1, norm_mix_g, win_g, tm)

    last = [d_win, d_wout.reshape(N_CHIPS, D_MODEL // N_CHIPS, D_MODEL)]
    pair_l = _add_pairs(last, _swap_halves(last, "last"), pos, 8, "last")

    g_small = {
        "norm_mix_g": in_vec[0:1], "pool_w": d_pw, "pool_b": mix_vec[0:1], "pool_scale": mix_vec[1:2],
        "conv_b": mix_vec[2:3], "gate_a_w": _diag_blocks(d_wa), "gate_a_b": mix_vec[3:4],
        "gate_x_w": _diag_blocks(d_wx), "gate_x_b": mix_vec[4:5], "lru_L": mix_vec[5:6], "norm_mlp_g": mlp_vec[0:1],
        "norm_ple_g": head_vec[0:1], "b_ple_gate": head_vec[2:3], "norm_final_g": head_vec[1:2],
    }
    d_cw = jnp.transpose(mix_vec[8:8 + CONV_WIDTH].reshape(CONV_WIDTH, N_CHIPS, LANES), (1, 0, 2)).reshape(-1, LANES)
    pieces = [_pad8(_rows128(g_small[k])) for k in SMALL] + [d_cw, _pad8(head_vec[3:4, :LANES])]
    offs = [0]
    for pc in pieces:
        offs.append(offs[-1] + pc.shape[0])
    if offs[-1] % 16:
        pieces.append(jnp.zeros((8, LANES), F32))
    got_l, red = _scatter_and_allreduce(pair_l, jnp.concatenate(pieces, axis=0))
    g_big = _join_halves(_sum_chips(pair_l + pair_e, got_l + got_e, pos, 8))
    loss = red[offs[-2], 0]
    g_cw = lax.dynamic_slice(red, (offs[len(SMALL)] + CONV_WIDTH * chip, 0), (CONV_WIDTH, LANES))

    def packed(src):
        return jnp.concatenate([_pad8(_rows128(src[k])) for k in SMALL] + [_pad8(src["conv_w"][0])], axis=0)

    n_small = offs[len(SMALL)]
    g_pack = jnp.concatenate([red[:n_small], _pad8(g_cw)], axis=0)
    (d_pack,), (m_pack,), (v_pack,), _ = _adamw([packed(W)], [g_pack], [packed(M)], [packed(V)], 1, "adamw_small")

    big2d = lambda src: [src[k][0] for k in BIG]
    d_big, m_big, v_big, g_big = _adamw(big2d(W), g_big, big2d(M), big2d(V), 8, "adamw_big")

    def unpack(pack, big_list):
        out = {}
        for idx, k in enumerate(SMALL):
            n_el = W[k].size
            out[k] = pack[offs[idx]:offs[idx + 1]].reshape(-1)[:n_el].reshape(W[k].shape)
        out["conv_w"] = pack[n_small:n_small + CONV_WIDTH].reshape(W["conv_w"].shape)
        for k, a in zip(BIG, big_list):
            out[k] = a.reshape(W[k].shape)
        return out

    grads = unpack(g_pack, g_big)
    deltas = unpack(d_pack, d_big)
    new_m = unpack(m_pack, m_big)
    new_v = unpack(v_pack, v_big)
    return (loss, dx[None], *[grads[k] for k in ORDER], *[deltas[k] for k in ORDER],
            *[new_m[k] for k in ORDER], *[new_v[k] for k in ORDER])
```

```python
import functools

import jax
import jax.numpy as jnp
from jax import lax
from jax.experimental import pallas as pl
from jax.experimental.pallas import tpu as pltpu

F32 = jnp.float32
BF16 = jnp.bfloat16
MESH = pl.DeviceIdType.MESH

D_MODEL = 1024
D_POOL = 512
D_LRU = 512
POOL_WINDOWS = (2, 4, 8, 16)
POOL_GROUP = 128
POOL_HALO = 16
CONV_WIDTH = 4
CONV_HALO = 8
PASS_STEPS = 2
LRU_HEADS = 8
LRU_BLOCK = 64
GATE_PAIRS = 4
LRU_C = 8.0
D_FF = 4096
PLE_DIM = 256
D_IN_PROJ = 1536
RMS_EPS = 1e-6
N_CHIPS = 4
FF_BLOCK = D_FF // N_CHIPS
FF_PAIR = 2

ADAM_LR = 0.001
ADAM_B1 = 0.9
ADAM_B2 = 0.999
ADAM_EPS = 1e-08
ADAM_WD = 0.01
ADAM_STEP = 10
ADAM_C1 = 1.0 / (1.0 - ADAM_B1 ** ADAM_STEP)
ADAM_C2 = 1.0 / (1.0 - ADAM_B2 ** ADAM_STEP)

VMEM_LIMIT = 56 * 1024 * 1024
GELU_C = 0.7978845608028654
GELU_A = 0.044715

NT = (((1,), (1,)), ((), ()))
TN = (((0,), (0,)), ((), ()))


def _dot(a, b):
    return jnp.dot(a, b, preferred_element_type=F32)


def _dot_nt(a, b):
    return lax.dot_general(a, b, NT, preferred_element_type=F32)


def _dot_tn(a, b):
    return lax.dot_general(a, b, TN, preferred_element_type=F32)


def _params(*sem):
    return pltpu.CompilerParams(dimension_semantics=sem, vmem_limit_bytes=VMEM_LIMIT)


def _full(shape):
    nd = len(shape)
    return pl.BlockSpec(shape, lambda *_: (0,) * nd)


def _resident(shape):
    nd = len(shape)
    return pl.BlockSpec(shape, lambda *_: (0,) * nd, pipeline_mode=pl.Buffered(1))


def _rstd(x):
    return lax.rsqrt(jnp.mean(x * x, axis=-1, keepdims=True) + RMS_EPS)


def _rms_bwd(x, g, dz):
    xr = x * _rstd(x)
    r = _rstd(x)
    dyg = dz * g
    dx = r * (dyg - xr * jnp.mean(dyg * xr, axis=-1, keepdims=True))
    dg = jnp.sum(dz * xr, axis=0, keepdims=True)
    return dx, dg


def _sigmoid(x):
    return 1.0 / (1.0 + jnp.exp(-x))


def _log_sigmoid(v):
    u = jnp.exp(-jnp.abs(v))
    w = 1.0 + u
    l1p = jnp.where(w == 1.0, u, jnp.log(w) * u / jnp.where(w == 1.0, 1.0, w - 1.0))
    return jnp.minimum(v, 0.0) - l1p


def _gelu(x):
    t = jnp.tanh(GELU_C * (x + GELU_A * x * x * x))
    return 0.5 * x * (1.0 + t), t


def _gelu_grad(x, t):
    return 0.5 * (1.0 + t) + 0.5 * x * (1.0 - t * t) * GELU_C * (1.0 + 3.0 * GELU_A * x * x)


def _rows(shape, t0):
    return lax.broadcasted_iota(jnp.int32, shape, 0) + t0


def _pool_inv(tm, t0):
    rows = _rows((tm, POOL_GROUP), t0)
    return jnp.concatenate([1.0 / jnp.minimum(rows + 1, w).astype(F32) for w in POOL_WINDOWS], axis=1)


def _pool_diff(u_pool, prev, t0):
    tm = u_pool.shape[0]
    rows = _rows((tm, POOL_GROUP), t0)
    outs, invs = [], []
    for g, w in enumerate(POOL_WINDOWS):
        sl = slice(POOL_GROUP * g, POOL_GROUP * (g + 1))
        ug = u_pool[:, sl]
        s = jnp.concatenate([prev[:, sl], ug], axis=0)
        k = 1
        while k < w:
            s = s + pltpu.roll(s, k, 0)
            k *= 2
        inv = 1.0 / jnp.minimum(rows + 1, w).astype(F32)
        outs.append(s[POOL_HALO:] * inv - ug)
        invs.append(inv)
    return jnp.concatenate(outs, axis=1), jnp.concatenate(invs, axis=1)


def _pool_diff_bwd(dd, inv, nxt):
    ddc = dd * inv
    outs = []
    for g, w in enumerate(POOL_WINDOWS):
        sl = slice(POOL_GROUP * g, POOL_GROUP * (g + 1))
        s = jnp.concatenate([ddc[:, sl], nxt[:, sl]], axis=0)
        n = s.shape[0]
        k = 1
        while k < w:
            s = s + pltpu.roll(s, n - k, 0)
            k *= 2
        outs.append(s[:n - POOL_HALO] - dd[:, sl])
    return jnp.concatenate(outs, axis=1), ddc


def _conv_taps(u, prev):
    ext = jnp.concatenate([prev, u], axis=0)
    return [pltpu.roll(ext, CONV_WIDTH - 1 - k, 0)[CONV_HALO:] if k < CONV_WIDTH - 1 else u for k in range(CONV_WIDTH)]


def _scan_rev(a, b, carry):
    tm = a.shape[0]
    r8 = _rows(a.shape, 0) & 7
    k = 1
    while k < 8:
        ar = pltpu.roll(a, tm - k, 0)
        br = pltpu.roll(b, tm - k, 0)
        m = r8 < 8 - k
        b = jnp.where(m, a * br + b, b)
        a = jnp.where(m, a * ar, a)
        k *= 2
    outs = []
    c = jnp.broadcast_to(carry, (8, a.shape[1]))
    for g in reversed(range(tm // 8)):
        gg = b[8 * g:8 * g + 8] + a[8 * g:8 * g + 8] * c
        outs.insert(0, gg)
        c = jnp.broadcast_to(gg[0:1], c.shape)
    return jnp.concatenate(outs, axis=0)


def _gate_dot(xbb, w_ref):
    return jnp.concatenate(
        [_dot(xbb[:, POOL_GROUP * q:POOL_GROUP * (q + 1)], w_ref[q]) for q in range(GATE_PAIRS)], axis=1)


def _gate_dot_nt(db, w_ref):
    return jnp.concatenate(
        [_dot_nt(db[:, POOL_GROUP * q:POOL_GROUP * (q + 1)], w_ref[q]) for q in range(GATE_PAIRS)], axis=1)


def _lru_decay(r, lsl8, t0):
    a = jnp.exp(r * lsl8)
    first = _rows(r.shape, t0) == 0
    m2 = 1.0 - a * a
    rs = lax.rsqrt(jnp.maximum(m2, jnp.finfo(F32).tiny))
    mult = jnp.where(first, 1.0, m2 * rs)
    return a, mult, rs, first


def _lru_chunks_fwd(r_ref, i_ref, xb_ref, ug_ref, gp_ref, h_ref, y_ref, row0, n_rows, lsl8, t0, hc):
    lsl = jnp.broadcast_to(lsl8, (8, D_LRU))
    sub8 = lax.broadcasted_iota(jnp.int32, (8, D_LRU), 0)

    def chunk(j, hc):
        o = j * 8
        rows = pl.ds(o, 8)
        r = _sigmoid(r_ref[rows, :])
        ig = _sigmoid(i_ref[rows, :])
        r_ref[rows, :] = r
        i_ref[rows, :] = ig
        a = jnp.exp(r * lsl)
        m2 = 1.0 - a * a
        mult = jnp.where(sub8 + (t0 + o) == 0, 1.0, m2 * lax.rsqrt(jnp.maximum(m2, jnp.finfo(F32).tiny)))
        b = mult * (ig * xb_ref[rows, :])
        k = 1
        while k < 8:
            m = sub8 >= k
            b = jnp.where(m, a * pltpu.roll(b, k, 0) + b, b)
            a = jnp.where(m, a * pltpu.roll(a, k, 0), a)
            k *= 2
        h = b + a * hc
        h_ref[pl.ds(row0 + o, 8), :] = h
        ug = ug_ref[rows, :]
        gl, th = _gelu(ug)
        y_ref[rows, :] = h * gl
        ug_ref[rows, :] = gl
        gp_ref[rows, :] = _gelu_grad(ug, th)
        return jnp.broadcast_to(h[7:8, :], (8, D_LRU))

    for j in range(n_rows // 8):
        hc = chunk(j, hc)
    return hc


def _fwd_mix(x, g1, w_in, pool_w, pool_b, pool_scale, conv_w, conv_b, wa, ba, wx, bx, lru_l, w_out, tm, sub, late):
    s_len = x.shape[0]
    n = s_len // tm
    nl = len(late)

    def body(x_ref, g1_ref, win_ref, pw_ref, pb_ref, ps_ref, cw_ref, cb_ref, wa_ref, ba_ref, wx_ref, bx_ref, l_ref,
             wout_ref, *rest):
        ulru_ref, d_ref, gg_ref, rg_ref, h_ref, h1_ref = rest[nl:nl + 6]
        late_ref = rest[nl + 6:2 * nl + 6]
        cpool, clru, ch, s_r, s_i, s_xb, s_ug, s_gp, s_y, ssem, rsem = rest[2 * nl + 6:]
        i = pl.program_id(0)

        @pl.when(i == 0)
        def _():
            _gather_send(late_ref, late, ssem, rsem)
            cpool[...] = jnp.zeros_like(cpool)
            clru[...] = jnp.zeros_like(clru)
            ch[...] = jnp.zeros_like(ch)

        lsl8 = LRU_C * _log_sigmoid(l_ref[...])
        cp, cl, hc = cpool[...], clru[...], ch[...]

        def in_proj(k):
            rs = slice(k * sub, (k + 1) * sub)
            xv = x_ref[rs, :]
            zb = (xv * _rstd(xv) * g1_ref[...]).astype(BF16)
            proj = jnp.concatenate([_dot(zb, win_ref[j]) for j in range(N_CHIPS)], axis=1)
            ulru_ref[rs, :] = proj[:, D_POOL:D_POOL + D_LRU].astype(BF16)
            return xv, proj

        nxt = in_proj(0)
        for k in range(tm // sub):
            rs = slice(k * sub, (k + 1) * sub)
            t0 = i * tm + k * sub
            xv, proj = nxt
            if k + 1 < tm // sub:
                nxt = in_proj(k + 1)
            u_pool = proj[:, :D_POOL]
            u_lru = proj[:, D_POOL:D_POOL + D_LRU]
            u_gate = proj[:, D_POOL + D_LRU:]

            d, _ = _pool_diff(u_pool, cp, t0)
            cp = u_pool[sub - POOL_HALO:]
            db = d.astype(BF16)
            d_ref[rs, :] = db
            yp = jnp.concatenate(
                [_dot(db[:, POOL_GROUP * g:POOL_GROUP * (g + 1)], pw_ref[g]) for g in range(len(POOL_WINDOWS))], axis=1)
            y_pool = (yp + pb_ref[...]) * ps_ref[...]

            taps = _conv_taps(u_lru, cl)
            cl = u_lru[sub - CONV_HALO:]
            xb = cb_ref[...]
            for q in range(CONV_WIDTH):
                xb = xb + taps[q] * cw_ref[q:q + 1, :]
            xbb = xb.astype(BF16)
            s_r[...] = _gate_dot(xbb, wa_ref) + ba_ref[...]
            s_i[...] = _gate_dot(xbb, wx_ref) + bx_ref[...]
            s_xb[...] = xb
            s_ug[...] = u_gate
            hc = _lru_chunks_fwd(s_r, s_i, s_xb, s_ug, s_gp, h_ref, s_y, k * sub, sub, lsl8, t0, hc)
            rg_ref[rs, :] = jnp.concatenate([s_r[...], s_i[...]], axis=1).astype(BF16)
            gg_ref[rs, :] = jnp.concatenate([s_ug[...], s_gp[...]], axis=1).astype(BF16)
            cat = jnp.concatenate([y_pool, s_y[...]], axis=1).astype(BF16)
            h1_ref[rs, :] = xv + _dot(cat, wout_ref[...])
        cpool[...] = cp
        clru[...] = cl
        ch[...] = hc

        @pl.when(i == max(n - PASS_STEPS, 0))
        def _():
            _gather_pass(late_ref, late, ssem, rsem)

        @pl.when(i == n - 1)
        def _():
            _gather_done(late_ref, late, ssem, rsem)

    row = lambda w: pl.BlockSpec((tm, w), lambda i: (i, 0))
    ins = [x, g1, w_in, pool_w, pool_b, pool_scale, conv_w, conv_b, wa, ba, wx, bx, lru_l, w_out]
    outs = pl.pallas_call(
        body, name="fwd_mix", grid=(n,),
        in_specs=[row(D_MODEL)] + [_resident(a.shape) for a in ins[1:]] + [ANY] * nl,
        out_specs=[row(D_LRU), row(D_POOL), row(2 * D_LRU), row(2 * D_LRU), row(D_LRU), row(D_MODEL)] + [ANY] * nl,
        out_shape=[jax.ShapeDtypeStruct((s_len, D_LRU), BF16), jax.ShapeDtypeStruct((s_len, D_POOL), BF16),
                   jax.ShapeDtypeStruct((s_len, 2 * D_LRU), BF16), jax.ShapeDtypeStruct((s_len, 2 * D_LRU), BF16),
                   jax.ShapeDtypeStruct((s_len, D_LRU), F32), jax.ShapeDtypeStruct((s_len, D_MODEL), F32)]
        + [jax.ShapeDtypeStruct(a.shape, a.dtype) for a in late],
        input_output_aliases={len(ins) + k: 6 + k for k in range(nl)},
        scratch_shapes=[pltpu.VMEM((POOL_HALO, D_POOL), F32), pltpu.VMEM((CONV_HALO, D_LRU), F32),
                        pltpu.VMEM((8, D_LRU), F32)] + [pltpu.VMEM((sub, D_LRU), F32)] * 6 + _gather_sems(nl),
        compiler_params=_params("arbitrary"),
    )(*ins, *late)
    return outs[:6], outs[6:]


def _fwd_mlp(h1, g2, w_up, w_down, tm, late):
    s_len = h1.shape[0]
    n = s_len // tm
    nl = len(late)

    def body(h1_ref, g2_ref, wup_ref, wdn_ref, *rest):
        z2_ref, ru_ref, h2_ref = rest[nl:nl + 3]
        late_ref = rest[nl + 3:2 * nl + 3]
        ssem, rsem = rest[2 * nl + 3:]
        i = pl.program_id(0)

        @pl.when(i == 0)
        def _():
            _gather_send(late_ref, late, ssem, rsem)

        hv = h1_ref[...]
        zb = (hv * _rstd(hv) * g2_ref[...]).astype(BF16)
        z2_ref[...] = zb
        acc = hv
        for j in range(N_CHIPS):
            ru = jnp.maximum(_dot(zb, wup_ref[j]), 0.0)
            ru_ref[:, FF_BLOCK * j:FF_BLOCK * (j + 1)] = ru.astype(BF16)
            acc = acc + _dot((ru * ru).astype(BF16), wdn_ref[j])
        h2_ref[...] = acc

        @pl.when(i == max(n - PASS_STEPS, 0))
        def _():
            _gather_pass(late_ref, late, ssem, rsem)

        @pl.when(i == n - 1)
        def _():
            _gather_done(late_ref, late, ssem, rsem)

    row = lambda w: pl.BlockSpec((tm, w), lambda i: (i, 0))
    outs = pl.pallas_call(
        body, name="fwd_mlp", grid=(n,),
        in_specs=[row(D_MODEL), _full(g2.shape), _resident(w_up.shape), _resident(w_down.shape)] + [ANY] * nl,
        out_specs=[row(D_MODEL), row(D_FF), row(D_MODEL)] + [ANY] * nl,
        out_shape=[jax.ShapeDtypeStruct((s_len, D_MODEL), BF16), jax.ShapeDtypeStruct((s_len, D_FF), BF16),
                   jax.ShapeDtypeStruct((s_len, D_MODEL), F32)] + [jax.ShapeDtypeStruct(a.shape, a.dtype) for a in late],
        input_output_aliases={4 + k: 3 + k for k in range(nl)},
        scratch_shapes=_gather_sems(nl),
        compiler_params=_params("arbitrary"),
    )(h1, g2, w_up, w_down, *late)
    return outs[:3], outs[3:]


def _head(h2, p, target, g3, w_pg, b_pg, w_pp, g4, tm):
    s_len = h2.shape[0]
    n = s_len // tm

    def body(h2_ref, p_ref, t_ref, g3_ref, wpg_ref, bpg_ref, wpp_ref, g4_ref,
             dh2_ref, dh2b_ref, dwpg_ref, dwpp_ref, vec_ref, a_pg, a_pp, a_vec, s_g, s_p, s_dpp, s_dpre, s_dh3):
        i = pl.program_id(0)

        @pl.when(i == 0)
        def _():
            a_pg[...] = jnp.zeros_like(a_pg)
            a_pp[...] = jnp.zeros_like(a_pp)
            a_vec[...] = jnp.zeros_like(a_vec)

        g3v = g3_ref[...]
        g4v = g4_ref[...]
        h2v = h2_ref[...]
        z3 = (h2v * _rstd(h2v) * g3v).astype(BF16)
        s_g[...] = _dot(z3, wpg_ref[...]) + bpg_ref[...]
        pb = p_ref[...].astype(BF16)
        s_p[...] = _dot(pb, wpp_ref[...])

        g3b = jnp.broadcast_to(g3v, (8, D_MODEL))
        g4b = jnp.broadcast_to(g4v, (8, D_MODEL))
        zero = jnp.zeros((8, D_MODEL), F32)
        p_g4, p_b, p_loss = zero, zero, zero
        for k in range(tm // 8):
            rows = pl.ds(8 * k, 8)
            h2c = h2_ref[rows, :]
            gate = _sigmoid(s_g[rows, :])
            pp = s_p[rows, :]
            h3 = h2c + gate * pp
            r4 = _rstd(h3)
            xr = h3 * r4
            diff = xr * g4b - t_ref[rows, :]
            p_loss = p_loss + diff * diff
            dy = diff * (1.0 / D_MODEL)
            dyg = dy * g4b
            dh3 = r4 * (dyg - xr * jnp.mean(dyg * xr, axis=-1, keepdims=True))
            p_g4 = p_g4 + dy * xr
            dpre = dh3 * pp * gate * (1.0 - gate)
            p_b = p_b + dpre
            s_dpp[rows, :] = dh3 * gate
            s_dpre[rows, :] = dpre
            s_dh3[rows, :] = dh3
        dpreb = s_dpre[...].astype(BF16)
        s_g[...] = _dot_nt(dpreb, wpg_ref[...])
        a_pg[...] += _dot_tn(z3, dpreb)
        a_pp[...] += _dot_tn(pb, s_dpp[...].astype(BF16))
        p_g3 = zero
        for k in range(tm // 8):
            rows = pl.ds(8 * k, 8)
            h2c = h2_ref[rows, :]
            r3 = _rstd(h2c)
            xr = h2c * r3
            dz = s_g[rows, :]
            dyg = dz * g3b
            dh2_ref[rows, :] = s_dh3[rows, :] + r3 * (dyg - xr * jnp.mean(dyg * xr, axis=-1, keepdims=True))
            p_g3 = p_g3 + dz * xr
        dh2b_ref[...] = dh2_ref[...].astype(BF16)
        a_vec[0:1, :] += jnp.sum(p_g3, axis=0, keepdims=True)
        a_vec[1:2, :] += jnp.sum(p_g4, axis=0, keepdims=True)
        a_vec[2:3, :] += jnp.sum(p_b, axis=0, keepdims=True)
        a_vec[3:4, :] += jnp.broadcast_to((0.5 / D_MODEL) * jnp.sum(p_loss, keepdims=True).reshape(1, 1), (1, D_MODEL))

        @pl.when(i == n - 1)
        def _():
            dwpg_ref[...] = a_pg[...].astype(BF16)
            for j in range(N_CHIPS):
                dwpp_ref[j] = a_pp[:, PLE_DIM * j:PLE_DIM * (j + 1)].astype(BF16)
            vec_ref[...] = a_vec[...]

    row = lambda w: pl.BlockSpec((tm, w), lambda i: (i, 0))
    ins = [h2, p, target, g3, w_pg, b_pg, w_pp, g4]
    return pl.pallas_call(
        body, name="head", grid=(n,),
        in_specs=[row(D_MODEL), row(PLE_DIM), row(D_MODEL)] + [_resident(a.shape) for a in ins[3:]],
        out_specs=[row(D_MODEL), row(D_MODEL), _full((D_MODEL, D_MODEL)), _full((N_CHIPS, PLE_DIM, PLE_DIM)),
                   _full((8, D_MODEL))],
        out_shape=[jax.ShapeDtypeStruct((s_len, D_MODEL), F32), jax.ShapeDtypeStruct((s_len, D_MODEL), BF16),
                   jax.ShapeDtypeStruct((D_MODEL, D_MODEL), BF16),
                   jax.ShapeDtypeStruct((N_CHIPS, PLE_DIM, PLE_DIM), BF16), jax.ShapeDtypeStruct((8, D_MODEL), F32)],
        scratch_shapes=[pltpu.VMEM((D_MODEL, D_MODEL), F32), pltpu.VMEM((PLE_DIM, D_MODEL), F32),
                        pltpu.VMEM((8, D_MODEL), F32)] + [pltpu.VMEM((tm, D_MODEL), F32)] * 5,
        compiler_params=_params("arbitrary"),
    )(*ins)


def _bwd_mlp_x(dh2, ru, h1, g2, w_up, w_down, tm):
    s_len = dh2.shape[0]
    n = s_len // tm

    def body(dh2_ref, ru_ref, h1_ref, g2_ref, wup_ref, wdn_ref, dup_ref, dh1_ref, dg2_ref, a_g):
        i = pl.program_id(0)

        @pl.when(i == 0)
        def _():
            a_g[...] = jnp.zeros_like(a_g)

        dh2v = dh2_ref[...]
        dhb = dh2v.astype(BF16)
        acc = jnp.zeros((tm, D_MODEL), F32)
        for j in range(N_CHIPS):
            sl = slice(FF_BLOCK * j, FF_BLOCK * (j + 1))
            dup = (_dot_nt(dhb, wdn_ref[j]) * (2.0 * ru_ref[:, sl].astype(F32))).astype(BF16)
            dup_ref[:, sl] = dup
            acc = acc + _dot_nt(dup, wup_ref[j])
        dx, dg = _rms_bwd(h1_ref[...], g2_ref[...], acc)
        dh1_ref[...] = dh2v + dx
        a_g[0:1, :] += dg

        @pl.when(i == n - 1)
        def _():
            dg2_ref[...] = a_g[...]

    row = lambda w: pl.BlockSpec((tm, w), lambda i: (i, 0))
    return pl.pallas_call(
        body, name="bwd_mlp_x", grid=(n,),
        in_specs=[row(D_MODEL), row(D_FF), row(D_MODEL), _full(g2.shape), _resident(w_up.shape), _resident(w_down.shape)],
        out_specs=[row(D_FF), row(D_MODEL), _full((8, D_MODEL))],
        out_shape=[jax.ShapeDtypeStruct((s_len, D_FF), BF16), jax.ShapeDtypeStruct((s_len, D_MODEL), F32),
                   jax.ShapeDtypeStruct((8, D_MODEL), F32)],
        scratch_shapes=[pltpu.VMEM((8, D_MODEL), F32)],
        compiler_params=_params("arbitrary"),
    )(dh2, ru, h1, g2, w_up, w_down)


def _bwd_mlp_w(z2, dup, ru, dh2, tk):
    s_len = z2.shape[0]
    n = s_len // tk

    def body(z2_ref, dup_ref, ru_ref, dh2_ref, dwup_ref, dwdn_ref, a_up, a_dn):
        t = pl.program_id(1)

        @pl.when(t == 0)
        def _():
            a_up[...] = jnp.zeros_like(a_up)
            a_dn[...] = jnp.zeros_like(a_dn)

        for b in range(FF_PAIR):
            sl = slice(FF_BLOCK * b, FF_BLOCK * (b + 1))
            ruv = ru_ref[:, sl]
            a_up[b] += _dot_tn(z2_ref[...], dup_ref[:, sl])
            a_dn[b] += _dot_tn(ruv * ruv, dh2_ref[...])

        @pl.when(t == n - 1)
        def _():
            dwup_ref[...] = a_up[...].astype(BF16)
            dwdn_ref[...] = a_dn[...].astype(BF16)

    tile = pl.BlockSpec((tk, D_MODEL), lambda j, t: (t, 0))
    ffb = pl.BlockSpec((tk, FF_PAIR * FF_BLOCK), lambda j, t: (t, j))
    return pl.pallas_call(
        body, name="bwd_mlp_w", grid=(N_CHIPS // FF_PAIR, n),
        in_specs=[tile, ffb, ffb, tile],
        out_specs=[pl.BlockSpec((FF_PAIR, D_MODEL, FF_BLOCK), lambda j, t: (j, 0, 0)),
                   pl.BlockSpec((FF_PAIR, FF_BLOCK, D_MODEL), lambda j, t: (j, 0, 0))],
        out_shape=[jax.ShapeDtypeStruct((N_CHIPS, D_MODEL, FF_BLOCK), BF16),
                   jax.ShapeDtypeStruct((N_CHIPS, FF_BLOCK, D_MODEL), BF16)],
        scratch_shapes=[pltpu.VMEM((FF_PAIR, D_MODEL, FF_BLOCK), F32), pltpu.VMEM((FF_PAIR, FF_BLOCK, D_MODEL), F32)],
        compiler_params=_params("arbitrary", "arbitrary"),
    )(z2, dup, ru, dh2)


MIX_VEC_ROWS = 16


def _bwd_mix(dh1, ulru, dpl, gg, rg, h, pool_w, pool_b, pool_scale, conv_w, conv_b, wa, ba, wx, bx, lru_l, w_out, tm, sub, early):
    s_len = dh1.shape[0]
    n = s_len // tm
    ng = len(POOL_WINDOWS)
    ne_ = len(early)

    def body(dh1_ref, ulru_ref, d_ref, gg_ref, rg_ref, h_ref, ulruh_ref, hh_ref, pw_ref, pb_ref, ps_ref, cw_ref, cb_ref,
             wa_ref, ba_ref, wx_ref, bx_ref, l_ref, wout_ref, *rest):
        early_ref = rest[:ne_]
        dproj_ref, dwout_ref, dpw_ref, dwa_ref, dwx_ref, vec_ref = rest[ne_:ne_ + 6]
        got_ref = rest[ne_ + 6:2 * ne_ + 6]
        a_out, a_pw, a_wa, a_wx, a_vec, c_g, c_dxb, c_ddc, ssem, rsem = rest[2 * ne_ + 6:]
        q = pl.program_id(0)
        i = n - 1 - q

        @pl.when(q == 0)
        def _():
            _scatter_send(early_ref, got_ref, early, ssem, rsem)
            for r in (a_out, a_pw, a_wa, a_wx, a_vec, c_g, c_dxb, c_ddc):
                r[...] = jnp.zeros_like(r)

        has_prev = (i > 0).astype(F32)
        lv = l_ref[...]
        lsl8 = LRU_C * _log_sigmoid(lv)
        cg, cdxb, cddc = c_g[0:1, :], c_dxb[...], c_ddc[...]
        vec, cats, dhbs = {}, [], []

        def add(row, v):
            vec[row] = v if row not in vec else vec[row] + v

        def first_matmuls(k):
            dh1b = dh1_ref[k * sub:(k + 1) * sub, :].astype(BF16)
            db = d_ref[k * sub:(k + 1) * sub, :]
            ypre = jnp.concatenate(
                [_dot(db[:, POOL_GROUP * g:POOL_GROUP * (g + 1)], pw_ref[g]) for g in range(ng)], axis=1) + pb_ref[...]
            return dh1b, _dot_nt(dh1b, wout_ref[...]), db, ypre

        nxt = first_matmuls(tm // sub - 1)
        for k in reversed(range(tm // sub)):
            rs = slice(k * sub, (k + 1) * sub)
            t0 = i * tm + k * sub
            dh1b, dcat, db, ypre = nxt
            if k > 0:
                nxt = first_matmuls(k - 1)
            dy_pool = dcat[:, :D_POOL]
            dy_lru = dcat[:, D_POOL:]

            u_lru = ulru_ref[rs, :].astype(F32)
            if k > 0:
                halo = ulru_ref[k * sub - POOL_HALO:k * sub, :].astype(F32)
                h_prev_row = h_ref[k * sub - 1:k * sub, :]
            else:
                halo = ulruh_ref[...].astype(F32) * has_prev
                h_prev_row = hh_ref[7:8, :] * has_prev

            inv = _pool_inv(sub, t0)
            dyp = dy_pool * ps_ref[...]
            dypb = dyp.astype(BF16)
            dds = []
            for g in range(ng):
                sl = slice(POOL_GROUP * g, POOL_GROUP * (g + 1))
                a_pw[g] += _dot_tn(db[:, sl], dypb[:, sl])
                dds.append(_dot_nt(dypb[:, sl], pw_ref[g]))
            du_pool, ddc = _pool_diff_bwd(jnp.concatenate(dds, axis=1), inv, cddc)
            cddc = ddc[:POOL_HALO]
            add(0, jnp.sum(dyp, axis=0, keepdims=True))
            add(1, jnp.sum(dy_pool * ypre, axis=0, keepdims=True))

            taps = _conv_taps(u_lru, halo[POOL_HALO - CONV_HALO:])
            xb = cb_ref[...]
            for c in range(CONV_WIDTH):
                xb = xb + taps[c] * cw_ref[c:c + 1, :]
            r = rg_ref[rs, :D_LRU].astype(F32)
            ig = rg_ref[rs, D_LRU:].astype(F32)
            a, mult, inv_mult, first = _lru_decay(r, lsl8, t0)
            hv = h_ref[rs, :]
            gl = gg_ref[rs, :D_LRU].astype(F32)
            du_gate = dy_lru * hv * gg_ref[rs, D_LRU:].astype(F32)
            cats.insert(0, jnp.concatenate([ypre * ps_ref[...], hv * gl], axis=1).astype(BF16))
            dhbs.insert(0, dh1b)
            last = _rows(a.shape, 0) == sub - 1
            a_next = jnp.where(last, 1.0, pltpu.roll(a, sub - 1, 0))
            gh = _scan_rev(a_next, dy_lru * gl, cg)
            cg = a[0:1, :] * gh[0:1, :]
            h_prev = jnp.where(_rows(hv.shape, 0) == 0, h_prev_row, pltpu.roll(hv, 1, 0))
            gix = gh * ig * xb
            dla = gh * h_prev * a - jnp.where(first, 0.0, gix * a * a * inv_mult)
            dpre_r = dla * lsl8 * r * (1.0 - r)
            dpre_i = gh * mult * xb * ig * (1.0 - ig)
            dprb = dpre_r.astype(BF16)
            dpib = dpre_i.astype(BF16)
            xbb = xb.astype(BF16)
            for pair in range(GATE_PAIRS):
                sl = slice(POOL_GROUP * pair, POOL_GROUP * (pair + 1))
                a_wa[pair] += _dot_tn(xbb[:, sl], dprb[:, sl])
                a_wx[pair] += _dot_tn(xbb[:, sl], dpib[:, sl])
            dxb = gh * mult * ig + _gate_dot_nt(dprb, wa_ref) + _gate_dot_nt(dpib, wx_ref)
            add(2, jnp.sum(dxb, axis=0, keepdims=True))
            add(3, jnp.sum(dpre_r, axis=0, keepdims=True))
            add(4, jnp.sum(dpre_i, axis=0, keepdims=True))
            add(5, jnp.sum(dla * r, axis=0, keepdims=True))
            ext = jnp.concatenate([dxb, cdxb], axis=0)
            cdxb = dxb[:CONV_HALO]
            ne = sub + CONV_HALO
            du_lru = dxb * cw_ref[CONV_WIDTH - 1:CONV_WIDTH, :]
            for c in range(CONV_WIDTH):
                add(8 + c, jnp.sum(dxb * taps[c], axis=0, keepdims=True))
                if c < CONV_WIDTH - 1:
                    du_lru = du_lru + pltpu.roll(ext, ne - (CONV_WIDTH - 1 - c), 0)[:sub] * cw_ref[c:c + 1, :]
            dproj_ref[rs, :] = jnp.concatenate([du_pool, du_lru, du_gate], axis=1).astype(BF16)
        a_out[...] += _dot_tn(jnp.concatenate(cats, axis=0), jnp.concatenate(dhbs, axis=0))
        c_g[...] = jnp.broadcast_to(cg, c_g.shape)
        c_dxb[...] = cdxb
        c_ddc[...] = cddc
        for row, v in vec.items():
            a_vec[row:row + 1, :] += v

        @pl.when(q == n - 1)
        def _():
            dwout_ref[...] = a_out[...].astype(BF16)
            dpw_ref[...] = a_pw[...]
            dwa_ref[...] = a_wa[...]
            dwx_ref[...] = a_wx[...]
            vec_ref[...] = a_vec[...]
            vec_ref[5:6, :] = a_vec[5:6, :] * (LRU_C * _sigmoid(-lv))
            _scatter_done(got_ref, early, ssem, rsem)

    rev =lambda w: pl.BlockSpec((tm, w), lambda q: (n - 1 - q, 0))
    halo_p = pl.BlockSpec((POOL_HALO, D_LRU), lambda q: (jnp.maximum((n - 1 - q) * (tm // POOL_HALO) - 1, 0), 0))
    halo_h = pl.BlockSpec((8, D_LRU), lambda q: (jnp.maximum((n - 1 - q) * (tm // 8) - 1, 0), 0))
    wts = [pool_w, pool_b, pool_scale, conv_w, conv_b, wa, ba, wx, bx, lru_l, w_out]
    outs = pl.pallas_call(
        body, name="bwd_mix", grid=(n,),
        in_specs=[rev(D_MODEL), rev(D_LRU), rev(D_POOL), rev(2 * D_LRU), rev(2 * D_LRU), rev(D_LRU), halo_p, halo_h]
        + [_resident(a.shape) for a in wts]
        + [ANY] * ne_,
        out_specs=[rev(D_IN_PROJ), _full((D_MODEL, D_MODEL)), _full((ng, POOL_GROUP, POOL_GROUP)),
                   _full(wa.shape), _full(wa.shape), _full((MIX_VEC_ROWS, D_LRU))] + [ANY] * ne_,
        out_shape=[jax.ShapeDtypeStruct((s_len, D_IN_PROJ), BF16), jax.ShapeDtypeStruct((D_MODEL, D_MODEL), BF16),
                   jax.ShapeDtypeStruct((ng, POOL_GROUP, POOL_GROUP), F32), jax.ShapeDtypeStruct(wa.shape, F32),
                   jax.ShapeDtypeStruct(wa.shape, F32), jax.ShapeDtypeStruct((MIX_VEC_ROWS, D_LRU), F32)]
        + _scatter_shapes(early),
        scratch_shapes=[pltpu.VMEM((D_MODEL, D_MODEL), F32), pltpu.VMEM((ng, POOL_GROUP, POOL_GROUP), F32),
                        pltpu.VMEM(wa.shape, F32), pltpu.VMEM(wa.shape, F32),
                        pltpu.VMEM((MIX_VEC_ROWS, D_LRU), F32), pltpu.VMEM((8, D_LRU), F32),
                        pltpu.VMEM((CONV_HALO, D_LRU), F32), pltpu.VMEM((POOL_HALO, D_POOL), F32)] + _scatter_sems(ne_),
        compiler_params=_params("arbitrary"),
    )(dh1, ulru, dpl, gg, rg, h, ulru, h, *wts, *early)
    return outs[:6], outs[6:]


def _bwd_in(dproj, x, dh1, g1, w_in, tm):
    s_len = x.shape[0]
    n = s_len // tm
    cb = D_IN_PROJ // N_CHIPS

    def body(dp_ref, x_ref, dh1_ref, g1_ref, win_ref, dx_ref, dwin_ref, dg1_ref, a_w, a_g):
        i = pl.program_id(0)

        @pl.when(i == 0)
        def _():
            a_w[...] = jnp.zeros_like(a_w)
            a_g[...] = jnp.zeros_like(a_g)

        dp = dp_ref[...]
        xv = x_ref[...]
        zb = (xv * _rstd(xv) * g1_ref[...]).astype(BF16)
        dz = jnp.zeros((tm, D_MODEL), F32)
        for j in range(N_CHIPS):
            dpj = dp[:, cb * j:cb * (j + 1)]
            dz = dz + _dot_nt(dpj, win_ref[j])
            a_w[j] += _dot_tn(zb, dpj)
        dx, dg = _rms_bwd(xv, g1_ref[...], dz)
        dx_ref[...] = dh1_ref[...] + dx
        a_g[0:1, :] += dg

        @pl.when(i == n - 1)
        def _():
            dwin_ref[...] = a_w[...].astype(BF16)
            dg1_ref[...] = a_g[...]

    row = lambda w: pl.BlockSpec((tm, w), lambda i: (i, 0))
    return pl.pallas_call(
        body, name="bwd_in", grid=(n,),
        in_specs=[row(D_IN_PROJ), row(D_MODEL), row(D_MODEL), _resident(g1.shape), _resident(w_in.shape)],
        out_specs=[row(D_MODEL), _full(w_in.shape), _full((8, D_MODEL))],
        out_shape=[jax.ShapeDtypeStruct((s_len, D_MODEL), F32), jax.ShapeDtypeStruct(w_in.shape, BF16),
                   jax.ShapeDtypeStruct((8, D_MODEL), F32)],
        scratch_shapes=[pltpu.VMEM(w_in.shape, F32), pltpu.VMEM((8, D_MODEL), F32)],
        compiler_params=_params("arbitrary"),
    )(dproj, x, dh1, g1, w_in)


def _place():
    x, y, c = lax.axis_index("x"), lax.axis_index("y"), lax.axis_index("c")
    chips = [(1 - x, y), (x, 1 - y), (1 - x, 1 - y)]
    return x, y, c, chips


def _rcopy(src, dst, ssem, rsem, dev):
    return pltpu.make_async_remote_copy(src_ref=src, dst_ref=dst, send_sem=ssem, recv_sem=rsem,
                                        device_id=dev, device_id_type=MESH)


ANY = pl.BlockSpec(memory_space=pl.ANY)
COPY_CHUNK_BYTES = 128 * 1024
ROW_ALIGN = 16


def _row_chunks(rows, row_bytes):
    per = max(ROW_ALIGN, (COPY_CHUNK_BYTES // row_bytes) // ROW_ALIGN * ROW_ALIGN)
    return [(r0, min(per, rows - r0)) for r0 in range(0, rows, per)]


def _row_bytes(a):
    return a.shape[-1] * jnp.dtype(a.dtype).itemsize


def _stack_own(shards, dtypes, pos, steps, tag, late=()):
    nw = len(shards)
    nl = len(late)

    def body(pos_ref, *refs):
        outs = refs[nw + nl:2 * nw + nl]
        late_ref = refs[2 * nw + nl:2 * nw + 2 * nl]
        i = pl.program_id(0)
        if nl:
            ssem, rsem = refs[2 * nw + 2 * nl:]

            @pl.when(i == 0)
            def _():
                _gather_send(late_ref, late, ssem, rsem)

        for w in range(nw):
            outs[w][0] = refs[w][...].astype(dtypes[w])

        if nl:
            @pl.when(i == max(steps - PASS_STEPS, 0))
            def _():
                _gather_pass(late_ref, late, ssem, rsem)

            @pl.when(i == steps - 1)
            def _():
                _gather_done(late_ref, late, ssem, rsem)

    def split(s):
        return s.shape[0] % (steps * ROW_ALIGN) == 0

    ins = [pl.BlockSpec((s.shape[0] // steps, s.shape[1]), lambda i, p: (i, 0)) if split(s)
           else pl.BlockSpec(s.shape, lambda i, p: (0, 0)) for s in shards]
    outs = [pl.BlockSpec((1, s.shape[0] // steps, s.shape[1]), lambda i, p: (p[0], i, 0)) if split(s)
            else pl.BlockSpec((1,) + s.shape, lambda i, p: (p[0], 0, 0)) for s in shards]
    res = pl.pallas_call(
        body, name="stack_own_" + tag,
        grid_spec=pltpu.PrefetchScalarGridSpec(
            num_scalar_prefetch=1, grid=(steps,), in_specs=ins + [ANY] * nl, out_specs=outs + [ANY] * nl,
            scratch_shapes=_gather_sems(nl) if nl else []),
        out_shape=[jax.ShapeDtypeStruct((N_CHIPS,) + s.shape, d) for s, d in zip(shards, dtypes)]
        + [jax.ShapeDtypeStruct(a.shape, a.dtype) for a in late],
        input_output_aliases={1 + nw + k: nw + k for k in range(nl)},
        compiler_params=_params("arbitrary"),
    )(pos, *shards, *late)
    return res[:nw], res[nw:]


def _gather_send(outs, stacks, ssem, rsem):
    x, y, c, chips = _place()
    me = 2 * x + y
    for w, st in enumerate(stacks):
        half = st.shape[1] // 2
        for s, (px, py) in enumerate(chips):
            for r0, rs in _row_chunks(half, _row_bytes(st)):
                piece = outs[w].at[me, pl.ds(c * half + r0, rs)]
                _rcopy(piece, piece, ssem.at[w, s], rsem.at[w, s], (px, py, c)).start()


def _gather_pass(outs, stacks, ssem, rsem):
    x, y, c, chips = _place()
    sib = (x, y, 1 - c)
    for w, st in enumerate(stacks):
        half = st.shape[1] // 2
        for s, (px, py) in enumerate(chips):
            blk = outs[w].at[2 * px + py, pl.ds(c * half, half)]
            _rcopy(blk, blk, ssem.at[w, s], rsem.at[w, s], sib).wait_recv()
            for r0, rs in _row_chunks(half, _row_bytes(st)):
                piece = outs[w].at[2 * px + py, pl.ds(c * half + r0, rs)]
                _rcopy(piece, piece, ssem.at[w, 3 + s], rsem.at[w, 3 + s], sib).start()


def _gather_done(outs, stacks, ssem, rsem):
    x, y, c, chips = _place()
    sib = (x, y, 1 - c)
    for w, st in enumerate(stacks):
        half = st.shape[1] // 2
        for s, (px, py) in enumerate(chips):
            blk = outs[w].at[2 * px + py, pl.ds((1 - c) * half, half)]
            _rcopy(blk, blk, ssem.at[w, 3 + s], rsem.at[w, 3 + s], sib).wait_recv()
    for w, st in enumerate(stacks):
        half = st.shape[1] // 2
        blk = outs[w].at[0, pl.ds(0, half)]
        for s in range(6):
            _rcopy(blk, blk, ssem.at[w, s], rsem.at[w, s], sib).wait_send()


def _gather_sems(nw):
    return [pltpu.SemaphoreType.DMA((nw, 6)), pltpu.SemaphoreType.DMA((nw, 6))]


def _swap_halves(grads, tag):
    nw = len(grads)

    def body(*refs):
        ins, got = refs[:nw], refs[nw:2 * nw]
        ssem, rsem = refs[2 * nw:]
        x, y, c, _ = _place()
        cps = []
        for w in range(nw):
            hr = grads[w].shape[1] // 2
            for k in range(N_CHIPS):
                for r0, rs in _row_chunks(hr, _row_bytes(grads[w])):
                    _rcopy(ins[w].at[k, pl.ds((1 - c) * hr + r0, rs)], got[w].at[k, pl.ds(r0, rs)],
                           ssem.at[w], rsem.at[w], (x, y, 1 - c)).start()
            cps.append(_rcopy(got[w], got[w], ssem.at[w], rsem.at[w], (x, y, 1 - c)))
        for cp in cps:
            cp.wait()

    return pl.pallas_call(
        body, name="swap_halves_" + tag,
        in_specs=[ANY] * nw, out_specs=[ANY] * nw,
        out_shape=[jax.ShapeDtypeStruct((g.shape[0], g.shape[1] // 2, g.shape[2]), g.dtype) for g in grads],
        scratch_shapes=[pltpu.SemaphoreType.DMA((nw,)), pltpu.SemaphoreType.DMA((nw,))],
    )(*grads)


def _add_pairs(grads, got, pos, steps, tag):
    nw = len(grads)

    def body(pos_ref, *refs):
        for w in range(nw):
            refs[2 * nw + w][...] = (refs[w][...].astype(F32) + refs[nw + w][...].astype(F32)).astype(BF16)

    blk = lambda a: (a.shape[0], a.shape[1] // steps, a.shape[2])
    own = [pl.BlockSpec(blk(a), lambda i, p: (0, p[1] * steps + i, 0)) for a in got]
    rec = [pl.BlockSpec(blk(a), lambda i, p: (0, i, 0)) for a in got]
    return pl.pallas_call(
        body, name="add_pairs_" + tag,
        grid_spec=pltpu.PrefetchScalarGridSpec(num_scalar_prefetch=1, grid=(steps,), in_specs=own + rec, out_specs=rec),
        out_shape=[jax.ShapeDtypeStruct(a.shape, BF16) for a in got],
        compiler_params=_params("arbitrary"),
    )(pos, *grads, *got)


def _scatter_send(ins, got, parts, ssem, rsem):
    x, y, c, chips = _place()
    for w, p in enumerate(parts):
        for s, (px, py) in enumerate(chips):
            for r0, rs in _row_chunks(p.shape[1], _row_bytes(p)):
                _rcopy(ins[w].at[2 * px + py, pl.ds(r0, rs)], got[w].at[s, pl.ds(r0, rs)],
                       ssem.at[w, s], rsem.at[w, s], (px, py, c)).start()


def _scatter_done(got, parts, ssem, rsem):
    x, y, c, chips = _place()
    for w in range(len(parts)):
        for s, (px, py) in enumerate(chips):
            _rcopy(got[w].at[s], got[w].at[s], ssem.at[w, s], rsem.at[w, s], (px, py, c)).wait()


def _scatter_sems(nw):
    return [pltpu.SemaphoreType.DMA((nw, 3)), pltpu.SemaphoreType.DMA((nw, 3))]


def _scatter_shapes(parts):
    return [jax.ShapeDtypeStruct((3,) + p.shape[1:], p.dtype) for p in parts]


def _sum_chips(parts, got, pos, steps):
    nw = len(parts)

    def body(pos_ref, *refs):
        for w in range(nw):
            acc = refs[w][0].astype(F32)
            for s in range(3):
                acc = acc + refs[nw + w][s].astype(F32)
            refs[2 * nw + w][...] = acc

    own = [pl.BlockSpec((1, p.shape[1] // steps, p.shape[2]), lambda i, ps: (ps[0], i, 0)) for p in parts]
    rec = [pl.BlockSpec((3, p.shape[1] // steps, p.shape[2]), lambda i, ps: (0, i, 0)) for p in parts]
    outs = [pl.BlockSpec((p.shape[1] // steps, p.shape[2]), lambda i, ps: (ps[1] * steps + i, 0)) for p in parts]
    return pl.pallas_call(
        body, name="sum_chips",
        grid_spec=pltpu.PrefetchScalarGridSpec(num_scalar_prefetch=1, grid=(steps,), in_specs=own + rec, out_specs=outs),
        out_shape=[jax.ShapeDtypeStruct((2 * p.shape[1], p.shape[2]), F32) for p in parts],
        compiler_params=_params("arbitrary"),
    )(pos, *parts, *got)


def _join_halves(shards):
    nw = len(shards)

    def body(*refs):
        outs = refs[nw:2 * nw]
        ssem, rsem = refs[2 * nw:]
        x, y, c, _ = _place()
        cps = []
        for w in range(nw):
            hr = shards[w].shape[0] // 2
            for r0, rs in _row_chunks(hr, _row_bytes(shards[w])):
                piece = outs[w].at[pl.ds(c * hr + r0, rs)]
                _rcopy(piece, piece, ssem.at[w], rsem.at[w], (x, y, 1 - c)).start()
            mine = outs[w].at[pl.ds(c * hr, hr)]
            cps.append(_rcopy(mine, mine, ssem.at[w], rsem.at[w], (x, y, 1 - c)))
        for cp in cps:
            cp.wait()

    return pl.pallas_call(
        body, name="join_halves",
        in_specs=[ANY] * nw, out_specs=[ANY] * nw,
        out_shape=[jax.ShapeDtypeStruct(s.shape, F32) for s in shards],
        input_output_aliases={w: w for w in range(nw)},
        scratch_shapes=[pltpu.SemaphoreType.DMA((nw,)), pltpu.SemaphoreType.DMA((nw,))],
    )(*shards)


def _scatter_and_allreduce(parts, packed):
    nw = len(parts)
    rows = packed.shape[0]
    half = rows // 2

    def body(*refs):
        ins, p_ref = refs[:nw], refs[nw]
        got, out_ref = refs[nw + 1:2 * nw + 1], refs[2 * nw + 1]
        rfull, rhalf, ssem, rsem, bsem_s, bsem_r = refs[2 * nw + 2:]
        x, y, c, _ = _place()
        sib = (x, y, 1 - c)
        _scatter_send(ins, got, parts, ssem, rsem)
        out_ref[...] = p_ref[...]
        cp = _rcopy(out_ref, rfull, bsem_s.at[0], bsem_r.at[0], sib)
        cp.start()
        cp.wait()
        out_ref[...] = out_ref[...] + rfull[...]
        mine = pl.ds(pl.multiple_of(c * half, 8), half)
        other = pl.ds(pl.multiple_of((1 - c) * half, 8), half)
        for st, peer in enumerate([(1 - x, y, c), (x, 1 - y, c)]):
            cp = _rcopy(out_ref.at[mine], rhalf.at[st], bsem_s.at[1 + st], bsem_r.at[1 + st], peer)
            cp.start()
            cp.wait()
            out_ref[mine, :] = out_ref[mine, :] + rhalf[st]
        cp = _rcopy(out_ref.at[mine], rhalf.at[2], bsem_s.at[3], bsem_r.at[3], sib)
        cp.start()
        cp.wait()
        out_ref[other, :] = rhalf[2]
        _scatter_done(got, parts, ssem, rsem)

    vm = pl.BlockSpec(memory_space=pltpu.VMEM)
    outs = pl.pallas_call(
        body, name="scatter_and_allreduce",
        in_specs=[ANY] * nw + [vm], out_specs=[ANY] * nw + [vm],
        out_shape=_scatter_shapes(parts) + [jax.ShapeDtypeStruct(packed.shape, F32)],
        scratch_shapes=[pltpu.VMEM(packed.shape, F32), pltpu.VMEM((3, half, packed.shape[1]), F32)] + _scatter_sems(nw)
        + [pltpu.SemaphoreType.DMA((4,)), pltpu.SemaphoreType.DMA((4,))],
        compiler_params=pltpu.CompilerParams(vmem_limit_bytes=VMEM_LIMIT),
    )(*parts, packed)
    return outs[:nw], outs[nw]


def _adamw_math(w, g, m, v):
    m = ADAM_B1 * m + (1.0 - ADAM_B1) * g
    v = ADAM_B2 * v + (1.0 - ADAM_B2) * (g * g)
    delta = -ADAM_LR * ((m * ADAM_C1) / (jnp.sqrt(v * ADAM_C2) + ADAM_EPS) + ADAM_WD * w)
    return delta, m, v


def _adamw(ws, gs, ms, vs, steps, name):
    nw = len(ws)

    def body(*refs):
        for k in range(nw):
            g = refs[nw + k][...]
            d, m, v = _adamw_math(refs[k][...], g, refs[2 * nw + k][...], refs[3 * nw + k][...])
            refs[4 * nw + k][...] = d
            refs[5 * nw + k][...] = m
            refs[6 * nw + k][...] = v
            refs[7 * nw + k][...] = g

    specs = [pl.BlockSpec((a.shape[0] // steps, a.shape[1]), lambda i: (i, 0)) for a in ws]
    shapes = [jax.ShapeDtypeStruct(a.shape, F32) for a in ws]
    outs = pl.pallas_call(
        body, name=name, grid=(steps,),
        in_specs=specs * 4, out_specs=specs * 4, out_shape=shapes * 4,
        compiler_params=_params("arbitrary"),
    )(*ws, *gs, *ms, *vs)
    return outs[:nw], outs[nw:2 * nw], outs[2 * nw:3 * nw], outs[3 * nw:]


SMALL = ["norm_mix_g", "pool_w", "pool_b", "pool_scale", "conv_b", "gate_a_w", "gate_a_b", "gate_x_w", "gate_x_b",
         "lru_L", "norm_mlp_g", "norm_ple_g", "b_ple_gate", "norm_final_g"]
BIG = ["w_in", "w_out", "w_up", "w_down", "w_ple_gate", "w_ple_proj"]
ORDER = ["norm_mix_g", "w_in", "pool_w", "pool_b", "pool_scale", "conv_w", "conv_b", "gate_a_w", "gate_a_b", "gate_x_w",
         "gate_x_b", "lru_L", "w_out", "norm_mlp_g", "w_up", "w_down", "norm_ple_g", "w_ple_gate", "b_ple_gate",
         "w_ple_proj", "norm_final_g"]
LANES = 128


def _pair_blocks(w):
    eye = jnp.eye(2, dtype=w.dtype)
    w4 = w.reshape(GATE_PAIRS, 2, LRU_BLOCK, LRU_BLOCK)
    return jnp.einsum("qaij,ab->qaibj", w4, eye).reshape(GATE_PAIRS, POOL_GROUP, POOL_GROUP)


def _diag_blocks(pairs):
    f = pairs.reshape(GATE_PAIRS, 2, LRU_BLOCK, 2, LRU_BLOCK)
    return jnp.stack([f[h // 2, h % 2, :, h % 2, :] for h in range(LRU_HEADS)])


def _rows128(a):
    return a.reshape(-1, LANES)


def _pad8(a):
    r = (-a.shape[0]) % 8
    return jnp.pad(a, ((0, r), (0, 0))) if r else a


def kernel(x, p, norm_mix_g, w_in, pool_w, pool_b, pool_scale, conv_w, conv_b, gate_a_w, gate_a_b, gate_x_w, gate_x_b, lru_L, w_out, norm_mlp_g, w_up, w_down, norm_ple_g, w_ple_gate, b_ple_gate, w_ple_proj, norm_final_g, loss_target, m_norm_mix_g, m_w_in, m_pool_w, m_pool_b, m_pool_scale, m_conv_w, m_conv_b, m_gate_a_w, m_gate_a_b, m_gate_x_w, m_gate_x_b, m_lru_L, m_w_out, m_norm_mlp_g, m_w_up, m_w_down, m_norm_ple_g, m_w_ple_gate, m_b_ple_gate, m_w_ple_proj, m_norm_final_g, v_norm_mix_g, v_w_in, v_pool_w, v_pool_b, v_pool_scale, v_conv_w, v_conv_b, v_gate_a_w, v_gate_a_b, v_gate_x_w, v_gate_x_b, v_lru_L, v_w_out, v_norm_mlp_g, v_w_up, v_w_down, v_norm_ple_g, v_w_ple_gate, v_b_ple_gate, v_w_ple_proj, v_norm_final_g):
    W = dict(norm_mix_g=norm_mix_g, w_in=w_in, pool_w=pool_w, pool_b=pool_b, pool_scale=pool_scale, conv_w=conv_w,
             conv_b=conv_b, gate_a_w=gate_a_w, gate_a_b=gate_a_b, gate_x_w=gate_x_w, gate_x_b=gate_x_b, lru_L=lru_L,
             w_out=w_out, norm_mlp_g=norm_mlp_g, w_up=w_up, w_down=w_down, norm_ple_g=norm_ple_g,
             w_ple_gate=w_ple_gate, b_ple_gate=b_ple_gate, w_ple_proj=w_ple_proj, norm_final_g=norm_final_g)
    M = dict(norm_mix_g=m_norm_mix_g, w_in=m_w_in, pool_w=m_pool_w, pool_b=m_pool_b, pool_scale=m_pool_scale,
             conv_w=m_conv_w, conv_b=m_conv_b, gate_a_w=m_gate_a_w, gate_a_b=m_gate_a_b, gate_x_w=m_gate_x_w,
             gate_x_b=m_gate_x_b, lru_L=m_lru_L, w_out=m_w_out, norm_mlp_g=m_norm_mlp_g, w_up=m_w_up, w_down=m_w_down,
             norm_ple_g=m_norm_ple_g, w_ple_gate=m_w_ple_gate, b_ple_gate=m_b_ple_gate, w_ple_proj=m_w_ple_proj,
             norm_final_g=m_norm_final_g)
    V = dict(norm_mix_g=v_norm_mix_g, w_in=v_w_in, pool_w=v_pool_w, pool_b=v_pool_b, pool_scale=v_pool_scale,
             conv_w=v_conv_w, conv_b=v_conv_b, gate_a_w=v_gate_a_w, gate_a_b=v_gate_a_b, gate_x_w=v_gate_x_w,
             gate_x_b=v_gate_x_b, lru_L=v_lru_L, w_out=v_w_out, norm_mlp_g=v_norm_mlp_g, w_up=v_w_up, w_down=v_w_down,
             norm_ple_g=v_norm_ple_g, w_ple_gate=v_w_ple_gate, b_ple_gate=v_b_ple_gate, w_ple_proj=v_w_ple_proj,
             norm_final_g=v_norm_final_g)

    s_len = x.shape[1]
    sub_mix = min(256, s_len)
    tm = min(512, s_len)
    chip = (2 * lax.axis_index("x") + lax.axis_index("y")).astype(jnp.int32)
    pos = jnp.stack([chip, lax.axis_index("c").astype(jnp.int32)])

    shards = [w_in[0], w_out[0], w_up[0], w_down[0], w_ple_gate[0], w_ple_proj[0], jnp.pad(conv_w[0], ((0, 12), (0, 0)))]
    first, _ = _stack_own([shards[0], shards[1], shards[6]], [BF16, BF16, F32], pos, 8, "first")
    (st_up, st_dn, st_pg, st_pp), (win_g, wout_g, cw_g) = _stack_own(shards[2:6], [BF16] * 4, pos, 8, "rest", first)
    wout_f = wout_g.reshape(D_MODEL, D_MODEL)
    cw_f = jnp.transpose(cw_g[:, :CONV_WIDTH], (1, 0, 2)).reshape(CONV_WIDTH, D_LRU)
    pw_b = pool_w[0].astype(BF16)
    wa_b = _pair_blocks(gate_a_w[0]).astype(BF16)
    wx_b = _pair_blocks(gate_x_w[0]).astype(BF16)
    pb_r = pool_b.reshape(1, D_POOL)
    ba_r = gate_a_b.reshape(1, D_LRU)
    bx_r = gate_x_b.reshape(1, D_LRU)
    g4 = norm_final_g.reshape(1, D_MODEL)
    mix_w = (pw_b, pb_r, pool_scale, cw_f, conv_b, wa_b, ba_r, wx_b, bx_r, lru_L, wout_f)

    xs, ps, ts = x[0], p[0, 0], loss_target[0]
    (ulru, dpl, gg, rg, hst, h1), (wup_g, wdn_g) = _fwd_mix(xs, norm_mix_g, win_g, *mix_w, tm, sub_mix, [st_up, st_dn])
    (z2, ru, h2), (wpg_g, wpp_g) = _fwd_mlp(h1, norm_mlp_g, wup_g, wdn_g, tm, [st_pg, st_pp])
    wpg_f = wpg_g.reshape(D_MODEL, D_MODEL)
    wpp_f = jnp.transpose(wpp_g, (1, 0, 2)).reshape(PLE_DIM, D_MODEL)
    dh2, dh2b, d_wpg, d_wpp, head_vec = _head(h2, ps, ts, norm_ple_g, wpg_f, b_ple_gate, wpp_f, g4, tm)
    dup, dh1, mlp_vec = _bwd_mlp_x(dh2, ru, h1, norm_mlp_g, wup_g, wdn_g, tm)
    d_wup, d_wdn = _bwd_mlp_w(z2, dup, ru, dh2b, tm)
    early = [d_wup, d_wdn, d_wpg.reshape(N_CHIPS, D_MODEL // N_CHIPS, D_MODEL), d_wpp]
    pair_e = _add_pairs(early, _swap_halves(early, "early"), pos, 8, "early")
    (dproj, d_wout, d_pw, d_wa, d_wx, mix_vec), got_e = _bwd_mix(dh1, ulru, dpl, gg, rg, hst, *mix_w, tm, sub_mix, pair_e)

    dx, d_win, in_vec = _bwd_in(dproj, xs, dh1, norm_mix_g, win_g, tm)

    last = [d_win, d_wout.reshape(N_CHIPS, D_MODEL // N_CHIPS, D_MODEL)]
    pair_l = _add_pairs(last, _swap_halves(last, "last"), pos, 8, "last")

    g_small = {
        "norm_mix_g": in_vec[0:1], "pool_w": d_pw, "pool_b": mix_vec[0:1], "pool_scale": mix_vec[1:2],
        "conv_b": mix_vec[2:3], "gate_a_w": _diag_blocks(d_wa), "gate_a_b": mix_vec[3:4],
        "gate_x_w": _diag_blocks(d_wx), "gate_x_b": mix_vec[4:5], "lru_L": mix_vec[5:6], "norm_mlp_g": mlp_vec[0:1],
        "norm_ple_g": head_vec[0:1], "b_ple_gate": head_vec[2:3], "norm_final_g": head_vec[1:2],
    }
    d_cw = jnp.transpose(mix_vec[8:8 + CONV_WIDTH].reshape(CONV_WIDTH, N_CHIPS, LANES), (1, 0, 2)).reshape(-1, LANES)
    pieces = [_pad8(_rows128(g_small[k])) for k in SMALL] + [d_cw, _pad8(head_vec[3:4, :LANES])]
    offs = [0]
    for pc in pieces:
        offs.append(offs[-1] + pc.shape[0])
    if offs[-1] % 16:
        pieces.append(jnp.zeros((8, LANES), F32))
    got_l, red = _scatter_and_allreduce(pair_l, jnp.concatenate(pieces, axis=0))
    g_big = _join_halves(_sum_chips(pair_l + pair_e, got_l + got_e, pos, 8))
    loss = red[offs[-2], 0]
    g_cw = lax.dynamic_slice(red, (offs[len(SMALL)] + CONV_WIDTH * chip, 0), (CONV_WIDTH, LANES))

    def packed(src):
        return jnp.concatenate([_pad8(_rows128(src[k])) for k in SMALL] + [_pad8(src["conv_w"][0])], axis=0)

    n_small = offs[len(SMALL)]
    g_pack = jnp.concatenate([red[:n_small], _pad8(g_cw)], axis=0)
    (d_pack,), (m_pack,), (v_pack,), _ = _adamw([packed(W)], [g_pack], [packed(M)], [packed(V)], 1, "adamw_small")

    big2d = lambda src: [src[k][0] for k in BIG]
    d_big, m_big, v_big, g_big = _adamw(big2d(W), g_big, big2d(M), big2d(V), 8, "adamw_big")

    def unpack(pack, big_list):
        out = {}
        for idx, k in enumerate(SMALL):
            n_el = W[k].size
            out[k] = pack[offs[idx]:offs[idx + 1]].reshape(-1)[:n_el].reshape(W[k].shape)
        out["conv_w"] = pack[n_small:n_small + CONV_WIDTH].reshape(W["conv_w"].shape)
        for k, a in zip(BIG, big_list):
            out[k] = a.reshape(W[k].shape)
        return out

    grads = unpack(g_pack, g_big)
    deltas = unpack(d_pack, d_big)
    new_m = unpack(m_pack, m_big)
    new_v = unpack(v_pack, v_big)
    return (loss, dx[None], *[grads[k] for k in ORDER], *[deltas[k] for k in ORDER],
            *[new_m[k] for k in ORDER], *[new_v[k] for k in ORDER])
```
